```python
import jax, jax.numpy as jnp
from jax import lax
import numpy as np

D_MODEL = 1024
BATCH = 8
SEQ = 16384
DEPTH = 1

N_META = 16
GRID_W = 64
CHUNK = 128
Q_BLOCK = 128
EPS = 1e-6
HG_HEADS = 4
HG_K = 128
HG_V = 128
HG_KW = HG_HEADS * HG_K
HG_VW = HG_HEADS * HG_V
AT_HEADS = 8
AT_KV_HEADS = 2
AT_HD = 64
AT_GROUP = AT_HEADS // AT_KV_HEADS
AT_W = AT_HEADS * AT_HD
AT_KVW = AT_KV_HEADS * AT_HD
ROPE_THETA = 10000.0
ROPE_AXIS = AT_HD // 2
D_FF = 2816
IN_SIZES = (HG_KW, HG_VW, HG_KW, HG_KW, HG_VW, AT_W, AT_KVW, AT_KVW, D_MODEL, D_MODEL)
D_IN = sum(IN_SIZES)

kernel_name = 'hybrid_hgrn2_axial_gqa_macaron_block'


def rms_norm(x, w):
    xf = x.astype(jnp.float32)
    y = xf * lax.rsqrt(jnp.mean(xf * xf, axis=-1, keepdims=True) + EPS)
    return (y * w.astype(jnp.float32)).astype(x.dtype)


def swiglu(x, w_gate, w_up, w_down):
    return (jax.nn.silu(x @ w_gate) * (x @ w_up)) @ w_down


def split_cols(z, sizes):
    outs = []
    start = 0
    for s in sizes:
        outs.append(z[..., start:start + s])
        start += s
    return outs


def axial_rope_tables(n_real):
    rows = n_real // GRID_W
    row = jnp.repeat(jnp.arange(rows, dtype=jnp.float32), GRID_W)
    col = jnp.tile(jnp.arange(GRID_W, dtype=jnp.float32), rows)
    zeros = jnp.zeros((N_META,), jnp.float32)
    row = jnp.concatenate([zeros, row])
    col = jnp.concatenate([zeros, col])
    inv = ROPE_THETA ** (-jnp.arange(0, ROPE_AXIS, 2, dtype=jnp.float32) / ROPE_AXIS)
    ang = jnp.concatenate([row[:, None] * inv, col[:, None] * inv], axis=-1)
    return jnp.cos(ang), jnp.sin(ang)


def apply_rope(x, cos, sin):
    xf = x.astype(jnp.float32).reshape(x.shape[:-1] + (AT_HD // 2, 2))
    x1, x2 = xf[..., 0], xf[..., 1]
    c = cos[None, :, None, :]
    s = sin[None, :, None, :]
    out = jnp.stack([x1 * c - x2 * s, x1 * s + x2 * c], axis=-1).reshape(x.shape)
    return out.astype(x.dtype)


def attend(qb, k, v):
    s = jnp.einsum('bkgqd,bksd->bkgqs', qb, k, preferred_element_type=jnp.float32) * (AT_HD ** -0.5)
    p = jax.nn.softmax(s, axis=-1)
    return jnp.einsum('bkgqs,bksd->bkgqd', p.astype(v.dtype), v)


def axial_gqa(zq, zk, zv, q_norm_w, k_norm_w, cos, sin):
    B, L, _ = zq.shape
    q = rms_norm(zq.reshape(B, L, AT_HEADS, AT_HD), q_norm_w)
    k = rms_norm(zk.reshape(B, L, AT_KV_HEADS, AT_HD), k_norm_w)
    v = zv.reshape(B, L, AT_KV_HEADS, AT_HD)
    q = apply_rope(q, cos, sin)
    k = apply_rope(k, cos, sin)
    q = q.transpose(0, 2, 1, 3).reshape(B, AT_KV_HEADS, AT_GROUP, L, AT_HD)
    k = k.transpose(0, 2, 1, 3)
    v = v.transpose(0, 2, 1, 3)
    n_real = L - N_META
    n_blk = n_real // Q_BLOCK
    o_meta = attend(q[:, :, :, :N_META], k, v)
    q_real = jnp.moveaxis(q[:, :, :, N_META:].reshape(B, AT_KV_HEADS, AT_GROUP, n_blk, Q_BLOCK, AT_HD), 3, 0)
    o_real = lax.map(lambda qb: attend(qb, k, v), q_real)
    o_real = jnp.moveaxis(o_real, 0, 3).reshape(B, AT_KV_HEADS, AT_GROUP, n_real, AT_HD)
    o = jnp.concatenate([o_meta, o_real], axis=3)
    return o.transpose(0, 3, 1, 2, 4).reshape(B, L, AT_W)


def gla_chunked(q, k, v, logf):
    B, T, H, K = q.shape
    n = T // CHUNK
    def to_chunks(a):
        return a.reshape(B, n, CHUNK, H, a.shape[-1]).transpose(1, 0, 3, 2, 4)
    mask = jnp.tril(jnp.ones((CHUNK, CHUNK), dtype=bool))

    def step(S, xs):
        qc, kc, vc, lc = xs
        b = jnp.cumsum(lc, axis=2)
        o_inter = jnp.einsum('bhck,bhkv->bhcv', qc * jnp.exp(b), S)
        diff = jnp.where(mask[None, None, :, :, None], b[:, :, :, None, :] - b[:, :, None, :, :], -jnp.inf)
        attn = jnp.einsum('bhtk,bhsk,bhtsk->bhts', qc, kc, jnp.exp(diff))
        o_intra = jnp.einsum('bhts,bhsv->bhtv', attn, vc)
        b_last = b[:, :, -1:, :]
        S_new = jnp.exp(b_last[:, :, 0, :])[..., None] * S + jnp.einsum('bhsk,bhsv->bhkv', kc * jnp.exp(b_last - b), vc)
        return S_new, o_inter + o_intra

    S0 = jnp.zeros((B, H, K, v.shape[-1]), jnp.float32)
    _, o = lax.scan(step, S0, (to_chunks(q), to_chunks(k), to_chunks(v), to_chunks(logf)))
    return o.transpose(1, 0, 3, 2, 4).reshape(B, T, H, v.shape[-1])


def hgrn2_bidir(zq, zi, zf_f, zf_b, zg, lb_f, lb_b, out_norm_w):
    B, L, _ = zq.shape
    q = jax.nn.silu(zq.astype(jnp.float32)).reshape(B, L, HG_HEADS, HG_K)
    v = zi.astype(jnp.float32).reshape(B, L, HG_HEADS, HG_V)

    def gates(zf, lb):
        lb = lb.reshape(HG_HEADS, HG_K)
        kk = (1.0 - lb) * jax.nn.sigmoid(-zf.astype(jnp.float32).reshape(B, L, HG_HEADS, HG_K))
        return kk, jnp.log1p(-kk)

    k_f, lf_f = gates(zf_f, lb_f)
    k_b, lf_b = gates(zf_b, lb_b)
    n_pad = CHUNK - N_META
    pad = lambda a: jnp.pad(a, ((0, 0), (n_pad, 0), (0, 0), (0, 0)))
    flip = lambda a: jnp.flip(a, axis=1)
    q_p, v_p = pad(q), pad(v)
    o_fwd = gla_chunked(q_p, pad(k_f), v_p, pad(lf_f))
    o_bwd = flip(gla_chunked(flip(q_p), flip(pad(k_b)), flip(v_p), flip(pad(lf_b))))
    o = (o_fwd + o_bwd)[:, n_pad:]
    o = o * lax.rsqrt(jnp.mean(o * o, axis=-1, keepdims=True) + EPS) * out_norm_w.astype(jnp.float32).reshape(HG_HEADS, HG_V)
    o = o.reshape(B, L, HG_VW) * jax.nn.silu(zg.astype(jnp.float32))
    return o.astype(zq.dtype)


def _fwd_setup_inputs(seed: int = 0) -> dict:
    key = jax.random.key(seed)
    ks = jax.random.split(key, 24)
    nrm = lambda k, shape, fan_in: jax.random.normal(k, shape, jnp.float32) * (fan_in ** -0.5)
    gain = lambda k, shape: 1.0 + 0.01 * jax.random.normal(k, shape, jnp.float32)
    return {
        'x': jax.random.normal(ks[0], (BATCH, SEQ, D_MODEL), jnp.float32),
        'meta_tokens': jax.random.normal(ks[1], (N_META, D_MODEL), jnp.float32),
        'ffn1_norm': gain(ks[2], (DEPTH, D_MODEL)),
        'ffn1_w_gate': nrm(ks[3], (DEPTH, D_MODEL, D_FF), D_MODEL),
        'ffn1_w_up': nrm(ks[4], (DEPTH, D_MODEL, D_FF), D_MODEL),
        'ffn1_w_down': nrm(ks[5], (DEPTH, D_FF, D_MODEL), D_FF),
        'mix_norm': gain(ks[6], (DEPTH, D_MODEL)),
        'w_in': nrm(ks[7], (DEPTH, D_MODEL, D_IN), D_MODEL),
        'hg_lb_fwd': 0.1 * jax.random.normal(ks[8], (DEPTH + 1, HG_KW), jnp.float32),
        'hg_lb_bwd': 0.1 * jax.random.normal(ks[9], (DEPTH + 1, HG_KW), jnp.float32),
        'hg_out_norm': gain(ks[10], (DEPTH, HG_VW)),
        'q_norm': gain(ks[11], (DEPTH, AT_HD)),
        'k_norm': gain(ks[12], (DEPTH, AT_HD)),
        'w_up_a': nrm(ks[13], (DEPTH, HG_VW, D_MODEL), HG_VW),
        'w_up_b': nrm(ks[14], (DEPTH, AT_W, D_MODEL), AT_W),
        'w_out': nrm(ks[15], (DEPTH, D_MODEL, D_MODEL), D_MODEL),
        'ffn2_norm': gain(ks[16], (DEPTH, D_MODEL)),
        'ffn2_w_gate': nrm(ks[17], (DEPTH, D_MODEL, D_FF), D_MODEL),
        'ffn2_w_up': nrm(ks[18], (DEPTH, D_MODEL, D_FF), D_MODEL),
        'ffn2_w_down': nrm(ks[19], (DEPTH, D_FF, D_MODEL), D_FF),
    }


def _fwd_reference(x, meta_tokens, ffn1_norm, ffn1_w_gate, ffn1_w_up, ffn1_w_down, mix_norm, w_in, hg_lb_fwd, hg_lb_bwd, hg_out_norm, q_norm, k_norm, w_up_a, w_up_b, w_out, ffn2_norm, ffn2_w_gate, ffn2_w_up, ffn2_w_down):
    B, n_real, _ = x.shape
    meta = jnp.broadcast_to(meta_tokens.astype(x.dtype)[None], (B, N_META, D_MODEL))
    h = jnp.concatenate([meta, x], axis=1)
    cos, sin = axial_rope_tables(n_real)
    lb_fwd_all = jnp.cumsum(jax.nn.softmax(hg_lb_fwd.astype(jnp.float32), axis=0), axis=0)
    lb_bwd_all = jnp.cumsum(jax.nn.softmax(hg_lb_bwd.astype(jnp.float32), axis=0), axis=0)
    for layer in range(DEPTH):
        h = h + 0.5 * swiglu(rms_norm(h, ffn1_norm[layer]), ffn1_w_gate[layer], ffn1_w_up[layer], ffn1_w_down[layer])
        u = rms_norm(h, mix_norm[layer])
        z = u @ w_in[layer]
        zq_a, zi_a, zf_f, zf_b, zg_a, zq_b, zk_b, zv_b, zgate_a, zgate_b = split_cols(z, IN_SIZES)
        y_a = hgrn2_bidir(zq_a, zi_a, zf_f, zf_b, zg_a, lb_fwd_all[layer], lb_bwd_all[layer], hg_out_norm[layer])
        y_b = axial_gqa(zq_b, zk_b, zv_b, q_norm[layer], k_norm[layer], cos, sin)
        mixed = jax.nn.sigmoid(zgate_a) * (y_a @ w_up_a[layer]) + jax.nn.sigmoid(zgate_b) * (y_b @ w_up_b[layer])
        h = h + mixed @ w_out[layer]
        h = h + 0.5 * swiglu(rms_norm(h, ffn2_norm[layer]), ffn2_w_gate[layer], ffn2_w_up[layer], ffn2_w_down[layer])
    return h[:, N_META:]


import jax as _jax
import jax.numpy as _jnp

TWIN_FORMAT = 'train_step'
FWD_PARAMS = ['x', 'meta_tokens', 'ffn1_norm', 'ffn1_w_gate', 'ffn1_w_up', 'ffn1_w_down', 'mix_norm', 'w_in', 'hg_lb_fwd', 'hg_lb_bwd', 'hg_out_norm', 'q_norm', 'k_norm', 'w_up_a', 'w_up_b', 'w_out', 'ffn2_norm', 'ffn2_w_gate', 'ffn2_w_up', 'ffn2_w_down']
TWIN_WEIGHTS = ['meta_tokens', 'ffn1_norm', 'ffn1_w_gate', 'ffn1_w_up', 'ffn1_w_down', 'mix_norm', 'w_in', 'hg_lb_fwd', 'hg_lb_bwd', 'hg_out_norm', 'q_norm', 'k_norm', 'w_up_a', 'w_up_b', 'w_out', 'ffn2_norm', 'ffn2_w_gate', 'ffn2_w_up', 'ffn2_w_down']
TWIN_DIFF_INPUT = 'x'
TWIN_INPUTS = ['x', 'meta_tokens', 'ffn1_norm', 'ffn1_w_gate', 'ffn1_w_up', 'ffn1_w_down', 'mix_norm', 'w_in', 'hg_lb_fwd', 'hg_lb_bwd', 'hg_out_norm', 'q_norm', 'k_norm', 'w_up_a', 'w_up_b', 'w_out', 'ffn2_norm', 'ffn2_w_gate', 'ffn2_w_up', 'ffn2_w_down', 'loss_target', 'm_meta_tokens', 'm_ffn1_norm', 'm_ffn1_w_gate', 'm_ffn1_w_up', 'm_ffn1_w_down', 'm_mix_norm', 'm_w_in', 'm_hg_lb_fwd', 'm_hg_lb_bwd', 'm_hg_out_norm', 'm_q_norm', 'm_k_norm', 'm_w_up_a', 'm_w_up_b', 'm_w_out', 'm_ffn2_norm', 'm_ffn2_w_gate', 'm_ffn2_w_up', 'm_ffn2_w_down', 'v_meta_tokens', 'v_ffn1_norm', 'v_ffn1_w_gate', 'v_ffn1_w_up', 'v_ffn1_w_down', 'v_mix_norm', 'v_w_in', 'v_hg_lb_fwd', 'v_hg_lb_bwd', 'v_hg_out_norm', 'v_q_norm', 'v_k_norm', 'v_w_up_a', 'v_w_up_b', 'v_w_out', 'v_ffn2_norm', 'v_ffn2_w_gate', 'v_ffn2_w_up', 'v_ffn2_w_down']
TWIN_OUTPUTS = ['loss', 'grad_x', 'grad_meta_tokens', 'grad_ffn1_norm', 'grad_ffn1_w_gate', 'grad_ffn1_w_up', 'grad_ffn1_w_down', 'grad_mix_norm', 'grad_w_in', 'grad_hg_lb_fwd', 'grad_hg_lb_bwd', 'grad_hg_out_norm', 'grad_q_norm', 'grad_k_norm', 'grad_w_up_a', 'grad_w_up_b', 'grad_w_out', 'grad_ffn2_norm', 'grad_ffn2_w_gate', 'grad_ffn2_w_up', 'grad_ffn2_w_down', 'delta_meta_tokens', 'delta_ffn1_norm', 'delta_ffn1_w_gate', 'delta_ffn1_w_up', 'delta_ffn1_w_down', 'delta_mix_norm', 'delta_w_in', 'delta_hg_lb_fwd', 'delta_hg_lb_bwd', 'delta_hg_out_norm', 'delta_q_norm', 'delta_k_norm', 'delta_w_up_a', 'delta_w_up_b', 'delta_w_out', 'delta_ffn2_norm', 'delta_ffn2_w_gate', 'delta_ffn2_w_up', 'delta_ffn2_w_down', 'new_m_meta_tokens', 'new_m_ffn1_norm', 'new_m_ffn1_w_gate', 'new_m_ffn1_w_up', 'new_m_ffn1_w_down', 'new_m_mix_norm', 'new_m_w_in', 'new_m_hg_lb_fwd', 'new_m_hg_lb_bwd', 'new_m_hg_out_norm', 'new_m_q_norm', 'new_m_k_norm', 'new_m_w_up_a', 'new_m_w_up_b', 'new_m_w_out', 'new_m_ffn2_norm', 'new_m_ffn2_w_gate', 'new_m_ffn2_w_up', 'new_m_ffn2_w_down', 'new_v_meta_tokens', 'new_v_ffn1_norm', 'new_v_ffn1_w_gate', 'new_v_ffn1_w_up', 'new_v_ffn1_w_down', 'new_v_mix_norm', 'new_v_w_in', 'new_v_hg_lb_fwd', 'new_v_hg_lb_bwd', 'new_v_hg_out_norm', 'new_v_q_norm', 'new_v_k_norm', 'new_v_w_up_a', 'new_v_w_up_b', 'new_v_w_out', 'new_v_ffn2_norm', 'new_v_ffn2_w_gate', 'new_v_ffn2_w_up', 'new_v_ffn2_w_down']
TWIN_LEAF_KINDS = {'loss': 'loss', 'grad_x': 'grad_x', 'grad_meta_tokens': 'grad_w', 'grad_ffn1_norm': 'grad_w', 'grad_ffn1_w_gate': 'grad_w', 'grad_ffn1_w_up': 'grad_w', 'grad_ffn1_w_down': 'grad_w', 'grad_mix_norm': 'grad_w', 'grad_w_in': 'grad_w', 'grad_hg_lb_fwd': 'grad_w', 'grad_hg_lb_bwd': 'grad_w', 'grad_hg_out_norm': 'grad_w', 'grad_q_norm': 'grad_w', 'grad_k_norm': 'grad_w', 'grad_w_up_a': 'grad_w', 'grad_w_up_b': 'grad_w', 'grad_w_out': 'grad_w', 'grad_ffn2_norm': 'grad_w', 'grad_ffn2_w_gate': 'grad_w', 'grad_ffn2_w_up': 'grad_w', 'grad_ffn2_w_down': 'grad_w', 'delta_meta_tokens': 'delta_w', 'delta_ffn1_norm': 'delta_w', 'delta_ffn1_w_gate': 'delta_w', 'delta_ffn1_w_up': 'delta_w', 'delta_ffn1_w_down': 'delta_w', 'delta_mix_norm': 'delta_w', 'delta_w_in': 'delta_w', 'delta_hg_lb_fwd': 'delta_w', 'delta_hg_lb_bwd': 'delta_w', 'delta_hg_out_norm': 'delta_w', 'delta_q_norm': 'delta_w', 'delta_k_norm': 'delta_w', 'delta_w_up_a': 'delta_w', 'delta_w_up_b': 'delta_w', 'delta_w_out': 'delta_w', 'delta_ffn2_norm': 'delta_w', 'delta_ffn2_w_gate': 'delta_w', 'delta_ffn2_w_up': 'delta_w', 'delta_ffn2_w_down': 'delta_w', 'new_m_meta_tokens': 'new_m', 'new_m_ffn1_norm': 'new_m', 'new_m_ffn1_w_gate': 'new_m', 'new_m_ffn1_w_up': 'new_m', 'new_m_ffn1_w_down': 'new_m', 'new_m_mix_norm': 'new_m', 'new_m_w_in': 'new_m', 'new_m_hg_lb_fwd': 'new_m', 'new_m_hg_lb_bwd': 'new_m', 'new_m_hg_out_norm': 'new_m', 'new_m_q_norm': 'new_m', 'new_m_k_norm': 'new_m', 'new_m_w_up_a': 'new_m', 'new_m_w_up_b': 'new_m', 'new_m_w_out': 'new_m', 'new_m_ffn2_norm': 'new_m', 'new_m_ffn2_w_gate': 'new_m', 'new_m_ffn2_w_up': 'new_m', 'new_m_ffn2_w_down': 'new_m', 'new_v_meta_tokens': 'new_v', 'new_v_ffn1_norm': 'new_v', 'new_v_ffn1_w_gate': 'new_v', 'new_v_ffn1_w_up': 'new_v', 'new_v_ffn1_w_down': 'new_v', 'new_v_mix_norm': 'new_v', 'new_v_w_in': 'new_v', 'new_v_hg_lb_fwd': 'new_v', 'new_v_hg_lb_bwd': 'new_v', 'new_v_hg_out_norm': 'new_v', 'new_v_q_norm': 'new_v', 'new_v_k_norm': 'new_v', 'new_v_w_up_a': 'new_v', 'new_v_w_up_b': 'new_v', 'new_v_w_out': 'new_v', 'new_v_ffn2_norm': 'new_v', 'new_v_ffn2_w_gate': 'new_v', 'new_v_ffn2_w_up': 'new_v', 'new_v_ffn2_w_down': 'new_v'}


def _forward(args):
    return _fwd_reference(*[args[k] for k in FWD_PARAMS])


def _output_shape():
    def fwd():
        inp = _fwd_setup_inputs(0)
        return _fwd_reference(*[inp[k] for k in FWD_PARAMS])
    out = _jax.eval_shape(fwd)
    return out.shape, out.dtype

N_MICROBATCH = 1
ADAM_LR = 0.001
ADAM_B1 = 0.9
ADAM_B2 = 0.999
ADAM_EPS = 1e-08
ADAM_WD = 0.01
ADAM_STEP = 10
PER_EXAMPLE_BATCH_AXIS = {'x': 0, 'loss_target': 0}
SHARED_INPUTS = []
_WEIGHT_DTYPES = {'meta_tokens': _jnp.float32, 'ffn1_norm': _jnp.float32, 'ffn1_w_gate': _jnp.float32, 'ffn1_w_up': _jnp.float32, 'ffn1_w_down': _jnp.float32, 'mix_norm': _jnp.float32, 'w_in': _jnp.float32, 'hg_lb_fwd': _jnp.float32, 'hg_lb_bwd': _jnp.float32, 'hg_out_norm': _jnp.float32, 'q_norm': _jnp.float32, 'k_norm': _jnp.float32, 'w_up_a': _jnp.float32, 'w_up_b': _jnp.float32, 'w_out': _jnp.float32, 'ffn2_norm': _jnp.float32, 'ffn2_w_gate': _jnp.float32, 'ffn2_w_up': _jnp.float32, 'ffn2_w_down': _jnp.float32}
MOMENT_SCALE = {'meta_tokens': 4.867987e-03, 'ffn1_norm': 2.448844e+01, 'ffn1_w_gate': 1.460745e-01, 'ffn1_w_up': 1.800753e-01, 'ffn1_w_down': 2.972108e-01, 'mix_norm': 1.622324e+01, 'w_in': 2.123864e-01, 'hg_lb_fwd': 1.754158e-02, 'hg_lb_bwd': 1.947539e-02, 'hg_out_norm': 2.631803e+01, 'q_norm': 6.491271e-01, 'k_norm': 6.490504e-01, 'w_up_a': 4.913877e-01, 'w_up_b': 3.778320e-02, 'w_out': 4.168171e-01, 'ffn2_norm': 2.472636e+01, 'ffn2_w_gate': 1.249918e-01, 'ffn2_w_up': 1.768848e-01, 'ffn2_w_down': 2.897370e-01}


def _to_microbatches(a, axis):
    t = _jnp.moveaxis(a, axis, 0)
    t = t.reshape((N_MICROBATCH, t.shape[0] // N_MICROBATCH) + t.shape[1:])
    return _jnp.moveaxis(t, 1, axis + 1)


def setup_inputs(seed: int = 0) -> dict:
    inp = _fwd_setup_inputs(seed)
    key = _jax.random.fold_in(_jax.random.key(seed), 7919)
    shape, _ = _output_shape()
    out = dict(inp)
    out["loss_target"] = _jax.random.normal(_jax.random.fold_in(key, 0), shape, _jnp.float32)
    for i, name in enumerate(TWIN_WEIGHTS):
        w = inp[name].astype(_jnp.float32)
        if MOMENT_SCALE is None:
            s = _jnp.sqrt(_jnp.mean(_jnp.square(w)) + 1e-30)
        else:
            s = MOMENT_SCALE[name]
        km, kv = _jax.random.split(_jax.random.fold_in(key, i + 1))
        out[name] = w
        out["m_" + name] = s * _jax.random.normal(km, w.shape, _jnp.float32)
        out["v_" + name] = (s * s) * _jax.random.uniform(kv, w.shape, _jnp.float32, 0.5, 1.5)
    if N_MICROBATCH > 1:
        for name, axis in PER_EXAMPLE_BATCH_AXIS.items():
            out[name] = _to_microbatches(out[name], axis)
    return {'x': out['x'], 'meta_tokens': out['meta_tokens'], 'ffn1_norm': out['ffn1_norm'], 'ffn1_w_gate': out['ffn1_w_gate'], 'ffn1_w_up': out['ffn1_w_up'], 'ffn1_w_down': out['ffn1_w_down'], 'mix_norm': out['mix_norm'], 'w_in': out['w_in'], 'hg_lb_fwd': out['hg_lb_fwd'], 'hg_lb_bwd': out['hg_lb_bwd'], 'hg_out_norm': out['hg_out_norm'], 'q_norm': out['q_norm'], 'k_norm': out['k_norm'], 'w_up_a': out['w_up_a'], 'w_up_b': out['w_up_b'], 'w_out': out['w_out'], 'ffn2_norm': out['ffn2_norm'], 'ffn2_w_gate': out['ffn2_w_gate'], 'ffn2_w_up': out['ffn2_w_up'], 'ffn2_w_down': out['ffn2_w_down'], 'loss_target': out['loss_target'], 'm_meta_tokens': out['m_meta_tokens'], 'm_ffn1_norm': out['m_ffn1_norm'], 'm_ffn1_w_gate': out['m_ffn1_w_gate'], 'm_ffn1_w_up': out['m_ffn1_w_up'], 'm_ffn1_w_down': out['m_ffn1_w_down'], 'm_mix_norm': out['m_mix_norm'], 'm_w_in': out['m_w_in'], 'm_hg_lb_fwd': out['m_hg_lb_fwd'], 'm_hg_lb_bwd': out['m_hg_lb_bwd'], 'm_hg_out_norm': out['m_hg_out_norm'], 'm_q_norm': out['m_q_norm'], 'm_k_norm': out['m_k_norm'], 'm_w_up_a': out['m_w_up_a'], 'm_w_up_b': out['m_w_up_b'], 'm_w_out': out['m_w_out'], 'm_ffn2_norm': out['m_ffn2_norm'], 'm_ffn2_w_gate': out['m_ffn2_w_gate'], 'm_ffn2_w_up': out['m_ffn2_w_up'], 'm_ffn2_w_down': out['m_ffn2_w_down'], 'v_meta_tokens': out['v_meta_tokens'], 'v_ffn1_norm': out['v_ffn1_norm'], 'v_ffn1_w_gate': out['v_ffn1_w_gate'], 'v_ffn1_w_up': out['v_ffn1_w_up'], 'v_ffn1_w_down': out['v_ffn1_w_down'], 'v_mix_norm': out['v_mix_norm'], 'v_w_in': out['v_w_in'], 'v_hg_lb_fwd': out['v_hg_lb_fwd'], 'v_hg_lb_bwd': out['v_hg_lb_bwd'], 'v_hg_out_norm': out['v_hg_out_norm'], 'v_q_norm': out['v_q_norm'], 'v_k_norm': out['v_k_norm'], 'v_w_up_a': out['v_w_up_a'], 'v_w_up_b': out['v_w_up_b'], 'v_w_out': out['v_w_out'], 'v_ffn2_norm': out['v_ffn2_norm'], 'v_ffn2_w_gate': out['v_ffn2_w_gate'], 'v_ffn2_w_up': out['v_ffn2_w_up'], 'v_ffn2_w_down': out['v_ffn2_w_down']}


def _loss(weights, diff, rest, loss_target):
    with _jax.named_scope("forward"):
        args = {**rest, TWIN_DIFF_INPUT: diff, **{k: w.astype(_WEIGHT_DTYPES[k]) for k, w in weights.items()}}
        y = _forward(args)
    with _jax.named_scope("loss_head"):
        err = _jnp.square(y.astype(_jnp.float32) - loss_target)
        return 0.5 * _jnp.sum(_jnp.mean(err, axis=-1)) if err.ndim else 0.5 * err


def _adamw(w, g, m, v):
    m = ADAM_B1 * m + (1.0 - ADAM_B1) * g
    v = ADAM_B2 * v + (1.0 - ADAM_B2) * _jnp.square(g)
    m_hat = m / (1.0 - ADAM_B1 ** ADAM_STEP)
    v_hat = v / (1.0 - ADAM_B2 ** ADAM_STEP)
    delta = -ADAM_LR * (m_hat / (_jnp.sqrt(v_hat) + ADAM_EPS) + ADAM_WD * w)
    return delta, m, v


def reference(x, meta_tokens, ffn1_norm, ffn1_w_gate, ffn1_w_up, ffn1_w_down, mix_norm, w_in, hg_lb_fwd, hg_lb_bwd, hg_out_norm, q_norm, k_norm, w_up_a, w_up_b, w_out, ffn2_norm, ffn2_w_gate, ffn2_w_up, ffn2_w_down, loss_target, m_meta_tokens, m_ffn1_norm, m_ffn1_w_gate, m_ffn1_w_up, m_ffn1_w_down, m_mix_norm, m_w_in, m_hg_lb_fwd, m_hg_lb_bwd, m_hg_out_norm, m_q_norm, m_k_norm, m_w_up_a, m_w_up_b, m_w_out, m_ffn2_norm, m_ffn2_w_gate, m_ffn2_w_up, m_ffn2_w_down, v_meta_tokens, v_ffn1_norm, v_ffn1_w_gate, v_ffn1_w_up, v_ffn1_w_down, v_mix_norm, v_w_in, v_hg_lb_fwd, v_hg_lb_bwd, v_hg_out_norm, v_q_norm, v_k_norm, v_w_up_a, v_w_up_b, v_w_out, v_ffn2_norm, v_ffn2_w_gate, v_ffn2_w_up, v_ffn2_w_down):
    given = dict(x=x, meta_tokens=meta_tokens, ffn1_norm=ffn1_norm, ffn1_w_gate=ffn1_w_gate, ffn1_w_up=ffn1_w_up, ffn1_w_down=ffn1_w_down, mix_norm=mix_norm, w_in=w_in, hg_lb_fwd=hg_lb_fwd, hg_lb_bwd=hg_lb_bwd, hg_out_norm=hg_out_norm, q_norm=q_norm, k_norm=k_norm, w_up_a=w_up_a, w_up_b=w_up_b, w_out=w_out, ffn2_norm=ffn2_norm, ffn2_w_gate=ffn2_w_gate, ffn2_w_up=ffn2_w_up, ffn2_w_down=ffn2_w_down, loss_target=loss_target, m_meta_tokens=m_meta_tokens, m_ffn1_norm=m_ffn1_norm, m_ffn1_w_gate=m_ffn1_w_gate, m_ffn1_w_up=m_ffn1_w_up, m_ffn1_w_down=m_ffn1_w_down, m_mix_norm=m_mix_norm, m_w_in=m_w_in, m_hg_lb_fwd=m_hg_lb_fwd, m_hg_lb_bwd=m_hg_lb_bwd, m_hg_out_norm=m_hg_out_norm, m_q_norm=m_q_norm, m_k_norm=m_k_norm, m_w_up_a=m_w_up_a, m_w_up_b=m_w_up_b, m_w_out=m_w_out, m_ffn2_norm=m_ffn2_norm, m_ffn2_w_gate=m_ffn2_w_gate, m_ffn2_w_up=m_ffn2_w_up, m_ffn2_w_down=m_ffn2_w_down, v_meta_tokens=v_meta_tokens, v_ffn1_norm=v_ffn1_norm, v_ffn1_w_gate=v_ffn1_w_gate, v_ffn1_w_up=v_ffn1_w_up, v_ffn1_w_down=v_ffn1_w_down, v_mix_norm=v_mix_norm, v_w_in=v_w_in, v_hg_lb_fwd=v_hg_lb_fwd, v_hg_lb_bwd=v_hg_lb_bwd, v_hg_out_norm=v_hg_out_norm, v_q_norm=v_q_norm, v_k_norm=v_k_norm, v_w_up_a=v_w_up_a, v_w_up_b=v_w_up_b, v_w_out=v_w_out, v_ffn2_norm=v_ffn2_norm, v_ffn2_w_gate=v_ffn2_w_gate, v_ffn2_w_up=v_ffn2_w_up, v_ffn2_w_down=v_ffn2_w_down)
    weights = {n: given[n] for n in TWIN_WEIGHTS}
    shared = {n: given[n] for n in SHARED_INPUTS}
    per_example = {n: given[n] for n in ['x']}
    grad_fn = _jax.value_and_grad(_loss, argnums=(0, 1))

    def one_microbatch(ex, loss_target):
        ex = dict(ex)
        diff = ex.pop(TWIN_DIFF_INPUT)
        return grad_fn(weights, diff, {**shared, **ex}, loss_target)

    if N_MICROBATCH == 1:
        loss, (grad_w, grad_x) = one_microbatch(per_example, given["loss_target"])
    else:
        def body(carry, xs):
            loss_sum, grad_sum = carry
            l_k, (gw_k, gx_k) = one_microbatch(xs[0], xs[1])
            with _jax.named_scope("update"):
                return (loss_sum + l_k, _jax.tree.map(_jnp.add, grad_sum, gw_k)), gx_k

        init = (_jnp.zeros((), _jnp.float32), _jax.tree.map(_jnp.zeros_like, weights))
        (loss, grad_w), grad_x = _jax.lax.scan(body, init, (per_example, given["loss_target"]))
    with _jax.named_scope("update"):
        delta_w, new_m, new_v = {}, {}, {}
        for n in TWIN_WEIGHTS:
            delta_w[n], new_m[n], new_v[n] = _adamw(weights[n], grad_w[n], given["m_" + n], given["v_" + n])
    return (loss, grad_x, *[grad_w[n] for n in TWIN_WEIGHTS], *[delta_w[n] for n in TWIN_WEIGHTS],
            *[new_m[n] for n in TWIN_WEIGHTS], *[new_v[n] for n in TWIN_WEIGHTS])
```

```python
import functools
import math

import numpy as np
import jax
import jax.numpy as jnp
from jax import lax
from jax.experimental import pallas as pl
from jax.experimental.pallas import tpu as pltpu

F32 = jnp.float32
BF16 = jnp.bfloat16
SDS = jax.ShapeDtypeStruct
MESH = pl.DeviceIdType.MESH

EPS = 1e-6
N_META = 16
PAD = 512
LANE = 128
CHUNK = 128
HG_HEADS = 4
HG_W = HG_HEADS * 128
AT_HEADS = 8
AT_KV = 2
AT_HD = 64
AT_W = AT_HEADS * AT_HD
AT_KVW = AT_KV * AT_HD
GRID_W = 64
ROPE_THETA = 10000.0
Z_HG = 5 * HG_W
Z_AT = AT_W + 2 * AT_KVW
ADAM_LR, ADAM_B1, ADAM_B2, ADAM_EPS, ADAM_WD, ADAM_STEP = 0.001, 0.9, 0.999, 1e-08, 0.01, 10
VMEM_DEFAULT = 48 * 1024 * 1024
VMEM_LARGE = 60 * 1024 * 1024
NEG = -1e30

MATS = ('ffn1_w_gate', 'ffn1_w_up', 'ffn1_w_down', 'w_in', 'w_up_a', 'w_up_b', 'w_out',
        'ffn2_w_gate', 'ffn2_w_up', 'ffn2_w_down')
ROW_SHARDED = ('ffn1_w_down', 'w_out', 'ffn2_w_down')
SMALLS = ('ffn1_norm', 'mix_norm', 'hg_lb_fwd', 'hg_lb_bwd', 'hg_out_norm', 'q_norm', 'k_norm', 'ffn2_norm')
WEIGHTS = ('meta_tokens', 'ffn1_norm', 'ffn1_w_gate', 'ffn1_w_up', 'ffn1_w_down', 'mix_norm', 'w_in', 'hg_lb_fwd',
           'hg_lb_bwd', 'hg_out_norm', 'q_norm', 'k_norm', 'w_up_a', 'w_up_b', 'w_out', 'ffn2_norm', 'ffn2_w_gate',
           'ffn2_w_up', 'ffn2_w_down')


def _params(sem=None, vmem=VMEM_DEFAULT):
    return pltpu.CompilerParams(dimension_semantics=sem, vmem_limit_bytes=vmem)


def _tile(n, pref, q=LANE):
    for d in range(min(pref, n), 0, -1):
        if n % d == 0 and d % q == 0:
            return d
    return n


def _sigmoid(x):
    return 1.0 / (1.0 + jnp.exp(-x))


def _dot(a, b, dims):
    return lax.dot_general(a, b, (dims, ((), ())), preferred_element_type=F32)


def _nn(a, b):
    return _dot(a, b, ((1,), (0,)))


def _nt(a, b):
    return _dot(a, b, ((1,), (1,)))


def _tn(a, b):
    return _dot(a, b, ((0,), (0,)))


def _split3(x):
    x1 = x.astype(BF16)
    r = x - x1.astype(F32)
    x2 = r.astype(BF16)
    x3 = (r - x2.astype(F32)).astype(BF16)
    return x1, x2, x3


def _exact_left(m01, x):
    x1, x2, x3 = _split3(x)
    return _nn(m01, x1) + _nn(m01, x2) + _nn(m01, x3)


def _exact_right(x, m01):
    x1, x2, x3 = _split3(x)
    return _nn(x1, m01) + _nn(x2, m01) + _nn(x3, m01)


def _mm(pairs, *, name, ta=False, tb=False, out_dtype=F32, tm=512, tn=1024, tk=1024, alpha=1.0, res=None):
    a0, b0 = pairs[0]
    M = a0.shape[1] if ta else a0.shape[0]
    K = a0.shape[0] if ta else a0.shape[1]
    N = b0.shape[0] if tb else b0.shape[1]
    tm, tn, tk = _tile(M, tm), _tile(N, tn), _tile(K, tk)
    nk = K // tk
    npair = len(pairs)
    dims = ((0 if ta else 1,), (1 if tb else 0,))

    def body(*refs):
        ab = refs[:2 * npair]
        pos = 2 * npair
        res_ref = None
        if res is not None:
            res_ref = refs[pos]
            pos += 1
        o_ref = refs[pos]

        def partial_sum():
            tot = None
            for p in range(npair):
                d = _dot(ab[2 * p][...].astype(BF16), ab[2 * p + 1][...].astype(BF16), dims)
                tot = d if tot is None else tot + d
            return tot

        def finish(acc):
            r = acc if alpha == 1.0 else acc * alpha
            if res_ref is not None:
                r = res_ref[...] + r
            o_ref[...] = r.astype(out_dtype)

        if nk == 1:
            finish(partial_sum())
        else:
            acc_ref = refs[pos + 1]
            k = pl.program_id(2)

            @pl.when(k == 0)
            def _():
                acc_ref[...] = jnp.zeros_like(acc_ref)

            acc_ref[...] += partial_sum()

            @pl.when(k == nk - 1)
            def _():
                finish(acc_ref[...])

    a_spec = pl.BlockSpec((tk, tm), lambda j, i, k: (k, i)) if ta else pl.BlockSpec((tm, tk), lambda j, i, k: (i, k))
    b_spec = pl.BlockSpec((tn, tk), lambda j, i, k: (j, k)) if tb else pl.BlockSpec((tk, tn), lambda j, i, k: (k, j))
    o_spec = pl.BlockSpec((tm, tn), lambda j, i, k: (i, j))
    in_specs, args = [], []
    for a, b in pairs:
        in_specs += [a_spec, b_spec]
        args += [a, b]
    if res is not None:
        in_specs.append(o_spec)
        args.append(res)
    return pl.pallas_call(
        body, grid=(N // tn, M // tm, nk), in_specs=in_specs, out_specs=o_spec,
        out_shape=SDS((M, N), out_dtype),
        scratch_shapes=[pltpu.VMEM((tm, tn), F32)] if nk > 1 else [],
        compiler_params=_params(("parallel", "parallel", "arbitrary")), name=name)(*args)


def _rmsnorm_fwd(h, w, *, name):
    L, D = h.shape
    tm = _tile(L, 512)

    def body(h_ref, w_ref, o_ref):
        x = h_ref[...]
        r = lax.rsqrt(jnp.mean(x * x, axis=-1, keepdims=True) + EPS)
        o_ref[...] = (x * r * w_ref[...]).astype(BF16)

    return pl.pallas_call(
        body, grid=(L // tm,),
        in_specs=[pl.BlockSpec((tm, D), lambda i: (i, 0)), pl.BlockSpec((1, D), lambda i: (0, 0))],
        out_specs=pl.BlockSpec((tm, D), lambda i: (i, 0)), out_shape=SDS((L, D), BF16),
        compiler_params=_params(("parallel",)), name=name)(h, w)


def _rmsnorm_bwd(h, w, dn, dres, *, name):
    L, D = h.shape
    tm = _tile(L, 512)

    def body(h_ref, w_ref, dn_ref, dres_ref, dh_ref, dw_ref):
        x = h_ref[...]
        r = lax.rsqrt(jnp.mean(x * x, axis=-1, keepdims=True) + EPS)
        xh = x * r
        dn = dn_ref[...]
        dxh = dn * w_ref[...]
        dh_ref[...] = dres_ref[...] + r * (dxh - xh * jnp.mean(dxh * xh, axis=-1, keepdims=True))

        @pl.when(pl.program_id(0) == 0)
        def _():
            dw_ref[...] = jnp.zeros_like(dw_ref)

        dw_ref[...] += jnp.sum(dn * xh, axis=0, keepdims=True)

    row = pl.BlockSpec((tm, D), lambda i: (i, 0))
    vec = pl.BlockSpec((1, D), lambda i: (0, 0))
    return pl.pallas_call(
        body, grid=(L // tm,), in_specs=[row, vec, row, row], out_specs=[row, vec],
        out_shape=[SDS((L, D), F32), SDS((1, D), F32)],
        compiler_params=_params(("arbitrary",)), name=name)(h, w, dn, dres)


def _ffn_up(n, wg, wu, *, name):
    L, D = n.shape
    Fd = wg.shape[1]
    tm, tn = _tile(L, 512), _tile(Fd, 1408)

    def body(n_ref, wg_ref, wu_ref, g_ref, u_ref, a_ref):
        x = n_ref[...]
        g = _nn(x, wg_ref[...])
        u = _nn(x, wu_ref[...])
        g_ref[...] = g
        u_ref[...] = u
        a_ref[...] = (g * _sigmoid(g) * u).astype(BF16)

    wspec = pl.BlockSpec((D, tn), lambda j, i: (0, j))
    ospec = pl.BlockSpec((tm, tn), lambda j, i: (i, j))
    return pl.pallas_call(
        body, grid=(Fd // tn, L // tm),
        in_specs=[pl.BlockSpec((tm, D), lambda j, i: (i, 0)), wspec, wspec],
        out_specs=[ospec, ospec, ospec],
        out_shape=[SDS((L, Fd), F32), SDS((L, Fd), F32), SDS((L, Fd), BF16)],
        compiler_params=_params(("parallel", "parallel")), name=name)(n, wg, wu)


def _ffn_dact(dh, wd, g, u, *, name):
    L, D = dh.shape
    Fd = wd.shape[0]
    tm, tn = _tile(L, 512), _tile(Fd, 1408)

    def body(dh_ref, wd_ref, g_ref, u_ref, dg_ref, du_ref):
        da = 0.5 * _nt(dh_ref[...].astype(BF16), wd_ref[...])
        g = g_ref[...]
        sg = _sigmoid(g)
        dg_ref[...] = (da * u_ref[...] * (sg * (1.0 + g * (1.0 - sg)))).astype(BF16)
        du_ref[...] = (da * (g * sg)).astype(BF16)

    ospec = pl.BlockSpec((tm, tn), lambda j, i: (i, j))
    return pl.pallas_call(
        body, grid=(Fd // tn, L // tm),
        in_specs=[pl.BlockSpec((tm, D), lambda j, i: (i, 0)), pl.BlockSpec((tn, D), lambda j, i: (j, 0)), ospec, ospec],
        out_specs=[ospec, ospec], out_shape=[SDS((L, Fd), BF16), SDS((L, Fd), BF16)],
        compiler_params=_params(("parallel", "parallel")), name=name)(dh, wd, g, u)


def _hg_masks(rev):
    t = lax.broadcasted_iota(jnp.int32, (CHUNK, CHUNK), 0)
    s = lax.broadcasted_iota(jnp.int32, (CHUNK, CHUNK), 1)
    causal = (s >= t) if rev else (s <= t)
    levels = []
    for sh in (6, 5, 4):
        same = jnp.right_shift(t, sh + 1) == jnp.right_shift(s, sh + 1)
        tq = jnp.bitwise_and(jnp.right_shift(t, sh), 1)
        sk = jnp.bitwise_and(jnp.right_shift(s, sh), 1)
        levels.append(same & (tq == (0 if rev else 1)) & (sk == (1 if rev else 0)))
    diag = (jnp.right_shift(t, 4) == jnp.right_shift(s, 4)) & causal
    return causal, levels, diag


def _hg_intra_factors(q, k, b, b_scr, rev):
    b_scr[...] = b
    row = lax.broadcasted_iota(jnp.int32, (CHUNK, LANE), 0)
    out = []
    for sh in (6, 5, 4):
        lb = 1 << sh
        pieces = []
        for p in range(0, CHUNK, 2 * lb):
            r = p + lb if rev else p + lb - 1
            pieces.append(jnp.broadcast_to(b_scr[pl.ds(r, 1), :], (2 * lb, LANE)))
        ref = pieces[0] if len(pieces) == 1 else jnp.concatenate(pieces, axis=0)
        qside = jnp.bitwise_and(jnp.right_shift(row, sh), 1) == (0 if rev else 1)
        eq = jnp.where(qside, jnp.exp(jnp.minimum(b - ref, 0.0)), 0.0)
        ek = jnp.where(qside, 0.0, jnp.exp(jnp.minimum(ref - b, 0.0)))
        out.append((eq, ek, (q * eq).astype(BF16), (k * ek).astype(BF16)))
    pieces = []
    for a in range(0, CHUNK, 16):
        r = a + (8 if rev else 7)
        pieces.append(jnp.broadcast_to(b_scr[pl.ds(r, 1), :], (16, LANE)))
    ref = jnp.concatenate(pieces, axis=0)
    eq = jnp.exp(jnp.minimum(b - ref, 80.0))
    ek = jnp.exp(jnp.minimum(ref - b, 80.0))
    out.append((eq, ek, (q * eq).astype(BF16), (k * ek).astype(BF16)))
    return out


def _hg_gate(zf, l0, l1, valid):
    mx = jnp.maximum(l0, l1)
    e0, e1 = jnp.exp(l0 - mx), jnp.exp(l1 - mx)
    p0 = e0 / (e0 + e1)
    sg = _sigmoid(-zf)
    k = jnp.where(valid, (1.0 - p0) * sg, 0.0)
    return p0, sg, k, jnp.log(1.0 - k)


def _hg_fwd(z, lbp, *, rev, name):
    L = z.shape[0]
    nc = L // CHUNK
    fcol = 3 if rev else 2

    def cidx(j):
        return nc - 1 - j if rev else j

    def body(zq_ref, zi_ref, zf_ref, lb_ref, o_ref, ssave_ref, st_scr, b_scr):
        j = pl.program_id(0)

        @pl.when(j == 0)
        def _():
            st_scr[...] = jnp.zeros_like(st_scr)

        causal, lmasks, dmask = _hg_masks(rev)
        tri = jnp.where(causal, 1.0, 0.0).astype(BF16)
        rowg = cidx(j) * CHUNK + lax.broadcasted_iota(jnp.int32, (CHUNK, LANE), 0)
        valid = rowg >= PAD - N_META
        last = 0 if rev else CHUNK - 1
        for hh in range(HG_HEADS):
            sl = slice(LANE * hh, LANE * (hh + 1))
            zq = zq_ref[:, sl]
            q = zq * _sigmoid(zq)
            v = zi_ref[:, sl].astype(BF16)
            _, _, k, g = _hg_gate(zf_ref[:, sl], lb_ref[0:1, sl], lb_ref[1:2, sl], valid)
            b = _exact_left(tri, g)
            st = st_scr[hh]
            ssave_ref[0, hh] = st
            o = _nt((q * jnp.exp(b)).astype(BF16), st.astype(BF16))
            a = None
            fac = _hg_intra_factors(q, k, b, b_scr, rev)
            for (eq, ek, qq, kk), msk in zip(fac, lmasks + [dmask]):
                t = jnp.where(msk, _nt(qq, kk), 0.0)
                a = t if a is None else a + t
            o_ref[:, sl] = o + _nn(a.astype(BF16), v)
            bl = b_scr[pl.ds(last, 1), :]
            kd = (k * jnp.exp(bl - b)).astype(BF16)
            st_scr[hh] = st * jnp.exp(bl) + _tn(v, kd)

    zspec = lambda col: pl.BlockSpec((CHUNK, HG_W), lambda j: (cidx(j), col))
    return pl.pallas_call(
        body, grid=(nc,),
        in_specs=[zspec(0), zspec(1), zspec(fcol), pl.BlockSpec((2, HG_W), lambda j: (0, 0))],
        out_specs=[pl.BlockSpec((CHUNK, HG_W), lambda j: (cidx(j), 0)),
                   pl.BlockSpec((1, HG_HEADS, LANE, LANE), lambda j: (cidx(j), 0, 0, 0))],
        out_shape=[SDS((L, HG_W), F32), SDS((nc, HG_HEADS, LANE, LANE), F32)],
        scratch_shapes=[pltpu.VMEM((HG_HEADS, LANE, LANE), F32), pltpu.VMEM((CHUNK, LANE), F32)],
        compiler_params=_params(("arbitrary",)), name=name)(z, z, z, lbp)


def _hg_bwd(z, lbp, do, ssave, prev, *, rev, name):
    L = z.shape[0]
    nc = L // CHUNK
    fcol = 3 if rev else 2
    final = prev is not None

    def cidx(j):
        return j if rev else nc - 1 - j

    def body(*refs):
        zq_ref, zi_ref, zf_ref, lb_ref, do_ref, ss_ref = refs[:6]
        pos = 6
        if final:
            dqin_ref, dvin_ref = refs[6:8]
            pos = 8
        dq_ref, dv_ref, dzf_ref, dlb_ref, dst_scr, b_scr = refs[pos:pos + 6]
        j = pl.program_id(0)

        @pl.when(j == 0)
        def _():
            dst_scr[...] = jnp.zeros_like(dst_scr)
            dlb_ref[...] = jnp.zeros_like(dlb_ref)

        causal, lmasks, dmask = _hg_masks(rev)
        tri = jnp.where(causal, 1.0, 0.0).astype(BF16)
        ti = lax.broadcasted_iota(jnp.int32, (CHUNK, CHUNK), 0)
        si = lax.broadcasted_iota(jnp.int32, (CHUNK, CHUNK), 1)
        tri_t = jnp.where((si <= ti) if rev else (si >= ti), 1.0, 0.0).astype(BF16)
        rowg = cidx(j) * CHUNK + lax.broadcasted_iota(jnp.int32, (CHUNK, LANE), 0)
        valid = rowg >= PAD - N_META
        last = 0 if rev else CHUNK - 1
        for hh in range(HG_HEADS):
            sl = slice(LANE * hh, LANE * (hh + 1))
            zq = zq_ref[:, sl]
            sq = _sigmoid(zq)
            q = zq * sq
            v = zi_ref[:, sl].astype(BF16)
            p0, sg, k, g = _hg_gate(zf_ref[:, sl], lb_ref[0:1, sl], lb_ref[1:2, sl], valid)
            b = _exact_left(tri, g)
            dob = do_ref[:, sl].astype(BF16)
            st = ss_ref[0, hh]
            dst = dst_scr[hh]
            stb, dstb = st.astype(BF16), dst.astype(BF16)
            eb = jnp.exp(b)
            qe = (q * eb).astype(BF16)
            fac = _hg_intra_factors(q, k, b, b_scr, rev)
            bl = b_scr[pl.ds(last, 1), :]
            ebl = jnp.exp(bl)
            kde = jnp.exp(bl - b)
            kd = (k * kde).astype(BF16)
            da = jnp.where(causal, _nt(dob, v), 0.0)
            dq = eb * _nn(dob, stb)
            dk_inter = kde * _nn(v, dstb)
            dk = dk_inter
            dv = _nt(kd, dstb)
            a = None
            db = q * dq - k * dk
            for (eq, ek, qq, kk), msk in zip(fac, lmasks + [dmask]):
                t = jnp.where(msk, _nt(qq, kk), 0.0)
                a = t if a is None else a + t
                dal = jnp.where(msk, da, 0.0).astype(BF16)
                mq = _nn(dal, kk)
                mk = _tn(dal, qq)
                dq = dq + eq * mq
                dk = dk + ek * mk
                db = db + (qq.astype(F32) * mq - kk.astype(F32) * mk)
            dv = dv + _tn(a.astype(BF16), dob)
            extra = ebl * jnp.sum(st * dst, axis=0, keepdims=True) + jnp.sum(k * dk_inter, axis=0, keepdims=True)
            dst_scr[hh] = dst * ebl + _tn(dob, qe)
            dg = _exact_left(tri_t, db) + extra
            dk_tot = dk - dg / (1.0 - k)
            dzf_ref[:, sl] = jnp.where(valid, dk_tot * (1.0 - p0) * (-sg * (1.0 - sg)), 0.0).astype(BF16)
            dlb_ref[:, sl] += jnp.sum(jnp.where(valid, -sg * dk_tot, 0.0), axis=0, keepdims=True)
            if final:
                dq_ref[:, sl] = ((dq + dqin_ref[:, sl]) * (sq * (1.0 + zq * (1.0 - sq)))).astype(BF16)
                dv_ref[:, sl] = (dv + dvin_ref[:, sl]).astype(BF16)
            else:
                dq_ref[:, sl] = dq
                dv_ref[:, sl] = dv

    zspec = lambda col: pl.BlockSpec((CHUNK, HG_W), lambda j: (cidx(j), col))
    rspec = pl.BlockSpec((CHUNK, HG_W), lambda j: (cidx(j), 0))
    in_specs = [zspec(0), zspec(1), zspec(fcol), pl.BlockSpec((2, HG_W), lambda j: (0, 0)), rspec,
                pl.BlockSpec((1, HG_HEADS, LANE, LANE), lambda j: (cidx(j), 0, 0, 0))]
    args = [z, z, z, lbp, do, ssave]
    if final:
        in_specs += [rspec, rspec]
        args += list(prev)
    odt = BF16 if final else F32
    return pl.pallas_call(
        body, grid=(nc,), in_specs=in_specs,
        out_specs=[rspec, rspec, rspec, pl.BlockSpec((1, HG_W), lambda j: (0, 0))],
        out_shape=[SDS((L, HG_W), odt), SDS((L, HG_W), odt), SDS((L, HG_W), BF16), SDS((1, HG_W), F32)],
        scratch_shapes=[pltpu.VMEM((HG_HEADS, LANE, LANE), F32), pltpu.VMEM((CHUNK, LANE), F32)],
        compiler_params=_params(("arbitrary",)), name=name)(*args)


def _hg_post_fwd(of, ob, z, w, *, name):
    L = of.shape[0]
    tm = _tile(L, 512)

    def body(of_ref, ob_ref, zg_ref, w_ref, y_ref):
        for hh in range(HG_HEADS):
            sl = slice(LANE * hh, LANE * (hh + 1))
            o = of_ref[:, sl] + ob_ref[:, sl]
            r = lax.rsqrt(jnp.mean(o * o, axis=-1, keepdims=True) + EPS)
            zg = zg_ref[:, sl]
            y_ref[:, sl] = (o * r * w_ref[:, sl] * (zg * _sigmoid(zg))).astype(BF16)

    row = pl.BlockSpec((tm, HG_W), lambda i: (i, 0))
    return pl.pallas_call(
        body, grid=(L // tm,),
        in_specs=[row, row, pl.BlockSpec((tm, HG_W), lambda i: (i, 4)), pl.BlockSpec((1, HG_W), lambda i: (0, 0))],
        out_specs=row, out_shape=SDS((L, HG_W), BF16),
        compiler_params=_params(("parallel",)), name=name)(of, ob, z, w)


def _hg_post_bwd(dy, of, ob, z, w, *, name):
    L = of.shape[0]
    tm = _tile(L, 512)

    def body(dy_ref, of_ref, ob_ref, zg_ref, w_ref, do_ref, dzg_ref, dw_ref):
        @pl.when(pl.program_id(0) == 0)
        def _():
            dw_ref[...] = jnp.zeros_like(dw_ref)

        for hh in range(HG_HEADS):
            sl = slice(LANE * hh, LANE * (hh + 1))
            o = of_ref[:, sl] + ob_ref[:, sl]
            r = lax.rsqrt(jnp.mean(o * o, axis=-1, keepdims=True) + EPS)
            xh = o * r
            zg = zg_ref[:, sl]
            sg = _sigmoid(zg)
            w = w_ref[:, sl]
            dy = dy_ref[:, sl]
            dys = dy * (zg * sg)
            dzg_ref[:, sl] = (dy * xh * w * (sg * (1.0 + zg * (1.0 - sg)))).astype(BF16)
            dw_ref[:, sl] += jnp.sum(dys * xh, axis=0, keepdims=True)
            dxh = dys * w
            do_ref[:, sl] = r * (dxh - xh * jnp.mean(dxh * xh, axis=-1, keepdims=True))

    row = pl.BlockSpec((tm, HG_W), lambda i: (i, 0))
    vec = pl.BlockSpec((1, HG_W), lambda i: (0, 0))
    return pl.pallas_call(
        body, grid=(L // tm,),
        in_specs=[row, row, row, pl.BlockSpec((tm, HG_W), lambda i: (i, 4)), vec],
        out_specs=[row, row, vec],
        out_shape=[SDS((L, HG_W), F32), SDS((L, HG_W), BF16), SDS((1, HG_W), F32)],
        compiler_params=_params(("arbitrary",)), name=name)(dy, of, ob, z, w)


N_GROUPS = (AT_HEADS + AT_KV) // 2


def _qk_to_group(wqk):
    d = wqk.shape[0]
    return wqk.reshape(d, N_GROUPS, 2, AT_HD // 2, 2).transpose(0, 1, 4, 2, 3).reshape(d, N_GROUPS * LANE)


def _qk_from_group(wqk):
    d = wqk.shape[0]
    return wqk.reshape(d, N_GROUPS, 2, 2, AT_HD // 2).transpose(0, 1, 3, 4, 2).reshape(d, N_GROUPS * LANE)


def _group_vec(w64):
    halves = w64.reshape(AT_HD // 2, 2).T
    return jnp.broadcast_to(halves[:, None, :], (2, 2, AT_HD // 2)).reshape(1, LANE)


def _ungroup_vec(w128):
    w = w128.reshape(2, 2, 32).sum(axis=1)
    return w.T.reshape(1, AT_HD)


def _rope_tables(L):
    n_real = L - PAD
    t = np.arange(n_real)
    row = np.concatenate([np.zeros(PAD), t // GRID_W]).astype(np.float32)
    col = np.concatenate([np.zeros(PAD), t % GRID_W]).astype(np.float32)
    inv = jnp.asarray(ROPE_THETA, F32) ** (-jnp.arange(0, AT_HD // 2, 2, dtype=F32) / (AT_HD // 2))
    ang = jnp.concatenate([jnp.asarray(row)[:, None] * inv, jnp.asarray(col)[:, None] * inv], axis=-1)
    cos, sin = jnp.cos(ang), jnp.sin(ang)
    cc = jnp.tile(cos, (1, 4))
    ss = jnp.concatenate([-sin, -sin, sin, sin], axis=1)
    return cc, ss


def _seg_matrix():
    a = lax.broadcasted_iota(jnp.int32, (LANE, LANE), 0)
    b = lax.broadcasted_iota(jnp.int32, (LANE, LANE), 1)
    same = jnp.bitwise_and(jnp.right_shift(a, 5), 1) == jnp.bitwise_and(jnp.right_shift(b, 5), 1)
    return jnp.where(same, 1.0, 0.0).astype(BF16)


def _slot_mask(shape, hp):
    lane = lax.broadcasted_iota(jnp.int32, shape, 1)
    return jnp.bitwise_and(jnp.right_shift(lane, 5), 1) == hp


def _at_prep(z, cc, ss, wq, wk, *, name):
    L = z.shape[0]
    tm = _tile(L, 512)
    qcol = Z_HG // AT_W
    kvcol = (Z_HG + AT_W) // (2 * LANE)

    def body(zq_ref, zkv_ref, cc_ref, ss_ref, wq_ref, wk_ref, qm_ref, kr_ref, vb_ref):
        seg = _seg_matrix()
        cc, ss = cc_ref[...], ss_ref[...]

        def normrope(x, w):
            r = lax.rsqrt(_exact_right(x * x, seg) * (1.0 / AT_HD) + EPS)
            y = x * r * w
            return y * cc + pltpu.roll(y, 64, 1) * ss

        for g in range(AT_HEADS // 2):
            o = normrope(zq_ref[:, LANE * g:LANE * (g + 1)], wq_ref[...]) * (AT_HD ** -0.5)
            for hp in range(2):
                h = 2 * g + hp
                tgt = h // (AT_HEADS // AT_KV)
                xm = jnp.where(_slot_mask(o.shape, hp), o, 0.0)
                if tgt != hp:
                    xm = pltpu.roll(xm, 32 if tgt == 1 else 96, 1)
                qm_ref[h] = xm.astype(BF16)
        kr_ref[...] = normrope(zkv_ref[:, :LANE], wk_ref[...]).astype(BF16)
        vb_ref[...] = zkv_ref[:, LANE:].astype(BF16)

    tab = pl.BlockSpec((tm, LANE), lambda i: (i, 0))
    vec = pl.BlockSpec((1, LANE), lambda i: (0, 0))
    return pl.pallas_call(
        body, grid=(L // tm,),
        in_specs=[pl.BlockSpec((tm, AT_W), lambda i: (i, qcol)), pl.BlockSpec((tm, 2 * LANE), lambda i: (i, kvcol)),
                  tab, tab, vec, vec],
        out_specs=[pl.BlockSpec((AT_HEADS, tm, LANE), lambda i: (0, i, 0)), tab, tab],
        out_shape=[SDS((AT_HEADS, L, LANE), BF16), SDS((L, LANE), BF16), SDS((L, LANE), BF16)],
        compiler_params=_params(("parallel",)), name=name)(z, z, cc, ss, wq, wk)


def _at_prep_bwd(dqm, dk2, dv2, z, cc, ss, wq, wk, *, name):
    L = z.shape[0]
    tm = _tile(L, 512)
    qcol = Z_HG // AT_W
    kvcol = (Z_HG + AT_W) // (2 * LANE)

    def body(dqm_ref, dk2_ref, dv2_ref, zq_ref, zkv_ref, cc_ref, ss_ref, wq_ref, wk_ref, dz_ref, dwq_ref, dwk_ref):
        @pl.when(pl.program_id(0) == 0)
        def _():
            dwq_ref[...] = jnp.zeros_like(dwq_ref)
            dwk_ref[...] = jnp.zeros_like(dwk_ref)

        seg = _seg_matrix()
        cc, ss = cc_ref[...], ss_ref[...]

        def back(x, w, do):
            dy = do * cc + pltpu.roll(do * ss, 64, 1)
            r = lax.rsqrt(_exact_right(x * x, seg) * (1.0 / AT_HD) + EPS)
            xh = x * r
            dxh = dy * w
            dx = r * (dxh - xh * (_exact_right(dxh * xh, seg) * (1.0 / AT_HD)))
            return dx, jnp.sum(dy * xh, axis=0, keepdims=True)

        for g in range(AT_HEADS // 2):
            do = None
            for hp in range(2):
                h = 2 * g + hp
                tgt = h // (AT_HEADS // AT_KV)
                d = jnp.where(_slot_mask((tm, LANE), tgt), dqm_ref[h], 0.0)
                if tgt != hp:
                    d = pltpu.roll(d, 96 if tgt == 1 else 32, 1)
                do = d if do is None else do + d
            dx, dw = back(zq_ref[:, LANE * g:LANE * (g + 1)], wq_ref[...], do * (AT_HD ** -0.5))
            dz_ref[:, LANE * g:LANE * (g + 1)] = dx.astype(BF16)
            dwq_ref[...] += dw
        dx, dw = back(zkv_ref[:, :LANE], wk_ref[...], dk2_ref[0] + dk2_ref[1])
        dz_ref[:, AT_W:AT_W + LANE] = dx.astype(BF16)
        dwk_ref[...] += dw
        dz_ref[:, AT_W + LANE:] = (dv2_ref[0] + dv2_ref[1]).astype(BF16)

    tab = pl.BlockSpec((tm, LANE), lambda i: (i, 0))
    vec = pl.BlockSpec((1, LANE), lambda i: (0, 0))
    two = pl.BlockSpec((AT_KV, tm, LANE), lambda i: (0, i, 0))
    return pl.pallas_call(
        body, grid=(L // tm,),
        in_specs=[pl.BlockSpec((AT_HEADS, tm, LANE), lambda i: (0, i, 0)), two, two,
                  pl.BlockSpec((tm, AT_W), lambda i: (i, qcol)), pl.BlockSpec((tm, 2 * LANE), lambda i: (i, kvcol)),
                  tab, tab, vec, vec],
        out_specs=[pl.BlockSpec((tm, Z_AT), lambda i: (i, 0)), vec, vec],
        out_shape=[SDS((L, Z_AT), BF16), SDS((1, LANE), F32), SDS((1, LANE), F32)],
        compiler_params=_params(("arbitrary",)), name=name)(dqm, dk2, dv2, z, z, cc, ss, wq, wk)


def _at_fwd(qm, kr, vb, *, name):
    L = kr.shape[0]
    G = AT_HEADS // AT_KV
    tq = _tile(L, 384)
    tk = PAD
    nk = L // tk
    R = G * tq

    def body(q_ref, k_ref, v_ref, ob_ref, of_ref, lse_ref, m_scr, l_scr, acc_scr):
        i = pl.program_id(1)
        q = q_ref[...].reshape(R, LANE)
        m_scr[...] = jnp.full_like(m_scr, NEG)
        l_scr[...] = jnp.zeros_like(l_scr)
        acc_scr[...] = jnp.zeros_like(acc_scr)

        def chunk(c, masked):
            start = c * tk if isinstance(c, int) else pl.multiple_of(c * tk, tk)
            s = _nt(q, k_ref[pl.ds(start, tk), :])
            if masked:
                col = lax.broadcasted_iota(jnp.int32, s.shape, 1)
                s = jnp.where(col >= PAD - N_META, s, NEG)
            m_prev = m_scr[...]
            m_new = jnp.maximum(m_prev, jnp.max(s, axis=1, keepdims=True))
            p = jnp.exp(s - m_new)
            alpha = jnp.exp(m_prev - m_new)
            l_scr[...] = alpha * l_scr[...] + jnp.sum(p, axis=1, keepdims=True)
            acc_scr[...] = alpha * acc_scr[...] + _nn(p.astype(BF16), v_ref[pl.ds(start, tk), :])
            m_scr[...] = m_new

        chunk(0, True)

        def loop(c, carry):
            chunk(c, False)
            return carry

        lax.fori_loop(1, nk, loop, 0)
        l = l_scr[...]
        rowg = i * tq + lax.broadcasted_iota(jnp.int32, (tq, LANE), 0)
        live = rowg >= PAD - N_META
        o = acc_scr[...] / l
        for g in range(G):
            og = jnp.where(live, o[g * tq:(g + 1) * tq], 0.0)
            ob_ref[:, LANE * g:LANE * (g + 1)] = og.astype(BF16)
            of_ref[:, LANE * g:LANE * (g + 1)] = og
        lse_ref[...] = (m_scr[...] + jnp.log(l)).reshape(G, tq, 1)

    full = pl.BlockSpec((L, LANE), lambda j, i: (0, 0))
    ospec = pl.BlockSpec((tq, G * LANE), lambda j, i: (i, j))
    return pl.pallas_call(
        body, grid=(AT_KV, L // tq),
        in_specs=[pl.BlockSpec((G, tq, LANE), lambda j, i: (j, i, 0)), full, full],
        out_specs=[ospec, ospec, pl.BlockSpec((G, tq, 1), lambda j, i: (j, i, 0))],
        out_shape=[SDS((L, AT_HEADS * LANE), BF16), SDS((L, AT_HEADS * LANE), F32), SDS((AT_HEADS, L, 1), F32)],
        scratch_shapes=[pltpu.VMEM((R, 1), F32), pltpu.VMEM((R, 1), F32), pltpu.VMEM((R, LANE), F32)],
        compiler_params=_params(("parallel", "parallel")), name=name)(qm, kr, vb)


def _at_bwd(qm, kr, vb, do8, of8, lse, *, name):
    L = kr.shape[0]
    G = AT_HEADS // AT_KV
    tq = _tile(L, 256)
    tk = PAD
    nk = L // tk
    nq = L // tq
    R = G * tq

    def body(q_ref, k_hbm, v_hbm, do_ref, o_ref, lse_ref, dq_ref, dk_hbm, dv_hbm,
             k_scr, v_scr, dk_scr, dv_scr, dq_scr, sem):
        j, i = pl.program_id(0), pl.program_id(1)

        @pl.when(i == 0)
        def _():
            ck = pltpu.make_async_copy(k_hbm, k_scr, sem.at[0])
            cv = pltpu.make_async_copy(v_hbm, v_scr, sem.at[1])
            ck.start()
            cv.start()
            dk_scr[...] = jnp.zeros_like(dk_scr)
            dv_scr[...] = jnp.zeros_like(dv_scr)
            ck.wait()
            cv.wait()

        q = q_ref[...].reshape(R, LANE)
        rowg = i * tq + lax.broadcasted_iota(jnp.int32, (tq, LANE), 0)
        live = rowg >= PAD - N_META
        do = jnp.concatenate([jnp.where(live, do_ref[:, LANE * g:LANE * (g + 1)], 0.0) for g in range(G)], axis=0)
        o = jnp.concatenate([o_ref[:, LANE * g:LANE * (g + 1)] for g in range(G)], axis=0)
        delta = jnp.sum(do * o, axis=1, keepdims=True)
        dob = do.astype(BF16)
        lse_v = lse_ref[...].reshape(R, 1)
        dq_scr[...] = jnp.zeros_like(dq_scr)

        def chunk(c, masked):
            start = c * tk if isinstance(c, int) else pl.multiple_of(c * tk, tk)
            k = k_scr[pl.ds(start, tk), :]
            v = v_scr[pl.ds(start, tk), :]
            s = _nt(q, k)
            if masked:
                col = lax.broadcasted_iota(jnp.int32, s.shape, 1)
                s = jnp.where(col >= PAD - N_META, s, NEG)
            p = jnp.exp(s - lse_v)
            ds = (p * (_nt(dob, v) - delta)).astype(BF16)
            dq_scr[...] += _nn(ds, k)
            dk_scr[pl.ds(start, tk), :] += _tn(ds, q)
            dv_scr[pl.ds(start, tk), :] += _tn(p.astype(BF16), dob)

        chunk(0, True)

        def loop(c, carry):
            chunk(c, False)
            return carry

        lax.fori_loop(1, nk, loop, 0)
        dq_ref[...] = dq_scr[...].reshape(G, tq, LANE)

        @pl.when(i == nq - 1)
        def _():
            ck = pltpu.make_async_copy(dk_scr, dk_hbm.at[j], sem.at[0])
            cv = pltpu.make_async_copy(dv_scr, dv_hbm.at[j], sem.at[1])
            ck.start()
            cv.start()
            ck.wait()
            cv.wait()

    anyspec = pl.BlockSpec(memory_space=pl.ANY)
    ospec = pl.BlockSpec((tq, G * LANE), lambda j, i: (i, j))
    return pl.pallas_call(
        body, grid=(AT_KV, nq),
        in_specs=[pl.BlockSpec((G, tq, LANE), lambda j, i: (j, i, 0)), anyspec, anyspec, ospec, ospec,
                  pl.BlockSpec((G, tq, 1), lambda j, i: (j, i, 0))],
        out_specs=[pl.BlockSpec((G, tq, LANE), lambda j, i: (j, i, 0)), anyspec, anyspec],
        out_shape=[SDS((AT_HEADS, L, LANE), F32), SDS((AT_KV, L, LANE), F32), SDS((AT_KV, L, LANE), F32)],
        scratch_shapes=[pltpu.VMEM((L, LANE), BF16), pltpu.VMEM((L, LANE), BF16), pltpu.VMEM((L, LANE), F32),
                        pltpu.VMEM((L, LANE), F32), pltpu.VMEM((R, LANE), F32), pltpu.SemaphoreType.DMA((2,))],
        compiler_params=_params(("arbitrary", "arbitrary"), VMEM_LARGE), name=name)(qm, kr, vb, do8, of8, lse)


def _merge_fwd(ya, o8, wua, wubp, z, *, name):
    L = ya.shape[0]
    D = wua.shape[1]
    tm, tn = _tile(L, 512), 256
    ga, gb = (Z_HG + Z_AT) // tn, (Z_HG + Z_AT + D) // tn

    def body(ya_ref, o8_ref, wa_ref, wb_ref, za_ref, zb_ref, mix_ref):
        pa = _nn(ya_ref[...], wa_ref[...])
        pb = _nn(o8_ref[...], wb_ref[...])
        mix_ref[...] = (_sigmoid(za_ref[...]) * pa + _sigmoid(zb_ref[...]) * pb).astype(BF16)

    return pl.pallas_call(
        body, grid=(D // tn, L // tm),
        in_specs=[pl.BlockSpec((tm, ya.shape[1]), lambda j, i: (i, 0)), pl.BlockSpec((tm, o8.shape[1]), lambda j, i: (i, 0)),
                  pl.BlockSpec((wua.shape[0], tn), lambda j, i: (0, j)), pl.BlockSpec((wubp.shape[0], tn), lambda j, i: (0, j)),
                  pl.BlockSpec((tm, tn), lambda j, i: (i, ga + j)), pl.BlockSpec((tm, tn), lambda j, i: (i, gb + j))],
        out_specs=pl.BlockSpec((tm, tn), lambda j, i: (i, j)), out_shape=SDS((L, D), BF16),
        compiler_params=_params(("parallel", "parallel")), name=name)(ya, o8, wua, wubp, z, z)


def _merge_bwd(dh, wout, ya, o8, wua, wubp, z, *, name):
    L = ya.shape[0]
    D = wua.shape[1]
    tm, tn = _tile(L, 512), 256
    ga, gb = (Z_HG + Z_AT) // tn, (Z_HG + Z_AT + D) // tn

    def body(dh_ref, wo_ref, ya_ref, o8_ref, wa_ref, wb_ref, za_ref, zb_ref, dpa_ref, dpb_ref, dza_ref, dzb_ref):
        dm = _nt(dh_ref[...].astype(BF16), wo_ref[...])
        pa = _nn(ya_ref[...], wa_ref[...])
        pb = _nn(o8_ref[...], wb_ref[...])
        sa, sb = _sigmoid(za_ref[...]), _sigmoid(zb_ref[...])
        dpa_ref[...] = (dm * sa).astype(BF16)
        dpb_ref[...] = (dm * sb).astype(BF16)
        dza_ref[...] = (dm * pa * sa * (1.0 - sa)).astype(BF16)
        dzb_ref[...] = (dm * pb * sb * (1.0 - sb)).astype(BF16)

    ospec = pl.BlockSpec((tm, tn), lambda j, i: (i, j))
    return pl.pallas_call(
        body, grid=(D // tn, L // tm),
        in_specs=[pl.BlockSpec((tm, D), lambda j, i: (i, 0)), pl.BlockSpec((tn, D), lambda j, i: (j, 0)),
                  pl.BlockSpec((tm, ya.shape[1]), lambda j, i: (i, 0)), pl.BlockSpec((tm, o8.shape[1]), lambda j, i: (i, 0)),
                  pl.BlockSpec((wua.shape[0], tn), lambda j, i: (0, j)), pl.BlockSpec((wubp.shape[0], tn), lambda j, i: (0, j)),
                  pl.BlockSpec((tm, tn), lambda j, i: (i, ga + j)), pl.BlockSpec((tm, tn), lambda j, i: (i, gb + j))],
        out_specs=[ospec] * 4, out_shape=[SDS((L, D), BF16)] * 4,
        compiler_params=_params(("parallel", "parallel")), name=name)(dh, wout, ya, o8, wua, wubp, z, z)


def _loss_head(h, tgt, *, name):
    L, D = h.shape
    tm = PAD

    def body(h_ref, t_ref, dh_ref, ls_ref):
        i = pl.program_id(0)

        @pl.when(i == 0)
        def _():
            ls_ref[...] = jnp.zeros_like(ls_ref)
            dh_ref[...] = jnp.zeros_like(dh_ref)

        @pl.when(i > 0)
        def _():
            e = h_ref[...] - t_ref[...]
            dh_ref[...] = e * (1.0 / D)
            s = jnp.sum(e * e, axis=0, keepdims=True)
            tot = s[:, :LANE]
            for c in range(1, D // LANE):
                tot = tot + s[:, LANE * c:LANE * (c + 1)]
            ls_ref[...] += tot

    return pl.pallas_call(
        body, grid=(L // tm,),
        in_specs=[pl.BlockSpec((tm, D), lambda i: (i, 0)), pl.BlockSpec((tm, D), lambda i: (jnp.maximum(i - 1, 0), 0))],
        out_specs=[pl.BlockSpec((tm, D), lambda i: (i, 0)), pl.BlockSpec((1, LANE), lambda i: (0, 0))],
        out_shape=[SDS((L, D), F32), SDS((1, LANE), F32)],
        compiler_params=_params(("arbitrary",)), name=name)(h, tgt)


def _adamw(w, g, m, v, *, name):
    shape = w.shape
    w2, g2, m2, v2 = [a.reshape(-1, shape[-1]) for a in (w, g, m, v)]
    rows, cols = w2.shape
    tr = _tile(rows, 256, 8)

    def body(w_ref, g_ref, m_ref, v_ref, d_ref, nm_ref, nv_ref):
        g = g_ref[...]
        m = ADAM_B1 * m_ref[...] + (1.0 - ADAM_B1) * g
        v = ADAM_B2 * v_ref[...] + (1.0 - ADAM_B2) * (g * g)
        m_hat = m / (1.0 - ADAM_B1 ** ADAM_STEP)
        v_hat = v / (1.0 - ADAM_B2 ** ADAM_STEP)
        d_ref[...] = -ADAM_LR * (m_hat / (jnp.sqrt(v_hat) + ADAM_EPS) + ADAM_WD * w_ref[...])
        nm_ref[...] = m
        nv_ref[...] = v

    spec = pl.BlockSpec((tr, cols), lambda i: (i, 0))
    outs = pl.pallas_call(
        body, grid=(rows // tr,), in_specs=[spec] * 4, out_specs=[spec] * 3, out_shape=[SDS((rows, cols), F32)] * 3,
        compiler_params=_params(("parallel",)), name=name)(w2, g2, m2, v2)
    return [o.reshape(shape) for o in outs]


def _sum_slabs(x, *, name):
    n, R, _ = x.shape
    tr = _tile(R, 2048, 8)

    def body(x_ref, o_ref):
        tot = x_ref[0]
        for s in range(1, n):
            tot = tot + x_ref[s]
        o_ref[...] = tot

    return pl.pallas_call(
        body, grid=(R // tr,), in_specs=[pl.BlockSpec((n, tr, LANE), lambda i: (0, i, 0))],
        out_specs=pl.BlockSpec((tr, LANE), lambda i: (i, 0)), out_shape=SDS((R, LANE), F32),
        compiler_params=_params(("parallel",)), name=name)(x)


def _add_pair(a, b, *, name):
    n, R, _ = a.shape
    tr = _tile(R, 2048, 8)

    def body(a_ref, b_ref, o_ref):
        o_ref[...] = a_ref[...] + b_ref[...]

    spec = pl.BlockSpec((1, tr, LANE), lambda s, i: (s, i, 0))
    return pl.pallas_call(
        body, grid=(n, R // tr), in_specs=[spec, spec], out_specs=spec, out_shape=SDS(a.shape, F32),
        compiler_params=_params(("parallel", "parallel")), name=name)(a, b)


def _place():
    return lax.axis_index("x"), lax.axis_index("y"), lax.axis_index("c")


def _allgather_small(v, *, name):
    m_per, n = v.shape

    def body(x_ref, out_ref, send_sems, recv_sems, local_sem):
        x, y, c = _place()
        me, sibling = (x, y, c), (x, y, 1 - c)
        chips = [(1 - x, y), (x, 1 - y), (1 - x, 1 - y)]

        def rows(px, py, pc):
            return out_ref.at[pl.ds((4 * px + 2 * py + pc) * m_per, m_per), :]

        def copy(k, block, to, src=None):
            return pltpu.make_async_remote_copy(
                src_ref=rows(*block) if src is None else src, dst_ref=rows(*block),
                send_sem=send_sems.at[k], recv_sem=recv_sems.at[k], device_id=to, device_id_type=MESH)

        mine = pltpu.make_async_copy(x_ref, rows(*me), local_sem)
        mine.start()
        first = [copy(0, me, sibling, src=x_ref)]
        first += [copy(1 + j, me, (*chip, c), src=x_ref) for j, chip in enumerate(chips)]
        for cp in first:
            cp.start()
        passed = [copy(4 + j, (*chip, c), sibling) for j, chip in enumerate(chips)]
        for j, chip in enumerate(chips):
            copy(1 + j, (*chip, c), me).wait_recv()
            passed[j].start()
        copy(0, sibling, me).wait_recv()
        for j, chip in enumerate(chips):
            copy(4 + j, (*chip, 1 - c), me).wait_recv()
        for cp in first + passed:
            cp.wait_send()
        mine.wait()

    return pl.pallas_call(
        body, out_shape=SDS((8 * m_per, n), v.dtype),
        in_specs=[pl.BlockSpec(memory_space=pltpu.VMEM)], out_specs=pl.BlockSpec(memory_space=pltpu.VMEM),
        scratch_shapes=[pltpu.SemaphoreType.DMA((7,)), pltpu.SemaphoreType.DMA((7,)), pltpu.SemaphoreType.DMA],
        name=name)(v)


def _gather_weights(wp, *, name):
    _, R, _ = wp.shape

    def body(w_ref, out_ref, send_sems, recv_sems, local_sem):
        x, y, c = _place()
        sibling = (x, y, 1 - c)
        chips = [(1 - x, y), (x, 1 - y), (1 - x, 1 - y)]

        def slot(px, py, half):
            return out_ref.at[2 * px + py, half]

        def copy(k, src, dst, to):
            return pltpu.make_async_remote_copy(src_ref=src, dst_ref=dst, send_sem=send_sems.at[k],
                                                recv_sem=recv_sems.at[k], device_id=to, device_id_type=MESH)

        mine = pltpu.make_async_copy(w_ref, out_ref.at[2 * x + y], local_sem)
        mine.start()
        first = [copy(j, w_ref.at[c], slot(x, y, c), (*chip, c)) for j, chip in enumerate(chips)]
        for cp in first:
            cp.start()
        passed = [copy(3 + j, slot(*chip, c), slot(*chip, c), sibling) for j, chip in enumerate(chips)]
        for j, chip in enumerate(chips):
            copy(j, w_ref.at[c], slot(*chip, c), (*chip, c)).wait_recv()
            passed[j].start()
        for j, chip in enumerate(chips):
            copy(3 + j, slot(*chip, 1 - c), slot(*chip, 1 - c), sibling).wait_recv()
        for cp in first + passed:
            cp.wait_send()
        mine.wait()

    anyspec = pl.BlockSpec(memory_space=pl.ANY)
    return pl.pallas_call(
        body, out_shape=SDS((4, 2, R, LANE), wp.dtype), in_specs=[anyspec], out_specs=anyspec,
        scratch_shapes=[pltpu.SemaphoreType.DMA((6,)), pltpu.SemaphoreType.DMA((6,)), pltpu.SemaphoreType.DMA],
        name=name)(wp)


def _pair_exchange(g, *, name):
    _, n, R, _ = g.shape

    def body(g_ref, out_ref, send_sem, recv_sem):
        x, y, c = _place()
        cp = pltpu.make_async_remote_copy(src_ref=g_ref.at[1 - c], dst_ref=out_ref, send_sem=send_sem,
                                          recv_sem=recv_sem, device_id=(x, y, 1 - c), device_id_type=MESH)
        cp.start()
        cp.wait()

    anyspec = pl.BlockSpec(memory_space=pl.ANY)
    return pl.pallas_call(
        body, out_shape=SDS((n, R, LANE), g.dtype), in_specs=[anyspec], out_specs=anyspec,
        scratch_shapes=[pltpu.SemaphoreType.DMA, pltpu.SemaphoreType.DMA], name=name)(g)


def _chip_exchange(part, *, name):
    n, R, _ = part.shape

    def body(p_ref, out_ref, send_sems, recv_sems, local_sem):
        x, y, c = _place()
        s_me = 2 * x + y
        chips = [(1 - x, y), (x, 1 - y), (1 - x, 1 - y)]

        def copy(k, chip):
            return pltpu.make_async_remote_copy(
                src_ref=p_ref.at[2 * chip[0] + chip[1]], dst_ref=out_ref.at[s_me], send_sem=send_sems.at[k],
                recv_sem=recv_sems.at[k], device_id=(*chip, c), device_id_type=MESH)

        def landed(k, chip):
            return pltpu.make_async_remote_copy(
                src_ref=p_ref.at[s_me], dst_ref=out_ref.at[2 * chip[0] + chip[1]], send_sem=send_sems.at[k],
                recv_sem=recv_sems.at[k], device_id=(*chip, c), device_id_type=MESH)

        mine = pltpu.make_async_copy(p_ref.at[s_me], out_ref.at[s_me], local_sem)
        mine.start()
        sends = [copy(k, chip) for k, chip in enumerate(chips)]
        for cp in sends:
            cp.start()
        for k, chip in enumerate(chips):
            landed(k, chip).wait_recv()
        for cp in sends:
            cp.wait_send()
        mine.wait()

    anyspec = pl.BlockSpec(memory_space=pl.ANY)
    return pl.pallas_call(
        body, out_shape=SDS((n, R, LANE), part.dtype), in_specs=[anyspec], out_specs=anyspec,
        scratch_shapes=[pltpu.SemaphoreType.DMA((3,)), pltpu.SemaphoreType.DMA((3,)), pltpu.SemaphoreType.DMA],
        name=name)(part)


def _pair_share(red, *, name):
    R, _ = red.shape

    def body(r_ref, out_ref, send_sem, recv_sem, local_sem):
        x, y, c = _place()
        mine = pltpu.make_async_copy(r_ref, out_ref.at[c], local_sem)
        mine.start()
        cp = pltpu.make_async_remote_copy(src_ref=r_ref, dst_ref=out_ref.at[c], send_sem=send_sem,
                                          recv_sem=recv_sem, device_id=(x, y, 1 - c), device_id_type=MESH)
        cp.start()
        pltpu.make_async_remote_copy(src_ref=r_ref, dst_ref=out_ref.at[1 - c], send_sem=send_sem,
                                     recv_sem=recv_sem, device_id=(x, y, 1 - c), device_id_type=MESH).wait_recv()
        cp.wait_send()
        mine.wait()

    anyspec = pl.BlockSpec(memory_space=pl.ANY)
    return pl.pallas_call(
        body, out_shape=SDS((2, R, LANE), red.dtype), in_specs=[anyspec], out_specs=anyspec,
        scratch_shapes=[pltpu.SemaphoreType.DMA, pltpu.SemaphoreType.DMA, pltpu.SemaphoreType.DMA], name=name)(red)


def _finish_small(gathered, lbf, lbb, *, rows, name):
    r_lbf, r_lbb = rows['lb_f'], rows['lb_b']

    def body(g_ref, lbf_ref, lbb_ref, o_ref, dlf_ref, dlb_ref):
        tot = g_ref[0]
        for s in range(1, 8):
            tot = tot + g_ref[s]
        o_ref[...] = tot
        o_ref[0:1, :] = jnp.broadcast_to(jnp.sum(o_ref[0:1, :], axis=1, keepdims=True), (1, LANE))
        for lb_ref, d_ref, r0 in ((lbf_ref, dlf_ref, r_lbf), (lbb_ref, dlb_ref, r_lbb)):
            for hh in range(HG_HEADS):
                sl = slice(LANE * hh, LANE * (hh + 1))
                l0, l1 = lb_ref[0:1, sl], lb_ref[1:2, sl]
                mx = jnp.maximum(l0, l1)
                e0, e1 = jnp.exp(l0 - mx), jnp.exp(l1 - mx)
                p0 = e0 / (e0 + e1)
                d0 = o_ref[r0 + hh:r0 + hh + 1, :] * p0 * (1.0 - p0)
                d_ref[0:1, sl] = d0
                d_ref[1:2, sl] = -d0

    vm = pl.BlockSpec(memory_space=pltpu.VMEM)
    return pl.pallas_call(
        body, in_specs=[vm, vm, vm], out_specs=[vm, vm, vm],
        out_shape=[SDS(gathered.shape[1:], F32), SDS(lbf.shape, F32), SDS(lbb.shape, F32)], name=name)(gathered, lbf, lbb)


def _pad_len(n):
    q = 2 * 128 * LANE
    return (n + q - 1) // q * q


def _pack_local(shards, dtype):
    flat = jnp.concatenate([s.astype(dtype).reshape(-1) for s in shards])
    n = flat.shape[0]
    flat = jnp.pad(flat, (0, _pad_len(n) - n))
    return flat.reshape(2, -1, LANE)


def _unpack_full(gathered, shapes, names):
    out, off = {}, 0
    for name in names:
        r, cs = shapes[name]
        blk = gathered[:, off:off + r * cs].reshape(4, r, cs)
        off += r * cs
        out[name] = blk.reshape(4 * r, cs) if name in ROW_SHARDED else blk.transpose(1, 0, 2).reshape(r, 4 * cs)
    return out


def _pack_grads(grads, shapes, names):
    cols = []
    for name in names:
        r, cs = shapes[name]
        g = grads[name]
        blk = g.reshape(4, r * cs) if name in ROW_SHARDED else g.reshape(r, 4, cs).transpose(1, 0, 2).reshape(4, r * cs)
        cols.append(blk)
    flat = jnp.concatenate(cols, axis=1)
    n = flat.shape[1]
    flat = jnp.pad(flat, ((0, 0), (0, _pad_len(n) - n)))
    return flat.reshape(4, 2, -1, LANE).transpose(1, 0, 2, 3)


def _unpack_shard(flat, shapes, names):
    out, off = {}, 0
    for name in names:
        r, cs = shapes[name]
        out[name] = flat[off:off + r * cs].reshape(r, cs)
        off += r * cs
    return out


def _local_step(x2, tgt2, meta, W, S):
    T, D = x2.shape
    L = PAD + T
    h0 = jnp.concatenate([jnp.zeros((PAD - N_META, D), F32), meta, x2], axis=0)

    qk0 = Z_HG
    w_in = jnp.concatenate([W['w_in'][:, :qk0], _qk_to_group(W['w_in'][:, qk0:qk0 + AT_W + AT_KVW]),
                            W['w_in'][:, qk0 + AT_W + AT_KVW:]], axis=1)
    heads = np.arange(AT_HEADS)
    onehot = jnp.asarray((heads[:, None] // (AT_HEADS // AT_KV) == np.arange(AT_KV)[None, :]).astype(np.float32))
    wubp = (W['w_up_b'].reshape(AT_HEADS, 1, AT_HD, D) * onehot[:, :, None, None].astype(BF16)).reshape(AT_HEADS * LANE, D)
    cc, ss = _rope_tables(L)
    wq_g, wk_g = _group_vec(S['q_norm']), _group_vec(S['k_norm'])

    def ffn_fwd(h, nw, wg, wu, wd, tag):
        n = _rmsnorm_fwd(h, nw, name=tag + "_norm")
        g, u, a = _ffn_up(n, wg, wu, name=tag + "_up")
        hn = _mm([(a, wd)], res=h, alpha=0.5, tm=512, tn=D, tk=wd.shape[0], name=tag + "_down")
        return hn, (n, g, u, a)

    def ffn_bwd(dh, h, nw, wg, wu, wd, saved, tag):
        n, g, u, a = saved
        dg, du = _ffn_dact(dh, wd, g, u, name=tag + "_dact")
        dn = _mm([(dg, wg), (du, wu)], tb=True, tm=512, tn=D, tk=1408, name=tag + "_dn")
        dwg = _mm([(n, dg)], ta=True, tm=D, tn=1408, tk=512, name=tag + "_dwg")
        dwu = _mm([(n, du)], ta=True, tm=D, tn=1408, tk=512, name=tag + "_dwu")
        dwd = _mm([(a, dh)], ta=True, alpha=0.5, tm=1408, tn=D, tk=512, name=tag + "_dwd")
        dhp, dnw = _rmsnorm_bwd(h, nw, dn, dh, name=tag + "_norm_bwd")
        return dhp, dnw, dwg, dwu, dwd

    h1, sv1 = ffn_fwd(h0, S['ffn1_norm'], W['ffn1_w_gate'], W['ffn1_w_up'], W['ffn1_w_down'], "ffn1")
    um = _rmsnorm_fwd(h1, S['mix_norm'], name="mix_norm")
    z = _mm([(um, w_in)], tm=512, tn=1792, tk=D, name="in_proj")
    of, sf = _hg_fwd(z, S['hg_lb_fwd'], rev=False, name="hg_fwd_f")
    ob, sb = _hg_fwd(z, S['hg_lb_bwd'], rev=True, name="hg_fwd_b")
    ya = _hg_post_fwd(of, ob, z, S['hg_out_norm'], name="hg_post")
    qm, kr, vb = _at_prep(z, cc, ss, wq_g, wk_g, name="at_prep")
    o8, o8f, lse = _at_fwd(qm, kr, vb, name="at_fwd")
    mixed = _merge_fwd(ya, o8, W['w_up_a'], wubp, z, name="merge")
    h2 = _mm([(mixed, W['w_out'])], res=h1, tm=512, tn=D, tk=D, name="out_proj")
    h3, sv2 = ffn_fwd(h2, S['ffn2_norm'], W['ffn2_w_gate'], W['ffn2_w_up'], W['ffn2_w_down'], "ffn2")
    dh3, loss_lanes = _loss_head(h3, tgt2, name="loss_head")

    G = {}
    dh2, dn_ffn2, G['ffn2_w_gate'], G['ffn2_w_up'], G['ffn2_w_down'] = ffn_bwd(
        dh3, h2, S['ffn2_norm'], W['ffn2_w_gate'], W['ffn2_w_up'], W['ffn2_w_down'], sv2, "ffn2")
    dpa, dpb, dzga, dzgb = _merge_bwd(dh2, W['w_out'], ya, o8, W['w_up_a'], wubp, z, name="merge_bwd")
    G['w_out'] = _mm([(mixed, dh2)], ta=True, tm=D, tn=D, tk=512, name="d_w_out")
    dya = _mm([(dpa, W['w_up_a'])], tb=True, tm=512, tn=HG_W, tk=D, name="d_ya")
    do8 = _mm([(dpb, wubp)], tb=True, tm=512, tn=AT_HEADS * LANE, tk=D, name="d_o8")
    G['w_up_a'] = _mm([(ya, dpa)], ta=True, tm=HG_W, tn=D, tk=512, name="d_w_up_a")
    dwubp = _mm([(o8, dpb)], ta=True, tm=AT_HEADS * LANE, tn=D, tk=512, name="d_w_up_b")
    G['w_up_b'] = (dwubp.reshape(AT_HEADS, AT_KV, AT_HD, D) * onehot[:, :, None, None]).sum(axis=1).reshape(AT_W, D)
    do_hg, dzg, d_hgn = _hg_post_bwd(dya, of, ob, z, S['hg_out_norm'], name="hg_post_bwd")
    dq_f, dv_f, dzf_f, dlb_f = _hg_bwd(z, S['hg_lb_fwd'], do_hg, sf, None, rev=False, name="hg_bwd_f")
    dzq, dzi, dzf_b, dlb_b = _hg_bwd(z, S['hg_lb_bwd'], do_hg, sb, (dq_f, dv_f), rev=True, name="hg_bwd_b")
    dqm, dk2, dv2 = _at_bwd(qm, kr, vb, do8, o8f, lse, name="at_bwd")
    dz_at, dwq_g, dwk_g = _at_prep_bwd(dqm, dk2, dv2, z, cc, ss, wq_g, wk_g, name="at_prep_bwd")
    dz = jnp.concatenate([dzq, dzi, dzf_f, dzf_b, dzg, dz_at, dzga, dzgb], axis=1)
    dum = _mm([(dz, w_in)], tb=True, tm=512, tn=D, tk=1792, name="d_um")
    dw_in_p = _mm([(um, dz)], ta=True, tm=D, tn=1792, tk=512, name="d_w_in")
    G['w_in'] = jnp.concatenate([dw_in_p[:, :qk0], _qk_from_group(dw_in_p[:, qk0:qk0 + AT_W + AT_KVW]),
                                 dw_in_p[:, qk0 + AT_W + AT_KVW:]], axis=1)
    dh1, dn_mix = _rmsnorm_bwd(h1, S['mix_norm'], dum, dh2, name="mix_norm_bwd")
    dh0, dn_ffn1, G['ffn1_w_gate'], G['ffn1_w_up'], G['ffn1_w_down'] = ffn_bwd(
        dh1, h0, S['ffn1_norm'], W['ffn1_w_gate'], W['ffn1_w_up'], W['ffn1_w_down'], sv1, "ffn1")

    small_rows = [('loss', loss_lanes), ('ffn1_norm', dn_ffn1.reshape(-1, LANE)), ('mix_norm', dn_mix.reshape(-1, LANE)),
                  ('ffn2_norm', dn_ffn2.reshape(-1, LANE)), ('hg_out_norm', d_hgn.reshape(-1, LANE)),
                  ('lb_f', dlb_f.reshape(-1, LANE)), ('lb_b', dlb_b.reshape(-1, LANE)), ('q_norm', dwq_g), ('k_norm', dwk_g)]
    return dh0[PAD:], dh0[PAD - N_META:PAD], G, small_rows


def kernel(x, meta_tokens, ffn1_norm, ffn1_w_gate, ffn1_w_up, ffn1_w_down, mix_norm, w_in, hg_lb_fwd, hg_lb_bwd, hg_out_norm, q_norm, k_norm, w_up_a, w_up_b, w_out, ffn2_norm, ffn2_w_gate, ffn2_w_up, ffn2_w_down, loss_target, m_meta_tokens, m_ffn1_norm, m_ffn1_w_gate, m_ffn1_w_up, m_ffn1_w_down, m_mix_norm, m_w_in, m_hg_lb_fwd, m_hg_lb_bwd, m_hg_out_norm, m_q_norm, m_k_norm, m_w_up_a, m_w_up_b, m_w_out, m_ffn2_norm, m_ffn2_w_gate, m_ffn2_w_up, m_ffn2_w_down, v_meta_tokens, v_ffn1_norm, v_ffn1_w_gate, v_ffn1_w_up, v_ffn1_w_down, v_mix_norm, v_w_in, v_hg_lb_fwd, v_hg_lb_bwd, v_hg_out_norm, v_q_norm, v_k_norm, v_w_up_a, v_w_up_b, v_w_out, v_ffn2_norm, v_ffn2_w_gate, v_ffn2_w_up, v_ffn2_w_down):
    given = dict(locals())
    w = {n: given[n] for n in WEIGHTS}
    mom = {n: given["m_" + n] for n in WEIGHTS}
    var = {n: given["v_" + n] for n in WEIGHTS}
    c = lax.axis_index("c")
    D = x.shape[-1]

    shard2d = {n: w[n].reshape(w[n].shape[-2:]) for n in MATS}
    shapes = {n: shard2d[n].shape for n in MATS}
    shapes['meta_tokens'] = w['meta_tokens'].shape
    n_w = sum(r * cs for r, cs in (shapes[n] for n in MATS))
    gathered = _gather_weights(_pack_local([shard2d[n] for n in MATS], BF16), name="gather_weights")
    W = _unpack_full(gathered.reshape(4, -1)[:, :n_w], shapes, MATS)
    meta_rows = w['meta_tokens'].reshape(-1, LANE)
    mg = _allgather_small(meta_rows, name="gather_meta").reshape(4, 2, N_META, -1)[:, 0]
    meta = mg.transpose(1, 0, 2).reshape(N_META, D)
    S = {n: w[n] for n in SMALLS}

    grad_x, dmeta, G, small_rows = _local_step(x[0], loss_target[0], meta, W, S)
    G['meta_tokens'] = dmeta

    names = MATS + ('meta_tokens',)
    gp = _pack_grads(G, shapes, names)
    from_sibling = _pair_exchange(gp, name="rs_pair_exchange")
    part = _add_pair(lax.dynamic_index_in_dim(gp, c, 0, keepdims=False), from_sibling, name="rs_pair_sum")
    slabs = _chip_exchange(part, name="rs_chip_exchange")
    red = _sum_slabs(slabs, name="rs_chip_sum")
    both = _pair_share(red, name="rs_pair_share")
    grads = _unpack_shard(both.reshape(-1), shapes, names)
    grads = {n: grads[n].reshape(w[n].shape) for n in names}

    rows, off = {}, 0
    for nme, blk in small_rows:
        rows[nme] = off
        off += blk.shape[0]
    block = jnp.concatenate([blk for _, blk in small_rows], axis=0)
    n_rows = (off + 7) // 8 * 8
    block = jnp.pad(block, ((0, n_rows - off), (0, 0)))
    allsmall = _allgather_small(block, name="gather_small").reshape(8, n_rows, LANE)
    tot, d_lbf, d_lbb = _finish_small(allsmall, w['hg_lb_fwd'], w['hg_lb_bwd'], rows=rows, name="finish_small")
    loss = 0.5 * tot[0, 0] / D

    def small(nme, shape):
        r0 = rows[nme]
        return tot[r0:r0 + shape[-1] // LANE].reshape(shape)

    grads['ffn1_norm'] = small('ffn1_norm', w['ffn1_norm'].shape)
    grads['mix_norm'] = small('mix_norm', w['mix_norm'].shape)
    grads['ffn2_norm'] = small('ffn2_norm', w['ffn2_norm'].shape)
    grads['hg_out_norm'] = small('hg_out_norm', w['hg_out_norm'].shape)
    grads['hg_lb_fwd'] = d_lbf
    grads['hg_lb_bwd'] = d_lbb
    grads['q_norm'] = _ungroup_vec(tot[rows['q_norm']])
    grads['k_norm'] = _ungroup_vec(tot[rows['k_norm']])

    delta, new_m, new_v = {}, {}, {}
    for n in WEIGHTS:
        delta[n], new_m[n], new_v[n] = _adamw(w[n], grads[n], mom[n], var[n], name="adamw_" + n)
    return (loss, grad_x[None], *[grads[n] for n in WEIGHTS], *[delta[n] for n in WEIGHTS],
            *[new_m[n] for n in WEIGHTS], *[new_v[n] for n in WEIGHTS])
```

```python
import functools
import math

import numpy as np
import jax
import jax.numpy as jnp
from jax import lax
from jax.experimental import pallas as pl
from jax.experimental.pallas import tpu as pltpu

F32 = jnp.float32
BF16 = jnp.bfloat16
SDS = jax.ShapeDtypeStruct
MESH = pl.DeviceIdType.MESH

EPS = 1e-6
N_META = 16
PAD = 512
LANE = 128
CHUNK = 128
HG_HEADS = 4
HG_W = HG_HEADS * 128
AT_HEADS = 8
AT_KV = 2
AT_HD = 64
AT_W = AT_HEADS * AT_HD
AT_KVW = AT_KV * AT_HD
GRID_W = 64
ROPE_THETA = 10000.0
Z_HG = 5 * HG_W
Z_AT = AT_W + 2 * AT_KVW
ADAM_LR, ADAM_B1, ADAM_B2, ADAM_EPS, ADAM_WD, ADAM_STEP = 0.001, 0.9, 0.999, 1e-08, 0.01, 10
VMEM_DEFAULT = 48 * 1024 * 1024
VMEM_LARGE = 60 * 1024 * 1024
NEG = -1e30

MATS = ('ffn1_w_gate', 'ffn1_w_up', 'ffn1_w_down', 'w_in', 'w_up_a', 'w_up_b', 'w_out',
        'ffn2_w_gate', 'ffn2_w_up', 'ffn2_w_down')
ROW_SHARDED = ('ffn1_w_down', 'w_out', 'ffn2_w_down')
SMALLS = ('ffn1_norm', 'mix_norm', 'hg_lb_fwd', 'hg_lb_bwd', 'hg_out_norm', 'q_norm', 'k_norm', 'ffn2_norm')
WEIGHTS = ('meta_tokens', 'ffn1_norm', 'ffn1_w_gate', 'ffn1_w_up', 'ffn1_w_down', 'mix_norm', 'w_in', 'hg_lb_fwd',
           'hg_lb_bwd', 'hg_out_norm', 'q_norm', 'k_norm', 'w_up_a', 'w_up_b', 'w_out', 'ffn2_norm', 'ffn2_w_gate',
           'ffn2_w_up', 'ffn2_w_down')


def _params(sem=None, vmem=VMEM_DEFAULT):
    return pltpu.CompilerParams(dimension_semantics=sem, vmem_limit_bytes=vmem)


def _tile(n, pref, q=LANE):
    for d in range(min(pref, n), 0, -1):
        if n % d == 0 and d % q == 0:
            return d
    return n


def _sigmoid(x):
    return 1.0 / (1.0 + jnp.exp(-x))


def _dot(a, b, dims):
    return lax.dot_general(a, b, (dims, ((), ())), preferred_element_type=F32)


def _nn(a, b):
    return _dot(a, b, ((1,), (0,)))


def _nt(a, b):
    return _dot(a, b, ((1,), (1,)))


def _tn(a, b):
    return _dot(a, b, ((0,), (0,)))


def _split3(x):
    x1 = x.astype(BF16)
    r = x - x1.astype(F32)
    x2 = r.astype(BF16)
    x3 = (r - x2.astype(F32)).astype(BF16)
    return x1, x2, x3


def _exact_left(m01, x):
    x1, x2, x3 = _split3(x)
    return _nn(m01, x1) + _nn(m01, x2) + _nn(m01, x3)


def _exact_right(x, m01):
    x1, x2, x3 = _split3(x)
    return _nn(x1, m01) + _nn(x2, m01) + _nn(x3, m01)


def _mm(pairs, *, name, ta=False, tb=False, out_dtype=F32, tm=512, tn=1024, tk=1024, alpha=1.0, res=None):
    a0, b0 = pairs[0]
    M = a0.shape[1] if ta else a0.shape[0]
    K = a0.shape[0] if ta else a0.shape[1]
    N = b0.shape[0] if tb else b0.shape[1]
    tm, tn, tk = _tile(M, tm), _tile(N, tn), _tile(K, tk)
    nk = K // tk
    npair = len(pairs)
    dims = ((0 if ta else 1,), (1 if tb else 0,))

    def body(*refs):
        ab = refs[:2 * npair]
        pos = 2 * npair
        res_ref = None
        if res is not None:
            res_ref = refs[pos]
            pos += 1
        o_ref = refs[pos]

        def partial_sum():
            tot = None
            for p in range(npair):
                d = _dot(ab[2 * p][...].astype(BF16), ab[2 * p + 1][...].astype(BF16), dims)
                tot = d if tot is None else tot + d
            return tot

        def finish(acc):
            r = acc if alpha == 1.0 else acc * alpha
            if res_ref is not None:
                r = res_ref[...] + r
            o_ref[...] = r.astype(out_dtype)

        if nk == 1:
            finish(partial_sum())
        else:
            acc_ref = refs[pos + 1]
            k = pl.program_id(2)

            @pl.when(k == 0)
            def _():
                acc_ref[...] = jnp.zeros_like(acc_ref)

            acc_ref[...] += partial_sum()

            @pl.when(k == nk - 1)
            def _():
                finish(acc_ref[...])

    a_spec = pl.BlockSpec((tk, tm), lambda j, i, k: (k, i)) if ta else pl.BlockSpec((tm, tk), lambda j, i, k: (i, k))
    b_spec = pl.BlockSpec((tn, tk), lambda j, i, k: (j, k)) if tb else pl.BlockSpec((tk, tn), lambda j, i, k: (k, j))
    o_spec = pl.BlockSpec((tm, tn), lambda j, i, k: (i, j))
    in_specs, args = [], []
    for a, b in pairs:
        in_specs += [a_spec, b_spec]
        args += [a, b]
    if res is not None:
        in_specs.append(o_spec)
        args.append(res)
    return pl.pallas_call(
        body, grid=(N // tn, M // tm, nk), in_specs=in_specs, out_specs=o_spec,
        out_shape=SDS((M, N), out_dtype),
        scratch_shapes=[pltpu.VMEM((tm, tn), F32)] if nk > 1 else [],
        compiler_params=_params(("parallel", "parallel", "arbitrary")), name=name)(*args)


def _rmsnorm_fwd(h, w, *, name):
    L, D = h.shape
    tm = _tile(L, 512)

    def body(h_ref, w_ref, o_ref):
        x = h_ref[...]
        r = lax.rsqrt(jnp.mean(x * x, axis=-1, keepdims=True) + EPS)
        o_ref[...] = (x * r * w_ref[...]).astype(BF16)

    return pl.pallas_call(
        body, grid=(L // tm,),
        in_specs=[pl.BlockSpec((tm, D), lambda i: (i, 0)), pl.BlockSpec((1, D), lambda i: (0, 0))],
        out_specs=pl.BlockSpec((tm, D), lambda i: (i, 0)), out_shape=SDS((L, D), BF16),
        compiler_params=_params(("parallel",)), name=name)(h, w)


def _rmsnorm_bwd(h, w, dn, dres, *, name):
    L, D = h.shape
    tm = _tile(L, 512)

    def body(h_ref, w_ref, dn_ref, dres_ref, dh_ref, dw_ref):
        x = h_ref[...]
        r = lax.rsqrt(jnp.mean(x * x, axis=-1, keepdims=True) + EPS)
        xh = x * r
        dn = dn_ref[...]
        dxh = dn * w_ref[...]
        dh_ref[...] = dres_ref[...] + r * (dxh - xh * jnp.mean(dxh * xh, axis=-1, keepdims=True))

        @pl.when(pl.program_id(0) == 0)
        def _():
            dw_ref[...] = jnp.zeros_like(dw_ref)

        dw_ref[...] += jnp.sum(dn * xh, axis=0, keepdims=True)

    row = pl.BlockSpec((tm, D), lambda i: (i, 0))
    vec = pl.BlockSpec((1, D), lambda i: (0, 0))
    return pl.pallas_call(
        body, grid=(L // tm,), in_specs=[row, vec, row, row], out_specs=[row, vec],
        out_shape=[SDS((L, D), F32), SDS((1, D), F32)],
        compiler_params=_params(("arbitrary",)), name=name)(h, w, dn, dres)


def _ffn_up(n, wg, wu, *, name):
    L, D = n.shape
    Fd = wg.shape[1]
    tm, tn = _tile(L, 512), _tile(Fd, 1408)

    def body(n_ref, wg_ref, wu_ref, g_ref, u_ref, a_ref):
        x = n_ref[...]
        g = _nn(x, wg_ref[...])
        u = _nn(x, wu_ref[...])
        g_ref[...] = g
        u_ref[...] = u
        a_ref[...] = (g * _sigmoid(g) * u).astype(BF16)

    wspec = pl.BlockSpec((D, tn), lambda j, i: (0, j))
    ospec = pl.BlockSpec((tm, tn), lambda j, i: (i, j))
    return pl.pallas_call(
        body, grid=(Fd // tn, L // tm),
        in_specs=[pl.BlockSpec((tm, D), lambda j, i: (i, 0)), wspec, wspec],
        out_specs=[ospec, ospec, ospec],
        out_shape=[SDS((L, Fd), F32), SDS((L, Fd), F32), SDS((L, Fd), BF16)],
        compiler_params=_params(("parallel", "parallel")), name=name)(n, wg, wu)


def _ffn_dact(dh, wd, g, u, *, name):
    L, D = dh.shape
    Fd = wd.shape[0]
    tm, tn = _tile(L, 512), _tile(Fd, 1408)

    def body(dh_ref, wd_ref, g_ref, u_ref, dg_ref, du_ref):
        da = 0.5 * _nt(dh_ref[...].astype(BF16), wd_ref[...])
        g = g_ref[...]
        sg = _sigmoid(g)
        dg_ref[...] = (da * u_ref[...] * (sg * (1.0 + g * (1.0 - sg)))).astype(BF16)
        du_ref[...] = (da * (g * sg)).astype(BF16)

    ospec = pl.BlockSpec((tm, tn), lambda j, i: (i, j))
    return pl.pallas_call(
        body, grid=(Fd // tn, L // tm),
        in_specs=[pl.BlockSpec((tm, D), lambda j, i: (i, 0)), pl.BlockSpec((tn, D), lambda j, i: (j, 0)), ospec, ospec],
        out_specs=[ospec, ospec], out_shape=[SDS((L, Fd), BF16), SDS((L, Fd), BF16)],
        compiler_params=_params(("parallel", "parallel")), name=name)(dh, wd, g, u)


def _hg_masks(rev):
    t = lax.broadcasted_iota(jnp.int32, (CHUNK, CHUNK), 0)
    s = lax.broadcasted_iota(jnp.int32, (CHUNK, CHUNK), 1)
    causal = (s >= t) if rev else (s <= t)
    levels = []
    for sh in (6, 5, 4):
        same = jnp.right_shift(t, sh + 1) == jnp.right_shift(s, sh + 1)
        tq = jnp.bitwise_and(jnp.right_shift(t, sh), 1)
        sk = jnp.bitwise_and(jnp.right_shift(s, sh), 1)
        levels.append(same & (tq == (0 if rev else 1)) & (sk == (1 if rev else 0)))
    diag = (jnp.right_shift(t, 4) == jnp.right_shift(s, 4)) & causal
    return causal, levels, diag


def _hg_intra_factors(q, k, b, b_scr, rev):
    b_scr[...] = b
    row = lax.broadcasted_iota(jnp.int32, (CHUNK, LANE), 0)
    out = []
    for sh in (6, 5, 4):
        lb = 1 << sh
        pieces = []
        for p in range(0, CHUNK, 2 * lb):
            r = p + lb if rev else p + lb - 1
            pieces.append(jnp.broadcast_to(b_scr[pl.ds(r, 1), :], (2 * lb, LANE)))
        ref = pieces[0] if len(pieces) == 1 else jnp.concatenate(pieces, axis=0)
        qside = jnp.bitwise_and(jnp.right_shift(row, sh), 1) == (0 if rev else 1)
        eq = jnp.where(qside, jnp.exp(jnp.minimum(b - ref, 0.0)), 0.0)
        ek = jnp.where(qside, 0.0, jnp.exp(jnp.minimum(ref - b, 0.0)))
        out.append((eq, ek, (q * eq).astype(BF16), (k * ek).astype(BF16)))
    pieces = []
    for a in range(0, CHUNK, 16):
        r = a + (8 if rev else 7)
        pieces.append(jnp.broadcast_to(b_scr[pl.ds(r, 1), :], (16, LANE)))
    ref = jnp.concatenate(pieces, axis=0)
    eq = jnp.exp(jnp.minimum(b - ref, 80.0))
    ek = jnp.exp(jnp.minimum(ref - b, 80.0))
    out.append((eq, ek, (q * eq).astype(BF16), (k * ek).astype(BF16)))
    return out


def _hg_gate(zf, l0, l1, valid):
    mx = jnp.maximum(l0, l1)
    e0, e1 = jnp.exp(l0 - mx), jnp.exp(l1 - mx)
    p0 = e0 / (e0 + e1)
    sg = _sigmoid(-zf)
    k = jnp.where(valid, (1.0 - p0) * sg, 0.0)
    return p0, sg, k, jnp.log(1.0 - k)


def _hg_fwd(z, lbp, *, rev, name):
    L = z.shape[0]
    nc = L // CHUNK
    fcol = 3 if rev else 2

    def cidx(j):
        return nc - 1 - j if rev else j

    def body(zq_ref, zi_ref, zf_ref, lb_ref, o_ref, ssave_ref, st_scr, b_scr):
        j = pl.program_id(0)

        @pl.when(j == 0)
        def _():
            st_scr[...] = jnp.zeros_like(st_scr)

        causal, lmasks, dmask = _hg_masks(rev)
        tri = jnp.where(causal, 1.0, 0.0).astype(BF16)
        rowg = cidx(j) * CHUNK + lax.broadcasted_iota(jnp.int32, (CHUNK, LANE), 0)
        valid = rowg >= PAD - N_META
        last = 0 if rev else CHUNK - 1
        for hh in range(HG_HEADS):
            sl = slice(LANE * hh, LANE * (hh + 1))
            zq = zq_ref[:, sl]
            q = zq * _sigmoid(zq)
            v = zi_ref[:, sl].astype(BF16)
            _, _, k, g = _hg_gate(zf_ref[:, sl], lb_ref[0:1, sl], lb_ref[1:2, sl], valid)
            b = _exact_left(tri, g)
            st = st_scr[hh]
            ssave_ref[0, hh] = st
            o = _nt((q * jnp.exp(b)).astype(BF16), st.astype(BF16))
            a = None
            fac = _hg_intra_factors(q, k, b, b_scr, rev)
            for (eq, ek, qq, kk), msk in zip(fac, lmasks + [dmask]):
                t = jnp.where(msk, _nt(qq, kk), 0.0)
                a = t if a is None else a + t
            o_ref[:, sl] = o + _nn(a.astype(BF16), v)
            bl = b_scr[pl.ds(last, 1), :]
            kd = (k * jnp.exp(bl - b)).astype(BF16)
            st_scr[hh] = st * jnp.exp(bl) + _tn(v, kd)

    zspec = lambda col: pl.BlockSpec((CHUNK, HG_W), lambda j: (cidx(j), col))
    return pl.pallas_call(
        body, grid=(nc,),
        in_specs=[zspec(0), zspec(1), zspec(fcol), pl.BlockSpec((2, HG_W), lambda j: (0, 0))],
        out_specs=[pl.BlockSpec((CHUNK, HG_W), lambda j: (cidx(j), 0)),
                   pl.BlockSpec((1, HG_HEADS, LANE, LANE), lambda j: (cidx(j), 0, 0, 0))],
        out_shape=[SDS((L, HG_W), F32), SDS((nc, HG_HEADS, LANE, LANE), F32)],
        scratch_shapes=[pltpu.VMEM((HG_HEADS, LANE, LANE), F32), pltpu.VMEM((CHUNK, LANE), F32)],
        compiler_params=_params(("arbitrary",)), name=name)(z, z, z, lbp)


def _hg_bwd(z, lbp, do, ssave, prev, *, rev, name):
    L = z.shape[0]
    nc = L // CHUNK
    fcol = 3 if rev else 2
    final = prev is not None

    def cidx(j):
        return j if rev else nc - 1 - j

    def body(*refs):
        zq_ref, zi_ref, zf_ref, lb_ref, do_ref, ss_ref = refs[:6]
        pos = 6
        if final:
            dqin_ref, dvin_ref = refs[6:8]
            pos = 8
        dq_ref, dv_ref, dzf_ref, dlb_ref, dst_scr, b_scr = refs[pos:pos + 6]
        j = pl.program_id(0)

        @pl.when(j == 0)
        def _():
            dst_scr[...] = jnp.zeros_like(dst_scr)
            dlb_ref[...] = jnp.zeros_like(dlb_ref)

        causal, lmasks, dmask = _hg_masks(rev)
        tri = jnp.where(causal, 1.0, 0.0).astype(BF16)
        ti = lax.broadcasted_iota(jnp.int32, (CHUNK, CHUNK), 0)
        si = lax.broadcasted_iota(jnp.int32, (CHUNK, CHUNK), 1)
        tri_t = jnp.where((si <= ti) if rev else (si >= ti), 1.0, 0.0).astype(BF16)
        rowg = cidx(j) * CHUNK + lax.broadcasted_iota(jnp.int32, (CHUNK, LANE), 0)
        valid = rowg >= PAD - N_META
        last = 0 if rev else CHUNK - 1
        for hh in range(HG_HEADS):
            sl = slice(LANE * hh, LANE * (hh + 1))
            zq = zq_ref[:, sl]
            sq = _sigmoid(zq)
            q = zq * sq
            v = zi_ref[:, sl].astype(BF16)
            p0, sg, k, g = _hg_gate(zf_ref[:, sl], lb_ref[0:1, sl], lb_ref[1:2, sl], valid)
            b = _exact_left(tri, g)
            dob = do_ref[:, sl].astype(BF16)
            st = ss_ref[0, hh]
            dst = dst_scr[hh]
            stb, dstb = st.astype(BF16), dst.astype(BF16)
            eb = jnp.exp(b)
            qe = (q * eb).astype(BF16)
            fac = _hg_intra_factors(q, k, b, b_scr, rev)
            bl = b_scr[pl.ds(last, 1), :]
            ebl = jnp.exp(bl)
            kde = jnp.exp(bl - b)
            kd = (k * kde).astype(BF16)
            da = jnp.where(causal, _nt(dob, v), 0.0)
            dq = eb * _nn(dob, stb)
            dk_inter = kde * _nn(v, dstb)
            dk = dk_inter
            dv = _nt(kd, dstb)
            a = None
            db = q * dq - k * dk
            for (eq, ek, qq, kk), msk in zip(fac, lmasks + [dmask]):
                t = jnp.where(msk, _nt(qq, kk), 0.0)
                a = t if a is None else a + t
                dal = jnp.where(msk, da, 0.0).astype(BF16)
                mq = _nn(dal, kk)
                mk = _tn(dal, qq)
                dq = dq + eq * mq
                dk = dk + ek * mk
                db = db + (qq.astype(F32) * mq - kk.astype(F32) * mk)
            dv = dv + _tn(a.astype(BF16), dob)
            extra = ebl * jnp.sum(st * dst, axis=0, keepdims=True) + jnp.sum(k * dk_inter, axis=0, keepdims=True)
            dst_scr[hh] = dst * ebl + _tn(dob, qe)
            dg = _exact_left(tri_t, db) + extra
            dk_tot = dk - dg / (1.0 - k)
            dzf_ref[:, sl] = jnp.where(valid, dk_tot * (1.0 - p0) * (-sg * (1.0 - sg)), 0.0).astype(BF16)
            dlb_ref[:, sl] += jnp.sum(jnp.where(valid, -sg * dk_tot, 0.0), axis=0, keepdims=True)
            if final:
                dq_ref[:, sl] = ((dq + dqin_ref[:, sl]) * (sq * (1.0 + zq * (1.0 - sq)))).astype(BF16)
                dv_ref[:, sl] = (dv + dvin_ref[:, sl]).astype(BF16)
            else:
                dq_ref[:, sl] = dq
                dv_ref[:, sl] = dv

    zspec = lambda col: pl.BlockSpec((CHUNK, HG_W), lambda j: (cidx(j), col))
    rspec = pl.BlockSpec((CHUNK, HG_W), lambda j: (cidx(j), 0))
    in_specs = [zspec(0), zspec(1), zspec(fcol), pl.BlockSpec((2, HG_W), lambda j: (0, 0)), rspec,
                pl.BlockSpec((1, HG_HEADS, LANE, LANE), lambda j: (cidx(j), 0, 0, 0))]
    args = [z, z, z, lbp, do, ssave]
    if final:
        in_specs += [rspec, rspec]
        args += list(prev)
    odt = BF16 if final else F32
    return pl.pallas_call(
        body, grid=(nc,), in_specs=in_specs,
        out_specs=[rspec, rspec, rspec, pl.BlockSpec((1, HG_W), lambda j: (0, 0))],
        out_shape=[SDS((L, HG_W), odt), SDS((L, HG_W), odt), SDS((L, HG_W), BF16), SDS((1, HG_W), F32)],
        scratch_shapes=[pltpu.VMEM((HG_HEADS, LANE, LANE), F32), pltpu.VMEM((CHUNK, LANE), F32)],
        compiler_params=_params(("arbitrary",)), name=name)(*args)


def _hg_post_fwd(of, ob, z, w, *, name):
    L = of.shape[0]
    tm = _tile(L, 512)

    def body(of_ref, ob_ref, zg_ref, w_ref, y_ref):
        for hh in range(HG_HEADS):
            sl = slice(LANE * hh, LANE * (hh + 1))
            o = of_ref[:, sl] + ob_ref[:, sl]
            r = lax.rsqrt(jnp.mean(o * o, axis=-1, keepdims=True) + EPS)
            zg = zg_ref[:, sl]
            y_ref[:, sl] = (o * r * w_ref[:, sl] * (zg * _sigmoid(zg))).astype(BF16)

    row = pl.BlockSpec((tm, HG_W), lambda i: (i, 0))
    return pl.pallas_call(
        body, grid=(L // tm,),
        in_specs=[row, row, pl.BlockSpec((tm, HG_W), lambda i: (i, 4)), pl.BlockSpec((1, HG_W), lambda i: (0, 0))],
        out_specs=row, out_shape=SDS((L, HG_W), BF16),
        compiler_params=_params(("parallel",)), name=name)(of, ob, z, w)


def _hg_post_bwd(dy, of, ob, z, w, *, name):
    L = of.shape[0]
    tm = _tile(L, 512)

    def body(dy_ref, of_ref, ob_ref, zg_ref, w_ref, do_ref, dzg_ref, dw_ref):
        @pl.when(pl.program_id(0) == 0)
        def _():
            dw_ref[...] = jnp.zeros_like(dw_ref)

        for hh in range(HG_HEADS):
            sl = slice(LANE * hh, LANE * (hh + 1))
            o = of_ref[:, sl] + ob_ref[:, sl]
            r = lax.rsqrt(jnp.mean(o * o, axis=-1, keepdims=True) + EPS)
            xh = o * r
            zg = zg_ref[:, sl]
            sg = _sigmoid(zg)
            w = w_ref[:, sl]
            dy = dy_ref[:, sl]
            dys = dy * (zg * sg)
            dzg_ref[:, sl] = (dy * xh * w * (sg * (1.0 + zg * (1.0 - sg)))).astype(BF16)
            dw_ref[:, sl] += jnp.sum(dys * xh, axis=0, keepdims=True)
            dxh = dys * w
            do_ref[:, sl] = r * (dxh - xh * jnp.mean(dxh * xh, axis=-1, keepdims=True))

    row = pl.BlockSpec((tm, HG_W), lambda i: (i, 0))
    vec = pl.BlockSpec((1, HG_W), lambda i: (0, 0))
    return pl.pallas_call(
        body, grid=(L // tm,),
        in_specs=[row, row, row, pl.BlockSpec((tm, HG_W), lambda i: (i, 4)), vec],
        out_specs=[row, row, vec],
        out_shape=[SDS((L, HG_W), F32), SDS((L, HG_W), BF16), SDS((1, HG_W), F32)],
        compiler_params=_params(("arbitrary",)), name=name)(dy, of, ob, z, w)


N_GROUPS = (AT_HEADS + AT_KV) // 2


def _qk_to_group(wqk):
    d = wqk.shape[0]
    return wqk.reshape(d, N_GROUPS, 2, AT_HD // 2, 2).transpose(0, 1, 4, 2, 3).reshape(d, N_GROUPS * LANE)


def _qk_from_group(wqk):
    d = wqk.shape[0]
    return wqk.reshape(d, N_GROUPS, 2, 2, AT_HD // 2).transpose(0, 1, 3, 4, 2).reshape(d, N_GROUPS * LANE)


def _group_vec(w64):
    halves = w64.reshape(AT_HD // 2, 2).T
    return jnp.broadcast_to(halves[:, None, :], (2, 2, AT_HD // 2)).reshape(1, LANE)


def _ungroup_vec(w128):
    w = w128.reshape(2, 2, 32).sum(axis=1)
    return w.T.reshape(1, AT_HD)


def _rope_tables(L):
    n_real = L - PAD
    t = np.arange(n_real)
    row = np.concatenate([np.zeros(PAD), t // GRID_W]).astype(np.float32)
    col = np.concatenate([np.zeros(PAD), t % GRID_W]).astype(np.float32)
    inv = jnp.asarray(ROPE_THETA, F32) ** (-jnp.arange(0, AT_HD // 2, 2, dtype=F32) / (AT_HD // 2))
    ang = jnp.concatenate([jnp.asarray(row)[:, None] * inv, jnp.asarray(col)[:, None] * inv], axis=-1)
    cos, sin = jnp.cos(ang), jnp.sin(ang)
    cc = jnp.tile(cos, (1, 4))
    ss = jnp.concatenate([-sin, -sin, sin, sin], axis=1)
    return cc, ss


def _seg_matrix():
    a = lax.broadcasted_iota(jnp.int32, (LANE, LANE), 0)
    b = lax.broadcasted_iota(jnp.int32, (LANE, LANE), 1)
    same = jnp.bitwise_and(jnp.right_shift(a, 5), 1) == jnp.bitwise_and(jnp.right_shift(b, 5), 1)
    return jnp.where(same, 1.0, 0.0).astype(BF16)


def _slot_mask(shape, hp):
    lane = lax.broadcasted_iota(jnp.int32, shape, 1)
    return jnp.bitwise_and(jnp.right_shift(lane, 5), 1) == hp


def _at_prep(z, cc, ss, wq, wk, *, name):
    L = z.shape[0]
    tm = _tile(L, 512)
    qcol = Z_HG // AT_W
    kvcol = (Z_HG + AT_W) // (2 * LANE)

    def body(zq_ref, zkv_ref, cc_ref, ss_ref, wq_ref, wk_ref, qm_ref, kr_ref, vb_ref):
        seg = _seg_matrix()
        cc, ss = cc_ref[...], ss_ref[...]

        def normrope(x, w):
            r = lax.rsqrt(_exact_right(x * x, seg) * (1.0 / AT_HD) + EPS)
            y = x * r * w
            return y * cc + pltpu.roll(y, 64, 1) * ss

        for g in range(AT_HEADS // 2):
            o = normrope(zq_ref[:, LANE * g:LANE * (g + 1)], wq_ref[...]) * (AT_HD ** -0.5)
            for hp in range(2):
                h = 2 * g + hp
                tgt = h // (AT_HEADS // AT_KV)
                xm = jnp.where(_slot_mask(o.shape, hp), o, 0.0)
                if tgt != hp:
                    xm = pltpu.roll(xm, 32 if tgt == 1 else 96, 1)
                qm_ref[h] = xm.astype(BF16)
        kr_ref[...] = normrope(zkv_ref[:, :LANE], wk_ref[...]).astype(BF16)
        v = zkv_ref[:, LANE:]
        low = lax.broadcasted_iota(jnp.int32, v.shape, 1) < AT_HD
        vb_ref[0] = jnp.where(low, v, 1.0).astype(BF16)
        vb_ref[1] = jnp.where(low, pltpu.roll(v, AT_HD, 1), 1.0).astype(BF16)

    tab = pl.BlockSpec((tm, LANE), lambda i: (i, 0))
    vec = pl.BlockSpec((1, LANE), lambda i: (0, 0))
    return pl.pallas_call(
        body, grid=(L // tm,),
        in_specs=[pl.BlockSpec((tm, AT_W), lambda i: (i, qcol)), pl.BlockSpec((tm, 2 * LANE), lambda i: (i, kvcol)),
                  tab, tab, vec, vec],
        out_specs=[pl.BlockSpec((AT_HEADS, tm, LANE), lambda i: (0, i, 0)), tab,
                   pl.BlockSpec((AT_KV, tm, LANE), lambda i: (0, i, 0))],
        out_shape=[SDS((AT_HEADS, L, LANE), BF16), SDS((L, LANE), BF16), SDS((AT_KV, L, LANE), BF16)],
        compiler_params=_params(("parallel",)), name=name)(z, z, cc, ss, wq, wk)


def _at_prep_bwd(dqm, dk2, dv2, z, cc, ss, wq, wk, *, name):
    L = z.shape[0]
    tm = _tile(L, 512)
    qcol = Z_HG // AT_W
    kvcol = (Z_HG + AT_W) // (2 * LANE)

    def body(dqm_ref, dk2_ref, dv2_ref, zq_ref, zkv_ref, cc_ref, ss_ref, wq_ref, wk_ref, dz_ref, dwq_ref, dwk_ref):
        @pl.when(pl.program_id(0) == 0)
        def _():
            dwq_ref[...] = jnp.zeros_like(dwq_ref)
            dwk_ref[...] = jnp.zeros_like(dwk_ref)

        seg = _seg_matrix()
        cc, ss = cc_ref[...], ss_ref[...]

        def back(x, w, do):
            dy = do * cc + pltpu.roll(do * ss, 64, 1)
            r = lax.rsqrt(_exact_right(x * x, seg) * (1.0 / AT_HD) + EPS)
            xh = x * r
            dxh = dy * w
            dx = r * (dxh - xh * (_exact_right(dxh * xh, seg) * (1.0 / AT_HD)))
            return dx, jnp.sum(dy * xh, axis=0, keepdims=True)

        for g in range(AT_HEADS // 2):
            do = None
            for hp in range(2):
                h = 2 * g + hp
                tgt = h // (AT_HEADS // AT_KV)
                d = jnp.where(_slot_mask((tm, LANE), tgt), dqm_ref[h], 0.0)
                if tgt != hp:
                    d = pltpu.roll(d, 96 if tgt == 1 else 32, 1)
                do = d if do is None else do + d
            dx, dw = back(zq_ref[:, LANE * g:LANE * (g + 1)], wq_ref[...], do * (AT_HD ** -0.5))
            dz_ref[:, LANE * g:LANE * (g + 1)] = dx.astype(BF16)
            dwq_ref[...] += dw
        dx, dw = back(zkv_ref[:, :LANE], wk_ref[...], dk2_ref[0] + dk2_ref[1])
        dz_ref[:, AT_W:AT_W + LANE] = dx.astype(BF16)
        dwk_ref[...] += dw
        dv0 = dv2_ref[0]
        low = lax.broadcasted_iota(jnp.int32, dv0.shape, 1) < AT_HD
        dz_ref[:, AT_W + LANE:] = jnp.where(low, dv0, pltpu.roll(dv2_ref[1], AT_HD, 1)).astype(BF16)

    tab = pl.BlockSpec((tm, LANE), lambda i: (i, 0))
    vec = pl.BlockSpec((1, LANE), lambda i: (0, 0))
    two = pl.BlockSpec((AT_KV, tm, LANE), lambda i: (0, i, 0))
    return pl.pallas_call(
        body, grid=(L // tm,),
        in_specs=[pl.BlockSpec((AT_HEADS, tm, LANE), lambda i: (0, i, 0)), two, two,
                  pl.BlockSpec((tm, AT_W), lambda i: (i, qcol)), pl.BlockSpec((tm, 2 * LANE), lambda i: (i, kvcol)),
                  tab, tab, vec, vec],
        out_specs=[pl.BlockSpec((tm, Z_AT), lambda i: (i, 0)), vec, vec],
        out_shape=[SDS((L, Z_AT), BF16), SDS((1, LANE), F32), SDS((1, LANE), F32)],
        compiler_params=_params(("arbitrary",)), name=name)(dqm, dk2, dv2, z, z, cc, ss, wq, wk)


def _at_fwd(qm, kr, vb, *, name):
    L = kr.shape[0]
    G = AT_HEADS // AT_KV
    tq = _tile(L, 384)
    tk = PAD
    nk = L // tk
    R = G * tq

    def body(q_ref, k_ref, v_ref, ob_ref, of_ref, lse_ref, m_scr, acc_scr):
        i = pl.program_id(1)
        q = q_ref[...].reshape(R, LANE)
        m_scr[...] = jnp.full_like(m_scr, NEG)
        acc_scr[...] = jnp.zeros_like(acc_scr)

        def chunk(c, masked):
            start = c * tk if isinstance(c, int) else pl.multiple_of(c * tk, tk)
            st = _nt(k_ref[pl.ds(start, tk), :], q)
            if masked:
                key = lax.broadcasted_iota(jnp.int32, st.shape, 0)
                st = jnp.where(key >= PAD - N_META, st, NEG)
            m_prev = m_scr[...]
            m_new = jnp.maximum(m_prev, jnp.max(st, axis=0, keepdims=True))
            pt = jnp.exp(st - m_new).astype(BF16)
            acc_scr[...] = jnp.exp(m_prev - m_new) * acc_scr[...] + _tn(v_ref[0, pl.ds(start, tk), :], pt)
            m_scr[...] = m_new

        chunk(0, True)

        def loop(c, carry):
            chunk(c, False)
            return carry

        lax.fori_loop(1, nk, loop, 0)
        l = acc_scr[pl.ds(AT_HD, 1), :]
        lse = m_scr[...] + jnp.log(l)
        o = (acc_scr[...] / l).T
        rowg = i * tq + lax.broadcasted_iota(jnp.int32, (tq, LANE), 0)
        live = rowg >= PAD - N_META
        for g in range(G):
            og = jnp.where(live, o[g * tq:(g + 1) * tq], 0.0)
            ob_ref[:, LANE * g:LANE * (g + 1)] = og.astype(BF16)
            of_ref[:, LANE * g:LANE * (g + 1)] = og
            lse_ref[g] = lse[:, g * tq:(g + 1) * tq]

    ospec = pl.BlockSpec((tq, G * LANE), lambda j, i: (i, j))
    return pl.pallas_call(
        body, grid=(AT_KV, L // tq),
        in_specs=[pl.BlockSpec((G, tq, LANE), lambda j, i: (j, i, 0)), pl.BlockSpec((L, LANE), lambda j, i: (0, 0)),
                  pl.BlockSpec((1, L, LANE), lambda j, i: (j, 0, 0))],
        out_specs=[ospec, ospec, pl.BlockSpec((G, 1, tq), lambda j, i: (j, 0, i))],
        out_shape=[SDS((L, AT_HEADS * LANE), BF16), SDS((L, AT_HEADS * LANE), F32), SDS((AT_HEADS, 1, L), F32)],
        scratch_shapes=[pltpu.VMEM((1, R), F32), pltpu.VMEM((LANE, R), F32)],
        compiler_params=_params(("parallel", "parallel")), name=name)(qm, kr, vb)


def _at_bwd(qm, kr, vb, do8, of8, lse, *, name):
    L = kr.shape[0]
    G = AT_HEADS // AT_KV
    tq = _tile(L, 256)
    tk = PAD
    nk = L // tk
    nq = L // tq
    R = G * tq

    def body(q_ref, k_hbm, v_hbm, do_ref, o_ref, lse_ref, dq_ref, dk_hbm, dv_hbm,
             k_scr, v_scr, dk_scr, dv_scr, dq_scr, sem):
        j, i = pl.program_id(0), pl.program_id(1)

        @pl.when(i == 0)
        def _():
            ck = pltpu.make_async_copy(k_hbm, k_scr, sem.at[0])
            cv = pltpu.make_async_copy(v_hbm.at[j], v_scr, sem.at[1])
            ck.start()
            cv.start()
            dk_scr[...] = jnp.zeros_like(dk_scr)
            dv_scr[...] = jnp.zeros_like(dv_scr)
            ck.wait()
            cv.wait()

        q = q_ref[...].reshape(R, LANE)
        rowg = i * tq + lax.broadcasted_iota(jnp.int32, (tq, LANE), 0)
        live = rowg >= PAD - N_META
        do = jnp.concatenate([jnp.where(live, do_ref[:, LANE * g:LANE * (g + 1)], 0.0) for g in range(G)], axis=0)
        o = jnp.concatenate([o_ref[:, LANE * g:LANE * (g + 1)] for g in range(G)], axis=0)
        delta = jnp.sum((do * o).T, axis=0, keepdims=True)
        dob = do.astype(BF16)
        lse_v = jnp.concatenate([lse_ref[g] for g in range(G)], axis=1)
        dq_scr[...] = jnp.zeros_like(dq_scr)

        def chunk(c, masked):
            start = c * tk if isinstance(c, int) else pl.multiple_of(c * tk, tk)
            k = k_scr[pl.ds(start, tk), :]
            v = v_scr[pl.ds(start, tk), :]
            st = _nt(k, q)
            if masked:
                key = lax.broadcasted_iota(jnp.int32, st.shape, 0)
                st = jnp.where(key >= PAD - N_META, st, NEG)
            pt = jnp.exp(st - lse_v)
            dst = (pt * (_nt(v, dob) - delta)).astype(BF16)
            dq_scr[...] += _tn(k, dst)
            dk_scr[pl.ds(start, tk), :] += _nn(dst, q)
            dv_scr[pl.ds(start, tk), :] += _nn(pt.astype(BF16), dob)

        chunk(0, True)

        def loop(c, carry):
            chunk(c, False)
            return carry

        lax.fori_loop(1, nk, loop, 0)
        dq_ref[...] = dq_scr[...].T.reshape(G, tq, LANE)

        @pl.when(i == nq - 1)
        def _():
            ck = pltpu.make_async_copy(dk_scr, dk_hbm.at[j], sem.at[0])
            cv = pltpu.make_async_copy(dv_scr, dv_hbm.at[j], sem.at[1])
            ck.start()
            cv.start()
            ck.wait()
            cv.wait()

    anyspec = pl.BlockSpec(memory_space=pl.ANY)
    ospec = pl.BlockSpec((tq, G * LANE), lambda j, i: (i, j))
    return pl.pallas_call(
        body, grid=(AT_KV, nq),
        in_specs=[pl.BlockSpec((G, tq, LANE), lambda j, i: (j, i, 0)), anyspec, anyspec, ospec, ospec,
                  pl.BlockSpec((G, 1, tq), lambda j, i: (j, 0, i))],
        out_specs=[pl.BlockSpec((G, tq, LANE), lambda j, i: (j, i, 0)), anyspec, anyspec],
        out_shape=[SDS((AT_HEADS, L, LANE), F32), SDS((AT_KV, L, LANE), F32), SDS((AT_KV, L, LANE), F32)],
        scratch_shapes=[pltpu.VMEM((L, LANE), BF16), pltpu.VMEM((L, LANE), BF16), pltpu.VMEM((L, LANE), F32),
                        pltpu.VMEM((L, LANE), F32), pltpu.VMEM((LANE, R), F32), pltpu.SemaphoreType.DMA((2,))],
        compiler_params=_params(("arbitrary", "arbitrary"), VMEM_LARGE), name=name)(qm, kr, vb, do8, of8, lse)


def _merge_fwd(ya, o8, wua, wubp, z, *, name):
    L = ya.shape[0]
    D = wua.shape[1]
    tm, tn = _tile(L, 512), 256
    ga, gb = (Z_HG + Z_AT) // tn, (Z_HG + Z_AT + D) // tn

    def body(ya_ref, o8_ref, wa_ref, wb_ref, za_ref, zb_ref, mix_ref):
        pa = _nn(ya_ref[...], wa_ref[...])
        pb = _nn(o8_ref[...], wb_ref[...])
        mix_ref[...] = (_sigmoid(za_ref[...]) * pa + _sigmoid(zb_ref[...]) * pb).astype(BF16)

    return pl.pallas_call(
        body, grid=(D // tn, L // tm),
        in_specs=[pl.BlockSpec((tm, ya.shape[1]), lambda j, i: (i, 0)), pl.BlockSpec((tm, o8.shape[1]), lambda j, i: (i, 0)),
                  pl.BlockSpec((wua.shape[0], tn), lambda j, i: (0, j)), pl.BlockSpec((wubp.shape[0], tn), lambda j, i: (0, j)),
                  pl.BlockSpec((tm, tn), lambda j, i: (i, ga + j)), pl.BlockSpec((tm, tn), lambda j, i: (i, gb + j))],
        out_specs=pl.BlockSpec((tm, tn), lambda j, i: (i, j)), out_shape=SDS((L, D), BF16),
        compiler_params=_params(("parallel", "parallel")), name=name)(ya, o8, wua, wubp, z, z)


def _merge_bwd(dh, wout, ya, o8, wua, wubp, z, *, name):
    L = ya.shape[0]
    D = wua.shape[1]
    tm, tn = _tile(L, 512), 256
    ga, gb = (Z_HG + Z_AT) // tn, (Z_HG + Z_AT + D) // tn

    def body(dh_ref, wo_ref, ya_ref, o8_ref, wa_ref, wb_ref, za_ref, zb_ref, dpa_ref, dpb_ref, dza_ref, dzb_ref):
        dm = _nt(dh_ref[...].astype(BF16), wo_ref[...])
        pa = _nn(ya_ref[...], wa_ref[...])
        pb = _nn(o8_ref[...], wb_ref[...])
        sa, sb = _sigmoid(za_ref[...]), _sigmoid(zb_ref[...])
        dpa_ref[...] = (dm * sa).astype(BF16)
        dpb_ref[...] = (dm * sb).astype(BF16)
        dza_ref[...] = (dm * pa * sa * (1.0 - sa)).astype(BF16)
        dzb_ref[...] = (dm * pb * sb * (1.0 - sb)).astype(BF16)

    ospec = pl.BlockSpec((tm, tn), lambda j, i: (i, j))
    return pl.pallas_call(
        body, grid=(D // tn, L // tm),
        in_specs=[pl.BlockSpec((tm, D), lambda j, i: (i, 0)), pl.BlockSpec((tn, D), lambda j, i: (j, 0)),
                  pl.BlockSpec((tm, ya.shape[1]), lambda j, i: (i, 0)), pl.BlockSpec((tm, o8.shape[1]), lambda j, i: (i, 0)),
                  pl.BlockSpec((wua.shape[0], tn), lambda j, i: (0, j)), pl.BlockSpec((wubp.shape[0], tn), lambda j, i: (0, j)),
                  pl.BlockSpec((tm, tn), lambda j, i: (i, ga + j)), pl.BlockSpec((tm, tn), lambda j, i: (i, gb + j))],
        out_specs=[ospec] * 4, out_shape=[SDS((L, D), BF16)] * 4,
        compiler_params=_params(("parallel", "parallel")), name=name)(dh, wout, ya, o8, wua, wubp, z, z)


def _loss_head(h, tgt, *, name):
    L, D = h.shape
    tm = PAD

    def body(h_ref, t_ref, dh_ref, ls_ref):
        i = pl.program_id(0)

        @pl.when(i == 0)
        def _():
            ls_ref[...] = jnp.zeros_like(ls_ref)
            dh_ref[...] = jnp.zeros_like(dh_ref)

        @pl.when(i > 0)
        def _():
            e = h_ref[...] - t_ref[...]
            dh_ref[...] = e * (1.0 / D)
            s = jnp.sum(e * e, axis=0, keepdims=True)
            tot = s[:, :LANE]
            for c in range(1, D // LANE):
                tot = tot + s[:, LANE * c:LANE * (c + 1)]
            ls_ref[...] += tot

    return pl.pallas_call(
        body, grid=(L // tm,),
        in_specs=[pl.BlockSpec((tm, D), lambda i: (i, 0)), pl.BlockSpec((tm, D), lambda i: (jnp.maximum(i - 1, 0), 0))],
        out_specs=[pl.BlockSpec((tm, D), lambda i: (i, 0)), pl.BlockSpec((1, LANE), lambda i: (0, 0))],
        out_shape=[SDS((L, D), F32), SDS((1, LANE), F32)],
        compiler_params=_params(("arbitrary",)), name=name)(h, tgt)


def _adamw(w, g, m, v, *, name):
    shape = w.shape
    w2, g2, m2, v2 = [a.reshape(-1, shape[-1]) for a in (w, g, m, v)]
    rows, cols = w2.shape
    tr = _tile(rows, 256, 8)

    def body(w_ref, g_ref, m_ref, v_ref, d_ref, nm_ref, nv_ref):
        g = g_ref[...]
        m = ADAM_B1 * m_ref[...] + (1.0 - ADAM_B1) * g
        v = ADAM_B2 * v_ref[...] + (1.0 - ADAM_B2) * (g * g)
        m_hat = m / (1.0 - ADAM_B1 ** ADAM_STEP)
        v_hat = v / (1.0 - ADAM_B2 ** ADAM_STEP)
        d_ref[...] = -ADAM_LR * (m_hat / (jnp.sqrt(v_hat) + ADAM_EPS) + ADAM_WD * w_ref[...])
        nm_ref[...] = m
        nv_ref[...] = v

    spec = pl.BlockSpec((tr, cols), lambda i: (i, 0))
    outs = pl.pallas_call(
        body, grid=(rows // tr,), in_specs=[spec] * 4, out_specs=[spec] * 3, out_shape=[SDS((rows, cols), F32)] * 3,
        compiler_params=_params(("parallel",)), name=name)(w2, g2, m2, v2)
    return [o.reshape(shape) for o in outs]


def _sum_slabs(x, *, name):
    n, R, _ = x.shape
    tr = _tile(R, 2048, 8)

    def body(x_ref, o_ref):
        tot = x_ref[0]
        for s in range(1, n):
            tot = tot + x_ref[s]
        o_ref[...] = tot

    return pl.pallas_call(
        body, grid=(R // tr,), in_specs=[pl.BlockSpec((n, tr, LANE), lambda i: (0, i, 0))],
        out_specs=pl.BlockSpec((tr, LANE), lambda i: (i, 0)), out_shape=SDS((R, LANE), F32),
        compiler_params=_params(("parallel",)), name=name)(x)


def _add_pair(a, b, *, name):
    n, R, _ = a.shape
    tr = _tile(R, 2048, 8)

    def body(a_ref, b_ref, o_ref):
        o_ref[...] = a_ref[...] + b_ref[...]

    spec = pl.BlockSpec((1, tr, LANE), lambda s, i: (s, i, 0))
    return pl.pallas_call(
        body, grid=(n, R // tr), in_specs=[spec, spec], out_specs=spec, out_shape=SDS(a.shape, F32),
        compiler_params=_params(("parallel", "parallel")), name=name)(a, b)


def _place():
    return lax.axis_index("x"), lax.axis_index("y"), lax.axis_index("c")


def _allgather_small(v, *, name):
    m_per, n = v.shape

    def body(x_ref, out_ref, send_sems, recv_sems, local_sem):
        x, y, c = _place()
        me, sibling = (x, y, c), (x, y, 1 - c)
        chips = [(1 - x, y), (x, 1 - y), (1 - x, 1 - y)]

        def rows(px, py, pc):
            return out_ref.at[pl.ds((4 * px + 2 * py + pc) * m_per, m_per), :]

        def copy(k, block, to, src=None):
            return pltpu.make_async_remote_copy(
                src_ref=rows(*block) if src is None else src, dst_ref=rows(*block),
                send_sem=send_sems.at[k], recv_sem=recv_sems.at[k], device_id=to, device_id_type=MESH)

        mine = pltpu.make_async_copy(x_ref, rows(*me), local_sem)
        mine.start()
        first = [copy(0, me, sibling, src=x_ref)]
        first += [copy(1 + j, me, (*chip, c), src=x_ref) for j, chip in enumerate(chips)]
        for cp in first:
            cp.start()
        passed = [copy(4 + j, (*chip, c), sibling) for j, chip in enumerate(chips)]
        for j, chip in enumerate(chips):
            copy(1 + j, (*chip, c), me).wait_recv()
            passed[j].start()
        copy(0, sibling, me).wait_recv()
        for j, chip in enumerate(chips):
            copy(4 + j, (*chip, 1 - c), me).wait_recv()
        for cp in first + passed:
            cp.wait_send()
        mine.wait()

    return pl.pallas_call(
        body, out_shape=SDS((8 * m_per, n), v.dtype),
        in_specs=[pl.BlockSpec(memory_space=pltpu.VMEM)], out_specs=pl.BlockSpec(memory_space=pltpu.VMEM),
        scratch_shapes=[pltpu.SemaphoreType.DMA((7,)), pltpu.SemaphoreType.DMA((7,)), pltpu.SemaphoreType.DMA],
        name=name)(v)


def _gather_weights(wp, *, name):
    _, R, _ = wp.shape

    def body(w_ref, out_ref, send_sems, recv_sems, local_sem):
        x, y, c = _place()
        sibling = (x, y, 1 - c)
        chips = [(1 - x, y), (x, 1 - y), (1 - x, 1 - y)]

        def slot(px, py, half):
            return out_ref.at[2 * px + py, half]

        def copy(k, src, dst, to):
            return pltpu.make_async_remote_copy(src_ref=src, dst_ref=dst, send_sem=send_sems.at[k],
                                                recv_sem=recv_sems.at[k], device_id=to, device_id_type=MESH)

        mine = pltpu.make_async_copy(w_ref, out_ref.at[2 * x + y], local_sem)
        mine.start()
        first = [copy(j, w_ref.at[c], slot(x, y, c), (*chip, c)) for j, chip in enumerate(chips)]
        for cp in first:
            cp.start()
        passed = [copy(3 + j, slot(*chip, c), slot(*chip, c), sibling) for j, chip in enumerate(chips)]
        for j, chip in enumerate(chips):
            copy(j, w_ref.at[c], slot(*chip, c), (*chip, c)).wait_recv()
            passed[j].start()
        for j, chip in enumerate(chips):
            copy(3 + j, slot(*chip, 1 - c), slot(*chip, 1 - c), sibling).wait_recv()
        for cp in first + passed:
            cp.wait_send()
        mine.wait()

    anyspec = pl.BlockSpec(memory_space=pl.ANY)
    return pl.pallas_call(
        body, out_shape=SDS((4, 2, R, LANE), wp.dtype), in_specs=[anyspec], out_specs=anyspec,
        scratch_shapes=[pltpu.SemaphoreType.DMA((6,)), pltpu.SemaphoreType.DMA((6,)), pltpu.SemaphoreType.DMA],
        name=name)(wp)


def _pair_exchange(g, *, name):
    _, n, R, _ = g.shape

    def body(g_ref, out_ref, send_sem, recv_sem):
        x, y, c = _place()
        cp = pltpu.make_async_remote_copy(src_ref=g_ref.at[1 - c], dst_ref=out_ref, send_sem=send_sem,
                                          recv_sem=recv_sem, device_id=(x, y, 1 - c), device_id_type=MESH)
        cp.start()
        cp.wait()

    anyspec = pl.BlockSpec(memory_space=pl.ANY)
    return pl.pallas_call(
        body, out_shape=SDS((n, R, LANE), g.dtype), in_specs=[anyspec], out_specs=anyspec,
        scratch_shapes=[pltpu.SemaphoreType.DMA, pltpu.SemaphoreType.DMA], name=name)(g)


def _chip_exchange(part, *, name):
    n, R, _ = part.shape

    def body(p_ref, out_ref, send_sems, recv_sems, local_sem):
        x, y, c = _place()
        s_me = 2 * x + y
        chips = [(1 - x, y), (x, 1 - y), (1 - x, 1 - y)]

        def copy(k, chip):
            return pltpu.make_async_remote_copy(
                src_ref=p_ref.at[2 * chip[0] + chip[1]], dst_ref=out_ref.at[s_me], send_sem=send_sems.at[k],
                recv_sem=recv_sems.at[k], device_id=(*chip, c), device_id_type=MESH)

        def landed(k, chip):
            return pltpu.make_async_remote_copy(
                src_ref=p_ref.at[s_me], dst_ref=out_ref.at[2 * chip[0] + chip[1]], send_sem=send_sems.at[k],
                recv_sem=recv_sems.at[k], device_id=(*chip, c), device_id_type=MESH)

        mine = pltpu.make_async_copy(p_ref.at[s_me], out_ref.at[s_me], local_sem)
        mine.start()
        sends = [copy(k, chip) for k, chip in enumerate(chips)]
        for cp in sends:
            cp.start()
        for k, chip in enumerate(chips):
            landed(k, chip).wait_recv()
        for cp in sends:
            cp.wait_send()
        mine.wait()

    anyspec = pl.BlockSpec(memory_space=pl.ANY)
    return pl.pallas_call(
        body, out_shape=SDS((n, R, LANE), part.dtype), in_specs=[anyspec], out_specs=anyspec,
        scratch_shapes=[pltpu.SemaphoreType.DMA((3,)), pltpu.SemaphoreType.DMA((3,)), pltpu.SemaphoreType.DMA],
        name=name)(part)


def _pair_share(red, *, name):
    R, _ = red.shape

    def body(r_ref, out_ref, send_sem, recv_sem, local_sem):
        x, y, c = _place()
        mine = pltpu.make_async_copy(r_ref, out_ref.at[c], local_sem)
        mine.start()
        cp = pltpu.make_async_remote_copy(src_ref=r_ref, dst_ref=out_ref.at[c], send_sem=send_sem,
                                          recv_sem=recv_sem, device_id=(x, y, 1 - c), device_id_type=MESH)
        cp.start()
        pltpu.make_async_remote_copy(src_ref=r_ref, dst_ref=out_ref.at[1 - c], send_sem=send_sem,
                                     recv_sem=recv_sem, device_id=(x, y, 1 - c), device_id_type=MESH).wait_recv()
        cp.wait_send()
        mine.wait()

    anyspec = pl.BlockSpec(memory_space=pl.ANY)
    return pl.pallas_call(
        body, out_shape=SDS((2, R, LANE), red.dtype), in_specs=[anyspec], out_specs=anyspec,
        scratch_shapes=[pltpu.SemaphoreType.DMA, pltpu.SemaphoreType.DMA, pltpu.SemaphoreType.DMA], name=name)(red)


def _finish_small(gathered, lbf, lbb, *, rows, name):
    r_lbf, r_lbb = rows['lb_f'], rows['lb_b']

    def body(g_ref, lbf_ref, lbb_ref, o_ref, dlf_ref, dlb_ref):
        tot = g_ref[0]
        for s in range(1, 8):
            tot = tot + g_ref[s]
        o_ref[...] = tot
        o_ref[0:1, :] = jnp.broadcast_to(jnp.sum(o_ref[0:1, :], axis=1, keepdims=True), (1, LANE))
        for lb_ref, d_ref, r0 in ((lbf_ref, dlf_ref, r_lbf), (lbb_ref, dlb_ref, r_lbb)):
            for hh in range(HG_HEADS):
                sl = slice(LANE * hh, LANE * (hh + 1))
                l0, l1 = lb_ref[0:1, sl], lb_ref[1:2, sl]
                mx = jnp.maximum(l0, l1)
                e0, e1 = jnp.exp(l0 - mx), jnp.exp(l1 - mx)
                p0 = e0 / (e0 + e1)
                d0 = o_ref[r0 + hh:r0 + hh + 1, :] * p0 * (1.0 - p0)
                d_ref[0:1, sl] = d0
                d_ref[1:2, sl] = -d0

    vm = pl.BlockSpec(memory_space=pltpu.VMEM)
    return pl.pallas_call(
        body, in_specs=[vm, vm, vm], out_specs=[vm, vm, vm],
        out_shape=[SDS(gathered.shape[1:], F32), SDS(lbf.shape, F32), SDS(lbb.shape, F32)], name=name)(gathered, lbf, lbb)


def _pad_len(n):
    q = 2 * 128 * LANE
    return (n + q - 1) // q * q


def _pack_local(shards, dtype):
    flat = jnp.concatenate([s.astype(dtype).reshape(-1) for s in shards])
    n = flat.shape[0]
    flat = jnp.pad(flat, (0, _pad_len(n) - n))
    return flat.reshape(2, -1, LANE)


def _unpack_full(gathered, shapes, names):
    out, off = {}, 0
    for name in names:
        r, cs = shapes[name]
        blk = gathered[:, off:off + r * cs].reshape(4, r, cs)
        off += r * cs
        out[name] = blk.reshape(4 * r, cs) if name in ROW_SHARDED else blk.transpose(1, 0, 2).reshape(r, 4 * cs)
    return out


def _pack_grads(grads, shapes, names):
    cols = []
    for name in names:
        r, cs = shapes[name]
        g = grads[name]
        blk = g.reshape(4, r * cs) if name in ROW_SHARDED else g.reshape(r, 4, cs).transpose(1, 0, 2).reshape(4, r * cs)
        cols.append(blk)
    flat = jnp.concatenate(cols, axis=1)
    n = flat.shape[1]
    flat = jnp.pad(flat, ((0, 0), (0, _pad_len(n) - n)))
    return flat.reshape(4, 2, -1, LANE).transpose(1, 0, 2, 3)


def _unpack_shard(flat, shapes, names):
    out, off = {}, 0
    for name in names:
        r, cs = shapes[name]
        out[name] = flat[off:off + r * cs].reshape(r, cs)
        off += r * cs
    return out


def _local_step(x2, tgt2, meta, W, S):
    T, D = x2.shape
    L = PAD + T
    h0 = jnp.concatenate([jnp.zeros((PAD - N_META, D), F32), meta, x2], axis=0)

    qk0 = Z_HG
    w_in = jnp.concatenate([W['w_in'][:, :qk0], _qk_to_group(W['w_in'][:, qk0:qk0 + AT_W + AT_KVW]),
                            W['w_in'][:, qk0 + AT_W + AT_KVW:]], axis=1)
    wubp = jnp.pad(W['w_up_b'].reshape(AT_HEADS, AT_HD, D), ((0, 0), (0, LANE - AT_HD), (0, 0))).reshape(AT_HEADS * LANE, D)
    cc, ss = _rope_tables(L)
    wq_g, wk_g = _group_vec(S['q_norm']), _group_vec(S['k_norm'])

    def ffn_fwd(h, nw, wg, wu, wd, tag):
        n = _rmsnorm_fwd(h, nw, name=tag + "_norm")
        g, u, a = _ffn_up(n, wg, wu, name=tag + "_up")
        hn = _mm([(a, wd)], res=h, alpha=0.5, tm=512, tn=D, tk=wd.shape[0], name=tag + "_down")
        return hn, (n, g, u, a)

    def ffn_bwd(dh, h, nw, wg, wu, wd, saved, tag):
        n, g, u, a = saved
        dg, du = _ffn_dact(dh, wd, g, u, name=tag + "_dact")
        dn = _mm([(dg, wg), (du, wu)], tb=True, tm=512, tn=D, tk=1408, name=tag + "_dn")
        dwg = _mm([(n, dg)], ta=True, tm=D, tn=1408, tk=512, name=tag + "_dwg")
        dwu = _mm([(n, du)], ta=True, tm=D, tn=1408, tk=512, name=tag + "_dwu")
        dwd = _mm([(a, dh)], ta=True, alpha=0.5, tm=1408, tn=D, tk=512, name=tag + "_dwd")
        dhp, dnw = _rmsnorm_bwd(h, nw, dn, dh, name=tag + "_norm_bwd")
        return dhp, dnw, dwg, dwu, dwd

    h1, sv1 = ffn_fwd(h0, S['ffn1_norm'], W['ffn1_w_gate'], W['ffn1_w_up'], W['ffn1_w_down'], "ffn1")
    um = _rmsnorm_fwd(h1, S['mix_norm'], name="mix_norm")
    z = _mm([(um, w_in)], tm=512, tn=1792, tk=D, name="in_proj")
    of, sf = _hg_fwd(z, S['hg_lb_fwd'], rev=False, name="hg_fwd_f")
    ob, sb = _hg_fwd(z, S['hg_lb_bwd'], rev=True, name="hg_fwd_b")
    ya = _hg_post_fwd(of, ob, z, S['hg_out_norm'], name="hg_post")
    qm, kr, vb = _at_prep(z, cc, ss, wq_g, wk_g, name="at_prep")
    o8, o8f, lse = _at_fwd(qm, kr, vb, name="at_fwd")
    mixed = _merge_fwd(ya, o8, W['w_up_a'], wubp, z, name="merge")
    h2 = _mm([(mixed, W['w_out'])], res=h1, tm=512, tn=D, tk=D, name="out_proj")
    h3, sv2 = ffn_fwd(h2, S['ffn2_norm'], W['ffn2_w_gate'], W['ffn2_w_up'], W['ffn2_w_down'], "ffn2")
    dh3, loss_lanes = _loss_head(h3, tgt2, name="loss_head")

    G = {}
    dh2, dn_ffn2, G['ffn2_w_gate'], G['ffn2_w_up'], G['ffn2_w_down'] = ffn_bwd(
        dh3, h2, S['ffn2_norm'], W['ffn2_w_gate'], W['ffn2_w_up'], W['ffn2_w_down'], sv2, "ffn2")
    dpa, dpb, dzga, dzgb = _merge_bwd(dh2, W['w_out'], ya, o8, W['w_up_a'], wubp, z, name="merge_bwd")
    G['w_out'] = _mm([(mixed, dh2)], ta=True, tm=D, tn=D, tk=512, name="d_w_out")
    dya = _mm([(dpa, W['w_up_a'])], tb=True, tm=512, tn=HG_W, tk=D, name="d_ya")
    do8 = _mm([(dpb, wubp)], tb=True, tm=512, tn=AT_HEADS * LANE, tk=D, name="d_o8")
    G['w_up_a'] = _mm([(ya, dpa)], ta=True, tm=HG_W, tn=D, tk=512, name="d_w_up_a")
    dwubp = _mm([(o8, dpb)], ta=True, tm=AT_HEADS * LANE, tn=D, tk=512, name="d_w_up_b")
    G['w_up_b'] = dwubp.reshape(AT_HEADS, LANE, D)[:, :AT_HD].reshape(AT_W, D)
    do_hg, dzg, d_hgn = _hg_post_bwd(dya, of, ob, z, S['hg_out_norm'], name="hg_post_bwd")
    dq_f, dv_f, dzf_f, dlb_f = _hg_bwd(z, S['hg_lb_fwd'], do_hg, sf, None, rev=False, name="hg_bwd_f")
    dzq, dzi, dzf_b, dlb_b = _hg_bwd(z, S['hg_lb_bwd'], do_hg, sb, (dq_f, dv_f), rev=True, name="hg_bwd_b")
    dqm, dk2, dv2 = _at_bwd(qm, kr, vb, do8, o8f, lse, name="at_bwd")
    dz_at, dwq_g, dwk_g = _at_prep_bwd(dqm, dk2, dv2, z, cc, ss, wq_g, wk_g, name="at_prep_bwd")
    dz = jnp.concatenate([dzq, dzi, dzf_f, dzf_b, dzg, dz_at, dzga, dzgb], axis=1)
    dum = _mm([(dz, w_in)], tb=True, tm=512, tn=D, tk=1792, name="d_um")
    dw_in_p = _mm([(um, dz)], ta=True, tm=D, tn=1792, tk=512, name="d_w_in")
    G['w_in'] = jnp.concatenate([dw_in_p[:, :qk0], _qk_from_group(dw_in_p[:, qk0:qk0 + AT_W + AT_KVW]),
                                 dw_in_p[:, qk0 + AT_W + AT_KVW:]], axis=1)
    dh1, dn_mix = _rmsnorm_bwd(h1, S['mix_norm'], dum, dh2, name="mix_norm_bwd")
    dh0, dn_ffn1, G['ffn1_w_gate'], G['ffn1_w_up'], G['ffn1_w_down'] = ffn_bwd(
        dh1, h0, S['ffn1_norm'], W['ffn1_w_gate'], W['ffn1_w_up'], W['ffn1_w_down'], sv1, "ffn1")

    small_rows = [('loss', loss_lanes), ('ffn1_norm', dn_ffn1.reshape(-1, LANE)), ('mix_norm', dn_mix.reshape(-1, LANE)),
                  ('ffn2_norm', dn_ffn2.reshape(-1, LANE)), ('hg_out_norm', d_hgn.reshape(-1, LANE)),
                  ('lb_f', dlb_f.reshape(-1, LANE)), ('lb_b', dlb_b.reshape(-1, LANE)), ('q_norm', dwq_g), ('k_norm', dwk_g)]
    return dh0[PAD:], dh0[PAD - N_META:PAD], G, small_rows


def kernel(x, meta_tokens, ffn1_norm, ffn1_w_gate, ffn1_w_up, ffn1_w_down, mix_norm, w_in, hg_lb_fwd, hg_lb_bwd, hg_out_norm, q_norm, k_norm, w_up_a, w_up_b, w_out, ffn2_norm, ffn2_w_gate, ffn2_w_up, ffn2_w_down, loss_target, m_meta_tokens, m_ffn1_norm, m_ffn1_w_gate, m_ffn1_w_up, m_ffn1_w_down, m_mix_norm, m_w_in, m_hg_lb_fwd, m_hg_lb_bwd, m_hg_out_norm, m_q_norm, m_k_norm, m_w_up_a, m_w_up_b, m_w_out, m_ffn2_norm, m_ffn2_w_gate, m_ffn2_w_up, m_ffn2_w_down, v_meta_tokens, v_ffn1_norm, v_ffn1_w_gate, v_ffn1_w_up, v_ffn1_w_down, v_mix_norm, v_w_in, v_hg_lb_fwd, v_hg_lb_bwd, v_hg_out_norm, v_q_norm, v_k_norm, v_w_up_a, v_w_up_b, v_w_out, v_ffn2_norm, v_ffn2_w_gate, v_ffn2_w_up, v_ffn2_w_down):
    given = dict(locals())
    w = {n: given[n] for n in WEIGHTS}
    mom = {n: given["m_" + n] for n in WEIGHTS}
    var = {n: given["v_" + n] for n in WEIGHTS}
    c = lax.axis_index("c")
    D = x.shape[-1]

    shard2d = {n: w[n].reshape(w[n].shape[-2:]) for n in MATS}
    shapes = {n: shard2d[n].shape for n in MATS}
    shapes['meta_tokens'] = w['meta_tokens'].shape
    n_w = sum(r * cs for r, cs in (shapes[n] for n in MATS))
    gathered = _gather_weights(_pack_local([shard2d[n] for n in MATS], BF16), name="gather_weights")
    W = _unpack_full(gathered.reshape(4, -1)[:, :n_w], shapes, MATS)
    meta_rows = w['meta_tokens'].reshape(-1, LANE)
    mg = _allgather_small(meta_rows, name="gather_meta").reshape(4, 2, N_META, -1)[:, 0]
    meta = mg.transpose(1, 0, 2).reshape(N_META, D)
    S = {n: w[n] for n in SMALLS}

    grad_x, dmeta, G, small_rows = _local_step(x[0], loss_target[0], meta, W, S)
    G['meta_tokens'] = dmeta

    names = MATS + ('meta_tokens',)
    gp = _pack_grads(G, shapes, names)
    from_sibling = _pair_exchange(gp, name="rs_pair_exchange")
    part = _add_pair(lax.dynamic_index_in_dim(gp, c, 0, keepdims=False), from_sibling, name="rs_pair_sum")
    slabs = _chip_exchange(part, name="rs_chip_exchange")
    red = _sum_slabs(slabs, name="rs_chip_sum")
    both = _pair_share(red, name="rs_pair_share")
    grads = _unpack_shard(both.reshape(-1), shapes, names)
    grads = {n: grads[n].reshape(w[n].shape) for n in names}

    rows, off = {}, 0
    for nme, blk in small_rows:
        rows[nme] = off
        off += blk.shape[0]
    block = jnp.concatenate([blk for _, blk in small_rows], axis=0)
    n_rows = (off + 7) // 8 * 8
    block = jnp.pad(block, ((0, n_rows - off), (0, 0)))
    allsmall = _allgather_small(block, name="gather_small").reshape(8, n_rows, LANE)
    tot, d_lbf, d_lbb = _finish_small(allsmall, w['hg_lb_fwd'], w['hg_lb_bwd'], rows=rows, name="finish_small")
    loss = 0.5 * tot[0, 0] / D

    def small(nme, shape):
        r0 = rows[nme]
        return tot[r0:r0 + shape[-1] // LANE].reshape(shape)

    grads['ffn1_norm'] = small('ffn1_norm', w['ffn1_norm'].shape)
    grads['mix_norm'] = small('mix_norm', w['mix_norm'].shape)
    grads['ffn2_norm'] = small('ffn2_norm', w['ffn2_norm'].shape)
    grads['hg_out_norm'] = small('hg_out_norm', w['hg_out_norm'].shape)
    grads['hg_lb_fwd'] = d_lbf
    grads['hg_lb_bwd'] = d_lbb
    grads['q_norm'] = _ungroup_vec(tot[rows['q_norm']])
    grads['k_norm'] = _ungroup_vec(tot[rows['k_norm']])

    delta, new_m, new_v = {}, {}, {}
    for n in WEIGHTS:
        delta[n], new_m[n], new_v[n] = _adamw(w[n], grads[n], mom[n], var[n], name="adamw_" + n)
    return (loss, grad_x[None], *[grads[n] for n in WEIGHTS], *[delta[n] for n in WEIGHTS],
            *[new_m[n] for n in WEIGHTS], *[new_v[n] for n in WEIGHTS])
```

```python
import functools
import math

import numpy as np
import jax
import jax.numpy as jnp
from jax import lax
from jax.experimental import pallas as pl
from jax.experimental.pallas import tpu as pltpu

F32 = jnp.float32
BF16 = jnp.bfloat16
SDS = jax.ShapeDtypeStruct
MESH = pl.DeviceIdType.MESH

EPS = 1e-6
N_META = 16
PAD = 512
LANE = 128
CHUNK = 128
HG_HEADS = 4
HG_W = HG_HEADS * 128
AT_HEADS = 8
AT_KV = 2
AT_HD = 64
AT_W = AT_HEADS * AT_HD
AT_KVW = AT_KV * AT_HD
VT_ROWS = AT_HD + 16
GRID_W = 64
ROPE_THETA = 10000.0
Z_HG = 5 * HG_W
Z_AT = AT_W + 2 * AT_KVW
ADAM_LR, ADAM_B1, ADAM_B2, ADAM_EPS, ADAM_WD, ADAM_STEP = 0.001, 0.9, 0.999, 1e-08, 0.01, 10
VMEM_DEFAULT = 48 * 1024 * 1024
VMEM_LARGE = 60 * 1024 * 1024
NEG = -1e30

MATS = ('ffn1_w_gate', 'ffn1_w_up', 'ffn1_w_down', 'w_in', 'w_up_a', 'w_up_b', 'w_out',
        'ffn2_w_gate', 'ffn2_w_up', 'ffn2_w_down')
ROW_SHARDED = ('ffn1_w_down', 'w_out', 'ffn2_w_down')
FFN_MATS = ('ffn1_w_gate', 'ffn1_w_up', 'ffn1_w_down', 'ffn2_w_gate', 'ffn2_w_up', 'ffn2_w_down')
SMALLS = ('ffn1_norm', 'mix_norm', 'hg_lb_fwd', 'hg_lb_bwd', 'hg_out_norm', 'q_norm', 'k_norm', 'ffn2_norm')
WEIGHTS = ('meta_tokens', 'ffn1_norm', 'ffn1_w_gate', 'ffn1_w_up', 'ffn1_w_down', 'mix_norm', 'w_in', 'hg_lb_fwd',
           'hg_lb_bwd', 'hg_out_norm', 'q_norm', 'k_norm', 'w_up_a', 'w_up_b', 'w_out', 'ffn2_norm', 'ffn2_w_gate',
           'ffn2_w_up', 'ffn2_w_down')


def _params(sem=None, vmem=VMEM_DEFAULT):
    return pltpu.CompilerParams(dimension_semantics=sem, vmem_limit_bytes=vmem)


def _tile(n, pref, q=LANE):
    for d in range(min(pref, n), 0, -1):
        if n % d == 0 and d % q == 0:
            return d
    return n


def _sigmoid(x):
    return 1.0 / (1.0 + jnp.exp(-x))


def _dot(a, b, dims):
    return lax.dot_general(a, b, (dims, ((), ())), preferred_element_type=F32)


def _nn(a, b):
    return _dot(a, b, ((1,), (0,)))


def _nt(a, b):
    return _dot(a, b, ((1,), (1,)))


def _tn(a, b):
    return _dot(a, b, ((0,), (0,)))


def _split3(x):
    x1 = x.astype(BF16)
    r = x - x1.astype(F32)
    x2 = r.astype(BF16)
    x3 = (r - x2.astype(F32)).astype(BF16)
    return x1, x2, x3


def _exact_left(m01, x):
    x1, x2, x3 = _split3(x)
    return _nn(m01, x1) + _nn(m01, x2) + _nn(m01, x3)


def _exact_right(x, m01):
    x1, x2, x3 = _split3(x)
    return _nn(x1, m01) + _nn(x2, m01) + _nn(x3, m01)


def _mm(pairs, *, name, ta=False, tb=False, out_dtype=F32, tm=512, tn=1024, tk=1024, alpha=1.0, res=None):
    a0, b0 = pairs[0]
    M = a0.shape[1] if ta else a0.shape[0]
    K = a0.shape[0] if ta else a0.shape[1]
    N = b0.shape[0] if tb else b0.shape[1]
    tm, tn, tk = _tile(M, tm), _tile(N, tn), _tile(K, tk)
    nk = K // tk
    npair = len(pairs)
    dims = ((0 if ta else 1,), (1 if tb else 0,))

    def body(*refs):
        ab = refs[:2 * npair]
        pos = 2 * npair
        res_ref = None
        if res is not None:
            res_ref = refs[pos]
            pos += 1
        o_ref = refs[pos]

        def partial_sum():
            tot = None
            for p in range(npair):
                d = _dot(ab[2 * p][...].astype(BF16), ab[2 * p + 1][...].astype(BF16), dims)
                tot = d if tot is None else tot + d
            return tot

        def finish(acc):
            r = acc if alpha == 1.0 else acc * alpha
            if res_ref is not None:
                r = res_ref[...] + r
            o_ref[...] = r.astype(out_dtype)

        if nk == 1:
            finish(partial_sum())
        else:
            acc_ref = refs[pos + 1]
            k = pl.program_id(2)

            @pl.when(k == 0)
            def _():
                acc_ref[...] = jnp.zeros_like(acc_ref)

            acc_ref[...] += partial_sum()

            @pl.when(k == nk - 1)
            def _():
                finish(acc_ref[...])

    a_spec = pl.BlockSpec((tk, tm), lambda j, i, k: (k, i)) if ta else pl.BlockSpec((tm, tk), lambda j, i, k: (i, k))
    b_spec = pl.BlockSpec((tn, tk), lambda j, i, k: (j, k)) if tb else pl.BlockSpec((tk, tn), lambda j, i, k: (k, j))
    o_spec = pl.BlockSpec((tm, tn), lambda j, i, k: (i, j))
    in_specs, args = [], []
    for a, b in pairs:
        in_specs += [a_spec, b_spec]
        args += [a, b]
    if res is not None:
        in_specs.append(o_spec)
        args.append(res)
    return pl.pallas_call(
        body, grid=(N // tn, M // tm, nk), in_specs=in_specs, out_specs=o_spec,
        out_shape=SDS((M, N), out_dtype),
        scratch_shapes=[pltpu.VMEM((tm, tn), F32)] if nk > 1 else [],
        compiler_params=_params(("parallel", "parallel", "arbitrary")), name=name)(*args)


def _rmsnorm_fwd(h, w, *, name):
    L, D = h.shape
    tm = _tile(L, 512)

    def body(h_ref, w_ref, o_ref):
        x = h_ref[...]
        r = lax.rsqrt(jnp.mean(x * x, axis=-1, keepdims=True) + EPS)
        o_ref[...] = (x * r * w_ref[...]).astype(BF16)

    return pl.pallas_call(
        body, grid=(L // tm,),
        in_specs=[pl.BlockSpec((tm, D), lambda i: (i, 0)), pl.BlockSpec((1, D), lambda i: (0, 0))],
        out_specs=pl.BlockSpec((tm, D), lambda i: (i, 0)), out_shape=SDS((L, D), BF16),
        compiler_params=_params(("parallel",)), name=name)(h, w)


def _rmsnorm_bwd(h, w, dn, dres, *, name):
    L, D = h.shape
    tm = _tile(L, 512)

    def body(h_ref, w_ref, dn_ref, dres_ref, dh_ref, dw_ref):
        x = h_ref[...]
        r = lax.rsqrt(jnp.mean(x * x, axis=-1, keepdims=True) + EPS)
        xh = x * r
        dn = dn_ref[...]
        dxh = dn * w_ref[...]
        dh_ref[...] = dres_ref[...] + r * (dxh - xh * jnp.mean(dxh * xh, axis=-1, keepdims=True))

        @pl.when(pl.program_id(0) == 0)
        def _():
            dw_ref[...] = jnp.zeros_like(dw_ref)

        dw_ref[...] += jnp.sum(dn * xh, axis=0, keepdims=True)

    row = pl.BlockSpec((tm, D), lambda i: (i, 0))
    vec = pl.BlockSpec((1, D), lambda i: (0, 0))
    return pl.pallas_call(
        body, grid=(L // tm,), in_specs=[row, vec, row, row], out_specs=[row, vec],
        out_shape=[SDS((L, D), F32), SDS((1, D), F32)],
        compiler_params=_params(("arbitrary",)), name=name)(h, w, dn, dres)


def _ffn_up(n, wg, wu, *, name):
    L, D = n.shape
    Fd = wg.shape[1]
    tm, tn = _tile(L, 512), _tile(Fd, 1408)

    def body(n_ref, wg_ref, wu_ref, g_ref, u_ref, a_ref):
        x = n_ref[...]
        g = _nn(x, wg_ref[...])
        u = _nn(x, wu_ref[...])
        g_ref[...] = g
        u_ref[...] = u
        a_ref[...] = (g * _sigmoid(g) * u).astype(BF16)

    wspec = pl.BlockSpec((D, tn), lambda j, i: (0, j))
    ospec = pl.BlockSpec((tm, tn), lambda j, i: (i, j))
    return pl.pallas_call(
        body, grid=(Fd // tn, L // tm),
        in_specs=[pl.BlockSpec((tm, D), lambda j, i: (i, 0)), wspec, wspec],
        out_specs=[ospec, ospec, ospec],
        out_shape=[SDS((L, Fd), F32), SDS((L, Fd), F32), SDS((L, Fd), BF16)],
        compiler_params=_params(("parallel", "parallel")), name=name)(n, wg, wu)


def _ffn_dact(dh, wd, g, u, *, name):
    L, D = dh.shape
    Fd = wd.shape[0]
    tm, tn = _tile(L, 512), _tile(Fd, 1408)

    def body(dh_ref, wd_ref, g_ref, u_ref, dg_ref, du_ref):
        da = 0.5 * _nt(dh_ref[...].astype(BF16), wd_ref[...])
        g = g_ref[...]
        sg = _sigmoid(g)
        dg_ref[...] = (da * u_ref[...] * (sg * (1.0 + g * (1.0 - sg)))).astype(BF16)
        du_ref[...] = (da * (g * sg)).astype(BF16)

    ospec = pl.BlockSpec((tm, tn), lambda j, i: (i, j))
    return pl.pallas_call(
        body, grid=(Fd // tn, L // tm),
        in_specs=[pl.BlockSpec((tm, D), lambda j, i: (i, 0)), pl.BlockSpec((tn, D), lambda j, i: (j, 0)), ospec, ospec],
        out_specs=[ospec, ospec], out_shape=[SDS((L, Fd), BF16), SDS((L, Fd), BF16)],
        compiler_params=_params(("parallel", "parallel")), name=name)(dh, wd, g, u)


def _ffn4_up(n, wg4, wu4, *, name):
    L, D = n.shape
    ns, _, cs = wg4.shape
    tm = _tile(L, 768)

    def body(n_ref, wg_ref, wu_ref, g_ref, u_ref, a_ref):
        x = n_ref[...]
        g = _nn(x, wg_ref[...])
        u = _nn(x, wu_ref[...])
        g_ref[...] = g
        u_ref[...] = u
        a_ref[...] = (g * _sigmoid(g) * u).astype(BF16)

    wspec = pl.BlockSpec((None, D, cs), lambda j, i: (j, 0, 0))
    ospec = pl.BlockSpec((None, tm, cs), lambda j, i: (j, i, 0))
    return pl.pallas_call(
        body, grid=(ns, L // tm),
        in_specs=[pl.BlockSpec((tm, D), lambda j, i: (i, 0)), wspec, wspec], out_specs=[ospec, ospec, ospec],
        out_shape=[SDS((ns, L, cs), F32), SDS((ns, L, cs), F32), SDS((ns, L, cs), BF16)],
        compiler_params=_params(("parallel", "parallel")), name=name)(n, wg4, wu4)


def _ffn4_down(a4, wd4, h, *, name):
    ns, L, cs = a4.shape
    D = wd4.shape[2]
    tm = _tile(L, 512)

    def body(a_ref, w_ref, h_ref, o_ref, acc_ref):
        j = pl.program_id(1)

        @pl.when(j == 0)
        def _():
            acc_ref[...] = jnp.zeros_like(acc_ref)

        acc_ref[...] += _nn(a_ref[...], w_ref[...])

        @pl.when(j == ns - 1)
        def _():
            o_ref[...] = h_ref[...] + 0.5 * acc_ref[...]

    row = pl.BlockSpec((tm, D), lambda i, j: (i, 0))
    return pl.pallas_call(
        body, grid=(L // tm, ns),
        in_specs=[pl.BlockSpec((None, tm, cs), lambda i, j: (j, i, 0)), pl.BlockSpec((None, cs, D), lambda i, j: (j, 0, 0)), row],
        out_specs=row, out_shape=SDS((L, D), F32), scratch_shapes=[pltpu.VMEM((tm, D), F32)],
        compiler_params=_params(("parallel", "arbitrary")), name=name)(a4, wd4, h)


def _ffn4_dact(dh, wd4, g4, u4, *, name):
    L, D = dh.shape
    ns, cs, _ = wd4.shape
    tm = _tile(L, 768)

    def body(dh_ref, wd_ref, g_ref, u_ref, dg_ref, du_ref):
        da = 0.5 * _nt(dh_ref[...].astype(BF16), wd_ref[...])
        g = g_ref[...]
        sg = _sigmoid(g)
        dg_ref[...] = (da * u_ref[...] * (sg * (1.0 + g * (1.0 - sg)))).astype(BF16)
        du_ref[...] = (da * (g * sg)).astype(BF16)

    ospec = pl.BlockSpec((None, tm, cs), lambda j, i: (j, i, 0))
    return pl.pallas_call(
        body, grid=(ns, L // tm),
        in_specs=[pl.BlockSpec((tm, D), lambda j, i: (i, 0)), pl.BlockSpec((None, cs, D), lambda j, i: (j, 0, 0)), ospec, ospec],
        out_specs=[ospec, ospec], out_shape=[SDS((ns, L, cs), BF16), SDS((ns, L, cs), BF16)],
        compiler_params=_params(("parallel", "parallel")), name=name)(dh, wd4, g4, u4)


def _ffn4_dn(dg4, du4, wg4, wu4, *, name):
    ns, L, cs = dg4.shape
    D = wg4.shape[1]
    tm = _tile(L, 512)

    def body(dg_ref, du_ref, wg_ref, wu_ref, o_ref, acc_ref):
        j = pl.program_id(1)

        @pl.when(j == 0)
        def _():
            acc_ref[...] = jnp.zeros_like(acc_ref)

        acc_ref[...] += _nt(dg_ref[...], wg_ref[...]) + _nt(du_ref[...], wu_ref[...])

        @pl.when(j == ns - 1)
        def _():
            o_ref[...] = acc_ref[...]

    aspec = pl.BlockSpec((None, tm, cs), lambda i, j: (j, i, 0))
    wspec = pl.BlockSpec((None, D, cs), lambda i, j: (j, 0, 0))
    return pl.pallas_call(
        body, grid=(L // tm, ns), in_specs=[aspec, aspec, wspec, wspec],
        out_specs=pl.BlockSpec((tm, D), lambda i, j: (i, 0)), out_shape=SDS((L, D), F32),
        scratch_shapes=[pltpu.VMEM((tm, D), F32)],
        compiler_params=_params(("parallel", "arbitrary")), name=name)(dg4, du4, wg4, wu4)


def _ffn4_dw(x, y4, *, x_is_rows, alpha=1.0, name):
    L, D = x.shape
    ns, _, cs = y4.shape
    tk = _tile(L, 512)
    nk = L // tk
    oshape = (D, cs) if x_is_rows else (cs, D)

    def body(x_ref, y_ref, o_ref, acc_ref):
        k = pl.program_id(1)

        @pl.when(k == 0)
        def _():
            acc_ref[...] = jnp.zeros_like(acc_ref)

        xb, yb = x_ref[...].astype(BF16), y_ref[...]
        acc_ref[...] += _tn(xb, yb) if x_is_rows else _tn(yb, xb)

        @pl.when(k == nk - 1)
        def _():
            o_ref[...] = acc_ref[...] if alpha == 1.0 else acc_ref[...] * alpha

    return pl.pallas_call(
        body, grid=(ns, nk),
        in_specs=[pl.BlockSpec((tk, D), lambda j, k: (k, 0)), pl.BlockSpec((None, tk, cs), lambda j, k: (j, k, 0))],
        out_specs=pl.BlockSpec((None,) + oshape, lambda j, k: (j, 0, 0)), out_shape=SDS((ns,) + oshape, F32),
        scratch_shapes=[pltpu.VMEM(oshape, F32)],
        compiler_params=_params(("parallel", "arbitrary")), name=name)(x, y4)


def _hg_masks(rev):
    t = lax.broadcasted_iota(jnp.int32, (CHUNK, CHUNK), 0)
    s = lax.broadcasted_iota(jnp.int32, (CHUNK, CHUNK), 1)
    causal = (s >= t) if rev else (s <= t)
    levels = []
    for sh in (6, 5, 4):
        same = jnp.right_shift(t, sh + 1) == jnp.right_shift(s, sh + 1)
        tq = jnp.bitwise_and(jnp.right_shift(t, sh), 1)
        sk = jnp.bitwise_and(jnp.right_shift(s, sh), 1)
        levels.append(same & (tq == (0 if rev else 1)) & (sk == (1 if rev else 0)))
    diag = (jnp.right_shift(t, 4) == jnp.right_shift(s, 4)) & causal
    return causal, levels, diag


def _hg_intra_factors(q, k, b, b_scr, rev):
    b_scr[...] = b
    row = lax.broadcasted_iota(jnp.int32, (CHUNK, LANE), 0)
    out = []
    for sh in (6, 5, 4):
        lb = 1 << sh
        pieces = []
        for p in range(0, CHUNK, 2 * lb):
            r = p + lb if rev else p + lb - 1
            pieces.append(jnp.broadcast_to(b_scr[pl.ds(r, 1), :], (2 * lb, LANE)))
        ref = pieces[0] if len(pieces) == 1 else jnp.concatenate(pieces, axis=0)
        qside = jnp.bitwise_and(jnp.right_shift(row, sh), 1) == (0 if rev else 1)
        eq = jnp.where(qside, jnp.exp(jnp.minimum(b - ref, 0.0)), 0.0)
        ek = jnp.where(qside, 0.0, jnp.exp(jnp.minimum(ref - b, 0.0)))
        out.append((eq, ek, (q * eq).astype(BF16), (k * ek).astype(BF16)))
    pieces = []
    for a in range(0, CHUNK, 16):
        r = a + (8 if rev else 7)
        pieces.append(jnp.broadcast_to(b_scr[pl.ds(r, 1), :], (16, LANE)))
    ref = jnp.concatenate(pieces, axis=0)
    eq = jnp.exp(jnp.minimum(b - ref, 80.0))
    ek = jnp.exp(jnp.minimum(ref - b, 80.0))
    out.append((eq, ek, (q * eq).astype(BF16), (k * ek).astype(BF16)))
    return out


def _hg_gate(zf, l0, l1, valid):
    mx = jnp.maximum(l0, l1)
    e0, e1 = jnp.exp(l0 - mx), jnp.exp(l1 - mx)
    p0 = e0 / (e0 + e1)
    sg = _sigmoid(-zf)
    k = jnp.where(valid, (1.0 - p0) * sg, 0.0)
    return p0, sg, k, jnp.log(1.0 - k)


def _hg_fwd(z, lbp, *, rev, name):
    L = z.shape[0]
    nc = L // CHUNK
    fcol = 3 if rev else 2

    def cidx(j):
        return nc - 1 - j if rev else j

    def body(zq_ref, zi_ref, zf_ref, lb_ref, o_ref, ssave_ref, st_scr, b_scr):
        j = pl.program_id(0)

        @pl.when(j == 0)
        def _():
            st_scr[...] = jnp.zeros_like(st_scr)

        causal, lmasks, dmask = _hg_masks(rev)
        tri = jnp.where(causal, 1.0, 0.0).astype(BF16)
        rowg = cidx(j) * CHUNK + lax.broadcasted_iota(jnp.int32, (CHUNK, LANE), 0)
        valid = rowg >= PAD - N_META
        last = 0 if rev else CHUNK - 1
        for hh in range(HG_HEADS):
            sl = slice(LANE * hh, LANE * (hh + 1))
            zq = zq_ref[:, sl]
            q = zq * _sigmoid(zq)
            v = zi_ref[:, sl].astype(BF16)
            _, _, k, g = _hg_gate(zf_ref[:, sl], lb_ref[0:1, sl], lb_ref[1:2, sl], valid)
            b = _exact_left(tri, g)
            st = st_scr[hh]
            ssave_ref[0, hh] = st
            o = _nt((q * jnp.exp(b)).astype(BF16), st.astype(BF16))
            a = None
            fac = _hg_intra_factors(q, k, b, b_scr, rev)
            for (eq, ek, qq, kk), msk in zip(fac, lmasks + [dmask]):
                t = jnp.where(msk, _nt(qq, kk), 0.0)
                a = t if a is None else a + t
            o_ref[:, sl] = o + _nn(a.astype(BF16), v)
            bl = b_scr[pl.ds(last, 1), :]
            kd = (k * jnp.exp(bl - b)).astype(BF16)
            st_scr[hh] = st * jnp.exp(bl) + _tn(v, kd)

    zspec = lambda col: pl.BlockSpec((CHUNK, HG_W), lambda j: (cidx(j), col))
    return pl.pallas_call(
        body, grid=(nc,),
        in_specs=[zspec(0), zspec(1), zspec(fcol), pl.BlockSpec((2, HG_W), lambda j: (0, 0))],
        out_specs=[pl.BlockSpec((CHUNK, HG_W), lambda j: (cidx(j), 0)),
                   pl.BlockSpec((1, HG_HEADS, LANE, LANE), lambda j: (cidx(j), 0, 0, 0))],
        out_shape=[SDS((L, HG_W), F32), SDS((nc, HG_HEADS, LANE, LANE), F32)],
        scratch_shapes=[pltpu.VMEM((HG_HEADS, LANE, LANE), F32), pltpu.VMEM((CHUNK, LANE), F32)],
        compiler_params=_params(("arbitrary",)), name=name)(z, z, z, lbp)


def _hg_bwd(z, lbp, do, ssave, prev, *, rev, name):
    L = z.shape[0]
    nc = L // CHUNK
    fcol = 3 if rev else 2
    final = prev is not None

    def cidx(j):
        return j if rev else nc - 1 - j

    def body(*refs):
        zq_ref, zi_ref, zf_ref, lb_ref, do_ref, ss_ref = refs[:6]
        pos = 6
        if final:
            dqin_ref, dvin_ref = refs[6:8]
            pos = 8
        dq_ref, dv_ref, dzf_ref, dlb_ref, dst_scr, b_scr = refs[pos:pos + 6]
        j = pl.program_id(0)

        @pl.when(j == 0)
        def _():
            dst_scr[...] = jnp.zeros_like(dst_scr)
            dlb_ref[...] = jnp.zeros_like(dlb_ref)

        causal, lmasks, dmask = _hg_masks(rev)
        tri = jnp.where(causal, 1.0, 0.0).astype(BF16)
        ti = lax.broadcasted_iota(jnp.int32, (CHUNK, CHUNK), 0)
        si = lax.broadcasted_iota(jnp.int32, (CHUNK, CHUNK), 1)
        tri_t = jnp.where((si <= ti) if rev else (si >= ti), 1.0, 0.0).astype(BF16)
        rowg = cidx(j) * CHUNK + lax.broadcasted_iota(jnp.int32, (CHUNK, LANE), 0)
        valid = rowg >= PAD - N_META
        last = 0 if rev else CHUNK - 1
        for hh in range(HG_HEADS):
            sl = slice(LANE * hh, LANE * (hh + 1))
            zq = zq_ref[:, sl]
            sq = _sigmoid(zq)
            q = zq * sq
            v = zi_ref[:, sl].astype(BF16)
            p0, sg, k, g = _hg_gate(zf_ref[:, sl], lb_ref[0:1, sl], lb_ref[1:2, sl], valid)
            b = _exact_left(tri, g)
            dob = do_ref[:, sl].astype(BF16)
            st = ss_ref[0, hh]
            dst = dst_scr[hh]
            stb, dstb = st.astype(BF16), dst.astype(BF16)
            eb = jnp.exp(b)
            qe = (q * eb).astype(BF16)
            fac = _hg_intra_factors(q, k, b, b_scr, rev)
            bl = b_scr[pl.ds(last, 1), :]
            ebl = jnp.exp(bl)
            kde = jnp.exp(bl - b)
            kd = (k * kde).astype(BF16)
            da = jnp.where(causal, _nt(dob, v), 0.0)
            dq = eb * _nn(dob, stb)
            dk_inter = kde * _nn(v, dstb)
            dk = dk_inter
            dv = _nt(kd, dstb)
            a = None
            db = q * dq - k * dk
            for (eq, ek, qq, kk), msk in zip(fac, lmasks + [dmask]):
                t = jnp.where(msk, _nt(qq, kk), 0.0)
                a = t if a is None else a + t
                dal = jnp.where(msk, da, 0.0).astype(BF16)
                mq = _nn(dal, kk)
                mk = _tn(dal, qq)
                dq = dq + eq * mq
                dk = dk + ek * mk
                db = db + (qq.astype(F32) * mq - kk.astype(F32) * mk)
            dv = dv + _tn(a.astype(BF16), dob)
            extra = ebl * jnp.sum(st * dst, axis=0, keepdims=True) + jnp.sum(k * dk_inter, axis=0, keepdims=True)
            dst_scr[hh] = dst * ebl + _tn(dob, qe)
            dg = _exact_left(tri_t, db) + extra
            dk_tot = dk - dg / (1.0 - k)
            dzf_ref[:, sl] = jnp.where(valid, dk_tot * (1.0 - p0) * (-sg * (1.0 - sg)), 0.0).astype(BF16)
            dlb_ref[:, sl] += jnp.sum(jnp.where(valid, -sg * dk_tot, 0.0), axis=0, keepdims=True)
            if final:
                dq_ref[:, sl] = ((dq + dqin_ref[:, sl]) * (sq * (1.0 + zq * (1.0 - sq)))).astype(BF16)
                dv_ref[:, sl] = (dv + dvin_ref[:, sl]).astype(BF16)
            else:
                dq_ref[:, sl] = dq
                dv_ref[:, sl] = dv

    zspec = lambda col: pl.BlockSpec((CHUNK, HG_W), lambda j: (cidx(j), col))
    rspec = pl.BlockSpec((CHUNK, HG_W), lambda j: (cidx(j), 0))
    in_specs = [zspec(0), zspec(1), zspec(fcol), pl.BlockSpec((2, HG_W), lambda j: (0, 0)), rspec,
                pl.BlockSpec((1, HG_HEADS, LANE, LANE), lambda j: (cidx(j), 0, 0, 0))]
    args = [z, z, z, lbp, do, ssave]
    if final:
        in_specs += [rspec, rspec]
        args += list(prev)
    odt = BF16 if final else F32
    return pl.pallas_call(
        body, grid=(nc,), in_specs=in_specs,
        out_specs=[rspec, rspec, rspec, pl.BlockSpec((1, HG_W), lambda j: (0, 0))],
        out_shape=[SDS((L, HG_W), odt), SDS((L, HG_W), odt), SDS((L, HG_W), BF16), SDS((1, HG_W), F32)],
        scratch_shapes=[pltpu.VMEM((HG_HEADS, LANE, LANE), F32), pltpu.VMEM((CHUNK, LANE), F32)],
        compiler_params=_params(("arbitrary",)), name=name)(*args)


def _hg_post_fwd(of, ob, z, w, *, name):
    L = of.shape[0]
    tm = _tile(L, 512)

    def body(of_ref, ob_ref, zg_ref, w_ref, y_ref):
        for hh in range(HG_HEADS):
            sl = slice(LANE * hh, LANE * (hh + 1))
            o = of_ref[:, sl] + ob_ref[:, sl]
            r = lax.rsqrt(jnp.mean(o * o, axis=-1, keepdims=True) + EPS)
            zg = zg_ref[:, sl]
            y_ref[:, sl] = (o * r * w_ref[:, sl] * (zg * _sigmoid(zg))).astype(BF16)

    row = pl.BlockSpec((tm, HG_W), lambda i: (i, 0))
    return pl.pallas_call(
        body, grid=(L // tm,),
        in_specs=[row, row, pl.BlockSpec((tm, HG_W), lambda i: (i, 4)), pl.BlockSpec((1, HG_W), lambda i: (0, 0))],
        out_specs=row, out_shape=SDS((L, HG_W), BF16),
        compiler_params=_params(("parallel",)), name=name)(of, ob, z, w)


def _hg_post_bwd(dy, of, ob, z, w, *, name):
    L = of.shape[0]
    tm = _tile(L, 512)

    def body(dy_ref, of_ref, ob_ref, zg_ref, w_ref, do_ref, dzg_ref, dw_ref):
        @pl.when(pl.program_id(0) == 0)
        def _():
            dw_ref[...] = jnp.zeros_like(dw_ref)

        for hh in range(HG_HEADS):
            sl = slice(LANE * hh, LANE * (hh + 1))
            o = of_ref[:, sl] + ob_ref[:, sl]
            r = lax.rsqrt(jnp.mean(o * o, axis=-1, keepdims=True) + EPS)
            xh = o * r
            zg = zg_ref[:, sl]
            sg = _sigmoid(zg)
            w = w_ref[:, sl]
            dy = dy_ref[:, sl]
            dys = dy * (zg * sg)
            dzg_ref[:, sl] = (dy * xh * w * (sg * (1.0 + zg * (1.0 - sg)))).astype(BF16)
            dw_ref[:, sl] += jnp.sum(dys * xh, axis=0, keepdims=True)
            dxh = dys * w
            do_ref[:, sl] = r * (dxh - xh * jnp.mean(dxh * xh, axis=-1, keepdims=True))

    row = pl.BlockSpec((tm, HG_W), lambda i: (i, 0))
    vec = pl.BlockSpec((1, HG_W), lambda i: (0, 0))
    return pl.pallas_call(
        body, grid=(L // tm,),
        in_specs=[row, row, row, pl.BlockSpec((tm, HG_W), lambda i: (i, 4)), vec],
        out_specs=[row, row, vec],
        out_shape=[SDS((L, HG_W), F32), SDS((L, HG_W), BF16), SDS((1, HG_W), F32)],
        compiler_params=_params(("arbitrary",)), name=name)(dy, of, ob, z, w)


N_GROUPS = (AT_HEADS + AT_KV) // 2


def _qk_to_group(wqk):
    d = wqk.shape[0]
    return wqk.reshape(d, N_GROUPS, 2, AT_HD // 2, 2).transpose(0, 1, 4, 2, 3).reshape(d, N_GROUPS * LANE)


def _qk_from_group(wqk):
    d = wqk.shape[0]
    return wqk.reshape(d, N_GROUPS, 2, 2, AT_HD // 2).transpose(0, 1, 3, 4, 2).reshape(d, N_GROUPS * LANE)


def _group_vec(w64):
    halves = w64.reshape(AT_HD // 2, 2).T
    return jnp.broadcast_to(halves[:, None, :], (2, 2, AT_HD // 2)).reshape(1, LANE)


def _ungroup_vec(w128):
    w = w128.reshape(2, 2, 32).sum(axis=1)
    return w.T.reshape(1, AT_HD)


def _rope_tables(L):
    n_real = L - PAD
    t = np.arange(n_real)
    row = np.concatenate([np.zeros(PAD), t // GRID_W]).astype(np.float32)
    col = np.concatenate([np.zeros(PAD), t % GRID_W]).astype(np.float32)
    inv = jnp.asarray(ROPE_THETA, F32) ** (-jnp.arange(0, AT_HD // 2, 2, dtype=F32) / (AT_HD // 2))
    ang = jnp.concatenate([jnp.asarray(row)[:, None] * inv, jnp.asarray(col)[:, None] * inv], axis=-1)
    cos, sin = jnp.cos(ang), jnp.sin(ang)
    cc = jnp.tile(cos, (1, 4))
    ss = jnp.concatenate([-sin, -sin, sin, sin], axis=1)
    return cc, ss


def _seg_matrix():
    a = lax.broadcasted_iota(jnp.int32, (LANE, LANE), 0)
    b = lax.broadcasted_iota(jnp.int32, (LANE, LANE), 1)
    same = jnp.bitwise_and(jnp.right_shift(a, 5), 1) == jnp.bitwise_and(jnp.right_shift(b, 5), 1)
    return jnp.where(same, 1.0, 0.0).astype(BF16)


def _slot_mask(shape, hp):
    lane = lax.broadcasted_iota(jnp.int32, shape, 1)
    return jnp.bitwise_and(jnp.right_shift(lane, 5), 1) == hp


def _at_prep(z, cc, ss, wq, wk, *, name):
    L = z.shape[0]
    tm = PAD
    qcol = Z_HG // AT_W
    kvcol = (Z_HG + AT_W) // (2 * LANE)

    def body(zq_ref, zkv_ref, cc_ref, ss_ref, wq_ref, wk_ref, qt_ref, kr_ref, krt_ref, vb_ref, vt_ref):
        seg = _seg_matrix()
        cc, ss = cc_ref[...], ss_ref[...]

        def normrope(x, w):
            r = lax.rsqrt(_exact_right(x * x, seg) * (1.0 / AT_HD) + EPS)
            y = x * r * w
            return y * cc + pltpu.roll(y, 64, 1) * ss

        for g in range(AT_HEADS // 2):
            o = normrope(zq_ref[:, LANE * g:LANE * (g + 1)], wq_ref[...]) * (AT_HD ** -0.5)
            for hp in range(2):
                h = 2 * g + hp
                tgt = h // (AT_HEADS // AT_KV)
                xm = jnp.where(_slot_mask(o.shape, hp), o, 0.0)
                if tgt != hp:
                    xm = pltpu.roll(xm, 32 if tgt == 1 else 96, 1)
                qt_ref[h] = xm.T.astype(BF16)
        kr = normrope(zkv_ref[:, :LANE], wk_ref[...])
        kr_ref[...] = kr.astype(BF16)
        krt_ref[0] = kr.T.astype(BF16)
        v = zkv_ref[:, LANE:]
        low = lax.broadcasted_iota(jnp.int32, v.shape, 1) < AT_HD
        vb_ref[0] = jnp.where(low, v, 0.0).astype(BF16)
        vb_ref[1] = jnp.where(low, pltpu.roll(v, AT_HD, 1), 0.0).astype(BF16)
        vt = v.T.astype(BF16)
        ones = jnp.ones((VT_ROWS - AT_HD, tm), BF16)
        for j in range(AT_KV):
            vt_ref[j, 0, 0:AT_HD, :] = vt[AT_HD * j:AT_HD * (j + 1)]
            vt_ref[j, 0, AT_HD:VT_ROWS, :] = ones

    tab = pl.BlockSpec((tm, LANE), lambda i: (i, 0))
    vec = pl.BlockSpec((1, LANE), lambda i: (0, 0))
    nt = L // tm
    return pl.pallas_call(
        body, grid=(nt,),
        in_specs=[pl.BlockSpec((tm, AT_W), lambda i: (i, qcol)), pl.BlockSpec((tm, 2 * LANE), lambda i: (i, kvcol)),
                  tab, tab, vec, vec],
        out_specs=[pl.BlockSpec((AT_HEADS, LANE, tm), lambda i: (0, 0, i)), tab,
                   pl.BlockSpec((1, LANE, tm), lambda i: (i, 0, 0)),
                   pl.BlockSpec((AT_KV, tm, LANE), lambda i: (0, i, 0)),
                   pl.BlockSpec((AT_KV, 1, VT_ROWS, tm), lambda i: (0, i, 0, 0))],
        out_shape=[SDS((AT_HEADS, LANE, L), BF16), SDS((L, LANE), BF16), SDS((nt, LANE, tm), BF16),
                   SDS((AT_KV, L, LANE), BF16), SDS((AT_KV, nt, VT_ROWS, tm), BF16)],
        compiler_params=_params(("parallel",)), name=name)(z, z, cc, ss, wq, wk)


def _at_prep_bwd(dqm, dk2, dv2, z, cc, ss, wq, wk, *, name):
    L = z.shape[0]
    tm = PAD
    qcol = Z_HG // AT_W
    kvcol = (Z_HG + AT_W) // (2 * LANE)

    def body(dqm_ref, dk2_ref, dv2_ref, zq_ref, zkv_ref, cc_ref, ss_ref, wq_ref, wk_ref, dz_ref, dwq_ref, dwk_ref):
        @pl.when(pl.program_id(0) == 0)
        def _():
            dwq_ref[...] = jnp.zeros_like(dwq_ref)
            dwk_ref[...] = jnp.zeros_like(dwk_ref)

        seg = _seg_matrix()
        cc, ss = cc_ref[...], ss_ref[...]

        def back(x, w, do):
            dy = do * cc + pltpu.roll(do * ss, 64, 1)
            r = lax.rsqrt(_exact_right(x * x, seg) * (1.0 / AT_HD) + EPS)
            xh = x * r
            dxh = dy * w
            dx = r * (dxh - xh * (_exact_right(dxh * xh, seg) * (1.0 / AT_HD)))
            return dx, jnp.sum(dy * xh, axis=0, keepdims=True)

        for g in range(AT_HEADS // 2):
            do = None
            for hp in range(2):
                h = 2 * g + hp
                tgt = h // (AT_HEADS // AT_KV)
                d = jnp.where(_slot_mask((tm, LANE), tgt), dqm_ref[h], 0.0)
                if tgt != hp:
                    d = pltpu.roll(d, 96 if tgt == 1 else 32, 1)
                do = d if do is None else do + d
            dx, dw = back(zq_ref[:, LANE * g:LANE * (g + 1)], wq_ref[...], do * (AT_HD ** -0.5))
            dz_ref[:, LANE * g:LANE * (g + 1)] = dx.astype(BF16)
            dwq_ref[...] += dw
        dx, dw = back(zkv_ref[:, :LANE], wk_ref[...], (dk2_ref[0, 0] + dk2_ref[1, 0]).T)
        dz_ref[:, AT_W:AT_W + LANE] = dx.astype(BF16)
        dwk_ref[...] += dw
        dz_ref[:, AT_W + LANE:] = jnp.concatenate([dv2_ref[0, 0], dv2_ref[1, 0]], axis=0).T.astype(BF16)

    tab = pl.BlockSpec((tm, LANE), lambda i: (i, 0))
    vec = pl.BlockSpec((1, LANE), lambda i: (0, 0))
    return pl.pallas_call(
        body, grid=(L // tm,),
        in_specs=[pl.BlockSpec((AT_HEADS, tm, LANE), lambda i: (0, i, 0)),
                  pl.BlockSpec((AT_KV, 1, LANE, tm), lambda i: (0, i, 0, 0)),
                  pl.BlockSpec((AT_KV, 1, AT_HD, tm), lambda i: (0, i, 0, 0)),
                  pl.BlockSpec((tm, AT_W), lambda i: (i, qcol)), pl.BlockSpec((tm, 2 * LANE), lambda i: (i, kvcol)),
                  tab, tab, vec, vec],
        out_specs=[pl.BlockSpec((tm, Z_AT), lambda i: (i, 0)), vec, vec],
        out_shape=[SDS((L, Z_AT), BF16), SDS((1, LANE), F32), SDS((1, LANE), F32)],
        compiler_params=_params(("arbitrary",)), name=name)(dqm, dk2, dv2, z, z, cc, ss, wq, wk)


def _at_fwd(qt, kr, vt, *, name):
    L = kr.shape[0]
    G = AT_HEADS // AT_KV
    tq = _tile(L, 384)
    tk = PAD
    nk = L // tk
    R = G * tq
    sb = R

    def body(q_ref, k_ref, v_ref, ob_ref, of_ref, lse_ref, m_scr, acc_scr):
        i = pl.program_id(1)
        qt = jnp.concatenate([q_ref[g] for g in range(G)], axis=1)
        m_scr[...] = jnp.full_like(m_scr, NEG)
        acc_scr[...] = jnp.zeros_like(acc_scr)

        def chunk(c, masked):
            start = c * tk if isinstance(c, int) else pl.multiple_of(c * tk, tk)
            k = k_ref[pl.ds(start, tk), :]
            vt = v_ref[0, c]
            for b in range(R // sb):
                sl = slice(sb * b, sb * (b + 1))
                st = _nn(k, qt[:, sl])
                if masked:
                    key = lax.broadcasted_iota(jnp.int32, st.shape, 0)
                    st = jnp.where(key >= PAD - N_META, st, NEG)
                m_prev = m_scr[:, sl]
                m_new = jnp.maximum(m_prev, jnp.max(st, axis=0, keepdims=True))
                pt = jnp.exp(st - m_new).astype(BF16)
                acc_scr[:, sl] = jnp.exp(m_prev - m_new) * acc_scr[:, sl] + _nn(vt, pt)
                m_scr[:, sl] = m_new

        chunk(0, True)

        def loop(c, carry):
            chunk(c, False)
            return carry

        lax.fori_loop(1, nk, loop, 0)
        l = acc_scr[pl.ds(AT_HD, 1), :]
        lse = m_scr[...] + jnp.log(l)
        on = acc_scr[0:AT_HD, :] / l
        o = jnp.concatenate([on[:, g * tq:(g + 1) * tq] for g in range(G)], axis=0).T
        rowg = i * tq + lax.broadcasted_iota(jnp.int32, o.shape, 0)
        o = jnp.where(rowg >= PAD - N_META, o, 0.0)
        ob_ref[...] = o.astype(BF16)
        of_ref[...] = o
        for g in range(G):
            lse_ref[g] = lse[:, g * tq:(g + 1) * tq]

    ospec = pl.BlockSpec((tq, G * AT_HD), lambda j, i: (i, j))
    return pl.pallas_call(
        body, grid=(AT_KV, L // tq),
        in_specs=[pl.BlockSpec((G, LANE, tq), lambda j, i: (j, 0, i)), pl.BlockSpec((L, LANE), lambda j, i: (0, 0)),
                  pl.BlockSpec((1, nk, VT_ROWS, tk), lambda j, i: (j, 0, 0, 0))],
        out_specs=[ospec, ospec, pl.BlockSpec((G, 1, tq), lambda j, i: (j, 0, i))],
        out_shape=[SDS((L, AT_W), BF16), SDS((L, AT_W), F32), SDS((AT_HEADS, 1, L), F32)],
        scratch_shapes=[pltpu.VMEM((1, R), F32), pltpu.VMEM((VT_ROWS, R), F32)],
        compiler_params=_params(("parallel", "parallel")), name=name)(qt, kr, vt)


def _at_bwd(qt, kr, krt, vb, do, of, lse, *, name):
    L = kr.shape[0]
    G = AT_HEADS // AT_KV
    tq = _tile(L, 256)
    tk = PAD
    nk = L // tk
    nq = L // tq
    R = G * tq
    sb = R

    def body(q_ref, k_hbm, kt_hbm, v_hbm, do_ref, o_ref, lse_ref, dq_ref, dk_hbm, dv_hbm,
             k_scr, kt_scr, v_scr, dk_scr, dv_scr, dq_scr, sem):
        j, i = pl.program_id(0), pl.program_id(1)

        @pl.when(i == 0)
        def _():
            cps = [pltpu.make_async_copy(k_hbm, k_scr, sem.at[0]), pltpu.make_async_copy(kt_hbm, kt_scr, sem.at[1]),
                   pltpu.make_async_copy(v_hbm.at[j], v_scr, sem.at[2])]
            for cp in cps:
                cp.start()
            dk_scr[...] = jnp.zeros_like(dk_scr)
            dv_scr[...] = jnp.zeros_like(dv_scr)
            for cp in cps:
                cp.wait()

        qt = jnp.concatenate([q_ref[g] for g in range(G)], axis=1)
        rowg = i * tq + lax.broadcasted_iota(jnp.int32, (tq, G * AT_HD), 0)
        dot_all = jnp.where(rowg >= PAD - N_META, do_ref[...], 0.0).T
        ot_all = o_ref[...].T
        dot = jnp.concatenate([dot_all[AT_HD * g:AT_HD * (g + 1)] for g in range(G)], axis=1)
        ot = jnp.concatenate([ot_all[AT_HD * g:AT_HD * (g + 1)] for g in range(G)], axis=1)
        delta = jnp.sum(dot * ot, axis=0, keepdims=True)
        dotb = dot.astype(BF16)
        dot128 = jnp.concatenate([dotb, jnp.zeros_like(dotb)], axis=0)
        lse_v = jnp.concatenate([lse_ref[g] for g in range(G)], axis=1)
        dq_scr[...] = jnp.zeros_like(dq_scr)

        def chunk(c, masked):
            start = c * tk if isinstance(c, int) else pl.multiple_of(c * tk, tk)
            k = k_scr[pl.ds(start, tk), :]
            kt = kt_scr[c]
            v = v_scr[pl.ds(start, tk), :]
            dkt, dvt = None, None
            for b in range(R // sb):
                sl = slice(sb * b, sb * (b + 1))
                st = _nn(k, qt[:, sl])
                if masked:
                    key = lax.broadcasted_iota(jnp.int32, st.shape, 0)
                    st = jnp.where(key >= PAD - N_META, st, NEG)
                pt = jnp.exp(st - lse_v[:, sl])
                dst = (pt * (_nn(v, dot128[:, sl]) - delta[:, sl])).astype(BF16)
                dq_scr[:, sl] += _nn(kt, dst)
                a = _nt(qt[:, sl], dst)
                e = _nt(dotb[:, sl], pt.astype(BF16))
                dkt = a if dkt is None else dkt + a
                dvt = e if dvt is None else dvt + e
            dk_scr[c] += dkt
            dv_scr[c] += dvt

        chunk(0, True)

        def loop(c, carry):
            chunk(c, False)
            return carry

        lax.fori_loop(1, nk, loop, 0)
        dq_ref[...] = dq_scr[...].T.reshape(G, tq, LANE)

        @pl.when(i == nq - 1)
        def _():
            ck = pltpu.make_async_copy(dk_scr, dk_hbm.at[j], sem.at[0])
            cv = pltpu.make_async_copy(dv_scr, dv_hbm.at[j], sem.at[1])
            ck.start()
            cv.start()
            ck.wait()
            cv.wait()

    anyspec = pl.BlockSpec(memory_space=pl.ANY)
    ospec = pl.BlockSpec((tq, G * AT_HD), lambda j, i: (i, j))
    return pl.pallas_call(
        body, grid=(AT_KV, nq),
        in_specs=[pl.BlockSpec((G, LANE, tq), lambda j, i: (j, 0, i)), anyspec, anyspec, anyspec, ospec, ospec,
                  pl.BlockSpec((G, 1, tq), lambda j, i: (j, 0, i))],
        out_specs=[pl.BlockSpec((G, tq, LANE), lambda j, i: (j, i, 0)), anyspec, anyspec],
        out_shape=[SDS((AT_HEADS, L, LANE), F32), SDS((AT_KV, nk, LANE, tk), F32), SDS((AT_KV, nk, AT_HD, tk), F32)],
        scratch_shapes=[pltpu.VMEM((L, LANE), BF16), pltpu.VMEM((nk, LANE, tk), BF16), pltpu.VMEM((L, LANE), BF16),
                        pltpu.VMEM((nk, LANE, tk), F32), pltpu.VMEM((nk, AT_HD, tk), F32), pltpu.VMEM((LANE, R), F32),
                        pltpu.SemaphoreType.DMA((3,))],
        compiler_params=_params(("arbitrary", "arbitrary"), VMEM_LARGE), name=name)(qt, kr, krt, vb, do, of, lse)


def _merge_fwd(ya, o8, wua, wubp, z, *, name):
    L = ya.shape[0]
    D = wua.shape[1]
    tm, tn = _tile(L, 512), 256
    ga, gb = (Z_HG + Z_AT) // tn, (Z_HG + Z_AT + D) // tn

    def body(ya_ref, o8_ref, wa_ref, wb_ref, za_ref, zb_ref, mix_ref):
        pa = _nn(ya_ref[...], wa_ref[...])
        pb = _nn(o8_ref[...], wb_ref[...])
        mix_ref[...] = (_sigmoid(za_ref[...]) * pa + _sigmoid(zb_ref[...]) * pb).astype(BF16)

    return pl.pallas_call(
        body, grid=(D // tn, L // tm),
        in_specs=[pl.BlockSpec((tm, ya.shape[1]), lambda j, i: (i, 0)), pl.BlockSpec((tm, o8.shape[1]), lambda j, i: (i, 0)),
                  pl.BlockSpec((wua.shape[0], tn), lambda j, i: (0, j)), pl.BlockSpec((wubp.shape[0], tn), lambda j, i: (0, j)),
                  pl.BlockSpec((tm, tn), lambda j, i: (i, ga + j)), pl.BlockSpec((tm, tn), lambda j, i: (i, gb + j))],
        out_specs=pl.BlockSpec((tm, tn), lambda j, i: (i, j)), out_shape=SDS((L, D), BF16),
        compiler_params=_params(("parallel", "parallel")), name=name)(ya, o8, wua, wubp, z, z)


def _merge_bwd(dh, wout, ya, o8, wua, wubp, z, *, name):
    L = ya.shape[0]
    D = wua.shape[1]
    tm, tn = _tile(L, 512), 256
    ga, gb = (Z_HG + Z_AT) // tn, (Z_HG + Z_AT + D) // tn

    def body(dh_ref, wo_ref, ya_ref, o8_ref, wa_ref, wb_ref, za_ref, zb_ref, dpa_ref, dpb_ref, dza_ref, dzb_ref):
        dm = _nt(dh_ref[...].astype(BF16), wo_ref[...])
        pa = _nn(ya_ref[...], wa_ref[...])
        pb = _nn(o8_ref[...], wb_ref[...])
        sa, sb = _sigmoid(za_ref[...]), _sigmoid(zb_ref[...])
        dpa_ref[...] = (dm * sa).astype(BF16)
        dpb_ref[...] = (dm * sb).astype(BF16)
        dza_ref[...] = (dm * pa * sa * (1.0 - sa)).astype(BF16)
        dzb_ref[...] = (dm * pb * sb * (1.0 - sb)).astype(BF16)

    ospec = pl.BlockSpec((tm, tn), lambda j, i: (i, j))
    return pl.pallas_call(
        body, grid=(D // tn, L // tm),
        in_specs=[pl.BlockSpec((tm, D), lambda j, i: (i, 0)), pl.BlockSpec((tn, D), lambda j, i: (j, 0)),
                  pl.BlockSpec((tm, ya.shape[1]), lambda j, i: (i, 0)), pl.BlockSpec((tm, o8.shape[1]), lambda j, i: (i, 0)),
                  pl.BlockSpec((wua.shape[0], tn), lambda j, i: (0, j)), pl.BlockSpec((wubp.shape[0], tn), lambda j, i: (0, j)),
                  pl.BlockSpec((tm, tn), lambda j, i: (i, ga + j)), pl.BlockSpec((tm, tn), lambda j, i: (i, gb + j))],
        out_specs=[ospec] * 4, out_shape=[SDS((L, D), BF16)] * 4,
        compiler_params=_params(("parallel", "parallel")), name=name)(dh, wout, ya, o8, wua, wubp, z, z)


def _loss_head(h, tgt, *, name):
    L, D = h.shape
    tm = PAD

    def body(h_ref, t_ref, dh_ref, ls_ref):
        i = pl.program_id(0)

        @pl.when(i == 0)
        def _():
            ls_ref[...] = jnp.zeros_like(ls_ref)
            dh_ref[...] = jnp.zeros_like(dh_ref)

        @pl.when(i > 0)
        def _():
            e = h_ref[...] - t_ref[...]
            dh_ref[...] = e * (1.0 / D)
            s = jnp.sum(e * e, axis=0, keepdims=True)
            tot = s[:, :LANE]
            for c in range(1, D // LANE):
                tot = tot + s[:, LANE * c:LANE * (c + 1)]
            ls_ref[...] += tot

    return pl.pallas_call(
        body, grid=(L // tm,),
        in_specs=[pl.BlockSpec((tm, D), lambda i: (i, 0)), pl.BlockSpec((tm, D), lambda i: (jnp.maximum(i - 1, 0), 0))],
        out_specs=[pl.BlockSpec((tm, D), lambda i: (i, 0)), pl.BlockSpec((1, LANE), lambda i: (0, 0))],
        out_shape=[SDS((L, D), F32), SDS((1, LANE), F32)],
        compiler_params=_params(("arbitrary",)), name=name)(h, tgt)


def _adamw(w, g, m, v, *, name):
    shape = w.shape
    w2, g2, m2, v2 = [a.reshape(-1, shape[-1]) for a in (w, g, m, v)]
    rows, cols = w2.shape
    tr = _tile(rows, 256, 8)

    def body(w_ref, g_ref, m_ref, v_ref, d_ref, nm_ref, nv_ref):
        g = g_ref[...]
        m = ADAM_B1 * m_ref[...] + (1.0 - ADAM_B1) * g
        v = ADAM_B2 * v_ref[...] + (1.0 - ADAM_B2) * (g * g)
        m_hat = m / (1.0 - ADAM_B1 ** ADAM_STEP)
        v_hat = v / (1.0 - ADAM_B2 ** ADAM_STEP)
        d_ref[...] = -ADAM_LR * (m_hat / (jnp.sqrt(v_hat) + ADAM_EPS) + ADAM_WD * w_ref[...])
        nm_ref[...] = m
        nv_ref[...] = v

    spec = pl.BlockSpec((tr, cols), lambda i: (i, 0))
    outs = pl.pallas_call(
        body, grid=(rows // tr,), in_specs=[spec] * 4, out_specs=[spec] * 3, out_shape=[SDS((rows, cols), F32)] * 3,
        compiler_params=_params(("parallel",)), name=name)(w2, g2, m2, v2)
    return [o.reshape(shape) for o in outs]


def _sum_slabs(x, *, name):
    n, R, _ = x.shape
    tr = _tile(R, 2048, 8)

    def body(x_ref, o_ref):
        tot = x_ref[0]
        for s in range(1, n):
            tot = tot + x_ref[s]
        o_ref[...] = tot

    return pl.pallas_call(
        body, grid=(R // tr,), in_specs=[pl.BlockSpec((n, tr, LANE), lambda i: (0, i, 0))],
        out_specs=pl.BlockSpec((tr, LANE), lambda i: (i, 0)), out_shape=SDS((R, LANE), F32),
        compiler_params=_params(("parallel",)), name=name)(x)


def _add_pair(a, b, *, name):
    n, R, _ = a.shape
    tr = _tile(R, 2048, 8)

    def body(a_ref, b_ref, o_ref):
        o_ref[...] = a_ref[...] + b_ref[...]

    spec = pl.BlockSpec((1, tr, LANE), lambda s, i: (s, i, 0))
    return pl.pallas_call(
        body, grid=(n, R // tr), in_specs=[spec, spec], out_specs=spec, out_shape=SDS(a.shape, F32),
        compiler_params=_params(("parallel", "parallel")), name=name)(a, b)


def _place():
    return lax.axis_index("x"), lax.axis_index("y"), lax.axis_index("c")


def _allgather_small(v, *, name):
    m_per, n = v.shape

    def body(x_ref, out_ref, send_sems, recv_sems, local_sem):
        x, y, c = _place()
        me, sibling = (x, y, c), (x, y, 1 - c)
        chips = [(1 - x, y), (x, 1 - y), (1 - x, 1 - y)]

        def rows(px, py, pc):
            return out_ref.at[pl.ds((4 * px + 2 * py + pc) * m_per, m_per), :]

        def copy(k, block, to, src=None):
            return pltpu.make_async_remote_copy(
                src_ref=rows(*block) if src is None else src, dst_ref=rows(*block),
                send_sem=send_sems.at[k], recv_sem=recv_sems.at[k], device_id=to, device_id_type=MESH)

        mine = pltpu.make_async_copy(x_ref, rows(*me), local_sem)
        mine.start()
        first = [copy(0, me, sibling, src=x_ref)]
        first += [copy(1 + j, me, (*chip, c), src=x_ref) for j, chip in enumerate(chips)]
        for cp in first:
            cp.start()
        passed = [copy(4 + j, (*chip, c), sibling) for j, chip in enumerate(chips)]
        for j, chip in enumerate(chips):
            copy(1 + j, (*chip, c), me).wait_recv()
            passed[j].start()
        copy(0, sibling, me).wait_recv()
        for j, chip in enumerate(chips):
            copy(4 + j, (*chip, 1 - c), me).wait_recv()
        for cp in first + passed:
            cp.wait_send()
        mine.wait()

    return pl.pallas_call(
        body, out_shape=SDS((8 * m_per, n), v.dtype),
        in_specs=[pl.BlockSpec(memory_space=pltpu.VMEM)], out_specs=pl.BlockSpec(memory_space=pltpu.VMEM),
        scratch_shapes=[pltpu.SemaphoreType.DMA((7,)), pltpu.SemaphoreType.DMA((7,)), pltpu.SemaphoreType.DMA],
        name=name)(v)


def _gather_weights(wp, *, name):
    _, R, _ = wp.shape

    def body(w_ref, out_ref, send_sems, recv_sems, local_sem):
        x, y, c = _place()
        sibling = (x, y, 1 - c)
        chips = [(1 - x, y), (x, 1 - y), (1 - x, 1 - y)]

        def slot(px, py, half):
            return out_ref.at[2 * px + py, half]

        def copy(k, src, dst, to):
            return pltpu.make_async_remote_copy(src_ref=src, dst_ref=dst, send_sem=send_sems.at[k],
                                                recv_sem=recv_sems.at[k], device_id=to, device_id_type=MESH)

        mine = pltpu.make_async_copy(w_ref, out_ref.at[2 * x + y], local_sem)
        mine.start()
        first = [copy(j, w_ref.at[c], slot(x, y, c), (*chip, c)) for j, chip in enumerate(chips)]
        for cp in first:
            cp.start()
        passed = [copy(3 + j, slot(*chip, c), slot(*chip, c), sibling) for j, chip in enumerate(chips)]
        for j, chip in enumerate(chips):
            copy(j, w_ref.at[c], slot(*chip, c), (*chip, c)).wait_recv()
            passed[j].start()
        for j, chip in enumerate(chips):
            copy(3 + j, slot(*chip, 1 - c), slot(*chip, 1 - c), sibling).wait_recv()
        for cp in first + passed:
            cp.wait_send()
        mine.wait()

    anyspec = pl.BlockSpec(memory_space=pl.ANY)
    return pl.pallas_call(
        body, out_shape=SDS((4, 2, R, LANE), wp.dtype), in_specs=[anyspec], out_specs=anyspec,
        scratch_shapes=[pltpu.SemaphoreType.DMA((6,)), pltpu.SemaphoreType.DMA((6,)), pltpu.SemaphoreType.DMA],
        name=name)(wp)


def _pair_exchange(g, *, name):
    _, n, R, _ = g.shape

    def body(g_ref, out_ref, send_sem, recv_sem):
        x, y, c = _place()
        cp = pltpu.make_async_remote_copy(src_ref=g_ref.at[1 - c], dst_ref=out_ref, send_sem=send_sem,
                                          recv_sem=recv_sem, device_id=(x, y, 1 - c), device_id_type=MESH)
        cp.start()
        cp.wait()

    anyspec = pl.BlockSpec(memory_space=pl.ANY)
    return pl.pallas_call(
        body, out_shape=SDS((n, R, LANE), g.dtype), in_specs=[anyspec], out_specs=anyspec,
        scratch_shapes=[pltpu.SemaphoreType.DMA, pltpu.SemaphoreType.DMA], name=name)(g)


def _chip_exchange(part, *, name):
    n, R, _ = part.shape

    def body(p_ref, out_ref, send_sems, recv_sems, local_sem):
        x, y, c = _place()
        s_me = 2 * x + y
        chips = [(1 - x, y), (x, 1 - y), (1 - x, 1 - y)]

        def copy(k, chip):
            return pltpu.make_async_remote_copy(
                src_ref=p_ref.at[2 * chip[0] + chip[1]], dst_ref=out_ref.at[s_me], send_sem=send_sems.at[k],
                recv_sem=recv_sems.at[k], device_id=(*chip, c), device_id_type=MESH)

        def landed(k, chip):
            return pltpu.make_async_remote_copy(
                src_ref=p_ref.at[s_me], dst_ref=out_ref.at[2 * chip[0] + chip[1]], send_sem=send_sems.at[k],
                recv_sem=recv_sems.at[k], device_id=(*chip, c), device_id_type=MESH)

        mine = pltpu.make_async_copy(p_ref.at[s_me], out_ref.at[s_me], local_sem)
        mine.start()
        sends = [copy(k, chip) for k, chip in enumerate(chips)]
        for cp in sends:
            cp.start()
        for k, chip in enumerate(chips):
            landed(k, chip).wait_recv()
        for cp in sends:
            cp.wait_send()
        mine.wait()

    anyspec = pl.BlockSpec(memory_space=pl.ANY)
    return pl.pallas_call(
        body, out_shape=SDS((n, R, LANE), part.dtype), in_specs=[anyspec], out_specs=anyspec,
        scratch_shapes=[pltpu.SemaphoreType.DMA((3,)), pltpu.SemaphoreType.DMA((3,)), pltpu.SemaphoreType.DMA],
        name=name)(part)


def _pair_share(red, *, name):
    R, _ = red.shape

    def body(r_ref, out_ref, send_sem, recv_sem, local_sem):
        x, y, c = _place()
        mine = pltpu.make_async_copy(r_ref, out_ref.at[c], local_sem)
        mine.start()
        cp = pltpu.make_async_remote_copy(src_ref=r_ref, dst_ref=out_ref.at[c], send_sem=send_sem,
                                          recv_sem=recv_sem, device_id=(x, y, 1 - c), device_id_type=MESH)
        cp.start()
        pltpu.make_async_remote_copy(src_ref=r_ref, dst_ref=out_ref.at[1 - c], send_sem=send_sem,
                                     recv_sem=recv_sem, device_id=(x, y, 1 - c), device_id_type=MESH).wait_recv()
        cp.wait_send()
        mine.wait()

    anyspec = pl.BlockSpec(memory_space=pl.ANY)
    return pl.pallas_call(
        body, out_shape=SDS((2, R, LANE), red.dtype), in_specs=[anyspec], out_specs=anyspec,
        scratch_shapes=[pltpu.SemaphoreType.DMA, pltpu.SemaphoreType.DMA, pltpu.SemaphoreType.DMA], name=name)(red)


def _chips(x, y):
    return [(1 - x, y), (x, 1 - y), (1 - x, 1 - y)]


def _gather_mats(shards, *, name):
    n = len(shards)

    def body(*refs):
        ins, outs = refs[:n], refs[n:2 * n]
        send_sems, recv_sems, fsend_sems, frecv_sems, local_sems = refs[2 * n:]
        x, y, c = _place()
        s_me, sibling, chips = 2 * x + y, (x, y, 1 - c), _chips(x, y)

        def copy(src, dst, ssem, rsem, to):
            return pltpu.make_async_remote_copy(src_ref=src, dst_ref=dst, send_sem=ssem, recv_sem=rsem,
                                                device_id=to, device_id_type=MESH)

        mine = [pltpu.make_async_copy(ins[t], outs[t].at[s_me], local_sems.at[t]) for t in range(n)]
        for cp in mine:
            cp.start()
        first = [copy(ins[t].at[c], outs[t].at[s_me, c], send_sems.at[3 * t + k], recv_sems.at[3 * t + k], (*chip, c))
                 for t in range(n) for k, chip in enumerate(chips)]
        for cp in first:
            cp.start()
        passed = []
        for t in range(n):
            for k, chip in enumerate(chips):
                slot = outs[t].at[2 * chip[0] + chip[1], c]
                copy(ins[t].at[c], slot, send_sems.at[3 * t + k], recv_sems.at[3 * t + k], (*chip, c)).wait_recv()
                fw = copy(slot, slot, fsend_sems.at[3 * t + k], frecv_sems.at[3 * t + k], sibling)
                fw.start()
                passed.append(fw)
        for t in range(n):
            for k, chip in enumerate(chips):
                slot = outs[t].at[2 * chip[0] + chip[1], 1 - c]
                copy(slot, slot, fsend_sems.at[3 * t + k], frecv_sems.at[3 * t + k], sibling).wait_recv()
        for cp in first + passed:
            cp.wait_send()
        for cp in mine:
            cp.wait()

    anyspec = pl.BlockSpec(memory_space=pl.ANY)
    return pl.pallas_call(
        body, out_shape=[SDS((4,) + s.shape, s.dtype) for s in shards], in_specs=[anyspec] * n, out_specs=[anyspec] * n,
        scratch_shapes=[pltpu.SemaphoreType.DMA((3 * n,))] * 4 + [pltpu.SemaphoreType.DMA((n,))], name=name)(*shards)


def _rs_pair_exchange(gs, *, name):
    n = len(gs)

    def body(*refs):
        ins, outs = refs[:n], refs[n:2 * n]
        send_sems, recv_sems = refs[2 * n:]
        x, y, c = _place()
        cps = [pltpu.make_async_remote_copy(src_ref=ins[t].at[k, 1 - c], dst_ref=outs[t].at[k],
                                            send_sem=send_sems.at[4 * t + k], recv_sem=recv_sems.at[4 * t + k],
                                            device_id=(x, y, 1 - c), device_id_type=MESH)
               for t in range(n) for k in range(4)]
        for cp in cps:
            cp.start()
        for cp in cps:
            cp.wait()

    anyspec = pl.BlockSpec(memory_space=pl.ANY)
    return pl.pallas_call(
        body, out_shape=[SDS((4,) + g.shape[2:], g.dtype) for g in gs], in_specs=[anyspec] * n, out_specs=[anyspec] * n,
        scratch_shapes=[pltpu.SemaphoreType.DMA((4 * n,))] * 2, name=name)(*gs)


def _rs_chip_exchange(parts, *, name):
    n = len(parts)

    def body(*refs):
        ins, outs = refs[:n], refs[n:2 * n]
        send_sems, recv_sems, local_sems = refs[2 * n:]
        x, y, c = _place()
        s_me, chips = 2 * x + y, _chips(x, y)

        def copy(t, k, chip, src_slot, dst_slot):
            return pltpu.make_async_remote_copy(
                src_ref=ins[t].at[src_slot], dst_ref=outs[t].at[dst_slot], send_sem=send_sems.at[3 * t + k],
                recv_sem=recv_sems.at[3 * t + k], device_id=(*chip, c), device_id_type=MESH)

        mine = [pltpu.make_async_copy(ins[t].at[s_me], outs[t].at[s_me], local_sems.at[t]) for t in range(n)]
        for cp in mine:
            cp.start()
        sends = [copy(t, k, chip, 2 * chip[0] + chip[1], s_me) for t in range(n) for k, chip in enumerate(chips)]
        for cp in sends:
            cp.start()
        for t in range(n):
            for k, chip in enumerate(chips):
                copy(t, k, chip, s_me, 2 * chip[0] + chip[1]).wait_recv()
        for cp in sends:
            cp.wait_send()
        for cp in mine:
            cp.wait()

    anyspec = pl.BlockSpec(memory_space=pl.ANY)
    return pl.pallas_call(
        body, out_shape=[SDS(p.shape, p.dtype) for p in parts], in_specs=[anyspec] * n, out_specs=[anyspec] * n,
        scratch_shapes=[pltpu.SemaphoreType.DMA((3 * n,))] * 2 + [pltpu.SemaphoreType.DMA((n,))], name=name)(*parts)


def _rs_pair_share(reds, *, name):
    n = len(reds)

    def body(*refs):
        ins, outs = refs[:n], refs[n:2 * n]
        send_sems, recv_sems, local_sems = refs[2 * n:]
        x, y, c = _place()

        def copy(t, half):
            return pltpu.make_async_remote_copy(src_ref=ins[t], dst_ref=outs[t].at[half], send_sem=send_sems.at[t],
                                                recv_sem=recv_sems.at[t], device_id=(x, y, 1 - c), device_id_type=MESH)

        mine = [pltpu.make_async_copy(ins[t], outs[t].at[c], local_sems.at[t]) for t in range(n)]
        for cp in mine:
            cp.start()
        sends = [copy(t, c) for t in range(n)]
        for cp in sends:
            cp.start()
        for t in range(n):
            copy(t, 1 - c).wait_recv()
        for cp in sends:
            cp.wait_send()
        for cp in mine:
            cp.wait()

    anyspec = pl.BlockSpec(memory_space=pl.ANY)
    return pl.pallas_call(
        body, out_shape=[SDS((2,) + r.shape, r.dtype) for r in reds], in_specs=[anyspec] * n, out_specs=[anyspec] * n,
        scratch_shapes=[pltpu.SemaphoreType.DMA((n,))] * 3, name=name)(*reds)


def _add_half(g, other, c1, *, out_dtype, name):
    _, _, h, cs = g.shape
    tr = _tile(h, 512, 16)

    def body(c_ref, g_ref, o_ref, out_ref):
        out_ref[...] = (g_ref[...] + o_ref[...]).astype(out_dtype)

    spec = pl.BlockSpec((None, tr, cs), lambda k, i, c: (k, i, 0))
    return pl.pallas_call(
        body, out_shape=SDS(other.shape, out_dtype),
        grid_spec=pltpu.PrefetchScalarGridSpec(
            num_scalar_prefetch=1, grid=(4, h // tr),
            in_specs=[pl.BlockSpec((None, None, tr, cs), lambda k, i, c: (k, c[0], i, 0)), spec], out_specs=spec),
        compiler_params=_params(("parallel", "parallel")), name=name)(c1, g, other)


def _sum4(x, *, name):
    n, h, cs = x.shape
    tr = _tile(h, 512, 16)

    def body(x_ref, o_ref):
        tot = x_ref[0].astype(F32)
        for s in range(1, n):
            tot = tot + x_ref[s].astype(F32)
        o_ref[...] = tot

    return pl.pallas_call(
        body, grid=(h // tr,), in_specs=[pl.BlockSpec((n, tr, cs), lambda i: (0, i, 0))],
        out_specs=pl.BlockSpec((tr, cs), lambda i: (i, 0)), out_shape=SDS((h, cs), F32),
        compiler_params=_params(("parallel",)), name=name)(x)


def _finish_small(gathered, lbf, lbb, *, rows, name):
    r_lbf, r_lbb = rows['lb_f'], rows['lb_b']

    def body(g_ref, lbf_ref, lbb_ref, o_ref, dlf_ref, dlb_ref):
        tot = g_ref[0]
        for s in range(1, 8):
            tot = tot + g_ref[s]
        o_ref[...] = tot
        o_ref[0:1, :] = jnp.broadcast_to(jnp.sum(o_ref[0:1, :], axis=1, keepdims=True), (1, LANE))
        for lb_ref, d_ref, r0 in ((lbf_ref, dlf_ref, r_lbf), (lbb_ref, dlb_ref, r_lbb)):
            for hh in range(HG_HEADS):
                sl = slice(LANE * hh, LANE * (hh + 1))
                l0, l1 = lb_ref[0:1, sl], lb_ref[1:2, sl]
                mx = jnp.maximum(l0, l1)
                e0, e1 = jnp.exp(l0 - mx), jnp.exp(l1 - mx)
                p0 = e0 / (e0 + e1)
                d0 = o_ref[r0 + hh:r0 + hh + 1, :] * p0 * (1.0 - p0)
                d_ref[0:1, sl] = d0
                d_ref[1:2, sl] = -d0

    vm = pl.BlockSpec(memory_space=pltpu.VMEM)
    return pl.pallas_call(
        body, in_specs=[vm, vm, vm], out_specs=[vm, vm, vm],
        out_shape=[SDS(gathered.shape[1:], F32), SDS(lbf.shape, F32), SDS(lbb.shape, F32)], name=name)(gathered, lbf, lbb)


def _pad_len(n):
    q = 2 * 128 * LANE
    return (n + q - 1) // q * q


def _pack_local(shards, dtype):
    flat = jnp.concatenate([s.astype(dtype).reshape(-1) for s in shards])
    n = flat.shape[0]
    flat = jnp.pad(flat, (0, _pad_len(n) - n))
    return flat.reshape(2, -1, LANE)


def _unpack_full(gathered, shapes, names):
    out, off = {}, 0
    for name in names:
        r, cs = shapes[name]
        blk = gathered[:, off:off + r * cs].reshape(4, r, cs)
        off += r * cs
        out[name] = blk.reshape(4 * r, cs) if name in ROW_SHARDED else blk.transpose(1, 0, 2).reshape(r, 4 * cs)
    return out


def _pack_grads(grads, shapes, names):
    cols = []
    for name in names:
        r, cs = shapes[name]
        g = grads[name]
        blk = g.reshape(4, r * cs) if name in ROW_SHARDED else g.reshape(r, 4, cs).transpose(1, 0, 2).reshape(4, r * cs)
        cols.append(blk)
    flat = jnp.concatenate(cols, axis=1)
    n = flat.shape[1]
    flat = jnp.pad(flat, ((0, 0), (0, _pad_len(n) - n)))
    return flat.reshape(4, 2, -1, LANE).transpose(1, 0, 2, 3)


def _unpack_shard(flat, shapes, names):
    out, off = {}, 0
    for name in names:
        r, cs = shapes[name]
        out[name] = flat[off:off + r * cs].reshape(r, cs)
        off += r * cs
    return out


def _local_step(x2, tgt2, meta, W, S):
    T, D = x2.shape
    L = PAD + T
    h0 = jnp.concatenate([jnp.zeros((PAD - N_META, D), F32), meta, x2], axis=0)

    qk0 = Z_HG
    w_in = jnp.concatenate([W['w_in'][:, :qk0], _qk_to_group(W['w_in'][:, qk0:qk0 + AT_W + AT_KVW]),
                            W['w_in'][:, qk0 + AT_W + AT_KVW:]], axis=1)
    cc, ss = _rope_tables(L)
    wq_g, wk_g = _group_vec(S['q_norm']), _group_vec(S['k_norm'])

    def ffn_fwd(h, nw, wg, wu, wd, tag):
        n = _rmsnorm_fwd(h, nw, name=tag + "_norm")
        g, u, a = _ffn4_up(n, wg, wu, name=tag + "_up")
        hn = _ffn4_down(a, wd, h, name=tag + "_down")
        return hn, (n, g, u, a)

    def ffn_bwd(dh, h, nw, wg, wu, wd, saved, tag):
        n, g, u, a = saved
        dg, du = _ffn4_dact(dh, wd, g, u, name=tag + "_dact")
        dn = _ffn4_dn(dg, du, wg, wu, name=tag + "_dn")
        dwg = _ffn4_dw(n, dg, x_is_rows=True, name=tag + "_dwg")
        dwu = _ffn4_dw(n, du, x_is_rows=True, name=tag + "_dwu")
        dwd = _ffn4_dw(dh, a, x_is_rows=False, alpha=0.5, name=tag + "_dwd")
        dhp, dnw = _rmsnorm_bwd(h, nw, dn, dh, name=tag + "_norm_bwd")
        return dhp, dnw, dwg, dwu, dwd

    h1, sv1 = ffn_fwd(h0, S['ffn1_norm'], W['ffn1_w_gate'], W['ffn1_w_up'], W['ffn1_w_down'], "ffn1")
    um = _rmsnorm_fwd(h1, S['mix_norm'], name="mix_norm")
    z = _mm([(um, w_in)], tm=512, tn=1792, tk=D, name="in_proj")
    of, sf = _hg_fwd(z, S['hg_lb_fwd'], rev=False, name="hg_fwd_f")
    ob, sb = _hg_fwd(z, S['hg_lb_bwd'], rev=True, name="hg_fwd_b")
    ya = _hg_post_fwd(of, ob, z, S['hg_out_norm'], name="hg_post")
    qt, kr, krt, vb, vt = _at_prep(z, cc, ss, wq_g, wk_g, name="at_prep")
    yb, yb_f32, lse = _at_fwd(qt, kr, vt, name="at_fwd")
    mixed = _merge_fwd(ya, yb, W['w_up_a'], W['w_up_b'], z, name="merge")
    h2 = _mm([(mixed, W['w_out'])], res=h1, tm=512, tn=D, tk=D, name="out_proj")
    h3, sv2 = ffn_fwd(h2, S['ffn2_norm'], W['ffn2_w_gate'], W['ffn2_w_up'], W['ffn2_w_down'], "ffn2")
    dh3, loss_lanes = _loss_head(h3, tgt2, name="loss_head")

    G = {}
    dh2, dn_ffn2, G['ffn2_w_gate'], G['ffn2_w_up'], G['ffn2_w_down'] = ffn_bwd(
        dh3, h2, S['ffn2_norm'], W['ffn2_w_gate'], W['ffn2_w_up'], W['ffn2_w_down'], sv2, "ffn2")
    dpa, dpb, dzga, dzgb = _merge_bwd(dh2, W['w_out'], ya, yb, W['w_up_a'], W['w_up_b'], z, name="merge_bwd")
    G['w_out'] = _mm([(mixed, dh2)], ta=True, tm=D, tn=D, tk=512, name="d_w_out")
    dya = _mm([(dpa, W['w_up_a'])], tb=True, tm=512, tn=HG_W, tk=D, name="d_ya")
    dyb = _mm([(dpb, W['w_up_b'])], tb=True, tm=512, tn=AT_W, tk=D, name="d_yb")
    G['w_up_a'] = _mm([(ya, dpa)], ta=True, tm=HG_W, tn=D, tk=512, name="d_w_up_a")
    G['w_up_b'] = _mm([(yb, dpb)], ta=True, tm=AT_W, tn=D, tk=512, name="d_w_up_b")
    do_hg, dzg, d_hgn = _hg_post_bwd(dya, of, ob, z, S['hg_out_norm'], name="hg_post_bwd")
    dq_f, dv_f, dzf_f, dlb_f = _hg_bwd(z, S['hg_lb_fwd'], do_hg, sf, None, rev=False, name="hg_bwd_f")
    dzq, dzi, dzf_b, dlb_b = _hg_bwd(z, S['hg_lb_bwd'], do_hg, sb, (dq_f, dv_f), rev=True, name="hg_bwd_b")
    dqm, dk2, dv2 = _at_bwd(qt, kr, krt, vb, dyb, yb_f32, lse, name="at_bwd")
    dz_at, dwq_g, dwk_g = _at_prep_bwd(dqm, dk2, dv2, z, cc, ss, wq_g, wk_g, name="at_prep_bwd")
    dz = jnp.concatenate([dzq, dzi, dzf_f, dzf_b, dzg, dz_at, dzga, dzgb], axis=1)
    dum = _mm([(dz, w_in)], tb=True, tm=512, tn=D, tk=1792, name="d_um")
    dw_in_p = _mm([(um, dz)], ta=True, tm=D, tn=1792, tk=512, name="d_w_in")
    G['w_in'] = jnp.concatenate([dw_in_p[:, :qk0], _qk_from_group(dw_in_p[:, qk0:qk0 + AT_W + AT_KVW]),
                                 dw_in_p[:, qk0 + AT_W + AT_KVW:]], axis=1)
    dh1, dn_mix = _rmsnorm_bwd(h1, S['mix_norm'], dum, dh2, name="mix_norm_bwd")
    dh0, dn_ffn1, G['ffn1_w_gate'], G['ffn1_w_up'], G['ffn1_w_down'] = ffn_bwd(
        dh1, h0, S['ffn1_norm'], W['ffn1_w_gate'], W['ffn1_w_up'], W['ffn1_w_down'], sv1, "ffn1")

    small_rows = [('loss', loss_lanes), ('ffn1_norm', dn_ffn1.reshape(-1, LANE)), ('mix_norm', dn_mix.reshape(-1, LANE)),
                  ('ffn2_norm', dn_ffn2.reshape(-1, LANE)), ('hg_out_norm', d_hgn.reshape(-1, LANE)),
                  ('lb_f', dlb_f.reshape(-1, LANE)), ('lb_b', dlb_b.reshape(-1, LANE)), ('q_norm', dwq_g), ('k_norm', dwk_g)]
    return dh0[PAD:], dh0[PAD - N_META:PAD], G, small_rows


def kernel(x, meta_tokens, ffn1_norm, ffn1_w_gate, ffn1_w_up, ffn1_w_down, mix_norm, w_in, hg_lb_fwd, hg_lb_bwd, hg_out_norm, q_norm, k_norm, w_up_a, w_up_b, w_out, ffn2_norm, ffn2_w_gate, ffn2_w_up, ffn2_w_down, loss_target, m_meta_tokens, m_ffn1_norm, m_ffn1_w_gate, m_ffn1_w_up, m_ffn1_w_down, m_mix_norm, m_w_in, m_hg_lb_fwd, m_hg_lb_bwd, m_hg_out_norm, m_q_norm, m_k_norm, m_w_up_a, m_w_up_b, m_w_out, m_ffn2_norm, m_ffn2_w_gate, m_ffn2_w_up, m_ffn2_w_down, v_meta_tokens, v_ffn1_norm, v_ffn1_w_gate, v_ffn1_w_up, v_ffn1_w_down, v_mix_norm, v_w_in, v_hg_lb_fwd, v_hg_lb_bwd, v_hg_out_norm, v_q_norm, v_k_norm, v_w_up_a, v_w_up_b, v_w_out, v_ffn2_norm, v_ffn2_w_gate, v_ffn2_w_up, v_ffn2_w_down):
    given = dict(locals())
    w = {n: given[n] for n in WEIGHTS}
    mom = {n: given["m_" + n] for n in WEIGHTS}
    var = {n: given["v_" + n] for n in WEIGHTS}
    c = lax.axis_index("c")
    D = x.shape[-1]

    shapes = {n: w[n].shape[-2:] for n in MATS + ('meta_tokens',)}
    halves = [w[n].astype(BF16).reshape(2, shapes[n][0] // 2, shapes[n][1]) for n in MATS]
    gathered = _gather_mats(halves, name="gather_weights")
    W = {}
    for n, g4 in zip(MATS, gathered):
        r, cs = shapes[n]
        g4 = g4.reshape(4, r, cs)
        if n in FFN_MATS:
            W[n] = g4
        elif n in ROW_SHARDED:
            W[n] = g4.reshape(4 * r, cs)
        else:
            W[n] = g4.transpose(1, 0, 2).reshape(r, 4 * cs)
    meta_rows = w['meta_tokens'].reshape(-1, LANE)
    mg = _allgather_small(meta_rows, name="gather_meta").reshape(4, 2, N_META, -1)[:, 0]
    meta = mg.transpose(1, 0, 2).reshape(N_META, D)
    S = {n: w[n] for n in SMALLS}

    grad_x, dmeta, G, small_rows = _local_step(x[0], loss_target[0], meta, W, S)
    G['meta_tokens'] = dmeta

    names = MATS + ('meta_tokens',)
    views = []
    for n in names:
        r, cs = shapes[n]
        if n in FFN_MATS:
            g4 = G[n]
        elif n in ROW_SHARDED:
            g4 = G[n].reshape(4, r, cs)
        else:
            g4 = G[n].reshape(r, 4, cs).transpose(1, 0, 2)
        views.append(g4.reshape(4, 2, r // 2, cs))
    c1 = c.astype(jnp.int32).reshape(1)
    from_sibling = _rs_pair_exchange(views, name="rs_pair_exchange")
    parts = [_add_half(v, o, c1, out_dtype=F32 if n == 'meta_tokens' else BF16, name="rs_pair_sum_" + n)
             for n, v, o in zip(names, views, from_sibling)]
    slabs = _rs_chip_exchange(parts, name="rs_chip_exchange")
    reds = [_sum4(s, name="rs_chip_sum_" + n) for n, s in zip(names, slabs)]
    both = _rs_pair_share(reds, name="rs_pair_share")
    grads = {n: b.reshape(w[n].shape) for n, b in zip(names, both)}

    rows, off = {}, 0
    for nme, blk in small_rows:
        rows[nme] = off
        off += blk.shape[0]
    block = jnp.concatenate([blk for _, blk in small_rows], axis=0)
    n_rows = (off + 7) // 8 * 8
    block = jnp.pad(block, ((0, n_rows - off), (0, 0)))
    allsmall = _allgather_small(block, name="gather_small").reshape(8, n_rows, LANE)
    tot, d_lbf, d_lbb = _finish_small(allsmall, w['hg_lb_fwd'], w['hg_lb_bwd'], rows=rows, name="finish_small")
    loss = 0.5 * tot[0, 0] / D

    def small(nme, shape):
        r0 = rows[nme]
        return tot[r0:r0 + shape[-1] // LANE].reshape(shape)

    grads['ffn1_norm'] = small('ffn1_norm', w['ffn1_norm'].shape)
    grads['mix_norm'] = small('mix_norm', w['mix_norm'].shape)
    grads['ffn2_norm'] = small('ffn2_norm', w['ffn2_norm'].shape)
    grads['hg_out_norm'] = small('hg_out_norm', w['hg_out_norm'].shape)
    grads['hg_lb_fwd'] = d_lbf
    grads['hg_lb_bwd'] = d_lbb
    grads['q_norm'] = _ungroup_vec(tot[rows['q_norm']])
    grads['k_norm'] = _ungroup_vec(tot[rows['k_norm']])

    delta, new_m, new_v = {}, {}, {}
    for n in WEIGHTS:
        delta[n], new_m[n], new_v[n] = _adamw(w[n], grads[n], mom[n], var[n], name="adamw_" + n)
    return (loss, grad_x[None], *[grads[n] for n in WEIGHTS], *[delta[n] for n in WEIGHTS],
            *[new_m[n] for n in WEIGHTS], *[new_v[n] for n in WEIGHTS])
```

```python
import functools
import math

import numpy as np
import jax
import jax.numpy as jnp
from jax import lax
from jax.experimental import pallas as pl
from jax.experimental.pallas import tpu as pltpu

F32 = jnp.float32
BF16 = jnp.bfloat16
SDS = jax.ShapeDtypeStruct
MESH = pl.DeviceIdType.MESH

EPS = 1e-6
N_META = 16
PAD = 512
LANE = 128
CHUNK = 128
HG_HEADS = 4
HG_W = HG_HEADS * 128
AT_HEADS = 8
AT_KV = 2
AT_HD = 64
AT_W = AT_HEADS * AT_HD
AT_KVW = AT_KV * AT_HD
VT_ROWS = AT_HD + 16
GRID_W = 64
ROPE_THETA = 10000.0
Z_HG = 5 * HG_W
Z_AT = AT_W + 2 * AT_KVW
ADAM_LR, ADAM_B1, ADAM_B2, ADAM_EPS, ADAM_WD, ADAM_STEP = 0.001, 0.9, 0.999, 1e-08, 0.01, 10
VMEM_DEFAULT = 48 * 1024 * 1024
VMEM_LARGE = 60 * 1024 * 1024
NEG = -1e30

MATS = ('ffn1_w_gate', 'ffn1_w_up', 'ffn1_w_down', 'w_in', 'w_up_a', 'w_up_b', 'w_out',
        'ffn2_w_gate', 'ffn2_w_up', 'ffn2_w_down')
ROW_SHARDED = ('ffn1_w_down', 'w_out', 'ffn2_w_down')
FFN_MATS = ('ffn1_w_gate', 'ffn1_w_up', 'ffn1_w_down', 'ffn2_w_gate', 'ffn2_w_up', 'ffn2_w_down')
SMALLS = ('ffn1_norm', 'mix_norm', 'hg_lb_fwd', 'hg_lb_bwd', 'hg_out_norm', 'q_norm', 'k_norm', 'ffn2_norm')
WEIGHTS = ('meta_tokens', 'ffn1_norm', 'ffn1_w_gate', 'ffn1_w_up', 'ffn1_w_down', 'mix_norm', 'w_in', 'hg_lb_fwd',
           'hg_lb_bwd', 'hg_out_norm', 'q_norm', 'k_norm', 'w_up_a', 'w_up_b', 'w_out', 'ffn2_norm', 'ffn2_w_gate',
           'ffn2_w_up', 'ffn2_w_down')


def _params(sem=None, vmem=VMEM_DEFAULT):
    return pltpu.CompilerParams(dimension_semantics=sem, vmem_limit_bytes=vmem)


def _tile(n, pref, q=LANE):
    for d in range(min(pref, n), 0, -1):
        if n % d == 0 and d % q == 0:
            return d
    return n


def _sigmoid(x):
    return 1.0 / (1.0 + jnp.exp(-x))


def _dot(a, b, dims):
    return lax.dot_general(a, b, (dims, ((), ())), preferred_element_type=F32)


def _nn(a, b):
    return _dot(a, b, ((1,), (0,)))


def _nt(a, b):
    return _dot(a, b, ((1,), (1,)))


def _tn(a, b):
    return _dot(a, b, ((0,), (0,)))


def _split3(x):
    x1 = x.astype(BF16)
    r = x - x1.astype(F32)
    x2 = r.astype(BF16)
    x3 = (r - x2.astype(F32)).astype(BF16)
    return x1, x2, x3


def _exact_left(m01, x):
    x1, x2, x3 = _split3(x)
    return _nn(m01, x1) + _nn(m01, x2) + _nn(m01, x3)


def _exact_right(x, m01):
    x1, x2, x3 = _split3(x)
    return _nn(x1, m01) + _nn(x2, m01) + _nn(x3, m01)


def _mm(pairs, *, name, ta=False, tb=False, out_dtype=F32, tm=512, tn=1024, tk=1024, alpha=1.0, res=None):
    a0, b0 = pairs[0]
    M = a0.shape[1] if ta else a0.shape[0]
    K = a0.shape[0] if ta else a0.shape[1]
    N = b0.shape[0] if tb else b0.shape[1]
    tm, tn, tk = _tile(M, tm), _tile(N, tn), _tile(K, tk)
    nk = K // tk
    npair = len(pairs)
    dims = ((0 if ta else 1,), (1 if tb else 0,))

    def body(*refs):
        ab = refs[:2 * npair]
        pos = 2 * npair
        res_ref = None
        if res is not None:
            res_ref = refs[pos]
            pos += 1
        o_ref = refs[pos]

        def partial_sum():
            tot = None
            for p in range(npair):
                d = _dot(ab[2 * p][...].astype(BF16), ab[2 * p + 1][...].astype(BF16), dims)
                tot = d if tot is None else tot + d
            return tot

        def finish(acc):
            r = acc if alpha == 1.0 else acc * alpha
            if res_ref is not None:
                r = res_ref[...] + r
            o_ref[...] = r.astype(out_dtype)

        if nk == 1:
            finish(partial_sum())
        else:
            acc_ref = refs[pos + 1]
            k = pl.program_id(2)

            @pl.when(k == 0)
            def _():
                acc_ref[...] = jnp.zeros_like(acc_ref)

            acc_ref[...] += partial_sum()

            @pl.when(k == nk - 1)
            def _():
                finish(acc_ref[...])

    a_spec = pl.BlockSpec((tk, tm), lambda j, i, k: (k, i)) if ta else pl.BlockSpec((tm, tk), lambda j, i, k: (i, k))
    b_spec = pl.BlockSpec((tn, tk), lambda j, i, k: (j, k)) if tb else pl.BlockSpec((tk, tn), lambda j, i, k: (k, j))
    o_spec = pl.BlockSpec((tm, tn), lambda j, i, k: (i, j))
    in_specs, args = [], []
    for a, b in pairs:
        in_specs += [a_spec, b_spec]
        args += [a, b]
    if res is not None:
        in_specs.append(o_spec)
        args.append(res)
    return pl.pallas_call(
        body, grid=(N // tn, M // tm, nk), in_specs=in_specs, out_specs=o_spec,
        out_shape=SDS((M, N), out_dtype),
        scratch_shapes=[pltpu.VMEM((tm, tn), F32)] if nk > 1 else [],
        compiler_params=_params(("parallel", "parallel", "arbitrary")), name=name)(*args)


def _rmsnorm_fwd(h, w, *, name):
    L, D = h.shape
    tm = _tile(L, 512)

    def body(h_ref, w_ref, o_ref):
        x = h_ref[...]
        r = lax.rsqrt(jnp.mean(x * x, axis=-1, keepdims=True) + EPS)
        o_ref[...] = (x * r * w_ref[...]).astype(BF16)

    return pl.pallas_call(
        body, grid=(L // tm,),
        in_specs=[pl.BlockSpec((tm, D), lambda i: (i, 0)), pl.BlockSpec((1, D), lambda i: (0, 0))],
        out_specs=pl.BlockSpec((tm, D), lambda i: (i, 0)), out_shape=SDS((L, D), BF16),
        compiler_params=_params(("parallel",)), name=name)(h, w)


def _rmsnorm_bwd(h, w, dn, dres, *, name):
    L, D = h.shape
    tm = _tile(L, 512)

    def body(h_ref, w_ref, dn_ref, dres_ref, dh_ref, dw_ref):
        x = h_ref[...]
        r = lax.rsqrt(jnp.mean(x * x, axis=-1, keepdims=True) + EPS)
        xh = x * r
        dn = dn_ref[...]
        dxh = dn * w_ref[...]
        dh_ref[...] = dres_ref[...] + r * (dxh - xh * jnp.mean(dxh * xh, axis=-1, keepdims=True))

        @pl.when(pl.program_id(0) == 0)
        def _():
            dw_ref[...] = jnp.zeros_like(dw_ref)

        dw_ref[...] += jnp.sum(dn * xh, axis=0, keepdims=True)

    row = pl.BlockSpec((tm, D), lambda i: (i, 0))
    vec = pl.BlockSpec((1, D), lambda i: (0, 0))
    return pl.pallas_call(
        body, grid=(L // tm,), in_specs=[row, vec, row, row], out_specs=[row, vec],
        out_shape=[SDS((L, D), F32), SDS((1, D), F32)],
        compiler_params=_params(("arbitrary",)), name=name)(h, w, dn, dres)


def _ffn_up(n, wg, wu, *, name):
    L, D = n.shape
    Fd = wg.shape[1]
    tm, tn = _tile(L, 512), _tile(Fd, 1408)

    def body(n_ref, wg_ref, wu_ref, g_ref, u_ref, a_ref):
        x = n_ref[...]
        g = _nn(x, wg_ref[...])
        u = _nn(x, wu_ref[...])
        g_ref[...] = g
        u_ref[...] = u
        a_ref[...] = (g * _sigmoid(g) * u).astype(BF16)

    wspec = pl.BlockSpec((D, tn), lambda j, i: (0, j))
    ospec = pl.BlockSpec((tm, tn), lambda j, i: (i, j))
    return pl.pallas_call(
        body, grid=(Fd // tn, L // tm),
        in_specs=[pl.BlockSpec((tm, D), lambda j, i: (i, 0)), wspec, wspec],
        out_specs=[ospec, ospec, ospec],
        out_shape=[SDS((L, Fd), F32), SDS((L, Fd), F32), SDS((L, Fd), BF16)],
        compiler_params=_params(("parallel", "parallel")), name=name)(n, wg, wu)


def _ffn_dact(dh, wd, g, u, *, name):
    L, D = dh.shape
    Fd = wd.shape[0]
    tm, tn = _tile(L, 512), _tile(Fd, 1408)

    def body(dh_ref, wd_ref, g_ref, u_ref, dg_ref, du_ref):
        da = 0.5 * _nt(dh_ref[...].astype(BF16), wd_ref[...])
        g = g_ref[...]
        sg = _sigmoid(g)
        dg_ref[...] = (da * u_ref[...] * (sg * (1.0 + g * (1.0 - sg)))).astype(BF16)
        du_ref[...] = (da * (g * sg)).astype(BF16)

    ospec = pl.BlockSpec((tm, tn), lambda j, i: (i, j))
    return pl.pallas_call(
        body, grid=(Fd // tn, L // tm),
        in_specs=[pl.BlockSpec((tm, D), lambda j, i: (i, 0)), pl.BlockSpec((tn, D), lambda j, i: (j, 0)), ospec, ospec],
        out_specs=[ospec, ospec], out_shape=[SDS((L, Fd), BF16), SDS((L, Fd), BF16)],
        compiler_params=_params(("parallel", "parallel")), name=name)(dh, wd, g, u)


def _ffn4_up(n, wg4, wu4, *, name):
    L, D = n.shape
    ns, _, cs = wg4.shape
    tm = _tile(L, 768)

    def body(n_ref, wg_ref, wu_ref, g_ref, u_ref, a_ref):
        x = n_ref[...]
        g = _nn(x, wg_ref[...])
        u = _nn(x, wu_ref[...])
        g_ref[...] = g
        u_ref[...] = u
        a_ref[...] = (g * _sigmoid(g) * u).astype(BF16)

    wspec = pl.BlockSpec((None, D, cs), lambda j, i: (j, 0, 0))
    ospec = pl.BlockSpec((None, tm, cs), lambda j, i: (j, i, 0))
    return pl.pallas_call(
        body, grid=(ns, L // tm),
        in_specs=[pl.BlockSpec((tm, D), lambda j, i: (i, 0)), wspec, wspec], out_specs=[ospec, ospec, ospec],
        out_shape=[SDS((ns, L, cs), F32), SDS((ns, L, cs), F32), SDS((ns, L, cs), BF16)],
        compiler_params=_params(("parallel", "parallel")), name=name)(n, wg4, wu4)


def _ffn4_down(a4, wd4, h, *, name):
    ns, L, cs = a4.shape
    D = wd4.shape[2]
    tm = _tile(L, 512)

    def body(a_ref, w_ref, h_ref, o_ref):
        acc = _nn(a_ref[0], w_ref[0])
        for j in range(1, ns):
            acc = acc + _nn(a_ref[j], w_ref[j])
        o_ref[...] = h_ref[...] + 0.5 * acc

    row = pl.BlockSpec((tm, D), lambda i: (i, 0))
    return pl.pallas_call(
        body, grid=(L // tm,),
        in_specs=[pl.BlockSpec((ns, tm, cs), lambda i: (0, i, 0)), pl.BlockSpec((ns, cs, D), lambda i: (0, 0, 0)), row],
        out_specs=row, out_shape=SDS((L, D), F32),
        compiler_params=_params(("parallel",)), name=name)(a4, wd4, h)


def _ffn4_dact(dh, wd4, g4, u4, *, name):
    L, D = dh.shape
    ns, cs, _ = wd4.shape
    tm = _tile(L, 768)

    def body(dh_ref, wd_ref, g_ref, u_ref, dg_ref, du_ref):
        da = 0.5 * _nt(dh_ref[...].astype(BF16), wd_ref[...])
        g = g_ref[...]
        sg = _sigmoid(g)
        dg_ref[...] = (da * u_ref[...] * (sg * (1.0 + g * (1.0 - sg)))).astype(BF16)
        du_ref[...] = (da * (g * sg)).astype(BF16)

    ospec = pl.BlockSpec((None, tm, cs), lambda j, i: (j, i, 0))
    return pl.pallas_call(
        body, grid=(ns, L // tm),
        in_specs=[pl.BlockSpec((tm, D), lambda j, i: (i, 0)), pl.BlockSpec((None, cs, D), lambda j, i: (j, 0, 0)), ospec, ospec],
        out_specs=[ospec, ospec], out_shape=[SDS((ns, L, cs), BF16), SDS((ns, L, cs), BF16)],
        compiler_params=_params(("parallel", "parallel")), name=name)(dh, wd4, g4, u4)


def _ffn4_dn(dg4, du4, wg4, wu4, *, name):
    ns, L, cs = dg4.shape
    D = wg4.shape[1]
    tm = _tile(L, 512)

    def body(dg_ref, du_ref, wg_ref, wu_ref, o_ref):
        acc = None
        for j in range(ns):
            t = _nt(dg_ref[j], wg_ref[j]) + _nt(du_ref[j], wu_ref[j])
            acc = t if acc is None else acc + t
        o_ref[...] = acc

    aspec = pl.BlockSpec((ns, tm, cs), lambda i: (0, i, 0))
    wspec = pl.BlockSpec((ns, D, cs), lambda i: (0, 0, 0))
    return pl.pallas_call(
        body, grid=(L // tm,), in_specs=[aspec, aspec, wspec, wspec],
        out_specs=pl.BlockSpec((tm, D), lambda i: (i, 0)), out_shape=SDS((L, D), F32),
        compiler_params=_params(("parallel",), VMEM_LARGE), name=name)(dg4, du4, wg4, wu4)


def _ffn4_dw(x, y4, *, x_is_rows, alpha=1.0, name):
    L, D = x.shape
    ns, _, cs = y4.shape
    tk = _tile(L, 512)
    nk = L // tk
    oshape = (D, cs) if x_is_rows else (cs, D)

    def body(x_ref, y_ref, o_ref):
        k = pl.program_id(0)

        @pl.when(k == 0)
        def _():
            o_ref[...] = jnp.zeros_like(o_ref)

        xb = x_ref[...].astype(BF16)
        if x_is_rows:
            xt = xb.T
            for j in range(ns):
                o_ref[j] += _nn(xt, y_ref[j])
        else:
            for j in range(ns):
                o_ref[j] += _tn(y_ref[j], xb)

        if alpha != 1.0:
            @pl.when(k == nk - 1)
            def _():
                o_ref[...] = o_ref[...] * alpha

    return pl.pallas_call(
        body, grid=(nk,),
        in_specs=[pl.BlockSpec((tk, D), lambda k: (k, 0)), pl.BlockSpec((ns, tk, cs), lambda k: (0, k, 0))],
        out_specs=pl.BlockSpec((ns,) + oshape, lambda k: (0, 0, 0)), out_shape=SDS((ns,) + oshape, F32),
        compiler_params=_params(("arbitrary",)), name=name)(x, y4)


def _hg_masks(rev):
    t = lax.broadcasted_iota(jnp.int32, (CHUNK, CHUNK), 0)
    s = lax.broadcasted_iota(jnp.int32, (CHUNK, CHUNK), 1)
    causal = (s >= t) if rev else (s <= t)
    levels = []
    for sh in (6, 5, 4):
        same = jnp.right_shift(t, sh + 1) == jnp.right_shift(s, sh + 1)
        tq = jnp.bitwise_and(jnp.right_shift(t, sh), 1)
        sk = jnp.bitwise_and(jnp.right_shift(s, sh), 1)
        levels.append(same & (tq == (0 if rev else 1)) & (sk == (1 if rev else 0)))
    diag = (jnp.right_shift(t, 4) == jnp.right_shift(s, 4)) & causal
    return causal, levels, diag


def _hg_intra_factors(q, k, b, b_scr, rev):
    b_scr[...] = b
    row = lax.broadcasted_iota(jnp.int32, (CHUNK, LANE), 0)
    out = []
    for sh in (6, 5, 4):
        lb = 1 << sh
        pieces = []
        for p in range(0, CHUNK, 2 * lb):
            r = p + lb if rev else p + lb - 1
            pieces.append(jnp.broadcast_to(b_scr[pl.ds(r, 1), :], (2 * lb, LANE)))
        ref = pieces[0] if len(pieces) == 1 else jnp.concatenate(pieces, axis=0)
        qside = jnp.bitwise_and(jnp.right_shift(row, sh), 1) == (0 if rev else 1)
        eq = jnp.where(qside, jnp.exp(jnp.minimum(b - ref, 0.0)), 0.0)
        ek = jnp.where(qside, 0.0, jnp.exp(jnp.minimum(ref - b, 0.0)))
        out.append((eq, ek, (q * eq).astype(BF16), (k * ek).astype(BF16)))
    pieces = []
    for a in range(0, CHUNK, 16):
        r = a + (8 if rev else 7)
        pieces.append(jnp.broadcast_to(b_scr[pl.ds(r, 1), :], (16, LANE)))
    ref = jnp.concatenate(pieces, axis=0)
    eq = jnp.exp(jnp.minimum(b - ref, 80.0))
    ek = jnp.exp(jnp.minimum(ref - b, 80.0))
    out.append((eq, ek, (q * eq).astype(BF16), (k * ek).astype(BF16)))
    return out


def _hg_gate(zf, l0, l1, valid):
    mx = jnp.maximum(l0, l1)
    e0, e1 = jnp.exp(l0 - mx), jnp.exp(l1 - mx)
    p0 = e0 / (e0 + e1)
    sg = _sigmoid(-zf)
    k = jnp.where(valid, (1.0 - p0) * sg, 0.0)
    return p0, sg, k, jnp.log(1.0 - k)


def _hg_fwd(z, lbp, *, rev, name):
    L = z.shape[0]
    nc = L // CHUNK
    fcol = 3 if rev else 2

    def cidx(j):
        return nc - 1 - j if rev else j

    def body(zq_ref, zi_ref, zf_ref, lb_ref, o_ref, ssave_ref, st_scr, b_scr):
        j = pl.program_id(0)

        @pl.when(j == 0)
        def _():
            st_scr[...] = jnp.zeros_like(st_scr)

        causal, lmasks, dmask = _hg_masks(rev)
        tri = jnp.where(causal, 1.0, 0.0).astype(BF16)
        rowg = cidx(j) * CHUNK + lax.broadcasted_iota(jnp.int32, (CHUNK, LANE), 0)
        valid = rowg >= PAD - N_META
        last = 0 if rev else CHUNK - 1
        for hh in range(HG_HEADS):
            sl = slice(LANE * hh, LANE * (hh + 1))
            zq = zq_ref[:, sl]
            q = zq * _sigmoid(zq)
            v = zi_ref[:, sl].astype(BF16)
            _, _, k, g = _hg_gate(zf_ref[:, sl], lb_ref[0:1, sl], lb_ref[1:2, sl], valid)
            b = _exact_left(tri, g)
            st = st_scr[hh]
            ssave_ref[0, hh] = st
            o = _nt((q * jnp.exp(b)).astype(BF16), st.astype(BF16))
            a = None
            fac = _hg_intra_factors(q, k, b, b_scr, rev)
            for (eq, ek, qq, kk), msk in zip(fac, lmasks + [dmask]):
                t = jnp.where(msk, _nt(qq, kk), 0.0)
                a = t if a is None else a + t
            o_ref[:, sl] = o + _nn(a.astype(BF16), v)
            bl = b_scr[pl.ds(last, 1), :]
            kd = (k * jnp.exp(bl - b)).astype(BF16)
            st_scr[hh] = st * jnp.exp(bl) + _tn(v, kd)

    zspec = lambda col: pl.BlockSpec((CHUNK, HG_W), lambda j: (cidx(j), col))
    return pl.pallas_call(
        body, grid=(nc,),
        in_specs=[zspec(0), zspec(1), zspec(fcol), pl.BlockSpec((2, HG_W), lambda j: (0, 0))],
        out_specs=[pl.BlockSpec((CHUNK, HG_W), lambda j: (cidx(j), 0)),
                   pl.BlockSpec((1, HG_HEADS, LANE, LANE), lambda j: (cidx(j), 0, 0, 0))],
        out_shape=[SDS((L, HG_W), F32), SDS((nc, HG_HEADS, LANE, LANE), F32)],
        scratch_shapes=[pltpu.VMEM((HG_HEADS, LANE, LANE), F32), pltpu.VMEM((CHUNK, LANE), F32)],
        compiler_params=_params(("arbitrary",)), name=name)(z, z, z, lbp)


def _hg_bwd(z, lbp, do, ssave, prev, *, rev, name):
    L = z.shape[0]
    nc = L // CHUNK
    fcol = 3 if rev else 2
    final = prev is not None

    def cidx(j):
        return j if rev else nc - 1 - j

    def body(*refs):
        zq_ref, zi_ref, zf_ref, lb_ref, do_ref, ss_ref = refs[:6]
        pos = 6
        if final:
            dqin_ref, dvin_ref = refs[6:8]
            pos = 8
        dq_ref, dv_ref, dzf_ref, dlb_ref, dst_scr, b_scr = refs[pos:pos + 6]
        j = pl.program_id(0)

        @pl.when(j == 0)
        def _():
            dst_scr[...] = jnp.zeros_like(dst_scr)
            dlb_ref[...] = jnp.zeros_like(dlb_ref)

        causal, lmasks, dmask = _hg_masks(rev)
        tri = jnp.where(causal, 1.0, 0.0).astype(BF16)
        ti = lax.broadcasted_iota(jnp.int32, (CHUNK, CHUNK), 0)
        si = lax.broadcasted_iota(jnp.int32, (CHUNK, CHUNK), 1)
        tri_t = jnp.where((si <= ti) if rev else (si >= ti), 1.0, 0.0).astype(BF16)
        rowg = cidx(j) * CHUNK + lax.broadcasted_iota(jnp.int32, (CHUNK, LANE), 0)
        valid = rowg >= PAD - N_META
        last = 0 if rev else CHUNK - 1
        for hh in range(HG_HEADS):
            sl = slice(LANE * hh, LANE * (hh + 1))
            zq = zq_ref[:, sl]
            sq = _sigmoid(zq)
            q = zq * sq
            v = zi_ref[:, sl].astype(BF16)
            p0, sg, k, g = _hg_gate(zf_ref[:, sl], lb_ref[0:1, sl], lb_ref[1:2, sl], valid)
            b = _exact_left(tri, g)
            dob = do_ref[:, sl].astype(BF16)
            st = ss_ref[0, hh]
            dst = dst_scr[hh]
            stb, dstb = st.astype(BF16), dst.astype(BF16)
            eb = jnp.exp(b)
            qe = (q * eb).astype(BF16)
            fac = _hg_intra_factors(q, k, b, b_scr, rev)
            bl = b_scr[pl.ds(last, 1), :]
            ebl = jnp.exp(bl)
            kde = jnp.exp(bl - b)
            kd = (k * kde).astype(BF16)
            da = jnp.where(causal, _nt(dob, v), 0.0)
            dq = eb * _nn(dob, stb)
            dk_inter = kde * _nn(v, dstb)
            dk = dk_inter
            dv = _nt(kd, dstb)
            a = None
            db = q * dq - k * dk
            for (eq, ek, qq, kk), msk in zip(fac, lmasks + [dmask]):
                t = jnp.where(msk, _nt(qq, kk), 0.0)
                a = t if a is None else a + t
                dal = jnp.where(msk, da, 0.0).astype(BF16)
                mq = _nn(dal, kk)
                mk = _tn(dal, qq)
                dq = dq + eq * mq
                dk = dk + ek * mk
                db = db + (qq.astype(F32) * mq - kk.astype(F32) * mk)
            dv = dv + _tn(a.astype(BF16), dob)
            extra = ebl * jnp.sum(st * dst, axis=0, keepdims=True) + jnp.sum(k * dk_inter, axis=0, keepdims=True)
            dst_scr[hh] = dst * ebl + _tn(dob, qe)
            dg = _exact_left(tri_t, db) + extra
            dk_tot = dk - dg / (1.0 - k)
            dzf_ref[:, sl] = jnp.where(valid, dk_tot * (1.0 - p0) * (-sg * (1.0 - sg)), 0.0).astype(BF16)
            dlb_ref[:, sl] += jnp.sum(jnp.where(valid, -sg * dk_tot, 0.0), axis=0, keepdims=True)
            if final:
                dq_ref[:, sl] = ((dq + dqin_ref[:, sl]) * (sq * (1.0 + zq * (1.0 - sq)))).astype(BF16)
                dv_ref[:, sl] = (dv + dvin_ref[:, sl]).astype(BF16)
            else:
                dq_ref[:, sl] = dq
                dv_ref[:, sl] = dv

    zspec = lambda col: pl.BlockSpec((CHUNK, HG_W), lambda j: (cidx(j), col))
    rspec = pl.BlockSpec((CHUNK, HG_W), lambda j: (cidx(j), 0))
    in_specs = [zspec(0), zspec(1), zspec(fcol), pl.BlockSpec((2, HG_W), lambda j: (0, 0)), rspec,
                pl.BlockSpec((1, HG_HEADS, LANE, LANE), lambda j: (cidx(j), 0, 0, 0))]
    args = [z, z, z, lbp, do, ssave]
    if final:
        in_specs += [rspec, rspec]
        args += list(prev)
    odt = BF16 if final else F32
    return pl.pallas_call(
        body, grid=(nc,), in_specs=in_specs,
        out_specs=[rspec, rspec, rspec, pl.BlockSpec((1, HG_W), lambda j: (0, 0))],
        out_shape=[SDS((L, HG_W), odt), SDS((L, HG_W), odt), SDS((L, HG_W), BF16), SDS((1, HG_W), F32)],
        scratch_shapes=[pltpu.VMEM((HG_HEADS, LANE, LANE), F32), pltpu.VMEM((CHUNK, LANE), F32)],
        compiler_params=_params(("arbitrary",)), name=name)(*args)


def _hg_post_fwd(of, ob, z, w, *, name):
    L = of.shape[0]
    tm = _tile(L, 512)

    def body(of_ref, ob_ref, zg_ref, w_ref, y_ref):
        for hh in range(HG_HEADS):
            sl = slice(LANE * hh, LANE * (hh + 1))
            o = of_ref[:, sl] + ob_ref[:, sl]
            r = lax.rsqrt(jnp.mean(o * o, axis=-1, keepdims=True) + EPS)
            zg = zg_ref[:, sl]
            y_ref[:, sl] = (o * r * w_ref[:, sl] * (zg * _sigmoid(zg))).astype(BF16)

    row = pl.BlockSpec((tm, HG_W), lambda i: (i, 0))
    return pl.pallas_call(
        body, grid=(L // tm,),
        in_specs=[row, row, pl.BlockSpec((tm, HG_W), lambda i: (i, 4)), pl.BlockSpec((1, HG_W), lambda i: (0, 0))],
        out_specs=row, out_shape=SDS((L, HG_W), BF16),
        compiler_params=_params(("parallel",)), name=name)(of, ob, z, w)


def _hg_post_bwd(dy, of, ob, z, w, *, name):
    L = of.shape[0]
    tm = _tile(L, 512)

    def body(dy_ref, of_ref, ob_ref, zg_ref, w_ref, do_ref, dzg_ref, dw_ref):
        @pl.when(pl.program_id(0) == 0)
        def _():
            dw_ref[...] = jnp.zeros_like(dw_ref)

        for hh in range(HG_HEADS):
            sl = slice(LANE * hh, LANE * (hh + 1))
            o = of_ref[:, sl] + ob_ref[:, sl]
            r = lax.rsqrt(jnp.mean(o * o, axis=-1, keepdims=True) + EPS)
            xh = o * r
            zg = zg_ref[:, sl]
            sg = _sigmoid(zg)
            w = w_ref[:, sl]
            dy = dy_ref[:, sl]
            dys = dy * (zg * sg)
            dzg_ref[:, sl] = (dy * xh * w * (sg * (1.0 + zg * (1.0 - sg)))).astype(BF16)
            dw_ref[:, sl] += jnp.sum(dys * xh, axis=0, keepdims=True)
            dxh = dys * w
            do_ref[:, sl] = r * (dxh - xh * jnp.mean(dxh * xh, axis=-1, keepdims=True))

    row = pl.BlockSpec((tm, HG_W), lambda i: (i, 0))
    vec = pl.BlockSpec((1, HG_W), lambda i: (0, 0))
    return pl.pallas_call(
        body, grid=(L // tm,),
        in_specs=[row, row, row, pl.BlockSpec((tm, HG_W), lambda i: (i, 4)), vec],
        out_specs=[row, row, vec],
        out_shape=[SDS((L, HG_W), F32), SDS((L, HG_W), BF16), SDS((1, HG_W), F32)],
        compiler_params=_params(("arbitrary",)), name=name)(dy, of, ob, z, w)


N_GROUPS = (AT_HEADS + AT_KV) // 2


def _qk_to_group(wqk):
    d = wqk.shape[0]
    return wqk.reshape(d, N_GROUPS, 2, AT_HD // 2, 2).transpose(0, 1, 4, 2, 3).reshape(d, N_GROUPS * LANE)


def _qk_from_group(wqk):
    d = wqk.shape[0]
    return wqk.reshape(d, N_GROUPS, 2, 2, AT_HD // 2).transpose(0, 1, 3, 4, 2).reshape(d, N_GROUPS * LANE)


def _group_vec(w64):
    halves = w64.reshape(AT_HD // 2, 2).T
    return jnp.broadcast_to(halves[:, None, :], (2, 2, AT_HD // 2)).reshape(1, LANE)


def _ungroup_vec(w128):
    w = w128.reshape(2, 2, 32).sum(axis=1)
    return w.T.reshape(1, AT_HD)


def _rope_tables(L):
    n_real = L - PAD
    t = np.arange(n_real)
    row = np.concatenate([np.zeros(PAD), t // GRID_W]).astype(np.float32)
    col = np.concatenate([np.zeros(PAD), t % GRID_W]).astype(np.float32)
    inv = jnp.asarray(ROPE_THETA, F32) ** (-jnp.arange(0, AT_HD // 2, 2, dtype=F32) / (AT_HD // 2))
    ang = jnp.concatenate([jnp.asarray(row)[:, None] * inv, jnp.asarray(col)[:, None] * inv], axis=-1)
    cos, sin = jnp.cos(ang), jnp.sin(ang)
    cc = jnp.tile(cos, (1, 4))
    ss = jnp.concatenate([-sin, -sin, sin, sin], axis=1)
    return cc, ss


def _seg_matrix():
    a = lax.broadcasted_iota(jnp.int32, (LANE, LANE), 0)
    b = lax.broadcasted_iota(jnp.int32, (LANE, LANE), 1)
    same = jnp.bitwise_and(jnp.right_shift(a, 5), 1) == jnp.bitwise_and(jnp.right_shift(b, 5), 1)
    return jnp.where(same, 1.0, 0.0).astype(BF16)


def _slot_mask(shape, hp):
    lane = lax.broadcasted_iota(jnp.int32, shape, 1)
    return jnp.bitwise_and(jnp.right_shift(lane, 5), 1) == hp


def _at_prep(z, cc, ss, wq, wk, *, name):
    L = z.shape[0]
    tm = PAD
    qcol = Z_HG // AT_W
    kvcol = (Z_HG + AT_W) // (2 * LANE)

    def body(zq_ref, zkv_ref, cc_ref, ss_ref, wq_ref, wk_ref, qt_ref, kr_ref, krt_ref, vb_ref, vt_ref):
        seg = _seg_matrix()
        cc, ss = cc_ref[...], ss_ref[...]

        def normrope(x, w):
            r = lax.rsqrt(_exact_right(x * x, seg) * (1.0 / AT_HD) + EPS)
            y = x * r * w
            return y * cc + pltpu.roll(y, 64, 1) * ss

        for g in range(AT_HEADS // 2):
            o = normrope(zq_ref[:, LANE * g:LANE * (g + 1)], wq_ref[...]) * (AT_HD ** -0.5)
            for hp in range(2):
                h = 2 * g + hp
                tgt = h // (AT_HEADS // AT_KV)
                xm = jnp.where(_slot_mask(o.shape, hp), o, 0.0)
                if tgt != hp:
                    xm = pltpu.roll(xm, 32 if tgt == 1 else 96, 1)
                qt_ref[h] = xm.T.astype(BF16)
        kr = normrope(zkv_ref[:, :LANE], wk_ref[...])
        kr_ref[...] = kr.astype(BF16)
        krt_ref[0] = kr.T.astype(BF16)
        v = zkv_ref[:, LANE:]
        low = lax.broadcasted_iota(jnp.int32, v.shape, 1) < AT_HD
        vb_ref[0] = jnp.where(low, v, 0.0).astype(BF16)
        vb_ref[1] = jnp.where(low, pltpu.roll(v, AT_HD, 1), 0.0).astype(BF16)
        vt = v.T.astype(BF16)
        ones = jnp.ones((VT_ROWS - AT_HD, tm), BF16)
        for j in range(AT_KV):
            vt_ref[j, 0, 0:AT_HD, :] = vt[AT_HD * j:AT_HD * (j + 1)]
            vt_ref[j, 0, AT_HD:VT_ROWS, :] = ones

    tab = pl.BlockSpec((tm, LANE), lambda i: (i, 0))
    vec = pl.BlockSpec((1, LANE), lambda i: (0, 0))
    nt = L // tm
    return pl.pallas_call(
        body, grid=(nt,),
        in_specs=[pl.BlockSpec((tm, AT_W), lambda i: (i, qcol)), pl.BlockSpec((tm, 2 * LANE), lambda i: (i, kvcol)),
                  tab, tab, vec, vec],
        out_specs=[pl.BlockSpec((AT_HEADS, LANE, tm), lambda i: (0, 0, i)), tab,
                   pl.BlockSpec((1, LANE, tm), lambda i: (i, 0, 0)),
                   pl.BlockSpec((AT_KV, tm, LANE), lambda i: (0, i, 0)),
                   pl.BlockSpec((AT_KV, 1, VT_ROWS, tm), lambda i: (0, i, 0, 0))],
        out_shape=[SDS((AT_HEADS, LANE, L), BF16), SDS((L, LANE), BF16), SDS((nt, LANE, tm), BF16),
                   SDS((AT_KV, L, LANE), BF16), SDS((AT_KV, nt, VT_ROWS, tm), BF16)],
        compiler_params=_params(("parallel",)), name=name)(z, z, cc, ss, wq, wk)


def _at_prep_bwd(dqm, dk2, dv2, z, cc, ss, wq, wk, *, name):
    L = z.shape[0]
    tm = PAD
    qcol = Z_HG // AT_W
    kvcol = (Z_HG + AT_W) // (2 * LANE)

    def body(dqm_ref, dk2_ref, dv2_ref, zq_ref, zkv_ref, cc_ref, ss_ref, wq_ref, wk_ref, dz_ref, dwq_ref, dwk_ref):
        @pl.when(pl.program_id(0) == 0)
        def _():
            dwq_ref[...] = jnp.zeros_like(dwq_ref)
            dwk_ref[...] = jnp.zeros_like(dwk_ref)

        seg = _seg_matrix()
        cc, ss = cc_ref[...], ss_ref[...]

        def back(x, w, do):
            dy = do * cc + pltpu.roll(do * ss, 64, 1)
            r = lax.rsqrt(_exact_right(x * x, seg) * (1.0 / AT_HD) + EPS)
            xh = x * r
            dxh = dy * w
            dx = r * (dxh - xh * (_exact_right(dxh * xh, seg) * (1.0 / AT_HD)))
            return dx, jnp.sum(dy * xh, axis=0, keepdims=True)

        for g in range(AT_HEADS // 2):
            do = None
            for hp in range(2):
                h = 2 * g + hp
                tgt = h // (AT_HEADS // AT_KV)
                d = jnp.where(_slot_mask((tm, LANE), tgt), dqm_ref[h], 0.0)
                if tgt != hp:
                    d = pltpu.roll(d, 96 if tgt == 1 else 32, 1)
                do = d if do is None else do + d
            dx, dw = back(zq_ref[:, LANE * g:LANE * (g + 1)], wq_ref[...], do * (AT_HD ** -0.5))
            dz_ref[:, LANE * g:LANE * (g + 1)] = dx.astype(BF16)
            dwq_ref[...] += dw
        dx, dw = back(zkv_ref[:, :LANE], wk_ref[...], (dk2_ref[0, 0] + dk2_ref[1, 0]).T)
        dz_ref[:, AT_W:AT_W + LANE] = dx.astype(BF16)
        dwk_ref[...] += dw
        dz_ref[:, AT_W + LANE:] = jnp.concatenate([dv2_ref[0, 0], dv2_ref[1, 0]], axis=0).T.astype(BF16)

    tab = pl.BlockSpec((tm, LANE), lambda i: (i, 0))
    vec = pl.BlockSpec((1, LANE), lambda i: (0, 0))
    return pl.pallas_call(
        body, grid=(L // tm,),
        in_specs=[pl.BlockSpec((AT_HEADS, tm, LANE), lambda i: (0, i, 0)),
                  pl.BlockSpec((AT_KV, 1, LANE, tm), lambda i: (0, i, 0, 0)),
                  pl.BlockSpec((AT_KV, 1, AT_HD, tm), lambda i: (0, i, 0, 0)),
                  pl.BlockSpec((tm, AT_W), lambda i: (i, qcol)), pl.BlockSpec((tm, 2 * LANE), lambda i: (i, kvcol)),
                  tab, tab, vec, vec],
        out_specs=[pl.BlockSpec((tm, Z_AT), lambda i: (i, 0)), vec, vec],
        out_shape=[SDS((L, Z_AT), BF16), SDS((1, LANE), F32), SDS((1, LANE), F32)],
        compiler_params=_params(("arbitrary",)), name=name)(dqm, dk2, dv2, z, z, cc, ss, wq, wk)


def _at_fwd(qt, kr, vt, *, name):
    L = kr.shape[0]
    G = AT_HEADS // AT_KV
    tq = _tile(L, 384)
    tk = PAD
    nk = L // tk
    R = G * tq
    sb = R

    def body(q_ref, k_ref, v_ref, ob_ref, of_ref, lse_ref, m_scr, acc_scr):
        i = pl.program_id(1)
        qt = jnp.concatenate([q_ref[g] for g in range(G)], axis=1)
        m_scr[...] = jnp.full_like(m_scr, NEG)
        acc_scr[...] = jnp.zeros_like(acc_scr)

        def chunk(c, masked):
            start = c * tk if isinstance(c, int) else pl.multiple_of(c * tk, tk)
            k = k_ref[pl.ds(start, tk), :]
            vt = v_ref[0, c]
            for b in range(R // sb):
                sl = slice(sb * b, sb * (b + 1))
                st = _nn(k, qt[:, sl])
                if masked:
                    key = lax.broadcasted_iota(jnp.int32, st.shape, 0)
                    st = jnp.where(key >= PAD - N_META, st, NEG)
                m_prev = m_scr[:, sl]
                m_new = jnp.maximum(m_prev, jnp.max(st, axis=0, keepdims=True))
                pt = jnp.exp(st - m_new).astype(BF16)
                acc_scr[:, sl] = jnp.exp(m_prev - m_new) * acc_scr[:, sl] + _nn(vt, pt)
                m_scr[:, sl] = m_new

        chunk(0, True)

        def loop(c, carry):
            chunk(c, False)
            return carry

        lax.fori_loop(1, nk, loop, 0)
        l = acc_scr[pl.ds(AT_HD, 1), :]
        lse = m_scr[...] + jnp.log(l)
        on = acc_scr[0:AT_HD, :] / l
        o = jnp.concatenate([on[:, g * tq:(g + 1) * tq] for g in range(G)], axis=0).T
        rowg = i * tq + lax.broadcasted_iota(jnp.int32, o.shape, 0)
        o = jnp.where(rowg >= PAD - N_META, o, 0.0)
        ob_ref[...] = o.astype(BF16)
        of_ref[...] = o
        for g in range(G):
            lse_ref[g] = lse[:, g * tq:(g + 1) * tq]

    ospec = pl.BlockSpec((tq, G * AT_HD), lambda j, i: (i, j))
    return pl.pallas_call(
        body, grid=(AT_KV, L // tq),
        in_specs=[pl.BlockSpec((G, LANE, tq), lambda j, i: (j, 0, i)), pl.BlockSpec((L, LANE), lambda j, i: (0, 0)),
                  pl.BlockSpec((1, nk, VT_ROWS, tk), lambda j, i: (j, 0, 0, 0))],
        out_specs=[ospec, ospec, pl.BlockSpec((G, 1, tq), lambda j, i: (j, 0, i))],
        out_shape=[SDS((L, AT_W), BF16), SDS((L, AT_W), F32), SDS((AT_HEADS, 1, L), F32)],
        scratch_shapes=[pltpu.VMEM((1, R), F32), pltpu.VMEM((VT_ROWS, R), F32)],
        compiler_params=_params(("parallel", "parallel")), name=name)(qt, kr, vt)


def _at_bwd(qt, kr, krt, vb, do, of, lse, *, name):
    L = kr.shape[0]
    G = AT_HEADS // AT_KV
    tq = _tile(L, 256)
    tk = PAD
    nk = L // tk
    nq = L // tq
    R = G * tq
    sb = R

    def body(q_ref, k_hbm, kt_hbm, v_hbm, do_ref, o_ref, lse_ref, dq_ref, dk_hbm, dv_hbm,
             k_scr, kt_scr, v_scr, dk_scr, dv_scr, dq_scr, sem):
        j, i = pl.program_id(0), pl.program_id(1)

        @pl.when(i == 0)
        def _():
            cps = [pltpu.make_async_copy(k_hbm, k_scr, sem.at[0]), pltpu.make_async_copy(kt_hbm, kt_scr, sem.at[1]),
                   pltpu.make_async_copy(v_hbm.at[j], v_scr, sem.at[2])]
            for cp in cps:
                cp.start()
            dk_scr[...] = jnp.zeros_like(dk_scr)
            dv_scr[...] = jnp.zeros_like(dv_scr)
            for cp in cps:
                cp.wait()

        qt = jnp.concatenate([q_ref[g] for g in range(G)], axis=1)
        rowg = i * tq + lax.broadcasted_iota(jnp.int32, (tq, G * AT_HD), 0)
        dot_all = jnp.where(rowg >= PAD - N_META, do_ref[...], 0.0).T
        ot_all = o_ref[...].T
        dot = jnp.concatenate([dot_all[AT_HD * g:AT_HD * (g + 1)] for g in range(G)], axis=1)
        ot = jnp.concatenate([ot_all[AT_HD * g:AT_HD * (g + 1)] for g in range(G)], axis=1)
        delta = jnp.sum(dot * ot, axis=0, keepdims=True)
        dotb = dot.astype(BF16)
        dot128 = jnp.concatenate([dotb, jnp.zeros_like(dotb)], axis=0)
        lse_v = jnp.concatenate([lse_ref[g] for g in range(G)], axis=1)
        dq_scr[...] = jnp.zeros_like(dq_scr)

        def chunk(c, masked):
            start = c * tk if isinstance(c, int) else pl.multiple_of(c * tk, tk)
            k = k_scr[pl.ds(start, tk), :]
            kt = kt_scr[c]
            v = v_scr[pl.ds(start, tk), :]
            dkt, dvt = None, None
            for b in range(R // sb):
                sl = slice(sb * b, sb * (b + 1))
                st = _nn(k, qt[:, sl])
                if masked:
                    key = lax.broadcasted_iota(jnp.int32, st.shape, 0)
                    st = jnp.where(key >= PAD - N_META, st, NEG)
                pt = jnp.exp(st - lse_v[:, sl])
                dst = (pt * (_nn(v, dot128[:, sl]) - delta[:, sl])).astype(BF16)
                dq_scr[:, sl] += _nn(kt, dst)
                a = _nt(qt[:, sl], dst)
                e = _nt(dotb[:, sl], pt.astype(BF16))
                dkt = a if dkt is None else dkt + a
                dvt = e if dvt is None else dvt + e
            dk_scr[c] += dkt
            dv_scr[c] += dvt

        chunk(0, True)

        def loop(c, carry):
            chunk(c, False)
            return carry

        lax.fori_loop(1, nk, loop, 0)
        dq_ref[...] = dq_scr[...].T.reshape(G, tq, LANE)

        @pl.when(i == nq - 1)
        def _():
            ck = pltpu.make_async_copy(dk_scr, dk_hbm.at[j], sem.at[0])
            cv = pltpu.make_async_copy(dv_scr, dv_hbm.at[j], sem.at[1])
            ck.start()
            cv.start()
            ck.wait()
            cv.wait()

    anyspec = pl.BlockSpec(memory_space=pl.ANY)
    ospec = pl.BlockSpec((tq, G * AT_HD), lambda j, i: (i, j))
    return pl.pallas_call(
        body, grid=(AT_KV, nq),
        in_specs=[pl.BlockSpec((G, LANE, tq), lambda j, i: (j, 0, i)), anyspec, anyspec, anyspec, ospec, ospec,
                  pl.BlockSpec((G, 1, tq), lambda j, i: (j, 0, i))],
        out_specs=[pl.BlockSpec((G, tq, LANE), lambda j, i: (j, i, 0)), anyspec, anyspec],
        out_shape=[SDS((AT_HEADS, L, LANE), F32), SDS((AT_KV, nk, LANE, tk), F32), SDS((AT_KV, nk, AT_HD, tk), F32)],
        scratch_shapes=[pltpu.VMEM((L, LANE), BF16), pltpu.VMEM((nk, LANE, tk), BF16), pltpu.VMEM((L, LANE), BF16),
                        pltpu.VMEM((nk, LANE, tk), F32), pltpu.VMEM((nk, AT_HD, tk), F32), pltpu.VMEM((LANE, R), F32),
                        pltpu.SemaphoreType.DMA((3,))],
        compiler_params=_params(("arbitrary", "arbitrary"), VMEM_LARGE), name=name)(qt, kr, krt, vb, do, of, lse)


def _merge_fwd(ya, o8, wua, wubp, z, *, name):
    L = ya.shape[0]
    D = wua.shape[1]
    tm, tn = _tile(L, 512), 256
    ga, gb = (Z_HG + Z_AT) // tn, (Z_HG + Z_AT + D) // tn

    def body(ya_ref, o8_ref, wa_ref, wb_ref, za_ref, zb_ref, mix_ref):
        pa = _nn(ya_ref[...], wa_ref[...])
        pb = _nn(o8_ref[...], wb_ref[...])
        mix_ref[...] = (_sigmoid(za_ref[...]) * pa + _sigmoid(zb_ref[...]) * pb).astype(BF16)

    return pl.pallas_call(
        body, grid=(D // tn, L // tm),
        in_specs=[pl.BlockSpec((tm, ya.shape[1]), lambda j, i: (i, 0)), pl.BlockSpec((tm, o8.shape[1]), lambda j, i: (i, 0)),
                  pl.BlockSpec((wua.shape[0], tn), lambda j, i: (0, j)), pl.BlockSpec((wubp.shape[0], tn), lambda j, i: (0, j)),
                  pl.BlockSpec((tm, tn), lambda j, i: (i, ga + j)), pl.BlockSpec((tm, tn), lambda j, i: (i, gb + j))],
        out_specs=pl.BlockSpec((tm, tn), lambda j, i: (i, j)), out_shape=SDS((L, D), BF16),
        compiler_params=_params(("parallel", "parallel")), name=name)(ya, o8, wua, wubp, z, z)


def _merge_bwd(dh, wout, ya, o8, wua, wubp, z, *, name):
    L = ya.shape[0]
    D = wua.shape[1]
    tm, tn = _tile(L, 512), 256
    ga, gb = (Z_HG + Z_AT) // tn, (Z_HG + Z_AT + D) // tn

    def body(dh_ref, wo_ref, ya_ref, o8_ref, wa_ref, wb_ref, za_ref, zb_ref, dpa_ref, dpb_ref, dza_ref, dzb_ref):
        dm = _nt(dh_ref[...].astype(BF16), wo_ref[...])
        pa = _nn(ya_ref[...], wa_ref[...])
        pb = _nn(o8_ref[...], wb_ref[...])
        sa, sb = _sigmoid(za_ref[...]), _sigmoid(zb_ref[...])
        dpa_ref[...] = (dm * sa).astype(BF16)
        dpb_ref[...] = (dm * sb).astype(BF16)
        dza_ref[...] = (dm * pa * sa * (1.0 - sa)).astype(BF16)
        dzb_ref[...] = (dm * pb * sb * (1.0 - sb)).astype(BF16)

    ospec = pl.BlockSpec((tm, tn), lambda j, i: (i, j))
    return pl.pallas_call(
        body, grid=(D // tn, L // tm),
        in_specs=[pl.BlockSpec((tm, D), lambda j, i: (i, 0)), pl.BlockSpec((tn, D), lambda j, i: (j, 0)),
                  pl.BlockSpec((tm, ya.shape[1]), lambda j, i: (i, 0)), pl.BlockSpec((tm, o8.shape[1]), lambda j, i: (i, 0)),
                  pl.BlockSpec((wua.shape[0], tn), lambda j, i: (0, j)), pl.BlockSpec((wubp.shape[0], tn), lambda j, i: (0, j)),
                  pl.BlockSpec((tm, tn), lambda j, i: (i, ga + j)), pl.BlockSpec((tm, tn), lambda j, i: (i, gb + j))],
        out_specs=[ospec] * 4, out_shape=[SDS((L, D), BF16)] * 4,
        compiler_params=_params(("parallel", "parallel")), name=name)(dh, wout, ya, o8, wua, wubp, z, z)


def _loss_head(h, tgt, *, name):
    L, D = h.shape
    tm = PAD

    def body(h_ref, t_ref, dh_ref, ls_ref):
        i = pl.program_id(0)

        @pl.when(i == 0)
        def _():
            ls_ref[...] = jnp.zeros_like(ls_ref)
            dh_ref[...] = jnp.zeros_like(dh_ref)

        @pl.when(i > 0)
        def _():
            e = h_ref[...] - t_ref[...]
            dh_ref[...] = e * (1.0 / D)
            s = jnp.sum(e * e, axis=0, keepdims=True)
            tot = s[:, :LANE]
            for c in range(1, D // LANE):
                tot = tot + s[:, LANE * c:LANE * (c + 1)]
            ls_ref[...] += tot

    return pl.pallas_call(
        body, grid=(L // tm,),
        in_specs=[pl.BlockSpec((tm, D), lambda i: (i, 0)), pl.BlockSpec((tm, D), lambda i: (jnp.maximum(i - 1, 0), 0))],
        out_specs=[pl.BlockSpec((tm, D), lambda i: (i, 0)), pl.BlockSpec((1, LANE), lambda i: (0, 0))],
        out_shape=[SDS((L, D), F32), SDS((1, LANE), F32)],
        compiler_params=_params(("arbitrary",)), name=name)(h, tgt)


def _adamw(w, g, m, v, *, name):
    shape = w.shape
    w2, g2, m2, v2 = [a.reshape(-1, shape[-1]) for a in (w, g, m, v)]
    rows, cols = w2.shape
    tr = _tile(rows, 256, 8)

    def body(w_ref, g_ref, m_ref, v_ref, d_ref, nm_ref, nv_ref):
        g = g_ref[...]
        m = ADAM_B1 * m_ref[...] + (1.0 - ADAM_B1) * g
        v = ADAM_B2 * v_ref[...] + (1.0 - ADAM_B2) * (g * g)
        m_hat = m / (1.0 - ADAM_B1 ** ADAM_STEP)
        v_hat = v / (1.0 - ADAM_B2 ** ADAM_STEP)
        d_ref[...] = -ADAM_LR * (m_hat / (jnp.sqrt(v_hat) + ADAM_EPS) + ADAM_WD * w_ref[...])
        nm_ref[...] = m
        nv_ref[...] = v

    spec = pl.BlockSpec((tr, cols), lambda i: (i, 0))
    outs = pl.pallas_call(
        body, grid=(rows // tr,), in_specs=[spec] * 4, out_specs=[spec] * 3, out_shape=[SDS((rows, cols), F32)] * 3,
        compiler_params=_params(("parallel",)), name=name)(w2, g2, m2, v2)
    return [o.reshape(shape) for o in outs]


def _sum_slabs(x, *, name):
    n, R, _ = x.shape
    tr = _tile(R, 2048, 8)

    def body(x_ref, o_ref):
        tot = x_ref[0]
        for s in range(1, n):
            tot = tot + x_ref[s]
        o_ref[...] = tot

    return pl.pallas_call(
        body, grid=(R // tr,), in_specs=[pl.BlockSpec((n, tr, LANE), lambda i: (0, i, 0))],
        out_specs=pl.BlockSpec((tr, LANE), lambda i: (i, 0)), out_shape=SDS((R, LANE), F32),
        compiler_params=_params(("parallel",)), name=name)(x)


def _add_pair(a, b, *, name):
    n, R, _ = a.shape
    tr = _tile(R, 2048, 8)

    def body(a_ref, b_ref, o_ref):
        o_ref[...] = a_ref[...] + b_ref[...]

    spec = pl.BlockSpec((1, tr, LANE), lambda s, i: (s, i, 0))
    return pl.pallas_call(
        body, grid=(n, R // tr), in_specs=[spec, spec], out_specs=spec, out_shape=SDS(a.shape, F32),
        compiler_params=_params(("parallel", "parallel")), name=name)(a, b)


def _place():
    return lax.axis_index("x"), lax.axis_index("y"), lax.axis_index("c")


def _allgather_small(v, *, name):
    m_per, n = v.shape

    def body(x_ref, out_ref, send_sems, recv_sems, local_sem):
        x, y, c = _place()
        me, sibling = (x, y, c), (x, y, 1 - c)
        chips = [(1 - x, y), (x, 1 - y), (1 - x, 1 - y)]

        def rows(px, py, pc):
            return out_ref.at[pl.ds((4 * px + 2 * py + pc) * m_per, m_per), :]

        def copy(k, block, to, src=None):
            return pltpu.make_async_remote_copy(
                src_ref=rows(*block) if src is None else src, dst_ref=rows(*block),
                send_sem=send_sems.at[k], recv_sem=recv_sems.at[k], device_id=to, device_id_type=MESH)

        mine = pltpu.make_async_copy(x_ref, rows(*me), local_sem)
        mine.start()
        first = [copy(0, me, sibling, src=x_ref)]
        first += [copy(1 + j, me, (*chip, c), src=x_ref) for j, chip in enumerate(chips)]
        for cp in first:
            cp.start()
        passed = [copy(4 + j, (*chip, c), sibling) for j, chip in enumerate(chips)]
        for j, chip in enumerate(chips):
            copy(1 + j, (*chip, c), me).wait_recv()
            passed[j].start()
        copy(0, sibling, me).wait_recv()
        for j, chip in enumerate(chips):
            copy(4 + j, (*chip, 1 - c), me).wait_recv()
        for cp in first + passed:
            cp.wait_send()
        mine.wait()

    return pl.pallas_call(
        body, out_shape=SDS((8 * m_per, n), v.dtype),
        in_specs=[pl.BlockSpec(memory_space=pltpu.VMEM)], out_specs=pl.BlockSpec(memory_space=pltpu.VMEM),
        scratch_shapes=[pltpu.SemaphoreType.DMA((7,)), pltpu.SemaphoreType.DMA((7,)), pltpu.SemaphoreType.DMA],
        name=name)(v)


def _gather_weights(wp, *, name):
    _, R, _ = wp.shape

    def body(w_ref, out_ref, send_sems, recv_sems, local_sem):
        x, y, c = _place()
        sibling = (x, y, 1 - c)
        chips = [(1 - x, y), (x, 1 - y), (1 - x, 1 - y)]

        def slot(px, py, half):
            return out_ref.at[2 * px + py, half]

        def copy(k, src, dst, to):
            return pltpu.make_async_remote_copy(src_ref=src, dst_ref=dst, send_sem=send_sems.at[k],
                                                recv_sem=recv_sems.at[k], device_id=to, device_id_type=MESH)

        mine = pltpu.make_async_copy(w_ref, out_ref.at[2 * x + y], local_sem)
        mine.start()
        first = [copy(j, w_ref.at[c], slot(x, y, c), (*chip, c)) for j, chip in enumerate(chips)]
        for cp in first:
            cp.start()
        passed = [copy(3 + j, slot(*chip, c), slot(*chip, c), sibling) for j, chip in enumerate(chips)]
        for j, chip in enumerate(chips):
            copy(j, w_ref.at[c], slot(*chip, c), (*chip, c)).wait_recv()
            passed[j].start()
        for j, chip in enumerate(chips):
            copy(3 + j, slot(*chip, 1 - c), slot(*chip, 1 - c), sibling).wait_recv()
        for cp in first + passed:
            cp.wait_send()
        mine.wait()

    anyspec = pl.BlockSpec(memory_space=pl.ANY)
    return pl.pallas_call(
        body, out_shape=SDS((4, 2, R, LANE), wp.dtype), in_specs=[anyspec], out_specs=anyspec,
        scratch_shapes=[pltpu.SemaphoreType.DMA((6,)), pltpu.SemaphoreType.DMA((6,)), pltpu.SemaphoreType.DMA],
        name=name)(wp)


def _pair_exchange(g, *, name):
    _, n, R, _ = g.shape

    def body(g_ref, out_ref, send_sem, recv_sem):
        x, y, c = _place()
        cp = pltpu.make_async_remote_copy(src_ref=g_ref.at[1 - c], dst_ref=out_ref, send_sem=send_sem,
                                          recv_sem=recv_sem, device_id=(x, y, 1 - c), device_id_type=MESH)
        cp.start()
        cp.wait()

    anyspec = pl.BlockSpec(memory_space=pl.ANY)
    return pl.pallas_call(
        body, out_shape=SDS((n, R, LANE), g.dtype), in_specs=[anyspec], out_specs=anyspec,
        scratch_shapes=[pltpu.SemaphoreType.DMA, pltpu.SemaphoreType.DMA], name=name)(g)


def _chip_exchange(part, *, name):
    n, R, _ = part.shape

    def body(p_ref, out_ref, send_sems, recv_sems, local_sem):
        x, y, c = _place()
        s_me = 2 * x + y
        chips = [(1 - x, y), (x, 1 - y), (1 - x, 1 - y)]

        def copy(k, chip):
            return pltpu.make_async_remote_copy(
                src_ref=p_ref.at[2 * chip[0] + chip[1]], dst_ref=out_ref.at[s_me], send_sem=send_sems.at[k],
                recv_sem=recv_sems.at[k], device_id=(*chip, c), device_id_type=MESH)

        def landed(k, chip):
            return pltpu.make_async_remote_copy(
                src_ref=p_ref.at[s_me], dst_ref=out_ref.at[2 * chip[0] + chip[1]], send_sem=send_sems.at[k],
                recv_sem=recv_sems.at[k], device_id=(*chip, c), device_id_type=MESH)

        mine = pltpu.make_async_copy(p_ref.at[s_me], out_ref.at[s_me], local_sem)
        mine.start()
        sends = [copy(k, chip) for k, chip in enumerate(chips)]
        for cp in sends:
            cp.start()
        for k, chip in enumerate(chips):
            landed(k, chip).wait_recv()
        for cp in sends:
            cp.wait_send()
        mine.wait()

    anyspec = pl.BlockSpec(memory_space=pl.ANY)
    return pl.pallas_call(
        body, out_shape=SDS((n, R, LANE), part.dtype), in_specs=[anyspec], out_specs=anyspec,
        scratch_shapes=[pltpu.SemaphoreType.DMA((3,)), pltpu.SemaphoreType.DMA((3,)), pltpu.SemaphoreType.DMA],
        name=name)(part)


def _pair_share(red, *, name):
    R, _ = red.shape

    def body(r_ref, out_ref, send_sem, recv_sem, local_sem):
        x, y, c = _place()
        mine = pltpu.make_async_copy(r_ref, out_ref.at[c], local_sem)
        mine.start()
        cp = pltpu.make_async_remote_copy(src_ref=r_ref, dst_ref=out_ref.at[c], send_sem=send_sem,
                                          recv_sem=recv_sem, device_id=(x, y, 1 - c), device_id_type=MESH)
        cp.start()
        pltpu.make_async_remote_copy(src_ref=r_ref, dst_ref=out_ref.at[1 - c], send_sem=send_sem,
                                     recv_sem=recv_sem, device_id=(x, y, 1 - c), device_id_type=MESH).wait_recv()
        cp.wait_send()
        mine.wait()

    anyspec = pl.BlockSpec(memory_space=pl.ANY)
    return pl.pallas_call(
        body, out_shape=SDS((2, R, LANE), red.dtype), in_specs=[anyspec], out_specs=anyspec,
        scratch_shapes=[pltpu.SemaphoreType.DMA, pltpu.SemaphoreType.DMA, pltpu.SemaphoreType.DMA], name=name)(red)


def _chips(x, y):
    return [(1 - x, y), (x, 1 - y), (1 - x, 1 - y)]


def _gather_mats(shards, *, name):
    n = len(shards)

    def body(*refs):
        ins, outs = refs[:n], refs[n:2 * n]
        send_sems, recv_sems, fsend_sems, frecv_sems = refs[2 * n:]
        x, y, c = _place()
        s_me, sibling, chips = 2 * x + y, (x, y, 1 - c), _chips(x, y)

        def copy(src, dst, ssem, rsem, to):
            return pltpu.make_async_remote_copy(src_ref=src, dst_ref=dst, send_sem=ssem, recv_sem=rsem,
                                                device_id=to, device_id_type=MESH)

        first = [copy(ins[t].at[c], outs[t].at[s_me, c], send_sems.at[3 * t + k], recv_sems.at[3 * t + k], (*chip, c))
                 for t in range(n) for k, chip in enumerate(chips)]
        for cp in first:
            cp.start()
        passed = []
        for t in range(n):
            for k, chip in enumerate(chips):
                slot = outs[t].at[2 * chip[0] + chip[1], c]
                copy(ins[t].at[c], slot, send_sems.at[3 * t + k], recv_sems.at[3 * t + k], (*chip, c)).wait_recv()
                fw = copy(slot, slot, fsend_sems.at[3 * t + k], frecv_sems.at[3 * t + k], sibling)
                fw.start()
                passed.append(fw)
        for t in range(n):
            for k, chip in enumerate(chips):
                slot = outs[t].at[2 * chip[0] + chip[1], 1 - c]
                copy(slot, slot, fsend_sems.at[3 * t + k], frecv_sems.at[3 * t + k], sibling).wait_recv()
        for cp in first + passed:
            cp.wait_send()

    anyspec = pl.BlockSpec(memory_space=pl.ANY)
    return pl.pallas_call(
        body, out_shape=[SDS((4,) + s.shape, s.dtype) for s in shards], in_specs=[anyspec] * n, out_specs=[anyspec] * n,
        scratch_shapes=[pltpu.SemaphoreType.DMA((3 * n,))] * 4, name=name)(*shards)


def _rs_pair_exchange(gs, *, name):
    n = len(gs)

    def body(*refs):
        ins, outs = refs[:n], refs[n:2 * n]
        send_sems, recv_sems = refs[2 * n:]
        x, y, c = _place()
        cps = [pltpu.make_async_remote_copy(src_ref=ins[t].at[k, 1 - c], dst_ref=outs[t].at[k],
                                            send_sem=send_sems.at[4 * t + k], recv_sem=recv_sems.at[4 * t + k],
                                            device_id=(x, y, 1 - c), device_id_type=MESH)
               for t in range(n) for k in range(4)]
        for cp in cps:
            cp.start()
        for cp in cps:
            cp.wait()

    anyspec = pl.BlockSpec(memory_space=pl.ANY)
    return pl.pallas_call(
        body, out_shape=[SDS((4,) + g.shape[2:], g.dtype) for g in gs], in_specs=[anyspec] * n, out_specs=[anyspec] * n,
        scratch_shapes=[pltpu.SemaphoreType.DMA((4 * n,))] * 2, name=name)(*gs)


def _rs_chip_exchange(parts, *, name):
    n = len(parts)

    def body(*refs):
        ins, outs = refs[:n], refs[n:2 * n]
        send_sems, recv_sems, local_sems = refs[2 * n:]
        x, y, c = _place()
        s_me, chips = 2 * x + y, _chips(x, y)

        def copy(t, k, chip, src_slot, dst_slot):
            return pltpu.make_async_remote_copy(
                src_ref=ins[t].at[src_slot], dst_ref=outs[t].at[dst_slot], send_sem=send_sems.at[3 * t + k],
                recv_sem=recv_sems.at[3 * t + k], device_id=(*chip, c), device_id_type=MESH)

        mine = [pltpu.make_async_copy(ins[t].at[s_me], outs[t].at[s_me], local_sems.at[t]) for t in range(n)]
        for cp in mine:
            cp.start()
        sends = [copy(t, k, chip, 2 * chip[0] + chip[1], s_me) for t in range(n) for k, chip in enumerate(chips)]
        for cp in sends:
            cp.start()
        for t in range(n):
            for k, chip in enumerate(chips):
                copy(t, k, chip, s_me, 2 * chip[0] + chip[1]).wait_recv()
        for cp in sends:
            cp.wait_send()
        for cp in mine:
            cp.wait()

    anyspec = pl.BlockSpec(memory_space=pl.ANY)
    return pl.pallas_call(
        body, out_shape=[SDS(p.shape, p.dtype) for p in parts], in_specs=[anyspec] * n, out_specs=[anyspec] * n,
        scratch_shapes=[pltpu.SemaphoreType.DMA((3 * n,))] * 2 + [pltpu.SemaphoreType.DMA((n,))], name=name)(*parts)


def _rs_pair_share(fulls, *, name):
    n = len(fulls)

    def body(*refs):
        ins, outs = refs[:n], refs[n:2 * n]
        send_sems, recv_sems = refs[2 * n:]
        x, y, c = _place()

        def copy(t, half):
            return pltpu.make_async_remote_copy(src_ref=ins[t].at[c], dst_ref=outs[t].at[half], send_sem=send_sems.at[t],
                                                recv_sem=recv_sems.at[t], device_id=(x, y, 1 - c), device_id_type=MESH)

        sends = [copy(t, c) for t in range(n)]
        for cp in sends:
            cp.start()
        for t in range(n):
            copy(t, 1 - c).wait_recv()
        for cp in sends:
            cp.wait_send()

    anyspec = pl.BlockSpec(memory_space=pl.ANY)
    return pl.pallas_call(
        body, out_shape=[SDS(f.shape, f.dtype) for f in fulls], in_specs=[anyspec] * n, out_specs=[anyspec] * n,
        input_output_aliases={t: t for t in range(n)},
        scratch_shapes=[pltpu.SemaphoreType.DMA((n,))] * 2, name=name)(*fulls)


def _add_half(g, other, c1, *, out_dtype, name):
    _, _, h, cs = g.shape
    tr = _tile(h, 512, 16)

    def body(c_ref, g_ref, o_ref, out_ref):
        out_ref[...] = (g_ref[...] + o_ref[...]).astype(out_dtype)

    spec = pl.BlockSpec((None, tr, cs), lambda k, i, c: (k, i, 0))
    return pl.pallas_call(
        body, out_shape=SDS(other.shape, out_dtype),
        grid_spec=pltpu.PrefetchScalarGridSpec(
            num_scalar_prefetch=1, grid=(4, h // tr),
            in_specs=[pl.BlockSpec((None, None, tr, cs), lambda k, i, c: (k, c[0], i, 0)), spec], out_specs=spec),
        compiler_params=_params(("parallel", "parallel")), name=name)(c1, g, other)


def _sum4(x, c1, *, name):
    n, h, cs = x.shape
    tr = _tile(h, 512, 16)

    def body(c_ref, x_ref, o_ref):
        tot = x_ref[0].astype(F32)
        for s in range(1, n):
            tot = tot + x_ref[s].astype(F32)
        o_ref[...] = tot

    return pl.pallas_call(
        body, out_shape=SDS((2, h, cs), F32),
        grid_spec=pltpu.PrefetchScalarGridSpec(
            num_scalar_prefetch=1, grid=(h // tr,),
            in_specs=[pl.BlockSpec((n, tr, cs), lambda i, c: (0, i, 0))],
            out_specs=pl.BlockSpec((None, tr, cs), lambda i, c: (c[0], i, 0))),
        compiler_params=_params(("parallel",)), name=name)(c1, x)


def _finish_small(gathered, lbf, lbb, *, rows, name):
    r_lbf, r_lbb = rows['lb_f'], rows['lb_b']

    def body(g_ref, lbf_ref, lbb_ref, o_ref, dlf_ref, dlb_ref):
        tot = g_ref[0]
        for s in range(1, 8):
            tot = tot + g_ref[s]
        o_ref[...] = tot
        o_ref[0:1, :] = jnp.broadcast_to(jnp.sum(o_ref[0:1, :], axis=1, keepdims=True), (1, LANE))
        for lb_ref, d_ref, r0 in ((lbf_ref, dlf_ref, r_lbf), (lbb_ref, dlb_ref, r_lbb)):
            for hh in range(HG_HEADS):
                sl = slice(LANE * hh, LANE * (hh + 1))
                l0, l1 = lb_ref[0:1, sl], lb_ref[1:2, sl]
                mx = jnp.maximum(l0, l1)
                e0, e1 = jnp.exp(l0 - mx), jnp.exp(l1 - mx)
                p0 = e0 / (e0 + e1)
                d0 = o_ref[r0 + hh:r0 + hh + 1, :] * p0 * (1.0 - p0)
                d_ref[0:1, sl] = d0
                d_ref[1:2, sl] = -d0

    vm = pl.BlockSpec(memory_space=pltpu.VMEM)
    return pl.pallas_call(
        body, in_specs=[vm, vm, vm], out_specs=[vm, vm, vm],
        out_shape=[SDS(gathered.shape[1:], F32), SDS(lbf.shape, F32), SDS(lbb.shape, F32)], name=name)(gathered, lbf, lbb)


def _pad_len(n):
    q = 2 * 128 * LANE
    return (n + q - 1) // q * q


def _pack_local(shards, dtype):
    flat = jnp.concatenate([s.astype(dtype).reshape(-1) for s in shards])
    n = flat.shape[0]
    flat = jnp.pad(flat, (0, _pad_len(n) - n))
    return flat.reshape(2, -1, LANE)


def _unpack_full(gathered, shapes, names):
    out, off = {}, 0
    for name in names:
        r, cs = shapes[name]
        blk = gathered[:, off:off + r * cs].reshape(4, r, cs)
        off += r * cs
        out[name] = blk.reshape(4 * r, cs) if name in ROW_SHARDED else blk.transpose(1, 0, 2).reshape(r, 4 * cs)
    return out


def _pack_grads(grads, shapes, names):
    cols = []
    for name in names:
        r, cs = shapes[name]
        g = grads[name]
        blk = g.reshape(4, r * cs) if name in ROW_SHARDED else g.reshape(r, 4, cs).transpose(1, 0, 2).reshape(4, r * cs)
        cols.append(blk)
    flat = jnp.concatenate(cols, axis=1)
    n = flat.shape[1]
    flat = jnp.pad(flat, ((0, 0), (0, _pad_len(n) - n)))
    return flat.reshape(4, 2, -1, LANE).transpose(1, 0, 2, 3)


def _unpack_shard(flat, shapes, names):
    out, off = {}, 0
    for name in names:
        r, cs = shapes[name]
        out[name] = flat[off:off + r * cs].reshape(r, cs)
        off += r * cs
    return out


def _local_step(x2, tgt2, meta, W, S):
    T, D = x2.shape
    L = PAD + T
    h0 = jnp.concatenate([jnp.zeros((PAD - N_META, D), F32), meta, x2], axis=0)

    qk0 = Z_HG
    w_in = jnp.concatenate([W['w_in'][:, :qk0], _qk_to_group(W['w_in'][:, qk0:qk0 + AT_W + AT_KVW]),
                            W['w_in'][:, qk0 + AT_W + AT_KVW:]], axis=1)
    cc, ss = _rope_tables(L)
    wq_g, wk_g = _group_vec(S['q_norm']), _group_vec(S['k_norm'])

    def ffn_fwd(h, nw, wg, wu, wd, tag):
        n = _rmsnorm_fwd(h, nw, name=tag + "_norm")
        g, u, a = _ffn4_up(n, wg, wu, name=tag + "_up")
        hn = _ffn4_down(a, wd, h, name=tag + "_down")
        return hn, (n, g, u, a)

    def ffn_bwd(dh, h, nw, wg, wu, wd, saved, tag):
        n, g, u, a = saved
        dg, du = _ffn4_dact(dh, wd, g, u, name=tag + "_dact")
        dn = _ffn4_dn(dg, du, wg, wu, name=tag + "_dn")
        dwg = _ffn4_dw(n, dg, x_is_rows=True, name=tag + "_dwg")
        dwu = _ffn4_dw(n, du, x_is_rows=True, name=tag + "_dwu")
        dwd = _ffn4_dw(dh, a, x_is_rows=False, alpha=0.5, name=tag + "_dwd")
        dhp, dnw = _rmsnorm_bwd(h, nw, dn, dh, name=tag + "_norm_bwd")
        return dhp, dnw, dwg, dwu, dwd

    h1, sv1 = ffn_fwd(h0, S['ffn1_norm'], W['ffn1_w_gate'], W['ffn1_w_up'], W['ffn1_w_down'], "ffn1")
    um = _rmsnorm_fwd(h1, S['mix_norm'], name="mix_norm")
    z = _mm([(um, w_in)], tm=512, tn=1792, tk=D, name="in_proj")
    of, sf = _hg_fwd(z, S['hg_lb_fwd'], rev=False, name="hg_fwd_f")
    ob, sb = _hg_fwd(z, S['hg_lb_bwd'], rev=True, name="hg_fwd_b")
    ya = _hg_post_fwd(of, ob, z, S['hg_out_norm'], name="hg_post")
    qt, kr, krt, vb, vt = _at_prep(z, cc, ss, wq_g, wk_g, name="at_prep")
    yb, yb_f32, lse = _at_fwd(qt, kr, vt, name="at_fwd")
    mixed = _merge_fwd(ya, yb, W['w_up_a'], W['w_up_b'], z, name="merge")
    h2 = _mm([(mixed, W['w_out'])], res=h1, tm=512, tn=D, tk=D, name="out_proj")
    h3, sv2 = ffn_fwd(h2, S['ffn2_norm'], W['ffn2_w_gate'], W['ffn2_w_up'], W['ffn2_w_down'], "ffn2")
    dh3, loss_lanes = _loss_head(h3, tgt2, name="loss_head")

    G = {}
    dh2, dn_ffn2, G['ffn2_w_gate'], G['ffn2_w_up'], G['ffn2_w_down'] = ffn_bwd(
        dh3, h2, S['ffn2_norm'], W['ffn2_w_gate'], W['ffn2_w_up'], W['ffn2_w_down'], sv2, "ffn2")
    dpa, dpb, dzga, dzgb = _merge_bwd(dh2, W['w_out'], ya, yb, W['w_up_a'], W['w_up_b'], z, name="merge_bwd")
    G['w_out'] = _mm([(mixed, dh2)], ta=True, tm=D, tn=D, tk=512, name="d_w_out")
    dya = _mm([(dpa, W['w_up_a'])], tb=True, tm=512, tn=HG_W, tk=D, name="d_ya")
    dyb = _mm([(dpb, W['w_up_b'])], tb=True, tm=512, tn=AT_W, tk=D, name="d_yb")
    G['w_up_a'] = _mm([(ya, dpa)], ta=True, tm=HG_W, tn=D, tk=512, name="d_w_up_a")
    G['w_up_b'] = _mm([(yb, dpb)], ta=True, tm=AT_W, tn=D, tk=512, name="d_w_up_b")
    do_hg, dzg, d_hgn = _hg_post_bwd(dya, of, ob, z, S['hg_out_norm'], name="hg_post_bwd")
    dq_f, dv_f, dzf_f, dlb_f = _hg_bwd(z, S['hg_lb_fwd'], do_hg, sf, None, rev=False, name="hg_bwd_f")
    dzq, dzi, dzf_b, dlb_b = _hg_bwd(z, S['hg_lb_bwd'], do_hg, sb, (dq_f, dv_f), rev=True, name="hg_bwd_b")
    dqm, dk2, dv2 = _at_bwd(qt, kr, krt, vb, dyb, yb_f32, lse, name="at_bwd")
    dz_at, dwq_g, dwk_g = _at_prep_bwd(dqm, dk2, dv2, z, cc, ss, wq_g, wk_g, name="at_prep_bwd")
    dz = jnp.concatenate([dzq, dzi, dzf_f, dzf_b, dzg, dz_at, dzga, dzgb], axis=1)
    dum = _mm([(dz, w_in)], tb=True, tm=512, tn=D, tk=1792, name="d_um")
    dw_in_p = _mm([(um, dz)], ta=True, tm=D, tn=1792, tk=512, name="d_w_in")
    G['w_in'] = jnp.concatenate([dw_in_p[:, :qk0], _qk_from_group(dw_in_p[:, qk0:qk0 + AT_W + AT_KVW]),
                                 dw_in_p[:, qk0 + AT_W + AT_KVW:]], axis=1)
    dh1, dn_mix = _rmsnorm_bwd(h1, S['mix_norm'], dum, dh2, name="mix_norm_bwd")
    dh0, dn_ffn1, G['ffn1_w_gate'], G['ffn1_w_up'], G['ffn1_w_down'] = ffn_bwd(
        dh1, h0, S['ffn1_norm'], W['ffn1_w_gate'], W['ffn1_w_up'], W['ffn1_w_down'], sv1, "ffn1")

    small_rows = [('loss', loss_lanes), ('ffn1_norm', dn_ffn1.reshape(-1, LANE)), ('mix_norm', dn_mix.reshape(-1, LANE)),
                  ('ffn2_norm', dn_ffn2.reshape(-1, LANE)), ('hg_out_norm', d_hgn.reshape(-1, LANE)),
                  ('lb_f', dlb_f.reshape(-1, LANE)), ('lb_b', dlb_b.reshape(-1, LANE)), ('q_norm', dwq_g), ('k_norm', dwk_g)]
    return dh0[PAD:], dh0[PAD - N_META:PAD], G, small_rows


def kernel(x, meta_tokens, ffn1_norm, ffn1_w_gate, ffn1_w_up, ffn1_w_down, mix_norm, w_in, hg_lb_fwd, hg_lb_bwd, hg_out_norm, q_norm, k_norm, w_up_a, w_up_b, w_out, ffn2_norm, ffn2_w_gate, ffn2_w_up, ffn2_w_down, loss_target, m_meta_tokens, m_ffn1_norm, m_ffn1_w_gate, m_ffn1_w_up, m_ffn1_w_down, m_mix_norm, m_w_in, m_hg_lb_fwd, m_hg_lb_bwd, m_hg_out_norm, m_q_norm, m_k_norm, m_w_up_a, m_w_up_b, m_w_out, m_ffn2_norm, m_ffn2_w_gate, m_ffn2_w_up, m_ffn2_w_down, v_meta_tokens, v_ffn1_norm, v_ffn1_w_gate, v_ffn1_w_up, v_ffn1_w_down, v_mix_norm, v_w_in, v_hg_lb_fwd, v_hg_lb_bwd, v_hg_out_norm, v_q_norm, v_k_norm, v_w_up_a, v_w_up_b, v_w_out, v_ffn2_norm, v_ffn2_w_gate, v_ffn2_w_up, v_ffn2_w_down):
    given = dict(locals())
    w = {n: given[n] for n in WEIGHTS}
    mom = {n: given["m_" + n] for n in WEIGHTS}
    var = {n: given["v_" + n] for n in WEIGHTS}
    c = lax.axis_index("c")
    D = x.shape[-1]

    shapes = {n: w[n].shape[-2:] for n in MATS + ('meta_tokens',)}
    halves = [w[n].astype(BF16).reshape(2, shapes[n][0] // 2, shapes[n][1]) for n in MATS]
    gathered = _gather_mats(halves, name="gather_weights")
    s_me = 2 * lax.axis_index("x") + lax.axis_index("y")
    W = {}
    for n, hv, g4 in zip(MATS, halves, gathered):
        r, cs = shapes[n]
        g4 = lax.dynamic_update_index_in_dim(g4, hv, s_me, 0).reshape(4, r, cs)
        if n in FFN_MATS:
            W[n] = g4
        elif n in ROW_SHARDED:
            W[n] = g4.reshape(4 * r, cs)
        else:
            W[n] = g4.transpose(1, 0, 2).reshape(r, 4 * cs)
    meta_rows = w['meta_tokens'].reshape(-1, LANE)
    mg = _allgather_small(meta_rows, name="gather_meta").reshape(4, 2, N_META, -1)[:, 0]
    meta = mg.transpose(1, 0, 2).reshape(N_META, D)
    S = {n: w[n] for n in SMALLS}

    grad_x, dmeta, G, small_rows = _local_step(x[0], loss_target[0], meta, W, S)
    G['meta_tokens'] = dmeta

    names = MATS + ('meta_tokens',)
    views = []
    for n in names:
        r, cs = shapes[n]
        if n in FFN_MATS:
            g4 = G[n]
        elif n in ROW_SHARDED:
            g4 = G[n].reshape(4, r, cs)
        else:
            g4 = G[n].reshape(r, 4, cs).transpose(1, 0, 2)
        views.append(g4.reshape(4, 2, r // 2, cs))
    c1 = c.astype(jnp.int32).reshape(1)
    from_sibling = _rs_pair_exchange(views, name="rs_pair_exchange")
    parts = [_add_half(v, o, c1, out_dtype=F32 if n == 'meta_tokens' else BF16, name="rs_pair_sum_" + n)
             for n, v, o in zip(names, views, from_sibling)]
    slabs = _rs_chip_exchange(parts, name="rs_chip_exchange")
    reds = [_sum4(s, c1, name="rs_chip_sum_" + n) for n, s in zip(names, slabs)]
    both = _rs_pair_share(reds, name="rs_pair_share")
    grads = {n: b.reshape(w[n].shape) for n, b in zip(names, both)}

    rows, off = {}, 0
    for nme, blk in small_rows:
        rows[nme] = off
        off += blk.shape[0]
    block = jnp.concatenate([blk for _, blk in small_rows], axis=0)
    n_rows = (off + 7) // 8 * 8
    block = jnp.pad(block, ((0, n_rows - off), (0, 0)))
    allsmall = _allgather_small(block, name="gather_small").reshape(8, n_rows, LANE)
    tot, d_lbf, d_lbb = _finish_small(allsmall, w['hg_lb_fwd'], w['hg_lb_bwd'], rows=rows, name="finish_small")
    loss = 0.5 * tot[0, 0] / D

    def small(nme, shape):
        r0 = rows[nme]
        return tot[r0:r0 + shape[-1] // LANE].reshape(shape)

    grads['ffn1_norm'] = small('ffn1_norm', w['ffn1_norm'].shape)
    grads['mix_norm'] = small('mix_norm', w['mix_norm'].shape)
    grads['ffn2_norm'] = small('ffn2_norm', w['ffn2_norm'].shape)
    grads['hg_out_norm'] = small('hg_out_norm', w['hg_out_norm'].shape)
    grads['hg_lb_fwd'] = d_lbf
    grads['hg_lb_bwd'] = d_lbb
    grads['q_norm'] = _ungroup_vec(tot[rows['q_norm']])
    grads['k_norm'] = _ungroup_vec(tot[rows['k_norm']])

    delta, new_m, new_v = {}, {}, {}
    for n in WEIGHTS:
        delta[n], new_m[n], new_v[n] = _adamw(w[n], grads[n], mom[n], var[n], name="adamw_" + n)
    return (loss, grad_x[None], *[grads[n] for n in WEIGHTS], *[delta[n] for n in WEIGHTS],
            *[new_m[n] for n in WEIGHTS], *[new_v[n] for n in WEIGHTS])
```

```python
import functools
import math

import numpy as np
import jax
import jax.numpy as jnp
from jax import lax
from jax.experimental import pallas as pl
from jax.experimental.pallas import tpu as pltpu

F32 = jnp.float32
BF16 = jnp.bfloat16
SDS = jax.ShapeDtypeStruct
MESH = pl.DeviceIdType.MESH

EPS = 1e-6
N_META = 16
PAD = 512
LANE = 128
CHUNK = 128
HG_HEADS = 4
HG_W = HG_HEADS * 128
AT_HEADS = 8
AT_KV = 2
AT_HD = 64
AT_W = AT_HEADS * AT_HD
AT_KVW = AT_KV * AT_HD
VT_ROWS = AT_HD + 16
FWD_CHUNKS_PER_STEP = 2
GRID_W = 64
ROPE_THETA = 10000.0
Z_HG = 5 * HG_W
Z_AT = AT_W + 2 * AT_KVW
ADAM_LR, ADAM_B1, ADAM_B2, ADAM_EPS, ADAM_WD, ADAM_STEP = 0.001, 0.9, 0.999, 1e-08, 0.01, 10
VMEM_DEFAULT = 48 * 1024 * 1024
VMEM_LARGE = 60 * 1024 * 1024
NEG = -1e30

MATS = ('ffn1_w_gate', 'ffn1_w_up', 'ffn1_w_down', 'w_in', 'w_up_a', 'w_up_b', 'w_out',
        'ffn2_w_gate', 'ffn2_w_up', 'ffn2_w_down')
ROW_SHARDED = ('ffn1_w_down', 'w_out', 'ffn2_w_down')
FFN_MATS = ('ffn1_w_gate', 'ffn1_w_up', 'ffn1_w_down', 'ffn2_w_gate', 'ffn2_w_up', 'ffn2_w_down')
SMALLS = ('ffn1_norm', 'mix_norm', 'hg_lb_fwd', 'hg_lb_bwd', 'hg_out_norm', 'q_norm', 'k_norm', 'ffn2_norm')
WEIGHTS = ('meta_tokens', 'ffn1_norm', 'ffn1_w_gate', 'ffn1_w_up', 'ffn1_w_down', 'mix_norm', 'w_in', 'hg_lb_fwd',
           'hg_lb_bwd', 'hg_out_norm', 'q_norm', 'k_norm', 'w_up_a', 'w_up_b', 'w_out', 'ffn2_norm', 'ffn2_w_gate',
           'ffn2_w_up', 'ffn2_w_down')


def _params(sem=None, vmem=VMEM_DEFAULT):
    return pltpu.CompilerParams(dimension_semantics=sem, vmem_limit_bytes=vmem)


def _tile(n, pref, q=LANE):
    for d in range(min(pref, n), 0, -1):
        if n % d == 0 and d % q == 0:
            return d
    return n


def _sigmoid(x):
    return 1.0 / (1.0 + jnp.exp(-x))


def _dot(a, b, dims):
    return lax.dot_general(a, b, (dims, ((), ())), preferred_element_type=F32)


def _nn(a, b):
    return _dot(a, b, ((1,), (0,)))


def _nt(a, b):
    return _dot(a, b, ((1,), (1,)))


def _tn(a, b):
    return _dot(a, b, ((0,), (0,)))


def _split3(x):
    x1 = x.astype(BF16)
    r = x - x1.astype(F32)
    x2 = r.astype(BF16)
    x3 = (r - x2.astype(F32)).astype(BF16)
    return x1, x2, x3


def _exact_left(m01, x):
    x1, x2, x3 = _split3(x)
    return _nn(m01, x1) + _nn(m01, x2) + _nn(m01, x3)


def _exact_right(x, m01):
    x1, x2, x3 = _split3(x)
    return _nn(x1, m01) + _nn(x2, m01) + _nn(x3, m01)


def _mm(pairs, *, name, ta=False, tb=False, out_dtype=F32, tm=512, tn=1024, tk=1024, alpha=1.0, res=None):
    a0, b0 = pairs[0]
    M = a0.shape[1] if ta else a0.shape[0]
    K = a0.shape[0] if ta else a0.shape[1]
    N = b0.shape[0] if tb else b0.shape[1]
    tm, tn, tk = _tile(M, tm), _tile(N, tn), _tile(K, tk)
    nk = K // tk
    npair = len(pairs)
    dims = ((0 if ta else 1,), (1 if tb else 0,))

    def body(*refs):
        ab = refs[:2 * npair]
        pos = 2 * npair
        res_ref = None
        if res is not None:
            res_ref = refs[pos]
            pos += 1
        o_ref = refs[pos]

        def partial_sum():
            tot = None
            for p in range(npair):
                d = _dot(ab[2 * p][...].astype(BF16), ab[2 * p + 1][...].astype(BF16), dims)
                tot = d if tot is None else tot + d
            return tot

        def finish(acc):
            r = acc if alpha == 1.0 else acc * alpha
            if res_ref is not None:
                r = res_ref[...] + r
            o_ref[...] = r.astype(out_dtype)

        if nk == 1:
            finish(partial_sum())
        else:
            acc_ref = refs[pos + 1]
            k = pl.program_id(2)

            @pl.when(k == 0)
            def _():
                acc_ref[...] = jnp.zeros_like(acc_ref)

            acc_ref[...] += partial_sum()

            @pl.when(k == nk - 1)
            def _():
                finish(acc_ref[...])

    a_spec = pl.BlockSpec((tk, tm), lambda j, i, k: (k, i)) if ta else pl.BlockSpec((tm, tk), lambda j, i, k: (i, k))
    b_spec = pl.BlockSpec((tn, tk), lambda j, i, k: (j, k)) if tb else pl.BlockSpec((tk, tn), lambda j, i, k: (k, j))
    o_spec = pl.BlockSpec((tm, tn), lambda j, i, k: (i, j))
    in_specs, args = [], []
    for a, b in pairs:
        in_specs += [a_spec, b_spec]
        args += [a, b]
    if res is not None:
        in_specs.append(o_spec)
        args.append(res)
    return pl.pallas_call(
        body, grid=(N // tn, M // tm, nk), in_specs=in_specs, out_specs=o_spec,
        out_shape=SDS((M, N), out_dtype),
        scratch_shapes=[pltpu.VMEM((tm, tn), F32)] if nk > 1 else [],
        compiler_params=_params(("parallel", "parallel", "arbitrary")), name=name)(*args)


def _rmsnorm_fwd(h, w, *, name):
    L, D = h.shape
    tm = _tile(L, 512)

    def body(h_ref, w_ref, o_ref):
        x = h_ref[...]
        r = lax.rsqrt(jnp.mean(x * x, axis=-1, keepdims=True) + EPS)
        o_ref[...] = (x * r * w_ref[...]).astype(BF16)

    return pl.pallas_call(
        body, grid=(L // tm,),
        in_specs=[pl.BlockSpec((tm, D), lambda i: (i, 0)), pl.BlockSpec((1, D), lambda i: (0, 0))],
        out_specs=pl.BlockSpec((tm, D), lambda i: (i, 0)), out_shape=SDS((L, D), BF16),
        compiler_params=_params(("parallel",)), name=name)(h, w)


def _rmsnorm_bwd(h, w, dn, dres, *, name):
    L, D = h.shape
    tm = _tile(L, 512)

    def body(h_ref, w_ref, dn_ref, dres_ref, dh_ref, dw_ref):
        x = h_ref[...]
        r = lax.rsqrt(jnp.mean(x * x, axis=-1, keepdims=True) + EPS)
        xh = x * r
        dn = dn_ref[...]
        dxh = dn * w_ref[...]
        dh_ref[...] = dres_ref[...] + r * (dxh - xh * jnp.mean(dxh * xh, axis=-1, keepdims=True))

        @pl.when(pl.program_id(0) == 0)
        def _():
            dw_ref[...] = jnp.zeros_like(dw_ref)

        dw_ref[...] += jnp.sum(dn * xh, axis=0, keepdims=True)

    row = pl.BlockSpec((tm, D), lambda i: (i, 0))
    vec = pl.BlockSpec((1, D), lambda i: (0, 0))
    return pl.pallas_call(
        body, grid=(L // tm,), in_specs=[row, vec, row, row], out_specs=[row, vec],
        out_shape=[SDS((L, D), F32), SDS((1, D), F32)],
        compiler_params=_params(("arbitrary",)), name=name)(h, w, dn, dres)


def _ffn_up(n, wg, wu, *, name):
    L, D = n.shape
    Fd = wg.shape[1]
    tm, tn = _tile(L, 512), _tile(Fd, 1408)

    def body(n_ref, wg_ref, wu_ref, g_ref, u_ref, a_ref):
        x = n_ref[...]
        g = _nn(x, wg_ref[...])
        u = _nn(x, wu_ref[...])
        g_ref[...] = g
        u_ref[...] = u
        a_ref[...] = (g * _sigmoid(g) * u).astype(BF16)

    wspec = pl.BlockSpec((D, tn), lambda j, i: (0, j))
    ospec = pl.BlockSpec((tm, tn), lambda j, i: (i, j))
    return pl.pallas_call(
        body, grid=(Fd // tn, L // tm),
        in_specs=[pl.BlockSpec((tm, D), lambda j, i: (i, 0)), wspec, wspec],
        out_specs=[ospec, ospec, ospec],
        out_shape=[SDS((L, Fd), F32), SDS((L, Fd), F32), SDS((L, Fd), BF16)],
        compiler_params=_params(("parallel", "parallel")), name=name)(n, wg, wu)


def _ffn_dact(dh, wd, g, u, *, name):
    L, D = dh.shape
    Fd = wd.shape[0]
    tm, tn = _tile(L, 512), _tile(Fd, 1408)

    def body(dh_ref, wd_ref, g_ref, u_ref, dg_ref, du_ref):
        da = 0.5 * _nt(dh_ref[...].astype(BF16), wd_ref[...])
        g = g_ref[...]
        sg = _sigmoid(g)
        dg_ref[...] = (da * u_ref[...] * (sg * (1.0 + g * (1.0 - sg)))).astype(BF16)
        du_ref[...] = (da * (g * sg)).astype(BF16)

    ospec = pl.BlockSpec((tm, tn), lambda j, i: (i, j))
    return pl.pallas_call(
        body, grid=(Fd // tn, L // tm),
        in_specs=[pl.BlockSpec((tm, D), lambda j, i: (i, 0)), pl.BlockSpec((tn, D), lambda j, i: (j, 0)), ospec, ospec],
        out_specs=[ospec, ospec], out_shape=[SDS((L, Fd), BF16), SDS((L, Fd), BF16)],
        compiler_params=_params(("parallel", "parallel")), name=name)(dh, wd, g, u)


def _ffn4_up(n, wg4, wu4, *, name):
    L, D = n.shape
    ns, _, cs = wg4.shape
    tm = _tile(L, 768)

    def body(n_ref, wg_ref, wu_ref, g_ref, u_ref, a_ref):
        x = n_ref[...]
        g = _nn(x, wg_ref[...])
        u = _nn(x, wu_ref[...])
        g_ref[...] = g
        u_ref[...] = u
        a_ref[...] = (g * _sigmoid(g) * u).astype(BF16)

    wspec = pl.BlockSpec((None, D, cs), lambda j, i: (j, 0, 0))
    ospec = pl.BlockSpec((None, tm, cs), lambda j, i: (j, i, 0))
    return pl.pallas_call(
        body, grid=(ns, L // tm),
        in_specs=[pl.BlockSpec((tm, D), lambda j, i: (i, 0)), wspec, wspec], out_specs=[ospec, ospec, ospec],
        out_shape=[SDS((ns, L, cs), F32), SDS((ns, L, cs), F32), SDS((ns, L, cs), BF16)],
        compiler_params=_params(("parallel", "parallel")), name=name)(n, wg4, wu4)


def _ffn4_down(a4, wd4, h, *, name):
    ns, L, cs = a4.shape
    D = wd4.shape[2]
    tm = _tile(L, 512)

    def body(a_ref, w_ref, h_ref, o_ref):
        acc = _nn(a_ref[0], w_ref[0])
        for j in range(1, ns):
            acc = acc + _nn(a_ref[j], w_ref[j])
        o_ref[...] = h_ref[...] + 0.5 * acc

    row = pl.BlockSpec((tm, D), lambda i: (i, 0))
    return pl.pallas_call(
        body, grid=(L // tm,),
        in_specs=[pl.BlockSpec((ns, tm, cs), lambda i: (0, i, 0)), pl.BlockSpec((ns, cs, D), lambda i: (0, 0, 0)), row],
        out_specs=row, out_shape=SDS((L, D), F32),
        compiler_params=_params(("parallel",)), name=name)(a4, wd4, h)


def _ffn4_dact(dh, wd4, g4, u4, *, name):
    L, D = dh.shape
    ns, cs, _ = wd4.shape
    tm = _tile(L, 768)

    def body(dh_ref, wd_ref, g_ref, u_ref, dg_ref, du_ref):
        da = 0.5 * _nt(dh_ref[...].astype(BF16), wd_ref[...])
        g = g_ref[...]
        sg = _sigmoid(g)
        dg_ref[...] = (da * u_ref[...] * (sg * (1.0 + g * (1.0 - sg)))).astype(BF16)
        du_ref[...] = (da * (g * sg)).astype(BF16)

    ospec = pl.BlockSpec((None, tm, cs), lambda j, i: (j, i, 0))
    return pl.pallas_call(
        body, grid=(ns, L // tm),
        in_specs=[pl.BlockSpec((tm, D), lambda j, i: (i, 0)), pl.BlockSpec((None, cs, D), lambda j, i: (j, 0, 0)), ospec, ospec],
        out_specs=[ospec, ospec], out_shape=[SDS((ns, L, cs), BF16), SDS((ns, L, cs), BF16)],
        compiler_params=_params(("parallel", "parallel")), name=name)(dh, wd4, g4, u4)


def _ffn4_dact_dn(dh, wd4, g4, u4, wg4, wu4, *, name):
    L, D = dh.shape
    ns, cs, _ = wd4.shape
    tm = _tile(L, 768)

    def body(dh_ref, wd_ref, g_ref, u_ref, wg_ref, wu_ref, dg_ref, du_ref, dn_ref, acc_ref):
        j = pl.program_id(1)
        da = 0.5 * _nt(dh_ref[...].astype(BF16), wd_ref[...])
        g = g_ref[...]
        sg = _sigmoid(g)
        dg = (da * u_ref[...] * (sg * (1.0 + g * (1.0 - sg)))).astype(BF16)
        du = (da * (g * sg)).astype(BF16)
        dg_ref[...] = dg
        du_ref[...] = du
        t = _nt(dg, wg_ref[...]) + _nt(du, wu_ref[...])

        @pl.when(j == 0)
        def _():
            acc_ref[...] = t

        @pl.when(j > 0)
        def _():
            acc_ref[...] += t

        @pl.when(j == ns - 1)
        def _():
            dn_ref[...] = acc_ref[...]

    row = pl.BlockSpec((tm, D), lambda i, j: (i, 0))
    aspec = pl.BlockSpec((None, tm, cs), lambda i, j: (j, i, 0))
    wcol = pl.BlockSpec((None, D, cs), lambda i, j: (j, 0, 0))
    return pl.pallas_call(
        body, grid=(L // tm, ns),
        in_specs=[row, pl.BlockSpec((None, cs, D), lambda i, j: (j, 0, 0)), aspec, aspec, wcol, wcol],
        out_specs=[aspec, aspec, row],
        out_shape=[SDS((ns, L, cs), BF16), SDS((ns, L, cs), BF16), SDS((L, D), F32)],
        scratch_shapes=[pltpu.VMEM((tm, D), F32)],
        compiler_params=_params(("parallel", "arbitrary")), name=name)(dh, wd4, g4, u4, wg4, wu4)


def _ffn4_dn(dg4, du4, wg4, wu4, *, name):
    ns, L, cs = dg4.shape
    D = wg4.shape[1]
    tm = _tile(L, 512)

    def body(dg_ref, du_ref, wg_ref, wu_ref, o_ref):
        acc = None
        for j in range(ns):
            t = _nt(dg_ref[j], wg_ref[j]) + _nt(du_ref[j], wu_ref[j])
            acc = t if acc is None else acc + t
        o_ref[...] = acc

    aspec = pl.BlockSpec((ns, tm, cs), lambda i: (0, i, 0))
    wspec = pl.BlockSpec((ns, D, cs), lambda i: (0, 0, 0))
    return pl.pallas_call(
        body, grid=(L // tm,), in_specs=[aspec, aspec, wspec, wspec],
        out_specs=pl.BlockSpec((tm, D), lambda i: (i, 0)), out_shape=SDS((L, D), F32),
        compiler_params=_params(("parallel",), VMEM_LARGE), name=name)(dg4, du4, wg4, wu4)


def _ffn4_dw(x, y4, *, x_is_rows, alpha=1.0, name):
    L, D = x.shape
    ns, _, cs = y4.shape
    tk = _tile(L, 512)
    nk = L // tk
    oshape = (D, cs) if x_is_rows else (cs, D)

    def body(x_ref, y_ref, o_ref):
        k = pl.program_id(0)

        @pl.when(k == 0)
        def _():
            o_ref[...] = jnp.zeros_like(o_ref)

        xb = x_ref[...].astype(BF16)
        if x_is_rows:
            xt = xb.T
            for j in range(ns):
                o_ref[j] += _nn(xt, y_ref[j])
        else:
            for j in range(ns):
                o_ref[j] += _tn(y_ref[j], xb)

        if alpha != 1.0:
            @pl.when(k == nk - 1)
            def _():
                o_ref[...] = o_ref[...] * alpha

    return pl.pallas_call(
        body, grid=(nk,),
        in_specs=[pl.BlockSpec((tk, D), lambda k: (k, 0)), pl.BlockSpec((ns, tk, cs), lambda k: (0, k, 0))],
        out_specs=pl.BlockSpec((ns,) + oshape, lambda k: (0, 0, 0)), out_shape=SDS((ns,) + oshape, F32),
        compiler_params=_params(("arbitrary",)), name=name)(x, y4)


def _hg_masks(rev):
    t = lax.broadcasted_iota(jnp.int32, (CHUNK, CHUNK), 0)
    s = lax.broadcasted_iota(jnp.int32, (CHUNK, CHUNK), 1)
    causal = (s >= t) if rev else (s <= t)
    levels = []
    for sh in (6, 5, 4):
        same = jnp.right_shift(t, sh + 1) == jnp.right_shift(s, sh + 1)
        tq = jnp.bitwise_and(jnp.right_shift(t, sh), 1)
        sk = jnp.bitwise_and(jnp.right_shift(s, sh), 1)
        levels.append(same & (tq == (0 if rev else 1)) & (sk == (1 if rev else 0)))
    diag = (jnp.right_shift(t, 4) == jnp.right_shift(s, 4)) & causal
    return causal, levels, diag


def _hg_intra_factors(q, k, b, b_scr, rev):
    b_scr[...] = b
    row = lax.broadcasted_iota(jnp.int32, (CHUNK, LANE), 0)
    out = []
    for sh in (6, 5, 4):
        lb = 1 << sh
        pieces = []
        for p in range(0, CHUNK, 2 * lb):
            r = p + lb if rev else p + lb - 1
            pieces.append(jnp.broadcast_to(b_scr[pl.ds(r, 1), :], (2 * lb, LANE)))
        ref = pieces[0] if len(pieces) == 1 else jnp.concatenate(pieces, axis=0)
        qside = jnp.bitwise_and(jnp.right_shift(row, sh), 1) == (0 if rev else 1)
        eq = jnp.where(qside, jnp.exp(jnp.minimum(b - ref, 0.0)), 0.0)
        ek = jnp.where(qside, 0.0, jnp.exp(jnp.minimum(ref - b, 0.0)))
        out.append((eq, ek, (q * eq).astype(BF16), (k * ek).astype(BF16)))
    pieces = []
    for a in range(0, CHUNK, 16):
        r = a + (8 if rev else 7)
        pieces.append(jnp.broadcast_to(b_scr[pl.ds(r, 1), :], (16, LANE)))
    ref = jnp.concatenate(pieces, axis=0)
    eq = jnp.exp(jnp.minimum(b - ref, 80.0))
    ek = jnp.exp(jnp.minimum(ref - b, 80.0))
    out.append((eq, ek, (q * eq).astype(BF16), (k * ek).astype(BF16)))
    return out


def _hg_gate(zf, l0, l1, valid):
    mx = jnp.maximum(l0, l1)
    e0, e1 = jnp.exp(l0 - mx), jnp.exp(l1 - mx)
    p0 = e0 / (e0 + e1)
    sg = _sigmoid(-zf)
    k = jnp.where(valid, (1.0 - p0) * sg, 0.0)
    return p0, sg, k, jnp.log(1.0 - k)


def _hg_fwd(z, lbp, *, rev, name):
    L = z.shape[0]
    nc = L // CHUNK
    fcol = 3 if rev else 2

    def cidx(j):
        return nc - 1 - j if rev else j

    def body(zq_ref, zi_ref, zf_ref, lb_ref, o_ref, ssave_ref, st_scr, b_scr):
        j = pl.program_id(0)

        @pl.when(j == 0)
        def _():
            st_scr[...] = jnp.zeros_like(st_scr)

        causal, lmasks, dmask = _hg_masks(rev)
        tri = jnp.where(causal, 1.0, 0.0).astype(BF16)
        rowg = cidx(j) * CHUNK + lax.broadcasted_iota(jnp.int32, (CHUNK, LANE), 0)
        valid = rowg >= PAD - N_META
        last = 0 if rev else CHUNK - 1
        for hh in range(HG_HEADS):
            sl = slice(LANE * hh, LANE * (hh + 1))
            zq = zq_ref[:, sl]
            q = zq * _sigmoid(zq)
            v = zi_ref[:, sl].astype(BF16)
            _, _, k, g = _hg_gate(zf_ref[:, sl], lb_ref[0:1, sl], lb_ref[1:2, sl], valid)
            b = _exact_left(tri, g)
            st = st_scr[hh]
            ssave_ref[0, hh] = st
            o = _nt((q * jnp.exp(b)).astype(BF16), st.astype(BF16))
            a = None
            fac = _hg_intra_factors(q, k, b, b_scr, rev)
            for (eq, ek, qq, kk), msk in zip(fac, lmasks + [dmask]):
                t = jnp.where(msk, _nt(qq, kk), 0.0)
                a = t if a is None else a + t
            o_ref[:, sl] = o + _nn(a.astype(BF16), v)
            bl = b_scr[pl.ds(last, 1), :]
            kd = (k * jnp.exp(bl - b)).astype(BF16)
            st_scr[hh] = st * jnp.exp(bl) + _tn(v, kd)

    zspec = lambda col: pl.BlockSpec((CHUNK, HG_W), lambda j: (cidx(j), col))
    return pl.pallas_call(
        body, grid=(nc,),
        in_specs=[zspec(0), zspec(1), zspec(fcol), pl.BlockSpec((2, HG_W), lambda j: (0, 0))],
        out_specs=[pl.BlockSpec((CHUNK, HG_W), lambda j: (cidx(j), 0)),
                   pl.BlockSpec((1, HG_HEADS, LANE, LANE), lambda j: (cidx(j), 0, 0, 0))],
        out_shape=[SDS((L, HG_W), F32), SDS((nc, HG_HEADS, LANE, LANE), F32)],
        scratch_shapes=[pltpu.VMEM((HG_HEADS, LANE, LANE), F32), pltpu.VMEM((CHUNK, LANE), F32)],
        compiler_params=_params(("arbitrary",)), name=name)(z, z, z, lbp)


def _hg_bwd(z, lbp, do, ssave, prev, *, rev, name):
    L = z.shape[0]
    nc = L // CHUNK
    fcol = 3 if rev else 2
    final = prev is not None

    def cidx(j):
        return j if rev else nc - 1 - j

    def body(*refs):
        zq_ref, zi_ref, zf_ref, lb_ref, do_ref, ss_ref = refs[:6]
        pos = 6
        if final:
            dqin_ref, dvin_ref = refs[6:8]
            pos = 8
        dq_ref, dv_ref, dzf_ref, dlb_ref, dst_scr, b_scr = refs[pos:pos + 6]
        j = pl.program_id(0)

        @pl.when(j == 0)
        def _():
            dst_scr[...] = jnp.zeros_like(dst_scr)
            dlb_ref[...] = jnp.zeros_like(dlb_ref)

        causal, lmasks, dmask = _hg_masks(rev)
        tri = jnp.where(causal, 1.0, 0.0).astype(BF16)
        ti = lax.broadcasted_iota(jnp.int32, (CHUNK, CHUNK), 0)
        si = lax.broadcasted_iota(jnp.int32, (CHUNK, CHUNK), 1)
        tri_t = jnp.where((si <= ti) if rev else (si >= ti), 1.0, 0.0).astype(BF16)
        rowg = cidx(j) * CHUNK + lax.broadcasted_iota(jnp.int32, (CHUNK, LANE), 0)
        valid = rowg >= PAD - N_META
        last = 0 if rev else CHUNK - 1
        for hh in range(HG_HEADS):
            sl = slice(LANE * hh, LANE * (hh + 1))
            zq = zq_ref[:, sl]
            sq = _sigmoid(zq)
            q = zq * sq
            v = zi_ref[:, sl].astype(BF16)
            p0, sg, k, g = _hg_gate(zf_ref[:, sl], lb_ref[0:1, sl], lb_ref[1:2, sl], valid)
            b = _exact_left(tri, g)
            dob = do_ref[:, sl].astype(BF16)
            st = ss_ref[0, hh]
            dst = dst_scr[hh]
            stb, dstb = st.astype(BF16), dst.astype(BF16)
            eb = jnp.exp(b)
            qe = (q * eb).astype(BF16)
            fac = _hg_intra_factors(q, k, b, b_scr, rev)
            bl = b_scr[pl.ds(last, 1), :]
            ebl = jnp.exp(bl)
            kde = jnp.exp(bl - b)
            kd = (k * kde).astype(BF16)
            da = jnp.where(causal, _nt(dob, v), 0.0)
            dq = eb * _nn(dob, stb)
            dk_inter = kde * _nn(v, dstb)
            dk = dk_inter
            dv = _nt(kd, dstb)
            a = None
            db = q * dq - k * dk
            for (eq, ek, qq, kk), msk in zip(fac, lmasks + [dmask]):
                t = jnp.where(msk, _nt(qq, kk), 0.0)
                a = t if a is None else a + t
                dal = jnp.where(msk, da, 0.0).astype(BF16)
                mq = _nn(dal, kk)
                mk = _tn(dal, qq)
                dq = dq + eq * mq
                dk = dk + ek * mk
                db = db + (qq.astype(F32) * mq - kk.astype(F32) * mk)
            dv = dv + _tn(a.astype(BF16), dob)
            extra = ebl * jnp.sum(st * dst, axis=0, keepdims=True) + jnp.sum(k * dk_inter, axis=0, keepdims=True)
            dst_scr[hh] = dst * ebl + _tn(dob, qe)
            dg = _exact_left(tri_t, db) + extra
            dk_tot = dk - dg / (1.0 - k)
            dzf_ref[:, sl] = jnp.where(valid, dk_tot * (1.0 - p0) * (-sg * (1.0 - sg)), 0.0).astype(BF16)
            dlb_ref[:, sl] += jnp.sum(jnp.where(valid, -sg * dk_tot, 0.0), axis=0, keepdims=True)
            if final:
                dq_ref[:, sl] = ((dq + dqin_ref[:, sl]) * (sq * (1.0 + zq * (1.0 - sq)))).astype(BF16)
                dv_ref[:, sl] = (dv + dvin_ref[:, sl]).astype(BF16)
            else:
                dq_ref[:, sl] = dq
                dv_ref[:, sl] = dv

    zspec = lambda col: pl.BlockSpec((CHUNK, HG_W), lambda j: (cidx(j), col))
    rspec = pl.BlockSpec((CHUNK, HG_W), lambda j: (cidx(j), 0))
    in_specs = [zspec(0), zspec(1), zspec(fcol), pl.BlockSpec((2, HG_W), lambda j: (0, 0)), rspec,
                pl.BlockSpec((1, HG_HEADS, LANE, LANE), lambda j: (cidx(j), 0, 0, 0))]
    args = [z, z, z, lbp, do, ssave]
    if final:
        in_specs += [rspec, rspec]
        args += list(prev)
    odt = BF16 if final else F32
    return pl.pallas_call(
        body, grid=(nc,), in_specs=in_specs,
        out_specs=[rspec, rspec, rspec, pl.BlockSpec((1, HG_W), lambda j: (0, 0))],
        out_shape=[SDS((L, HG_W), odt), SDS((L, HG_W), odt), SDS((L, HG_W), BF16), SDS((1, HG_W), F32)],
        scratch_shapes=[pltpu.VMEM((HG_HEADS, LANE, LANE), F32), pltpu.VMEM((CHUNK, LANE), F32)],
        compiler_params=_params(("arbitrary",)), name=name)(*args)


def _hg_post_fwd(of, ob, z, w, *, name):
    L = of.shape[0]
    tm = _tile(L, 512)

    def body(of_ref, ob_ref, zg_ref, w_ref, y_ref):
        for hh in range(HG_HEADS):
            sl = slice(LANE * hh, LANE * (hh + 1))
            o = of_ref[:, sl] + ob_ref[:, sl]
            r = lax.rsqrt(jnp.mean(o * o, axis=-1, keepdims=True) + EPS)
            zg = zg_ref[:, sl]
            y_ref[:, sl] = (o * r * w_ref[:, sl] * (zg * _sigmoid(zg))).astype(BF16)

    row = pl.BlockSpec((tm, HG_W), lambda i: (i, 0))
    return pl.pallas_call(
        body, grid=(L // tm,),
        in_specs=[row, row, pl.BlockSpec((tm, HG_W), lambda i: (i, 4)), pl.BlockSpec((1, HG_W), lambda i: (0, 0))],
        out_specs=row, out_shape=SDS((L, HG_W), BF16),
        compiler_params=_params(("parallel",)), name=name)(of, ob, z, w)


def _hg_post_bwd(dy, of, ob, z, w, *, name):
    L = of.shape[0]
    tm = _tile(L, 512)

    def body(dy_ref, of_ref, ob_ref, zg_ref, w_ref, do_ref, dzg_ref, dw_ref):
        @pl.when(pl.program_id(0) == 0)
        def _():
            dw_ref[...] = jnp.zeros_like(dw_ref)

        for hh in range(HG_HEADS):
            sl = slice(LANE * hh, LANE * (hh + 1))
            o = of_ref[:, sl] + ob_ref[:, sl]
            r = lax.rsqrt(jnp.mean(o * o, axis=-1, keepdims=True) + EPS)
            xh = o * r
            zg = zg_ref[:, sl]
            sg = _sigmoid(zg)
            w = w_ref[:, sl]
            dy = dy_ref[:, sl]
            dys = dy * (zg * sg)
            dzg_ref[:, sl] = (dy * xh * w * (sg * (1.0 + zg * (1.0 - sg)))).astype(BF16)
            dw_ref[:, sl] += jnp.sum(dys * xh, axis=0, keepdims=True)
            dxh = dys * w
            do_ref[:, sl] = r * (dxh - xh * jnp.mean(dxh * xh, axis=-1, keepdims=True))

    row = pl.BlockSpec((tm, HG_W), lambda i: (i, 0))
    vec = pl.BlockSpec((1, HG_W), lambda i: (0, 0))
    return pl.pallas_call(
        body, grid=(L // tm,),
        in_specs=[row, row, row, pl.BlockSpec((tm, HG_W), lambda i: (i, 4)), vec],
        out_specs=[row, row, vec],
        out_shape=[SDS((L, HG_W), F32), SDS((L, HG_W), BF16), SDS((1, HG_W), F32)],
        compiler_params=_params(("arbitrary",)), name=name)(dy, of, ob, z, w)


N_GROUPS = (AT_HEADS + AT_KV) // 2


def _qk_to_group(wqk):
    d = wqk.shape[0]
    return wqk.reshape(d, N_GROUPS, 2, AT_HD // 2, 2).transpose(0, 1, 4, 2, 3).reshape(d, N_GROUPS * LANE)


def _qk_from_group(wqk):
    d = wqk.shape[0]
    return wqk.reshape(d, N_GROUPS, 2, 2, AT_HD // 2).transpose(0, 1, 3, 4, 2).reshape(d, N_GROUPS * LANE)


def _group_vec(w64):
    halves = w64.reshape(AT_HD // 2, 2).T
    return jnp.broadcast_to(halves[:, None, :], (2, 2, AT_HD // 2)).reshape(1, LANE)


def _ungroup_vec(w128):
    w = w128.reshape(2, 2, 32).sum(axis=1)
    return w.T.reshape(1, AT_HD)


def _rope_tables(L):
    n_real = L - PAD
    t = np.arange(n_real)
    row = np.concatenate([np.zeros(PAD), t // GRID_W]).astype(np.float32)
    col = np.concatenate([np.zeros(PAD), t % GRID_W]).astype(np.float32)
    inv = jnp.asarray(ROPE_THETA, F32) ** (-jnp.arange(0, AT_HD // 2, 2, dtype=F32) / (AT_HD // 2))
    ang = jnp.concatenate([jnp.asarray(row)[:, None] * inv, jnp.asarray(col)[:, None] * inv], axis=-1)
    cos, sin = jnp.cos(ang), jnp.sin(ang)
    cc = jnp.tile(cos, (1, 4))
    ss = jnp.concatenate([-sin, -sin, sin, sin], axis=1)
    return cc, ss


def _seg_matrix():
    a = lax.broadcasted_iota(jnp.int32, (LANE, LANE), 0)
    b = lax.broadcasted_iota(jnp.int32, (LANE, LANE), 1)
    same = jnp.bitwise_and(jnp.right_shift(a, 5), 1) == jnp.bitwise_and(jnp.right_shift(b, 5), 1)
    return jnp.where(same, 1.0, 0.0).astype(BF16)


def _slot_mask(shape, hp):
    lane = lax.broadcasted_iota(jnp.int32, shape, 1)
    return jnp.bitwise_and(jnp.right_shift(lane, 5), 1) == hp


def _at_prep(z, cc, ss, wq, wk, *, name):
    L = z.shape[0]
    tm = PAD
    qcol = Z_HG // AT_W
    kvcol = (Z_HG + AT_W) // (2 * LANE)

    def body(zq_ref, zkv_ref, cc_ref, ss_ref, wq_ref, wk_ref, qm_ref, qt_ref, kr_ref, krt_ref, vb_ref, vt_ref):
        seg = _seg_matrix()
        cc, ss = cc_ref[...], ss_ref[...]

        def normrope(x, w):
            r = lax.rsqrt(_exact_right(x * x, seg) * (1.0 / AT_HD) + EPS)
            y = x * r * w
            return y * cc + pltpu.roll(y, 64, 1) * ss

        for g in range(AT_HEADS // 2):
            o = normrope(zq_ref[:, LANE * g:LANE * (g + 1)], wq_ref[...]) * (AT_HD ** -0.5)
            for hp in range(2):
                h = 2 * g + hp
                tgt = h // (AT_HEADS // AT_KV)
                xm = jnp.where(_slot_mask(o.shape, hp), o, 0.0)
                if tgt != hp:
                    xm = pltpu.roll(xm, 32 if tgt == 1 else 96, 1)
                qm_ref[h] = xm.astype(BF16)
                qt_ref[h] = xm.T.astype(BF16)
        kr = normrope(zkv_ref[:, :LANE], wk_ref[...])
        kr_ref[...] = kr.astype(BF16)
        krt_ref[0] = kr.T.astype(BF16)
        v = zkv_ref[:, LANE:]
        low = lax.broadcasted_iota(jnp.int32, v.shape, 1) < AT_HD
        vb_ref[0] = jnp.where(low, v, 0.0).astype(BF16)
        vb_ref[1] = jnp.where(low, pltpu.roll(v, AT_HD, 1), 0.0).astype(BF16)
        vt = v.T.astype(BF16)
        ones = jnp.ones((VT_ROWS - AT_HD, tm), BF16)
        for j in range(AT_KV):
            vt_ref[j, 0, 0:AT_HD, :] = vt[AT_HD * j:AT_HD * (j + 1)]
            vt_ref[j, 0, AT_HD:VT_ROWS, :] = ones

    tab = pl.BlockSpec((tm, LANE), lambda i: (i, 0))
    vec = pl.BlockSpec((1, LANE), lambda i: (0, 0))
    nt = L // tm
    return pl.pallas_call(
        body, grid=(nt,),
        in_specs=[pl.BlockSpec((tm, AT_W), lambda i: (i, qcol)), pl.BlockSpec((tm, 2 * LANE), lambda i: (i, kvcol)),
                  tab, tab, vec, vec],
        out_specs=[pl.BlockSpec((AT_HEADS, tm, LANE), lambda i: (0, i, 0)),
                   pl.BlockSpec((AT_HEADS, LANE, tm), lambda i: (0, 0, i)), tab,
                   pl.BlockSpec((1, LANE, tm), lambda i: (i, 0, 0)),
                   pl.BlockSpec((AT_KV, tm, LANE), lambda i: (0, i, 0)),
                   pl.BlockSpec((AT_KV, 1, VT_ROWS, tm), lambda i: (0, i, 0, 0))],
        out_shape=[SDS((AT_HEADS, L, LANE), BF16), SDS((AT_HEADS, LANE, L), BF16), SDS((L, LANE), BF16),
                   SDS((nt, LANE, tm), BF16), SDS((AT_KV, L, LANE), BF16), SDS((AT_KV, nt, VT_ROWS, tm), BF16)],
        compiler_params=_params(("parallel",)), name=name)(z, z, cc, ss, wq, wk)


def _at_prep_bwd(dqm, dk2, dv2, z, cc, ss, wq, wk, *, name):
    L = z.shape[0]
    tm = PAD
    qcol = Z_HG // AT_W
    kvcol = (Z_HG + AT_W) // (2 * LANE)

    def body(dqm_ref, dk2_ref, dv2_ref, zq_ref, zkv_ref, cc_ref, ss_ref, wq_ref, wk_ref, dz_ref, dwq_ref, dwk_ref):
        @pl.when(pl.program_id(0) == 0)
        def _():
            dwq_ref[...] = jnp.zeros_like(dwq_ref)
            dwk_ref[...] = jnp.zeros_like(dwk_ref)

        seg = _seg_matrix()
        cc, ss = cc_ref[...], ss_ref[...]

        def back(x, w, do):
            dy = do * cc + pltpu.roll(do * ss, 64, 1)
            r = lax.rsqrt(_exact_right(x * x, seg) * (1.0 / AT_HD) + EPS)
            xh = x * r
            dxh = dy * w
            dx = r * (dxh - xh * (_exact_right(dxh * xh, seg) * (1.0 / AT_HD)))
            return dx, jnp.sum(dy * xh, axis=0, keepdims=True)

        for g in range(AT_HEADS // 2):
            do = None
            for hp in range(2):
                h = 2 * g + hp
                tgt = h // (AT_HEADS // AT_KV)
                d = jnp.where(_slot_mask((tm, LANE), tgt), dqm_ref[h], 0.0)
                if tgt != hp:
                    d = pltpu.roll(d, 96 if tgt == 1 else 32, 1)
                do = d if do is None else do + d
            dx, dw = back(zq_ref[:, LANE * g:LANE * (g + 1)], wq_ref[...], do * (AT_HD ** -0.5))
            dz_ref[:, LANE * g:LANE * (g + 1)] = dx.astype(BF16)
            dwq_ref[...] += dw
        dx, dw = back(zkv_ref[:, :LANE], wk_ref[...], dk2_ref[0] + dk2_ref[1])
        dz_ref[:, AT_W:AT_W + LANE] = dx.astype(BF16)
        dwk_ref[...] += dw
        dv0 = dv2_ref[0]
        low = lax.broadcasted_iota(jnp.int32, dv0.shape, 1) < AT_HD
        dz_ref[:, AT_W + LANE:] = jnp.where(low, dv0, pltpu.roll(dv2_ref[1], AT_HD, 1)).astype(BF16)

    tab = pl.BlockSpec((tm, LANE), lambda i: (i, 0))
    vec = pl.BlockSpec((1, LANE), lambda i: (0, 0))
    two = pl.BlockSpec((AT_KV, tm, LANE), lambda i: (0, i, 0))
    return pl.pallas_call(
        body, grid=(L // tm,),
        in_specs=[pl.BlockSpec((AT_HEADS, tm, LANE), lambda i: (0, i, 0)), two, two,
                  pl.BlockSpec((tm, AT_W), lambda i: (i, qcol)), pl.BlockSpec((tm, 2 * LANE), lambda i: (i, kvcol)),
                  tab, tab, vec, vec],
        out_specs=[pl.BlockSpec((tm, Z_AT), lambda i: (i, 0)), vec, vec],
        out_shape=[SDS((L, Z_AT), BF16), SDS((1, LANE), F32), SDS((1, LANE), F32)],
        compiler_params=_params(("arbitrary",)), name=name)(dqm, dk2, dv2, z, z, cc, ss, wq, wk)


def _at_fwd(qt, kr, vt, *, name):
    L = kr.shape[0]
    G = AT_HEADS // AT_KV
    tq = _tile(L, 384)
    tk = PAD
    nk = L // tk
    R = G * tq
    per = FWD_CHUNKS_PER_STEP if (nk - 1) % FWD_CHUNKS_PER_STEP == 0 else 1

    def body(q_ref, k_ref, v_ref, ob_ref, of_ref, lse_ref, m_scr, acc_scr):
        i = pl.program_id(1)
        qt = jnp.concatenate([q_ref[g] for g in range(G)], axis=1)
        m_scr[...] = jnp.full_like(m_scr, NEG)
        acc_scr[...] = jnp.zeros_like(acc_scr)

        def chunks(c, n, masked):
            start = c * tk if isinstance(c, int) else pl.multiple_of(c * tk, tk)
            st = _nn(k_ref[pl.ds(start, n * tk), :], qt)
            if masked:
                key = lax.broadcasted_iota(jnp.int32, st.shape, 0)
                st = jnp.where(key >= PAD - N_META, st, NEG)
            m_prev = m_scr[...]
            m_new = jnp.maximum(m_prev, jnp.max(st, axis=0, keepdims=True))
            pt = jnp.exp(st - m_new).astype(BF16)
            acc = jnp.exp(m_prev - m_new) * acc_scr[...]
            for u in range(n):
                acc = acc + _nn(v_ref[0, c + u], pt[u * tk:(u + 1) * tk])
            acc_scr[...] = acc
            m_scr[...] = m_new

        chunks(0, 1, True)

        def loop(t, carry):
            chunks(1 + per * t, per, False)
            return carry

        lax.fori_loop(0, (nk - 1) // per, loop, 0)
        l = acc_scr[pl.ds(AT_HD, 1), :]
        lse = m_scr[...] + jnp.log(l)
        on = acc_scr[0:AT_HD, :] / l
        o = jnp.concatenate([on[:, g * tq:(g + 1) * tq] for g in range(G)], axis=0).T
        rowg = i * tq + lax.broadcasted_iota(jnp.int32, o.shape, 0)
        o = jnp.where(rowg >= PAD - N_META, o, 0.0)
        ob_ref[...] = o.astype(BF16)
        of_ref[...] = o
        for g in range(G):
            lse_ref[g] = lse[:, g * tq:(g + 1) * tq]

    ospec = pl.BlockSpec((tq, G * AT_HD), lambda j, i: (i, j))
    return pl.pallas_call(
        body, grid=(AT_KV, L // tq),
        in_specs=[pl.BlockSpec((G, LANE, tq), lambda j, i: (j, 0, i)), pl.BlockSpec((L, LANE), lambda j, i: (0, 0)),
                  pl.BlockSpec((1, nk, VT_ROWS, tk), lambda j, i: (j, 0, 0, 0))],
        out_specs=[ospec, ospec, pl.BlockSpec((G, 1, tq), lambda j, i: (j, 0, i))],
        out_shape=[SDS((L, AT_W), BF16), SDS((L, AT_W), F32), SDS((AT_HEADS, 1, L), F32)],
        scratch_shapes=[pltpu.VMEM((1, R), F32), pltpu.VMEM((VT_ROWS, R), F32)],
        compiler_params=_params(("parallel", "parallel")), name=name)(qt, kr, vt)


def _at_bwd(qm, qt, kr, krt, vb, do, of, lse, *, name):
    L = kr.shape[0]
    G = AT_HEADS // AT_KV
    tq = _tile(L, 256)
    tk = PAD
    nk = L // tk
    nq = L // tq
    R = G * tq

    def body(qm_ref, q_ref, k_hbm, kt_hbm, v_hbm, do_ref, o_ref, lse_ref, dq_ref, dk_hbm, dv_hbm,
             k_scr, kt_scr, v_scr, dk_scr, dv_scr, dq_scr, sem):
        j, i = pl.program_id(0), pl.program_id(1)

        @pl.when(i == 0)
        def _():
            cps = [pltpu.make_async_copy(k_hbm, k_scr, sem.at[0]), pltpu.make_async_copy(kt_hbm, kt_scr, sem.at[1]),
                   pltpu.make_async_copy(v_hbm.at[j], v_scr, sem.at[2])]
            for cp in cps:
                cp.start()
            dk_scr[...] = jnp.zeros_like(dk_scr)
            dv_scr[...] = jnp.zeros_like(dv_scr)
            for cp in cps:
                cp.wait()

        qt = jnp.concatenate([q_ref[g] for g in range(G)], axis=1)
        rowg = i * tq + lax.broadcasted_iota(jnp.int32, (tq, G * AT_HD), 0)
        dot_all = jnp.where(rowg >= PAD - N_META, do_ref[...], 0.0).T
        ot_all = o_ref[...].T
        dot = jnp.concatenate([dot_all[AT_HD * g:AT_HD * (g + 1)] for g in range(G)], axis=1)
        ot = jnp.concatenate([ot_all[AT_HD * g:AT_HD * (g + 1)] for g in range(G)], axis=1)
        delta = jnp.sum(dot * ot, axis=0, keepdims=True)
        dot128 = jnp.concatenate([dot, jnp.zeros_like(dot)], axis=0)
        dor = dot128.T.astype(BF16)
        dot128 = dot128.astype(BF16)
        qr = qm_ref[...].reshape(R, LANE)
        lse_v = jnp.concatenate([lse_ref[g] for g in range(G)], axis=1)
        dq_scr[...] = jnp.zeros_like(dq_scr)

        def chunk(c, masked):
            start = c * tk if isinstance(c, int) else pl.multiple_of(c * tk, tk)
            k = k_scr[pl.ds(start, tk), :]
            kt = kt_scr[c]
            v = v_scr[pl.ds(start, tk), :]
            st = _nn(k, qt)
            if masked:
                key = lax.broadcasted_iota(jnp.int32, st.shape, 0)
                st = jnp.where(key >= PAD - N_META, st, NEG)
            pt = jnp.exp(st - lse_v)
            dst = (pt * (_nn(v, dot128) - delta)).astype(BF16)
            dq_scr[...] += _nn(kt, dst)
            dk_scr[pl.ds(start, tk), :] += _nn(dst, qr)
            dv_scr[pl.ds(start, tk), :] += _nn(pt.astype(BF16), dor)

        chunk(0, True)

        def loop(c, carry):
            chunk(c, False)
            return carry

        lax.fori_loop(1, nk, loop, 0)
        dq_ref[...] = dq_scr[...].T.reshape(G, tq, LANE)

        @pl.when(i == nq - 1)
        def _():
            ck = pltpu.make_async_copy(dk_scr, dk_hbm.at[j], sem.at[0])
            cv = pltpu.make_async_copy(dv_scr, dv_hbm.at[j], sem.at[1])
            ck.start()
            cv.start()
            ck.wait()
            cv.wait()

    anyspec = pl.BlockSpec(memory_space=pl.ANY)
    ospec = pl.BlockSpec((tq, G * AT_HD), lambda j, i: (i, j))
    return pl.pallas_call(
        body, grid=(AT_KV, nq),
        in_specs=[pl.BlockSpec((G, tq, LANE), lambda j, i: (j, i, 0)), pl.BlockSpec((G, LANE, tq), lambda j, i: (j, 0, i)),
                  anyspec, anyspec, anyspec, ospec, ospec, pl.BlockSpec((G, 1, tq), lambda j, i: (j, 0, i))],
        out_specs=[pl.BlockSpec((G, tq, LANE), lambda j, i: (j, i, 0)), anyspec, anyspec],
        out_shape=[SDS((AT_HEADS, L, LANE), F32), SDS((AT_KV, L, LANE), F32), SDS((AT_KV, L, LANE), F32)],
        scratch_shapes=[pltpu.VMEM((L, LANE), BF16), pltpu.VMEM((nk, LANE, tk), BF16), pltpu.VMEM((L, LANE), BF16),
                        pltpu.VMEM((L, LANE), F32), pltpu.VMEM((L, LANE), F32), pltpu.VMEM((LANE, R), F32),
                        pltpu.SemaphoreType.DMA((3,))],
        compiler_params=_params(("arbitrary", "arbitrary"), VMEM_LARGE), name=name)(qm, qt, kr, krt, vb, do, of, lse)


def _merge_fwd(ya, o8, wua, wubp, z, *, name):
    L = ya.shape[0]
    D = wua.shape[1]
    tm, tn = _tile(L, 512), 256
    ga, gb = (Z_HG + Z_AT) // tn, (Z_HG + Z_AT + D) // tn

    def body(ya_ref, o8_ref, wa_ref, wb_ref, za_ref, zb_ref, mix_ref):
        pa = _nn(ya_ref[...], wa_ref[...])
        pb = _nn(o8_ref[...], wb_ref[...])
        mix_ref[...] = (_sigmoid(za_ref[...]) * pa + _sigmoid(zb_ref[...]) * pb).astype(BF16)

    return pl.pallas_call(
        body, grid=(D // tn, L // tm),
        in_specs=[pl.BlockSpec((tm, ya.shape[1]), lambda j, i: (i, 0)), pl.BlockSpec((tm, o8.shape[1]), lambda j, i: (i, 0)),
                  pl.BlockSpec((wua.shape[0], tn), lambda j, i: (0, j)), pl.BlockSpec((wubp.shape[0], tn), lambda j, i: (0, j)),
                  pl.BlockSpec((tm, tn), lambda j, i: (i, ga + j)), pl.BlockSpec((tm, tn), lambda j, i: (i, gb + j))],
        out_specs=pl.BlockSpec((tm, tn), lambda j, i: (i, j)), out_shape=SDS((L, D), BF16),
        compiler_params=_params(("parallel", "parallel")), name=name)(ya, o8, wua, wubp, z, z)


def _merge_bwd(dh, wout, ya, o8, wua, wubp, z, *, name):
    L = ya.shape[0]
    D = wua.shape[1]
    tm, tn = _tile(L, 512), 256
    ga, gb = (Z_HG + Z_AT) // tn, (Z_HG + Z_AT + D) // tn

    def body(dh_ref, wo_ref, ya_ref, o8_ref, wa_ref, wb_ref, za_ref, zb_ref, dpa_ref, dpb_ref, dza_ref, dzb_ref):
        dm = _nt(dh_ref[...].astype(BF16), wo_ref[...])
        pa = _nn(ya_ref[...], wa_ref[...])
        pb = _nn(o8_ref[...], wb_ref[...])
        sa, sb = _sigmoid(za_ref[...]), _sigmoid(zb_ref[...])
        dpa_ref[...] = (dm * sa).astype(BF16)
        dpb_ref[...] = (dm * sb).astype(BF16)
        dza_ref[...] = (dm * pa * sa * (1.0 - sa)).astype(BF16)
        dzb_ref[...] = (dm * pb * sb * (1.0 - sb)).astype(BF16)

    ospec = pl.BlockSpec((tm, tn), lambda j, i: (i, j))
    return pl.pallas_call(
        body, grid=(D // tn, L // tm),
        in_specs=[pl.BlockSpec((tm, D), lambda j, i: (i, 0)), pl.BlockSpec((tn, D), lambda j, i: (j, 0)),
                  pl.BlockSpec((tm, ya.shape[1]), lambda j, i: (i, 0)), pl.BlockSpec((tm, o8.shape[1]), lambda j, i: (i, 0)),
                  pl.BlockSpec((wua.shape[0], tn), lambda j, i: (0, j)), pl.BlockSpec((wubp.shape[0], tn), lambda j, i: (0, j)),
                  pl.BlockSpec((tm, tn), lambda j, i: (i, ga + j)), pl.BlockSpec((tm, tn), lambda j, i: (i, gb + j))],
        out_specs=[ospec] * 4, out_shape=[SDS((L, D), BF16)] * 4,
        compiler_params=_params(("parallel", "parallel")), name=name)(dh, wout, ya, o8, wua, wubp, z, z)


def _loss_head(h, tgt, *, name):
    L, D = h.shape
    tm = PAD

    def body(h_ref, t_ref, dh_ref, ls_ref):
        i = pl.program_id(0)

        @pl.when(i == 0)
        def _():
            ls_ref[...] = jnp.zeros_like(ls_ref)
            dh_ref[...] = jnp.zeros_like(dh_ref)

        @pl.when(i > 0)
        def _():
            e = h_ref[...] - t_ref[...]
            dh_ref[...] = e * (1.0 / D)
            s = jnp.sum(e * e, axis=0, keepdims=True)
            tot = s[:, :LANE]
            for c in range(1, D // LANE):
                tot = tot + s[:, LANE * c:LANE * (c + 1)]
            ls_ref[...] += tot

    return pl.pallas_call(
        body, grid=(L // tm,),
        in_specs=[pl.BlockSpec((tm, D), lambda i: (i, 0)), pl.BlockSpec((tm, D), lambda i: (jnp.maximum(i - 1, 0), 0))],
        out_specs=[pl.BlockSpec((tm, D), lambda i: (i, 0)), pl.BlockSpec((1, LANE), lambda i: (0, 0))],
        out_shape=[SDS((L, D), F32), SDS((1, LANE), F32)],
        compiler_params=_params(("arbitrary",)), name=name)(h, tgt)


def _adamw(w, g, m, v, *, name):
    shape = w.shape
    w2, g2, m2, v2 = [a.reshape(-1, shape[-1]) for a in (w, g, m, v)]
    rows, cols = w2.shape
    tr = _tile(rows, 256, 8)

    def body(w_ref, g_ref, m_ref, v_ref, d_ref, nm_ref, nv_ref):
        g = g_ref[...]
        m = ADAM_B1 * m_ref[...] + (1.0 - ADAM_B1) * g
        v = ADAM_B2 * v_ref[...] + (1.0 - ADAM_B2) * (g * g)
        m_hat = m / (1.0 - ADAM_B1 ** ADAM_STEP)
        v_hat = v / (1.0 - ADAM_B2 ** ADAM_STEP)
        d_ref[...] = -ADAM_LR * (m_hat / (jnp.sqrt(v_hat) + ADAM_EPS) + ADAM_WD * w_ref[...])
        nm_ref[...] = m
        nv_ref[...] = v

    spec = pl.BlockSpec((tr, cols), lambda i: (i, 0))
    outs = pl.pallas_call(
        body, grid=(rows // tr,), in_specs=[spec] * 4, out_specs=[spec] * 3, out_shape=[SDS((rows, cols), F32)] * 3,
        compiler_params=_params(("parallel",)), name=name)(w2, g2, m2, v2)
    return [o.reshape(shape) for o in outs]


def _sum_slabs(x, *, name):
    n, R, _ = x.shape
    tr = _tile(R, 2048, 8)

    def body(x_ref, o_ref):
        tot = x_ref[0]
        for s in range(1, n):
            tot = tot + x_ref[s]
        o_ref[...] = tot

    return pl.pallas_call(
        body, grid=(R // tr,), in_specs=[pl.BlockSpec((n, tr, LANE), lambda i: (0, i, 0))],
        out_specs=pl.BlockSpec((tr, LANE), lambda i: (i, 0)), out_shape=SDS((R, LANE), F32),
        compiler_params=_params(("parallel",)), name=name)(x)


def _add_pair(a, b, *, name):
    n, R, _ = a.shape
    tr = _tile(R, 2048, 8)

    def body(a_ref, b_ref, o_ref):
        o_ref[...] = a_ref[...] + b_ref[...]

    spec = pl.BlockSpec((1, tr, LANE), lambda s, i: (s, i, 0))
    return pl.pallas_call(
        body, grid=(n, R // tr), in_specs=[spec, spec], out_specs=spec, out_shape=SDS(a.shape, F32),
        compiler_params=_params(("parallel", "parallel")), name=name)(a, b)


def _place():
    return lax.axis_index("x"), lax.axis_index("y"), lax.axis_index("c")


def _allgather_small(v, *, name):
    m_per, n = v.shape

    def body(x_ref, out_ref, send_sems, recv_sems, local_sem):
        x, y, c = _place()
        me, sibling = (x, y, c), (x, y, 1 - c)
        chips = [(1 - x, y), (x, 1 - y), (1 - x, 1 - y)]

        def rows(px, py, pc):
            return out_ref.at[pl.ds((4 * px + 2 * py + pc) * m_per, m_per), :]

        def copy(k, block, to, src=None):
            return pltpu.make_async_remote_copy(
                src_ref=rows(*block) if src is None else src, dst_ref=rows(*block),
                send_sem=send_sems.at[k], recv_sem=recv_sems.at[k], device_id=to, device_id_type=MESH)

        mine = pltpu.make_async_copy(x_ref, rows(*me), local_sem)
        mine.start()
        first = [copy(0, me, sibling, src=x_ref)]
        first += [copy(1 + j, me, (*chip, c), src=x_ref) for j, chip in enumerate(chips)]
        for cp in first:
            cp.start()
        passed = [copy(4 + j, (*chip, c), sibling) for j, chip in enumerate(chips)]
        for j, chip in enumerate(chips):
            copy(1 + j, (*chip, c), me).wait_recv()
            passed[j].start()
        copy(0, sibling, me).wait_recv()
        for j, chip in enumerate(chips):
            copy(4 + j, (*chip, 1 - c), me).wait_recv()
        for cp in first + passed:
            cp.wait_send()
        mine.wait()

    return pl.pallas_call(
        body, out_shape=SDS((8 * m_per, n), v.dtype),
        in_specs=[pl.BlockSpec(memory_space=pltpu.VMEM)], out_specs=pl.BlockSpec(memory_space=pltpu.VMEM),
        scratch_shapes=[pltpu.SemaphoreType.DMA((7,)), pltpu.SemaphoreType.DMA((7,)), pltpu.SemaphoreType.DMA],
        name=name)(v)


def _gather_weights(wp, *, name):
    _, R, _ = wp.shape

    def body(w_ref, out_ref, send_sems, recv_sems, local_sem):
        x, y, c = _place()
        sibling = (x, y, 1 - c)
        chips = [(1 - x, y), (x, 1 - y), (1 - x, 1 - y)]

        def slot(px, py, half):
            return out_ref.at[2 * px + py, half]

        def copy(k, src, dst, to):
            return pltpu.make_async_remote_copy(src_ref=src, dst_ref=dst, send_sem=send_sems.at[k],
                                                recv_sem=recv_sems.at[k], device_id=to, device_id_type=MESH)

        mine = pltpu.make_async_copy(w_ref, out_ref.at[2 * x + y], local_sem)
        mine.start()
        first = [copy(j, w_ref.at[c], slot(x, y, c), (*chip, c)) for j, chip in enumerate(chips)]
        for cp in first:
            cp.start()
        passed = [copy(3 + j, slot(*chip, c), slot(*chip, c), sibling) for j, chip in enumerate(chips)]
        for j, chip in enumerate(chips):
            copy(j, w_ref.at[c], slot(*chip, c), (*chip, c)).wait_recv()
            passed[j].start()
        for j, chip in enumerate(chips):
            copy(3 + j, slot(*chip, 1 - c), slot(*chip, 1 - c), sibling).wait_recv()
        for cp in first + passed:
            cp.wait_send()
        mine.wait()

    anyspec = pl.BlockSpec(memory_space=pl.ANY)
    return pl.pallas_call(
        body, out_shape=SDS((4, 2, R, LANE), wp.dtype), in_specs=[anyspec], out_specs=anyspec,
        scratch_shapes=[pltpu.SemaphoreType.DMA((6,)), pltpu.SemaphoreType.DMA((6,)), pltpu.SemaphoreType.DMA],
        name=name)(wp)


def _pair_exchange(g, *, name):
    _, n, R, _ = g.shape

    def body(g_ref, out_ref, send_sem, recv_sem):
        x, y, c = _place()
        cp = pltpu.make_async_remote_copy(src_ref=g_ref.at[1 - c], dst_ref=out_ref, send_sem=send_sem,
                                          recv_sem=recv_sem, device_id=(x, y, 1 - c), device_id_type=MESH)
        cp.start()
        cp.wait()

    anyspec = pl.BlockSpec(memory_space=pl.ANY)
    return pl.pallas_call(
        body, out_shape=SDS((n, R, LANE), g.dtype), in_specs=[anyspec], out_specs=anyspec,
        scratch_shapes=[pltpu.SemaphoreType.DMA, pltpu.SemaphoreType.DMA], name=name)(g)


def _chip_exchange(part, *, name):
    n, R, _ = part.shape

    def body(p_ref, out_ref, send_sems, recv_sems, local_sem):
        x, y, c = _place()
        s_me = 2 * x + y
        chips = [(1 - x, y), (x, 1 - y), (1 - x, 1 - y)]

        def copy(k, chip):
            return pltpu.make_async_remote_copy(
                src_ref=p_ref.at[2 * chip[0] + chip[1]], dst_ref=out_ref.at[s_me], send_sem=send_sems.at[k],
                recv_sem=recv_sems.at[k], device_id=(*chip, c), device_id_type=MESH)

        def landed(k, chip):
            return pltpu.make_async_remote_copy(
                src_ref=p_ref.at[s_me], dst_ref=out_ref.at[2 * chip[0] + chip[1]], send_sem=send_sems.at[k],
                recv_sem=recv_sems.at[k], device_id=(*chip, c), device_id_type=MESH)

        mine = pltpu.make_async_copy(p_ref.at[s_me], out_ref.at[s_me], local_sem)
        mine.start()
        sends = [copy(k, chip) for k, chip in enumerate(chips)]
        for cp in sends:
            cp.start()
        for k, chip in enumerate(chips):
            landed(k, chip).wait_recv()
        for cp in sends:
            cp.wait_send()
        mine.wait()

    anyspec = pl.BlockSpec(memory_space=pl.ANY)
    return pl.pallas_call(
        body, out_shape=SDS((n, R, LANE), part.dtype), in_specs=[anyspec], out_specs=anyspec,
        scratch_shapes=[pltpu.SemaphoreType.DMA((3,)), pltpu.SemaphoreType.DMA((3,)), pltpu.SemaphoreType.DMA],
        name=name)(part)


def _pair_share(red, *, name):
    R, _ = red.shape

    def body(r_ref, out_ref, send_sem, recv_sem, local_sem):
        x, y, c = _place()
        mine = pltpu.make_async_copy(r_ref, out_ref.at[c], local_sem)
        mine.start()
        cp = pltpu.make_async_remote_copy(src_ref=r_ref, dst_ref=out_ref.at[c], send_sem=send_sem,
                                          recv_sem=recv_sem, device_id=(x, y, 1 - c), device_id_type=MESH)
        cp.start()
        pltpu.make_async_remote_copy(src_ref=r_ref, dst_ref=out_ref.at[1 - c], send_sem=send_sem,
                                     recv_sem=recv_sem, device_id=(x, y, 1 - c), device_id_type=MESH).wait_recv()
        cp.wait_send()
        mine.wait()

    anyspec = pl.BlockSpec(memory_space=pl.ANY)
    return pl.pallas_call(
        body, out_shape=SDS((2, R, LANE), red.dtype), in_specs=[anyspec], out_specs=anyspec,
        scratch_shapes=[pltpu.SemaphoreType.DMA, pltpu.SemaphoreType.DMA, pltpu.SemaphoreType.DMA], name=name)(red)


def _chips(x, y):
    return [(1 - x, y), (x, 1 - y), (1 - x, 1 - y)]


def _gather_mats(shards, *, name):
    n = len(shards)

    def body(*refs):
        ins, outs = refs[:n], refs[n:2 * n]
        send_sems, recv_sems, fsend_sems, frecv_sems = refs[2 * n:]
        x, y, c = _place()
        s_me, sibling, chips = 2 * x + y, (x, y, 1 - c), _chips(x, y)

        def copy(src, dst, ssem, rsem, to):
            return pltpu.make_async_remote_copy(src_ref=src, dst_ref=dst, send_sem=ssem, recv_sem=rsem,
                                                device_id=to, device_id_type=MESH)

        first = [copy(ins[t].at[c], outs[t].at[s_me, c], send_sems.at[3 * t + k], recv_sems.at[3 * t + k], (*chip, c))
                 for t in range(n) for k, chip in enumerate(chips)]
        for cp in first:
            cp.start()
        passed = []
        for t in range(n):
            for k, chip in enumerate(chips):
                slot = outs[t].at[2 * chip[0] + chip[1], c]
                copy(ins[t].at[c], slot, send_sems.at[3 * t + k], recv_sems.at[3 * t + k], (*chip, c)).wait_recv()
                fw = copy(slot, slot, fsend_sems.at[3 * t + k], frecv_sems.at[3 * t + k], sibling)
                fw.start()
                passed.append(fw)
        for t in range(n):
            for k, chip in enumerate(chips):
                slot = outs[t].at[2 * chip[0] + chip[1], 1 - c]
                copy(slot, slot, fsend_sems.at[3 * t + k], frecv_sems.at[3 * t + k], sibling).wait_recv()
        for cp in first + passed:
            cp.wait_send()

    anyspec = pl.BlockSpec(memory_space=pl.ANY)
    return pl.pallas_call(
        body, out_shape=[SDS((4,) + s.shape, s.dtype) for s in shards], in_specs=[anyspec] * n, out_specs=[anyspec] * n,
        scratch_shapes=[pltpu.SemaphoreType.DMA((3 * n,))] * 4, name=name)(*shards)


def _rs_pair_exchange(gs, *, name):
    n = len(gs)

    def body(*refs):
        ins, outs = refs[:n], refs[n:2 * n]
        send_sems, recv_sems = refs[2 * n:]
        x, y, c = _place()
        cps = [pltpu.make_async_remote_copy(src_ref=ins[t].at[k, 1 - c], dst_ref=outs[t].at[k],
                                            send_sem=send_sems.at[4 * t + k], recv_sem=recv_sems.at[4 * t + k],
                                            device_id=(x, y, 1 - c), device_id_type=MESH)
               for t in range(n) for k in range(4)]
        for cp in cps:
            cp.start()
        for cp in cps:
            cp.wait()

    anyspec = pl.BlockSpec(memory_space=pl.ANY)
    return pl.pallas_call(
        body, out_shape=[SDS((4,) + g.shape[2:], g.dtype) for g in gs], in_specs=[anyspec] * n, out_specs=[anyspec] * n,
        scratch_shapes=[pltpu.SemaphoreType.DMA((4 * n,))] * 2, name=name)(*gs)


def _rs_chip_exchange(parts, *, name):
    n = len(parts)

    def body(*refs):
        ins, outs = refs[:n], refs[n:2 * n]
        send_sems, recv_sems, local_sems = refs[2 * n:]
        x, y, c = _place()
        s_me, chips = 2 * x + y, _chips(x, y)

        def copy(t, k, chip, src_slot, dst_slot):
            return pltpu.make_async_remote_copy(
                src_ref=ins[t].at[src_slot], dst_ref=outs[t].at[dst_slot], send_sem=send_sems.at[3 * t + k],
                recv_sem=recv_sems.at[3 * t + k], device_id=(*chip, c), device_id_type=MESH)

        mine = [pltpu.make_async_copy(ins[t].at[s_me], outs[t].at[s_me], local_sems.at[t]) for t in range(n)]
        for cp in mine:
            cp.start()
        sends = [copy(t, k, chip, 2 * chip[0] + chip[1], s_me) for t in range(n) for k, chip in enumerate(chips)]
        for cp in sends:
            cp.start()
        for t in range(n):
            for k, chip in enumerate(chips):
                copy(t, k, chip, s_me, 2 * chip[0] + chip[1]).wait_recv()
        for cp in sends:
            cp.wait_send()
        for cp in mine:
            cp.wait()

    anyspec = pl.BlockSpec(memory_space=pl.ANY)
    return pl.pallas_call(
        body, out_shape=[SDS(p.shape, p.dtype) for p in parts], in_specs=[anyspec] * n, out_specs=[anyspec] * n,
        scratch_shapes=[pltpu.SemaphoreType.DMA((3 * n,))] * 2 + [pltpu.SemaphoreType.DMA((n,))], name=name)(*parts)


def _rs_pair_share(fulls, *, name):
    n = len(fulls)

    def body(*refs):
        ins, outs = refs[:n], refs[n:2 * n]
        send_sems, recv_sems = refs[2 * n:]
        x, y, c = _place()

        def copy(t, half):
            return pltpu.make_async_remote_copy(src_ref=ins[t].at[c], dst_ref=outs[t].at[half], send_sem=send_sems.at[t],
                                                recv_sem=recv_sems.at[t], device_id=(x, y, 1 - c), device_id_type=MESH)

        sends = [copy(t, c) for t in range(n)]
        for cp in sends:
            cp.start()
        for t in range(n):
            copy(t, 1 - c).wait_recv()
        for cp in sends:
            cp.wait_send()

    anyspec = pl.BlockSpec(memory_space=pl.ANY)
    return pl.pallas_call(
        body, out_shape=[SDS(f.shape, f.dtype) for f in fulls], in_specs=[anyspec] * n, out_specs=[anyspec] * n,
        input_output_aliases={t: t for t in range(n)},
        scratch_shapes=[pltpu.SemaphoreType.DMA((n,))] * 2, name=name)(*fulls)


def _add_half(g, other, c1, *, out_dtype, name):
    _, _, h, cs = g.shape
    tr = _tile(h, 512, 16)

    def body(c_ref, g_ref, o_ref, out_ref):
        out_ref[...] = (g_ref[...] + o_ref[...]).astype(out_dtype)

    spec = pl.BlockSpec((None, tr, cs), lambda k, i, c: (k, i, 0))
    return pl.pallas_call(
        body, out_shape=SDS(other.shape, out_dtype),
        grid_spec=pltpu.PrefetchScalarGridSpec(
            num_scalar_prefetch=1, grid=(4, h // tr),
            in_specs=[pl.BlockSpec((None, None, tr, cs), lambda k, i, c: (k, c[0], i, 0)), spec], out_specs=spec),
        compiler_params=_params(("parallel", "parallel")), name=name)(c1, g, other)


def _sum4(x, c1, *, name):
    n, h, cs = x.shape
    tr = _tile(h, 512, 16)

    def body(c_ref, x_ref, o_ref):
        tot = x_ref[0].astype(F32)
        for s in range(1, n):
            tot = tot + x_ref[s].astype(F32)
        o_ref[...] = tot

    return pl.pallas_call(
        body, out_shape=SDS((2, h, cs), F32),
        grid_spec=pltpu.PrefetchScalarGridSpec(
            num_scalar_prefetch=1, grid=(h // tr,),
            in_specs=[pl.BlockSpec((n, tr, cs), lambda i, c: (0, i, 0))],
            out_specs=pl.BlockSpec((None, tr, cs), lambda i, c: (c[0], i, 0))),
        compiler_params=_params(("parallel",)), name=name)(c1, x)


def _finish_small(gathered, lbf, lbb, *, rows, name):
    r_lbf, r_lbb = rows['lb_f'], rows['lb_b']

    def body(g_ref, lbf_ref, lbb_ref, o_ref, dlf_ref, dlb_ref):
        tot = g_ref[0]
        for s in range(1, 8):
            tot = tot + g_ref[s]
        o_ref[...] = tot
        o_ref[0:1, :] = jnp.broadcast_to(jnp.sum(o_ref[0:1, :], axis=1, keepdims=True), (1, LANE))
        for lb_ref, d_ref, r0 in ((lbf_ref, dlf_ref, r_lbf), (lbb_ref, dlb_ref, r_lbb)):
            for hh in range(HG_HEADS):
                sl = slice(LANE * hh, LANE * (hh + 1))
                l0, l1 = lb_ref[0:1, sl], lb_ref[1:2, sl]
                mx = jnp.maximum(l0, l1)
                e0, e1 = jnp.exp(l0 - mx), jnp.exp(l1 - mx)
                p0 = e0 / (e0 + e1)
                d0 = o_ref[r0 + hh:r0 + hh + 1, :] * p0 * (1.0 - p0)
                d_ref[0:1, sl] = d0
                d_ref[1:2, sl] = -d0

    vm = pl.BlockSpec(memory_space=pltpu.VMEM)
    return pl.pallas_call(
        body, in_specs=[vm, vm, vm], out_specs=[vm, vm, vm],
        out_shape=[SDS(gathered.shape[1:], F32), SDS(lbf.shape, F32), SDS(lbb.shape, F32)], name=name)(gathered, lbf, lbb)


def _pad_len(n):
    q = 2 * 128 * LANE
    return (n + q - 1) // q * q


def _pack_local(shards, dtype):
    flat = jnp.concatenate([s.astype(dtype).reshape(-1) for s in shards])
    n = flat.shape[0]
    flat = jnp.pad(flat, (0, _pad_len(n) - n))
    return flat.reshape(2, -1, LANE)


def _unpack_full(gathered, shapes, names):
    out, off = {}, 0
    for name in names:
        r, cs = shapes[name]
        blk = gathered[:, off:off + r * cs].reshape(4, r, cs)
        off += r * cs
        out[name] = blk.reshape(4 * r, cs) if name in ROW_SHARDED else blk.transpose(1, 0, 2).reshape(r, 4 * cs)
    return out


def _pack_grads(grads, shapes, names):
    cols = []
    for name in names:
        r, cs = shapes[name]
        g = grads[name]
        blk = g.reshape(4, r * cs) if name in ROW_SHARDED else g.reshape(r, 4, cs).transpose(1, 0, 2).reshape(4, r * cs)
        cols.append(blk)
    flat = jnp.concatenate(cols, axis=1)
    n = flat.shape[1]
    flat = jnp.pad(flat, ((0, 0), (0, _pad_len(n) - n)))
    return flat.reshape(4, 2, -1, LANE).transpose(1, 0, 2, 3)


def _unpack_shard(flat, shapes, names):
    out, off = {}, 0
    for name in names:
        r, cs = shapes[name]
        out[name] = flat[off:off + r * cs].reshape(r, cs)
        off += r * cs
    return out


def _local_step(x2, tgt2, meta, W, S):
    T, D = x2.shape
    L = PAD + T
    h0 = jnp.concatenate([jnp.zeros((PAD - N_META, D), F32), meta, x2], axis=0)

    qk0 = Z_HG
    w_in = jnp.concatenate([W['w_in'][:, :qk0], _qk_to_group(W['w_in'][:, qk0:qk0 + AT_W + AT_KVW]),
                            W['w_in'][:, qk0 + AT_W + AT_KVW:]], axis=1)
    cc, ss = _rope_tables(L)
    wq_g, wk_g = _group_vec(S['q_norm']), _group_vec(S['k_norm'])

    def ffn_fwd(h, nw, wg, wu, wd, tag):
        n = _rmsnorm_fwd(h, nw, name=tag + "_norm")
        g, u, a = _ffn4_up(n, wg, wu, name=tag + "_up")
        hn = _ffn4_down(a, wd, h, name=tag + "_down")
        return hn, (n, g, u, a)

    def ffn_bwd(dh, h, nw, wg, wu, wd, saved, tag):
        n, g, u, a = saved
        dg, du, dn = _ffn4_dact_dn(dh, wd, g, u, wg, wu, name=tag + "_dact_dn")
        dwg = _ffn4_dw(n, dg, x_is_rows=True, name=tag + "_dwg")
        dwu = _ffn4_dw(n, du, x_is_rows=True, name=tag + "_dwu")
        dwd = _ffn4_dw(dh, a, x_is_rows=False, alpha=0.5, name=tag + "_dwd")
        dhp, dnw = _rmsnorm_bwd(h, nw, dn, dh, name=tag + "_norm_bwd")
        return dhp, dnw, dwg, dwu, dwd

    h1, sv1 = ffn_fwd(h0, S['ffn1_norm'], W['ffn1_w_gate'], W['ffn1_w_up'], W['ffn1_w_down'], "ffn1")
    um = _rmsnorm_fwd(h1, S['mix_norm'], name="mix_norm")
    z = _mm([(um, w_in)], tm=512, tn=1792, tk=D, name="in_proj")
    of, sf = _hg_fwd(z, S['hg_lb_fwd'], rev=False, name="hg_fwd_f")
    ob, sb = _hg_fwd(z, S['hg_lb_bwd'], rev=True, name="hg_fwd_b")
    ya = _hg_post_fwd(of, ob, z, S['hg_out_norm'], name="hg_post")
    qm, qt, kr, krt, vb, vt = _at_prep(z, cc, ss, wq_g, wk_g, name="at_prep")
    yb, yb_f32, lse = _at_fwd(qt, kr, vt, name="at_fwd")
    mixed = _merge_fwd(ya, yb, W['w_up_a'], W['w_up_b'], z, name="merge")
    h2 = _mm([(mixed, W['w_out'])], res=h1, tm=512, tn=D, tk=D, name="out_proj")
    h3, sv2 = ffn_fwd(h2, S['ffn2_norm'], W['ffn2_w_gate'], W['ffn2_w_up'], W['ffn2_w_down'], "ffn2")
    dh3, loss_lanes = _loss_head(h3, tgt2, name="loss_head")

    G = {}
    dh2, dn_ffn2, G['ffn2_w_gate'], G['ffn2_w_up'], G['ffn2_w_down'] = ffn_bwd(
        dh3, h2, S['ffn2_norm'], W['ffn2_w_gate'], W['ffn2_w_up'], W['ffn2_w_down'], sv2, "ffn2")
    dpa, dpb, dzga, dzgb = _merge_bwd(dh2, W['w_out'], ya, yb, W['w_up_a'], W['w_up_b'], z, name="merge_bwd")
    G['w_out'] = _mm([(mixed, dh2)], ta=True, tm=D, tn=D, tk=512, name="d_w_out")
    dya = _mm([(dpa, W['w_up_a'])], tb=True, tm=512, tn=HG_W, tk=D, name="d_ya")
    dyb = _mm([(dpb, W['w_up_b'])], tb=True, tm=512, tn=AT_W, tk=D, name="d_yb")
    G['w_up_a'] = _mm([(ya, dpa)], ta=True, tm=HG_W, tn=D, tk=512, name="d_w_up_a")
    G['w_up_b'] = _mm([(yb, dpb)], ta=True, tm=AT_W, tn=D, tk=512, name="d_w_up_b")
    do_hg, dzg, d_hgn = _hg_post_bwd(dya, of, ob, z, S['hg_out_norm'], name="hg_post_bwd")
    dq_f, dv_f, dzf_f, dlb_f = _hg_bwd(z, S['hg_lb_fwd'], do_hg, sf, None, rev=False, name="hg_bwd_f")
    dzq, dzi, dzf_b, dlb_b = _hg_bwd(z, S['hg_lb_bwd'], do_hg, sb, (dq_f, dv_f), rev=True, name="hg_bwd_b")
    dqm, dk2, dv2 = _at_bwd(qm, qt, kr, krt, vb, dyb, yb_f32, lse, name="at_bwd")
    dz_at, dwq_g, dwk_g = _at_prep_bwd(dqm, dk2, dv2, z, cc, ss, wq_g, wk_g, name="at_prep_bwd")
    dz = jnp.concatenate([dzq, dzi, dzf_f, dzf_b, dzg, dz_at, dzga, dzgb], axis=1)
    dum = _mm([(dz, w_in)], tb=True, tm=512, tn=D, tk=1792, name="d_um")
    dw_in_p = _mm([(um, dz)], ta=True, tm=D, tn=1792, tk=512, name="d_w_in")
    G['w_in'] = jnp.concatenate([dw_in_p[:, :qk0], _qk_from_group(dw_in_p[:, qk0:qk0 + AT_W + AT_KVW]),
                                 dw_in_p[:, qk0 + AT_W + AT_KVW:]], axis=1)
    dh1, dn_mix = _rmsnorm_bwd(h1, S['mix_norm'], dum, dh2, name="mix_norm_bwd")
    dh0, dn_ffn1, G['ffn1_w_gate'], G['ffn1_w_up'], G['ffn1_w_down'] = ffn_bwd(
        dh1, h0, S['ffn1_norm'], W['ffn1_w_gate'], W['ffn1_w_up'], W['ffn1_w_down'], sv1, "ffn1")

    small_rows = [('loss', loss_lanes), ('ffn1_norm', dn_ffn1.reshape(-1, LANE)), ('mix_norm', dn_mix.reshape(-1, LANE)),
                  ('ffn2_norm', dn_ffn2.reshape(-1, LANE)), ('hg_out_norm', d_hgn.reshape(-1, LANE)),
                  ('lb_f', dlb_f.reshape(-1, LANE)), ('lb_b', dlb_b.reshape(-1, LANE)), ('q_norm', dwq_g), ('k_norm', dwk_g)]
    return dh0[PAD:], dh0[PAD - N_META:PAD], G, small_rows


def kernel(x, meta_tokens, ffn1_norm, ffn1_w_gate, ffn1_w_up, ffn1_w_down, mix_norm, w_in, hg_lb_fwd, hg_lb_bwd, hg_out_norm, q_norm, k_norm, w_up_a, w_up_b, w_out, ffn2_norm, ffn2_w_gate, ffn2_w_up, ffn2_w_down, loss_target, m_meta_tokens, m_ffn1_norm, m_ffn1_w_gate, m_ffn1_w_up, m_ffn1_w_down, m_mix_norm, m_w_in, m_hg_lb_fwd, m_hg_lb_bwd, m_hg_out_norm, m_q_norm, m_k_norm, m_w_up_a, m_w_up_b, m_w_out, m_ffn2_norm, m_ffn2_w_gate, m_ffn2_w_up, m_ffn2_w_down, v_meta_tokens, v_ffn1_norm, v_ffn1_w_gate, v_ffn1_w_up, v_ffn1_w_down, v_mix_norm, v_w_in, v_hg_lb_fwd, v_hg_lb_bwd, v_hg_out_norm, v_q_norm, v_k_norm, v_w_up_a, v_w_up_b, v_w_out, v_ffn2_norm, v_ffn2_w_gate, v_ffn2_w_up, v_ffn2_w_down):
    given = dict(locals())
    w = {n: given[n] for n in WEIGHTS}
    mom = {n: given["m_" + n] for n in WEIGHTS}
    var = {n: given["v_" + n] for n in WEIGHTS}
    c = lax.axis_index("c")
    D = x.shape[-1]

    shapes = {n: w[n].shape[-2:] for n in MATS + ('meta_tokens',)}
    halves = [w[n].astype(BF16).reshape(2, shapes[n][0] // 2, shapes[n][1]) for n in MATS]
    gathered = _gather_mats(halves, name="gather_weights")
    s_me = 2 * lax.axis_index("x") + lax.axis_index("y")
    W = {}
    for n, hv, g4 in zip(MATS, halves, gathered):
        r, cs = shapes[n]
        g4 = lax.dynamic_update_index_in_dim(g4, hv, s_me, 0).reshape(4, r, cs)
        if n in FFN_MATS:
            W[n] = g4
        elif n in ROW_SHARDED:
            W[n] = g4.reshape(4 * r, cs)
        else:
            W[n] = g4.transpose(1, 0, 2).reshape(r, 4 * cs)
    meta_rows = w['meta_tokens'].reshape(-1, LANE)
    mg = _allgather_small(meta_rows, name="gather_meta").reshape(4, 2, N_META, -1)[:, 0]
    meta = mg.transpose(1, 0, 2).reshape(N_META, D)
    S = {n: w[n] for n in SMALLS}

    grad_x, dmeta, G, small_rows = _local_step(x[0], loss_target[0], meta, W, S)
    G['meta_tokens'] = dmeta

    names = MATS + ('meta_tokens',)
    views = []
    for n in names:
        r, cs = shapes[n]
        if n in FFN_MATS:
            g4 = G[n]
        elif n in ROW_SHARDED:
            g4 = G[n].reshape(4, r, cs)
        else:
            g4 = G[n].reshape(r, 4, cs).transpose(1, 0, 2)
        views.append(g4.reshape(4, 2, r // 2, cs))
    c1 = c.astype(jnp.int32).reshape(1)
    from_sibling = _rs_pair_exchange(views, name="rs_pair_exchange")
    parts = [_add_half(v, o, c1, out_dtype=F32 if n == 'meta_tokens' else BF16, name="rs_pair_sum_" + n)
             for n, v, o in zip(names, views, from_sibling)]
    slabs = _rs_chip_exchange(parts, name="rs_chip_exchange")
    reds = [_sum4(s, c1, name="rs_chip_sum_" + n) for n, s in zip(names, slabs)]
    both = _rs_pair_share(reds, name="rs_pair_share")
    grads = {n: b.reshape(w[n].shape) for n, b in zip(names, both)}

    rows, off = {}, 0
    for nme, blk in small_rows:
        rows[nme] = off
        off += blk.shape[0]
    block = jnp.concatenate([blk for _, blk in small_rows], axis=0)
    n_rows = (off + 7) // 8 * 8
    block = jnp.pad(block, ((0, n_rows - off), (0, 0)))
    allsmall = _allgather_small(block, name="gather_small").reshape(8, n_rows, LANE)
    tot, d_lbf, d_lbb = _finish_small(allsmall, w['hg_lb_fwd'], w['hg_lb_bwd'], rows=rows, name="finish_small")
    loss = 0.5 * tot[0, 0] / D

    def small(nme, shape):
        r0 = rows[nme]
        return tot[r0:r0 + shape[-1] // LANE].reshape(shape)

    grads['ffn1_norm'] = small('ffn1_norm', w['ffn1_norm'].shape)
    grads['mix_norm'] = small('mix_norm', w['mix_norm'].shape)
    grads['ffn2_norm'] = small('ffn2_norm', w['ffn2_norm'].shape)
    grads['hg_out_norm'] = small('hg_out_norm', w['hg_out_norm'].shape)
    grads['hg_lb_fwd'] = d_lbf
    grads['hg_lb_bwd'] = d_lbb
    grads['q_norm'] = _ungroup_vec(tot[rows['q_norm']])
    grads['k_norm'] = _ungroup_vec(tot[rows['k_norm']])

    delta, new_m, new_v = {}, {}, {}
    for n in WEIGHTS:
        delta[n], new_m[n], new_v[n] = _adamw(w[n], grads[n], mom[n], var[n], name="adamw_" + n)
    return (loss, grad_x[None], *[grads[n] for n in WEIGHTS], *[delta[n] for n in WEIGHTS],
            *[new_m[n] for n in WEIGHTS], *[new_v[n] for n in WEIGHTS])
```

```python
import functools
import math

import numpy as np
import jax
import jax.numpy as jnp
from jax import lax
from jax.experimental import pallas as pl
from jax.experimental.pallas import tpu as pltpu

F32 = jnp.float32
BF16 = jnp.bfloat16
SDS = jax.ShapeDtypeStruct
MESH = pl.DeviceIdType.MESH

EPS = 1e-6
N_META = 16
PAD = 512
LANE = 128
CHUNK = 128
HG_HEADS = 4
HG_W = HG_HEADS * 128
AT_HEADS = 8
AT_KV = 2
AT_HD = 64
AT_W = AT_HEADS * AT_HD
AT_KVW = AT_KV * AT_HD
VT_ROWS = AT_HD + 16
FWD_CHUNKS_PER_STEP = 4
GRID_W = 64
ROPE_THETA = 10000.0
Z_HG = 5 * HG_W
Z_AT = AT_W + 2 * AT_KVW
ADAM_LR, ADAM_B1, ADAM_B2, ADAM_EPS, ADAM_WD, ADAM_STEP = 0.001, 0.9, 0.999, 1e-08, 0.01, 10
VMEM_DEFAULT = 48 * 1024 * 1024
VMEM_LARGE = 60 * 1024 * 1024
NEG = -1e30

MATS = ('ffn1_w_gate', 'ffn1_w_up', 'ffn1_w_down', 'w_in', 'w_up_a', 'w_up_b', 'w_out',
        'ffn2_w_gate', 'ffn2_w_up', 'ffn2_w_down')
ROW_SHARDED = ('ffn1_w_down', 'w_out', 'ffn2_w_down')
FFN_MATS = ('ffn1_w_gate', 'ffn1_w_up', 'ffn1_w_down', 'ffn2_w_gate', 'ffn2_w_up', 'ffn2_w_down')
SMALLS = ('ffn1_norm', 'mix_norm', 'hg_lb_fwd', 'hg_lb_bwd', 'hg_out_norm', 'q_norm', 'k_norm', 'ffn2_norm')
WEIGHTS = ('meta_tokens', 'ffn1_norm', 'ffn1_w_gate', 'ffn1_w_up', 'ffn1_w_down', 'mix_norm', 'w_in', 'hg_lb_fwd',
           'hg_lb_bwd', 'hg_out_norm', 'q_norm', 'k_norm', 'w_up_a', 'w_up_b', 'w_out', 'ffn2_norm', 'ffn2_w_gate',
           'ffn2_w_up', 'ffn2_w_down')


def _params(sem=None, vmem=VMEM_DEFAULT):
    return pltpu.CompilerParams(dimension_semantics=sem, vmem_limit_bytes=vmem)


def _tile(n, pref, q=LANE):
    for d in range(min(pref, n), 0, -1):
        if n % d == 0 and d % q == 0:
            return d
    return n


def _sigmoid(x):
    return 1.0 / (1.0 + jnp.exp(-x))


def _dot(a, b, dims):
    return lax.dot_general(a, b, (dims, ((), ())), preferred_element_type=F32)


def _nn(a, b):
    return _dot(a, b, ((1,), (0,)))


def _nt(a, b):
    return _dot(a, b, ((1,), (1,)))


def _tn(a, b):
    return _dot(a, b, ((0,), (0,)))


def _split3(x):
    x1 = x.astype(BF16)
    r = x - x1.astype(F32)
    x2 = r.astype(BF16)
    x3 = (r - x2.astype(F32)).astype(BF16)
    return x1, x2, x3


def _exact_left(m01, x):
    x1, x2, x3 = _split3(x)
    return _nn(m01, x1) + _nn(m01, x2) + _nn(m01, x3)


def _exact_right(x, m01):
    x1, x2, x3 = _split3(x)
    return _nn(x1, m01) + _nn(x2, m01) + _nn(x3, m01)


def _mm(pairs, *, name, ta=False, tb=False, out_dtype=F32, tm=512, tn=1024, tk=1024, alpha=1.0, res=None):
    a0, b0 = pairs[0]
    M = a0.shape[1] if ta else a0.shape[0]
    K = a0.shape[0] if ta else a0.shape[1]
    N = b0.shape[0] if tb else b0.shape[1]
    tm, tn, tk = _tile(M, tm), _tile(N, tn), _tile(K, tk)
    nk = K // tk
    npair = len(pairs)
    dims = ((0 if ta else 1,), (1 if tb else 0,))

    def body(*refs):
        ab = refs[:2 * npair]
        pos = 2 * npair
        res_ref = None
        if res is not None:
            res_ref = refs[pos]
            pos += 1
        o_ref = refs[pos]

        def partial_sum():
            tot = None
            for p in range(npair):
                d = _dot(ab[2 * p][...].astype(BF16), ab[2 * p + 1][...].astype(BF16), dims)
                tot = d if tot is None else tot + d
            return tot

        def finish(acc):
            r = acc if alpha == 1.0 else acc * alpha
            if res_ref is not None:
                r = res_ref[...] + r
            o_ref[...] = r.astype(out_dtype)

        if nk == 1:
            finish(partial_sum())
        else:
            acc_ref = refs[pos + 1]
            k = pl.program_id(2)

            @pl.when(k == 0)
            def _():
                acc_ref[...] = jnp.zeros_like(acc_ref)

            acc_ref[...] += partial_sum()

            @pl.when(k == nk - 1)
            def _():
                finish(acc_ref[...])

    a_spec = pl.BlockSpec((tk, tm), lambda j, i, k: (k, i)) if ta else pl.BlockSpec((tm, tk), lambda j, i, k: (i, k))
    b_spec = pl.BlockSpec((tn, tk), lambda j, i, k: (j, k)) if tb else pl.BlockSpec((tk, tn), lambda j, i, k: (k, j))
    o_spec = pl.BlockSpec((tm, tn), lambda j, i, k: (i, j))
    in_specs, args = [], []
    for a, b in pairs:
        in_specs += [a_spec, b_spec]
        args += [a, b]
    if res is not None:
        in_specs.append(o_spec)
        args.append(res)
    return pl.pallas_call(
        body, grid=(N // tn, M // tm, nk), in_specs=in_specs, out_specs=o_spec,
        out_shape=SDS((M, N), out_dtype),
        scratch_shapes=[pltpu.VMEM((tm, tn), F32)] if nk > 1 else [],
        compiler_params=_params(("parallel", "parallel", "arbitrary")), name=name)(*args)


def _rmsnorm_fwd(h, w, *, name):
    L, D = h.shape
    tm = _tile(L, 512)

    def body(h_ref, w_ref, o_ref):
        x = h_ref[...]
        r = lax.rsqrt(jnp.mean(x * x, axis=-1, keepdims=True) + EPS)
        o_ref[...] = (x * r * w_ref[...]).astype(BF16)

    return pl.pallas_call(
        body, grid=(L // tm,),
        in_specs=[pl.BlockSpec((tm, D), lambda i: (i, 0)), pl.BlockSpec((1, D), lambda i: (0, 0))],
        out_specs=pl.BlockSpec((tm, D), lambda i: (i, 0)), out_shape=SDS((L, D), BF16),
        compiler_params=_params(("parallel",)), name=name)(h, w)


def _rmsnorm_bwd(h, w, dn, dres, *, name):
    L, D = h.shape
    tm = _tile(L, 512)

    def body(h_ref, w_ref, dn_ref, dres_ref, dh_ref, dw_ref):
        x = h_ref[...]
        r = lax.rsqrt(jnp.mean(x * x, axis=-1, keepdims=True) + EPS)
        xh = x * r
        dn = dn_ref[...]
        dxh = dn * w_ref[...]
        dh_ref[...] = dres_ref[...] + r * (dxh - xh * jnp.mean(dxh * xh, axis=-1, keepdims=True))

        @pl.when(pl.program_id(0) == 0)
        def _():
            dw_ref[...] = jnp.zeros_like(dw_ref)

        dw_ref[...] += jnp.sum(dn * xh, axis=0, keepdims=True)

    row = pl.BlockSpec((tm, D), lambda i: (i, 0))
    vec = pl.BlockSpec((1, D), lambda i: (0, 0))
    return pl.pallas_call(
        body, grid=(L // tm,), in_specs=[row, vec, row, row], out_specs=[row, vec],
        out_shape=[SDS((L, D), F32), SDS((1, D), F32)],
        compiler_params=_params(("arbitrary",)), name=name)(h, w, dn, dres)


def _ffn_up(n, wg, wu, *, name):
    L, D = n.shape
    Fd = wg.shape[1]
    tm, tn = _tile(L, 512), _tile(Fd, 1408)

    def body(n_ref, wg_ref, wu_ref, g_ref, u_ref, a_ref):
        x = n_ref[...]
        g = _nn(x, wg_ref[...])
        u = _nn(x, wu_ref[...])
        g_ref[...] = g
        u_ref[...] = u
        a_ref[...] = (g * _sigmoid(g) * u).astype(BF16)

    wspec = pl.BlockSpec((D, tn), lambda j, i: (0, j))
    ospec = pl.BlockSpec((tm, tn), lambda j, i: (i, j))
    return pl.pallas_call(
        body, grid=(Fd // tn, L // tm),
        in_specs=[pl.BlockSpec((tm, D), lambda j, i: (i, 0)), wspec, wspec],
        out_specs=[ospec, ospec, ospec],
        out_shape=[SDS((L, Fd), F32), SDS((L, Fd), F32), SDS((L, Fd), BF16)],
        compiler_params=_params(("parallel", "parallel")), name=name)(n, wg, wu)


def _ffn_dact(dh, wd, g, u, *, name):
    L, D = dh.shape
    Fd = wd.shape[0]
    tm, tn = _tile(L, 512), _tile(Fd, 1408)

    def body(dh_ref, wd_ref, g_ref, u_ref, dg_ref, du_ref):
        da = 0.5 * _nt(dh_ref[...].astype(BF16), wd_ref[...])
        g = g_ref[...]
        sg = _sigmoid(g)
        dg_ref[...] = (da * u_ref[...] * (sg * (1.0 + g * (1.0 - sg)))).astype(BF16)
        du_ref[...] = (da * (g * sg)).astype(BF16)

    ospec = pl.BlockSpec((tm, tn), lambda j, i: (i, j))
    return pl.pallas_call(
        body, grid=(Fd // tn, L // tm),
        in_specs=[pl.BlockSpec((tm, D), lambda j, i: (i, 0)), pl.BlockSpec((tn, D), lambda j, i: (j, 0)), ospec, ospec],
        out_specs=[ospec, ospec], out_shape=[SDS((L, Fd), BF16), SDS((L, Fd), BF16)],
        compiler_params=_params(("parallel", "parallel")), name=name)(dh, wd, g, u)


def _ffn4_up(n, wg4, wu4, *, name):
    L, D = n.shape
    ns, _, cs = wg4.shape
    tm = _tile(L, 768)

    def body(n_ref, wg_ref, wu_ref, g_ref, u_ref, a_ref):
        x = n_ref[...]
        g = _nn(x, wg_ref[...])
        u = _nn(x, wu_ref[...])
        g_ref[...] = g
        u_ref[...] = u
        a_ref[...] = (g * _sigmoid(g) * u).astype(BF16)

    wspec = pl.BlockSpec((None, D, cs), lambda j, i: (j, 0, 0))
    ospec = pl.BlockSpec((None, tm, cs), lambda j, i: (j, i, 0))
    return pl.pallas_call(
        body, grid=(ns, L // tm),
        in_specs=[pl.BlockSpec((tm, D), lambda j, i: (i, 0)), wspec, wspec], out_specs=[ospec, ospec, ospec],
        out_shape=[SDS((ns, L, cs), F32), SDS((ns, L, cs), F32), SDS((ns, L, cs), BF16)],
        compiler_params=_params(("parallel", "parallel")), name=name)(n, wg4, wu4)


def _ffn4_down(a4, wd4, h, *, name):
    ns, L, cs = a4.shape
    D = wd4.shape[2]
    tm = _tile(L, 512)

    def body(a_ref, w_ref, h_ref, o_ref):
        acc = _nn(a_ref[0], w_ref[0])
        for j in range(1, ns):
            acc = acc + _nn(a_ref[j], w_ref[j])
        o_ref[...] = h_ref[...] + 0.5 * acc

    row = pl.BlockSpec((tm, D), lambda i: (i, 0))
    return pl.pallas_call(
        body, grid=(L // tm,),
        in_specs=[pl.BlockSpec((ns, tm, cs), lambda i: (0, i, 0)), pl.BlockSpec((ns, cs, D), lambda i: (0, 0, 0)), row],
        out_specs=row, out_shape=SDS((L, D), F32),
        compiler_params=_params(("parallel",)), name=name)(a4, wd4, h)


def _ffn4_dact(dh, wd4, g4, u4, *, name):
    L, D = dh.shape
    ns, cs, _ = wd4.shape
    tm = _tile(L, 768)

    def body(dh_ref, wd_ref, g_ref, u_ref, dg_ref, du_ref):
        da = 0.5 * _nt(dh_ref[...].astype(BF16), wd_ref[...])
        g = g_ref[...]
        sg = _sigmoid(g)
        dg_ref[...] = (da * u_ref[...] * (sg * (1.0 + g * (1.0 - sg)))).astype(BF16)
        du_ref[...] = (da * (g * sg)).astype(BF16)

    ospec = pl.BlockSpec((None, tm, cs), lambda j, i: (j, i, 0))
    return pl.pallas_call(
        body, grid=(ns, L // tm),
        in_specs=[pl.BlockSpec((tm, D), lambda j, i: (i, 0)), pl.BlockSpec((None, cs, D), lambda j, i: (j, 0, 0)), ospec, ospec],
        out_specs=[ospec, ospec], out_shape=[SDS((ns, L, cs), BF16), SDS((ns, L, cs), BF16)],
        compiler_params=_params(("parallel", "parallel")), name=name)(dh, wd4, g4, u4)


def _ffn4_dact_dn(dh, wd4, g4, u4, wg4, wu4, *, name):
    L, D = dh.shape
    ns, cs, _ = wd4.shape
    tm = _tile(L, 768)

    def body(dh_ref, wd_ref, g_ref, u_ref, wg_ref, wu_ref, dg_ref, du_ref, dn_ref, acc_ref):
        j = pl.program_id(1)
        da = 0.5 * _nt(dh_ref[...].astype(BF16), wd_ref[...])
        g = g_ref[...]
        sg = _sigmoid(g)
        dg = (da * u_ref[...] * (sg * (1.0 + g * (1.0 - sg)))).astype(BF16)
        du = (da * (g * sg)).astype(BF16)
        dg_ref[...] = dg
        du_ref[...] = du
        t = _nt(dg, wg_ref[...]) + _nt(du, wu_ref[...])

        @pl.when(j == 0)
        def _():
            acc_ref[...] = t

        @pl.when(j > 0)
        def _():
            acc_ref[...] += t

        @pl.when(j == ns - 1)
        def _():
            dn_ref[...] = acc_ref[...]

    row = pl.BlockSpec((tm, D), lambda i, j: (i, 0))
    aspec = pl.BlockSpec((None, tm, cs), lambda i, j: (j, i, 0))
    wcol = pl.BlockSpec((None, D, cs), lambda i, j: (j, 0, 0))
    return pl.pallas_call(
        body, grid=(L // tm, ns),
        in_specs=[row, pl.BlockSpec((None, cs, D), lambda i, j: (j, 0, 0)), aspec, aspec, wcol, wcol],
        out_specs=[aspec, aspec, row],
        out_shape=[SDS((ns, L, cs), BF16), SDS((ns, L, cs), BF16), SDS((L, D), F32)],
        scratch_shapes=[pltpu.VMEM((tm, D), F32)],
        compiler_params=_params(("parallel", "arbitrary")), name=name)(dh, wd4, g4, u4, wg4, wu4)


def _ffn4_dn(dg4, du4, wg4, wu4, *, name):
    ns, L, cs = dg4.shape
    D = wg4.shape[1]
    tm = _tile(L, 512)

    def body(dg_ref, du_ref, wg_ref, wu_ref, o_ref):
        acc = None
        for j in range(ns):
            t = _nt(dg_ref[j], wg_ref[j]) + _nt(du_ref[j], wu_ref[j])
            acc = t if acc is None else acc + t
        o_ref[...] = acc

    aspec = pl.BlockSpec((ns, tm, cs), lambda i: (0, i, 0))
    wspec = pl.BlockSpec((ns, D, cs), lambda i: (0, 0, 0))
    return pl.pallas_call(
        body, grid=(L // tm,), in_specs=[aspec, aspec, wspec, wspec],
        out_specs=pl.BlockSpec((tm, D), lambda i: (i, 0)), out_shape=SDS((L, D), F32),
        compiler_params=_params(("parallel",), VMEM_LARGE), name=name)(dg4, du4, wg4, wu4)


def _ffn4_dw(x, y4, *, x_is_rows, alpha=1.0, name):
    L, D = x.shape
    ns, _, cs = y4.shape
    tk = _tile(L, 512)
    nk = L // tk
    oshape = (D, cs) if x_is_rows else (cs, D)

    def body(x_ref, y_ref, o_ref):
        k = pl.program_id(0)

        @pl.when(k == 0)
        def _():
            o_ref[...] = jnp.zeros_like(o_ref)

        xb = x_ref[...].astype(BF16)
        if x_is_rows:
            xt = xb.T
            for j in range(ns):
                o_ref[j] += _nn(xt, y_ref[j])
        else:
            for j in range(ns):
                o_ref[j] += _tn(y_ref[j], xb)

        if alpha != 1.0:
            @pl.when(k == nk - 1)
            def _():
                o_ref[...] = o_ref[...] * alpha

    return pl.pallas_call(
        body, grid=(nk,),
        in_specs=[pl.BlockSpec((tk, D), lambda k: (k, 0)), pl.BlockSpec((ns, tk, cs), lambda k: (0, k, 0))],
        out_specs=pl.BlockSpec((ns,) + oshape, lambda k: (0, 0, 0)), out_shape=SDS((ns,) + oshape, F32),
        compiler_params=_params(("arbitrary",)), name=name)(x, y4)


def _hg_masks(rev):
    t = lax.broadcasted_iota(jnp.int32, (CHUNK, CHUNK), 0)
    s = lax.broadcasted_iota(jnp.int32, (CHUNK, CHUNK), 1)
    causal = (s >= t) if rev else (s <= t)
    levels = []
    for sh in (6, 5, 4):
        same = jnp.right_shift(t, sh + 1) == jnp.right_shift(s, sh + 1)
        tq = jnp.bitwise_and(jnp.right_shift(t, sh), 1)
        sk = jnp.bitwise_and(jnp.right_shift(s, sh), 1)
        levels.append(same & (tq == (0 if rev else 1)) & (sk == (1 if rev else 0)))
    diag = (jnp.right_shift(t, 4) == jnp.right_shift(s, 4)) & causal
    return causal, levels, diag


def _hg_intra_factors(q, k, b, b_scr, rev):
    b_scr[...] = b
    row = lax.broadcasted_iota(jnp.int32, (CHUNK, LANE), 0)
    out = []
    for sh in (6, 5, 4):
        lb = 1 << sh
        pieces = []
        for p in range(0, CHUNK, 2 * lb):
            r = p + lb if rev else p + lb - 1
            pieces.append(jnp.broadcast_to(b_scr[pl.ds(r, 1), :], (2 * lb, LANE)))
        ref = pieces[0] if len(pieces) == 1 else jnp.concatenate(pieces, axis=0)
        qside = jnp.bitwise_and(jnp.right_shift(row, sh), 1) == (0 if rev else 1)
        eq = jnp.where(qside, jnp.exp(jnp.minimum(b - ref, 0.0)), 0.0)
        ek = jnp.where(qside, 0.0, jnp.exp(jnp.minimum(ref - b, 0.0)))
        out.append((eq, ek, (q * eq).astype(BF16), (k * ek).astype(BF16)))
    pieces = []
    for a in range(0, CHUNK, 16):
        r = a + (8 if rev else 7)
        pieces.append(jnp.broadcast_to(b_scr[pl.ds(r, 1), :], (16, LANE)))
    ref = jnp.concatenate(pieces, axis=0)
    eq = jnp.exp(jnp.minimum(b - ref, 80.0))
    ek = jnp.exp(jnp.minimum(ref - b, 80.0))
    out.append((eq, ek, (q * eq).astype(BF16), (k * ek).astype(BF16)))
    return out


def _hg_gate(zf, l0, l1, valid):
    mx = jnp.maximum(l0, l1)
    e0, e1 = jnp.exp(l0 - mx), jnp.exp(l1 - mx)
    p0 = e0 / (e0 + e1)
    sg = _sigmoid(-zf)
    k = jnp.where(valid, (1.0 - p0) * sg, 0.0)
    return p0, sg, k, jnp.log(1.0 - k)


def _hg_fwd(z, lbp, *, rev, name):
    L = z.shape[0]
    nc = L // CHUNK
    fcol = 3 if rev else 2

    def cidx(j):
        return nc - 1 - j if rev else j

    def body(zq_ref, zi_ref, zf_ref, lb_ref, o_ref, ssave_ref, st_scr, b_scr):
        j = pl.program_id(0)

        @pl.when(j == 0)
        def _():
            st_scr[...] = jnp.zeros_like(st_scr)

        causal, lmasks, dmask = _hg_masks(rev)
        tri = jnp.where(causal, 1.0, 0.0).astype(BF16)
        rowg = cidx(j) * CHUNK + lax.broadcasted_iota(jnp.int32, (CHUNK, LANE), 0)
        valid = rowg >= PAD - N_META
        last = 0 if rev else CHUNK - 1
        for hh in range(HG_HEADS):
            sl = slice(LANE * hh, LANE * (hh + 1))
            zq = zq_ref[:, sl]
            q = zq * _sigmoid(zq)
            v = zi_ref[:, sl].astype(BF16)
            _, _, k, g = _hg_gate(zf_ref[:, sl], lb_ref[0:1, sl], lb_ref[1:2, sl], valid)
            b = _exact_left(tri, g)
            st = st_scr[hh]
            ssave_ref[0, hh] = st
            o = _nt((q * jnp.exp(b)).astype(BF16), st.astype(BF16))
            a = None
            fac = _hg_intra_factors(q, k, b, b_scr, rev)
            for (eq, ek, qq, kk), msk in zip(fac, lmasks + [dmask]):
                t = jnp.where(msk, _nt(qq, kk), 0.0)
                a = t if a is None else a + t
            o_ref[:, sl] = o + _nn(a.astype(BF16), v)
            bl = b_scr[pl.ds(last, 1), :]
            kd = (k * jnp.exp(bl - b)).astype(BF16)
            st_scr[hh] = st * jnp.exp(bl) + _tn(v, kd)

    zspec = lambda col: pl.BlockSpec((CHUNK, HG_W), lambda j: (cidx(j), col))
    return pl.pallas_call(
        body, grid=(nc,),
        in_specs=[zspec(0), zspec(1), zspec(fcol), pl.BlockSpec((2, HG_W), lambda j: (0, 0))],
        out_specs=[pl.BlockSpec((CHUNK, HG_W), lambda j: (cidx(j), 0)),
                   pl.BlockSpec((1, HG_HEADS, LANE, LANE), lambda j: (cidx(j), 0, 0, 0))],
        out_shape=[SDS((L, HG_W), F32), SDS((nc, HG_HEADS, LANE, LANE), F32)],
        scratch_shapes=[pltpu.VMEM((HG_HEADS, LANE, LANE), F32), pltpu.VMEM((CHUNK, LANE), F32)],
        compiler_params=_params(("arbitrary",)), name=name)(z, z, z, lbp)


def _hg_bwd(z, lbp, do, ssave, prev, *, rev, name):
    L = z.shape[0]
    nc = L // CHUNK
    fcol = 3 if rev else 2
    final = prev is not None

    def cidx(j):
        return j if rev else nc - 1 - j

    def body(*refs):
        zq_ref, zi_ref, zf_ref, lb_ref, do_ref, ss_ref = refs[:6]
        pos = 6
        if final:
            dqin_ref, dvin_ref = refs[6:8]
            pos = 8
        dq_ref, dv_ref, dzf_ref, dlb_ref, dst_scr, b_scr = refs[pos:pos + 6]
        j = pl.program_id(0)

        @pl.when(j == 0)
        def _():
            dst_scr[...] = jnp.zeros_like(dst_scr)
            dlb_ref[...] = jnp.zeros_like(dlb_ref)

        causal, lmasks, dmask = _hg_masks(rev)
        tri = jnp.where(causal, 1.0, 0.0).astype(BF16)
        ti = lax.broadcasted_iota(jnp.int32, (CHUNK, CHUNK), 0)
        si = lax.broadcasted_iota(jnp.int32, (CHUNK, CHUNK), 1)
        tri_t = jnp.where((si <= ti) if rev else (si >= ti), 1.0, 0.0).astype(BF16)
        rowg = cidx(j) * CHUNK + lax.broadcasted_iota(jnp.int32, (CHUNK, LANE), 0)
        valid = rowg >= PAD - N_META
        last = 0 if rev else CHUNK - 1
        for hh in range(HG_HEADS):
            sl = slice(LANE * hh, LANE * (hh + 1))
            zq = zq_ref[:, sl]
            sq = _sigmoid(zq)
            q = zq * sq
            v = zi_ref[:, sl].astype(BF16)
            p0, sg, k, g = _hg_gate(zf_ref[:, sl], lb_ref[0:1, sl], lb_ref[1:2, sl], valid)
            b = _exact_left(tri, g)
            dob = do_ref[:, sl].astype(BF16)
            st = ss_ref[0, hh]
            dst = dst_scr[hh]
            stb, dstb = st.astype(BF16), dst.astype(BF16)
            eb = jnp.exp(b)
            qe = (q * eb).astype(BF16)
            fac = _hg_intra_factors(q, k, b, b_scr, rev)
            bl = b_scr[pl.ds(last, 1), :]
            ebl = jnp.exp(bl)
            kde = jnp.exp(bl - b)
            kd = (k * kde).astype(BF16)
            da = jnp.where(causal, _nt(dob, v), 0.0)
            dq = eb * _nn(dob, stb)
            dk_inter = kde * _nn(v, dstb)
            dk = dk_inter
            dv = _nt(kd, dstb)
            a = None
            db = q * dq - k * dk
            for (eq, ek, qq, kk), msk in zip(fac, lmasks + [dmask]):
                t = jnp.where(msk, _nt(qq, kk), 0.0)
                a = t if a is None else a + t
                dal = jnp.where(msk, da, 0.0).astype(BF16)
                mq = _nn(dal, kk)
                mk = _tn(dal, qq)
                dq = dq + eq * mq
                dk = dk + ek * mk
                db = db + (qq.astype(F32) * mq - kk.astype(F32) * mk)
            dv = dv + _tn(a.astype(BF16), dob)
            extra = ebl * jnp.sum(st * dst, axis=0, keepdims=True) + jnp.sum(k * dk_inter, axis=0, keepdims=True)
            dst_scr[hh] = dst * ebl + _tn(dob, qe)
            dg = _exact_left(tri_t, db) + extra
            dk_tot = dk - dg / (1.0 - k)
            dzf_ref[:, sl] = jnp.where(valid, dk_tot * (1.0 - p0) * (-sg * (1.0 - sg)), 0.0).astype(BF16)
            dlb_ref[:, sl] += jnp.sum(jnp.where(valid, -sg * dk_tot, 0.0), axis=0, keepdims=True)
            if final:
                dq_ref[:, sl] = ((dq + dqin_ref[:, sl]) * (sq * (1.0 + zq * (1.0 - sq)))).astype(BF16)
                dv_ref[:, sl] = (dv + dvin_ref[:, sl]).astype(BF16)
            else:
                dq_ref[:, sl] = dq
                dv_ref[:, sl] = dv

    zspec = lambda col: pl.BlockSpec((CHUNK, HG_W), lambda j: (cidx(j), col))
    rspec = pl.BlockSpec((CHUNK, HG_W), lambda j: (cidx(j), 0))
    in_specs = [zspec(0), zspec(1), zspec(fcol), pl.BlockSpec((2, HG_W), lambda j: (0, 0)), rspec,
                pl.BlockSpec((1, HG_HEADS, LANE, LANE), lambda j: (cidx(j), 0, 0, 0))]
    args = [z, z, z, lbp, do, ssave]
    if final:
        in_specs += [rspec, rspec]
        args += list(prev)
    odt = BF16 if final else F32
    return pl.pallas_call(
        body, grid=(nc,), in_specs=in_specs,
        out_specs=[rspec, rspec, rspec, pl.BlockSpec((1, HG_W), lambda j: (0, 0))],
        out_shape=[SDS((L, HG_W), odt), SDS((L, HG_W), odt), SDS((L, HG_W), BF16), SDS((1, HG_W), F32)],
        scratch_shapes=[pltpu.VMEM((HG_HEADS, LANE, LANE), F32), pltpu.VMEM((CHUNK, LANE), F32)],
        compiler_params=_params(("arbitrary",)), name=name)(*args)


def _hg_post_fwd(of, ob, z, w, *, name):
    L = of.shape[0]
    tm = _tile(L, 512)

    def body(of_ref, ob_ref, zg_ref, w_ref, y_ref):
        for hh in range(HG_HEADS):
            sl = slice(LANE * hh, LANE * (hh + 1))
            o = of_ref[:, sl] + ob_ref[:, sl]
            r = lax.rsqrt(jnp.mean(o * o, axis=-1, keepdims=True) + EPS)
            zg = zg_ref[:, sl]
            y_ref[:, sl] = (o * r * w_ref[:, sl] * (zg * _sigmoid(zg))).astype(BF16)

    row = pl.BlockSpec((tm, HG_W), lambda i: (i, 0))
    return pl.pallas_call(
        body, grid=(L // tm,),
        in_specs=[row, row, pl.BlockSpec((tm, HG_W), lambda i: (i, 4)), pl.BlockSpec((1, HG_W), lambda i: (0, 0))],
        out_specs=row, out_shape=SDS((L, HG_W), BF16),
        compiler_params=_params(("parallel",)), name=name)(of, ob, z, w)


def _hg_post_bwd(dy, of, ob, z, w, *, name):
    L = of.shape[0]
    tm = _tile(L, 512)

    def body(dy_ref, of_ref, ob_ref, zg_ref, w_ref, do_ref, dzg_ref, dw_ref):
        @pl.when(pl.program_id(0) == 0)
        def _():
            dw_ref[...] = jnp.zeros_like(dw_ref)

        for hh in range(HG_HEADS):
            sl = slice(LANE * hh, LANE * (hh + 1))
            o = of_ref[:, sl] + ob_ref[:, sl]
            r = lax.rsqrt(jnp.mean(o * o, axis=-1, keepdims=True) + EPS)
            xh = o * r
            zg = zg_ref[:, sl]
            sg = _sigmoid(zg)
            w = w_ref[:, sl]
            dy = dy_ref[:, sl]
            dys = dy * (zg * sg)
            dzg_ref[:, sl] = (dy * xh * w * (sg * (1.0 + zg * (1.0 - sg)))).astype(BF16)
            dw_ref[:, sl] += jnp.sum(dys * xh, axis=0, keepdims=True)
            dxh = dys * w
            do_ref[:, sl] = r * (dxh - xh * jnp.mean(dxh * xh, axis=-1, keepdims=True))

    row = pl.BlockSpec((tm, HG_W), lambda i: (i, 0))
    vec = pl.BlockSpec((1, HG_W), lambda i: (0, 0))
    return pl.pallas_call(
        body, grid=(L // tm,),
        in_specs=[row, row, row, pl.BlockSpec((tm, HG_W), lambda i: (i, 4)), vec],
        out_specs=[row, row, vec],
        out_shape=[SDS((L, HG_W), F32), SDS((L, HG_W), BF16), SDS((1, HG_W), F32)],
        compiler_params=_params(("arbitrary",)), name=name)(dy, of, ob, z, w)


N_GROUPS = (AT_HEADS + AT_KV) // 2


def _qk_to_group(wqk):
    d = wqk.shape[0]
    return wqk.reshape(d, N_GROUPS, 2, AT_HD // 2, 2).transpose(0, 1, 4, 2, 3).reshape(d, N_GROUPS * LANE)


def _qk_from_group(wqk):
    d = wqk.shape[0]
    return wqk.reshape(d, N_GROUPS, 2, 2, AT_HD // 2).transpose(0, 1, 3, 4, 2).reshape(d, N_GROUPS * LANE)


def _group_vec(w64):
    halves = w64.reshape(AT_HD // 2, 2).T
    return jnp.broadcast_to(halves[:, None, :], (2, 2, AT_HD // 2)).reshape(1, LANE)


def _ungroup_vec(w128):
    w = w128.reshape(2, 2, 32).sum(axis=1)
    return w.T.reshape(1, AT_HD)


def _rope_tables(L):
    n_real = L - PAD
    t = np.arange(n_real)
    row = np.concatenate([np.zeros(PAD), t // GRID_W]).astype(np.float32)
    col = np.concatenate([np.zeros(PAD), t % GRID_W]).astype(np.float32)
    inv = jnp.asarray(ROPE_THETA, F32) ** (-jnp.arange(0, AT_HD // 2, 2, dtype=F32) / (AT_HD // 2))
    ang = jnp.concatenate([jnp.asarray(row)[:, None] * inv, jnp.asarray(col)[:, None] * inv], axis=-1)
    cos, sin = jnp.cos(ang), jnp.sin(ang)
    cc = jnp.tile(cos, (1, 4))
    ss = jnp.concatenate([-sin, -sin, sin, sin], axis=1)
    return cc, ss


def _seg_matrix():
    a = lax.broadcasted_iota(jnp.int32, (LANE, LANE), 0)
    b = lax.broadcasted_iota(jnp.int32, (LANE, LANE), 1)
    same = jnp.bitwise_and(jnp.right_shift(a, 5), 1) == jnp.bitwise_and(jnp.right_shift(b, 5), 1)
    return jnp.where(same, 1.0, 0.0).astype(BF16)


def _slot_mask(shape, hp):
    lane = lax.broadcasted_iota(jnp.int32, shape, 1)
    return jnp.bitwise_and(jnp.right_shift(lane, 5), 1) == hp


def _at_prep(z, cc, ss, wq, wk, *, name):
    L = z.shape[0]
    tm = PAD
    qcol = Z_HG // AT_W
    kvcol = (Z_HG + AT_W) // (2 * LANE)

    def body(zq_ref, zkv_ref, cc_ref, ss_ref, wq_ref, wk_ref, qm_ref, qt_ref, kr_ref, krt_ref, vb_ref, vt_ref):
        seg = _seg_matrix()
        cc, ss = cc_ref[...], ss_ref[...]

        def normrope(x, w):
            r = lax.rsqrt(_exact_right(x * x, seg) * (1.0 / AT_HD) + EPS)
            y = x * r * w
            return y * cc + pltpu.roll(y, 64, 1) * ss

        for g in range(AT_HEADS // 2):
            o = normrope(zq_ref[:, LANE * g:LANE * (g + 1)], wq_ref[...]) * (AT_HD ** -0.5)
            for hp in range(2):
                h = 2 * g + hp
                tgt = h // (AT_HEADS // AT_KV)
                xm = jnp.where(_slot_mask(o.shape, hp), o, 0.0)
                if tgt != hp:
                    xm = pltpu.roll(xm, 32 if tgt == 1 else 96, 1)
                qm_ref[h] = xm.astype(BF16)
                qt_ref[h] = xm.T.astype(BF16)
        kr = normrope(zkv_ref[:, :LANE], wk_ref[...])
        kr_ref[...] = kr.astype(BF16)
        krt_ref[0] = kr.T.astype(BF16)
        v = zkv_ref[:, LANE:]
        low = lax.broadcasted_iota(jnp.int32, v.shape, 1) < AT_HD
        vb_ref[0] = jnp.where(low, v, 0.0).astype(BF16)
        vb_ref[1] = jnp.where(low, pltpu.roll(v, AT_HD, 1), 0.0).astype(BF16)
        vt = v.T.astype(BF16)
        ones = jnp.ones((VT_ROWS - AT_HD, tm), BF16)
        for j in range(AT_KV):
            vt_ref[j, 0, 0:AT_HD, :] = vt[AT_HD * j:AT_HD * (j + 1)]
            vt_ref[j, 0, AT_HD:VT_ROWS, :] = ones

    tab = pl.BlockSpec((tm, LANE), lambda i: (i, 0))
    vec = pl.BlockSpec((1, LANE), lambda i: (0, 0))
    nt = L // tm
    return pl.pallas_call(
        body, grid=(nt,),
        in_specs=[pl.BlockSpec((tm, AT_W), lambda i: (i, qcol)), pl.BlockSpec((tm, 2 * LANE), lambda i: (i, kvcol)),
                  tab, tab, vec, vec],
        out_specs=[pl.BlockSpec((AT_HEADS, tm, LANE), lambda i: (0, i, 0)),
                   pl.BlockSpec((AT_HEADS, LANE, tm), lambda i: (0, 0, i)), tab,
                   pl.BlockSpec((1, LANE, tm), lambda i: (i, 0, 0)),
                   pl.BlockSpec((AT_KV, tm, LANE), lambda i: (0, i, 0)),
                   pl.BlockSpec((AT_KV, 1, VT_ROWS, tm), lambda i: (0, i, 0, 0))],
        out_shape=[SDS((AT_HEADS, L, LANE), BF16), SDS((AT_HEADS, LANE, L), BF16), SDS((L, LANE), BF16),
                   SDS((nt, LANE, tm), BF16), SDS((AT_KV, L, LANE), BF16), SDS((AT_KV, nt, VT_ROWS, tm), BF16)],
        compiler_params=_params(("parallel",)), name=name)(z, z, cc, ss, wq, wk)


def _at_prep_bwd(dqm, dk2, dv2, z, cc, ss, wq, wk, *, name):
    L = z.shape[0]
    tm = PAD
    qcol = Z_HG // AT_W
    kvcol = (Z_HG + AT_W) // (2 * LANE)

    def body(dqm_ref, dk2_ref, dv2_ref, zq_ref, zkv_ref, cc_ref, ss_ref, wq_ref, wk_ref, dz_ref, dwq_ref, dwk_ref):
        @pl.when(pl.program_id(0) == 0)
        def _():
            dwq_ref[...] = jnp.zeros_like(dwq_ref)
            dwk_ref[...] = jnp.zeros_like(dwk_ref)

        seg = _seg_matrix()
        cc, ss = cc_ref[...], ss_ref[...]

        def back(x, w, do):
            dy = do * cc + pltpu.roll(do * ss, 64, 1)
            r = lax.rsqrt(_exact_right(x * x, seg) * (1.0 / AT_HD) + EPS)
            xh = x * r
            dxh = dy * w
            dx = r * (dxh - xh * (_exact_right(dxh * xh, seg) * (1.0 / AT_HD)))
            return dx, jnp.sum(dy * xh, axis=0, keepdims=True)

        for g in range(AT_HEADS // 2):
            do = None
            for hp in range(2):
                h = 2 * g + hp
                tgt = h // (AT_HEADS // AT_KV)
                d = jnp.where(_slot_mask((tm, LANE), tgt), dqm_ref[h], 0.0)
                if tgt != hp:
                    d = pltpu.roll(d, 96 if tgt == 1 else 32, 1)
                do = d if do is None else do + d
            dx, dw = back(zq_ref[:, LANE * g:LANE * (g + 1)], wq_ref[...], do * (AT_HD ** -0.5))
            dz_ref[:, LANE * g:LANE * (g + 1)] = dx.astype(BF16)
            dwq_ref[...] += dw
        dx, dw = back(zkv_ref[:, :LANE], wk_ref[...], dk2_ref[0] + dk2_ref[1])
        dz_ref[:, AT_W:AT_W + LANE] = dx.astype(BF16)
        dwk_ref[...] += dw
        dv0 = dv2_ref[0]
        low = lax.broadcasted_iota(jnp.int32, dv0.shape, 1) < AT_HD
        dz_ref[:, AT_W + LANE:] = jnp.where(low, dv0, pltpu.roll(dv2_ref[1], AT_HD, 1)).astype(BF16)

    tab = pl.BlockSpec((tm, LANE), lambda i: (i, 0))
    vec = pl.BlockSpec((1, LANE), lambda i: (0, 0))
    two = pl.BlockSpec((AT_KV, tm, LANE), lambda i: (0, i, 0))
    return pl.pallas_call(
        body, grid=(L // tm,),
        in_specs=[pl.BlockSpec((AT_HEADS, tm, LANE), lambda i: (0, i, 0)), two, two,
                  pl.BlockSpec((tm, AT_W), lambda i: (i, qcol)), pl.BlockSpec((tm, 2 * LANE), lambda i: (i, kvcol)),
                  tab, tab, vec, vec],
        out_specs=[pl.BlockSpec((tm, Z_AT), lambda i: (i, 0)), vec, vec],
        out_shape=[SDS((L, Z_AT), BF16), SDS((1, LANE), F32), SDS((1, LANE), F32)],
        compiler_params=_params(("arbitrary",)), name=name)(dqm, dk2, dv2, z, z, cc, ss, wq, wk)


def _at_fwd(qt, kr, vt, *, name):
    L = kr.shape[0]
    G = AT_HEADS // AT_KV
    tq = _tile(L, 384)
    tk = PAD
    nk = L // tk
    R = G * tq
    per = FWD_CHUNKS_PER_STEP if (nk - 1) % FWD_CHUNKS_PER_STEP == 0 else 1

    def body(q_ref, k_ref, v_ref, ob_ref, of_ref, lse_ref, m_scr, acc_scr):
        i = pl.program_id(1)
        qt = jnp.concatenate([q_ref[g] for g in range(G)], axis=1)
        m_scr[...] = jnp.full_like(m_scr, NEG)
        acc_scr[...] = jnp.zeros_like(acc_scr)

        def chunks(c, n, masked):
            start = c * tk if isinstance(c, int) else pl.multiple_of(c * tk, tk)
            st = _nn(k_ref[pl.ds(start, n * tk), :], qt)
            if masked:
                key = lax.broadcasted_iota(jnp.int32, st.shape, 0)
                st = jnp.where(key >= PAD - N_META, st, NEG)
            m_prev = m_scr[...]
            m_new = jnp.maximum(m_prev, jnp.max(st, axis=0, keepdims=True))
            pt = jnp.exp(st - m_new).astype(BF16)
            acc = jnp.exp(m_prev - m_new) * acc_scr[...]
            for u in range(n):
                acc = acc + _nn(v_ref[0, c + u], pt[u * tk:(u + 1) * tk])
            acc_scr[...] = acc
            m_scr[...] = m_new

        chunks(0, 1, True)

        def loop(t, carry):
            chunks(1 + per * t, per, False)
            return carry

        lax.fori_loop(0, (nk - 1) // per, loop, 0)
        l = acc_scr[pl.ds(AT_HD, 1), :]
        lse = m_scr[...] + jnp.log(l)
        on = acc_scr[0:AT_HD, :] / l
        o = jnp.concatenate([on[:, g * tq:(g + 1) * tq] for g in range(G)], axis=0).T
        rowg = i * tq + lax.broadcasted_iota(jnp.int32, o.shape, 0)
        o = jnp.where(rowg >= PAD - N_META, o, 0.0)
        ob_ref[...] = o.astype(BF16)
        of_ref[...] = o
        for g in range(G):
            lse_ref[g] = lse[:, g * tq:(g + 1) * tq]

    ospec = pl.BlockSpec((tq, G * AT_HD), lambda j, i: (i, j))
    return pl.pallas_call(
        body, grid=(AT_KV, L // tq),
        in_specs=[pl.BlockSpec((G, LANE, tq), lambda j, i: (j, 0, i)), pl.BlockSpec((L, LANE), lambda j, i: (0, 0)),
                  pl.BlockSpec((1, nk, VT_ROWS, tk), lambda j, i: (j, 0, 0, 0))],
        out_specs=[ospec, ospec, pl.BlockSpec((G, 1, tq), lambda j, i: (j, 0, i))],
        out_shape=[SDS((L, AT_W), BF16), SDS((L, AT_W), F32), SDS((AT_HEADS, 1, L), F32)],
        scratch_shapes=[pltpu.VMEM((1, R), F32), pltpu.VMEM((VT_ROWS, R), F32)],
        compiler_params=_params(("parallel", "parallel")), name=name)(qt, kr, vt)


def _at_bwd(qm, qt, kr, krt, vb, do, of, lse, *, name):
    L = kr.shape[0]
    G = AT_HEADS // AT_KV
    tq = _tile(L, 256)
    tk = PAD
    nk = L // tk
    nq = L // tq
    R = G * tq

    def body(qm_ref, q_ref, k_hbm, kt_hbm, v_hbm, do_ref, o_ref, lse_ref, dq_ref, dk_hbm, dv_hbm,
             k_scr, kt_scr, v_scr, dk_scr, dv_scr, dq_scr, sem):
        j, i = pl.program_id(0), pl.program_id(1)

        @pl.when(i == 0)
        def _():
            cps = [pltpu.make_async_copy(k_hbm, k_scr, sem.at[0]), pltpu.make_async_copy(kt_hbm, kt_scr, sem.at[1]),
                   pltpu.make_async_copy(v_hbm.at[j], v_scr, sem.at[2])]
            for cp in cps:
                cp.start()
            dk_scr[...] = jnp.zeros_like(dk_scr)
            dv_scr[...] = jnp.zeros_like(dv_scr)
            for cp in cps:
                cp.wait()

        qt = jnp.concatenate([q_ref[g] for g in range(G)], axis=1)
        rowg = i * tq + lax.broadcasted_iota(jnp.int32, (tq, G * AT_HD), 0)
        dot_all = jnp.where(rowg >= PAD - N_META, do_ref[...], 0.0).T
        ot_all = o_ref[...].T
        dot = jnp.concatenate([dot_all[AT_HD * g:AT_HD * (g + 1)] for g in range(G)], axis=1)
        ot = jnp.concatenate([ot_all[AT_HD * g:AT_HD * (g + 1)] for g in range(G)], axis=1)
        delta = jnp.sum(dot * ot, axis=0, keepdims=True)
        dot128 = jnp.concatenate([dot, jnp.zeros_like(dot)], axis=0)
        dor = dot128.T.astype(BF16)
        dot128 = dot128.astype(BF16)
        qr = qm_ref[...].reshape(R, LANE)
        lse_v = jnp.concatenate([lse_ref[g] for g in range(G)], axis=1)
        dq_scr[...] = jnp.zeros_like(dq_scr)

        def chunk(c, masked):
            start = c * tk if isinstance(c, int) else pl.multiple_of(c * tk, tk)
            k = k_scr[pl.ds(start, tk), :]
            kt = kt_scr[c]
            v = v_scr[pl.ds(start, tk), :]
            st = _nn(k, qt)
            if masked:
                key = lax.broadcasted_iota(jnp.int32, st.shape, 0)
                st = jnp.where(key >= PAD - N_META, st, NEG)
            pt = jnp.exp(st - lse_v)
            dst = (pt * (_nn(v, dot128) - delta)).astype(BF16)
            dq_scr[...] += _nn(kt, dst)
            dk_scr[pl.ds(start, tk), :] += _nn(dst, qr)
            dv_scr[pl.ds(start, tk), :] += _nn(pt.astype(BF16), dor)

        chunk(0, True)

        def loop(c, carry):
            chunk(c, False)
            return carry

        lax.fori_loop(1, nk, loop, 0)
        dq_ref[...] = dq_scr[...].T.reshape(G, tq, LANE)

        @pl.when(i == nq - 1)
        def _():
            ck = pltpu.make_async_copy(dk_scr, dk_hbm.at[j], sem.at[0])
            cv = pltpu.make_async_copy(dv_scr, dv_hbm.at[j], sem.at[1])
            ck.start()
            cv.start()
            ck.wait()
            cv.wait()

    anyspec = pl.BlockSpec(memory_space=pl.ANY)
    ospec = pl.BlockSpec((tq, G * AT_HD), lambda j, i: (i, j))
    return pl.pallas_call(
        body, grid=(AT_KV, nq),
        in_specs=[pl.BlockSpec((G, tq, LANE), lambda j, i: (j, i, 0)), pl.BlockSpec((G, LANE, tq), lambda j, i: (j, 0, i)),
                  anyspec, anyspec, anyspec, ospec, ospec, pl.BlockSpec((G, 1, tq), lambda j, i: (j, 0, i))],
        out_specs=[pl.BlockSpec((G, tq, LANE), lambda j, i: (j, i, 0)), anyspec, anyspec],
        out_shape=[SDS((AT_HEADS, L, LANE), F32), SDS((AT_KV, L, LANE), F32), SDS((AT_KV, L, LANE), F32)],
        scratch_shapes=[pltpu.VMEM((L, LANE), BF16), pltpu.VMEM((nk, LANE, tk), BF16), pltpu.VMEM((L, LANE), BF16),
                        pltpu.VMEM((L, LANE), F32), pltpu.VMEM((L, LANE), F32), pltpu.VMEM((LANE, R), F32),
                        pltpu.SemaphoreType.DMA((3,))],
        compiler_params=_params(("arbitrary", "arbitrary"), VMEM_LARGE), name=name)(qm, qt, kr, krt, vb, do, of, lse)


def _merge_fwd(ya, o8, wua, wubp, z, *, name):
    L = ya.shape[0]
    D = wua.shape[1]
    tm, tn = _tile(L, 512), 256
    ga, gb = (Z_HG + Z_AT) // tn, (Z_HG + Z_AT + D) // tn

    def body(ya_ref, o8_ref, wa_ref, wb_ref, za_ref, zb_ref, mix_ref):
        pa = _nn(ya_ref[...], wa_ref[...])
        pb = _nn(o8_ref[...], wb_ref[...])
        mix_ref[...] = (_sigmoid(za_ref[...]) * pa + _sigmoid(zb_ref[...]) * pb).astype(BF16)

    return pl.pallas_call(
        body, grid=(D // tn, L // tm),
        in_specs=[pl.BlockSpec((tm, ya.shape[1]), lambda j, i: (i, 0)), pl.BlockSpec((tm, o8.shape[1]), lambda j, i: (i, 0)),
                  pl.BlockSpec((wua.shape[0], tn), lambda j, i: (0, j)), pl.BlockSpec((wubp.shape[0], tn), lambda j, i: (0, j)),
                  pl.BlockSpec((tm, tn), lambda j, i: (i, ga + j)), pl.BlockSpec((tm, tn), lambda j, i: (i, gb + j))],
        out_specs=pl.BlockSpec((tm, tn), lambda j, i: (i, j)), out_shape=SDS((L, D), BF16),
        compiler_params=_params(("parallel", "parallel")), name=name)(ya, o8, wua, wubp, z, z)


def _merge_bwd(dh, wout, ya, o8, wua, wubp, z, *, name):
    L = ya.shape[0]
    D = wua.shape[1]
    tm, tn = _tile(L, 512), 256
    ga, gb = (Z_HG + Z_AT) // tn, (Z_HG + Z_AT + D) // tn

    def body(dh_ref, wo_ref, ya_ref, o8_ref, wa_ref, wb_ref, za_ref, zb_ref, dpa_ref, dpb_ref, dza_ref, dzb_ref):
        dm = _nt(dh_ref[...].astype(BF16), wo_ref[...])
        pa = _nn(ya_ref[...], wa_ref[...])
        pb = _nn(o8_ref[...], wb_ref[...])
        sa, sb = _sigmoid(za_ref[...]), _sigmoid(zb_ref[...])
        dpa_ref[...] = (dm * sa).astype(BF16)
        dpb_ref[...] = (dm * sb).astype(BF16)
        dza_ref[...] = (dm * pa * sa * (1.0 - sa)).astype(BF16)
        dzb_ref[...] = (dm * pb * sb * (1.0 - sb)).astype(BF16)

    ospec = pl.BlockSpec((tm, tn), lambda j, i: (i, j))
    return pl.pallas_call(
        body, grid=(D // tn, L // tm),
        in_specs=[pl.BlockSpec((tm, D), lambda j, i: (i, 0)), pl.BlockSpec((tn, D), lambda j, i: (j, 0)),
                  pl.BlockSpec((tm, ya.shape[1]), lambda j, i: (i, 0)), pl.BlockSpec((tm, o8.shape[1]), lambda j, i: (i, 0)),
                  pl.BlockSpec((wua.shape[0], tn), lambda j, i: (0, j)), pl.BlockSpec((wubp.shape[0], tn), lambda j, i: (0, j)),
                  pl.BlockSpec((tm, tn), lambda j, i: (i, ga + j)), pl.BlockSpec((tm, tn), lambda j, i: (i, gb + j))],
        out_specs=[ospec] * 4, out_shape=[SDS((L, D), BF16)] * 4,
        compiler_params=_params(("parallel", "parallel")), name=name)(dh, wout, ya, o8, wua, wubp, z, z)


def _loss_head(h, tgt, *, name):
    L, D = h.shape
    tm = PAD

    def body(h_ref, t_ref, dh_ref, ls_ref):
        i = pl.program_id(0)

        @pl.when(i == 0)
        def _():
            ls_ref[...] = jnp.zeros_like(ls_ref)
            dh_ref[...] = jnp.zeros_like(dh_ref)

        @pl.when(i > 0)
        def _():
            e = h_ref[...] - t_ref[...]
            dh_ref[...] = e * (1.0 / D)
            s = jnp.sum(e * e, axis=0, keepdims=True)
            tot = s[:, :LANE]
            for c in range(1, D // LANE):
                tot = tot + s[:, LANE * c:LANE * (c + 1)]
            ls_ref[...] += tot

    return pl.pallas_call(
        body, grid=(L // tm,),
        in_specs=[pl.BlockSpec((tm, D), lambda i: (i, 0)), pl.BlockSpec((tm, D), lambda i: (jnp.maximum(i - 1, 0), 0))],
        out_specs=[pl.BlockSpec((tm, D), lambda i: (i, 0)), pl.BlockSpec((1, LANE), lambda i: (0, 0))],
        out_shape=[SDS((L, D), F32), SDS((1, LANE), F32)],
        compiler_params=_params(("arbitrary",)), name=name)(h, tgt)


def _adamw(w, g, m, v, *, name):
    shape = w.shape
    w2, g2, m2, v2 = [a.reshape(-1, shape[-1]) for a in (w, g, m, v)]
    rows, cols = w2.shape
    tr = _tile(rows, 256, 8)

    def body(w_ref, g_ref, m_ref, v_ref, d_ref, nm_ref, nv_ref):
        g = g_ref[...]
        m = ADAM_B1 * m_ref[...] + (1.0 - ADAM_B1) * g
        v = ADAM_B2 * v_ref[...] + (1.0 - ADAM_B2) * (g * g)
        m_hat = m / (1.0 - ADAM_B1 ** ADAM_STEP)
        v_hat = v / (1.0 - ADAM_B2 ** ADAM_STEP)
        d_ref[...] = -ADAM_LR * (m_hat / (jnp.sqrt(v_hat) + ADAM_EPS) + ADAM_WD * w_ref[...])
        nm_ref[...] = m
        nv_ref[...] = v

    spec = pl.BlockSpec((tr, cols), lambda i: (i, 0))
    outs = pl.pallas_call(
        body, grid=(rows // tr,), in_specs=[spec] * 4, out_specs=[spec] * 3, out_shape=[SDS((rows, cols), F32)] * 3,
        compiler_params=_params(("parallel",)), name=name)(w2, g2, m2, v2)
    return [o.reshape(shape) for o in outs]


def _sum_slabs(x, *, name):
    n, R, _ = x.shape
    tr = _tile(R, 2048, 8)

    def body(x_ref, o_ref):
        tot = x_ref[0]
        for s in range(1, n):
            tot = tot + x_ref[s]
        o_ref[...] = tot

    return pl.pallas_call(
        body, grid=(R // tr,), in_specs=[pl.BlockSpec((n, tr, LANE), lambda i: (0, i, 0))],
        out_specs=pl.BlockSpec((tr, LANE), lambda i: (i, 0)), out_shape=SDS((R, LANE), F32),
        compiler_params=_params(("parallel",)), name=name)(x)


def _add_pair(a, b, *, name):
    n, R, _ = a.shape
    tr = _tile(R, 2048, 8)

    def body(a_ref, b_ref, o_ref):
        o_ref[...] = a_ref[...] + b_ref[...]

    spec = pl.BlockSpec((1, tr, LANE), lambda s, i: (s, i, 0))
    return pl.pallas_call(
        body, grid=(n, R // tr), in_specs=[spec, spec], out_specs=spec, out_shape=SDS(a.shape, F32),
        compiler_params=_params(("parallel", "parallel")), name=name)(a, b)


def _place():
    return lax.axis_index("x"), lax.axis_index("y"), lax.axis_index("c")


def _allgather_small(v, *, name):
    m_per, n = v.shape

    def body(x_ref, out_ref, send_sems, recv_sems, local_sem):
        x, y, c = _place()
        me, sibling = (x, y, c), (x, y, 1 - c)
        chips = [(1 - x, y), (x, 1 - y), (1 - x, 1 - y)]

        def rows(px, py, pc):
            return out_ref.at[pl.ds((4 * px + 2 * py + pc) * m_per, m_per), :]

        def copy(k, block, to, src=None):
            return pltpu.make_async_remote_copy(
                src_ref=rows(*block) if src is None else src, dst_ref=rows(*block),
                send_sem=send_sems.at[k], recv_sem=recv_sems.at[k], device_id=to, device_id_type=MESH)

        mine = pltpu.make_async_copy(x_ref, rows(*me), local_sem)
        mine.start()
        first = [copy(0, me, sibling, src=x_ref)]
        first += [copy(1 + j, me, (*chip, c), src=x_ref) for j, chip in enumerate(chips)]
        for cp in first:
            cp.start()
        passed = [copy(4 + j, (*chip, c), sibling) for j, chip in enumerate(chips)]
        for j, chip in enumerate(chips):
            copy(1 + j, (*chip, c), me).wait_recv()
            passed[j].start()
        copy(0, sibling, me).wait_recv()
        for j, chip in enumerate(chips):
            copy(4 + j, (*chip, 1 - c), me).wait_recv()
        for cp in first + passed:
            cp.wait_send()
        mine.wait()

    return pl.pallas_call(
        body, out_shape=SDS((8 * m_per, n), v.dtype),
        in_specs=[pl.BlockSpec(memory_space=pltpu.VMEM)], out_specs=pl.BlockSpec(memory_space=pltpu.VMEM),
        scratch_shapes=[pltpu.SemaphoreType.DMA((7,)), pltpu.SemaphoreType.DMA((7,)), pltpu.SemaphoreType.DMA],
        name=name)(v)


def _gather_weights(wp, *, name):
    _, R, _ = wp.shape

    def body(w_ref, out_ref, send_sems, recv_sems, local_sem):
        x, y, c = _place()
        sibling = (x, y, 1 - c)
        chips = [(1 - x, y), (x, 1 - y), (1 - x, 1 - y)]

        def slot(px, py, half):
            return out_ref.at[2 * px + py, half]

        def copy(k, src, dst, to):
            return pltpu.make_async_remote_copy(src_ref=src, dst_ref=dst, send_sem=send_sems.at[k],
                                                recv_sem=recv_sems.at[k], device_id=to, device_id_type=MESH)

        mine = pltpu.make_async_copy(w_ref, out_ref.at[2 * x + y], local_sem)
        mine.start()
        first = [copy(j, w_ref.at[c], slot(x, y, c), (*chip, c)) for j, chip in enumerate(chips)]
        for cp in first:
            cp.start()
        passed = [copy(3 + j, slot(*chip, c), slot(*chip, c), sibling) for j, chip in enumerate(chips)]
        for j, chip in enumerate(chips):
            copy(j, w_ref.at[c], slot(*chip, c), (*chip, c)).wait_recv()
            passed[j].start()
        for j, chip in enumerate(chips):
            copy(3 + j, slot(*chip, 1 - c), slot(*chip, 1 - c), sibling).wait_recv()
        for cp in first + passed:
            cp.wait_send()
        mine.wait()

    anyspec = pl.BlockSpec(memory_space=pl.ANY)
    return pl.pallas_call(
        body, out_shape=SDS((4, 2, R, LANE), wp.dtype), in_specs=[anyspec], out_specs=anyspec,
        scratch_shapes=[pltpu.SemaphoreType.DMA((6,)), pltpu.SemaphoreType.DMA((6,)), pltpu.SemaphoreType.DMA],
        name=name)(wp)


def _pair_exchange(g, *, name):
    _, n, R, _ = g.shape

    def body(g_ref, out_ref, send_sem, recv_sem):
        x, y, c = _place()
        cp = pltpu.make_async_remote_copy(src_ref=g_ref.at[1 - c], dst_ref=out_ref, send_sem=send_sem,
                                          recv_sem=recv_sem, device_id=(x, y, 1 - c), device_id_type=MESH)
        cp.start()
        cp.wait()

    anyspec = pl.BlockSpec(memory_space=pl.ANY)
    return pl.pallas_call(
        body, out_shape=SDS((n, R, LANE), g.dtype), in_specs=[anyspec], out_specs=anyspec,
        scratch_shapes=[pltpu.SemaphoreType.DMA, pltpu.SemaphoreType.DMA], name=name)(g)


def _chip_exchange(part, *, name):
    n, R, _ = part.shape

    def body(p_ref, out_ref, send_sems, recv_sems, local_sem):
        x, y, c = _place()
        s_me = 2 * x + y
        chips = [(1 - x, y), (x, 1 - y), (1 - x, 1 - y)]

        def copy(k, chip):
            return pltpu.make_async_remote_copy(
                src_ref=p_ref.at[2 * chip[0] + chip[1]], dst_ref=out_ref.at[s_me], send_sem=send_sems.at[k],
                recv_sem=recv_sems.at[k], device_id=(*chip, c), device_id_type=MESH)

        def landed(k, chip):
            return pltpu.make_async_remote_copy(
                src_ref=p_ref.at[s_me], dst_ref=out_ref.at[2 * chip[0] + chip[1]], send_sem=send_sems.at[k],
                recv_sem=recv_sems.at[k], device_id=(*chip, c), device_id_type=MESH)

        mine = pltpu.make_async_copy(p_ref.at[s_me], out_ref.at[s_me], local_sem)
        mine.start()
        sends = [copy(k, chip) for k, chip in enumerate(chips)]
        for cp in sends:
            cp.start()
        for k, chip in enumerate(chips):
            landed(k, chip).wait_recv()
        for cp in sends:
            cp.wait_send()
        mine.wait()

    anyspec = pl.BlockSpec(memory_space=pl.ANY)
    return pl.pallas_call(
        body, out_shape=SDS((n, R, LANE), part.dtype), in_specs=[anyspec], out_specs=anyspec,
        scratch_shapes=[pltpu.SemaphoreType.DMA((3,)), pltpu.SemaphoreType.DMA((3,)), pltpu.SemaphoreType.DMA],
        name=name)(part)


def _pair_share(red, *, name):
    R, _ = red.shape

    def body(r_ref, out_ref, send_sem, recv_sem, local_sem):
        x, y, c = _place()
        mine = pltpu.make_async_copy(r_ref, out_ref.at[c], local_sem)
        mine.start()
        cp = pltpu.make_async_remote_copy(src_ref=r_ref, dst_ref=out_ref.at[c], send_sem=send_sem,
                                          recv_sem=recv_sem, device_id=(x, y, 1 - c), device_id_type=MESH)
        cp.start()
        pltpu.make_async_remote_copy(src_ref=r_ref, dst_ref=out_ref.at[1 - c], send_sem=send_sem,
                                     recv_sem=recv_sem, device_id=(x, y, 1 - c), device_id_type=MESH).wait_recv()
        cp.wait_send()
        mine.wait()

    anyspec = pl.BlockSpec(memory_space=pl.ANY)
    return pl.pallas_call(
        body, out_shape=SDS((2, R, LANE), red.dtype), in_specs=[anyspec], out_specs=anyspec,
        scratch_shapes=[pltpu.SemaphoreType.DMA, pltpu.SemaphoreType.DMA, pltpu.SemaphoreType.DMA], name=name)(red)


def _chips(x, y):
    return [(1 - x, y), (x, 1 - y), (1 - x, 1 - y)]


def _gather_mats(shards, *, name):
    n = len(shards)

    def body(*refs):
        ins, outs = refs[:n], refs[n:2 * n]
        send_sems, recv_sems, fsend_sems, frecv_sems = refs[2 * n:]
        x, y, c = _place()
        s_me, sibling, chips = 2 * x + y, (x, y, 1 - c), _chips(x, y)

        def copy(src, dst, ssem, rsem, to):
            return pltpu.make_async_remote_copy(src_ref=src, dst_ref=dst, send_sem=ssem, recv_sem=rsem,
                                                device_id=to, device_id_type=MESH)

        first = [copy(ins[t].at[c], outs[t].at[s_me, c], send_sems.at[3 * t + k], recv_sems.at[3 * t + k], (*chip, c))
                 for t in range(n) for k, chip in enumerate(chips)]
        for cp in first:
            cp.start()
        passed = []
        for t in range(n):
            for k, chip in enumerate(chips):
                slot = outs[t].at[2 * chip[0] + chip[1], c]
                copy(ins[t].at[c], slot, send_sems.at[3 * t + k], recv_sems.at[3 * t + k], (*chip, c)).wait_recv()
                fw = copy(slot, slot, fsend_sems.at[3 * t + k], frecv_sems.at[3 * t + k], sibling)
                fw.start()
                passed.append(fw)
        for t in range(n):
            for k, chip in enumerate(chips):
                slot = outs[t].at[2 * chip[0] + chip[1], 1 - c]
                copy(slot, slot, fsend_sems.at[3 * t + k], frecv_sems.at[3 * t + k], sibling).wait_recv()
        for cp in first + passed:
            cp.wait_send()

    anyspec = pl.BlockSpec(memory_space=pl.ANY)
    return pl.pallas_call(
        body, out_shape=[SDS((4,) + s.shape, s.dtype) for s in shards], in_specs=[anyspec] * n, out_specs=[anyspec] * n,
        scratch_shapes=[pltpu.SemaphoreType.DMA((3 * n,))] * 4, name=name)(*shards)


def _rs_pair_exchange(gs, *, name):
    n = len(gs)

    def body(*refs):
        ins, outs = refs[:n], refs[n:2 * n]
        send_sems, recv_sems = refs[2 * n:]
        x, y, c = _place()
        cps = [pltpu.make_async_remote_copy(src_ref=ins[t].at[k, 1 - c], dst_ref=outs[t].at[k],
                                            send_sem=send_sems.at[4 * t + k], recv_sem=recv_sems.at[4 * t + k],
                                            device_id=(x, y, 1 - c), device_id_type=MESH)
               for t in range(n) for k in range(4)]
        for cp in cps:
            cp.start()
        for cp in cps:
            cp.wait()

    anyspec = pl.BlockSpec(memory_space=pl.ANY)
    return pl.pallas_call(
        body, out_shape=[SDS((4,) + g.shape[2:], g.dtype) for g in gs], in_specs=[anyspec] * n, out_specs=[anyspec] * n,
        scratch_shapes=[pltpu.SemaphoreType.DMA((4 * n,))] * 2, name=name)(*gs)


def _rs_chip_exchange(parts, *, name):
    n = len(parts)

    def body(*refs):
        ins, outs = refs[:n], refs[n:2 * n]
        send_sems, recv_sems, local_sems = refs[2 * n:]
        x, y, c = _place()
        s_me, chips = 2 * x + y, _chips(x, y)

        def copy(t, k, chip, src_slot, dst_slot):
            return pltpu.make_async_remote_copy(
                src_ref=ins[t].at[src_slot], dst_ref=outs[t].at[dst_slot], send_sem=send_sems.at[3 * t + k],
                recv_sem=recv_sems.at[3 * t + k], device_id=(*chip, c), device_id_type=MESH)

        mine = [pltpu.make_async_copy(ins[t].at[s_me], outs[t].at[s_me], local_sems.at[t]) for t in range(n)]
        for cp in mine:
            cp.start()
        sends = [copy(t, k, chip, 2 * chip[0] + chip[1], s_me) for t in range(n) for k, chip in enumerate(chips)]
        for cp in sends:
            cp.start()
        for t in range(n):
            for k, chip in enumerate(chips):
                copy(t, k, chip, s_me, 2 * chip[0] + chip[1]).wait_recv()
        for cp in sends:
            cp.wait_send()
        for cp in mine:
            cp.wait()

    anyspec = pl.BlockSpec(memory_space=pl.ANY)
    return pl.pallas_call(
        body, out_shape=[SDS(p.shape, p.dtype) for p in parts], in_specs=[anyspec] * n, out_specs=[anyspec] * n,
        scratch_shapes=[pltpu.SemaphoreType.DMA((3 * n,))] * 2 + [pltpu.SemaphoreType.DMA((n,))], name=name)(*parts)


def _rs_pair_share(fulls, *, name):
    n = len(fulls)

    def body(*refs):
        ins, outs = refs[:n], refs[n:2 * n]
        send_sems, recv_sems = refs[2 * n:]
        x, y, c = _place()

        def copy(t, half):
            return pltpu.make_async_remote_copy(src_ref=ins[t].at[c], dst_ref=outs[t].at[half], send_sem=send_sems.at[t],
                                                recv_sem=recv_sems.at[t], device_id=(x, y, 1 - c), device_id_type=MESH)

        sends = [copy(t, c) for t in range(n)]
        for cp in sends:
            cp.start()
        for t in range(n):
            copy(t, 1 - c).wait_recv()
        for cp in sends:
            cp.wait_send()

    anyspec = pl.BlockSpec(memory_space=pl.ANY)
    return pl.pallas_call(
        body, out_shape=[SDS(f.shape, f.dtype) for f in fulls], in_specs=[anyspec] * n, out_specs=[anyspec] * n,
        input_output_aliases={t: t for t in range(n)},
        scratch_shapes=[pltpu.SemaphoreType.DMA((n,))] * 2, name=name)(*fulls)


def _add_half(g, other, c1, *, out_dtype, name):
    _, _, h, cs = g.shape
    tr = _tile(h, 512, 16)

    def body(c_ref, g_ref, o_ref, out_ref):
        out_ref[...] = (g_ref[...] + o_ref[...]).astype(out_dtype)

    spec = pl.BlockSpec((None, tr, cs), lambda k, i, c: (k, i, 0))
    return pl.pallas_call(
        body, out_shape=SDS(other.shape, out_dtype),
        grid_spec=pltpu.PrefetchScalarGridSpec(
            num_scalar_prefetch=1, grid=(4, h // tr),
            in_specs=[pl.BlockSpec((None, None, tr, cs), lambda k, i, c: (k, c[0], i, 0)), spec], out_specs=spec),
        compiler_params=_params(("parallel", "parallel")), name=name)(c1, g, other)


def _sum4(x, c1, *, name):
    n, h, cs = x.shape
    tr = _tile(h, 512, 16)

    def body(c_ref, x_ref, o_ref):
        tot = x_ref[0].astype(F32)
        for s in range(1, n):
            tot = tot + x_ref[s].astype(F32)
        o_ref[...] = tot

    return pl.pallas_call(
        body, out_shape=SDS((2, h, cs), F32),
        grid_spec=pltpu.PrefetchScalarGridSpec(
            num_scalar_prefetch=1, grid=(h // tr,),
            in_specs=[pl.BlockSpec((n, tr, cs), lambda i, c: (0, i, 0))],
            out_specs=pl.BlockSpec((None, tr, cs), lambda i, c: (c[0], i, 0))),
        compiler_params=_params(("parallel",)), name=name)(c1, x)


def _finish_small(gathered, lbf, lbb, *, rows, name):
    r_lbf, r_lbb = rows['lb_f'], rows['lb_b']

    def body(g_ref, lbf_ref, lbb_ref, o_ref, dlf_ref, dlb_ref):
        tot = g_ref[0]
        for s in range(1, 8):
            tot = tot + g_ref[s]
        o_ref[...] = tot
        o_ref[0:1, :] = jnp.broadcast_to(jnp.sum(o_ref[0:1, :], axis=1, keepdims=True), (1, LANE))
        for lb_ref, d_ref, r0 in ((lbf_ref, dlf_ref, r_lbf), (lbb_ref, dlb_ref, r_lbb)):
            for hh in range(HG_HEADS):
                sl = slice(LANE * hh, LANE * (hh + 1))
                l0, l1 = lb_ref[0:1, sl], lb_ref[1:2, sl]
                mx = jnp.maximum(l0, l1)
                e0, e1 = jnp.exp(l0 - mx), jnp.exp(l1 - mx)
                p0 = e0 / (e0 + e1)
                d0 = o_ref[r0 + hh:r0 + hh + 1, :] * p0 * (1.0 - p0)
                d_ref[0:1, sl] = d0
                d_ref[1:2, sl] = -d0

    vm = pl.BlockSpec(memory_space=pltpu.VMEM)
    return pl.pallas_call(
        body, in_specs=[vm, vm, vm], out_specs=[vm, vm, vm],
        out_shape=[SDS(gathered.shape[1:], F32), SDS(lbf.shape, F32), SDS(lbb.shape, F32)], name=name)(gathered, lbf, lbb)


def _pad_len(n):
    q = 2 * 128 * LANE
    return (n + q - 1) // q * q


def _pack_local(shards, dtype):
    flat = jnp.concatenate([s.astype(dtype).reshape(-1) for s in shards])
    n = flat.shape[0]
    flat = jnp.pad(flat, (0, _pad_len(n) - n))
    return flat.reshape(2, -1, LANE)


def _unpack_full(gathered, shapes, names):
    out, off = {}, 0
    for name in names:
        r, cs = shapes[name]
        blk = gathered[:, off:off + r * cs].reshape(4, r, cs)
        off += r * cs
        out[name] = blk.reshape(4 * r, cs) if name in ROW_SHARDED else blk.transpose(1, 0, 2).reshape(r, 4 * cs)
    return out


def _pack_grads(grads, shapes, names):
    cols = []
    for name in names:
        r, cs = shapes[name]
        g = grads[name]
        blk = g.reshape(4, r * cs) if name in ROW_SHARDED else g.reshape(r, 4, cs).transpose(1, 0, 2).reshape(4, r * cs)
        cols.append(blk)
    flat = jnp.concatenate(cols, axis=1)
    n = flat.shape[1]
    flat = jnp.pad(flat, ((0, 0), (0, _pad_len(n) - n)))
    return flat.reshape(4, 2, -1, LANE).transpose(1, 0, 2, 3)


def _unpack_shard(flat, shapes, names):
    out, off = {}, 0
    for name in names:
        r, cs = shapes[name]
        out[name] = flat[off:off + r * cs].reshape(r, cs)
        off += r * cs
    return out


def _local_step(x2, tgt2, meta, W, S):
    T, D = x2.shape
    L = PAD + T
    h0 = jnp.concatenate([jnp.zeros((PAD - N_META, D), F32), meta, x2], axis=0)

    qk0 = Z_HG
    w_in = jnp.concatenate([W['w_in'][:, :qk0], _qk_to_group(W['w_in'][:, qk0:qk0 + AT_W + AT_KVW]),
                            W['w_in'][:, qk0 + AT_W + AT_KVW:]], axis=1)
    cc, ss = _rope_tables(L)
    wq_g, wk_g = _group_vec(S['q_norm']), _group_vec(S['k_norm'])

    def ffn_fwd(h, nw, wg, wu, wd, tag):
        n = _rmsnorm_fwd(h, nw, name=tag + "_norm")
        g, u, a = _ffn4_up(n, wg, wu, name=tag + "_up")
        hn = _ffn4_down(a, wd, h, name=tag + "_down")
        return hn, (n, g, u, a)

    def ffn_bwd(dh, h, nw, wg, wu, wd, saved, tag):
        n, g, u, a = saved
        dg, du = _ffn4_dact(dh, wd, g, u, name=tag + "_dact")
        dn = _ffn4_dn(dg, du, wg, wu, name=tag + "_dn")
        dwg = _ffn4_dw(n, dg, x_is_rows=True, name=tag + "_dwg")
        dwu = _ffn4_dw(n, du, x_is_rows=True, name=tag + "_dwu")
        dwd = _ffn4_dw(dh, a, x_is_rows=False, alpha=0.5, name=tag + "_dwd")
        dhp, dnw = _rmsnorm_bwd(h, nw, dn, dh, name=tag + "_norm_bwd")
        return dhp, dnw, dwg, dwu, dwd

    h1, sv1 = ffn_fwd(h0, S['ffn1_norm'], W['ffn1_w_gate'], W['ffn1_w_up'], W['ffn1_w_down'], "ffn1")
    um = _rmsnorm_fwd(h1, S['mix_norm'], name="mix_norm")
    z = _mm([(um, w_in)], tm=512, tn=1792, tk=D, name="in_proj")
    of, sf = _hg_fwd(z, S['hg_lb_fwd'], rev=False, name="hg_fwd_f")
    ob, sb = _hg_fwd(z, S['hg_lb_bwd'], rev=True, name="hg_fwd_b")
    ya = _hg_post_fwd(of, ob, z, S['hg_out_norm'], name="hg_post")
    qm, qt, kr, krt, vb, vt = _at_prep(z, cc, ss, wq_g, wk_g, name="at_prep")
    yb, yb_f32, lse = _at_fwd(qt, kr, vt, name="at_fwd")
    mixed = _merge_fwd(ya, yb, W['w_up_a'], W['w_up_b'], z, name="merge")
    h2 = _mm([(mixed, W['w_out'])], res=h1, tm=512, tn=D, tk=D, name="out_proj")
    h3, sv2 = ffn_fwd(h2, S['ffn2_norm'], W['ffn2_w_gate'], W['ffn2_w_up'], W['ffn2_w_down'], "ffn2")
    dh3, loss_lanes = _loss_head(h3, tgt2, name="loss_head")

    G = {}
    dh2, dn_ffn2, G['ffn2_w_gate'], G['ffn2_w_up'], G['ffn2_w_down'] = ffn_bwd(
        dh3, h2, S['ffn2_norm'], W['ffn2_w_gate'], W['ffn2_w_up'], W['ffn2_w_down'], sv2, "ffn2")
    dpa, dpb, dzga, dzgb = _merge_bwd(dh2, W['w_out'], ya, yb, W['w_up_a'], W['w_up_b'], z, name="merge_bwd")
    G['w_out'] = _mm([(mixed, dh2)], ta=True, tm=D, tn=D, tk=512, name="d_w_out")
    dya = _mm([(dpa, W['w_up_a'])], tb=True, tm=512, tn=HG_W, tk=D, name="d_ya")
    dyb = _mm([(dpb, W['w_up_b'])], tb=True, tm=512, tn=AT_W, tk=D, name="d_yb")
    G['w_up_a'] = _mm([(ya, dpa)], ta=True, tm=HG_W, tn=D, tk=512, name="d_w_up_a")
    G['w_up_b'] = _mm([(yb, dpb)], ta=True, tm=AT_W, tn=D, tk=512, name="d_w_up_b")
    do_hg, dzg, d_hgn = _hg_post_bwd(dya, of, ob, z, S['hg_out_norm'], name="hg_post_bwd")
    dq_f, dv_f, dzf_f, dlb_f = _hg_bwd(z, S['hg_lb_fwd'], do_hg, sf, None, rev=False, name="hg_bwd_f")
    dzq, dzi, dzf_b, dlb_b = _hg_bwd(z, S['hg_lb_bwd'], do_hg, sb, (dq_f, dv_f), rev=True, name="hg_bwd_b")
    dqm, dk2, dv2 = _at_bwd(qm, qt, kr, krt, vb, dyb, yb_f32, lse, name="at_bwd")
    dz_at, dwq_g, dwk_g = _at_prep_bwd(dqm, dk2, dv2, z, cc, ss, wq_g, wk_g, name="at_prep_bwd")
    dz = jnp.concatenate([dzq, dzi, dzf_f, dzf_b, dzg, dz_at, dzga, dzgb], axis=1)
    dum = _mm([(dz, w_in)], tb=True, tm=512, tn=D, tk=1792, name="d_um")
    dw_in_p = _mm([(um, dz)], ta=True, tm=D, tn=1792, tk=512, name="d_w_in")
    G['w_in'] = jnp.concatenate([dw_in_p[:, :qk0], _qk_from_group(dw_in_p[:, qk0:qk0 + AT_W + AT_KVW]),
                                 dw_in_p[:, qk0 + AT_W + AT_KVW:]], axis=1)
    dh1, dn_mix = _rmsnorm_bwd(h1, S['mix_norm'], dum, dh2, name="mix_norm_bwd")
    dh0, dn_ffn1, G['ffn1_w_gate'], G['ffn1_w_up'], G['ffn1_w_down'] = ffn_bwd(
        dh1, h0, S['ffn1_norm'], W['ffn1_w_gate'], W['ffn1_w_up'], W['ffn1_w_down'], sv1, "ffn1")

    small_rows = [('loss', loss_lanes), ('ffn1_norm', dn_ffn1.reshape(-1, LANE)), ('mix_norm', dn_mix.reshape(-1, LANE)),
                  ('ffn2_norm', dn_ffn2.reshape(-1, LANE)), ('hg_out_norm', d_hgn.reshape(-1, LANE)),
                  ('lb_f', dlb_f.reshape(-1, LANE)), ('lb_b', dlb_b.reshape(-1, LANE)), ('q_norm', dwq_g), ('k_norm', dwk_g)]
    return dh0[PAD:], dh0[PAD - N_META:PAD], G, small_rows


def kernel(x, meta_tokens, ffn1_norm, ffn1_w_gate, ffn1_w_up, ffn1_w_down, mix_norm, w_in, hg_lb_fwd, hg_lb_bwd, hg_out_norm, q_norm, k_norm, w_up_a, w_up_b, w_out, ffn2_norm, ffn2_w_gate, ffn2_w_up, ffn2_w_down, loss_target, m_meta_tokens, m_ffn1_norm, m_ffn1_w_gate, m_ffn1_w_up, m_ffn1_w_down, m_mix_norm, m_w_in, m_hg_lb_fwd, m_hg_lb_bwd, m_hg_out_norm, m_q_norm, m_k_norm, m_w_up_a, m_w_up_b, m_w_out, m_ffn2_norm, m_ffn2_w_gate, m_ffn2_w_up, m_ffn2_w_down, v_meta_tokens, v_ffn1_norm, v_ffn1_w_gate, v_ffn1_w_up, v_ffn1_w_down, v_mix_norm, v_w_in, v_hg_lb_fwd, v_hg_lb_bwd, v_hg_out_norm, v_q_norm, v_k_norm, v_w_up_a, v_w_up_b, v_w_out, v_ffn2_norm, v_ffn2_w_gate, v_ffn2_w_up, v_ffn2_w_down):
    given = dict(locals())
    w = {n: given[n] for n in WEIGHTS}
    mom = {n: given["m_" + n] for n in WEIGHTS}
    var = {n: given["v_" + n] for n in WEIGHTS}
    c = lax.axis_index("c")
    D = x.shape[-1]

    shapes = {n: w[n].shape[-2:] for n in MATS + ('meta_tokens',)}
    halves = [w[n].astype(BF16).reshape(2, shapes[n][0] // 2, shapes[n][1]) for n in MATS]
    gathered = _gather_mats(halves, name="gather_weights")
    s_me = 2 * lax.axis_index("x") + lax.axis_index("y")
    W = {}
    for n, hv, g4 in zip(MATS, halves, gathered):
        r, cs = shapes[n]
        g4 = lax.dynamic_update_index_in_dim(g4, hv, s_me, 0).reshape(4, r, cs)
        if n in FFN_MATS:
            W[n] = g4
        elif n in ROW_SHARDED:
            W[n] = g4.reshape(4 * r, cs)
        else:
            W[n] = g4.transpose(1, 0, 2).reshape(r, 4 * cs)
    meta_rows = w['meta_tokens'].reshape(-1, LANE)
    mg = _allgather_small(meta_rows, name="gather_meta").reshape(4, 2, N_META, -1)[:, 0]
    meta = mg.transpose(1, 0, 2).reshape(N_META, D)
    S = {n: w[n] for n in SMALLS}

    grad_x, dmeta, G, small_rows = _local_step(x[0], loss_target[0], meta, W, S)
    G['meta_tokens'] = dmeta

    names = MATS + ('meta_tokens',)
    views = []
    for n in names:
        r, cs = shapes[n]
        if n in FFN_MATS:
            g4 = G[n]
        elif n in ROW_SHARDED:
            g4 = G[n].reshape(4, r, cs)
        else:
            g4 = G[n].reshape(r, 4, cs).transpose(1, 0, 2)
        views.append(g4.reshape(4, 2, r // 2, cs))
    c1 = c.astype(jnp.int32).reshape(1)
    from_sibling = _rs_pair_exchange(views, name="rs_pair_exchange")
    parts = [_add_half(v, o, c1, out_dtype=F32 if n == 'meta_tokens' else BF16, name="rs_pair_sum_" + n)
             for n, v, o in zip(names, views, from_sibling)]
    slabs = _rs_chip_exchange(parts, name="rs_chip_exchange")
    reds = [_sum4(s, c1, name="rs_chip_sum_" + n) for n, s in zip(names, slabs)]
    both = _rs_pair_share(reds, name="rs_pair_share")
    grads = {n: b.reshape(w[n].shape) for n, b in zip(names, both)}

    rows, off = {}, 0
    for nme, blk in small_rows:
        rows[nme] = off
        off += blk.shape[0]
    block = jnp.concatenate([blk for _, blk in small_rows], axis=0)
    n_rows = (off + 7) // 8 * 8
    block = jnp.pad(block, ((0, n_rows - off), (0, 0)))
    allsmall = _allgather_small(block, name="gather_small").reshape(8, n_rows, LANE)
    tot, d_lbf, d_lbb = _finish_small(allsmall, w['hg_lb_fwd'], w['hg_lb_bwd'], rows=rows, name="finish_small")
    loss = 0.5 * tot[0, 0] / D

    def small(nme, shape):
        r0 = rows[nme]
        return tot[r0:r0 + shape[-1] // LANE].reshape(shape)

    grads['ffn1_norm'] = small('ffn1_norm', w['ffn1_norm'].shape)
    grads['mix_norm'] = small('mix_norm', w['mix_norm'].shape)
    grads['ffn2_norm'] = small('ffn2_norm', w['ffn2_norm'].shape)
    grads['hg_out_norm'] = small('hg_out_norm', w['hg_out_norm'].shape)
    grads['hg_lb_fwd'] = d_lbf
    grads['hg_lb_bwd'] = d_lbb
    grads['q_norm'] = _ungroup_vec(tot[rows['q_norm']])
    grads['k_norm'] = _ungroup_vec(tot[rows['k_norm']])

    delta, new_m, new_v = {}, {}, {}
    for n in WEIGHTS:
        delta[n], new_m[n], new_v[n] = _adamw(w[n], grads[n], mom[n], var[n], name="adamw_" + n)
    return (loss, grad_x[None], *[grads[n] for n in WEIGHTS], *[delta[n] for n in WEIGHTS],
            *[new_m[n] for n in WEIGHTS], *[new_v[n] for n in WEIGHTS])
```

```python
import numpy as np
import jax
import jax.numpy as jnp
from jax import lax
from jax.experimental import pallas as pl
from jax.experimental.pallas import tpu as pltpu

F32 = jnp.float32
BF16 = jnp.bfloat16
SDS = jax.ShapeDtypeStruct
MESH = pl.DeviceIdType.MESH

EPS = 1e-6
N_META = 16
PAD = 512
LANE = 128
CHUNK = 128
HG_HEADS = 4
HG_W = HG_HEADS * 128
AT_HEADS = 8
AT_KV = 2
AT_HD = 64
AT_W = AT_HEADS * AT_HD
AT_KVW = AT_KV * AT_HD
VT_ROWS = AT_HD + 16
FWD_CHUNKS_PER_STEP = 4
GRID_W = 64
ROPE_THETA = 10000.0
Z_HG = 5 * HG_W
Z_AT = AT_W + 2 * AT_KVW
ADAM_LR, ADAM_B1, ADAM_B2, ADAM_EPS, ADAM_WD, ADAM_STEP = 0.001, 0.9, 0.999, 1e-08, 0.01, 10
VMEM_DEFAULT = 48 * 1024 * 1024
VMEM_LARGE = 60 * 1024 * 1024
NEG = -1e30

MATS = ('ffn1_w_gate', 'ffn1_w_up', 'ffn1_w_down', 'w_in', 'w_up_a', 'w_up_b', 'w_out',
        'ffn2_w_gate', 'ffn2_w_up', 'ffn2_w_down')
ROW_SHARDED = ('ffn1_w_down', 'w_out', 'ffn2_w_down')
FFN_MATS = ('ffn1_w_gate', 'ffn1_w_up', 'ffn1_w_down', 'ffn2_w_gate', 'ffn2_w_up', 'ffn2_w_down')
SMALLS = ('ffn1_norm', 'mix_norm', 'hg_lb_fwd', 'hg_lb_bwd', 'hg_out_norm', 'q_norm', 'k_norm', 'ffn2_norm')
WEIGHTS = ('meta_tokens', 'ffn1_norm', 'ffn1_w_gate', 'ffn1_w_up', 'ffn1_w_down', 'mix_norm', 'w_in', 'hg_lb_fwd',
           'hg_lb_bwd', 'hg_out_norm', 'q_norm', 'k_norm', 'w_up_a', 'w_up_b', 'w_out', 'ffn2_norm', 'ffn2_w_gate',
           'ffn2_w_up', 'ffn2_w_down')


def _params(sem=None, vmem=VMEM_DEFAULT):
    return pltpu.CompilerParams(dimension_semantics=sem, vmem_limit_bytes=vmem)


def _tile(n, pref, q=LANE):
    for d in range(min(pref, n), 0, -1):
        if n % d == 0 and d % q == 0:
            return d
    return n


def _sigmoid(x):
    return 1.0 / (1.0 + jnp.exp(-x))


def _dot(a, b, dims):
    return lax.dot_general(a, b, (dims, ((), ())), preferred_element_type=F32)


def _nn(a, b):
    return _dot(a, b, ((1,), (0,)))


def _nt(a, b):
    return _dot(a, b, ((1,), (1,)))


def _tn(a, b):
    return _dot(a, b, ((0,), (0,)))


def _split3(x):
    x1 = x.astype(BF16)
    r = x - x1.astype(F32)
    x2 = r.astype(BF16)
    x3 = (r - x2.astype(F32)).astype(BF16)
    return x1, x2, x3


def _exact_left(m01, x):
    x1, x2, x3 = _split3(x)
    return _nn(m01, x1) + _nn(m01, x2) + _nn(m01, x3)


def _exact_right(x, m01):
    x1, x2, x3 = _split3(x)
    return _nn(x1, m01) + _nn(x2, m01) + _nn(x3, m01)


def _mm(pairs, *, name, ta=False, tb=False, out_dtype=F32, tm=512, tn=1024, tk=1024, alpha=1.0, res=None):
    a0, b0 = pairs[0]
    M = a0.shape[1] if ta else a0.shape[0]
    K = a0.shape[0] if ta else a0.shape[1]
    N = b0.shape[0] if tb else b0.shape[1]
    tm, tn, tk = _tile(M, tm), _tile(N, tn), _tile(K, tk)
    nk = K // tk
    npair = len(pairs)
    dims = ((0 if ta else 1,), (1 if tb else 0,))

    def body(*refs):
        ab = refs[:2 * npair]
        pos = 2 * npair
        res_ref = None
        if res is not None:
            res_ref = refs[pos]
            pos += 1
        o_ref = refs[pos]

        def partial_sum():
            tot = None
            for p in range(npair):
                d = _dot(ab[2 * p][...].astype(BF16), ab[2 * p + 1][...].astype(BF16), dims)
                tot = d if tot is None else tot + d
            return tot

        def finish(acc):
            r = acc if alpha == 1.0 else acc * alpha
            if res_ref is not None:
                r = res_ref[...] + r
            o_ref[...] = r.astype(out_dtype)

        if nk == 1:
            finish(partial_sum())
        else:
            acc_ref = refs[pos + 1]
            k = pl.program_id(2)

            @pl.when(k == 0)
            def _():
                acc_ref[...] = jnp.zeros_like(acc_ref)

            acc_ref[...] += partial_sum()

            @pl.when(k == nk - 1)
            def _():
                finish(acc_ref[...])

    a_spec = pl.BlockSpec((tk, tm), lambda j, i, k: (k, i)) if ta else pl.BlockSpec((tm, tk), lambda j, i, k: (i, k))
    b_spec = pl.BlockSpec((tn, tk), lambda j, i, k: (j, k)) if tb else pl.BlockSpec((tk, tn), lambda j, i, k: (k, j))
    o_spec = pl.BlockSpec((tm, tn), lambda j, i, k: (i, j))
    in_specs, args = [], []
    for a, b in pairs:
        in_specs += [a_spec, b_spec]
        args += [a, b]
    if res is not None:
        in_specs.append(o_spec)
        args.append(res)
    return pl.pallas_call(
        body, grid=(N // tn, M // tm, nk), in_specs=in_specs, out_specs=o_spec,
        out_shape=SDS((M, N), out_dtype),
        scratch_shapes=[pltpu.VMEM((tm, tn), F32)] if nk > 1 else [],
        compiler_params=_params(("parallel", "parallel", "arbitrary")), name=name)(*args)


def _rmsnorm_fwd(h, w, *, name):
    L, D = h.shape
    tm = _tile(L, 512)

    def body(h_ref, w_ref, o_ref):
        x = h_ref[...]
        r = lax.rsqrt(jnp.mean(x * x, axis=-1, keepdims=True) + EPS)
        o_ref[...] = (x * r * w_ref[...]).astype(BF16)

    return pl.pallas_call(
        body, grid=(L // tm,),
        in_specs=[pl.BlockSpec((tm, D), lambda i: (i, 0)), pl.BlockSpec((1, D), lambda i: (0, 0))],
        out_specs=pl.BlockSpec((tm, D), lambda i: (i, 0)), out_shape=SDS((L, D), BF16),
        compiler_params=_params(("parallel",)), name=name)(h, w)


def _rmsnorm_bwd(h, w, dn, dres, *, split=False, name):
    L, D = h.shape
    tm = PAD if split else _tile(L, 512)

    def body(h_ref, w_ref, dn_ref, dres_ref, dh_ref, *rest):
        dw_ref = rest[-1]
        i = pl.program_id(0)
        x = h_ref[...]
        r = lax.rsqrt(jnp.mean(x * x, axis=-1, keepdims=True) + EPS)
        xh = x * r
        dn = dn_ref[...]
        dxh = dn * w_ref[...]
        dh = dres_ref[...] + r * (dxh - xh * jnp.mean(dxh * xh, axis=-1, keepdims=True))
        dh_ref[...] = dh

        @pl.when(i == 0)
        def _():
            dw_ref[...] = jnp.zeros_like(dw_ref)
            if split:
                rest[0][...] = dh[PAD - N_META:]

        dw_ref[...] += jnp.sum(dn * xh, axis=0, keepdims=True)

    row = pl.BlockSpec((tm, D), lambda i: (i, 0))
    vec = pl.BlockSpec((1, D), lambda i: (0, 0))
    if split:
        out_specs = [pl.BlockSpec((tm, D), lambda i: (jnp.maximum(i - 1, 0), 0)), pl.BlockSpec((N_META, D), lambda i: (0, 0)), vec]
        out_shape = [SDS((L - PAD, D), F32), SDS((N_META, D), F32), SDS((1, D), F32)]
    else:
        out_specs, out_shape = [row, vec], [SDS((L, D), F32), SDS((1, D), F32)]
    return pl.pallas_call(
        body, grid=(L // tm,), in_specs=[row, vec, row, row], out_specs=out_specs, out_shape=out_shape,
        compiler_params=_params(("arbitrary",)), name=name)(h, w, dn, dres)


def _ffn4_up(n, wg4, wu4, *, name):
    L, D = n.shape
    ns, _, cs = wg4.shape
    tm = _tile(L, 768)

    def body(n_ref, wg_ref, wu_ref, g_ref, u_ref, a_ref):
        x = n_ref[...]
        g = _nn(x, wg_ref[...])
        u = _nn(x, wu_ref[...])
        g_ref[...] = g.astype(BF16)
        u_ref[...] = u.astype(BF16)
        a_ref[...] = (g * _sigmoid(g) * u).astype(BF16)

    wspec = pl.BlockSpec((None, D, cs), lambda j, i: (j, 0, 0))
    ospec = pl.BlockSpec((None, tm, cs), lambda j, i: (j, i, 0))
    return pl.pallas_call(
        body, grid=(ns, L // tm),
        in_specs=[pl.BlockSpec((tm, D), lambda j, i: (i, 0)), wspec, wspec], out_specs=[ospec, ospec, ospec],
        out_shape=[SDS((ns, L, cs), BF16), SDS((ns, L, cs), BF16), SDS((ns, L, cs), BF16)],
        compiler_params=_params(("parallel", "parallel")), name=name)(n, wg4, wu4)


def _ffn4_down(a4, wd4, h, *, name):
    ns, L, cs = a4.shape
    D = wd4.shape[2]
    tm = _tile(L, 512)

    def body(a_ref, w_ref, h_ref, o_ref):
        acc = _nn(a_ref[0], w_ref[0])
        for j in range(1, ns):
            acc = acc + _nn(a_ref[j], w_ref[j])
        o_ref[...] = h_ref[...] + 0.5 * acc

    row = pl.BlockSpec((tm, D), lambda i: (i, 0))
    return pl.pallas_call(
        body, grid=(L // tm,),
        in_specs=[pl.BlockSpec((ns, tm, cs), lambda i: (0, i, 0)), pl.BlockSpec((ns, cs, D), lambda i: (0, 0, 0)), row],
        out_specs=row, out_shape=SDS((L, D), F32),
        compiler_params=_params(("parallel",)), name=name)(a4, wd4, h)


def _ffn4_dact(dh, wd4, g4, u4, *, name):
    L, D = dh.shape
    ns, cs, _ = wd4.shape
    tm = _tile(L, 768)

    def body(dh_ref, wd_ref, g_ref, u_ref, dg_ref, du_ref):
        da = 0.5 * _nt(dh_ref[...].astype(BF16), wd_ref[...])
        g = g_ref[...].astype(F32)
        sg = _sigmoid(g)
        dg_ref[...] = (da * u_ref[...].astype(F32) * (sg * (1.0 + g * (1.0 - sg)))).astype(BF16)
        du_ref[...] = (da * (g * sg)).astype(BF16)

    ospec = pl.BlockSpec((None, tm, cs), lambda j, i: (j, i, 0))
    return pl.pallas_call(
        body, grid=(ns, L // tm),
        in_specs=[pl.BlockSpec((tm, D), lambda j, i: (i, 0)), pl.BlockSpec((None, cs, D), lambda j, i: (j, 0, 0)), ospec, ospec],
        out_specs=[ospec, ospec], out_shape=[SDS((ns, L, cs), BF16), SDS((ns, L, cs), BF16)],
        compiler_params=_params(("parallel", "parallel")), name=name)(dh, wd4, g4, u4)


def _ffn4_dn(dg4, du4, wg4, wu4, *, name):
    ns, L, cs = dg4.shape
    D = wg4.shape[1]
    tm = _tile(L, 512)

    def body(dg_ref, du_ref, wg_ref, wu_ref, o_ref):
        acc = None
        for j in range(ns):
            t = _nt(dg_ref[j], wg_ref[j]) + _nt(du_ref[j], wu_ref[j])
            acc = t if acc is None else acc + t
        o_ref[...] = acc

    aspec = pl.BlockSpec((ns, tm, cs), lambda i: (0, i, 0))
    wspec = pl.BlockSpec((ns, D, cs), lambda i: (0, 0, 0))
    return pl.pallas_call(
        body, grid=(L // tm,), in_specs=[aspec, aspec, wspec, wspec],
        out_specs=pl.BlockSpec((tm, D), lambda i: (i, 0)), out_shape=SDS((L, D), F32),
        compiler_params=_params(("parallel",), VMEM_LARGE), name=name)(dg4, du4, wg4, wu4)


def _ffn4_dw(x, y4, *, x_is_rows, alpha=1.0, name):
    L, D = x.shape
    ns, _, cs = y4.shape
    tk = _tile(L, 512)
    nk = L // tk
    oshape = (D, cs) if x_is_rows else (cs, D)

    def body(x_ref, y_ref, o_ref):
        k = pl.program_id(0)

        @pl.when(k == 0)
        def _():
            o_ref[...] = jnp.zeros_like(o_ref)

        xb = x_ref[...].astype(BF16)
        if x_is_rows:
            xt = xb.T
            for j in range(ns):
                o_ref[j] += _nn(xt, y_ref[j])
        else:
            for j in range(ns):
                o_ref[j] += _tn(y_ref[j], xb)

        if alpha != 1.0:
            @pl.when(k == nk - 1)
            def _():
                o_ref[...] = o_ref[...] * alpha

    return pl.pallas_call(
        body, grid=(nk,),
        in_specs=[pl.BlockSpec((tk, D), lambda k: (k, 0)), pl.BlockSpec((ns, tk, cs), lambda k: (0, k, 0))],
        out_specs=pl.BlockSpec((ns,) + oshape, lambda k: (0, 0, 0)), out_shape=SDS((ns,) + oshape, F32),
        compiler_params=_params(("arbitrary",)), name=name)(x, y4)


def _hg_masks(rev):
    t = lax.broadcasted_iota(jnp.int32, (CHUNK, CHUNK), 0)
    s = lax.broadcasted_iota(jnp.int32, (CHUNK, CHUNK), 1)
    causal = (s >= t) if rev else (s <= t)
    levels = []
    for sh in (6, 5, 4):
        same = jnp.right_shift(t, sh + 1) == jnp.right_shift(s, sh + 1)
        tq = jnp.bitwise_and(jnp.right_shift(t, sh), 1)
        sk = jnp.bitwise_and(jnp.right_shift(s, sh), 1)
        levels.append(same & (tq == (0 if rev else 1)) & (sk == (1 if rev else 0)))
    diag = (jnp.right_shift(t, 4) == jnp.right_shift(s, 4)) & causal
    return causal, levels, diag


def _hg_intra_factors(q, k, b, b_scr, rev):
    b_scr[...] = b
    row = lax.broadcasted_iota(jnp.int32, (CHUNK, LANE), 0)
    out = []
    for sh in (6, 5, 4):
        lb = 1 << sh
        pieces = []
        for p in range(0, CHUNK, 2 * lb):
            r = p + lb if rev else p + lb - 1
            pieces.append(jnp.broadcast_to(b_scr[pl.ds(r, 1), :], (2 * lb, LANE)))
        ref = pieces[0] if len(pieces) == 1 else jnp.concatenate(pieces, axis=0)
        qside = jnp.bitwise_and(jnp.right_shift(row, sh), 1) == (0 if rev else 1)
        eq = jnp.where(qside, jnp.exp(jnp.minimum(b - ref, 0.0)), 0.0)
        ek = jnp.where(qside, 0.0, jnp.exp(jnp.minimum(ref - b, 0.0)))
        out.append((eq, ek, (q * eq).astype(BF16), (k * ek).astype(BF16)))
    pieces = []
    for a in range(0, CHUNK, 16):
        r = a + (8 if rev else 7)
        pieces.append(jnp.broadcast_to(b_scr[pl.ds(r, 1), :], (16, LANE)))
    ref = jnp.concatenate(pieces, axis=0)
    eq = jnp.exp(jnp.minimum(b - ref, 80.0))
    ek = jnp.exp(jnp.minimum(ref - b, 80.0))
    out.append((eq, ek, (q * eq).astype(BF16), (k * ek).astype(BF16)))
    return out


def _hg_gate(zf, l0, l1, valid):
    mx = jnp.maximum(l0, l1)
    e0, e1 = jnp.exp(l0 - mx), jnp.exp(l1 - mx)
    p0 = e0 / (e0 + e1)
    sg = _sigmoid(-zf)
    k = jnp.where(valid, (1.0 - p0) * sg, 0.0)
    return p0, sg, k, jnp.log(1.0 - k)


def _hg_fwd(z, lbp, *, rev, name):
    L = z.shape[0]
    nc = L // CHUNK
    fcol = 3 if rev else 2

    def cidx(j):
        return nc - 1 - j if rev else j

    def body(zq_ref, zi_ref, zf_ref, lb_ref, o_ref, ssave_ref, st_scr, b_scr):
        j = pl.program_id(0)

        @pl.when(j == 0)
        def _():
            st_scr[...] = jnp.zeros_like(st_scr)

        causal, lmasks, dmask = _hg_masks(rev)
        tri = jnp.where(causal, 1.0, 0.0).astype(BF16)
        rowg = cidx(j) * CHUNK + lax.broadcasted_iota(jnp.int32, (CHUNK, LANE), 0)
        valid = rowg >= PAD - N_META
        last = 0 if rev else CHUNK - 1
        for hh in range(HG_HEADS):
            sl = slice(LANE * hh, LANE * (hh + 1))
            zq = zq_ref[:, sl]
            q = zq * _sigmoid(zq)
            v = zi_ref[:, sl].astype(BF16)
            _, _, k, g = _hg_gate(zf_ref[:, sl], lb_ref[0:1, sl], lb_ref[1:2, sl], valid)
            b = _exact_left(tri, g)
            st = st_scr[hh]
            ssave_ref[0, hh] = st
            o = _nt((q * jnp.exp(b)).astype(BF16), st.astype(BF16))
            a = None
            fac = _hg_intra_factors(q, k, b, b_scr, rev)
            for (eq, ek, qq, kk), msk in zip(fac, lmasks + [dmask]):
                t = jnp.where(msk, _nt(qq, kk), 0.0)
                a = t if a is None else a + t
            o_ref[:, sl] = o + _nn(a.astype(BF16), v)
            bl = b_scr[pl.ds(last, 1), :]
            kd = (k * jnp.exp(bl - b)).astype(BF16)
            st_scr[hh] = st * jnp.exp(bl) + _tn(v, kd)

    zspec = lambda col: pl.BlockSpec((CHUNK, HG_W), lambda j: (cidx(j), col))
    return pl.pallas_call(
        body, grid=(nc,),
        in_specs=[zspec(0), zspec(1), zspec(fcol), pl.BlockSpec((2, HG_W), lambda j: (0, 0))],
        out_specs=[pl.BlockSpec((CHUNK, HG_W), lambda j: (cidx(j), 0)),
                   pl.BlockSpec((1, HG_HEADS, LANE, LANE), lambda j: (cidx(j), 0, 0, 0))],
        out_shape=[SDS((L, HG_W), F32), SDS((nc, HG_HEADS, LANE, LANE), F32)],
        scratch_shapes=[pltpu.VMEM((HG_HEADS, LANE, LANE), F32), pltpu.VMEM((CHUNK, LANE), F32)],
        compiler_params=_params(("arbitrary",)), name=name)(z, z, z, lbp)


def _hg_bwd(z, lbp, do, ssave, prev, *, rev, name):
    L = z.shape[0]
    nc = L // CHUNK
    fcol = 3 if rev else 2
    final = prev is not None

    def cidx(j):
        return j if rev else nc - 1 - j

    def body(*refs):
        zq_ref, zi_ref, zf_ref, lb_ref, do_ref, ss_ref = refs[:6]
        pos = 6
        if final:
            dqin_ref, dvin_ref = refs[6:8]
            pos = 8
        dq_ref, dv_ref, dzf_ref, dlb_ref, dst_scr, b_scr = refs[pos:pos + 6]
        j = pl.program_id(0)

        @pl.when(j == 0)
        def _():
            dst_scr[...] = jnp.zeros_like(dst_scr)
            dlb_ref[...] = jnp.zeros_like(dlb_ref)

        causal, lmasks, dmask = _hg_masks(rev)
        tri = jnp.where(causal, 1.0, 0.0).astype(BF16)
        ti = lax.broadcasted_iota(jnp.int32, (CHUNK, CHUNK), 0)
        si = lax.broadcasted_iota(jnp.int32, (CHUNK, CHUNK), 1)
        tri_t = jnp.where((si <= ti) if rev else (si >= ti), 1.0, 0.0).astype(BF16)
        rowg = cidx(j) * CHUNK + lax.broadcasted_iota(jnp.int32, (CHUNK, LANE), 0)
        valid = rowg >= PAD - N_META
        last = 0 if rev else CHUNK - 1
        for hh in range(HG_HEADS):
            sl = slice(LANE * hh, LANE * (hh + 1))
            zq = zq_ref[:, sl]
            sq = _sigmoid(zq)
            q = zq * sq
            v = zi_ref[:, sl].astype(BF16)
            p0, sg, k, g = _hg_gate(zf_ref[:, sl], lb_ref[0:1, sl], lb_ref[1:2, sl], valid)
            b = _exact_left(tri, g)
            dob = do_ref[:, sl].astype(BF16)
            st = ss_ref[0, hh]
            dst = dst_scr[hh]
            stb, dstb = st.astype(BF16), dst.astype(BF16)
            eb = jnp.exp(b)
            qe = (q * eb).astype(BF16)
            fac = _hg_intra_factors(q, k, b, b_scr, rev)
            bl = b_scr[pl.ds(last, 1), :]
            ebl = jnp.exp(bl)
            kde = jnp.exp(bl - b)
            kd = (k * kde).astype(BF16)
            da = jnp.where(causal, _nt(dob, v), 0.0)
            dq = eb * _nn(dob, stb)
            dk_inter = kde * _nn(v, dstb)
            dk = dk_inter
            dv = _nt(kd, dstb)
            a = None
            db = q * dq - k * dk
            for (eq, ek, qq, kk), msk in zip(fac, lmasks + [dmask]):
                t = jnp.where(msk, _nt(qq, kk), 0.0)
                a = t if a is None else a + t
                dal = jnp.where(msk, da, 0.0).astype(BF16)
                mq = _nn(dal, kk)
                mk = _tn(dal, qq)
                dq = dq + eq * mq
                dk = dk + ek * mk
                db = db + (qq.astype(F32) * mq - kk.astype(F32) * mk)
            dv = dv + _tn(a.astype(BF16), dob)
            extra = ebl * jnp.sum(st * dst, axis=0, keepdims=True) + jnp.sum(k * dk_inter, axis=0, keepdims=True)
            dst_scr[hh] = dst * ebl + _tn(dob, qe)
            dg = _exact_left(tri_t, db) + extra
            dk_tot = dk - dg / (1.0 - k)
            dzf_ref[:, sl] = jnp.where(valid, dk_tot * (1.0 - p0) * (-sg * (1.0 - sg)), 0.0).astype(BF16)
            dlb_ref[:, sl] += jnp.sum(jnp.where(valid, -sg * dk_tot, 0.0), axis=0, keepdims=True)
            if final:
                dq_ref[:, sl] = ((dq + dqin_ref[:, sl]) * (sq * (1.0 + zq * (1.0 - sq)))).astype(BF16)
                dv_ref[:, sl] = (dv + dvin_ref[:, sl]).astype(BF16)
            else:
                dq_ref[:, sl] = dq
                dv_ref[:, sl] = dv

    zspec = lambda col: pl.BlockSpec((CHUNK, HG_W), lambda j: (cidx(j), col))
    rspec = pl.BlockSpec((CHUNK, HG_W), lambda j: (cidx(j), 0))
    in_specs = [zspec(0), zspec(1), zspec(fcol), pl.BlockSpec((2, HG_W), lambda j: (0, 0)), rspec,
                pl.BlockSpec((1, HG_HEADS, LANE, LANE), lambda j: (cidx(j), 0, 0, 0))]
    args = [z, z, z, lbp, do, ssave]
    if final:
        in_specs += [rspec, rspec]
        args += list(prev)
    odt = BF16 if final else F32
    return pl.pallas_call(
        body, grid=(nc,), in_specs=in_specs,
        out_specs=[rspec, rspec, rspec, pl.BlockSpec((1, HG_W), lambda j: (0, 0))],
        out_shape=[SDS((L, HG_W), odt), SDS((L, HG_W), odt), SDS((L, HG_W), BF16), SDS((1, HG_W), F32)],
        scratch_shapes=[pltpu.VMEM((HG_HEADS, LANE, LANE), F32), pltpu.VMEM((CHUNK, LANE), F32)],
        compiler_params=_params(("arbitrary",)), name=name)(*args)


def _hg_post_fwd(of, ob, z, w, *, name):
    L = of.shape[0]
    tm = _tile(L, 512)

    def body(of_ref, ob_ref, zg_ref, w_ref, y_ref):
        for hh in range(HG_HEADS):
            sl = slice(LANE * hh, LANE * (hh + 1))
            o = of_ref[:, sl] + ob_ref[:, sl]
            r = lax.rsqrt(jnp.mean(o * o, axis=-1, keepdims=True) + EPS)
            zg = zg_ref[:, sl]
            y_ref[:, sl] = (o * r * w_ref[:, sl] * (zg * _sigmoid(zg))).astype(BF16)

    row = pl.BlockSpec((tm, HG_W), lambda i: (i, 0))
    return pl.pallas_call(
        body, grid=(L // tm,),
        in_specs=[row, row, pl.BlockSpec((tm, HG_W), lambda i: (i, 4)), pl.BlockSpec((1, HG_W), lambda i: (0, 0))],
        out_specs=row, out_shape=SDS((L, HG_W), BF16),
        compiler_params=_params(("parallel",)), name=name)(of, ob, z, w)


def _hg_post_bwd(dy, of, ob, z, w, *, name):
    L = of.shape[0]
    tm = _tile(L, 512)

    def body(dy_ref, of_ref, ob_ref, zg_ref, w_ref, do_ref, dzg_ref, dw_ref):
        @pl.when(pl.program_id(0) == 0)
        def _():
            dw_ref[...] = jnp.zeros_like(dw_ref)

        for hh in range(HG_HEADS):
            sl = slice(LANE * hh, LANE * (hh + 1))
            o = of_ref[:, sl] + ob_ref[:, sl]
            r = lax.rsqrt(jnp.mean(o * o, axis=-1, keepdims=True) + EPS)
            xh = o * r
            zg = zg_ref[:, sl]
            sg = _sigmoid(zg)
            w = w_ref[:, sl]
            dy = dy_ref[:, sl]
            dys = dy * (zg * sg)
            dzg_ref[:, sl] = (dy * xh * w * (sg * (1.0 + zg * (1.0 - sg)))).astype(BF16)
            dw_ref[:, sl] += jnp.sum(dys * xh, axis=0, keepdims=True)
            dxh = dys * w
            do_ref[:, sl] = r * (dxh - xh * jnp.mean(dxh * xh, axis=-1, keepdims=True))

    row = pl.BlockSpec((tm, HG_W), lambda i: (i, 0))
    vec = pl.BlockSpec((1, HG_W), lambda i: (0, 0))
    return pl.pallas_call(
        body, grid=(L // tm,),
        in_specs=[row, row, row, pl.BlockSpec((tm, HG_W), lambda i: (i, 4)), vec],
        out_specs=[row, row, vec],
        out_shape=[SDS((L, HG_W), F32), SDS((L, HG_W), BF16), SDS((1, HG_W), F32)],
        compiler_params=_params(("arbitrary",)), name=name)(dy, of, ob, z, w)


N_GROUPS = (AT_HEADS + AT_KV) // 2


def _qk_to_group(wqk):
    d = wqk.shape[0]
    return wqk.reshape(d, N_GROUPS, 2, AT_HD // 2, 2).transpose(0, 1, 4, 2, 3).reshape(d, N_GROUPS * LANE)


def _qk_from_group(wqk):
    d = wqk.shape[0]
    return wqk.reshape(d, N_GROUPS, 2, 2, AT_HD // 2).transpose(0, 1, 3, 4, 2).reshape(d, N_GROUPS * LANE)


def _group_vec(w64):
    halves = w64.reshape(AT_HD // 2, 2).T
    return jnp.broadcast_to(halves[:, None, :], (2, 2, AT_HD // 2)).reshape(1, LANE)


def _ungroup_vec(w128):
    w = w128.reshape(2, 2, 32).sum(axis=1)
    return w.T.reshape(1, AT_HD)


def _rope_tables(L):
    n_real = L - PAD
    t = np.arange(n_real)
    row = np.concatenate([np.zeros(PAD), t // GRID_W]).astype(np.float32)
    col = np.concatenate([np.zeros(PAD), t % GRID_W]).astype(np.float32)
    inv = jnp.asarray(ROPE_THETA, F32) ** (-jnp.arange(0, AT_HD // 2, 2, dtype=F32) / (AT_HD // 2))
    ang = jnp.concatenate([jnp.asarray(row)[:, None] * inv, jnp.asarray(col)[:, None] * inv], axis=-1)
    cos, sin = jnp.cos(ang), jnp.sin(ang)
    cc = jnp.tile(cos, (1, 4))
    ss = jnp.concatenate([-sin, -sin, sin, sin], axis=1)
    return cc, ss


def _seg_matrix():
    a = lax.broadcasted_iota(jnp.int32, (LANE, LANE), 0)
    b = lax.broadcasted_iota(jnp.int32, (LANE, LANE), 1)
    same = jnp.bitwise_and(jnp.right_shift(a, 5), 1) == jnp.bitwise_and(jnp.right_shift(b, 5), 1)
    return jnp.where(same, 1.0, 0.0).astype(BF16)


def _slot_mask(shape, hp):
    lane = lax.broadcasted_iota(jnp.int32, shape, 1)
    return jnp.bitwise_and(jnp.right_shift(lane, 5), 1) == hp


def _at_prep(z, cc, ss, wq, wk, *, name):
    L = z.shape[0]
    tm = PAD
    qcol = Z_HG // AT_W
    kvcol = (Z_HG + AT_W) // (2 * LANE)

    def body(zq_ref, zkv_ref, cc_ref, ss_ref, wq_ref, wk_ref, qm_ref, qt_ref, kr_ref, krt_ref, vb_ref, vt_ref):
        seg = _seg_matrix()
        cc, ss = cc_ref[...], ss_ref[...]

        def normrope(x, w):
            r = lax.rsqrt(_exact_right(x * x, seg) * (1.0 / AT_HD) + EPS)
            y = x * r * w
            return y * cc + pltpu.roll(y, 64, 1) * ss

        for g in range(AT_HEADS // 2):
            o = normrope(zq_ref[:, LANE * g:LANE * (g + 1)], wq_ref[...]) * (AT_HD ** -0.5)
            for hp in range(2):
                h = 2 * g + hp
                tgt = h // (AT_HEADS // AT_KV)
                xm = jnp.where(_slot_mask(o.shape, hp), o, 0.0)
                if tgt != hp:
                    xm = pltpu.roll(xm, 32 if tgt == 1 else 96, 1)
                qm_ref[h] = xm.astype(BF16)
                qt_ref[h] = xm.T.astype(BF16)
        kr = normrope(zkv_ref[:, :LANE], wk_ref[...])
        kr_ref[...] = kr.astype(BF16)
        krt_ref[0] = kr.T.astype(BF16)
        v = zkv_ref[:, LANE:]
        low = lax.broadcasted_iota(jnp.int32, v.shape, 1) < AT_HD
        vb_ref[0] = jnp.where(low, v, 0.0).astype(BF16)
        vb_ref[1] = jnp.where(low, pltpu.roll(v, AT_HD, 1), 0.0).astype(BF16)
        vt = v.T.astype(BF16)
        ones = jnp.ones((VT_ROWS - AT_HD, tm), BF16)
        for j in range(AT_KV):
            vt_ref[j, 0, 0:AT_HD, :] = vt[AT_HD * j:AT_HD * (j + 1)]
            vt_ref[j, 0, AT_HD:VT_ROWS, :] = ones

    tab = pl.BlockSpec((tm, LANE), lambda i: (i, 0))
    vec = pl.BlockSpec((1, LANE), lambda i: (0, 0))
    nt = L // tm
    return pl.pallas_call(
        body, grid=(nt,),
        in_specs=[pl.BlockSpec((tm, AT_W), lambda i: (i, qcol)), pl.BlockSpec((tm, 2 * LANE), lambda i: (i, kvcol)),
                  tab, tab, vec, vec],
        out_specs=[pl.BlockSpec((AT_HEADS, tm, LANE), lambda i: (0, i, 0)),
                   pl.BlockSpec((AT_HEADS, LANE, tm), lambda i: (0, 0, i)), tab,
                   pl.BlockSpec((1, LANE, tm), lambda i: (i, 0, 0)),
                   pl.BlockSpec((AT_KV, tm, LANE), lambda i: (0, i, 0)),
                   pl.BlockSpec((AT_KV, 1, VT_ROWS, tm), lambda i: (0, i, 0, 0))],
        out_shape=[SDS((AT_HEADS, L, LANE), BF16), SDS((AT_HEADS, LANE, L), BF16), SDS((L, LANE), BF16),
                   SDS((nt, LANE, tm), BF16), SDS((AT_KV, L, LANE), BF16), SDS((AT_KV, nt, VT_ROWS, tm), BF16)],
        compiler_params=_params(("parallel",)), name=name)(z, z, cc, ss, wq, wk)


def _at_prep_bwd(dqm, dk2, dv2, z, cc, ss, wq, wk, *, name):
    L = z.shape[0]
    tm = PAD
    qcol = Z_HG // AT_W
    kvcol = (Z_HG + AT_W) // (2 * LANE)

    def body(dqm_ref, dk2_ref, dv2_ref, zq_ref, zkv_ref, cc_ref, ss_ref, wq_ref, wk_ref, dz_ref, dwq_ref, dwk_ref):
        @pl.when(pl.program_id(0) == 0)
        def _():
            dwq_ref[...] = jnp.zeros_like(dwq_ref)
            dwk_ref[...] = jnp.zeros_like(dwk_ref)

        seg = _seg_matrix()
        cc, ss = cc_ref[...], ss_ref[...]

        def back(x, w, do):
            dy = do * cc + pltpu.roll(do * ss, 64, 1)
            r = lax.rsqrt(_exact_right(x * x, seg) * (1.0 / AT_HD) + EPS)
            xh = x * r
            dxh = dy * w
            dx = r * (dxh - xh * (_exact_right(dxh * xh, seg) * (1.0 / AT_HD)))
            return dx, jnp.sum(dy * xh, axis=0, keepdims=True)

        for g in range(AT_HEADS // 2):
            do = None
            for hp in range(2):
                h = 2 * g + hp
                tgt = h // (AT_HEADS // AT_KV)
                d = jnp.where(_slot_mask((tm, LANE), tgt), dqm_ref[h], 0.0)
                if tgt != hp:
                    d = pltpu.roll(d, 96 if tgt == 1 else 32, 1)
                do = d if do is None else do + d
            dx, dw = back(zq_ref[:, LANE * g:LANE * (g + 1)], wq_ref[...], do * (AT_HD ** -0.5))
            dz_ref[:, LANE * g:LANE * (g + 1)] = dx.astype(BF16)
            dwq_ref[...] += dw
        dx, dw = back(zkv_ref[:, :LANE], wk_ref[...], dk2_ref[0] + dk2_ref[1])
        dz_ref[:, AT_W:AT_W + LANE] = dx.astype(BF16)
        dwk_ref[...] += dw
        dv0 = dv2_ref[0]
        low = lax.broadcasted_iota(jnp.int32, dv0.shape, 1) < AT_HD
        dz_ref[:, AT_W + LANE:] = jnp.where(low, dv0, pltpu.roll(dv2_ref[1], AT_HD, 1)).astype(BF16)

    tab = pl.BlockSpec((tm, LANE), lambda i: (i, 0))
    vec = pl.BlockSpec((1, LANE), lambda i: (0, 0))
    two = pl.BlockSpec((AT_KV, tm, LANE), lambda i: (0, i, 0))
    return pl.pallas_call(
        body, grid=(L // tm,),
        in_specs=[pl.BlockSpec((AT_HEADS, tm, LANE), lambda i: (0, i, 0)), two, two,
                  pl.BlockSpec((tm, AT_W), lambda i: (i, qcol)), pl.BlockSpec((tm, 2 * LANE), lambda i: (i, kvcol)),
                  tab, tab, vec, vec],
        out_specs=[pl.BlockSpec((tm, Z_AT), lambda i: (i, 0)), vec, vec],
        out_shape=[SDS((L, Z_AT), BF16), SDS((1, LANE), F32), SDS((1, LANE), F32)],
        compiler_params=_params(("arbitrary",)), name=name)(dqm, dk2, dv2, z, z, cc, ss, wq, wk)


def _at_fwd(qt, kr, vt, *, name):
    L = kr.shape[0]
    G = AT_HEADS // AT_KV
    tq = _tile(L, 384)
    tk = PAD
    nk = L // tk
    R = G * tq
    per = FWD_CHUNKS_PER_STEP if (nk - 1) % FWD_CHUNKS_PER_STEP == 0 else 1

    def body(q_ref, k_ref, v_ref, ob_ref, of_ref, lse_ref, m_scr, acc_scr):
        i = pl.program_id(1)
        qt = jnp.concatenate([q_ref[g] for g in range(G)], axis=1)
        m_scr[...] = jnp.full_like(m_scr, NEG)
        acc_scr[...] = jnp.zeros_like(acc_scr)

        def chunks(c, n, masked):
            start = c * tk if isinstance(c, int) else pl.multiple_of(c * tk, tk)
            st = _nn(k_ref[pl.ds(start, n * tk), :], qt).astype(BF16)
            if masked:
                key = lax.broadcasted_iota(jnp.int32, st.shape, 0)
                st = jnp.where(key >= PAD - N_META, st, NEG)
            m_prev = m_scr[...]
            m_new = jnp.maximum(m_prev, jnp.max(st, axis=0, keepdims=True).astype(F32))
            pt = jnp.exp(st - m_new.astype(BF16))
            acc = jnp.exp(m_prev - m_new) * acc_scr[...]
            for u in range(n):
                acc = acc + _nn(v_ref[0, c + u], pt[u * tk:(u + 1) * tk])
            acc_scr[...] = acc
            m_scr[...] = m_new

        chunks(0, 1, True)

        def loop(t, carry):
            chunks(1 + per * t, per, False)
            return carry

        lax.fori_loop(0, (nk - 1) // per, loop, 0)
        l = acc_scr[pl.ds(AT_HD, 1), :]
        lse = m_scr[...] + jnp.log(l)
        on = acc_scr[0:AT_HD, :] / l
        o = jnp.concatenate([on[:, g * tq:(g + 1) * tq] for g in range(G)], axis=0).T
        rowg = i * tq + lax.broadcasted_iota(jnp.int32, o.shape, 0)
        o = jnp.where(rowg >= PAD - N_META, o, 0.0)
        ob_ref[...] = o.astype(BF16)
        of_ref[...] = o
        for g in range(G):
            lse_ref[g] = lse[:, g * tq:(g + 1) * tq]

    ospec = pl.BlockSpec((tq, G * AT_HD), lambda j, i: (i, j))
    return pl.pallas_call(
        body, grid=(AT_KV, L // tq),
        in_specs=[pl.BlockSpec((G, LANE, tq), lambda j, i: (j, 0, i)), pl.BlockSpec((L, LANE), lambda j, i: (0, 0)),
                  pl.BlockSpec((1, nk, VT_ROWS, tk), lambda j, i: (j, 0, 0, 0))],
        out_specs=[ospec, ospec, pl.BlockSpec((G, 1, tq), lambda j, i: (j, 0, i))],
        out_shape=[SDS((L, AT_W), BF16), SDS((L, AT_W), F32), SDS((AT_HEADS, 1, L), F32)],
        scratch_shapes=[pltpu.VMEM((1, R), F32), pltpu.VMEM((VT_ROWS, R), F32)],
        compiler_params=_params(("parallel", "parallel")), name=name)(qt, kr, vt)


def _at_bwd(qm, qt, kr, krt, vb, do, of, lse, *, name):
    L = kr.shape[0]
    G = AT_HEADS // AT_KV
    tq = _tile(L, 256)
    tk = PAD
    nk = L // tk
    nq = L // tq
    R = G * tq

    def body(qm_ref, q_ref, k_hbm, kt_hbm, v_hbm, do_ref, o_ref, lse_ref, dq_ref, dk_hbm, dv_hbm,
             k_scr, kt_scr, v_scr, dk_scr, dv_scr, dq_scr, sem):
        j, i = pl.program_id(0), pl.program_id(1)

        @pl.when(i == 0)
        def _():
            cps = [pltpu.make_async_copy(k_hbm, k_scr, sem.at[0]), pltpu.make_async_copy(kt_hbm, kt_scr, sem.at[1]),
                   pltpu.make_async_copy(v_hbm.at[j], v_scr, sem.at[2])]
            for cp in cps:
                cp.start()
            dk_scr[...] = jnp.zeros_like(dk_scr)
            dv_scr[...] = jnp.zeros_like(dv_scr)
            for cp in cps:
                cp.wait()

        qt = jnp.concatenate([q_ref[g] for g in range(G)], axis=1)
        rowg = i * tq + lax.broadcasted_iota(jnp.int32, (tq, G * AT_HD), 0)
        dot_all = jnp.where(rowg >= PAD - N_META, do_ref[...], 0.0).T
        ot_all = o_ref[...].T
        dot = jnp.concatenate([dot_all[AT_HD * g:AT_HD * (g + 1)] for g in range(G)], axis=1)
        ot = jnp.concatenate([ot_all[AT_HD * g:AT_HD * (g + 1)] for g in range(G)], axis=1)
        delta = jnp.sum(dot * ot, axis=0, keepdims=True)
        dot128 = jnp.concatenate([dot, jnp.zeros_like(dot)], axis=0)
        dor = dot128.T.astype(BF16)
        dot128 = dot128.astype(BF16)
        qr = qm_ref[...].reshape(R, LANE)
        lse_v = jnp.concatenate([lse_ref[g] for g in range(G)], axis=1)
        dq_scr[...] = jnp.zeros_like(dq_scr)

        def chunk(c, masked):
            start = c * tk if isinstance(c, int) else pl.multiple_of(c * tk, tk)
            k = k_scr[pl.ds(start, tk), :]
            kt = kt_scr[c]
            v = v_scr[pl.ds(start, tk), :]
            st = _nn(k, qt)
            if masked:
                key = lax.broadcasted_iota(jnp.int32, st.shape, 0)
                st = jnp.where(key >= PAD - N_META, st, NEG)
            pt = jnp.exp(st - lse_v)
            dst = (pt * (_nn(v, dot128) - delta)).astype(BF16)
            dq_scr[...] += _nn(kt, dst)
            dk_scr[pl.ds(start, tk), :] += _nn(dst, qr)
            dv_scr[pl.ds(start, tk), :] += _nn(pt.astype(BF16), dor)

        chunk(0, True)

        def loop(c, carry):
            chunk(c, False)
            return carry

        lax.fori_loop(1, nk, loop, 0)
        dq_ref[...] = dq_scr[...].T.reshape(G, tq, LANE)

        @pl.when(i == nq - 1)
        def _():
            ck = pltpu.make_async_copy(dk_scr, dk_hbm.at[j], sem.at[0])
            cv = pltpu.make_async_copy(dv_scr, dv_hbm.at[j], sem.at[1])
            ck.start()
            cv.start()
            ck.wait()
            cv.wait()

    anyspec = pl.BlockSpec(memory_space=pl.ANY)
    ospec = pl.BlockSpec((tq, G * AT_HD), lambda j, i: (i, j))
    return pl.pallas_call(
        body, grid=(AT_KV, nq),
        in_specs=[pl.BlockSpec((G, tq, LANE), lambda j, i: (j, i, 0)), pl.BlockSpec((G, LANE, tq), lambda j, i: (j, 0, i)),
                  anyspec, anyspec, anyspec, ospec, ospec, pl.BlockSpec((G, 1, tq), lambda j, i: (j, 0, i))],
        out_specs=[pl.BlockSpec((G, tq, LANE), lambda j, i: (j, i, 0)), anyspec, anyspec],
        out_shape=[SDS((AT_HEADS, L, LANE), F32), SDS((AT_KV, L, LANE), F32), SDS((AT_KV, L, LANE), F32)],
        scratch_shapes=[pltpu.VMEM((L, LANE), BF16), pltpu.VMEM((nk, LANE, tk), BF16), pltpu.VMEM((L, LANE), BF16),
                        pltpu.VMEM((L, LANE), F32), pltpu.VMEM((L, LANE), F32), pltpu.VMEM((LANE, R), F32),
                        pltpu.SemaphoreType.DMA((3,))],
        compiler_params=_params(("arbitrary", "arbitrary"), VMEM_LARGE), name=name)(qm, qt, kr, krt, vb, do, of, lse)


def _merge_fwd(ya, o8, wua, wubp, z, *, name):
    L = ya.shape[0]
    D = wua.shape[1]
    tm, tn = _tile(L, 512), 256
    ga, gb = (Z_HG + Z_AT) // tn, (Z_HG + Z_AT + D) // tn

    def body(ya_ref, o8_ref, wa_ref, wb_ref, za_ref, zb_ref, mix_ref):
        pa = _nn(ya_ref[...], wa_ref[...])
        pb = _nn(o8_ref[...], wb_ref[...])
        mix_ref[...] = (_sigmoid(za_ref[...]) * pa + _sigmoid(zb_ref[...]) * pb).astype(BF16)

    return pl.pallas_call(
        body, grid=(D // tn, L // tm),
        in_specs=[pl.BlockSpec((tm, ya.shape[1]), lambda j, i: (i, 0)), pl.BlockSpec((tm, o8.shape[1]), lambda j, i: (i, 0)),
                  pl.BlockSpec((wua.shape[0], tn), lambda j, i: (0, j)), pl.BlockSpec((wubp.shape[0], tn), lambda j, i: (0, j)),
                  pl.BlockSpec((tm, tn), lambda j, i: (i, ga + j)), pl.BlockSpec((tm, tn), lambda j, i: (i, gb + j))],
        out_specs=pl.BlockSpec((tm, tn), lambda j, i: (i, j)), out_shape=SDS((L, D), BF16),
        compiler_params=_params(("parallel", "parallel")), name=name)(ya, o8, wua, wubp, z, z)


def _merge_bwd(dh, wout, ya, o8, wua, wubp, z, *, name):
    L = ya.shape[0]
    D = wua.shape[1]
    tm, tn = _tile(L, 512), 256
    ga, gb = (Z_HG + Z_AT) // tn, (Z_HG + Z_AT + D) // tn

    def body(dh_ref, wo_ref, ya_ref, o8_ref, wa_ref, wb_ref, za_ref, zb_ref, dpa_ref, dpb_ref, dza_ref, dzb_ref):
        dm = _nt(dh_ref[...].astype(BF16), wo_ref[...])
        pa = _nn(ya_ref[...], wa_ref[...])
        pb = _nn(o8_ref[...], wb_ref[...])
        sa, sb = _sigmoid(za_ref[...]), _sigmoid(zb_ref[...])
        dpa_ref[...] = (dm * sa).astype(BF16)
        dpb_ref[...] = (dm * sb).astype(BF16)
        dza_ref[...] = (dm * pa * sa * (1.0 - sa)).astype(BF16)
        dzb_ref[...] = (dm * pb * sb * (1.0 - sb)).astype(BF16)

    ospec = pl.BlockSpec((tm, tn), lambda j, i: (i, j))
    return pl.pallas_call(
        body, grid=(D // tn, L // tm),
        in_specs=[pl.BlockSpec((tm, D), lambda j, i: (i, 0)), pl.BlockSpec((tn, D), lambda j, i: (j, 0)),
                  pl.BlockSpec((tm, ya.shape[1]), lambda j, i: (i, 0)), pl.BlockSpec((tm, o8.shape[1]), lambda j, i: (i, 0)),
                  pl.BlockSpec((wua.shape[0], tn), lambda j, i: (0, j)), pl.BlockSpec((wubp.shape[0], tn), lambda j, i: (0, j)),
                  pl.BlockSpec((tm, tn), lambda j, i: (i, ga + j)), pl.BlockSpec((tm, tn), lambda j, i: (i, gb + j))],
        out_specs=[ospec] * 4, out_shape=[SDS((L, D), BF16)] * 4,
        compiler_params=_params(("parallel", "parallel")), name=name)(dh, wout, ya, o8, wua, wubp, z, z)


def _loss_head(h, tgt, *, name):
    L, D = h.shape
    tm = PAD

    def body(h_ref, t_ref, dh_ref, ls_ref):
        i = pl.program_id(0)

        @pl.when(i == 0)
        def _():
            ls_ref[...] = jnp.zeros_like(ls_ref)
            dh_ref[...] = jnp.zeros_like(dh_ref)

        @pl.when(i > 0)
        def _():
            e = h_ref[...] - t_ref[...]
            dh_ref[...] = e * (1.0 / D)
            s = jnp.sum(e * e, axis=0, keepdims=True)
            tot = s[:, :LANE]
            for c in range(1, D // LANE):
                tot = tot + s[:, LANE * c:LANE * (c + 1)]
            ls_ref[...] += tot

    return pl.pallas_call(
        body, grid=(L // tm,),
        in_specs=[pl.BlockSpec((tm, D), lambda i: (i, 0)), pl.BlockSpec((tm, D), lambda i: (jnp.maximum(i - 1, 0), 0))],
        out_specs=[pl.BlockSpec((tm, D), lambda i: (i, 0)), pl.BlockSpec((1, LANE), lambda i: (0, 0))],
        out_shape=[SDS((L, D), F32), SDS((1, LANE), F32)],
        compiler_params=_params(("arbitrary",)), name=name)(h, tgt)


def _adamw(w, g, m, v, *, name):
    shape = w.shape
    w2, g2, m2, v2 = [a.reshape(-1, shape[-1]) for a in (w, g, m, v)]
    rows, cols = w2.shape
    tr = _tile(rows, 256, 8)

    def body(w_ref, g_ref, m_ref, v_ref, d_ref, nm_ref, nv_ref):
        g = g_ref[...]
        m = ADAM_B1 * m_ref[...] + (1.0 - ADAM_B1) * g
        v = ADAM_B2 * v_ref[...] + (1.0 - ADAM_B2) * (g * g)
        m_hat = m / (1.0 - ADAM_B1 ** ADAM_STEP)
        v_hat = v / (1.0 - ADAM_B2 ** ADAM_STEP)
        d_ref[...] = -ADAM_LR * (m_hat / (jnp.sqrt(v_hat) + ADAM_EPS) + ADAM_WD * w_ref[...])
        nm_ref[...] = m
        nv_ref[...] = v

    spec = pl.BlockSpec((tr, cols), lambda i: (i, 0))
    outs = pl.pallas_call(
        body, grid=(rows // tr,), in_specs=[spec] * 4, out_specs=[spec] * 3, out_shape=[SDS((rows, cols), F32)] * 3,
        compiler_params=_params(("parallel",)), name=name)(w2, g2, m2, v2)
    return [o.reshape(shape) for o in outs]


def _place():
    return lax.axis_index("x"), lax.axis_index("y"), lax.axis_index("c")


def _allgather_small(v, *, name):
    m_per, n = v.shape

    def body(x_ref, out_ref, send_sems, recv_sems, local_sem):
        x, y, c = _place()
        me, sibling = (x, y, c), (x, y, 1 - c)
        chips = [(1 - x, y), (x, 1 - y), (1 - x, 1 - y)]

        def rows(px, py, pc):
            return out_ref.at[pl.ds((4 * px + 2 * py + pc) * m_per, m_per), :]

        def copy(k, block, to, src=None):
            return pltpu.make_async_remote_copy(
                src_ref=rows(*block) if src is None else src, dst_ref=rows(*block),
                send_sem=send_sems.at[k], recv_sem=recv_sems.at[k], device_id=to, device_id_type=MESH)

        mine = pltpu.make_async_copy(x_ref, rows(*me), local_sem)
        mine.start()
        first = [copy(0, me, sibling, src=x_ref)]
        first += [copy(1 + j, me, (*chip, c), src=x_ref) for j, chip in enumerate(chips)]
        for cp in first:
            cp.start()
        passed = [copy(4 + j, (*chip, c), sibling) for j, chip in enumerate(chips)]
        for j, chip in enumerate(chips):
            copy(1 + j, (*chip, c), me).wait_recv()
            passed[j].start()
        copy(0, sibling, me).wait_recv()
        for j, chip in enumerate(chips):
            copy(4 + j, (*chip, 1 - c), me).wait_recv()
        for cp in first + passed:
            cp.wait_send()
        mine.wait()

    return pl.pallas_call(
        body, out_shape=SDS((8 * m_per, n), v.dtype),
        in_specs=[pl.BlockSpec(memory_space=pltpu.VMEM)], out_specs=pl.BlockSpec(memory_space=pltpu.VMEM),
        scratch_shapes=[pltpu.SemaphoreType.DMA((7,)), pltpu.SemaphoreType.DMA((7,)), pltpu.SemaphoreType.DMA],
        name=name)(v)


def _chips(x, y):
    return [(1 - x, y), (x, 1 - y), (1 - x, 1 - y)]


def _gather_mats(shards, *, name):
    n = len(shards)

    def body(*refs):
        ins, outs = refs[:n], refs[n:2 * n]
        send_sems, recv_sems, fsend_sems, frecv_sems = refs[2 * n:]
        x, y, c = _place()
        s_me, sibling, chips = 2 * x + y, (x, y, 1 - c), _chips(x, y)

        def copy(src, dst, ssem, rsem, to):
            return pltpu.make_async_remote_copy(src_ref=src, dst_ref=dst, send_sem=ssem, recv_sem=rsem,
                                                device_id=to, device_id_type=MESH)

        first = [copy(ins[t].at[c], outs[t].at[s_me, c], send_sems.at[3 * t + k], recv_sems.at[3 * t + k], (*chip, c))
                 for t in range(n) for k, chip in enumerate(chips)]
        for cp in first:
            cp.start()
        passed = []
        for t in range(n):
            for k, chip in enumerate(chips):
                slot = outs[t].at[2 * chip[0] + chip[1], c]
                copy(ins[t].at[c], slot, send_sems.at[3 * t + k], recv_sems.at[3 * t + k], (*chip, c)).wait_recv()
                fw = copy(slot, slot, fsend_sems.at[3 * t + k], frecv_sems.at[3 * t + k], sibling)
                fw.start()
                passed.append(fw)
        for t in range(n):
            for k, chip in enumerate(chips):
                slot = outs[t].at[2 * chip[0] + chip[1], 1 - c]
                copy(slot, slot, fsend_sems.at[3 * t + k], frecv_sems.at[3 * t + k], sibling).wait_recv()
        for cp in first + passed:
            cp.wait_send()

    anyspec = pl.BlockSpec(memory_space=pl.ANY)
    return pl.pallas_call(
        body, out_shape=[SDS((4,) + s.shape, s.dtype) for s in shards], in_specs=[anyspec] * n, out_specs=[anyspec] * n,
        scratch_shapes=[pltpu.SemaphoreType.DMA((3 * n,))] * 4, name=name)(*shards)


def _rs_pair_exchange(gs, *, name):
    n = len(gs)

    def body(*refs):
        ins, outs = refs[:n], refs[n:2 * n]
        send_sems, recv_sems = refs[2 * n:]
        x, y, c = _place()
        cps = [pltpu.make_async_remote_copy(src_ref=ins[t].at[k, 1 - c], dst_ref=outs[t].at[k],
                                            send_sem=send_sems.at[4 * t + k], recv_sem=recv_sems.at[4 * t + k],
                                            device_id=(x, y, 1 - c), device_id_type=MESH)
               for t in range(n) for k in range(4)]
        for cp in cps:
            cp.start()
        for cp in cps:
            cp.wait()

    anyspec = pl.BlockSpec(memory_space=pl.ANY)
    return pl.pallas_call(
        body, out_shape=[SDS((4,) + g.shape[2:], g.dtype) for g in gs], in_specs=[anyspec] * n, out_specs=[anyspec] * n,
        scratch_shapes=[pltpu.SemaphoreType.DMA((4 * n,))] * 2, name=name)(*gs)


def _rs_chip_exchange(parts, *, name):
    n = len(parts)

    def body(*refs):
        ins, outs = refs[:n], refs[n:2 * n]
        send_sems, recv_sems, local_sems = refs[2 * n:]
        x, y, c = _place()
        s_me, chips = 2 * x + y, _chips(x, y)

        def copy(t, k, chip, src_slot, dst_slot):
            return pltpu.make_async_remote_copy(
                src_ref=ins[t].at[src_slot], dst_ref=outs[t].at[dst_slot], send_sem=send_sems.at[3 * t + k],
                recv_sem=recv_sems.at[3 * t + k], device_id=(*chip, c), device_id_type=MESH)

        mine = [pltpu.make_async_copy(ins[t].at[s_me], outs[t].at[s_me], local_sems.at[t]) for t in range(n)]
        for cp in mine:
            cp.start()
        sends = [copy(t, k, chip, 2 * chip[0] + chip[1], s_me) for t in range(n) for k, chip in enumerate(chips)]
        for cp in sends:
            cp.start()
        for t in range(n):
            for k, chip in enumerate(chips):
                copy(t, k, chip, s_me, 2 * chip[0] + chip[1]).wait_recv()
        for cp in sends:
            cp.wait_send()
        for cp in mine:
            cp.wait()

    anyspec = pl.BlockSpec(memory_space=pl.ANY)
    return pl.pallas_call(
        body, out_shape=[SDS(p.shape, p.dtype) for p in parts], in_specs=[anyspec] * n, out_specs=[anyspec] * n,
        scratch_shapes=[pltpu.SemaphoreType.DMA((3 * n,))] * 2 + [pltpu.SemaphoreType.DMA((n,))], name=name)(*parts)


def _rs_pair_share(fulls, *, name):
    n = len(fulls)

    def body(*refs):
        ins, outs = refs[:n], refs[n:2 * n]
        send_sems, recv_sems = refs[2 * n:]
        x, y, c = _place()

        def copy(t, half):
            return pltpu.make_async_remote_copy(src_ref=ins[t].at[c], dst_ref=outs[t].at[half], send_sem=send_sems.at[t],
                                                recv_sem=recv_sems.at[t], device_id=(x, y, 1 - c), device_id_type=MESH)

        sends = [copy(t, c) for t in range(n)]
        for cp in sends:
            cp.start()
        for t in range(n):
            copy(t, 1 - c).wait_recv()
        for cp in sends:
            cp.wait_send()

    anyspec = pl.BlockSpec(memory_space=pl.ANY)
    return pl.pallas_call(
        body, out_shape=[SDS(f.shape, f.dtype) for f in fulls], in_specs=[anyspec] * n, out_specs=[anyspec] * n,
        input_output_aliases={t: t for t in range(n)},
        scratch_shapes=[pltpu.SemaphoreType.DMA((n,))] * 2, name=name)(*fulls)


def _add_half(g, other, c1, *, out_dtype, name):
    _, _, h, cs = g.shape
    tr = _tile(h, 512, 16)

    def body(c_ref, g_ref, o_ref, out_ref):
        out_ref[...] = (g_ref[...] + o_ref[...]).astype(out_dtype)

    spec = pl.BlockSpec((None, tr, cs), lambda k, i, c: (k, i, 0))
    return pl.pallas_call(
        body, out_shape=SDS(other.shape, out_dtype),
        grid_spec=pltpu.PrefetchScalarGridSpec(
            num_scalar_prefetch=1, grid=(4, h // tr),
            in_specs=[pl.BlockSpec((None, None, tr, cs), lambda k, i, c: (k, c[0], i, 0)), spec], out_specs=spec),
        compiler_params=_params(("parallel", "parallel")), name=name)(c1, g, other)


def _sum4(x, c1, *, name):
    n, h, cs = x.shape
    tr = _tile(h, 512, 16)

    def body(c_ref, x_ref, o_ref):
        tot = x_ref[0].astype(F32)
        for s in range(1, n):
            tot = tot + x_ref[s].astype(F32)
        o_ref[...] = tot

    return pl.pallas_call(
        body, out_shape=SDS((2, h, cs), F32),
        grid_spec=pltpu.PrefetchScalarGridSpec(
            num_scalar_prefetch=1, grid=(h // tr,),
            in_specs=[pl.BlockSpec((n, tr, cs), lambda i, c: (0, i, 0))],
            out_specs=pl.BlockSpec((None, tr, cs), lambda i, c: (c[0], i, 0))),
        compiler_params=_params(("parallel",)), name=name)(c1, x)


def _finish_small(gathered, lbf, lbb, *, rows, name):
    r_lbf, r_lbb = rows['lb_f'], rows['lb_b']

    def body(g_ref, lbf_ref, lbb_ref, o_ref, dlf_ref, dlb_ref):
        tot = g_ref[0]
        for s in range(1, 8):
            tot = tot + g_ref[s]
        o_ref[...] = tot
        o_ref[0:1, :] = jnp.broadcast_to(jnp.sum(o_ref[0:1, :], axis=1, keepdims=True), (1, LANE))
        for lb_ref, d_ref, r0 in ((lbf_ref, dlf_ref, r_lbf), (lbb_ref, dlb_ref, r_lbb)):
            for hh in range(HG_HEADS):
                sl = slice(LANE * hh, LANE * (hh + 1))
                l0, l1 = lb_ref[0:1, sl], lb_ref[1:2, sl]
                mx = jnp.maximum(l0, l1)
                e0, e1 = jnp.exp(l0 - mx), jnp.exp(l1 - mx)
                p0 = e0 / (e0 + e1)
                d0 = o_ref[r0 + hh:r0 + hh + 1, :] * p0 * (1.0 - p0)
                d_ref[0:1, sl] = d0
                d_ref[1:2, sl] = -d0

    vm = pl.BlockSpec(memory_space=pltpu.VMEM)
    return pl.pallas_call(
        body, in_specs=[vm, vm, vm], out_specs=[vm, vm, vm],
        out_shape=[SDS(gathered.shape[1:], F32), SDS(lbf.shape, F32), SDS(lbb.shape, F32)], name=name)(gathered, lbf, lbb)


def _local_step(x2, tgt2, meta, W, S):
    T, D = x2.shape
    L = PAD + T
    h0 = jnp.concatenate([jnp.zeros((PAD - N_META, D), F32), meta, x2], axis=0)

    qk0 = Z_HG
    w_in = jnp.concatenate([W['w_in'][:, :qk0], _qk_to_group(W['w_in'][:, qk0:qk0 + AT_W + AT_KVW]),
                            W['w_in'][:, qk0 + AT_W + AT_KVW:]], axis=1)
    cc, ss = _rope_tables(L)
    wq_g, wk_g = _group_vec(S['q_norm']), _group_vec(S['k_norm'])

    def ffn_fwd(h, nw, wg, wu, wd, tag):
        n = _rmsnorm_fwd(h, nw, name=tag + "_norm")
        g, u, a = _ffn4_up(n, wg, wu, name=tag + "_up")
        hn = _ffn4_down(a, wd, h, name=tag + "_down")
        return hn, (n, g, u, a)

    def ffn_bwd(dh, h, nw, wg, wu, wd, saved, tag, split=False):
        n, g, u, a = saved
        dg, du = _ffn4_dact(dh, wd, g, u, name=tag + "_dact")
        dn = _ffn4_dn(dg, du, wg, wu, name=tag + "_dn")
        dwg = _ffn4_dw(n, dg, x_is_rows=True, name=tag + "_dwg")
        dwu = _ffn4_dw(n, du, x_is_rows=True, name=tag + "_dwu")
        dwd = _ffn4_dw(dh, a, x_is_rows=False, alpha=0.5, name=tag + "_dwd")
        *dhp, dnw = _rmsnorm_bwd(h, nw, dn, dh, split=split, name=tag + "_norm_bwd")
        return (dhp if split else dhp[0]), dnw, dwg, dwu, dwd

    h1, sv1 = ffn_fwd(h0, S['ffn1_norm'], W['ffn1_w_gate'], W['ffn1_w_up'], W['ffn1_w_down'], "ffn1")
    um = _rmsnorm_fwd(h1, S['mix_norm'], name="mix_norm")
    z = _mm([(um, w_in)], tm=512, tn=1792, tk=D, name="in_proj")
    of, sf = _hg_fwd(z, S['hg_lb_fwd'], rev=False, name="hg_fwd_f")
    ob, sb = _hg_fwd(z, S['hg_lb_bwd'], rev=True, name="hg_fwd_b")
    ya = _hg_post_fwd(of, ob, z, S['hg_out_norm'], name="hg_post")
    qm, qt, kr, krt, vb, vt = _at_prep(z, cc, ss, wq_g, wk_g, name="at_prep")
    yb, yb_f32, lse = _at_fwd(qt, kr, vt, name="at_fwd")
    mixed = _merge_fwd(ya, yb, W['w_up_a'], W['w_up_b'], z, name="merge")
    h2 = _mm([(mixed, W['w_out'])], res=h1, tm=512, tn=D, tk=D, name="out_proj")
    h3, sv2 = ffn_fwd(h2, S['ffn2_norm'], W['ffn2_w_gate'], W['ffn2_w_up'], W['ffn2_w_down'], "ffn2")
    dh3, loss_lanes = _loss_head(h3, tgt2, name="loss_head")

    G = {}
    dh2, dn_ffn2, G['ffn2_w_gate'], G['ffn2_w_up'], G['ffn2_w_down'] = ffn_bwd(
        dh3, h2, S['ffn2_norm'], W['ffn2_w_gate'], W['ffn2_w_up'], W['ffn2_w_down'], sv2, "ffn2")
    dpa, dpb, dzga, dzgb = _merge_bwd(dh2, W['w_out'], ya, yb, W['w_up_a'], W['w_up_b'], z, name="merge_bwd")
    G['w_out'] = _mm([(mixed, dh2)], ta=True, tm=D, tn=D, tk=512, name="d_w_out")
    dya = _mm([(dpa, W['w_up_a'])], tb=True, tm=512, tn=HG_W, tk=D, name="d_ya")
    dyb = _mm([(dpb, W['w_up_b'])], tb=True, tm=512, tn=AT_W, tk=D, name="d_yb")
    G['w_up_a'] = _mm([(ya, dpa)], ta=True, tm=HG_W, tn=D, tk=512, name="d_w_up_a")
    G['w_up_b'] = _mm([(yb, dpb)], ta=True, tm=AT_W, tn=D, tk=512, name="d_w_up_b")
    do_hg, dzg, d_hgn = _hg_post_bwd(dya, of, ob, z, S['hg_out_norm'], name="hg_post_bwd")
    dq_f, dv_f, dzf_f, dlb_f = _hg_bwd(z, S['hg_lb_fwd'], do_hg, sf, None, rev=False, name="hg_bwd_f")
    dzq, dzi, dzf_b, dlb_b = _hg_bwd(z, S['hg_lb_bwd'], do_hg, sb, (dq_f, dv_f), rev=True, name="hg_bwd_b")
    dqm, dk2, dv2 = _at_bwd(qm, qt, kr, krt, vb, dyb, yb_f32, lse, name="at_bwd")
    dz_at, dwq_g, dwk_g = _at_prep_bwd(dqm, dk2, dv2, z, cc, ss, wq_g, wk_g, name="at_prep_bwd")
    dz = jnp.concatenate([dzq, dzi, dzf_f, dzf_b, dzg, dz_at, dzga, dzgb], axis=1)
    dum = _mm([(dz, w_in)], tb=True, tm=512, tn=D, tk=1792, name="d_um")
    dw_in_p = _mm([(um, dz)], ta=True, tm=D, tn=1792, tk=512, name="d_w_in")
    G['w_in'] = jnp.concatenate([dw_in_p[:, :qk0], _qk_from_group(dw_in_p[:, qk0:qk0 + AT_W + AT_KVW]),
                                 dw_in_p[:, qk0 + AT_W + AT_KVW:]], axis=1)
    dh1, dn_mix = _rmsnorm_bwd(h1, S['mix_norm'], dum, dh2, name="mix_norm_bwd")
    (grad_x, dmeta), dn_ffn1, G['ffn1_w_gate'], G['ffn1_w_up'], G['ffn1_w_down'] = ffn_bwd(
        dh1, h0, S['ffn1_norm'], W['ffn1_w_gate'], W['ffn1_w_up'], W['ffn1_w_down'], sv1, "ffn1", split=True)

    small_rows = [('loss', loss_lanes), ('ffn1_norm', dn_ffn1.reshape(-1, LANE)), ('mix_norm', dn_mix.reshape(-1, LANE)),
                  ('ffn2_norm', dn_ffn2.reshape(-1, LANE)), ('hg_out_norm', d_hgn.reshape(-1, LANE)),
                  ('lb_f', dlb_f.reshape(-1, LANE)), ('lb_b', dlb_b.reshape(-1, LANE)), ('q_norm', dwq_g), ('k_norm', dwk_g)]
    return grad_x, dmeta, G, small_rows


def kernel(x, meta_tokens, ffn1_norm, ffn1_w_gate, ffn1_w_up, ffn1_w_down, mix_norm, w_in, hg_lb_fwd, hg_lb_bwd, hg_out_norm, q_norm, k_norm, w_up_a, w_up_b, w_out, ffn2_norm, ffn2_w_gate, ffn2_w_up, ffn2_w_down, loss_target, m_meta_tokens, m_ffn1_norm, m_ffn1_w_gate, m_ffn1_w_up, m_ffn1_w_down, m_mix_norm, m_w_in, m_hg_lb_fwd, m_hg_lb_bwd, m_hg_out_norm, m_q_norm, m_k_norm, m_w_up_a, m_w_up_b, m_w_out, m_ffn2_norm, m_ffn2_w_gate, m_ffn2_w_up, m_ffn2_w_down, v_meta_tokens, v_ffn1_norm, v_ffn1_w_gate, v_ffn1_w_up, v_ffn1_w_down, v_mix_norm, v_w_in, v_hg_lb_fwd, v_hg_lb_bwd, v_hg_out_norm, v_q_norm, v_k_norm, v_w_up_a, v_w_up_b, v_w_out, v_ffn2_norm, v_ffn2_w_gate, v_ffn2_w_up, v_ffn2_w_down):
    given = dict(locals())
    w = {n: given[n] for n in WEIGHTS}
    mom = {n: given["m_" + n] for n in WEIGHTS}
    var = {n: given["v_" + n] for n in WEIGHTS}
    c = lax.axis_index("c")
    D = x.shape[-1]

    shapes = {n: w[n].shape[-2:] for n in MATS + ('meta_tokens',)}
    halves = [w[n].astype(BF16).reshape(2, shapes[n][0] // 2, shapes[n][1]) for n in MATS]
    gathered = _gather_mats(halves, name="gather_weights")
    s_me = 2 * lax.axis_index("x") + lax.axis_index("y")
    W = {}
    for n, hv, g4 in zip(MATS, halves, gathered):
        r, cs = shapes[n]
        g4 = lax.dynamic_update_index_in_dim(g4, hv, s_me, 0).reshape(4, r, cs)
        if n in FFN_MATS:
            W[n] = g4
        elif n in ROW_SHARDED:
            W[n] = g4.reshape(4 * r, cs)
        else:
            W[n] = g4.transpose(1, 0, 2).reshape(r, 4 * cs)
    meta_rows = w['meta_tokens'].reshape(-1, LANE)
    mg = _allgather_small(meta_rows, name="gather_meta").reshape(4, 2, N_META, -1)[:, 0]
    meta = mg.transpose(1, 0, 2).reshape(N_META, D)
    S = {n: w[n] for n in SMALLS}

    grad_x, dmeta, G, small_rows = _local_step(x[0], loss_target[0], meta, W, S)
    G['meta_tokens'] = dmeta

    names = MATS + ('meta_tokens',)
    views = []
    for n in names:
        r, cs = shapes[n]
        if n in FFN_MATS:
            g4 = G[n]
        elif n in ROW_SHARDED:
            g4 = G[n].reshape(4, r, cs)
        else:
            g4 = G[n].reshape(r, 4, cs).transpose(1, 0, 2)
        views.append(g4.reshape(4, 2, r // 2, cs))
    c1 = c.astype(jnp.int32).reshape(1)
    from_sibling = _rs_pair_exchange(views, name="rs_pair_exchange")
    parts = [_add_half(v, o, c1, out_dtype=F32 if n == 'meta_tokens' else BF16, name="rs_pair_sum_" + n)
             for n, v, o in zip(names, views, from_sibling)]
    slabs = _rs_chip_exchange(parts, name="rs_chip_exchange")
    reds = [_sum4(s, c1, name="rs_chip_sum_" + n) for n, s in zip(names, slabs)]
    both = _rs_pair_share(reds, name="rs_pair_share")
    grads = {n: b.reshape(w[n].shape) for n, b in zip(names, both)}

    rows, off = {}, 0
    for nme, blk in small_rows:
        rows[nme] = off
        off += blk.shape[0]
    block = jnp.concatenate([blk for _, blk in small_rows], axis=0)
    n_rows = (off + 7) // 8 * 8
    block = jnp.pad(block, ((0, n_rows - off), (0, 0)))
    allsmall = _allgather_small(block, name="gather_small").reshape(8, n_rows, LANE)
    tot, d_lbf, d_lbb = _finish_small(allsmall, w['hg_lb_fwd'], w['hg_lb_bwd'], rows=rows, name="finish_small")
    loss = 0.5 * tot[0, 0] / D

    def small(nme, shape):
        r0 = rows[nme]
        return tot[r0:r0 + shape[-1] // LANE].reshape(shape)

    grads['ffn1_norm'] = small('ffn1_norm', w['ffn1_norm'].shape)
    grads['mix_norm'] = small('mix_norm', w['mix_norm'].shape)
    grads['ffn2_norm'] = small('ffn2_norm', w['ffn2_norm'].shape)
    grads['hg_out_norm'] = small('hg_out_norm', w['hg_out_norm'].shape)
    grads['hg_lb_fwd'] = d_lbf
    grads['hg_lb_bwd'] = d_lbb
    grads['q_norm'] = _ungroup_vec(tot[rows['q_norm']])
    grads['k_norm'] = _ungroup_vec(tot[rows['k_norm']])

    delta, new_m, new_v = {}, {}, {}
    for n in WEIGHTS:
        delta[n], new_m[n], new_v[n] = _adamw(w[n], grads[n], mom[n], var[n], name="adamw_" + n)
    return (loss, grad_x[None], *[grads[n] for n in WEIGHTS], *[delta[n] for n in WEIGHTS],
            *[new_m[n] for n in WEIGHTS], *[new_v[n] for n in WEIGHTS])
```

```python
import numpy as np
import jax
import jax.numpy as jnp
from jax import lax
from jax.experimental import pallas as pl
from jax.experimental.pallas import tpu as pltpu

F32 = jnp.float32
BF16 = jnp.bfloat16
SDS = jax.ShapeDtypeStruct
MESH = pl.DeviceIdType.MESH

EPS = 1e-6
N_META = 16
PAD = 512
LANE = 128
CHUNK = 128
HG_HEADS = 4
HG_W = HG_HEADS * 128
AT_HEADS = 8
AT_KV = 2
AT_HD = 64
AT_W = AT_HEADS * AT_HD
AT_KVW = AT_KV * AT_HD
VT_ROWS = AT_HD + 16
FWD_CHUNKS_PER_STEP = 4
GRID_W = 64
ROPE_THETA = 10000.0
Z_HG = 5 * HG_W
Z_AT = AT_W + 2 * AT_KVW
ADAM_LR, ADAM_B1, ADAM_B2, ADAM_EPS, ADAM_WD, ADAM_STEP = 0.001, 0.9, 0.999, 1e-08, 0.01, 10
VMEM_DEFAULT = 48 * 1024 * 1024
VMEM_LARGE = 60 * 1024 * 1024
NEG = -1e30

MATS = ('ffn1_w_gate', 'ffn1_w_up', 'ffn1_w_down', 'w_in', 'w_up_a', 'w_up_b', 'w_out',
        'ffn2_w_gate', 'ffn2_w_up', 'ffn2_w_down')
ROW_SHARDED = ('ffn1_w_down', 'w_out', 'ffn2_w_down')
FFN_MATS = ('ffn1_w_gate', 'ffn1_w_up', 'ffn1_w_down', 'ffn2_w_gate', 'ffn2_w_up', 'ffn2_w_down')
SMALLS = ('ffn1_norm', 'mix_norm', 'hg_lb_fwd', 'hg_lb_bwd', 'hg_out_norm', 'q_norm', 'k_norm', 'ffn2_norm')
WEIGHTS = ('meta_tokens', 'ffn1_norm', 'ffn1_w_gate', 'ffn1_w_up', 'ffn1_w_down', 'mix_norm', 'w_in', 'hg_lb_fwd',
           'hg_lb_bwd', 'hg_out_norm', 'q_norm', 'k_norm', 'w_up_a', 'w_up_b', 'w_out', 'ffn2_norm', 'ffn2_w_gate',
           'ffn2_w_up', 'ffn2_w_down')


def _params(sem=None, vmem=VMEM_DEFAULT):
    return pltpu.CompilerParams(dimension_semantics=sem, vmem_limit_bytes=vmem)


def _tile(n, pref, q=LANE):
    for d in range(min(pref, n), 0, -1):
        if n % d == 0 and d % q == 0:
            return d
    return n


def _sigmoid(x):
    return 1.0 / (1.0 + jnp.exp(-x))


def _dot(a, b, dims):
    return lax.dot_general(a, b, (dims, ((), ())), preferred_element_type=F32)


def _nn(a, b):
    return _dot(a, b, ((1,), (0,)))


def _nt(a, b):
    return _dot(a, b, ((1,), (1,)))


def _tn(a, b):
    return _dot(a, b, ((0,), (0,)))


def _split3(x):
    x1 = x.astype(BF16)
    r = x - x1.astype(F32)
    x2 = r.astype(BF16)
    x3 = (r - x2.astype(F32)).astype(BF16)
    return x1, x2, x3


def _exact_left(m01, x):
    x1, x2, x3 = _split3(x)
    return _nn(m01, x1) + _nn(m01, x2) + _nn(m01, x3)


def _exact_right(x, m01):
    x1, x2, x3 = _split3(x)
    return _nn(x1, m01) + _nn(x2, m01) + _nn(x3, m01)


def _mm(pairs, *, name, ta=False, tb=False, out_dtype=F32, tm=512, tn=1024, tk=1024, alpha=1.0, res=None):
    a0, b0 = pairs[0]
    M = a0.shape[1] if ta else a0.shape[0]
    K = a0.shape[0] if ta else a0.shape[1]
    N = b0.shape[0] if tb else b0.shape[1]
    tm, tn, tk = _tile(M, tm), _tile(N, tn), _tile(K, tk)
    nk = K // tk
    npair = len(pairs)
    dims = ((0 if ta else 1,), (1 if tb else 0,))

    def body(*refs):
        ab = refs[:2 * npair]
        pos = 2 * npair
        res_ref = None
        if res is not None:
            res_ref = refs[pos]
            pos += 1
        o_ref = refs[pos]

        def partial_sum():
            tot = None
            for p in range(npair):
                d = _dot(ab[2 * p][...].astype(BF16), ab[2 * p + 1][...].astype(BF16), dims)
                tot = d if tot is None else tot + d
            return tot

        def finish(acc):
            r = acc if alpha == 1.0 else acc * alpha
            if res_ref is not None:
                r = res_ref[...] + r
            o_ref[...] = r.astype(out_dtype)

        if nk == 1:
            finish(partial_sum())
        else:
            acc_ref = refs[pos + 1]
            k = pl.program_id(2)

            @pl.when(k == 0)
            def _():
                acc_ref[...] = jnp.zeros_like(acc_ref)

            acc_ref[...] += partial_sum()

            @pl.when(k == nk - 1)
            def _():
                finish(acc_ref[...])

    a_spec = pl.BlockSpec((tk, tm), lambda j, i, k: (k, i)) if ta else pl.BlockSpec((tm, tk), lambda j, i, k: (i, k))
    b_spec = pl.BlockSpec((tn, tk), lambda j, i, k: (j, k)) if tb else pl.BlockSpec((tk, tn), lambda j, i, k: (k, j))
    o_spec = pl.BlockSpec((tm, tn), lambda j, i, k: (i, j))
    in_specs, args = [], []
    for a, b in pairs:
        in_specs += [a_spec, b_spec]
        args += [a, b]
    if res is not None:
        in_specs.append(o_spec)
        args.append(res)
    return pl.pallas_call(
        body, grid=(N // tn, M // tm, nk), in_specs=in_specs, out_specs=o_spec,
        out_shape=SDS((M, N), out_dtype),
        scratch_shapes=[pltpu.VMEM((tm, tn), F32)] if nk > 1 else [],
        compiler_params=_params(("parallel", "parallel", "arbitrary")), name=name)(*args)


def _rmsnorm_fwd(h, w, *, name):
    L, D = h.shape
    tm = _tile(L, 512)

    def body(h_ref, w_ref, o_ref):
        x = h_ref[...]
        r = lax.rsqrt(jnp.mean(x * x, axis=-1, keepdims=True) + EPS)
        o_ref[...] = (x * r * w_ref[...]).astype(BF16)

    return pl.pallas_call(
        body, grid=(L // tm,),
        in_specs=[pl.BlockSpec((tm, D), lambda i: (i, 0)), pl.BlockSpec((1, D), lambda i: (0, 0))],
        out_specs=pl.BlockSpec((tm, D), lambda i: (i, 0)), out_shape=SDS((L, D), BF16),
        compiler_params=_params(("parallel",)), name=name)(h, w)


def _rmsnorm_bwd(h, w, dn, dres, *, split=False, name):
    L, D = h.shape
    tm = PAD if split else _tile(L, 512)

    def body(h_ref, w_ref, dn_ref, dres_ref, dh_ref, *rest):
        dw_ref = rest[-1]
        i = pl.program_id(0)
        x = h_ref[...]
        r = lax.rsqrt(jnp.mean(x * x, axis=-1, keepdims=True) + EPS)
        xh = x * r
        dn = dn_ref[...]
        dxh = dn * w_ref[...]
        dh = dres_ref[...] + r * (dxh - xh * jnp.mean(dxh * xh, axis=-1, keepdims=True))
        dh_ref[...] = dh

        @pl.when(i == 0)
        def _():
            dw_ref[...] = jnp.zeros_like(dw_ref)
            if split:
                rest[0][...] = dh[PAD - N_META:]

        dw_ref[...] += jnp.sum(dn * xh, axis=0, keepdims=True)

    row = pl.BlockSpec((tm, D), lambda i: (i, 0))
    vec = pl.BlockSpec((1, D), lambda i: (0, 0))
    if split:
        out_specs = [pl.BlockSpec((tm, D), lambda i: (jnp.maximum(i - 1, 0), 0)), pl.BlockSpec((N_META, D), lambda i: (0, 0)), vec]
        out_shape = [SDS((L - PAD, D), F32), SDS((N_META, D), F32), SDS((1, D), F32)]
    else:
        out_specs, out_shape = [row, vec], [SDS((L, D), F32), SDS((1, D), F32)]
    return pl.pallas_call(
        body, grid=(L // tm,), in_specs=[row, vec, row, row], out_specs=out_specs, out_shape=out_shape,
        compiler_params=_params(("arbitrary",)), name=name)(h, w, dn, dres)


def _ffn4_up(n, wg4, wu4, *, name):
    L, D = n.shape
    ns, _, cs = wg4.shape
    tm = _tile(L, 768)

    def body(n_ref, wg_ref, wu_ref, g_ref, u_ref, a_ref):
        x = n_ref[...]
        g = _nn(x, wg_ref[...])
        u = _nn(x, wu_ref[...])
        g_ref[...] = g.astype(BF16)
        u_ref[...] = u.astype(BF16)
        a_ref[...] = (g * _sigmoid(g) * u).astype(BF16)

    wspec = pl.BlockSpec((None, D, cs), lambda j, i: (j, 0, 0))
    ospec = pl.BlockSpec((None, tm, cs), lambda j, i: (j, i, 0))
    return pl.pallas_call(
        body, grid=(ns, L // tm),
        in_specs=[pl.BlockSpec((tm, D), lambda j, i: (i, 0)), wspec, wspec], out_specs=[ospec, ospec, ospec],
        out_shape=[SDS((ns, L, cs), BF16), SDS((ns, L, cs), BF16), SDS((ns, L, cs), BF16)],
        compiler_params=_params(("parallel", "parallel")), name=name)(n, wg4, wu4)


def _ffn4_down(a4, wd4, h, *, name):
    ns, L, cs = a4.shape
    D = wd4.shape[2]
    tm = _tile(L, 512)

    def body(a_ref, w_ref, h_ref, o_ref):
        acc = _nn(a_ref[0], w_ref[0])
        for j in range(1, ns):
            acc = acc + _nn(a_ref[j], w_ref[j])
        o_ref[...] = h_ref[...] + 0.5 * acc

    row = pl.BlockSpec((tm, D), lambda i: (i, 0))
    return pl.pallas_call(
        body, grid=(L // tm,),
        in_specs=[pl.BlockSpec((ns, tm, cs), lambda i: (0, i, 0)), pl.BlockSpec((ns, cs, D), lambda i: (0, 0, 0)), row],
        out_specs=row, out_shape=SDS((L, D), F32),
        compiler_params=_params(("parallel",)), name=name)(a4, wd4, h)


def _ffn4_dact(dh, wd4, g4, u4, *, name):
    L, D = dh.shape
    ns, cs, _ = wd4.shape
    tm = _tile(L, 768)

    def body(dh_ref, wd_ref, g_ref, u_ref, dg_ref, du_ref):
        da = 0.5 * _nt(dh_ref[...].astype(BF16), wd_ref[...])
        g = g_ref[...].astype(F32)
        sg = _sigmoid(g)
        dg_ref[...] = (da * u_ref[...].astype(F32) * (sg * (1.0 + g * (1.0 - sg)))).astype(BF16)
        du_ref[...] = (da * (g * sg)).astype(BF16)

    ospec = pl.BlockSpec((None, tm, cs), lambda j, i: (j, i, 0))
    return pl.pallas_call(
        body, grid=(ns, L // tm),
        in_specs=[pl.BlockSpec((tm, D), lambda j, i: (i, 0)), pl.BlockSpec((None, cs, D), lambda j, i: (j, 0, 0)), ospec, ospec],
        out_specs=[ospec, ospec], out_shape=[SDS((ns, L, cs), BF16), SDS((ns, L, cs), BF16)],
        compiler_params=_params(("parallel", "parallel")), name=name)(dh, wd4, g4, u4)


def _ffn4_dn(dg4, du4, wg4, wu4, *, name):
    ns, L, cs = dg4.shape
    D = wg4.shape[1]
    tm = _tile(L, 512)

    def body(dg_ref, du_ref, wg_ref, wu_ref, o_ref):
        acc = None
        for j in range(ns):
            t = _nt(dg_ref[j], wg_ref[j]) + _nt(du_ref[j], wu_ref[j])
            acc = t if acc is None else acc + t
        o_ref[...] = acc

    aspec = pl.BlockSpec((ns, tm, cs), lambda i: (0, i, 0))
    wspec = pl.BlockSpec((ns, D, cs), lambda i: (0, 0, 0))
    return pl.pallas_call(
        body, grid=(L // tm,), in_specs=[aspec, aspec, wspec, wspec],
        out_specs=pl.BlockSpec((tm, D), lambda i: (i, 0)), out_shape=SDS((L, D), F32),
        compiler_params=_params(("parallel",), VMEM_LARGE), name=name)(dg4, du4, wg4, wu4)


def _ffn4_dw(x, y4, *, x_is_rows, alpha=1.0, name):
    L, D = x.shape
    ns, _, cs = y4.shape
    tk = _tile(L, 512)
    nk = L // tk
    oshape = (D, cs) if x_is_rows else (cs, D)

    def body(x_ref, y_ref, o_ref):
        k = pl.program_id(0)

        @pl.when(k == 0)
        def _():
            o_ref[...] = jnp.zeros_like(o_ref)

        xb = x_ref[...].astype(BF16)
        if x_is_rows:
            xt = xb.T
            for j in range(ns):
                o_ref[j] += _nn(xt, y_ref[j])
        else:
            for j in range(ns):
                o_ref[j] += _tn(y_ref[j], xb)

        if alpha != 1.0:
            @pl.when(k == nk - 1)
            def _():
                o_ref[...] = o_ref[...] * alpha

    return pl.pallas_call(
        body, grid=(nk,),
        in_specs=[pl.BlockSpec((tk, D), lambda k: (k, 0)), pl.BlockSpec((ns, tk, cs), lambda k: (0, k, 0))],
        out_specs=pl.BlockSpec((ns,) + oshape, lambda k: (0, 0, 0)), out_shape=SDS((ns,) + oshape, F32),
        compiler_params=_params(("arbitrary",)), name=name)(x, y4)


def _hg_masks(rev):
    t = lax.broadcasted_iota(jnp.int32, (CHUNK, CHUNK), 0)
    s = lax.broadcasted_iota(jnp.int32, (CHUNK, CHUNK), 1)
    causal = (s >= t) if rev else (s <= t)
    levels = []
    for sh in (6, 5, 4):
        same = jnp.right_shift(t, sh + 1) == jnp.right_shift(s, sh + 1)
        tq = jnp.bitwise_and(jnp.right_shift(t, sh), 1)
        sk = jnp.bitwise_and(jnp.right_shift(s, sh), 1)
        levels.append(same & (tq == (0 if rev else 1)) & (sk == (1 if rev else 0)))
    diag = (jnp.right_shift(t, 4) == jnp.right_shift(s, 4)) & causal
    return causal, levels, diag


def _hg_intra_factors(q, k, b, b_scr, rev):
    b_scr[...] = b
    row = lax.broadcasted_iota(jnp.int32, (CHUNK, LANE), 0)
    out = []
    for sh in (6, 5, 4):
        lb = 1 << sh
        pieces = []
        for p in range(0, CHUNK, 2 * lb):
            r = p + lb if rev else p + lb - 1
            pieces.append(jnp.broadcast_to(b_scr[pl.ds(r, 1), :], (2 * lb, LANE)))
        ref = pieces[0] if len(pieces) == 1 else jnp.concatenate(pieces, axis=0)
        qside = jnp.bitwise_and(jnp.right_shift(row, sh), 1) == (0 if rev else 1)
        eq = jnp.where(qside, jnp.exp(jnp.minimum(b - ref, 0.0)), 0.0)
        ek = jnp.where(qside, 0.0, jnp.exp(jnp.minimum(ref - b, 0.0)))
        out.append((eq, ek, (q * eq).astype(BF16), (k * ek).astype(BF16)))
    pieces = []
    for a in range(0, CHUNK, 16):
        r = a + (8 if rev else 7)
        pieces.append(jnp.broadcast_to(b_scr[pl.ds(r, 1), :], (16, LANE)))
    ref = jnp.concatenate(pieces, axis=0)
    eq = jnp.exp(jnp.minimum(b - ref, 80.0))
    ek = jnp.exp(jnp.minimum(ref - b, 80.0))
    out.append((eq, ek, (q * eq).astype(BF16), (k * ek).astype(BF16)))
    return out


def _hg_gate(zf, l0, l1, valid):
    mx = jnp.maximum(l0, l1)
    e0, e1 = jnp.exp(l0 - mx), jnp.exp(l1 - mx)
    p0 = e0 / (e0 + e1)
    sg = _sigmoid(-zf)
    k = jnp.where(valid, (1.0 - p0) * sg, 0.0)
    return p0, sg, k, jnp.log(1.0 - k)


def _hg_fwd(z, lbp, *, rev, name):
    L = z.shape[0]
    nc = L // CHUNK
    fcol = 3 if rev else 2

    def cidx(j):
        return nc - 1 - j if rev else j

    def body(zq_ref, zi_ref, zf_ref, lb_ref, o_ref, ssave_ref, st_scr, b_scr):
        j = pl.program_id(0)

        @pl.when(j == 0)
        def _():
            st_scr[...] = jnp.zeros_like(st_scr)

        causal, lmasks, dmask = _hg_masks(rev)
        tri = jnp.where(causal, 1.0, 0.0).astype(BF16)
        rowg = cidx(j) * CHUNK + lax.broadcasted_iota(jnp.int32, (CHUNK, LANE), 0)
        valid = rowg >= PAD - N_META
        last = 0 if rev else CHUNK - 1
        for hh in range(HG_HEADS):
            sl = slice(LANE * hh, LANE * (hh + 1))
            zq = zq_ref[:, sl]
            q = zq * _sigmoid(zq)
            v = zi_ref[:, sl].astype(BF16)
            _, _, k, g = _hg_gate(zf_ref[:, sl], lb_ref[0:1, sl], lb_ref[1:2, sl], valid)
            b = _exact_left(tri, g)
            st = st_scr[hh]
            ssave_ref[0, hh] = st
            o = _nt((q * jnp.exp(b)).astype(BF16), st.astype(BF16))
            a = None
            fac = _hg_intra_factors(q, k, b, b_scr, rev)
            for (eq, ek, qq, kk), msk in zip(fac, lmasks + [dmask]):
                t = jnp.where(msk, _nt(qq, kk), 0.0)
                a = t if a is None else a + t
            o_ref[:, sl] = o + _nn(a.astype(BF16), v)
            bl = b_scr[pl.ds(last, 1), :]
            kd = (k * jnp.exp(bl - b)).astype(BF16)
            st_scr[hh] = st * jnp.exp(bl) + _tn(v, kd)

    zspec = lambda col: pl.BlockSpec((CHUNK, HG_W), lambda j: (cidx(j), col))
    return pl.pallas_call(
        body, grid=(nc,),
        in_specs=[zspec(0), zspec(1), zspec(fcol), pl.BlockSpec((2, HG_W), lambda j: (0, 0))],
        out_specs=[pl.BlockSpec((CHUNK, HG_W), lambda j: (cidx(j), 0)),
                   pl.BlockSpec((1, HG_HEADS, LANE, LANE), lambda j: (cidx(j), 0, 0, 0))],
        out_shape=[SDS((L, HG_W), F32), SDS((nc, HG_HEADS, LANE, LANE), F32)],
        scratch_shapes=[pltpu.VMEM((HG_HEADS, LANE, LANE), F32), pltpu.VMEM((CHUNK, LANE), F32)],
        compiler_params=_params(("arbitrary",)), name=name)(z, z, z, lbp)


def _hg_bwd(z, lbp, do, ssave, prev, *, rev, name):
    L = z.shape[0]
    nc = L // CHUNK
    fcol = 3 if rev else 2
    final = prev is not None

    def cidx(j):
        return j if rev else nc - 1 - j

    def body(*refs):
        zq_ref, zi_ref, zf_ref, lb_ref, do_ref, ss_ref = refs[:6]
        pos = 6
        if final:
            dqin_ref, dvin_ref = refs[6:8]
            pos = 8
        dq_ref, dv_ref, dzf_ref, dlb_ref, dst_scr, b_scr = refs[pos:pos + 6]
        j = pl.program_id(0)

        @pl.when(j == 0)
        def _():
            dst_scr[...] = jnp.zeros_like(dst_scr)
            dlb_ref[...] = jnp.zeros_like(dlb_ref)

        causal, lmasks, dmask = _hg_masks(rev)
        tri = jnp.where(causal, 1.0, 0.0).astype(BF16)
        ti = lax.broadcasted_iota(jnp.int32, (CHUNK, CHUNK), 0)
        si = lax.broadcasted_iota(jnp.int32, (CHUNK, CHUNK), 1)
        tri_t = jnp.where((si <= ti) if rev else (si >= ti), 1.0, 0.0).astype(BF16)
        rowg = cidx(j) * CHUNK + lax.broadcasted_iota(jnp.int32, (CHUNK, LANE), 0)
        valid = rowg >= PAD - N_META
        last = 0 if rev else CHUNK - 1
        for hh in range(HG_HEADS):
            sl = slice(LANE * hh, LANE * (hh + 1))
            zq = zq_ref[:, sl]
            sq = _sigmoid(zq)
            q = zq * sq
            v = zi_ref[:, sl].astype(BF16)
            p0, sg, k, g = _hg_gate(zf_ref[:, sl], lb_ref[0:1, sl], lb_ref[1:2, sl], valid)
            b = _exact_left(tri, g)
            dob = do_ref[:, sl].astype(BF16)
            st = ss_ref[0, hh]
            dst = dst_scr[hh]
            stb, dstb = st.astype(BF16), dst.astype(BF16)
            eb = jnp.exp(b)
            qe = (q * eb).astype(BF16)
            fac = _hg_intra_factors(q, k, b, b_scr, rev)
            bl = b_scr[pl.ds(last, 1), :]
            ebl = jnp.exp(bl)
            kde = jnp.exp(bl - b)
            kd = (k * kde).astype(BF16)
            da = jnp.where(causal, _nt(dob, v), 0.0)
            dq = eb * _nn(dob, stb)
            dk_inter = kde * _nn(v, dstb)
            dk = dk_inter
            dv = _nt(kd, dstb)
            a = None
            db = q * dq - k * dk
            for (eq, ek, qq, kk), msk in zip(fac, lmasks + [dmask]):
                t = jnp.where(msk, _nt(qq, kk), 0.0)
                a = t if a is None else a + t
                dal = jnp.where(msk, da, 0.0).astype(BF16)
                mq = _nn(dal, kk)
                mk = _tn(dal, qq)
                dq = dq + eq * mq
                dk = dk + ek * mk
                db = db + (qq.astype(F32) * mq - kk.astype(F32) * mk)
            dv = dv + _tn(a.astype(BF16), dob)
            extra = ebl * jnp.sum(st * dst, axis=0, keepdims=True) + jnp.sum(k * dk_inter, axis=0, keepdims=True)
            dst_scr[hh] = dst * ebl + _tn(dob, qe)
            dg = _exact_left(tri_t, db) + extra
            dk_tot = dk - dg / (1.0 - k)
            dzf_ref[:, sl] = jnp.where(valid, dk_tot * (1.0 - p0) * (-sg * (1.0 - sg)), 0.0).astype(BF16)
            dlb_ref[:, sl] += jnp.sum(jnp.where(valid, -sg * dk_tot, 0.0), axis=0, keepdims=True)
            if final:
                dq_ref[:, sl] = ((dq + dqin_ref[:, sl]) * (sq * (1.0 + zq * (1.0 - sq)))).astype(BF16)
                dv_ref[:, sl] = (dv + dvin_ref[:, sl]).astype(BF16)
            else:
                dq_ref[:, sl] = dq
                dv_ref[:, sl] = dv

    zspec = lambda col: pl.BlockSpec((CHUNK, HG_W), lambda j: (cidx(j), col))
    rspec = pl.BlockSpec((CHUNK, HG_W), lambda j: (cidx(j), 0))
    in_specs = [zspec(0), zspec(1), zspec(fcol), pl.BlockSpec((2, HG_W), lambda j: (0, 0)), rspec,
                pl.BlockSpec((1, HG_HEADS, LANE, LANE), lambda j: (cidx(j), 0, 0, 0))]
    args = [z, z, z, lbp, do, ssave]
    if final:
        in_specs += [rspec, rspec]
        args += list(prev)
    odt = BF16 if final else F32
    return pl.pallas_call(
        body, grid=(nc,), in_specs=in_specs,
        out_specs=[rspec, rspec, rspec, pl.BlockSpec((1, HG_W), lambda j: (0, 0))],
        out_shape=[SDS((L, HG_W), odt), SDS((L, HG_W), odt), SDS((L, HG_W), BF16), SDS((1, HG_W), F32)],
        scratch_shapes=[pltpu.VMEM((HG_HEADS, LANE, LANE), F32), pltpu.VMEM((CHUNK, LANE), F32)],
        compiler_params=_params(("arbitrary",)), name=name)(*args)


def _hg_post_fwd(of, ob, z, w, *, name):
    L = of.shape[0]
    tm = _tile(L, 512)

    def body(of_ref, ob_ref, zg_ref, w_ref, y_ref):
        for hh in range(HG_HEADS):
            sl = slice(LANE * hh, LANE * (hh + 1))
            o = of_ref[:, sl] + ob_ref[:, sl]
            r = lax.rsqrt(jnp.mean(o * o, axis=-1, keepdims=True) + EPS)
            zg = zg_ref[:, sl]
            y_ref[:, sl] = (o * r * w_ref[:, sl] * (zg * _sigmoid(zg))).astype(BF16)

    row = pl.BlockSpec((tm, HG_W), lambda i: (i, 0))
    return pl.pallas_call(
        body, grid=(L // tm,),
        in_specs=[row, row, pl.BlockSpec((tm, HG_W), lambda i: (i, 4)), pl.BlockSpec((1, HG_W), lambda i: (0, 0))],
        out_specs=row, out_shape=SDS((L, HG_W), BF16),
        compiler_params=_params(("parallel",)), name=name)(of, ob, z, w)


def _hg_post_bwd(dy, of, ob, z, w, *, name):
    L = of.shape[0]
    tm = _tile(L, 512)

    def body(dy_ref, of_ref, ob_ref, zg_ref, w_ref, do_ref, dzg_ref, dw_ref):
        @pl.when(pl.program_id(0) == 0)
        def _():
            dw_ref[...] = jnp.zeros_like(dw_ref)

        for hh in range(HG_HEADS):
            sl = slice(LANE * hh, LANE * (hh + 1))
            o = of_ref[:, sl] + ob_ref[:, sl]
            r = lax.rsqrt(jnp.mean(o * o, axis=-1, keepdims=True) + EPS)
            xh = o * r
            zg = zg_ref[:, sl]
            sg = _sigmoid(zg)
            w = w_ref[:, sl]
            dy = dy_ref[:, sl]
            dys = dy * (zg * sg)
            dzg_ref[:, sl] = (dy * xh * w * (sg * (1.0 + zg * (1.0 - sg)))).astype(BF16)
            dw_ref[:, sl] += jnp.sum(dys * xh, axis=0, keepdims=True)
            dxh = dys * w
            do_ref[:, sl] = r * (dxh - xh * jnp.mean(dxh * xh, axis=-1, keepdims=True))

    row = pl.BlockSpec((tm, HG_W), lambda i: (i, 0))
    vec = pl.BlockSpec((1, HG_W), lambda i: (0, 0))
    return pl.pallas_call(
        body, grid=(L // tm,),
        in_specs=[row, row, row, pl.BlockSpec((tm, HG_W), lambda i: (i, 4)), vec],
        out_specs=[row, row, vec],
        out_shape=[SDS((L, HG_W), F32), SDS((L, HG_W), BF16), SDS((1, HG_W), F32)],
        compiler_params=_params(("arbitrary",)), name=name)(dy, of, ob, z, w)


N_GROUPS = (AT_HEADS + AT_KV) // 2


def _qk_to_group(wqk):
    d = wqk.shape[0]
    return wqk.reshape(d, N_GROUPS, 2, AT_HD // 2, 2).transpose(0, 1, 4, 2, 3).reshape(d, N_GROUPS * LANE)


def _qk_from_group(wqk):
    d = wqk.shape[0]
    return wqk.reshape(d, N_GROUPS, 2, 2, AT_HD // 2).transpose(0, 1, 3, 4, 2).reshape(d, N_GROUPS * LANE)


def _group_vec(w64):
    halves = w64.reshape(AT_HD // 2, 2).T
    return jnp.broadcast_to(halves[:, None, :], (2, 2, AT_HD // 2)).reshape(1, LANE)


def _ungroup_vec(w128):
    w = w128.reshape(2, 2, 32).sum(axis=1)
    return w.T.reshape(1, AT_HD)


def _rope_tables(L):
    n_real = L - PAD
    t = np.arange(n_real)
    row = np.concatenate([np.zeros(PAD), t // GRID_W]).astype(np.float32)
    col = np.concatenate([np.zeros(PAD), t % GRID_W]).astype(np.float32)
    inv = jnp.asarray(ROPE_THETA, F32) ** (-jnp.arange(0, AT_HD // 2, 2, dtype=F32) / (AT_HD // 2))
    ang = jnp.concatenate([jnp.asarray(row)[:, None] * inv, jnp.asarray(col)[:, None] * inv], axis=-1)
    cos, sin = jnp.cos(ang), jnp.sin(ang)
    cc = jnp.tile(cos, (1, 4))
    ss = jnp.concatenate([-sin, -sin, sin, sin], axis=1)
    return cc, ss


def _seg_matrix():
    a = lax.broadcasted_iota(jnp.int32, (LANE, LANE), 0)
    b = lax.broadcasted_iota(jnp.int32, (LANE, LANE), 1)
    same = jnp.bitwise_and(jnp.right_shift(a, 5), 1) == jnp.bitwise_and(jnp.right_shift(b, 5), 1)
    return jnp.where(same, 1.0, 0.0).astype(BF16)


def _slot_mask(shape, hp):
    lane = lax.broadcasted_iota(jnp.int32, shape, 1)
    return jnp.bitwise_and(jnp.right_shift(lane, 5), 1) == hp


def _at_prep(z, cc, ss, wq, wk, *, name):
    L = z.shape[0]
    tm = PAD
    qcol = Z_HG // AT_W
    kvcol = (Z_HG + AT_W) // (2 * LANE)

    def body(zq_ref, zkv_ref, cc_ref, ss_ref, wq_ref, wk_ref, qm_ref, qt_ref, kr_ref, krt_ref, vb_ref, vt_ref):
        seg = _seg_matrix()
        cc, ss = cc_ref[...], ss_ref[...]

        def normrope(x, w):
            r = lax.rsqrt(_exact_right(x * x, seg) * (1.0 / AT_HD) + EPS)
            y = x * r * w
            return y * cc + pltpu.roll(y, 64, 1) * ss

        for g in range(AT_HEADS // 2):
            o = normrope(zq_ref[:, LANE * g:LANE * (g + 1)], wq_ref[...]) * (AT_HD ** -0.5)
            for hp in range(2):
                h = 2 * g + hp
                tgt = h // (AT_HEADS // AT_KV)
                xm = jnp.where(_slot_mask(o.shape, hp), o, 0.0)
                if tgt != hp:
                    xm = pltpu.roll(xm, 32 if tgt == 1 else 96, 1)
                qm_ref[h] = xm.astype(BF16)
                qt_ref[h] = xm.T.astype(BF16)
        kr = normrope(zkv_ref[:, :LANE], wk_ref[...])
        kr_ref[...] = kr.astype(BF16)
        krt_ref[0] = kr.T.astype(BF16)
        v = zkv_ref[:, LANE:]
        low = lax.broadcasted_iota(jnp.int32, v.shape, 1) < AT_HD
        vb_ref[0] = jnp.where(low, v, 0.0).astype(BF16)
        vb_ref[1] = jnp.where(low, pltpu.roll(v, AT_HD, 1), 0.0).astype(BF16)
        vt = v.T.astype(BF16)
        ones = jnp.ones((VT_ROWS - AT_HD, tm), BF16)
        for j in range(AT_KV):
            vt_ref[j, 0, 0:AT_HD, :] = vt[AT_HD * j:AT_HD * (j + 1)]
            vt_ref[j, 0, AT_HD:VT_ROWS, :] = ones

    tab = pl.BlockSpec((tm, LANE), lambda i: (i, 0))
    vec = pl.BlockSpec((1, LANE), lambda i: (0, 0))
    nt = L // tm
    return pl.pallas_call(
        body, grid=(nt,),
        in_specs=[pl.BlockSpec((tm, AT_W), lambda i: (i, qcol)), pl.BlockSpec((tm, 2 * LANE), lambda i: (i, kvcol)),
                  tab, tab, vec, vec],
        out_specs=[pl.BlockSpec((AT_HEADS, tm, LANE), lambda i: (0, i, 0)),
                   pl.BlockSpec((AT_HEADS, LANE, tm), lambda i: (0, 0, i)), tab,
                   pl.BlockSpec((1, LANE, tm), lambda i: (i, 0, 0)),
                   pl.BlockSpec((AT_KV, tm, LANE), lambda i: (0, i, 0)),
                   pl.BlockSpec((AT_KV, 1, VT_ROWS, tm), lambda i: (0, i, 0, 0))],
        out_shape=[SDS((AT_HEADS, L, LANE), BF16), SDS((AT_HEADS, LANE, L), BF16), SDS((L, LANE), BF16),
                   SDS((nt, LANE, tm), BF16), SDS((AT_KV, L, LANE), BF16), SDS((AT_KV, nt, VT_ROWS, tm), BF16)],
        compiler_params=_params(("parallel",)), name=name)(z, z, cc, ss, wq, wk)


def _at_prep_bwd(dqm, dk2, dv2, z, cc, ss, wq, wk, *, name):
    L = z.shape[0]
    tm = PAD
    qcol = Z_HG // AT_W
    kvcol = (Z_HG + AT_W) // (2 * LANE)

    def body(dqm_ref, dk2_ref, dv2_ref, zq_ref, zkv_ref, cc_ref, ss_ref, wq_ref, wk_ref, dz_ref, dwq_ref, dwk_ref):
        @pl.when(pl.program_id(0) == 0)
        def _():
            dwq_ref[...] = jnp.zeros_like(dwq_ref)
            dwk_ref[...] = jnp.zeros_like(dwk_ref)

        seg = _seg_matrix()
        cc, ss = cc_ref[...], ss_ref[...]

        def back(x, w, do):
            dy = do * cc + pltpu.roll(do * ss, 64, 1)
            r = lax.rsqrt(_exact_right(x * x, seg) * (1.0 / AT_HD) + EPS)
            xh = x * r
            dxh = dy * w
            dx = r * (dxh - xh * (_exact_right(dxh * xh, seg) * (1.0 / AT_HD)))
            return dx, jnp.sum(dy * xh, axis=0, keepdims=True)

        for g in range(AT_HEADS // 2):
            do = None
            for hp in range(2):
                h = 2 * g + hp
                tgt = h // (AT_HEADS // AT_KV)
                d = jnp.where(_slot_mask((tm, LANE), tgt), dqm_ref[h], 0.0)
                if tgt != hp:
                    d = pltpu.roll(d, 96 if tgt == 1 else 32, 1)
                do = d if do is None else do + d
            dx, dw = back(zq_ref[:, LANE * g:LANE * (g + 1)], wq_ref[...], do * (AT_HD ** -0.5))
            dz_ref[:, LANE * g:LANE * (g + 1)] = dx.astype(BF16)
            dwq_ref[...] += dw
        dx, dw = back(zkv_ref[:, :LANE], wk_ref[...], dk2_ref[0] + dk2_ref[1])
        dz_ref[:, AT_W:AT_W + LANE] = dx.astype(BF16)
        dwk_ref[...] += dw
        dv0 = dv2_ref[0]
        low = lax.broadcasted_iota(jnp.int32, dv0.shape, 1) < AT_HD
        dz_ref[:, AT_W + LANE:] = jnp.where(low, dv0, pltpu.roll(dv2_ref[1], AT_HD, 1)).astype(BF16)

    tab = pl.BlockSpec((tm, LANE), lambda i: (i, 0))
    vec = pl.BlockSpec((1, LANE), lambda i: (0, 0))
    two = pl.BlockSpec((AT_KV, tm, LANE), lambda i: (0, i, 0))
    return pl.pallas_call(
        body, grid=(L // tm,),
        in_specs=[pl.BlockSpec((AT_HEADS, tm, LANE), lambda i: (0, i, 0)), two, two,
                  pl.BlockSpec((tm, AT_W), lambda i: (i, qcol)), pl.BlockSpec((tm, 2 * LANE), lambda i: (i, kvcol)),
                  tab, tab, vec, vec],
        out_specs=[pl.BlockSpec((tm, Z_AT), lambda i: (i, 0)), vec, vec],
        out_shape=[SDS((L, Z_AT), BF16), SDS((1, LANE), F32), SDS((1, LANE), F32)],
        compiler_params=_params(("arbitrary",)), name=name)(dqm, dk2, dv2, z, z, cc, ss, wq, wk)


def _at_fwd(qt, kr, vt, *, name):
    L = kr.shape[0]
    G = AT_HEADS // AT_KV
    tq = _tile(L, 384)
    tk = PAD
    nk = L // tk
    R = G * tq
    per = FWD_CHUNKS_PER_STEP if (nk - 1) % FWD_CHUNKS_PER_STEP == 0 else 1

    def body(q_ref, k_ref, v_ref, ob_ref, of_ref, lse_ref, m_scr, acc_scr):
        i = pl.program_id(1)
        qt = jnp.concatenate([q_ref[g] for g in range(G)], axis=1)
        m_scr[...] = jnp.full_like(m_scr, NEG)
        acc_scr[...] = jnp.zeros_like(acc_scr)

        def chunks(c, n, masked):
            start = c * tk if isinstance(c, int) else pl.multiple_of(c * tk, tk)
            st = _nn(k_ref[pl.ds(start, n * tk), :], qt).astype(BF16)
            if masked:
                key = lax.broadcasted_iota(jnp.int32, st.shape, 0)
                st = jnp.where(key >= PAD - N_META, st, NEG)
            m_prev = m_scr[...]
            m_new = jnp.maximum(m_prev, jnp.max(st, axis=0, keepdims=True).astype(F32))
            pt = jnp.exp(st - m_new.astype(BF16))
            acc = jnp.exp(m_prev - m_new) * acc_scr[...]
            for u in range(n):
                acc = acc + _nn(v_ref[0, c + u], pt[u * tk:(u + 1) * tk])
            acc_scr[...] = acc
            m_scr[...] = m_new

        chunks(0, 1, True)

        def loop(t, carry):
            chunks(1 + per * t, per, False)
            return carry

        lax.fori_loop(0, (nk - 1) // per, loop, 0)
        l = acc_scr[pl.ds(AT_HD, 1), :]
        lse = m_scr[...] + jnp.log(l)
        on = acc_scr[0:AT_HD, :] / l
        o = jnp.concatenate([on[:, g * tq:(g + 1) * tq] for g in range(G)], axis=0).T
        rowg = i * tq + lax.broadcasted_iota(jnp.int32, o.shape, 0)
        o = jnp.where(rowg >= PAD - N_META, o, 0.0)
        ob_ref[...] = o.astype(BF16)
        of_ref[...] = o
        for g in range(G):
            lse_ref[g] = lse[:, g * tq:(g + 1) * tq]

    ospec = pl.BlockSpec((tq, G * AT_HD), lambda j, i: (i, j))
    return pl.pallas_call(
        body, grid=(AT_KV, L // tq),
        in_specs=[pl.BlockSpec((G, LANE, tq), lambda j, i: (j, 0, i)), pl.BlockSpec((L, LANE), lambda j, i: (0, 0)),
                  pl.BlockSpec((1, nk, VT_ROWS, tk), lambda j, i: (j, 0, 0, 0))],
        out_specs=[ospec, ospec, pl.BlockSpec((G, 1, tq), lambda j, i: (j, 0, i))],
        out_shape=[SDS((L, AT_W), BF16), SDS((L, AT_W), F32), SDS((AT_HEADS, 1, L), F32)],
        scratch_shapes=[pltpu.VMEM((1, R), F32), pltpu.VMEM((VT_ROWS, R), F32)],
        compiler_params=_params(("parallel", "parallel")), name=name)(qt, kr, vt)


def _at_bwd(qm, qt, kr, krt, vb, do, of, lse, *, name):
    L = kr.shape[0]
    G = AT_HEADS // AT_KV
    tq = _tile(L, 256)
    tk = PAD
    nk = L // tk
    nq = L // tq
    R = G * tq

    def body(qm_ref, q_ref, k_hbm, kt_hbm, v_hbm, do_ref, o_ref, lse_ref, dq_ref, dk_hbm, dv_hbm,
             k_scr, kt_scr, v_scr, dk_scr, dv_scr, dq_scr, sem):
        j, i = pl.program_id(0), pl.program_id(1)

        @pl.when(i == 0)
        def _():
            cps = [pltpu.make_async_copy(k_hbm, k_scr, sem.at[0]), pltpu.make_async_copy(kt_hbm, kt_scr, sem.at[1]),
                   pltpu.make_async_copy(v_hbm.at[j], v_scr, sem.at[2])]
            for cp in cps:
                cp.start()
            dk_scr[...] = jnp.zeros_like(dk_scr)
            dv_scr[...] = jnp.zeros_like(dv_scr)
            for cp in cps:
                cp.wait()

        qt = jnp.concatenate([q_ref[g] for g in range(G)], axis=1)
        rowg = i * tq + lax.broadcasted_iota(jnp.int32, (tq, G * AT_HD), 0)
        dot_all = jnp.where(rowg >= PAD - N_META, do_ref[...], 0.0).T
        ot_all = o_ref[...].T
        dot = jnp.concatenate([dot_all[AT_HD * g:AT_HD * (g + 1)] for g in range(G)], axis=1)
        ot = jnp.concatenate([ot_all[AT_HD * g:AT_HD * (g + 1)] for g in range(G)], axis=1)
        delta = jnp.sum(dot * ot, axis=0, keepdims=True)
        dot128 = jnp.concatenate([dot, jnp.zeros_like(dot)], axis=0)
        dor = dot128.T.astype(BF16)
        dot128 = dot128.astype(BF16)
        qr = qm_ref[...].reshape(R, LANE)
        lse_v = jnp.concatenate([lse_ref[g] for g in range(G)], axis=1)
        dq_scr[...] = jnp.zeros_like(dq_scr)

        def chunk(c, masked):
            start = c * tk if isinstance(c, int) else pl.multiple_of(c * tk, tk)
            k = k_scr[pl.ds(start, tk), :]
            kt = kt_scr[c]
            v = v_scr[pl.ds(start, tk), :]
            st = _nn(k, qt).astype(BF16).astype(F32)
            if masked:
                key = lax.broadcasted_iota(jnp.int32, st.shape, 0)
                st = jnp.where(key >= PAD - N_META, st, NEG)
            pt = jnp.exp(st - lse_v)
            dst = (pt * (_nn(v, dot128) - delta)).astype(BF16)
            dq_scr[...] += _nn(kt, dst)
            dk_scr[pl.ds(start, tk), :] += _nn(dst, qr)
            dv_scr[pl.ds(start, tk), :] += _nn(pt.astype(BF16), dor)

        chunk(0, True)

        def loop(c, carry):
            chunk(c, False)
            return carry

        lax.fori_loop(1, nk, loop, 0)
        dq_ref[...] = dq_scr[...].T.reshape(G, tq, LANE)

        @pl.when(i == nq - 1)
        def _():
            ck = pltpu.make_async_copy(dk_scr, dk_hbm.at[j], sem.at[0])
            cv = pltpu.make_async_copy(dv_scr, dv_hbm.at[j], sem.at[1])
            ck.start()
            cv.start()
            ck.wait()
            cv.wait()

    anyspec = pl.BlockSpec(memory_space=pl.ANY)
    ospec = pl.BlockSpec((tq, G * AT_HD), lambda j, i: (i, j))
    return pl.pallas_call(
        body, grid=(AT_KV, nq),
        in_specs=[pl.BlockSpec((G, tq, LANE), lambda j, i: (j, i, 0)), pl.BlockSpec((G, LANE, tq), lambda j, i: (j, 0, i)),
                  anyspec, anyspec, anyspec, ospec, ospec, pl.BlockSpec((G, 1, tq), lambda j, i: (j, 0, i))],
        out_specs=[pl.BlockSpec((G, tq, LANE), lambda j, i: (j, i, 0)), anyspec, anyspec],
        out_shape=[SDS((AT_HEADS, L, LANE), F32), SDS((AT_KV, L, LANE), F32), SDS((AT_KV, L, LANE), F32)],
        scratch_shapes=[pltpu.VMEM((L, LANE), BF16), pltpu.VMEM((nk, LANE, tk), BF16), pltpu.VMEM((L, LANE), BF16),
                        pltpu.VMEM((L, LANE), F32), pltpu.VMEM((L, LANE), F32), pltpu.VMEM((LANE, R), F32),
                        pltpu.SemaphoreType.DMA((3,))],
        compiler_params=_params(("arbitrary", "arbitrary"), VMEM_LARGE), name=name)(qm, qt, kr, krt, vb, do, of, lse)


def _merge_fwd(ya, o8, wua, wubp, z, *, name):
    L = ya.shape[0]
    D = wua.shape[1]
    tm, tn = _tile(L, 1536), 256
    ga, gb = (Z_HG + Z_AT) // tn, (Z_HG + Z_AT + D) // tn

    def body(ya_ref, o8_ref, wa_ref, wb_ref, za_ref, zb_ref, mix_ref):
        pa = _nn(ya_ref[...], wa_ref[...])
        pb = _nn(o8_ref[...], wb_ref[...])
        mix_ref[...] = (_sigmoid(za_ref[...]) * pa + _sigmoid(zb_ref[...]) * pb).astype(BF16)

    return pl.pallas_call(
        body, grid=(D // tn, L // tm),
        in_specs=[pl.BlockSpec((tm, ya.shape[1]), lambda j, i: (i, 0)), pl.BlockSpec((tm, o8.shape[1]), lambda j, i: (i, 0)),
                  pl.BlockSpec((wua.shape[0], tn), lambda j, i: (0, j)), pl.BlockSpec((wubp.shape[0], tn), lambda j, i: (0, j)),
                  pl.BlockSpec((tm, tn), lambda j, i: (i, ga + j)), pl.BlockSpec((tm, tn), lambda j, i: (i, gb + j))],
        out_specs=pl.BlockSpec((tm, tn), lambda j, i: (i, j)), out_shape=SDS((L, D), BF16),
        compiler_params=_params(("parallel", "parallel")), name=name)(ya, o8, wua, wubp, z, z)


def _merge_bwd(dh, wout, ya, o8, wua, wubp, z, *, name):
    L = ya.shape[0]
    D = wua.shape[1]
    tm, tn = _tile(L, 1536), 256
    ga, gb = (Z_HG + Z_AT) // tn, (Z_HG + Z_AT + D) // tn

    def body(dh_ref, wo_ref, ya_ref, o8_ref, wa_ref, wb_ref, za_ref, zb_ref, dpa_ref, dpb_ref, dza_ref, dzb_ref):
        dm = _nt(dh_ref[...].astype(BF16), wo_ref[...])
        pa = _nn(ya_ref[...], wa_ref[...])
        pb = _nn(o8_ref[...], wb_ref[...])
        sa, sb = _sigmoid(za_ref[...]), _sigmoid(zb_ref[...])
        dpa_ref[...] = (dm * sa).astype(BF16)
        dpb_ref[...] = (dm * sb).astype(BF16)
        dza_ref[...] = (dm * pa * sa * (1.0 - sa)).astype(BF16)
        dzb_ref[...] = (dm * pb * sb * (1.0 - sb)).astype(BF16)

    ospec = pl.BlockSpec((tm, tn), lambda j, i: (i, j))
    return pl.pallas_call(
        body, grid=(D // tn, L // tm),
        in_specs=[pl.BlockSpec((tm, D), lambda j, i: (i, 0)), pl.BlockSpec((tn, D), lambda j, i: (j, 0)),
                  pl.BlockSpec((tm, ya.shape[1]), lambda j, i: (i, 0)), pl.BlockSpec((tm, o8.shape[1]), lambda j, i: (i, 0)),
                  pl.BlockSpec((wua.shape[0], tn), lambda j, i: (0, j)), pl.BlockSpec((wubp.shape[0], tn), lambda j, i: (0, j)),
                  pl.BlockSpec((tm, tn), lambda j, i: (i, ga + j)), pl.BlockSpec((tm, tn), lambda j, i: (i, gb + j))],
        out_specs=[ospec] * 4, out_shape=[SDS((L, D), BF16)] * 4,
        compiler_params=_params(("parallel", "parallel")), name=name)(dh, wout, ya, o8, wua, wubp, z, z)


def _loss_head(h, tgt, *, name):
    L, D = h.shape
    tm = PAD

    def body(h_ref, t_ref, dh_ref, ls_ref):
        i = pl.program_id(0)

        @pl.when(i == 0)
        def _():
            ls_ref[...] = jnp.zeros_like(ls_ref)
            dh_ref[...] = jnp.zeros_like(dh_ref)

        @pl.when(i > 0)
        def _():
            e = h_ref[...] - t_ref[...]
            dh_ref[...] = e * (1.0 / D)
            s = jnp.sum(e * e, axis=0, keepdims=True)
            tot = s[:, :LANE]
            for c in range(1, D // LANE):
                tot = tot + s[:, LANE * c:LANE * (c + 1)]
            ls_ref[...] += tot

    return pl.pallas_call(
        body, grid=(L // tm,),
        in_specs=[pl.BlockSpec((tm, D), lambda i: (i, 0)), pl.BlockSpec((tm, D), lambda i: (jnp.maximum(i - 1, 0), 0))],
        out_specs=[pl.BlockSpec((tm, D), lambda i: (i, 0)), pl.BlockSpec((1, LANE), lambda i: (0, 0))],
        out_shape=[SDS((L, D), F32), SDS((1, LANE), F32)],
        compiler_params=_params(("arbitrary",)), name=name)(h, tgt)


def _adamw(w, g, m, v, *, name):
    shape = w.shape
    w2, g2, m2, v2 = [a.reshape(-1, shape[-1]) for a in (w, g, m, v)]
    rows, cols = w2.shape
    tr = _tile(rows, 256, 8)

    def body(w_ref, g_ref, m_ref, v_ref, d_ref, nm_ref, nv_ref):
        g = g_ref[...]
        m = ADAM_B1 * m_ref[...] + (1.0 - ADAM_B1) * g
        v = ADAM_B2 * v_ref[...] + (1.0 - ADAM_B2) * (g * g)
        m_hat = m / (1.0 - ADAM_B1 ** ADAM_STEP)
        v_hat = v / (1.0 - ADAM_B2 ** ADAM_STEP)
        d_ref[...] = -ADAM_LR * (m_hat / (jnp.sqrt(v_hat) + ADAM_EPS) + ADAM_WD * w_ref[...])
        nm_ref[...] = m
        nv_ref[...] = v

    spec = pl.BlockSpec((tr, cols), lambda i: (i, 0))
    outs = pl.pallas_call(
        body, grid=(rows // tr,), in_specs=[spec] * 4, out_specs=[spec] * 3, out_shape=[SDS((rows, cols), F32)] * 3,
        compiler_params=_params(("parallel",)), name=name)(w2, g2, m2, v2)
    return [o.reshape(shape) for o in outs]


def _place():
    return lax.axis_index("x"), lax.axis_index("y"), lax.axis_index("c")


def _allgather_small(v, *, name):
    m_per, n = v.shape

    def body(x_ref, out_ref, send_sems, recv_sems, local_sem):
        x, y, c = _place()
        me, sibling = (x, y, c), (x, y, 1 - c)
        chips = [(1 - x, y), (x, 1 - y), (1 - x, 1 - y)]

        def rows(px, py, pc):
            return out_ref.at[pl.ds((4 * px + 2 * py + pc) * m_per, m_per), :]

        def copy(k, block, to, src=None):
            return pltpu.make_async_remote_copy(
                src_ref=rows(*block) if src is None else src, dst_ref=rows(*block),
                send_sem=send_sems.at[k], recv_sem=recv_sems.at[k], device_id=to, device_id_type=MESH)

        mine = pltpu.make_async_copy(x_ref, rows(*me), local_sem)
        mine.start()
        first = [copy(0, me, sibling, src=x_ref)]
        first += [copy(1 + j, me, (*chip, c), src=x_ref) for j, chip in enumerate(chips)]
        for cp in first:
            cp.start()
        passed = [copy(4 + j, (*chip, c), sibling) for j, chip in enumerate(chips)]
        for j, chip in enumerate(chips):
            copy(1 + j, (*chip, c), me).wait_recv()
            passed[j].start()
        copy(0, sibling, me).wait_recv()
        for j, chip in enumerate(chips):
            copy(4 + j, (*chip, 1 - c), me).wait_recv()
        for cp in first + passed:
            cp.wait_send()
        mine.wait()

    return pl.pallas_call(
        body, out_shape=SDS((8 * m_per, n), v.dtype),
        in_specs=[pl.BlockSpec(memory_space=pltpu.VMEM)], out_specs=pl.BlockSpec(memory_space=pltpu.VMEM),
        scratch_shapes=[pltpu.SemaphoreType.DMA((7,)), pltpu.SemaphoreType.DMA((7,)), pltpu.SemaphoreType.DMA],
        name=name)(v)


def _chips(x, y):
    return [(1 - x, y), (x, 1 - y), (1 - x, 1 - y)]


def _gather_mats(shards, *, name):
    n = len(shards)

    def body(*refs):
        ins, outs = refs[:n], refs[n:2 * n]
        send_sems, recv_sems, fsend_sems, frecv_sems = refs[2 * n:]
        x, y, c = _place()
        s_me, sibling, chips = 2 * x + y, (x, y, 1 - c), _chips(x, y)

        def copy(src, dst, ssem, rsem, to):
            return pltpu.make_async_remote_copy(src_ref=src, dst_ref=dst, send_sem=ssem, recv_sem=rsem,
                                                device_id=to, device_id_type=MESH)

        first = [copy(ins[t].at[c], outs[t].at[s_me, c], send_sems.at[3 * t + k], recv_sems.at[3 * t + k], (*chip, c))
                 for t in range(n) for k, chip in enumerate(chips)]
        for cp in first:
            cp.start()
        passed = []
        for t in range(n):
            for k, chip in enumerate(chips):
                slot = outs[t].at[2 * chip[0] + chip[1], c]
                copy(ins[t].at[c], slot, send_sems.at[3 * t + k], recv_sems.at[3 * t + k], (*chip, c)).wait_recv()
                fw = copy(slot, slot, fsend_sems.at[3 * t + k], frecv_sems.at[3 * t + k], sibling)
                fw.start()
                passed.append(fw)
        for t in range(n):
            for k, chip in enumerate(chips):
                slot = outs[t].at[2 * chip[0] + chip[1], 1 - c]
                copy(slot, slot, fsend_sems.at[3 * t + k], frecv_sems.at[3 * t + k], sibling).wait_recv()
        for cp in first + passed:
            cp.wait_send()

    anyspec = pl.BlockSpec(memory_space=pl.ANY)
    return pl.pallas_call(
        body, out_shape=[SDS((4,) + s.shape, s.dtype) for s in shards], in_specs=[anyspec] * n, out_specs=[anyspec] * n,
        scratch_shapes=[pltpu.SemaphoreType.DMA((3 * n,))] * 4, name=name)(*shards)


def _rs_pair_exchange(gs, *, name):
    n = len(gs)

    def body(*refs):
        ins, outs = refs[:n], refs[n:2 * n]
        send_sems, recv_sems = refs[2 * n:]
        x, y, c = _place()
        cps = [pltpu.make_async_remote_copy(src_ref=ins[t].at[k, 1 - c], dst_ref=outs[t].at[k],
                                            send_sem=send_sems.at[4 * t + k], recv_sem=recv_sems.at[4 * t + k],
                                            device_id=(x, y, 1 - c), device_id_type=MESH)
               for t in range(n) for k in range(4)]
        for cp in cps:
            cp.start()
        for cp in cps:
            cp.wait()

    anyspec = pl.BlockSpec(memory_space=pl.ANY)
    return pl.pallas_call(
        body, out_shape=[SDS((4,) + g.shape[2:], g.dtype) for g in gs], in_specs=[anyspec] * n, out_specs=[anyspec] * n,
        scratch_shapes=[pltpu.SemaphoreType.DMA((4 * n,))] * 2, name=name)(*gs)


def _rs_chip_exchange(parts, *, name):
    n = len(parts)

    def body(*refs):
        ins, outs = refs[:n], refs[n:2 * n]
        send_sems, recv_sems, local_sems = refs[2 * n:]
        x, y, c = _place()
        s_me, chips = 2 * x + y, _chips(x, y)

        def copy(t, k, chip, src_slot, dst_slot):
            return pltpu.make_async_remote_copy(
                src_ref=ins[t].at[src_slot], dst_ref=outs[t].at[dst_slot], send_sem=send_sems.at[3 * t + k],
                recv_sem=recv_sems.at[3 * t + k], device_id=(*chip, c), device_id_type=MESH)

        mine = [pltpu.make_async_copy(ins[t].at[s_me], outs[t].at[s_me], local_sems.at[t]) for t in range(n)]
        for cp in mine:
            cp.start()
        sends = [copy(t, k, chip, 2 * chip[0] + chip[1], s_me) for t in range(n) for k, chip in enumerate(chips)]
        for cp in sends:
            cp.start()
        for t in range(n):
            for k, chip in enumerate(chips):
                copy(t, k, chip, s_me, 2 * chip[0] + chip[1]).wait_recv()
        for cp in sends:
            cp.wait_send()
        for cp in mine:
            cp.wait()

    anyspec = pl.BlockSpec(memory_space=pl.ANY)
    return pl.pallas_call(
        body, out_shape=[SDS(p.shape, p.dtype) for p in parts], in_specs=[anyspec] * n, out_specs=[anyspec] * n,
        scratch_shapes=[pltpu.SemaphoreType.DMA((3 * n,))] * 2 + [pltpu.SemaphoreType.DMA((n,))], name=name)(*parts)


def _rs_pair_share(fulls, *, name):
    n = len(fulls)

    def body(*refs):
        ins, outs = refs[:n], refs[n:2 * n]
        send_sems, recv_sems = refs[2 * n:]
        x, y, c = _place()

        def copy(t, half):
            return pltpu.make_async_remote_copy(src_ref=ins[t].at[c], dst_ref=outs[t].at[half], send_sem=send_sems.at[t],
                                                recv_sem=recv_sems.at[t], device_id=(x, y, 1 - c), device_id_type=MESH)

        sends = [copy(t, c) for t in range(n)]
        for cp in sends:
            cp.start()
        for t in range(n):
            copy(t, 1 - c).wait_recv()
        for cp in sends:
            cp.wait_send()

    anyspec = pl.BlockSpec(memory_space=pl.ANY)
    return pl.pallas_call(
        body, out_shape=[SDS(f.shape, f.dtype) for f in fulls], in_specs=[anyspec] * n, out_specs=[anyspec] * n,
        input_output_aliases={t: t for t in range(n)},
        scratch_shapes=[pltpu.SemaphoreType.DMA((n,))] * 2, name=name)(*fulls)


def _add_half(g, other, c1, *, out_dtype, name):
    _, _, h, cs = g.shape
    tr = _tile(h, 512, 16)

    def body(c_ref, g_ref, o_ref, out_ref):
        out_ref[...] = (g_ref[...] + o_ref[...]).astype(out_dtype)

    spec = pl.BlockSpec((None, tr, cs), lambda k, i, c: (k, i, 0))
    return pl.pallas_call(
        body, out_shape=SDS(other.shape, out_dtype),
        grid_spec=pltpu.PrefetchScalarGridSpec(
            num_scalar_prefetch=1, grid=(4, h // tr),
            in_specs=[pl.BlockSpec((None, None, tr, cs), lambda k, i, c: (k, c[0], i, 0)), spec], out_specs=spec),
        compiler_params=_params(("parallel", "parallel")), name=name)(c1, g, other)


def _sum4(x, c1, *, name):
    n, h, cs = x.shape
    tr = _tile(h, 512, 16)

    def body(c_ref, x_ref, o_ref):
        tot = x_ref[0].astype(F32)
        for s in range(1, n):
            tot = tot + x_ref[s].astype(F32)
        o_ref[...] = tot

    return pl.pallas_call(
        body, out_shape=SDS((2, h, cs), F32),
        grid_spec=pltpu.PrefetchScalarGridSpec(
            num_scalar_prefetch=1, grid=(h // tr,),
            in_specs=[pl.BlockSpec((n, tr, cs), lambda i, c: (0, i, 0))],
            out_specs=pl.BlockSpec((None, tr, cs), lambda i, c: (c[0], i, 0))),
        compiler_params=_params(("parallel",)), name=name)(c1, x)


def _finish_small(gathered, lbf, lbb, *, rows, name):
    r_lbf, r_lbb = rows['lb_f'], rows['lb_b']

    def body(g_ref, lbf_ref, lbb_ref, o_ref, dlf_ref, dlb_ref):
        tot = g_ref[0]
        for s in range(1, 8):
            tot = tot + g_ref[s]
        o_ref[...] = tot
        o_ref[0:1, :] = jnp.broadcast_to(jnp.sum(o_ref[0:1, :], axis=1, keepdims=True), (1, LANE))
        for lb_ref, d_ref, r0 in ((lbf_ref, dlf_ref, r_lbf), (lbb_ref, dlb_ref, r_lbb)):
            for hh in range(HG_HEADS):
                sl = slice(LANE * hh, LANE * (hh + 1))
                l0, l1 = lb_ref[0:1, sl], lb_ref[1:2, sl]
                mx = jnp.maximum(l0, l1)
                e0, e1 = jnp.exp(l0 - mx), jnp.exp(l1 - mx)
                p0 = e0 / (e0 + e1)
                d0 = o_ref[r0 + hh:r0 + hh + 1, :] * p0 * (1.0 - p0)
                d_ref[0:1, sl] = d0
                d_ref[1:2, sl] = -d0

    vm = pl.BlockSpec(memory_space=pltpu.VMEM)
    return pl.pallas_call(
        body, in_specs=[vm, vm, vm], out_specs=[vm, vm, vm],
        out_shape=[SDS(gathered.shape[1:], F32), SDS(lbf.shape, F32), SDS(lbb.shape, F32)], name=name)(gathered, lbf, lbb)


def _local_step(x2, tgt2, meta, W, S):
    T, D = x2.shape
    L = PAD + T
    h0 = jnp.concatenate([jnp.zeros((PAD - N_META, D), F32), meta, x2], axis=0)

    qk0 = Z_HG
    w_in = jnp.concatenate([W['w_in'][:, :qk0], _qk_to_group(W['w_in'][:, qk0:qk0 + AT_W + AT_KVW]),
                            W['w_in'][:, qk0 + AT_W + AT_KVW:]], axis=1)
    cc, ss = _rope_tables(L)
    wq_g, wk_g = _group_vec(S['q_norm']), _group_vec(S['k_norm'])

    def ffn_fwd(h, nw, wg, wu, wd, tag):
        n = _rmsnorm_fwd(h, nw, name=tag + "_norm")
        g, u, a = _ffn4_up(n, wg, wu, name=tag + "_up")
        hn = _ffn4_down(a, wd, h, name=tag + "_down")
        return hn, (n, g, u, a)

    def ffn_bwd(dh, h, nw, wg, wu, wd, saved, tag, split=False):
        n, g, u, a = saved
        dg, du = _ffn4_dact(dh, wd, g, u, name=tag + "_dact")
        dn = _ffn4_dn(dg, du, wg, wu, name=tag + "_dn")
        dwg = _ffn4_dw(n, dg, x_is_rows=True, name=tag + "_dwg")
        dwu = _ffn4_dw(n, du, x_is_rows=True, name=tag + "_dwu")
        dwd = _ffn4_dw(dh, a, x_is_rows=False, alpha=0.5, name=tag + "_dwd")
        *dhp, dnw = _rmsnorm_bwd(h, nw, dn, dh, split=split, name=tag + "_norm_bwd")
        return (dhp if split else dhp[0]), dnw, dwg, dwu, dwd

    h1, sv1 = ffn_fwd(h0, S['ffn1_norm'], W['ffn1_w_gate'], W['ffn1_w_up'], W['ffn1_w_down'], "ffn1")
    um = _rmsnorm_fwd(h1, S['mix_norm'], name="mix_norm")
    z = _mm([(um, w_in)], tm=512, tn=1792, tk=D, name="in_proj")
    of, sf = _hg_fwd(z, S['hg_lb_fwd'], rev=False, name="hg_fwd_f")
    ob, sb = _hg_fwd(z, S['hg_lb_bwd'], rev=True, name="hg_fwd_b")
    ya = _hg_post_fwd(of, ob, z, S['hg_out_norm'], name="hg_post")
    qm, qt, kr, krt, vb, vt = _at_prep(z, cc, ss, wq_g, wk_g, name="at_prep")
    yb, yb_f32, lse = _at_fwd(qt, kr, vt, name="at_fwd")
    mixed = _merge_fwd(ya, yb, W['w_up_a'], W['w_up_b'], z, name="merge")
    h2 = _mm([(mixed, W['w_out'])], res=h1, tm=512, tn=D, tk=D, name="out_proj")
    h3, sv2 = ffn_fwd(h2, S['ffn2_norm'], W['ffn2_w_gate'], W['ffn2_w_up'], W['ffn2_w_down'], "ffn2")
    dh3, loss_lanes = _loss_head(h3, tgt2, name="loss_head")

    G = {}
    dh2, dn_ffn2, G['ffn2_w_gate'], G['ffn2_w_up'], G['ffn2_w_down'] = ffn_bwd(
        dh3, h2, S['ffn2_norm'], W['ffn2_w_gate'], W['ffn2_w_up'], W['ffn2_w_down'], sv2, "ffn2")
    dpa, dpb, dzga, dzgb = _merge_bwd(dh2, W['w_out'], ya, yb, W['w_up_a'], W['w_up_b'], z, name="merge_bwd")
    G['w_out'] = _mm([(mixed, dh2)], ta=True, tm=D, tn=D, tk=512, name="d_w_out")
    dya = _mm([(dpa, W['w_up_a'])], tb=True, tm=512, tn=HG_W, tk=D, name="d_ya")
    dyb = _mm([(dpb, W['w_up_b'])], tb=True, tm=512, tn=AT_W, tk=D, name="d_yb")
    G['w_up_a'] = _mm([(ya, dpa)], ta=True, tm=HG_W, tn=D, tk=512, name="d_w_up_a")
    G['w_up_b'] = _mm([(yb, dpb)], ta=True, tm=AT_W, tn=D, tk=512, name="d_w_up_b")
    do_hg, dzg, d_hgn = _hg_post_bwd(dya, of, ob, z, S['hg_out_norm'], name="hg_post_bwd")
    dq_f, dv_f, dzf_f, dlb_f = _hg_bwd(z, S['hg_lb_fwd'], do_hg, sf, None, rev=False, name="hg_bwd_f")
    dzq, dzi, dzf_b, dlb_b = _hg_bwd(z, S['hg_lb_bwd'], do_hg, sb, (dq_f, dv_f), rev=True, name="hg_bwd_b")
    dqm, dk2, dv2 = _at_bwd(qm, qt, kr, krt, vb, dyb, yb_f32, lse, name="at_bwd")
    dz_at, dwq_g, dwk_g = _at_prep_bwd(dqm, dk2, dv2, z, cc, ss, wq_g, wk_g, name="at_prep_bwd")
    dz = jnp.concatenate([dzq, dzi, dzf_f, dzf_b, dzg, dz_at, dzga, dzgb], axis=1)
    dum = _mm([(dz, w_in)], tb=True, tm=512, tn=D, tk=1792, name="d_um")
    dw_in_p = _mm([(um, dz)], ta=True, tm=D, tn=1792, tk=512, name="d_w_in")
    G['w_in'] = jnp.concatenate([dw_in_p[:, :qk0], _qk_from_group(dw_in_p[:, qk0:qk0 + AT_W + AT_KVW]),
                                 dw_in_p[:, qk0 + AT_W + AT_KVW:]], axis=1)
    dh1, dn_mix = _rmsnorm_bwd(h1, S['mix_norm'], dum, dh2, name="mix_norm_bwd")
    (grad_x, dmeta), dn_ffn1, G['ffn1_w_gate'], G['ffn1_w_up'], G['ffn1_w_down'] = ffn_bwd(
        dh1, h0, S['ffn1_norm'], W['ffn1_w_gate'], W['ffn1_w_up'], W['ffn1_w_down'], sv1, "ffn1", split=True)

    small_rows = [('loss', loss_lanes), ('ffn1_norm', dn_ffn1.reshape(-1, LANE)), ('mix_norm', dn_mix.reshape(-1, LANE)),
                  ('ffn2_norm', dn_ffn2.reshape(-1, LANE)), ('hg_out_norm', d_hgn.reshape(-1, LANE)),
                  ('lb_f', dlb_f.reshape(-1, LANE)), ('lb_b', dlb_b.reshape(-1, LANE)), ('q_norm', dwq_g), ('k_norm', dwk_g)]
    return grad_x, dmeta, G, small_rows


def kernel(x, meta_tokens, ffn1_norm, ffn1_w_gate, ffn1_w_up, ffn1_w_down, mix_norm, w_in, hg_lb_fwd, hg_lb_bwd, hg_out_norm, q_norm, k_norm, w_up_a, w_up_b, w_out, ffn2_norm, ffn2_w_gate, ffn2_w_up, ffn2_w_down, loss_target, m_meta_tokens, m_ffn1_norm, m_ffn1_w_gate, m_ffn1_w_up, m_ffn1_w_down, m_mix_norm, m_w_in, m_hg_lb_fwd, m_hg_lb_bwd, m_hg_out_norm, m_q_norm, m_k_norm, m_w_up_a, m_w_up_b, m_w_out, m_ffn2_norm, m_ffn2_w_gate, m_ffn2_w_up, m_ffn2_w_down, v_meta_tokens, v_ffn1_norm, v_ffn1_w_gate, v_ffn1_w_up, v_ffn1_w_down, v_mix_norm, v_w_in, v_hg_lb_fwd, v_hg_lb_bwd, v_hg_out_norm, v_q_norm, v_k_norm, v_w_up_a, v_w_up_b, v_w_out, v_ffn2_norm, v_ffn2_w_gate, v_ffn2_w_up, v_ffn2_w_down):
    given = dict(locals())
    w = {n: given[n] for n in WEIGHTS}
    mom = {n: given["m_" + n] for n in WEIGHTS}
    var = {n: given["v_" + n] for n in WEIGHTS}
    c = lax.axis_index("c")
    D = x.shape[-1]

    shapes = {n: w[n].shape[-2:] for n in MATS + ('meta_tokens',)}
    halves = [w[n].astype(BF16).reshape(2, shapes[n][0] // 2, shapes[n][1]) for n in MATS]
    gathered = _gather_mats(halves, name="gather_weights")
    s_me = 2 * lax.axis_index("x") + lax.axis_index("y")
    W = {}
    for n, hv, g4 in zip(MATS, halves, gathered):
        r, cs = shapes[n]
        g4 = lax.dynamic_update_index_in_dim(g4, hv, s_me, 0).reshape(4, r, cs)
        if n in FFN_MATS:
            W[n] = g4
        elif n in ROW_SHARDED:
            W[n] = g4.reshape(4 * r, cs)
        else:
            W[n] = g4.transpose(1, 0, 2).reshape(r, 4 * cs)
    meta_rows = w['meta_tokens'].reshape(-1, LANE)
    mg = _allgather_small(meta_rows, name="gather_meta").reshape(4, 2, N_META, -1)[:, 0]
    meta = mg.transpose(1, 0, 2).reshape(N_META, D)
    S = {n: w[n] for n in SMALLS}

    grad_x, dmeta, G, small_rows = _local_step(x[0], loss_target[0], meta, W, S)
    G['meta_tokens'] = dmeta

    names = MATS + ('meta_tokens',)
    views = []
    for n in names:
        r, cs = shapes[n]
        if n in FFN_MATS:
            g4 = G[n]
        elif n in ROW_SHARDED:
            g4 = G[n].reshape(4, r, cs)
        else:
            g4 = G[n].reshape(r, 4, cs).transpose(1, 0, 2)
        views.append(g4.reshape(4, 2, r // 2, cs))
    c1 = c.astype(jnp.int32).reshape(1)
    from_sibling = _rs_pair_exchange(views, name="rs_pair_exchange")
    parts = [_add_half(v, o, c1, out_dtype=F32 if n == 'meta_tokens' else BF16, name="rs_pair_sum_" + n)
             for n, v, o in zip(names, views, from_sibling)]
    slabs = _rs_chip_exchange(parts, name="rs_chip_exchange")
    reds = [_sum4(s, c1, name="rs_chip_sum_" + n) for n, s in zip(names, slabs)]
    both = _rs_pair_share(reds, name="rs_pair_share")
    grads = {n: b.reshape(w[n].shape) for n, b in zip(names, both)}

    rows, off = {}, 0
    for nme, blk in small_rows:
        rows[nme] = off
        off += blk.shape[0]
    block = jnp.concatenate([blk for _, blk in small_rows], axis=0)
    n_rows = (off + 7) // 8 * 8
    block = jnp.pad(block, ((0, n_rows - off), (0, 0)))
    allsmall = _allgather_small(block, name="gather_small").reshape(8, n_rows, LANE)
    tot, d_lbf, d_lbb = _finish_small(allsmall, w['hg_lb_fwd'], w['hg_lb_bwd'], rows=rows, name="finish_small")
    loss = 0.5 * tot[0, 0] / D

    def small(nme, shape):
        r0 = rows[nme]
        return tot[r0:r0 + shape[-1] // LANE].reshape(shape)

    grads['ffn1_norm'] = small('ffn1_norm', w['ffn1_norm'].shape)
    grads['mix_norm'] = small('mix_norm', w['mix_norm'].shape)
    grads['ffn2_norm'] = small('ffn2_norm', w['ffn2_norm'].shape)
    grads['hg_out_norm'] = small('hg_out_norm', w['hg_out_norm'].shape)
    grads['hg_lb_fwd'] = d_lbf
    grads['hg_lb_bwd'] = d_lbb
    grads['q_norm'] = _ungroup_vec(tot[rows['q_norm']])
    grads['k_norm'] = _ungroup_vec(tot[rows['k_norm']])

    delta, new_m, new_v = {}, {}, {}
    for n in WEIGHTS:
        delta[n], new_m[n], new_v[n] = _adamw(w[n], grads[n], mom[n], var[n], name="adamw_" + n)
    return (loss, grad_x[None], *[grads[n] for n in WEIGHTS], *[delta[n] for n in WEIGHTS],
            *[new_m[n] for n in WEIGHTS], *[new_v[n] for n in WEIGHTS])
```

```python
import numpy as np
import jax
import jax.numpy as jnp
from jax import lax
from jax.experimental import pallas as pl
from jax.experimental.pallas import tpu as pltpu

F32 = jnp.float32
BF16 = jnp.bfloat16
SDS = jax.ShapeDtypeStruct
MESH = pl.DeviceIdType.MESH

EPS = 1e-6
N_META = 16
PAD = 512
LANE = 128
CHUNK = 128
HG_HEADS = 4
HG_W = HG_HEADS * 128
AT_HEADS = 8
AT_KV = 2
AT_HD = 64
AT_W = AT_HEADS * AT_HD
AT_KVW = AT_KV * AT_HD
VT_ROWS = AT_HD + 16
FWD_CHUNKS_PER_STEP = 4
GRID_W = 64
ROPE_THETA = 10000.0
Z_HG = 5 * HG_W
Z_AT = AT_W + 2 * AT_KVW
ADAM_LR, ADAM_B1, ADAM_B2, ADAM_EPS, ADAM_WD, ADAM_STEP = 0.001, 0.9, 0.999, 1e-08, 0.01, 10
VMEM_DEFAULT = 48 * 1024 * 1024
VMEM_LARGE = 60 * 1024 * 1024
NEG = -1e30

MATS = ('ffn1_w_gate', 'ffn1_w_up', 'ffn1_w_down', 'w_in', 'w_up_a', 'w_up_b', 'w_out',
        'ffn2_w_gate', 'ffn2_w_up', 'ffn2_w_down')
ROW_SHARDED = ('ffn1_w_down', 'w_out', 'ffn2_w_down')
FFN_MATS = ('ffn1_w_gate', 'ffn1_w_up', 'ffn1_w_down', 'ffn2_w_gate', 'ffn2_w_up', 'ffn2_w_down')
SMALLS = ('ffn1_norm', 'mix_norm', 'hg_lb_fwd', 'hg_lb_bwd', 'hg_out_norm', 'q_norm', 'k_norm', 'ffn2_norm')
WEIGHTS = ('meta_tokens', 'ffn1_norm', 'ffn1_w_gate', 'ffn1_w_up', 'ffn1_w_down', 'mix_norm', 'w_in', 'hg_lb_fwd',
           'hg_lb_bwd', 'hg_out_norm', 'q_norm', 'k_norm', 'w_up_a', 'w_up_b', 'w_out', 'ffn2_norm', 'ffn2_w_gate',
           'ffn2_w_up', 'ffn2_w_down')


def _params(sem=None, vmem=VMEM_DEFAULT):
    return pltpu.CompilerParams(dimension_semantics=sem, vmem_limit_bytes=vmem)


def _tile(n, pref, q=LANE):
    for d in range(min(pref, n), 0, -1):
        if n % d == 0 and d % q == 0:
            return d
    return n


def _sigmoid(x):
    return 1.0 / (1.0 + jnp.exp(-x))


def _dot(a, b, dims):
    return lax.dot_general(a, b, (dims, ((), ())), preferred_element_type=F32)


def _nn(a, b):
    return _dot(a, b, ((1,), (0,)))


def _nt(a, b):
    return _dot(a, b, ((1,), (1,)))


def _tn(a, b):
    return _dot(a, b, ((0,), (0,)))


def _split3(x):
    x1 = x.astype(BF16)
    r = x - x1.astype(F32)
    x2 = r.astype(BF16)
    x3 = (r - x2.astype(F32)).astype(BF16)
    return x1, x2, x3


def _exact_left(m01, x):
    x1, x2, x3 = _split3(x)
    return _nn(m01, x1) + _nn(m01, x2) + _nn(m01, x3)


def _exact_right(x, m01):
    x1, x2, x3 = _split3(x)
    return _nn(x1, m01) + _nn(x2, m01) + _nn(x3, m01)


def _mm(pairs, *, name, ta=False, tb=False, out_dtype=F32, tm=512, tn=1024, tk=1024, alpha=1.0, res=None):
    a0, b0 = pairs[0]
    M = a0.shape[1] if ta else a0.shape[0]
    K = a0.shape[0] if ta else a0.shape[1]
    N = b0.shape[0] if tb else b0.shape[1]
    tm, tn, tk = _tile(M, tm), _tile(N, tn), _tile(K, tk)
    nk = K // tk
    npair = len(pairs)
    dims = ((0 if ta else 1,), (1 if tb else 0,))

    def body(*refs):
        ab = refs[:2 * npair]
        pos = 2 * npair
        res_ref = None
        if res is not None:
            res_ref = refs[pos]
            pos += 1
        o_ref = refs[pos]

        def partial_sum():
            tot = None
            for p in range(npair):
                d = _dot(ab[2 * p][...].astype(BF16), ab[2 * p + 1][...].astype(BF16), dims)
                tot = d if tot is None else tot + d
            return tot

        def finish(acc):
            r = acc if alpha == 1.0 else acc * alpha
            if res_ref is not None:
                r = res_ref[...] + r
            o_ref[...] = r.astype(out_dtype)

        if nk == 1:
            finish(partial_sum())
        else:
            acc_ref = refs[pos + 1]
            k = pl.program_id(2)

            @pl.when(k == 0)
            def _():
                acc_ref[...] = jnp.zeros_like(acc_ref)

            acc_ref[...] += partial_sum()

            @pl.when(k == nk - 1)
            def _():
                finish(acc_ref[...])

    a_spec = pl.BlockSpec((tk, tm), lambda j, i, k: (k, i)) if ta else pl.BlockSpec((tm, tk), lambda j, i, k: (i, k))
    b_spec = pl.BlockSpec((tn, tk), lambda j, i, k: (j, k)) if tb else pl.BlockSpec((tk, tn), lambda j, i, k: (k, j))
    o_spec = pl.BlockSpec((tm, tn), lambda j, i, k: (i, j))
    in_specs, args = [], []
    for a, b in pairs:
        in_specs += [a_spec, b_spec]
        args += [a, b]
    if res is not None:
        in_specs.append(o_spec)
        args.append(res)
    return pl.pallas_call(
        body, grid=(N // tn, M // tm, nk), in_specs=in_specs, out_specs=o_spec,
        out_shape=SDS((M, N), out_dtype),
        scratch_shapes=[pltpu.VMEM((tm, tn), F32)] if nk > 1 else [],
        compiler_params=_params(("parallel", "parallel", "arbitrary")), name=name)(*args)


def _rmsnorm_fwd(h, w, *, name):
    L, D = h.shape
    tm = _tile(L, 512)

    def body(h_ref, w_ref, o_ref):
        x = h_ref[...]
        r = lax.rsqrt(jnp.mean(x * x, axis=-1, keepdims=True) + EPS)
        o_ref[...] = (x * r * w_ref[...]).astype(BF16)

    return pl.pallas_call(
        body, grid=(L // tm,),
        in_specs=[pl.BlockSpec((tm, D), lambda i: (i, 0)), pl.BlockSpec((1, D), lambda i: (0, 0))],
        out_specs=pl.BlockSpec((tm, D), lambda i: (i, 0)), out_shape=SDS((L, D), BF16),
        compiler_params=_params(("parallel",)), name=name)(h, w)


def _rmsnorm_bwd(h, w, dn, dres, *, split=False, name):
    L, D = h.shape
    tm = PAD if split else _tile(L, 512)

    def body(h_ref, w_ref, dn_ref, dres_ref, dh_ref, *rest):
        dw_ref = rest[-1]
        i = pl.program_id(0)
        x = h_ref[...]
        r = lax.rsqrt(jnp.mean(x * x, axis=-1, keepdims=True) + EPS)
        xh = x * r
        dn = dn_ref[...]
        dxh = dn * w_ref[...]
        dh = dres_ref[...] + r * (dxh - xh * jnp.mean(dxh * xh, axis=-1, keepdims=True))
        dh_ref[...] = dh

        @pl.when(i == 0)
        def _():
            dw_ref[...] = jnp.zeros_like(dw_ref)
            if split:
                rest[0][...] = dh[PAD - N_META:]

        dw_ref[...] += jnp.sum(dn * xh, axis=0, keepdims=True)

    row = pl.BlockSpec((tm, D), lambda i: (i, 0))
    vec = pl.BlockSpec((1, D), lambda i: (0, 0))
    if split:
        out_specs = [pl.BlockSpec((tm, D), lambda i: (jnp.maximum(i - 1, 0), 0)), pl.BlockSpec((N_META, D), lambda i: (0, 0)), vec]
        out_shape = [SDS((L - PAD, D), F32), SDS((N_META, D), F32), SDS((1, D), F32)]
    else:
        out_specs, out_shape = [row, vec], [SDS((L, D), F32), SDS((1, D), F32)]
    return pl.pallas_call(
        body, grid=(L // tm,), in_specs=[row, vec, row, row], out_specs=out_specs, out_shape=out_shape,
        compiler_params=_params(("arbitrary",)), name=name)(h, w, dn, dres)


def _ffn4_up(n, wg4, wu4, *, name):
    L, D = n.shape
    ns, _, cs = wg4.shape
    tm = _tile(L, 768)

    def body(n_ref, wg_ref, wu_ref, g_ref, u_ref, a_ref):
        x = n_ref[...]
        g = _nn(x, wg_ref[...])
        u = _nn(x, wu_ref[...])
        g_ref[...] = g.astype(BF16)
        u_ref[...] = u.astype(BF16)
        a_ref[...] = (g * _sigmoid(g) * u).astype(BF16)

    wspec = pl.BlockSpec((None, D, cs), lambda j, i: (j, 0, 0))
    ospec = pl.BlockSpec((None, tm, cs), lambda j, i: (j, i, 0))
    return pl.pallas_call(
        body, grid=(ns, L // tm),
        in_specs=[pl.BlockSpec((tm, D), lambda j, i: (i, 0)), wspec, wspec], out_specs=[ospec, ospec, ospec],
        out_shape=[SDS((ns, L, cs), BF16), SDS((ns, L, cs), BF16), SDS((ns, L, cs), BF16)],
        compiler_params=_params(("parallel", "parallel")), name=name)(n, wg4, wu4)


def _ffn4_down(a4, wd4, h, *, name):
    ns, L, cs = a4.shape
    D = wd4.shape[2]
    tm = _tile(L, 512)

    def body(a_ref, w_ref, h_ref, o_ref):
        acc = _nn(a_ref[0], w_ref[0])
        for j in range(1, ns):
            acc = acc + _nn(a_ref[j], w_ref[j])
        o_ref[...] = h_ref[...] + 0.5 * acc

    row = pl.BlockSpec((tm, D), lambda i: (i, 0))
    return pl.pallas_call(
        body, grid=(L // tm,),
        in_specs=[pl.BlockSpec((ns, tm, cs), lambda i: (0, i, 0)), pl.BlockSpec((ns, cs, D), lambda i: (0, 0, 0)), row],
        out_specs=row, out_shape=SDS((L, D), F32),
        compiler_params=_params(("parallel",)), name=name)(a4, wd4, h)


def _ffn4_dact(dh, wd4, g4, u4, *, name):
    L, D = dh.shape
    ns, cs, _ = wd4.shape
    tm = _tile(L, 768)

    def body(dh_ref, wd_ref, g_ref, u_ref, dg_ref, du_ref):
        da = 0.5 * _nt(dh_ref[...].astype(BF16), wd_ref[...])
        g = g_ref[...].astype(F32)
        sg = _sigmoid(g)
        dg_ref[...] = (da * u_ref[...].astype(F32) * (sg * (1.0 + g * (1.0 - sg)))).astype(BF16)
        du_ref[...] = (da * (g * sg)).astype(BF16)

    ospec = pl.BlockSpec((None, tm, cs), lambda j, i: (j, i, 0))
    return pl.pallas_call(
        body, grid=(ns, L // tm),
        in_specs=[pl.BlockSpec((tm, D), lambda j, i: (i, 0)), pl.BlockSpec((None, cs, D), lambda j, i: (j, 0, 0)), ospec, ospec],
        out_specs=[ospec, ospec], out_shape=[SDS((ns, L, cs), BF16), SDS((ns, L, cs), BF16)],
        compiler_params=_params(("parallel", "parallel")), name=name)(dh, wd4, g4, u4)


def _ffn4_dn(dg4, du4, wg4, wu4, *, name):
    ns, L, cs = dg4.shape
    D = wg4.shape[1]
    tm = _tile(L, 512)

    def body(dg_ref, du_ref, wg_ref, wu_ref, o_ref):
        acc = None
        for j in range(ns):
            t = _nt(dg_ref[j], wg_ref[j]) + _nt(du_ref[j], wu_ref[j])
            acc = t if acc is None else acc + t
        o_ref[...] = acc

    aspec = pl.BlockSpec((ns, tm, cs), lambda i: (0, i, 0))
    wspec = pl.BlockSpec((ns, D, cs), lambda i: (0, 0, 0))
    return pl.pallas_call(
        body, grid=(L // tm,), in_specs=[aspec, aspec, wspec, wspec],
        out_specs=pl.BlockSpec((tm, D), lambda i: (i, 0)), out_shape=SDS((L, D), F32),
        compiler_params=_params(("parallel",), VMEM_LARGE), name=name)(dg4, du4, wg4, wu4)


def _ffn4_dw(x, y4, *, x_is_rows, alpha=1.0, name):
    L, D = x.shape
    ns, _, cs = y4.shape
    tk = _tile(L, 512)
    nk = L // tk
    oshape = (D, cs) if x_is_rows else (cs, D)

    def body(x_ref, y_ref, o_ref):
        k = pl.program_id(0)

        @pl.when(k == 0)
        def _():
            o_ref[...] = jnp.zeros_like(o_ref)

        xb = x_ref[...].astype(BF16)
        if x_is_rows:
            xt = xb.T
            for j in range(ns):
                o_ref[j] += _nn(xt, y_ref[j])
        else:
            for j in range(ns):
                o_ref[j] += _tn(y_ref[j], xb)

        if alpha != 1.0:
            @pl.when(k == nk - 1)
            def _():
                o_ref[...] = o_ref[...] * alpha

    return pl.pallas_call(
        body, grid=(nk,),
        in_specs=[pl.BlockSpec((tk, D), lambda k: (k, 0)), pl.BlockSpec((ns, tk, cs), lambda k: (0, k, 0))],
        out_specs=pl.BlockSpec((ns,) + oshape, lambda k: (0, 0, 0)), out_shape=SDS((ns,) + oshape, F32),
        compiler_params=_params(("arbitrary",)), name=name)(x, y4)


def _hg_masks(rev):
    t = lax.broadcasted_iota(jnp.int32, (CHUNK, CHUNK), 0)
    s = lax.broadcasted_iota(jnp.int32, (CHUNK, CHUNK), 1)
    causal = (s >= t) if rev else (s <= t)
    levels = []
    for sh in (6, 5, 4):
        same = jnp.right_shift(t, sh + 1) == jnp.right_shift(s, sh + 1)
        tq = jnp.bitwise_and(jnp.right_shift(t, sh), 1)
        sk = jnp.bitwise_and(jnp.right_shift(s, sh), 1)
        levels.append(same & (tq == (0 if rev else 1)) & (sk == (1 if rev else 0)))
    diag = (jnp.right_shift(t, 4) == jnp.right_shift(s, 4)) & causal
    return causal, levels, diag


def _hg_intra_factors(q, k, b, b_scr, rev):
    b_scr[...] = b
    row = lax.broadcasted_iota(jnp.int32, (CHUNK, LANE), 0)
    out = []
    for sh in (6, 5, 4):
        lb = 1 << sh
        pieces = []
        for p in range(0, CHUNK, 2 * lb):
            r = p + lb if rev else p + lb - 1
            pieces.append(jnp.broadcast_to(b_scr[pl.ds(r, 1), :], (2 * lb, LANE)))
        ref = pieces[0] if len(pieces) == 1 else jnp.concatenate(pieces, axis=0)
        qside = jnp.bitwise_and(jnp.right_shift(row, sh), 1) == (0 if rev else 1)
        d = b - ref
        e = jnp.exp(jnp.minimum(jnp.where(qside, d, -d), 0.0))
        eq = jnp.where(qside, e, 0.0)
        ek = jnp.where(qside, 0.0, e)
        out.append((eq, ek, (q * eq).astype(BF16), (k * ek).astype(BF16)))
    pieces = []
    for a in range(0, CHUNK, 16):
        r = a + (8 if rev else 7)
        pieces.append(jnp.broadcast_to(b_scr[pl.ds(r, 1), :], (16, LANE)))
    ref = jnp.concatenate(pieces, axis=0)
    eq = jnp.exp(jnp.minimum(b - ref, 80.0))
    ek = jnp.exp(jnp.minimum(ref - b, 80.0))
    out.append((eq, ek, (q * eq).astype(BF16), (k * ek).astype(BF16)))
    return out


def _hg_gate(zf, l0, l1, valid):
    mx = jnp.maximum(l0, l1)
    e0, e1 = jnp.exp(l0 - mx), jnp.exp(l1 - mx)
    p0 = e0 / (e0 + e1)
    sg = _sigmoid(-zf)
    k = jnp.where(valid, (1.0 - p0) * sg, 0.0)
    return p0, sg, k, jnp.log(1.0 - k)


def _hg_fwd(z, lbp, *, rev, name):
    L = z.shape[0]
    nc = L // CHUNK
    fcol = 3 if rev else 2

    def cidx(j):
        return nc - 1 - j if rev else j

    def body(zq_ref, zi_ref, zf_ref, lb_ref, o_ref, ssave_ref, st_scr, b_scr):
        j = pl.program_id(0)

        @pl.when(j == 0)
        def _():
            st_scr[...] = jnp.zeros_like(st_scr)

        causal, lmasks, dmask = _hg_masks(rev)
        tri = jnp.where(causal, 1.0, 0.0).astype(BF16)
        rowg = cidx(j) * CHUNK + lax.broadcasted_iota(jnp.int32, (CHUNK, LANE), 0)
        valid = rowg >= PAD - N_META
        last = 0 if rev else CHUNK - 1
        for hh in range(HG_HEADS):
            sl = slice(LANE * hh, LANE * (hh + 1))
            zq = zq_ref[:, sl]
            q = zq * _sigmoid(zq)
            v = zi_ref[:, sl].astype(BF16)
            _, _, k, g = _hg_gate(zf_ref[:, sl], lb_ref[0:1, sl], lb_ref[1:2, sl], valid)
            b = _exact_left(tri, g)
            st = st_scr[hh]
            ssave_ref[0, hh] = st
            o = _nt((q * jnp.exp(b)).astype(BF16), st.astype(BF16))
            a = None
            fac = _hg_intra_factors(q, k, b, b_scr, rev)
            for (eq, ek, qq, kk), msk in zip(fac, lmasks + [dmask]):
                t = jnp.where(msk, _nt(qq, kk), 0.0)
                a = t if a is None else a + t
            o_ref[:, sl] = o + _nn(a.astype(BF16), v)
            bl = b_scr[pl.ds(last, 1), :]
            kd = (k * jnp.exp(bl - b)).astype(BF16)
            st_scr[hh] = st * jnp.exp(bl) + _tn(v, kd)

    zspec = lambda col: pl.BlockSpec((CHUNK, HG_W), lambda j: (cidx(j), col))
    return pl.pallas_call(
        body, grid=(nc,),
        in_specs=[zspec(0), zspec(1), zspec(fcol), pl.BlockSpec((2, HG_W), lambda j: (0, 0))],
        out_specs=[pl.BlockSpec((CHUNK, HG_W), lambda j: (cidx(j), 0)),
                   pl.BlockSpec((1, HG_HEADS, LANE, LANE), lambda j: (cidx(j), 0, 0, 0))],
        out_shape=[SDS((L, HG_W), F32), SDS((nc, HG_HEADS, LANE, LANE), F32)],
        scratch_shapes=[pltpu.VMEM((HG_HEADS, LANE, LANE), F32), pltpu.VMEM((CHUNK, LANE), F32)],
        compiler_params=_params(("arbitrary",)), name=name)(z, z, z, lbp)


def _hg_bwd(z, lbp, do, ssave, prev, *, rev, name):
    L = z.shape[0]
    nc = L // CHUNK
    fcol = 3 if rev else 2
    final = prev is not None

    def cidx(j):
        return j if rev else nc - 1 - j

    def body(*refs):
        zq_ref, zi_ref, zf_ref, lb_ref, do_ref, ss_ref = refs[:6]
        pos = 6
        if final:
            dqin_ref, dvin_ref = refs[6:8]
            pos = 8
        dq_ref, dv_ref, dzf_ref, dlb_ref, dst_scr, b_scr = refs[pos:pos + 6]
        j = pl.program_id(0)

        @pl.when(j == 0)
        def _():
            dst_scr[...] = jnp.zeros_like(dst_scr)
            dlb_ref[...] = jnp.zeros_like(dlb_ref)

        causal, lmasks, dmask = _hg_masks(rev)
        tri = jnp.where(causal, 1.0, 0.0).astype(BF16)
        ti = lax.broadcasted_iota(jnp.int32, (CHUNK, CHUNK), 0)
        si = lax.broadcasted_iota(jnp.int32, (CHUNK, CHUNK), 1)
        tri_t = jnp.where((si <= ti) if rev else (si >= ti), 1.0, 0.0).astype(BF16)
        rowg = cidx(j) * CHUNK + lax.broadcasted_iota(jnp.int32, (CHUNK, LANE), 0)
        valid = rowg >= PAD - N_META
        last = 0 if rev else CHUNK - 1
        for hh in range(HG_HEADS):
            sl = slice(LANE * hh, LANE * (hh + 1))
            zq = zq_ref[:, sl]
            sq = _sigmoid(zq)
            q = zq * sq
            v = zi_ref[:, sl].astype(BF16)
            p0, sg, k, g = _hg_gate(zf_ref[:, sl], lb_ref[0:1, sl], lb_ref[1:2, sl], valid)
            b = _exact_left(tri, g)
            dob = do_ref[:, sl].astype(BF16)
            st = ss_ref[0, hh]
            dst = dst_scr[hh]
            stb, dstb = st.astype(BF16), dst.astype(BF16)
            eb = jnp.exp(b)
            qe = (q * eb).astype(BF16)
            fac = _hg_intra_factors(q, k, b, b_scr, rev)
            bl = b_scr[pl.ds(last, 1), :]
            ebl = jnp.exp(bl)
            kde = jnp.exp(bl - b)
            kd = (k * kde).astype(BF16)
            da = jnp.where(causal, _nt(dob, v), 0.0)
            dq = eb * _nn(dob, stb)
            dk_inter = kde * _nn(v, dstb)
            dk = dk_inter
            dv = _nt(kd, dstb)
            a = None
            db = q * dq - k * dk
            for (eq, ek, qq, kk), msk in zip(fac, lmasks + [dmask]):
                t = jnp.where(msk, _nt(qq, kk), 0.0)
                a = t if a is None else a + t
                dal = jnp.where(msk, da, 0.0).astype(BF16)
                mq = _nn(dal, kk)
                mk = _tn(dal, qq)
                dq = dq + eq * mq
                dk = dk + ek * mk
                db = db + (qq.astype(F32) * mq - kk.astype(F32) * mk)
            dv = dv + _tn(a.astype(BF16), dob)
            extra = ebl * jnp.sum(st * dst, axis=0, keepdims=True) + jnp.sum(k * dk_inter, axis=0, keepdims=True)
            dst_scr[hh] = dst * ebl + _tn(dob, qe)
            dg = _exact_left(tri_t, db) + extra
            dk_tot = dk - dg / (1.0 - k)
            dzf_ref[:, sl] = jnp.where(valid, dk_tot * (1.0 - p0) * (-sg * (1.0 - sg)), 0.0).astype(BF16)
            dlb_ref[:, sl] += jnp.sum(jnp.where(valid, -sg * dk_tot, 0.0), axis=0, keepdims=True)
            if final:
                dq_ref[:, sl] = ((dq + dqin_ref[:, sl]) * (sq * (1.0 + zq * (1.0 - sq)))).astype(BF16)
                dv_ref[:, sl] = (dv + dvin_ref[:, sl]).astype(BF16)
            else:
                dq_ref[:, sl] = dq
                dv_ref[:, sl] = dv

    zspec = lambda col: pl.BlockSpec((CHUNK, HG_W), lambda j: (cidx(j), col))
    rspec = pl.BlockSpec((CHUNK, HG_W), lambda j: (cidx(j), 0))
    in_specs = [zspec(0), zspec(1), zspec(fcol), pl.BlockSpec((2, HG_W), lambda j: (0, 0)), rspec,
                pl.BlockSpec((1, HG_HEADS, LANE, LANE), lambda j: (cidx(j), 0, 0, 0))]
    args = [z, z, z, lbp, do, ssave]
    if final:
        in_specs += [rspec, rspec]
        args += list(prev)
    odt = BF16 if final else F32
    return pl.pallas_call(
        body, grid=(nc,), in_specs=in_specs,
        out_specs=[rspec, rspec, rspec, pl.BlockSpec((1, HG_W), lambda j: (0, 0))],
        out_shape=[SDS((L, HG_W), odt), SDS((L, HG_W), odt), SDS((L, HG_W), BF16), SDS((1, HG_W), F32)],
        scratch_shapes=[pltpu.VMEM((HG_HEADS, LANE, LANE), F32), pltpu.VMEM((CHUNK, LANE), F32)],
        compiler_params=_params(("arbitrary",)), name=name)(*args)


def _hg_post_fwd(of, ob, z, w, *, name):
    L = of.shape[0]
    tm = _tile(L, 512)

    def body(of_ref, ob_ref, zg_ref, w_ref, y_ref):
        for hh in range(HG_HEADS):
            sl = slice(LANE * hh, LANE * (hh + 1))
            o = of_ref[:, sl] + ob_ref[:, sl]
            r = lax.rsqrt(jnp.mean(o * o, axis=-1, keepdims=True) + EPS)
            zg = zg_ref[:, sl]
            y_ref[:, sl] = (o * r * w_ref[:, sl] * (zg * _sigmoid(zg))).astype(BF16)

    row = pl.BlockSpec((tm, HG_W), lambda i: (i, 0))
    return pl.pallas_call(
        body, grid=(L // tm,),
        in_specs=[row, row, pl.BlockSpec((tm, HG_W), lambda i: (i, 4)), pl.BlockSpec((1, HG_W), lambda i: (0, 0))],
        out_specs=row, out_shape=SDS((L, HG_W), BF16),
        compiler_params=_params(("parallel",)), name=name)(of, ob, z, w)


def _hg_post_bwd(dy, of, ob, z, w, *, name):
    L = of.shape[0]
    tm = _tile(L, 512)

    def body(dy_ref, of_ref, ob_ref, zg_ref, w_ref, do_ref, dzg_ref, dw_ref):
        @pl.when(pl.program_id(0) == 0)
        def _():
            dw_ref[...] = jnp.zeros_like(dw_ref)

        for hh in range(HG_HEADS):
            sl = slice(LANE * hh, LANE * (hh + 1))
            o = of_ref[:, sl] + ob_ref[:, sl]
            r = lax.rsqrt(jnp.mean(o * o, axis=-1, keepdims=True) + EPS)
            xh = o * r
            zg = zg_ref[:, sl]
            sg = _sigmoid(zg)
            w = w_ref[:, sl]
            dy = dy_ref[:, sl]
            dys = dy * (zg * sg)
            dzg_ref[:, sl] = (dy * xh * w * (sg * (1.0 + zg * (1.0 - sg)))).astype(BF16)
            dw_ref[:, sl] += jnp.sum(dys * xh, axis=0, keepdims=True)
            dxh = dys * w
            do_ref[:, sl] = r * (dxh - xh * jnp.mean(dxh * xh, axis=-1, keepdims=True))

    row = pl.BlockSpec((tm, HG_W), lambda i: (i, 0))
    vec = pl.BlockSpec((1, HG_W), lambda i: (0, 0))
    return pl.pallas_call(
        body, grid=(L // tm,),
        in_specs=[row, row, row, pl.BlockSpec((tm, HG_W), lambda i: (i, 4)), vec],
        out_specs=[row, row, vec],
        out_shape=[SDS((L, HG_W), F32), SDS((L, HG_W), BF16), SDS((1, HG_W), F32)],
        compiler_params=_params(("arbitrary",)), name=name)(dy, of, ob, z, w)


N_GROUPS = (AT_HEADS + AT_KV) // 2


def _qk_to_group(wqk):
    d = wqk.shape[0]
    return wqk.reshape(d, N_GROUPS, 2, AT_HD // 2, 2).transpose(0, 1, 4, 2, 3).reshape(d, N_GROUPS * LANE)


def _qk_from_group(wqk):
    d = wqk.shape[0]
    return wqk.reshape(d, N_GROUPS, 2, 2, AT_HD // 2).transpose(0, 1, 3, 4, 2).reshape(d, N_GROUPS * LANE)


def _group_vec(w64):
    halves = w64.reshape(AT_HD // 2, 2).T
    return jnp.broadcast_to(halves[:, None, :], (2, 2, AT_HD // 2)).reshape(1, LANE)


def _ungroup_vec(w128):
    w = w128.reshape(2, 2, 32).sum(axis=1)
    return w.T.reshape(1, AT_HD)


def _rope_tables(L):
    n_real = L - PAD
    t = np.arange(n_real)
    row = np.concatenate([np.zeros(PAD), t // GRID_W]).astype(np.float32)
    col = np.concatenate([np.zeros(PAD), t % GRID_W]).astype(np.float32)
    inv = jnp.asarray(ROPE_THETA, F32) ** (-jnp.arange(0, AT_HD // 2, 2, dtype=F32) / (AT_HD // 2))
    ang = jnp.concatenate([jnp.asarray(row)[:, None] * inv, jnp.asarray(col)[:, None] * inv], axis=-1)
    cos, sin = jnp.cos(ang), jnp.sin(ang)
    cc = jnp.tile(cos, (1, 4))
    ss = jnp.concatenate([-sin, -sin, sin, sin], axis=1)
    return cc, ss


def _seg_matrix():
    a = lax.broadcasted_iota(jnp.int32, (LANE, LANE), 0)
    b = lax.broadcasted_iota(jnp.int32, (LANE, LANE), 1)
    same = jnp.bitwise_and(jnp.right_shift(a, 5), 1) == jnp.bitwise_and(jnp.right_shift(b, 5), 1)
    return jnp.where(same, 1.0, 0.0).astype(BF16)


def _slot_mask(shape, hp):
    lane = lax.broadcasted_iota(jnp.int32, shape, 1)
    return jnp.bitwise_and(jnp.right_shift(lane, 5), 1) == hp


def _at_prep(z, cc, ss, wq, wk, *, name):
    L = z.shape[0]
    tm = PAD
    qcol = Z_HG // AT_W
    kvcol = (Z_HG + AT_W) // (2 * LANE)

    def body(zq_ref, zkv_ref, cc_ref, ss_ref, wq_ref, wk_ref, qm_ref, qt_ref, kr_ref, krt_ref, vb_ref, vt_ref):
        seg = _seg_matrix()
        cc, ss = cc_ref[...], ss_ref[...]

        def normrope(x, w):
            r = lax.rsqrt(_exact_right(x * x, seg) * (1.0 / AT_HD) + EPS)
            y = x * r * w
            return y * cc + pltpu.roll(y, 64, 1) * ss

        for g in range(AT_HEADS // 2):
            o = normrope(zq_ref[:, LANE * g:LANE * (g + 1)], wq_ref[...]) * (AT_HD ** -0.5)
            for hp in range(2):
                h = 2 * g + hp
                tgt = h // (AT_HEADS // AT_KV)
                xm = jnp.where(_slot_mask(o.shape, hp), o, 0.0)
                if tgt != hp:
                    xm = pltpu.roll(xm, 32 if tgt == 1 else 96, 1)
                qm_ref[h] = xm.astype(BF16)
                qt_ref[h] = xm.T.astype(BF16)
        kr = normrope(zkv_ref[:, :LANE], wk_ref[...])
        kr_ref[...] = kr.astype(BF16)
        krt_ref[0] = kr.T.astype(BF16)
        v = zkv_ref[:, LANE:]
        low = lax.broadcasted_iota(jnp.int32, v.shape, 1) < AT_HD
        vb_ref[0] = jnp.where(low, v, 0.0).astype(BF16)
        vb_ref[1] = jnp.where(low, pltpu.roll(v, AT_HD, 1), 0.0).astype(BF16)
        vt = v.T.astype(BF16)
        ones = jnp.ones((VT_ROWS - AT_HD, tm), BF16)
        for j in range(AT_KV):
            vt_ref[j, 0, 0:AT_HD, :] = vt[AT_HD * j:AT_HD * (j + 1)]
            vt_ref[j, 0, AT_HD:VT_ROWS, :] = ones

    tab = pl.BlockSpec((tm, LANE), lambda i: (i, 0))
    vec = pl.BlockSpec((1, LANE), lambda i: (0, 0))
    nt = L // tm
    return pl.pallas_call(
        body, grid=(nt,),
        in_specs=[pl.BlockSpec((tm, AT_W), lambda i: (i, qcol)), pl.BlockSpec((tm, 2 * LANE), lambda i: (i, kvcol)),
                  tab, tab, vec, vec],
        out_specs=[pl.BlockSpec((AT_HEADS, tm, LANE), lambda i: (0, i, 0)),
                   pl.BlockSpec((AT_HEADS, LANE, tm), lambda i: (0, 0, i)), tab,
                   pl.BlockSpec((1, LANE, tm), lambda i: (i, 0, 0)),
                   pl.BlockSpec((AT_KV, tm, LANE), lambda i: (0, i, 0)),
                   pl.BlockSpec((AT_KV, 1, VT_ROWS, tm), lambda i: (0, i, 0, 0))],
        out_shape=[SDS((AT_HEADS, L, LANE), BF16), SDS((AT_HEADS, LANE, L), BF16), SDS((L, LANE), BF16),
                   SDS((nt, LANE, tm), BF16), SDS((AT_KV, L, LANE), BF16), SDS((AT_KV, nt, VT_ROWS, tm), BF16)],
        compiler_params=_params(("parallel",)), name=name)(z, z, cc, ss, wq, wk)


def _at_prep_bwd(dqm, dk2, dv2, z, cc, ss, wq, wk, *, name):
    L = z.shape[0]
    tm = PAD
    qcol = Z_HG // AT_W
    kvcol = (Z_HG + AT_W) // (2 * LANE)

    def body(dqm_ref, dk2_ref, dv2_ref, zq_ref, zkv_ref, cc_ref, ss_ref, wq_ref, wk_ref, dz_ref, dwq_ref, dwk_ref):
        @pl.when(pl.program_id(0) == 0)
        def _():
            dwq_ref[...] = jnp.zeros_like(dwq_ref)
            dwk_ref[...] = jnp.zeros_like(dwk_ref)

        seg = _seg_matrix()
        cc, ss = cc_ref[...], ss_ref[...]

        def back(x, w, do):
            dy = do * cc + pltpu.roll(do * ss, 64, 1)
            r = lax.rsqrt(_exact_right(x * x, seg) * (1.0 / AT_HD) + EPS)
            xh = x * r
            dxh = dy * w
            dx = r * (dxh - xh * (_exact_right(dxh * xh, seg) * (1.0 / AT_HD)))
            return dx, jnp.sum(dy * xh, axis=0, keepdims=True)

        for g in range(AT_HEADS // 2):
            do = None
            for hp in range(2):
                h = 2 * g + hp
                tgt = h // (AT_HEADS // AT_KV)
                d = jnp.where(_slot_mask((tm, LANE), tgt), dqm_ref[h], 0.0)
                if tgt != hp:
                    d = pltpu.roll(d, 96 if tgt == 1 else 32, 1)
                do = d if do is None else do + d
            dx, dw = back(zq_ref[:, LANE * g:LANE * (g + 1)], wq_ref[...], do * (AT_HD ** -0.5))
            dz_ref[:, LANE * g:LANE * (g + 1)] = dx.astype(BF16)
            dwq_ref[...] += dw
        dx, dw = back(zkv_ref[:, :LANE], wk_ref[...], dk2_ref[0] + dk2_ref[1])
        dz_ref[:, AT_W:AT_W + LANE] = dx.astype(BF16)
        dwk_ref[...] += dw
        dv0 = dv2_ref[0]
        low = lax.broadcasted_iota(jnp.int32, dv0.shape, 1) < AT_HD
        dz_ref[:, AT_W + LANE:] = jnp.where(low, dv0, pltpu.roll(dv2_ref[1], AT_HD, 1)).astype(BF16)

    tab = pl.BlockSpec((tm, LANE), lambda i: (i, 0))
    vec = pl.BlockSpec((1, LANE), lambda i: (0, 0))
    two = pl.BlockSpec((AT_KV, tm, LANE), lambda i: (0, i, 0))
    return pl.pallas_call(
        body, grid=(L // tm,),
        in_specs=[pl.BlockSpec((AT_HEADS, tm, LANE), lambda i: (0, i, 0)), two, two,
                  pl.BlockSpec((tm, AT_W), lambda i: (i, qcol)), pl.BlockSpec((tm, 2 * LANE), lambda i: (i, kvcol)),
                  tab, tab, vec, vec],
        out_specs=[pl.BlockSpec((tm, Z_AT), lambda i: (i, 0)), vec, vec],
        out_shape=[SDS((L, Z_AT), BF16), SDS((1, LANE), F32), SDS((1, LANE), F32)],
        compiler_params=_params(("arbitrary",)), name=name)(dqm, dk2, dv2, z, z, cc, ss, wq, wk)


def _at_fwd(qt, kr, vt, *, name):
    L = kr.shape[0]
    G = AT_HEADS // AT_KV
    tq = _tile(L, 384)
    tk = PAD
    nk = L // tk
    R = G * tq
    per = FWD_CHUNKS_PER_STEP if (nk - 1) % FWD_CHUNKS_PER_STEP == 0 else 1

    def body(q_ref, k_ref, v_ref, ob_ref, of_ref, lse_ref, m_scr, acc_scr):
        i = pl.program_id(1)
        qt = jnp.concatenate([q_ref[g] for g in range(G)], axis=1)
        m_scr[...] = jnp.full_like(m_scr, NEG)
        acc_scr[...] = jnp.zeros_like(acc_scr)

        def chunks(c, n, masked):
            start = c * tk if isinstance(c, int) else pl.multiple_of(c * tk, tk)
            st = _nn(k_ref[pl.ds(start, n * tk), :], qt).astype(BF16)
            if masked:
                key = lax.broadcasted_iota(jnp.int32, st.shape, 0)
                st = jnp.where(key >= PAD - N_META, st, NEG)
            m_prev = m_scr[...]
            m_new = jnp.maximum(m_prev, jnp.max(st, axis=0, keepdims=True).astype(F32))
            pt = jnp.exp(st - m_new.astype(BF16))
            acc = jnp.exp(m_prev - m_new) * acc_scr[...]
            for u in range(n):
                acc = acc + _nn(v_ref[0, c + u], pt[u * tk:(u + 1) * tk])
            acc_scr[...] = acc
            m_scr[...] = m_new

        chunks(0, 1, True)

        def loop(t, carry):
            chunks(1 + per * t, per, False)
            return carry

        lax.fori_loop(0, (nk - 1) // per, loop, 0)
        l = acc_scr[pl.ds(AT_HD, 1), :]
        lse = m_scr[...] + jnp.log(l)
        on = acc_scr[0:AT_HD, :] / l
        o = jnp.concatenate([on[:, g * tq:(g + 1) * tq] for g in range(G)], axis=0).T
        rowg = i * tq + lax.broadcasted_iota(jnp.int32, o.shape, 0)
        o = jnp.where(rowg >= PAD - N_META, o, 0.0)
        ob_ref[...] = o.astype(BF16)
        of_ref[...] = o
        for g in range(G):
            lse_ref[g] = lse[:, g * tq:(g + 1) * tq]

    ospec = pl.BlockSpec((tq, G * AT_HD), lambda j, i: (i, j))
    return pl.pallas_call(
        body, grid=(AT_KV, L // tq),
        in_specs=[pl.BlockSpec((G, LANE, tq), lambda j, i: (j, 0, i)), pl.BlockSpec((L, LANE), lambda j, i: (0, 0)),
                  pl.BlockSpec((1, nk, VT_ROWS, tk), lambda j, i: (j, 0, 0, 0))],
        out_specs=[ospec, ospec, pl.BlockSpec((G, 1, tq), lambda j, i: (j, 0, i))],
        out_shape=[SDS((L, AT_W), BF16), SDS((L, AT_W), F32), SDS((AT_HEADS, 1, L), F32)],
        scratch_shapes=[pltpu.VMEM((1, R), F32), pltpu.VMEM((VT_ROWS, R), F32)],
        compiler_params=_params(("parallel", "parallel")), name=name)(qt, kr, vt)


def _at_bwd(qm, qt, kr, krt, vb, do, of, lse, *, name):
    L = kr.shape[0]
    G = AT_HEADS // AT_KV
    tq = _tile(L, 384)
    tk = PAD
    nk = L // tk
    nq = L // tq
    R = G * tq

    def body(qm_ref, q_ref, k_hbm, kt_hbm, v_hbm, do_ref, o_ref, lse_ref, dq_ref, dk_hbm, dv_hbm,
             k_scr, kt_scr, v_scr, dk_scr, dv_scr, dq_scr, sem):
        j, i = pl.program_id(0), pl.program_id(1)

        @pl.when(i == 0)
        def _():
            cps = [pltpu.make_async_copy(k_hbm, k_scr, sem.at[0]), pltpu.make_async_copy(kt_hbm, kt_scr, sem.at[1]),
                   pltpu.make_async_copy(v_hbm.at[j], v_scr, sem.at[2])]
            for cp in cps:
                cp.start()
            dk_scr[...] = jnp.zeros_like(dk_scr)
            dv_scr[...] = jnp.zeros_like(dv_scr)
            for cp in cps:
                cp.wait()

        qt = jnp.concatenate([q_ref[g] for g in range(G)], axis=1)
        rowg = i * tq + lax.broadcasted_iota(jnp.int32, (tq, G * AT_HD), 0)
        dot_all = jnp.where(rowg >= PAD - N_META, do_ref[...], 0.0).T
        ot_all = o_ref[...].T
        dot = jnp.concatenate([dot_all[AT_HD * g:AT_HD * (g + 1)] for g in range(G)], axis=1)
        ot = jnp.concatenate([ot_all[AT_HD * g:AT_HD * (g + 1)] for g in range(G)], axis=1)
        delta = jnp.sum(dot * ot, axis=0, keepdims=True)
        dot128 = jnp.concatenate([dot, jnp.zeros_like(dot)], axis=0)
        dor = dot128.T.astype(BF16)
        dot128 = dot128.astype(BF16)
        qr = qm_ref[...].reshape(R, LANE)
        lse_v = jnp.concatenate([lse_ref[g] for g in range(G)], axis=1)
        dq_scr[...] = jnp.zeros_like(dq_scr)

        def chunk(c, masked):
            start = c * tk if isinstance(c, int) else pl.multiple_of(c * tk, tk)
            k = k_scr[pl.ds(start, tk), :]
            kt = kt_scr[c]
            v = v_scr[pl.ds(start, tk), :]
            st = _nn(k, qt)
            if masked:
                key = lax.broadcasted_iota(jnp.int32, st.shape, 0)
                st = jnp.where(key >= PAD - N_META, st, NEG)
            pt = jnp.exp(st - lse_v)
            dst = (pt * (_nn(v, dot128) - delta)).astype(BF16)
            dq_scr[...] += _nn(kt, dst)
            dk_scr[pl.ds(start, tk), :] += _nn(dst, qr)
            dv_scr[pl.ds(start, tk), :] += _nn(pt.astype(BF16), dor)

        chunk(0, True)

        def loop(c, carry):
            chunk(c, False)
            return carry

        lax.fori_loop(1, nk, loop, 0)
        dq_ref[...] = dq_scr[...].T.reshape(G, tq, LANE)

        @pl.when(i == nq - 1)
        def _():
            ck = pltpu.make_async_copy(dk_scr, dk_hbm.at[j], sem.at[0])
            cv = pltpu.make_async_copy(dv_scr, dv_hbm.at[j], sem.at[1])
            ck.start()
            cv.start()
            ck.wait()
            cv.wait()

    anyspec = pl.BlockSpec(memory_space=pl.ANY)
    ospec = pl.BlockSpec((tq, G * AT_HD), lambda j, i: (i, j))
    return pl.pallas_call(
        body, grid=(AT_KV, nq),
        in_specs=[pl.BlockSpec((G, tq, LANE), lambda j, i: (j, i, 0)), pl.BlockSpec((G, LANE, tq), lambda j, i: (j, 0, i)),
                  anyspec, anyspec, anyspec, ospec, ospec, pl.BlockSpec((G, 1, tq), lambda j, i: (j, 0, i))],
        out_specs=[pl.BlockSpec((G, tq, LANE), lambda j, i: (j, i, 0)), anyspec, anyspec],
        out_shape=[SDS((AT_HEADS, L, LANE), F32), SDS((AT_KV, L, LANE), F32), SDS((AT_KV, L, LANE), F32)],
        scratch_shapes=[pltpu.VMEM((L, LANE), BF16), pltpu.VMEM((nk, LANE, tk), BF16), pltpu.VMEM((L, LANE), BF16),
                        pltpu.VMEM((L, LANE), F32), pltpu.VMEM((L, LANE), F32), pltpu.VMEM((LANE, R), F32),
                        pltpu.SemaphoreType.DMA((3,))],
        compiler_params=_params(("arbitrary", "arbitrary"), VMEM_LARGE), name=name)(qm, qt, kr, krt, vb, do, of, lse)


def _merge_fwd(ya, o8, wua, wubp, z, *, name):
    L = ya.shape[0]
    D = wua.shape[1]
    tm, tn = _tile(L, 1536), 256
    ga, gb = (Z_HG + Z_AT) // tn, (Z_HG + Z_AT + D) // tn

    def body(ya_ref, o8_ref, wa_ref, wb_ref, za_ref, zb_ref, mix_ref):
        pa = _nn(ya_ref[...], wa_ref[...])
        pb = _nn(o8_ref[...], wb_ref[...])
        mix_ref[...] = (_sigmoid(za_ref[...]) * pa + _sigmoid(zb_ref[...]) * pb).astype(BF16)

    return pl.pallas_call(
        body, grid=(D // tn, L // tm),
        in_specs=[pl.BlockSpec((tm, ya.shape[1]), lambda j, i: (i, 0)), pl.BlockSpec((tm, o8.shape[1]), lambda j, i: (i, 0)),
                  pl.BlockSpec((wua.shape[0], tn), lambda j, i: (0, j)), pl.BlockSpec((wubp.shape[0], tn), lambda j, i: (0, j)),
                  pl.BlockSpec((tm, tn), lambda j, i: (i, ga + j)), pl.BlockSpec((tm, tn), lambda j, i: (i, gb + j))],
        out_specs=pl.BlockSpec((tm, tn), lambda j, i: (i, j)), out_shape=SDS((L, D), BF16),
        compiler_params=_params(("parallel", "parallel")), name=name)(ya, o8, wua, wubp, z, z)


def _merge_bwd(dh, wout, ya, o8, wua, wubp, z, *, name):
    L = ya.shape[0]
    D = wua.shape[1]
    tm, tn = _tile(L, 1536), 256
    ga, gb = (Z_HG + Z_AT) // tn, (Z_HG + Z_AT + D) // tn

    def body(dh_ref, wo_ref, ya_ref, o8_ref, wa_ref, wb_ref, za_ref, zb_ref, dpa_ref, dpb_ref, dza_ref, dzb_ref):
        dm = _nt(dh_ref[...].astype(BF16), wo_ref[...])
        pa = _nn(ya_ref[...], wa_ref[...])
        pb = _nn(o8_ref[...], wb_ref[...])
        sa, sb = _sigmoid(za_ref[...]), _sigmoid(zb_ref[...])
        dpa_ref[...] = (dm * sa).astype(BF16)
        dpb_ref[...] = (dm * sb).astype(BF16)
        dza_ref[...] = (dm * pa * sa * (1.0 - sa)).astype(BF16)
        dzb_ref[...] = (dm * pb * sb * (1.0 - sb)).astype(BF16)

    ospec = pl.BlockSpec((tm, tn), lambda j, i: (i, j))
    return pl.pallas_call(
        body, grid=(D // tn, L // tm),
        in_specs=[pl.BlockSpec((tm, D), lambda j, i: (i, 0)), pl.BlockSpec((tn, D), lambda j, i: (j, 0)),
                  pl.BlockSpec((tm, ya.shape[1]), lambda j, i: (i, 0)), pl.BlockSpec((tm, o8.shape[1]), lambda j, i: (i, 0)),
                  pl.BlockSpec((wua.shape[0], tn), lambda j, i: (0, j)), pl.BlockSpec((wubp.shape[0], tn), lambda j, i: (0, j)),
                  pl.BlockSpec((tm, tn), lambda j, i: (i, ga + j)), pl.BlockSpec((tm, tn), lambda j, i: (i, gb + j))],
        out_specs=[ospec] * 4, out_shape=[SDS((L, D), BF16)] * 4,
        compiler_params=_params(("parallel", "parallel")), name=name)(dh, wout, ya, o8, wua, wubp, z, z)


def _loss_head(h, tgt, *, name):
    L, D = h.shape
    tm = PAD

    def body(h_ref, t_ref, dh_ref, ls_ref):
        i = pl.program_id(0)

        @pl.when(i == 0)
        def _():
            ls_ref[...] = jnp.zeros_like(ls_ref)
            dh_ref[...] = jnp.zeros_like(dh_ref)

        @pl.when(i > 0)
        def _():
            e = h_ref[...] - t_ref[...]
            dh_ref[...] = e * (1.0 / D)
            s = jnp.sum(e * e, axis=0, keepdims=True)
            tot = s[:, :LANE]
            for c in range(1, D // LANE):
                tot = tot + s[:, LANE * c:LANE * (c + 1)]
            ls_ref[...] += tot

    return pl.pallas_call(
        body, grid=(L // tm,),
        in_specs=[pl.BlockSpec((tm, D), lambda i: (i, 0)), pl.BlockSpec((tm, D), lambda i: (jnp.maximum(i - 1, 0), 0))],
        out_specs=[pl.BlockSpec((tm, D), lambda i: (i, 0)), pl.BlockSpec((1, LANE), lambda i: (0, 0))],
        out_shape=[SDS((L, D), F32), SDS((1, LANE), F32)],
        compiler_params=_params(("arbitrary",)), name=name)(h, tgt)


def _adamw(w, g, m, v, *, name):
    shape = w.shape
    w2, g2, m2, v2 = [a.reshape(-1, shape[-1]) for a in (w, g, m, v)]
    rows, cols = w2.shape
    tr = _tile(rows, 256, 8)

    def body(w_ref, g_ref, m_ref, v_ref, d_ref, nm_ref, nv_ref):
        g = g_ref[...]
        m = ADAM_B1 * m_ref[...] + (1.0 - ADAM_B1) * g
        v = ADAM_B2 * v_ref[...] + (1.0 - ADAM_B2) * (g * g)
        m_hat = m / (1.0 - ADAM_B1 ** ADAM_STEP)
        v_hat = v / (1.0 - ADAM_B2 ** ADAM_STEP)
        d_ref[...] = -ADAM_LR * (m_hat / (jnp.sqrt(v_hat) + ADAM_EPS) + ADAM_WD * w_ref[...])
        nm_ref[...] = m
        nv_ref[...] = v

    spec = pl.BlockSpec((tr, cols), lambda i: (i, 0))
    outs = pl.pallas_call(
        body, grid=(rows // tr,), in_specs=[spec] * 4, out_specs=[spec] * 3, out_shape=[SDS((rows, cols), F32)] * 3,
        compiler_params=_params(("parallel",)), name=name)(w2, g2, m2, v2)
    return [o.reshape(shape) for o in outs]


def _place():
    return lax.axis_index("x"), lax.axis_index("y"), lax.axis_index("c")


def _allgather_small(v, *, name):
    m_per, n = v.shape

    def body(x_ref, out_ref, send_sems, recv_sems, local_sem):
        x, y, c = _place()
        me, sibling = (x, y, c), (x, y, 1 - c)
        chips = [(1 - x, y), (x, 1 - y), (1 - x, 1 - y)]

        def rows(px, py, pc):
            return out_ref.at[pl.ds((4 * px + 2 * py + pc) * m_per, m_per), :]

        def copy(k, block, to, src=None):
            return pltpu.make_async_remote_copy(
                src_ref=rows(*block) if src is None else src, dst_ref=rows(*block),
                send_sem=send_sems.at[k], recv_sem=recv_sems.at[k], device_id=to, device_id_type=MESH)

        mine = pltpu.make_async_copy(x_ref, rows(*me), local_sem)
        mine.start()
        first = [copy(0, me, sibling, src=x_ref)]
        first += [copy(1 + j, me, (*chip, c), src=x_ref) for j, chip in enumerate(chips)]
        for cp in first:
            cp.start()
        passed = [copy(4 + j, (*chip, c), sibling) for j, chip in enumerate(chips)]
        for j, chip in enumerate(chips):
            copy(1 + j, (*chip, c), me).wait_recv()
            passed[j].start()
        copy(0, sibling, me).wait_recv()
        for j, chip in enumerate(chips):
            copy(4 + j, (*chip, 1 - c), me).wait_recv()
        for cp in first + passed:
            cp.wait_send()
        mine.wait()

    return pl.pallas_call(
        body, out_shape=SDS((8 * m_per, n), v.dtype),
        in_specs=[pl.BlockSpec(memory_space=pltpu.VMEM)], out_specs=pl.BlockSpec(memory_space=pltpu.VMEM),
        scratch_shapes=[pltpu.SemaphoreType.DMA((7,)), pltpu.SemaphoreType.DMA((7,)), pltpu.SemaphoreType.DMA],
        name=name)(v)


def _chips(x, y):
    return [(1 - x, y), (x, 1 - y), (1 - x, 1 - y)]


def _gather_mats(shards, *, name):
    n = len(shards)

    def body(*refs):
        ins, outs = refs[:n], refs[n:2 * n]
        send_sems, recv_sems, fsend_sems, frecv_sems = refs[2 * n:]
        x, y, c = _place()
        s_me, sibling, chips = 2 * x + y, (x, y, 1 - c), _chips(x, y)

        def copy(src, dst, ssem, rsem, to):
            return pltpu.make_async_remote_copy(src_ref=src, dst_ref=dst, send_sem=ssem, recv_sem=rsem,
                                                device_id=to, device_id_type=MESH)

        first = [copy(ins[t].at[c], outs[t].at[s_me, c], send_sems.at[3 * t + k], recv_sems.at[3 * t + k], (*chip, c))
                 for t in range(n) for k, chip in enumerate(chips)]
        for cp in first:
            cp.start()
        passed = []
        for t in range(n):
            for k, chip in enumerate(chips):
                slot = outs[t].at[2 * chip[0] + chip[1], c]
                copy(ins[t].at[c], slot, send_sems.at[3 * t + k], recv_sems.at[3 * t + k], (*chip, c)).wait_recv()
                fw = copy(slot, slot, fsend_sems.at[3 * t + k], frecv_sems.at[3 * t + k], sibling)
                fw.start()
                passed.append(fw)
        for t in range(n):
            for k, chip in enumerate(chips):
                slot = outs[t].at[2 * chip[0] + chip[1], 1 - c]
                copy(slot, slot, fsend_sems.at[3 * t + k], frecv_sems.at[3 * t + k], sibling).wait_recv()
        for cp in first + passed:
            cp.wait_send()

    anyspec = pl.BlockSpec(memory_space=pl.ANY)
    return pl.pallas_call(
        body, out_shape=[SDS((4,) + s.shape, s.dtype) for s in shards], in_specs=[anyspec] * n, out_specs=[anyspec] * n,
        scratch_shapes=[pltpu.SemaphoreType.DMA((3 * n,))] * 4, name=name)(*shards)


def _rs_pair_exchange(gs, *, name):
    n = len(gs)

    def body(*refs):
        ins, outs = refs[:n], refs[n:2 * n]
        send_sems, recv_sems = refs[2 * n:]
        x, y, c = _place()
        cps = [pltpu.make_async_remote_copy(src_ref=ins[t].at[k, 1 - c], dst_ref=outs[t].at[k],
                                            send_sem=send_sems.at[4 * t + k], recv_sem=recv_sems.at[4 * t + k],
                                            device_id=(x, y, 1 - c), device_id_type=MESH)
               for t in range(n) for k in range(4)]
        for cp in cps:
            cp.start()
        for cp in cps:
            cp.wait()

    anyspec = pl.BlockSpec(memory_space=pl.ANY)
    return pl.pallas_call(
        body, out_shape=[SDS((4,) + g.shape[2:], g.dtype) for g in gs], in_specs=[anyspec] * n, out_specs=[anyspec] * n,
        scratch_shapes=[pltpu.SemaphoreType.DMA((4 * n,))] * 2, name=name)(*gs)


def _rs_chip_exchange(parts, *, name):
    n = len(parts)

    def body(*refs):
        ins, outs = refs[:n], refs[n:2 * n]
        send_sems, recv_sems, local_sems = refs[2 * n:]
        x, y, c = _place()
        s_me, chips = 2 * x + y, _chips(x, y)

        def copy(t, k, chip, src_slot, dst_slot):
            return pltpu.make_async_remote_copy(
                src_ref=ins[t].at[src_slot], dst_ref=outs[t].at[dst_slot], send_sem=send_sems.at[3 * t + k],
                recv_sem=recv_sems.at[3 * t + k], device_id=(*chip, c), device_id_type=MESH)

        mine = [pltpu.make_async_copy(ins[t].at[s_me], outs[t].at[s_me], local_sems.at[t]) for t in range(n)]
        for cp in mine:
            cp.start()
        sends = [copy(t, k, chip, 2 * chip[0] + chip[1], s_me) for t in range(n) for k, chip in enumerate(chips)]
        for cp in sends:
            cp.start()
        for t in range(n):
            for k, chip in enumerate(chips):
                copy(t, k, chip, s_me, 2 * chip[0] + chip[1]).wait_recv()
        for cp in sends:
            cp.wait_send()
        for cp in mine:
            cp.wait()

    anyspec = pl.BlockSpec(memory_space=pl.ANY)
    return pl.pallas_call(
        body, out_shape=[SDS(p.shape, p.dtype) for p in parts], in_specs=[anyspec] * n, out_specs=[anyspec] * n,
        scratch_shapes=[pltpu.SemaphoreType.DMA((3 * n,))] * 2 + [pltpu.SemaphoreType.DMA((n,))], name=name)(*parts)


def _rs_pair_share(fulls, *, name):
    n = len(fulls)

    def body(*refs):
        ins, outs = refs[:n], refs[n:2 * n]
        send_sems, recv_sems = refs[2 * n:]
        x, y, c = _place()

        def copy(t, half):
            return pltpu.make_async_remote_copy(src_ref=ins[t].at[c], dst_ref=outs[t].at[half], send_sem=send_sems.at[t],
                                                recv_sem=recv_sems.at[t], device_id=(x, y, 1 - c), device_id_type=MESH)

        sends = [copy(t, c) for t in range(n)]
        for cp in sends:
            cp.start()
        for t in range(n):
            copy(t, 1 - c).wait_recv()
        for cp in sends:
            cp.wait_send()

    anyspec = pl.BlockSpec(memory_space=pl.ANY)
    return pl.pallas_call(
        body, out_shape=[SDS(f.shape, f.dtype) for f in fulls], in_specs=[anyspec] * n, out_specs=[anyspec] * n,
        input_output_aliases={t: t for t in range(n)},
        scratch_shapes=[pltpu.SemaphoreType.DMA((n,))] * 2, name=name)(*fulls)


def _add_half(g, other, c1, *, out_dtype, name):
    _, _, h, cs = g.shape
    tr = _tile(h, 512, 16)

    def body(c_ref, g_ref, o_ref, out_ref):
        out_ref[...] = (g_ref[...] + o_ref[...]).astype(out_dtype)

    spec = pl.BlockSpec((None, tr, cs), lambda k, i, c: (k, i, 0))
    return pl.pallas_call(
        body, out_shape=SDS(other.shape, out_dtype),
        grid_spec=pltpu.PrefetchScalarGridSpec(
            num_scalar_prefetch=1, grid=(4, h // tr),
            in_specs=[pl.BlockSpec((None, None, tr, cs), lambda k, i, c: (k, c[0], i, 0)), spec], out_specs=spec),
        compiler_params=_params(("parallel", "parallel")), name=name)(c1, g, other)


def _sum4(x, c1, *, name):
    n, h, cs = x.shape
    tr = _tile(h, 512, 16)

    def body(c_ref, x_ref, o_ref):
        tot = x_ref[0].astype(F32)
        for s in range(1, n):
            tot = tot + x_ref[s].astype(F32)
        o_ref[...] = tot

    return pl.pallas_call(
        body, out_shape=SDS((2, h, cs), F32),
        grid_spec=pltpu.PrefetchScalarGridSpec(
            num_scalar_prefetch=1, grid=(h // tr,),
            in_specs=[pl.BlockSpec((n, tr, cs), lambda i, c: (0, i, 0))],
            out_specs=pl.BlockSpec((None, tr, cs), lambda i, c: (c[0], i, 0))),
        compiler_params=_params(("parallel",)), name=name)(c1, x)


def _finish_small(gathered, lbf, lbb, *, rows, name):
    r_lbf, r_lbb = rows['lb_f'], rows['lb_b']

    def body(g_ref, lbf_ref, lbb_ref, o_ref, dlf_ref, dlb_ref):
        tot = g_ref[0]
        for s in range(1, 8):
            tot = tot + g_ref[s]
        o_ref[...] = tot
        o_ref[0:1, :] = jnp.broadcast_to(jnp.sum(o_ref[0:1, :], axis=1, keepdims=True), (1, LANE))
        for lb_ref, d_ref, r0 in ((lbf_ref, dlf_ref, r_lbf), (lbb_ref, dlb_ref, r_lbb)):
            for hh in range(HG_HEADS):
                sl = slice(LANE * hh, LANE * (hh + 1))
                l0, l1 = lb_ref[0:1, sl], lb_ref[1:2, sl]
                mx = jnp.maximum(l0, l1)
                e0, e1 = jnp.exp(l0 - mx), jnp.exp(l1 - mx)
                p0 = e0 / (e0 + e1)
                d0 = o_ref[r0 + hh:r0 + hh + 1, :] * p0 * (1.0 - p0)
                d_ref[0:1, sl] = d0
                d_ref[1:2, sl] = -d0

    vm = pl.BlockSpec(memory_space=pltpu.VMEM)
    return pl.pallas_call(
        body, in_specs=[vm, vm, vm], out_specs=[vm, vm, vm],
        out_shape=[SDS(gathered.shape[1:], F32), SDS(lbf.shape, F32), SDS(lbb.shape, F32)], name=name)(gathered, lbf, lbb)


def _local_step(x2, tgt2, meta, W, S):
    T, D = x2.shape
    L = PAD + T
    h0 = jnp.concatenate([jnp.zeros((PAD - N_META, D), F32), meta, x2], axis=0)

    qk0 = Z_HG
    w_in = jnp.concatenate([W['w_in'][:, :qk0], _qk_to_group(W['w_in'][:, qk0:qk0 + AT_W + AT_KVW]),
                            W['w_in'][:, qk0 + AT_W + AT_KVW:]], axis=1)
    cc, ss = _rope_tables(L)
    wq_g, wk_g = _group_vec(S['q_norm']), _group_vec(S['k_norm'])

    def ffn_fwd(h, nw, wg, wu, wd, tag):
        n = _rmsnorm_fwd(h, nw, name=tag + "_norm")
        g, u, a = _ffn4_up(n, wg, wu, name=tag + "_up")
        hn = _ffn4_down(a, wd, h, name=tag + "_down")
        return hn, (n, g, u, a)

    def ffn_bwd(dh, h, nw, wg, wu, wd, saved, tag, split=False):
        n, g, u, a = saved
        dg, du = _ffn4_dact(dh, wd, g, u, name=tag + "_dact")
        dn = _ffn4_dn(dg, du, wg, wu, name=tag + "_dn")
        dwg = _ffn4_dw(n, dg, x_is_rows=True, name=tag + "_dwg")
        dwu = _ffn4_dw(n, du, x_is_rows=True, name=tag + "_dwu")
        dwd = _ffn4_dw(dh, a, x_is_rows=False, alpha=0.5, name=tag + "_dwd")
        *dhp, dnw = _rmsnorm_bwd(h, nw, dn, dh, split=split, name=tag + "_norm_bwd")
        return (dhp if split else dhp[0]), dnw, dwg, dwu, dwd

    h1, sv1 = ffn_fwd(h0, S['ffn1_norm'], W['ffn1_w_gate'], W['ffn1_w_up'], W['ffn1_w_down'], "ffn1")
    um = _rmsnorm_fwd(h1, S['mix_norm'], name="mix_norm")
    z = _mm([(um, w_in)], tm=512, tn=1792, tk=D, name="in_proj")
    of, sf = _hg_fwd(z, S['hg_lb_fwd'], rev=False, name="hg_fwd_f")
    ob, sb = _hg_fwd(z, S['hg_lb_bwd'], rev=True, name="hg_fwd_b")
    ya = _hg_post_fwd(of, ob, z, S['hg_out_norm'], name="hg_post")
    qm, qt, kr, krt, vb, vt = _at_prep(z, cc, ss, wq_g, wk_g, name="at_prep")
    yb, yb_f32, lse = _at_fwd(qt, kr, vt, name="at_fwd")
    mixed = _merge_fwd(ya, yb, W['w_up_a'], W['w_up_b'], z, name="merge")
    h2 = _mm([(mixed, W['w_out'])], res=h1, tm=512, tn=D, tk=D, name="out_proj")
    h3, sv2 = ffn_fwd(h2, S['ffn2_norm'], W['ffn2_w_gate'], W['ffn2_w_up'], W['ffn2_w_down'], "ffn2")
    dh3, loss_lanes = _loss_head(h3, tgt2, name="loss_head")

    G = {}
    dh2, dn_ffn2, G['ffn2_w_gate'], G['ffn2_w_up'], G['ffn2_w_down'] = ffn_bwd(
        dh3, h2, S['ffn2_norm'], W['ffn2_w_gate'], W['ffn2_w_up'], W['ffn2_w_down'], sv2, "ffn2")
    dpa, dpb, dzga, dzgb = _merge_bwd(dh2, W['w_out'], ya, yb, W['w_up_a'], W['w_up_b'], z, name="merge_bwd")
    G['w_out'] = _mm([(mixed, dh2)], ta=True, tm=D, tn=D, tk=512, name="d_w_out")
    dya = _mm([(dpa, W['w_up_a'])], tb=True, tm=512, tn=HG_W, tk=D, name="d_ya")
    dyb = _mm([(dpb, W['w_up_b'])], tb=True, tm=512, tn=AT_W, tk=D, name="d_yb")
    G['w_up_a'] = _mm([(ya, dpa)], ta=True, tm=HG_W, tn=D, tk=512, name="d_w_up_a")
    G['w_up_b'] = _mm([(yb, dpb)], ta=True, tm=AT_W, tn=D, tk=512, name="d_w_up_b")
    do_hg, dzg, d_hgn = _hg_post_bwd(dya, of, ob, z, S['hg_out_norm'], name="hg_post_bwd")
    dq_f, dv_f, dzf_f, dlb_f = _hg_bwd(z, S['hg_lb_fwd'], do_hg, sf, None, rev=False, name="hg_bwd_f")
    dzq, dzi, dzf_b, dlb_b = _hg_bwd(z, S['hg_lb_bwd'], do_hg, sb, (dq_f, dv_f), rev=True, name="hg_bwd_b")
    dqm, dk2, dv2 = _at_bwd(qm, qt, kr, krt, vb, dyb, yb_f32, lse, name="at_bwd")
    dz_at, dwq_g, dwk_g = _at_prep_bwd(dqm, dk2, dv2, z, cc, ss, wq_g, wk_g, name="at_prep_bwd")
    dz = jnp.concatenate([dzq, dzi, dzf_f, dzf_b, dzg, dz_at, dzga, dzgb], axis=1)
    dum = _mm([(dz, w_in)], tb=True, tm=512, tn=D, tk=1792, name="d_um")
    dw_in_p = _mm([(um, dz)], ta=True, tm=D, tn=1792, tk=512, name="d_w_in")
    G['w_in'] = jnp.concatenate([dw_in_p[:, :qk0], _qk_from_group(dw_in_p[:, qk0:qk0 + AT_W + AT_KVW]),
                                 dw_in_p[:, qk0 + AT_W + AT_KVW:]], axis=1)
    dh1, dn_mix = _rmsnorm_bwd(h1, S['mix_norm'], dum, dh2, name="mix_norm_bwd")
    (grad_x, dmeta), dn_ffn1, G['ffn1_w_gate'], G['ffn1_w_up'], G['ffn1_w_down'] = ffn_bwd(
        dh1, h0, S['ffn1_norm'], W['ffn1_w_gate'], W['ffn1_w_up'], W['ffn1_w_down'], sv1, "ffn1", split=True)

    small_rows = [('loss', loss_lanes), ('ffn1_norm', dn_ffn1.reshape(-1, LANE)), ('mix_norm', dn_mix.reshape(-1, LANE)),
                  ('ffn2_norm', dn_ffn2.reshape(-1, LANE)), ('hg_out_norm', d_hgn.reshape(-1, LANE)),
                  ('lb_f', dlb_f.reshape(-1, LANE)), ('lb_b', dlb_b.reshape(-1, LANE)), ('q_norm', dwq_g), ('k_norm', dwk_g)]
    return grad_x, dmeta, G, small_rows


def kernel(x, meta_tokens, ffn1_norm, ffn1_w_gate, ffn1_w_up, ffn1_w_down, mix_norm, w_in, hg_lb_fwd, hg_lb_bwd, hg_out_norm, q_norm, k_norm, w_up_a, w_up_b, w_out, ffn2_norm, ffn2_w_gate, ffn2_w_up, ffn2_w_down, loss_target, m_meta_tokens, m_ffn1_norm, m_ffn1_w_gate, m_ffn1_w_up, m_ffn1_w_down, m_mix_norm, m_w_in, m_hg_lb_fwd, m_hg_lb_bwd, m_hg_out_norm, m_q_norm, m_k_norm, m_w_up_a, m_w_up_b, m_w_out, m_ffn2_norm, m_ffn2_w_gate, m_ffn2_w_up, m_ffn2_w_down, v_meta_tokens, v_ffn1_norm, v_ffn1_w_gate, v_ffn1_w_up, v_ffn1_w_down, v_mix_norm, v_w_in, v_hg_lb_fwd, v_hg_lb_bwd, v_hg_out_norm, v_q_norm, v_k_norm, v_w_up_a, v_w_up_b, v_w_out, v_ffn2_norm, v_ffn2_w_gate, v_ffn2_w_up, v_ffn2_w_down):
    given = dict(locals())
    w = {n: given[n] for n in WEIGHTS}
    mom = {n: given["m_" + n] for n in WEIGHTS}
    var = {n: given["v_" + n] for n in WEIGHTS}
    c = lax.axis_index("c")
    D = x.shape[-1]

    shapes = {n: w[n].shape[-2:] for n in MATS + ('meta_tokens',)}
    halves = [w[n].astype(BF16).reshape(2, shapes[n][0] // 2, shapes[n][1]) for n in MATS]
    gathered = _gather_mats(halves, name="gather_weights")
    s_me = 2 * lax.axis_index("x") + lax.axis_index("y")
    W = {}
    for n, hv, g4 in zip(MATS, halves, gathered):
        r, cs = shapes[n]
        g4 = lax.dynamic_update_index_in_dim(g4, hv, s_me, 0).reshape(4, r, cs)
        if n in FFN_MATS:
            W[n] = g4
        elif n in ROW_SHARDED:
            W[n] = g4.reshape(4 * r, cs)
        else:
            W[n] = g4.transpose(1, 0, 2).reshape(r, 4 * cs)
    meta_rows = w['meta_tokens'].reshape(-1, LANE)
    mg = _allgather_small(meta_rows, name="gather_meta").reshape(4, 2, N_META, -1)[:, 0]
    meta = mg.transpose(1, 0, 2).reshape(N_META, D)
    S = {n: w[n] for n in SMALLS}

    grad_x, dmeta, G, small_rows = _local_step(x[0], loss_target[0], meta, W, S)
    G['meta_tokens'] = dmeta

    names = MATS + ('meta_tokens',)
    views = []
    for n in names:
        r, cs = shapes[n]
        if n in FFN_MATS:
            g4 = G[n]
        elif n in ROW_SHARDED:
            g4 = G[n].reshape(4, r, cs)
        else:
            g4 = G[n].reshape(r, 4, cs).transpose(1, 0, 2)
        views.append(g4.reshape(4, 2, r // 2, cs))
    c1 = c.astype(jnp.int32).reshape(1)
    from_sibling = _rs_pair_exchange(views, name="rs_pair_exchange")
    parts = [_add_half(v, o, c1, out_dtype=F32 if n == 'meta_tokens' else BF16, name="rs_pair_sum_" + n)
             for n, v, o in zip(names, views, from_sibling)]
    slabs = _rs_chip_exchange(parts, name="rs_chip_exchange")
    reds = [_sum4(s, c1, name="rs_chip_sum_" + n) for n, s in zip(names, slabs)]
    both = _rs_pair_share(reds, name="rs_pair_share")
    grads = {n: b.reshape(w[n].shape) for n, b in zip(names, both)}

    rows, off = {}, 0
    for nme, blk in small_rows:
        rows[nme] = off
        off += blk.shape[0]
    block = jnp.concatenate([blk for _, blk in small_rows], axis=0)
    n_rows = (off + 7) // 8 * 8
    block = jnp.pad(block, ((0, n_rows - off), (0, 0)))
    allsmall = _allgather_small(block, name="gather_small").reshape(8, n_rows, LANE)
    tot, d_lbf, d_lbb = _finish_small(allsmall, w['hg_lb_fwd'], w['hg_lb_bwd'], rows=rows, name="finish_small")
    loss = 0.5 * tot[0, 0] / D

    def small(nme, shape):
        r0 = rows[nme]
        return tot[r0:r0 + shape[-1] // LANE].reshape(shape)

    grads['ffn1_norm'] = small('ffn1_norm', w['ffn1_norm'].shape)
    grads['mix_norm'] = small('mix_norm', w['mix_norm'].shape)
    grads['ffn2_norm'] = small('ffn2_norm', w['ffn2_norm'].shape)
    grads['hg_out_norm'] = small('hg_out_norm', w['hg_out_norm'].shape)
    grads['hg_lb_fwd'] = d_lbf
    grads['hg_lb_bwd'] = d_lbb
    grads['q_norm'] = _ungroup_vec(tot[rows['q_norm']])
    grads['k_norm'] = _ungroup_vec(tot[rows['k_norm']])

    delta, new_m, new_v = {}, {}, {}
    for n in WEIGHTS:
        delta[n], new_m[n], new_v[n] = _adamw(w[n], grads[n], mom[n], var[n], name="adamw_" + n)
    return (loss, grad_x[None], *[grads[n] for n in WEIGHTS], *[delta[n] for n in WEIGHTS],
            *[new_m[n] for n in WEIGHTS], *[new_v[n] for n in WEIGHTS])
```

```python
import numpy as np
import jax
import jax.numpy as jnp
from jax import lax
from jax.experimental import pallas as pl
from jax.experimental.pallas import tpu as pltpu

F32 = jnp.float32
BF16 = jnp.bfloat16
SDS = jax.ShapeDtypeStruct
MESH = pl.DeviceIdType.MESH

EPS = 1e-6
N_META = 16
PAD = 512
LANE = 128
CHUNK = 128
HG_HEADS = 4
HG_W = HG_HEADS * 128
AT_HEADS = 8
AT_KV = 2
AT_HD = 64
AT_W = AT_HEADS * AT_HD
AT_KVW = AT_KV * AT_HD
VT_ROWS = AT_HD + 16
FWD_CHUNKS_PER_STEP = 4
GRID_W = 64
ROPE_THETA = 10000.0
Z_HG = 5 * HG_W
Z_AT = AT_W + 2 * AT_KVW
ADAM_LR, ADAM_B1, ADAM_B2, ADAM_EPS, ADAM_WD, ADAM_STEP = 0.001, 0.9, 0.999, 1e-08, 0.01, 10
VMEM_DEFAULT = 48 * 1024 * 1024
VMEM_LARGE = 60 * 1024 * 1024
NEG = -1e30

MATS = ('ffn1_w_gate', 'ffn1_w_up', 'ffn1_w_down', 'w_in', 'w_up_a', 'w_up_b', 'w_out',
        'ffn2_w_gate', 'ffn2_w_up', 'ffn2_w_down')
ROW_SHARDED = ('ffn1_w_down', 'w_out', 'ffn2_w_down')
FFN_MATS = ('ffn1_w_gate', 'ffn1_w_up', 'ffn1_w_down', 'ffn2_w_gate', 'ffn2_w_up', 'ffn2_w_down')
SMALLS = ('ffn1_norm', 'mix_norm', 'hg_lb_fwd', 'hg_lb_bwd', 'hg_out_norm', 'q_norm', 'k_norm', 'ffn2_norm')
WEIGHTS = ('meta_tokens', 'ffn1_norm', 'ffn1_w_gate', 'ffn1_w_up', 'ffn1_w_down', 'mix_norm', 'w_in', 'hg_lb_fwd',
           'hg_lb_bwd', 'hg_out_norm', 'q_norm', 'k_norm', 'w_up_a', 'w_up_b', 'w_out', 'ffn2_norm', 'ffn2_w_gate',
           'ffn2_w_up', 'ffn2_w_down')


def _params(sem=None, vmem=VMEM_DEFAULT):
    return pltpu.CompilerParams(dimension_semantics=sem, vmem_limit_bytes=vmem)


def _tile(n, pref, q=LANE):
    for d in range(min(pref, n), 0, -1):
        if n % d == 0 and d % q == 0:
            return d
    return n


def _sigmoid(x):
    return 0.5 * jnp.tanh(0.5 * x) + 0.5


def _dot(a, b, dims):
    return lax.dot_general(a, b, (dims, ((), ())), preferred_element_type=F32)


def _nn(a, b):
    return _dot(a, b, ((1,), (0,)))


def _nt(a, b):
    return _dot(a, b, ((1,), (1,)))


def _tn(a, b):
    return _dot(a, b, ((0,), (0,)))


def _split3(x):
    x1 = x.astype(BF16)
    r = x - x1.astype(F32)
    x2 = r.astype(BF16)
    x3 = (r - x2.astype(F32)).astype(BF16)
    return x1, x2, x3


def _exact_left(m01, x):
    x1, x2, x3 = _split3(x)
    return _nn(m01, x1) + _nn(m01, x2) + _nn(m01, x3)


def _exact_right(x, m01):
    x1, x2, x3 = _split3(x)
    return _nn(x1, m01) + _nn(x2, m01) + _nn(x3, m01)


def _mm(pairs, *, name, ta=False, tb=False, out_dtype=F32, tm=512, tn=1024, tk=1024, alpha=1.0, res=None):
    a0, b0 = pairs[0]
    M = a0.shape[1] if ta else a0.shape[0]
    K = a0.shape[0] if ta else a0.shape[1]
    N = b0.shape[0] if tb else b0.shape[1]
    tm, tn, tk = _tile(M, tm), _tile(N, tn), _tile(K, tk)
    nk = K // tk
    npair = len(pairs)
    dims = ((0 if ta else 1,), (1 if tb else 0,))

    def body(*refs):
        ab = refs[:2 * npair]
        pos = 2 * npair
        res_ref = None
        if res is not None:
            res_ref = refs[pos]
            pos += 1
        o_ref = refs[pos]

        def partial_sum():
            tot = None
            for p in range(npair):
                d = _dot(ab[2 * p][...].astype(BF16), ab[2 * p + 1][...].astype(BF16), dims)
                tot = d if tot is None else tot + d
            return tot

        def finish(acc):
            r = acc if alpha == 1.0 else acc * alpha
            if res_ref is not None:
                r = res_ref[...] + r
            o_ref[...] = r.astype(out_dtype)

        if nk == 1:
            finish(partial_sum())
        else:
            acc_ref = refs[pos + 1]
            k = pl.program_id(2)

            @pl.when(k == 0)
            def _():
                acc_ref[...] = jnp.zeros_like(acc_ref)

            acc_ref[...] += partial_sum()

            @pl.when(k == nk - 1)
            def _():
                finish(acc_ref[...])

    a_spec = pl.BlockSpec((tk, tm), lambda j, i, k: (k, i)) if ta else pl.BlockSpec((tm, tk), lambda j, i, k: (i, k))
    b_spec = pl.BlockSpec((tn, tk), lambda j, i, k: (j, k)) if tb else pl.BlockSpec((tk, tn), lambda j, i, k: (k, j))
    o_spec = pl.BlockSpec((tm, tn), lambda j, i, k: (i, j))
    in_specs, args = [], []
    for a, b in pairs:
        in_specs += [a_spec, b_spec]
        args += [a, b]
    if res is not None:
        in_specs.append(o_spec)
        args.append(res)
    return pl.pallas_call(
        body, grid=(N // tn, M // tm, nk), in_specs=in_specs, out_specs=o_spec,
        out_shape=SDS((M, N), out_dtype),
        scratch_shapes=[pltpu.VMEM((tm, tn), F32)] if nk > 1 else [],
        compiler_params=_params(("parallel", "parallel", "arbitrary")), name=name)(*args)


def _rmsnorm_fwd(h, w, *, name):
    L, D = h.shape
    tm = _tile(L, 512)

    def body(h_ref, w_ref, o_ref):
        x = h_ref[...]
        r = lax.rsqrt(jnp.mean(x * x, axis=-1, keepdims=True) + EPS)
        o_ref[...] = (x * r * w_ref[...]).astype(BF16)

    return pl.pallas_call(
        body, grid=(L // tm,),
        in_specs=[pl.BlockSpec((tm, D), lambda i: (i, 0)), pl.BlockSpec((1, D), lambda i: (0, 0))],
        out_specs=pl.BlockSpec((tm, D), lambda i: (i, 0)), out_shape=SDS((L, D), BF16),
        compiler_params=_params(("parallel",)), name=name)(h, w)


def _rmsnorm_bwd(h, w, dn, dres, *, split=False, name):
    L, D = h.shape
    tm = PAD if split else _tile(L, 512)

    def body(h_ref, w_ref, dn_ref, dres_ref, dh_ref, *rest):
        dw_ref = rest[-1]
        i = pl.program_id(0)
        x = h_ref[...]
        r = lax.rsqrt(jnp.mean(x * x, axis=-1, keepdims=True) + EPS)
        xh = x * r
        dn = dn_ref[...]
        dxh = dn * w_ref[...]
        dh = dres_ref[...] + r * (dxh - xh * jnp.mean(dxh * xh, axis=-1, keepdims=True))
        dh_ref[...] = dh

        @pl.when(i == 0)
        def _():
            dw_ref[...] = jnp.zeros_like(dw_ref)
            if split:
                rest[0][...] = dh[PAD - N_META:]

        dw_ref[...] += jnp.sum(dn * xh, axis=0, keepdims=True)

    row = pl.BlockSpec((tm, D), lambda i: (i, 0))
    vec = pl.BlockSpec((1, D), lambda i: (0, 0))
    if split:
        out_specs = [pl.BlockSpec((tm, D), lambda i: (jnp.maximum(i - 1, 0), 0)), pl.BlockSpec((N_META, D), lambda i: (0, 0)), vec]
        out_shape = [SDS((L - PAD, D), F32), SDS((N_META, D), F32), SDS((1, D), F32)]
    else:
        out_specs, out_shape = [row, vec], [SDS((L, D), F32), SDS((1, D), F32)]
    return pl.pallas_call(
        body, grid=(L // tm,), in_specs=[row, vec, row, row], out_specs=out_specs, out_shape=out_shape,
        compiler_params=_params(("arbitrary",)), name=name)(h, w, dn, dres)


def _ffn4_up(n, wg4, wu4, *, name):
    L, D = n.shape
    ns, _, cs = wg4.shape
    tm = _tile(L, 768)

    def body(n_ref, wg_ref, wu_ref, ag_ref, au_ref, a_ref):
        x = n_ref[...]
        g = _nn(x, wg_ref[...])
        u = _nn(x, wu_ref[...])
        sg = _sigmoid(g)
        silu = g * sg
        ag_ref[...] = (u * (sg * (1.0 + g * (1.0 - sg)))).astype(BF16)
        au_ref[...] = silu.astype(BF16)
        a_ref[...] = (silu * u).astype(BF16)

    wspec = pl.BlockSpec((None, D, cs), lambda j, i: (j, 0, 0))
    ospec = pl.BlockSpec((None, tm, cs), lambda j, i: (j, i, 0))
    return pl.pallas_call(
        body, grid=(ns, L // tm),
        in_specs=[pl.BlockSpec((tm, D), lambda j, i: (i, 0)), wspec, wspec], out_specs=[ospec, ospec, ospec],
        out_shape=[SDS((ns, L, cs), BF16), SDS((ns, L, cs), BF16), SDS((ns, L, cs), BF16)],
        compiler_params=_params(("parallel", "parallel")), name=name)(n, wg4, wu4)


def _ffn4_down(a4, wd4, h, *, name):
    ns, L, cs = a4.shape
    D = wd4.shape[2]
    tm = _tile(L, 512)

    def body(a_ref, w_ref, h_ref, o_ref):
        acc = _nn(a_ref[0], w_ref[0])
        for j in range(1, ns):
            acc = acc + _nn(a_ref[j], w_ref[j])
        o_ref[...] = h_ref[...] + 0.5 * acc

    row = pl.BlockSpec((tm, D), lambda i: (i, 0))
    return pl.pallas_call(
        body, grid=(L // tm,),
        in_specs=[pl.BlockSpec((ns, tm, cs), lambda i: (0, i, 0)), pl.BlockSpec((ns, cs, D), lambda i: (0, 0, 0)), row],
        out_specs=row, out_shape=SDS((L, D), F32),
        compiler_params=_params(("parallel",)), name=name)(a4, wd4, h)


def _ffn4_dact(dh, wd4, ag4, au4, *, name):
    L, D = dh.shape
    ns, cs, _ = wd4.shape
    tm = _tile(L, 768)

    def body(dh_ref, wd_ref, ag_ref, au_ref, dg_ref, du_ref):
        da = 0.5 * _nt(dh_ref[...].astype(BF16), wd_ref[...])
        dg_ref[...] = (da * ag_ref[...].astype(F32)).astype(BF16)
        du_ref[...] = (da * au_ref[...].astype(F32)).astype(BF16)

    ospec = pl.BlockSpec((None, tm, cs), lambda j, i: (j, i, 0))
    return pl.pallas_call(
        body, grid=(ns, L // tm),
        in_specs=[pl.BlockSpec((tm, D), lambda j, i: (i, 0)), pl.BlockSpec((None, cs, D), lambda j, i: (j, 0, 0)), ospec, ospec],
        out_specs=[ospec, ospec], out_shape=[SDS((ns, L, cs), BF16), SDS((ns, L, cs), BF16)],
        compiler_params=_params(("parallel", "parallel")), name=name)(dh, wd4, ag4, au4)


def _ffn4_dn(dg4, du4, wg4, wu4, *, name):
    ns, L, cs = dg4.shape
    D = wg4.shape[1]
    tm = _tile(L, 512)

    def body(dg_ref, du_ref, wg_ref, wu_ref, o_ref):
        acc = None
        for j in range(ns):
            t = _nt(dg_ref[j], wg_ref[j]) + _nt(du_ref[j], wu_ref[j])
            acc = t if acc is None else acc + t
        o_ref[...] = acc

    aspec = pl.BlockSpec((ns, tm, cs), lambda i: (0, i, 0))
    wspec = pl.BlockSpec((ns, D, cs), lambda i: (0, 0, 0))
    return pl.pallas_call(
        body, grid=(L // tm,), in_specs=[aspec, aspec, wspec, wspec],
        out_specs=pl.BlockSpec((tm, D), lambda i: (i, 0)), out_shape=SDS((L, D), F32),
        compiler_params=_params(("parallel",), VMEM_LARGE), name=name)(dg4, du4, wg4, wu4)


def _ffn4_dw(x, y4, *, x_is_rows, alpha=1.0, name):
    L, D = x.shape
    ns, _, cs = y4.shape
    tk = _tile(L, 512)
    nk = L // tk
    oshape = (D, cs) if x_is_rows else (cs, D)

    def body(x_ref, y_ref, o_ref):
        k = pl.program_id(0)

        @pl.when(k == 0)
        def _():
            o_ref[...] = jnp.zeros_like(o_ref)

        xb = x_ref[...].astype(BF16)
        if x_is_rows:
            xt = xb.T
            for j in range(ns):
                o_ref[j] += _nn(xt, y_ref[j])
        else:
            for j in range(ns):
                o_ref[j] += _tn(y_ref[j], xb)

        if alpha != 1.0:
            @pl.when(k == nk - 1)
            def _():
                o_ref[...] = o_ref[...] * alpha

    return pl.pallas_call(
        body, grid=(nk,),
        in_specs=[pl.BlockSpec((tk, D), lambda k: (k, 0)), pl.BlockSpec((ns, tk, cs), lambda k: (0, k, 0))],
        out_specs=pl.BlockSpec((ns,) + oshape, lambda k: (0, 0, 0)), out_shape=SDS((ns,) + oshape, F32),
        compiler_params=_params(("arbitrary",)), name=name)(x, y4)


def _hg_masks(rev):
    t = lax.broadcasted_iota(jnp.int32, (CHUNK, CHUNK), 0)
    s = lax.broadcasted_iota(jnp.int32, (CHUNK, CHUNK), 1)
    causal = (s >= t) if rev else (s <= t)
    levels = []
    for sh in (6, 5, 4):
        same = jnp.right_shift(t, sh + 1) == jnp.right_shift(s, sh + 1)
        tq = jnp.bitwise_and(jnp.right_shift(t, sh), 1)
        sk = jnp.bitwise_and(jnp.right_shift(s, sh), 1)
        levels.append(same & (tq == (0 if rev else 1)) & (sk == (1 if rev else 0)))
    diag = (jnp.right_shift(t, 4) == jnp.right_shift(s, 4)) & causal
    return causal, levels, diag


def _hg_intra_factors(q, k, b, b_scr, rev):
    b_scr[...] = b
    row = lax.broadcasted_iota(jnp.int32, (CHUNK, LANE), 0)
    out = []
    for sh in (6, 5, 4):
        lb = 1 << sh
        pieces = []
        for p in range(0, CHUNK, 2 * lb):
            r = p + lb if rev else p + lb - 1
            pieces.append(jnp.broadcast_to(b_scr[pl.ds(r, 1), :], (2 * lb, LANE)))
        ref = pieces[0] if len(pieces) == 1 else jnp.concatenate(pieces, axis=0)
        qside = jnp.bitwise_and(jnp.right_shift(row, sh), 1) == (0 if rev else 1)
        d = b - ref
        e = jnp.exp(jnp.minimum(jnp.where(qside, d, -d), 0.0))
        eq = jnp.where(qside, e, 0.0)
        ek = jnp.where(qside, 0.0, e)
        out.append((eq, ek, (q * eq).astype(BF16), (k * ek).astype(BF16)))
    pieces = []
    for a in range(0, CHUNK, 16):
        r = a + (8 if rev else 7)
        pieces.append(jnp.broadcast_to(b_scr[pl.ds(r, 1), :], (16, LANE)))
    ref = jnp.concatenate(pieces, axis=0)
    eq = jnp.exp(jnp.minimum(b - ref, 80.0))
    ek = jnp.exp(jnp.minimum(ref - b, 80.0))
    out.append((eq, ek, (q * eq).astype(BF16), (k * ek).astype(BF16)))
    return out


def _hg_gate(zf, l0, l1, valid):
    mx = jnp.maximum(l0, l1)
    e0, e1 = jnp.exp(l0 - mx), jnp.exp(l1 - mx)
    p0 = e0 / (e0 + e1)
    sg = _sigmoid(-zf)
    k = jnp.where(valid, (1.0 - p0) * sg, 0.0)
    return p0, sg, k, jnp.log(1.0 - k)


def _hg_fwd(z, lbp, *, rev, name):
    L = z.shape[0]
    nc = L // CHUNK
    fcol = 3 if rev else 2

    def cidx(j):
        return nc - 1 - j if rev else j

    def body(zq_ref, zi_ref, zf_ref, lb_ref, o_ref, ssave_ref, st_scr, b_scr):
        j = pl.program_id(0)

        @pl.when(j == 0)
        def _():
            st_scr[...] = jnp.zeros_like(st_scr)

        causal, lmasks, dmask = _hg_masks(rev)
        tri = jnp.where(causal, 1.0, 0.0).astype(BF16)
        rowg = cidx(j) * CHUNK + lax.broadcasted_iota(jnp.int32, (CHUNK, LANE), 0)
        valid = rowg >= PAD - N_META
        last = 0 if rev else CHUNK - 1
        for hh in range(HG_HEADS):
            sl = slice(LANE * hh, LANE * (hh + 1))
            zq = zq_ref[:, sl]
            q = zq * _sigmoid(zq)
            v = zi_ref[:, sl].astype(BF16)
            _, _, k, g = _hg_gate(zf_ref[:, sl], lb_ref[0:1, sl], lb_ref[1:2, sl], valid)
            b = _exact_left(tri, g)
            st = st_scr[hh]
            ssave_ref[0, hh] = st
            o = _nt((q * jnp.exp(b)).astype(BF16), st.astype(BF16))
            a = None
            fac = _hg_intra_factors(q, k, b, b_scr, rev)
            for (eq, ek, qq, kk), msk in zip(fac, lmasks + [dmask]):
                t = jnp.where(msk, _nt(qq, kk), 0.0)
                a = t if a is None else a + t
            o_ref[:, sl] = o + _nn(a.astype(BF16), v)
            bl = b_scr[pl.ds(last, 1), :]
            kd = (k * jnp.exp(bl - b)).astype(BF16)
            st_scr[hh] = st * jnp.exp(bl) + _tn(v, kd)

    zspec = lambda col: pl.BlockSpec((CHUNK, HG_W), lambda j: (cidx(j), col))
    return pl.pallas_call(
        body, grid=(nc,),
        in_specs=[zspec(0), zspec(1), zspec(fcol), pl.BlockSpec((2, HG_W), lambda j: (0, 0))],
        out_specs=[pl.BlockSpec((CHUNK, HG_W), lambda j: (cidx(j), 0)),
                   pl.BlockSpec((1, HG_HEADS, LANE, LANE), lambda j: (cidx(j), 0, 0, 0))],
        out_shape=[SDS((L, HG_W), F32), SDS((nc, HG_HEADS, LANE, LANE), F32)],
        scratch_shapes=[pltpu.VMEM((HG_HEADS, LANE, LANE), F32), pltpu.VMEM((CHUNK, LANE), F32)],
        compiler_params=_params(("arbitrary",)), name=name)(z, z, z, lbp)


def _hg_bwd(z, lbp, do, ssave, prev, *, rev, name):
    L = z.shape[0]
    nc = L // CHUNK
    fcol = 3 if rev else 2
    final = prev is not None

    def cidx(j):
        return j if rev else nc - 1 - j

    def body(*refs):
        zq_ref, zi_ref, zf_ref, lb_ref, do_ref, ss_ref = refs[:6]
        pos = 6
        if final:
            dqin_ref, dvin_ref = refs[6:8]
            pos = 8
        dq_ref, dv_ref, dzf_ref, dlb_ref, dst_scr, b_scr = refs[pos:pos + 6]
        j = pl.program_id(0)

        @pl.when(j == 0)
        def _():
            dst_scr[...] = jnp.zeros_like(dst_scr)
            dlb_ref[...] = jnp.zeros_like(dlb_ref)

        causal, lmasks, dmask = _hg_masks(rev)
        tri = jnp.where(causal, 1.0, 0.0).astype(BF16)
        ti = lax.broadcasted_iota(jnp.int32, (CHUNK, CHUNK), 0)
        si = lax.broadcasted_iota(jnp.int32, (CHUNK, CHUNK), 1)
        tri_t = jnp.where((si <= ti) if rev else (si >= ti), 1.0, 0.0).astype(BF16)
        rowg = cidx(j) * CHUNK + lax.broadcasted_iota(jnp.int32, (CHUNK, LANE), 0)
        valid = rowg >= PAD - N_META
        last = 0 if rev else CHUNK - 1
        for hh in range(HG_HEADS):
            sl = slice(LANE * hh, LANE * (hh + 1))
            zq = zq_ref[:, sl]
            sq = _sigmoid(zq)
            q = zq * sq
            v = zi_ref[:, sl].astype(BF16)
            p0, sg, k, g = _hg_gate(zf_ref[:, sl], lb_ref[0:1, sl], lb_ref[1:2, sl], valid)
            b = _exact_left(tri, g)
            dob = do_ref[:, sl].astype(BF16)
            st = ss_ref[0, hh]
            dst = dst_scr[hh]
            stb, dstb = st.astype(BF16), dst.astype(BF16)
            eb = jnp.exp(b)
            qe = (q * eb).astype(BF16)
            fac = _hg_intra_factors(q, k, b, b_scr, rev)
            bl = b_scr[pl.ds(last, 1), :]
            ebl = jnp.exp(bl)
            kde = jnp.exp(bl - b)
            kd = (k * kde).astype(BF16)
            da = jnp.where(causal, _nt(dob, v), 0.0)
            dq = eb * _nn(dob, stb)
            dk_inter = kde * _nn(v, dstb)
            dk = dk_inter
            dv = _nt(kd, dstb)
            a = None
            db = q * dq - k * dk
            for (eq, ek, qq, kk), msk in zip(fac, lmasks + [dmask]):
                t = jnp.where(msk, _nt(qq, kk), 0.0)
                a = t if a is None else a + t
                dal = jnp.where(msk, da, 0.0).astype(BF16)
                mq = _nn(dal, kk)
                mk = _tn(dal, qq)
                dq = dq + eq * mq
                dk = dk + ek * mk
                db = db + (qq.astype(F32) * mq - kk.astype(F32) * mk)
            dv = dv + _tn(a.astype(BF16), dob)
            extra = ebl * jnp.sum(st * dst, axis=0, keepdims=True) + jnp.sum(k * dk_inter, axis=0, keepdims=True)
            dst_scr[hh] = dst * ebl + _tn(dob, qe)
            dg = _exact_left(tri_t, db) + extra
            dk_tot = dk - dg / (1.0 - k)
            dzf_ref[:, sl] = jnp.where(valid, dk_tot * (1.0 - p0) * (-sg * (1.0 - sg)), 0.0).astype(BF16)
            dlb_ref[:, sl] += jnp.sum(jnp.where(valid, -sg * dk_tot, 0.0), axis=0, keepdims=True)
            if final:
                dq_ref[:, sl] = ((dq + dqin_ref[:, sl]) * (sq * (1.0 + zq * (1.0 - sq)))).astype(BF16)
                dv_ref[:, sl] = (dv + dvin_ref[:, sl]).astype(BF16)
            else:
                dq_ref[:, sl] = dq
                dv_ref[:, sl] = dv

    zspec = lambda col: pl.BlockSpec((CHUNK, HG_W), lambda j: (cidx(j), col))
    rspec = pl.BlockSpec((CHUNK, HG_W), lambda j: (cidx(j), 0))
    in_specs = [zspec(0), zspec(1), zspec(fcol), pl.BlockSpec((2, HG_W), lambda j: (0, 0)), rspec,
                pl.BlockSpec((1, HG_HEADS, LANE, LANE), lambda j: (cidx(j), 0, 0, 0))]
    args = [z, z, z, lbp, do, ssave]
    if final:
        in_specs += [rspec, rspec]
        args += list(prev)
    odt = BF16 if final else F32
    return pl.pallas_call(
        body, grid=(nc,), in_specs=in_specs,
        out_specs=[rspec, rspec, rspec, pl.BlockSpec((1, HG_W), lambda j: (0, 0))],
        out_shape=[SDS((L, HG_W), odt), SDS((L, HG_W), odt), SDS((L, HG_W), BF16), SDS((1, HG_W), F32)],
        scratch_shapes=[pltpu.VMEM((HG_HEADS, LANE, LANE), F32), pltpu.VMEM((CHUNK, LANE), F32)],
        compiler_params=_params(("arbitrary",)), name=name)(*args)


def _hg_post_fwd(of, ob, z, w, *, name):
    L = of.shape[0]
    tm = _tile(L, 512)

    def body(of_ref, ob_ref, zg_ref, w_ref, y_ref):
        for hh in range(HG_HEADS):
            sl = slice(LANE * hh, LANE * (hh + 1))
            o = of_ref[:, sl] + ob_ref[:, sl]
            r = lax.rsqrt(jnp.mean(o * o, axis=-1, keepdims=True) + EPS)
            zg = zg_ref[:, sl]
            y_ref[:, sl] = (o * r * w_ref[:, sl] * (zg * _sigmoid(zg))).astype(BF16)

    row = pl.BlockSpec((tm, HG_W), lambda i: (i, 0))
    return pl.pallas_call(
        body, grid=(L // tm,),
        in_specs=[row, row, pl.BlockSpec((tm, HG_W), lambda i: (i, 4)), pl.BlockSpec((1, HG_W), lambda i: (0, 0))],
        out_specs=row, out_shape=SDS((L, HG_W), BF16),
        compiler_params=_params(("parallel",)), name=name)(of, ob, z, w)


def _hg_post_bwd(dy, of, ob, z, w, *, name):
    L = of.shape[0]
    tm = _tile(L, 512)

    def body(dy_ref, of_ref, ob_ref, zg_ref, w_ref, do_ref, dzg_ref, dw_ref):
        @pl.when(pl.program_id(0) == 0)
        def _():
            dw_ref[...] = jnp.zeros_like(dw_ref)

        for hh in range(HG_HEADS):
            sl = slice(LANE * hh, LANE * (hh + 1))
            o = of_ref[:, sl] + ob_ref[:, sl]
            r = lax.rsqrt(jnp.mean(o * o, axis=-1, keepdims=True) + EPS)
            xh = o * r
            zg = zg_ref[:, sl]
            sg = _sigmoid(zg)
            w = w_ref[:, sl]
            dy = dy_ref[:, sl]
            dys = dy * (zg * sg)
            dzg_ref[:, sl] = (dy * xh * w * (sg * (1.0 + zg * (1.0 - sg)))).astype(BF16)
            dw_ref[:, sl] += jnp.sum(dys * xh, axis=0, keepdims=True)
            dxh = dys * w
            do_ref[:, sl] = r * (dxh - xh * jnp.mean(dxh * xh, axis=-1, keepdims=True))

    row = pl.BlockSpec((tm, HG_W), lambda i: (i, 0))
    vec = pl.BlockSpec((1, HG_W), lambda i: (0, 0))
    return pl.pallas_call(
        body, grid=(L // tm,),
        in_specs=[row, row, row, pl.BlockSpec((tm, HG_W), lambda i: (i, 4)), vec],
        out_specs=[row, row, vec],
        out_shape=[SDS((L, HG_W), F32), SDS((L, HG_W), BF16), SDS((1, HG_W), F32)],
        compiler_params=_params(("arbitrary",)), name=name)(dy, of, ob, z, w)


N_GROUPS = (AT_HEADS + AT_KV) // 2


def _qk_to_group(wqk):
    d = wqk.shape[0]
    return wqk.reshape(d, N_GROUPS, 2, AT_HD // 2, 2).transpose(0, 1, 4, 2, 3).reshape(d, N_GROUPS * LANE)


def _qk_from_group(wqk):
    d = wqk.shape[0]
    return wqk.reshape(d, N_GROUPS, 2, 2, AT_HD // 2).transpose(0, 1, 3, 4, 2).reshape(d, N_GROUPS * LANE)


def _group_vec(w64):
    halves = w64.reshape(AT_HD // 2, 2).T
    return jnp.broadcast_to(halves[:, None, :], (2, 2, AT_HD // 2)).reshape(1, LANE)


def _ungroup_vec(w128):
    w = w128.reshape(2, 2, 32).sum(axis=1)
    return w.T.reshape(1, AT_HD)


def _rope_tables(L):
    n_real = L - PAD
    t = np.arange(n_real)
    row = np.concatenate([np.zeros(PAD), t // GRID_W]).astype(np.float32)
    col = np.concatenate([np.zeros(PAD), t % GRID_W]).astype(np.float32)
    inv = jnp.asarray(ROPE_THETA, F32) ** (-jnp.arange(0, AT_HD // 2, 2, dtype=F32) / (AT_HD // 2))
    ang = jnp.concatenate([jnp.asarray(row)[:, None] * inv, jnp.asarray(col)[:, None] * inv], axis=-1)
    cos, sin = jnp.cos(ang), jnp.sin(ang)
    cc = jnp.tile(cos, (1, 4))
    ss = jnp.concatenate([-sin, -sin, sin, sin], axis=1)
    return cc, ss


def _seg_matrix():
    a = lax.broadcasted_iota(jnp.int32, (LANE, LANE), 0)
    b = lax.broadcasted_iota(jnp.int32, (LANE, LANE), 1)
    same = jnp.bitwise_and(jnp.right_shift(a, 5), 1) == jnp.bitwise_and(jnp.right_shift(b, 5), 1)
    return jnp.where(same, 1.0, 0.0).astype(BF16)


def _slot_mask(shape, hp):
    lane = lax.broadcasted_iota(jnp.int32, shape, 1)
    return jnp.bitwise_and(jnp.right_shift(lane, 5), 1) == hp


def _at_prep(z, cc, ss, wq, wk, *, name):
    L = z.shape[0]
    tm = PAD
    qcol = Z_HG // AT_W
    kvcol = (Z_HG + AT_W) // (2 * LANE)

    def body(zq_ref, zkv_ref, cc_ref, ss_ref, wq_ref, wk_ref, qm_ref, qt_ref, kr_ref, krt_ref, vb_ref, vt_ref):
        seg = _seg_matrix()
        cc, ss = cc_ref[...], ss_ref[...]

        def normrope(x, w):
            r = lax.rsqrt(_exact_right(x * x, seg) * (1.0 / AT_HD) + EPS)
            y = x * r * w
            return y * cc + pltpu.roll(y, 64, 1) * ss

        for g in range(AT_HEADS // 2):
            o = normrope(zq_ref[:, LANE * g:LANE * (g + 1)], wq_ref[...]) * (AT_HD ** -0.5)
            for hp in range(2):
                h = 2 * g + hp
                tgt = h // (AT_HEADS // AT_KV)
                xm = jnp.where(_slot_mask(o.shape, hp), o, 0.0)
                if tgt != hp:
                    xm = pltpu.roll(xm, 32 if tgt == 1 else 96, 1)
                qm_ref[h] = xm.astype(BF16)
                qt_ref[h] = xm.T.astype(BF16)
        kr = normrope(zkv_ref[:, :LANE], wk_ref[...])
        kr_ref[...] = kr.astype(BF16)
        krt_ref[0] = kr.T.astype(BF16)
        v = zkv_ref[:, LANE:]
        low = lax.broadcasted_iota(jnp.int32, v.shape, 1) < AT_HD
        vb_ref[0] = jnp.where(low, v, 0.0).astype(BF16)
        vb_ref[1] = jnp.where(low, pltpu.roll(v, AT_HD, 1), 0.0).astype(BF16)
        vt = v.T.astype(BF16)
        ones = jnp.ones((VT_ROWS - AT_HD, tm), BF16)
        for j in range(AT_KV):
            vt_ref[j, 0, 0:AT_HD, :] = vt[AT_HD * j:AT_HD * (j + 1)]
            vt_ref[j, 0, AT_HD:VT_ROWS, :] = ones

    tab = pl.BlockSpec((tm, LANE), lambda i: (i, 0))
    vec = pl.BlockSpec((1, LANE), lambda i: (0, 0))
    nt = L // tm
    return pl.pallas_call(
        body, grid=(nt,),
        in_specs=[pl.BlockSpec((tm, AT_W), lambda i: (i, qcol)), pl.BlockSpec((tm, 2 * LANE), lambda i: (i, kvcol)),
                  tab, tab, vec, vec],
        out_specs=[pl.BlockSpec((AT_HEADS, tm, LANE), lambda i: (0, i, 0)),
                   pl.BlockSpec((AT_HEADS, LANE, tm), lambda i: (0, 0, i)), tab,
                   pl.BlockSpec((1, LANE, tm), lambda i: (i, 0, 0)),
                   pl.BlockSpec((AT_KV, tm, LANE), lambda i: (0, i, 0)),
                   pl.BlockSpec((AT_KV, 1, VT_ROWS, tm), lambda i: (0, i, 0, 0))],
        out_shape=[SDS((AT_HEADS, L, LANE), BF16), SDS((AT_HEADS, LANE, L), BF16), SDS((L, LANE), BF16),
                   SDS((nt, LANE, tm), BF16), SDS((AT_KV, L, LANE), BF16), SDS((AT_KV, nt, VT_ROWS, tm), BF16)],
        compiler_params=_params(("parallel",)), name=name)(z, z, cc, ss, wq, wk)


def _at_prep_bwd(dqm, dk2, dv2, z, cc, ss, wq, wk, *, name):
    L = z.shape[0]
    tm = PAD
    qcol = Z_HG // AT_W
    kvcol = (Z_HG + AT_W) // (2 * LANE)

    def body(dqm_ref, dk2_ref, dv2_ref, zq_ref, zkv_ref, cc_ref, ss_ref, wq_ref, wk_ref, dz_ref, dwq_ref, dwk_ref):
        @pl.when(pl.program_id(0) == 0)
        def _():
            dwq_ref[...] = jnp.zeros_like(dwq_ref)
            dwk_ref[...] = jnp.zeros_like(dwk_ref)

        seg = _seg_matrix()
        cc, ss = cc_ref[...], ss_ref[...]

        def back(x, w, do):
            dy = do * cc + pltpu.roll(do * ss, 64, 1)
            r = lax.rsqrt(_exact_right(x * x, seg) * (1.0 / AT_HD) + EPS)
            xh = x * r
            dxh = dy * w
            dx = r * (dxh - xh * (_exact_right(dxh * xh, seg) * (1.0 / AT_HD)))
            return dx, jnp.sum(dy * xh, axis=0, keepdims=True)

        for g in range(AT_HEADS // 2):
            do = None
            for hp in range(2):
                h = 2 * g + hp
                tgt = h // (AT_HEADS // AT_KV)
                d = jnp.where(_slot_mask((tm, LANE), tgt), dqm_ref[h], 0.0)
                if tgt != hp:
                    d = pltpu.roll(d, 96 if tgt == 1 else 32, 1)
                do = d if do is None else do + d
            dx, dw = back(zq_ref[:, LANE * g:LANE * (g + 1)], wq_ref[...], do * (AT_HD ** -0.5))
            dz_ref[:, LANE * g:LANE * (g + 1)] = dx.astype(BF16)
            dwq_ref[...] += dw
        dx, dw = back(zkv_ref[:, :LANE], wk_ref[...], dk2_ref[0] + dk2_ref[1])
        dz_ref[:, AT_W:AT_W + LANE] = dx.astype(BF16)
        dwk_ref[...] += dw
        dv0 = dv2_ref[0]
        low = lax.broadcasted_iota(jnp.int32, dv0.shape, 1) < AT_HD
        dz_ref[:, AT_W + LANE:] = jnp.where(low, dv0, pltpu.roll(dv2_ref[1], AT_HD, 1)).astype(BF16)

    tab = pl.BlockSpec((tm, LANE), lambda i: (i, 0))
    vec = pl.BlockSpec((1, LANE), lambda i: (0, 0))
    two = pl.BlockSpec((AT_KV, tm, LANE), lambda i: (0, i, 0))
    return pl.pallas_call(
        body, grid=(L // tm,),
        in_specs=[pl.BlockSpec((AT_HEADS, tm, LANE), lambda i: (0, i, 0)), two, two,
                  pl.BlockSpec((tm, AT_W), lambda i: (i, qcol)), pl.BlockSpec((tm, 2 * LANE), lambda i: (i, kvcol)),
                  tab, tab, vec, vec],
        out_specs=[pl.BlockSpec((tm, Z_AT), lambda i: (i, 0)), vec, vec],
        out_shape=[SDS((L, Z_AT), BF16), SDS((1, LANE), F32), SDS((1, LANE), F32)],
        compiler_params=_params(("arbitrary",)), name=name)(dqm, dk2, dv2, z, z, cc, ss, wq, wk)


def _at_fwd(qt, kr, vt, *, name):
    L = kr.shape[0]
    G = AT_HEADS // AT_KV
    tq = _tile(L, 384)
    tk = PAD
    nk = L // tk
    R = G * tq
    per = FWD_CHUNKS_PER_STEP if (nk - 1) % FWD_CHUNKS_PER_STEP == 0 else 1

    def body(q_ref, k_ref, v_ref, ob_ref, of_ref, lse_ref, m_scr, acc_scr):
        i = pl.program_id(1)
        qt = jnp.concatenate([q_ref[g] for g in range(G)], axis=1)
        m_scr[...] = jnp.full_like(m_scr, NEG)
        acc_scr[...] = jnp.zeros_like(acc_scr)

        def chunks(c, n, masked):
            start = c * tk if isinstance(c, int) else pl.multiple_of(c * tk, tk)
            st = _nn(k_ref[pl.ds(start, n * tk), :], qt).astype(BF16)
            if masked:
                key = lax.broadcasted_iota(jnp.int32, st.shape, 0)
                st = jnp.where(key >= PAD - N_META, st, NEG)
            m_prev = m_scr[...]
            m_new = jnp.maximum(m_prev, jnp.max(st, axis=0, keepdims=True).astype(F32))
            pt = jnp.exp(st - m_new.astype(BF16))
            acc = jnp.exp(m_prev - m_new) * acc_scr[...]
            for u in range(n):
                acc = acc + _nn(v_ref[0, c + u], pt[u * tk:(u + 1) * tk])
            acc_scr[...] = acc
            m_scr[...] = m_new

        chunks(0, 1, True)

        def loop(t, carry):
            chunks(1 + per * t, per, False)
            return carry

        lax.fori_loop(0, (nk - 1) // per, loop, 0)
        l = acc_scr[pl.ds(AT_HD, 1), :]
        lse = m_scr[...] + jnp.log(l)
        on = acc_scr[0:AT_HD, :] / l
        o = jnp.concatenate([on[:, g * tq:(g + 1) * tq] for g in range(G)], axis=0).T
        rowg = i * tq + lax.broadcasted_iota(jnp.int32, o.shape, 0)
        o = jnp.where(rowg >= PAD - N_META, o, 0.0)
        ob_ref[...] = o.astype(BF16)
        of_ref[...] = o
        for g in range(G):
            lse_ref[g] = lse[:, g * tq:(g + 1) * tq]

    ospec = pl.BlockSpec((tq, G * AT_HD), lambda j, i: (i, j))
    return pl.pallas_call(
        body, grid=(AT_KV, L // tq),
        in_specs=[pl.BlockSpec((G, LANE, tq), lambda j, i: (j, 0, i)), pl.BlockSpec((L, LANE), lambda j, i: (0, 0)),
                  pl.BlockSpec((1, nk, VT_ROWS, tk), lambda j, i: (j, 0, 0, 0))],
        out_specs=[ospec, ospec, pl.BlockSpec((G, 1, tq), lambda j, i: (j, 0, i))],
        out_shape=[SDS((L, AT_W), BF16), SDS((L, AT_W), F32), SDS((AT_HEADS, 1, L), F32)],
        scratch_shapes=[pltpu.VMEM((1, R), F32), pltpu.VMEM((VT_ROWS, R), F32)],
        compiler_params=_params(("parallel", "parallel")), name=name)(qt, kr, vt)


def _at_bwd(qm, qt, kr, krt, vb, do, of, lse, *, name):
    L = kr.shape[0]
    G = AT_HEADS // AT_KV
    tq = _tile(L, 384)
    tk = PAD
    nk = L // tk
    nq = L // tq
    R = G * tq

    def body(qm_ref, q_ref, k_hbm, kt_hbm, v_hbm, do_ref, o_ref, lse_ref, dq_ref, dk_hbm, dv_hbm,
             k_scr, kt_scr, v_scr, dk_scr, dv_scr, dq_scr, sem):
        j, i = pl.program_id(0), pl.program_id(1)

        @pl.when(i == 0)
        def _():
            cps = [pltpu.make_async_copy(k_hbm, k_scr, sem.at[0]), pltpu.make_async_copy(kt_hbm, kt_scr, sem.at[1]),
                   pltpu.make_async_copy(v_hbm.at[j], v_scr, sem.at[2])]
            for cp in cps:
                cp.start()
            dk_scr[...] = jnp.zeros_like(dk_scr)
            dv_scr[...] = jnp.zeros_like(dv_scr)
            for cp in cps:
                cp.wait()

        qt = jnp.concatenate([q_ref[g] for g in range(G)], axis=1)
        rowg = i * tq + lax.broadcasted_iota(jnp.int32, (tq, G * AT_HD), 0)
        dot_all = jnp.where(rowg >= PAD - N_META, do_ref[...], 0.0).T
        ot_all = o_ref[...].T
        dot = jnp.concatenate([dot_all[AT_HD * g:AT_HD * (g + 1)] for g in range(G)], axis=1)
        ot = jnp.concatenate([ot_all[AT_HD * g:AT_HD * (g + 1)] for g in range(G)], axis=1)
        delta = jnp.sum(dot * ot, axis=0, keepdims=True)
        dot128 = jnp.concatenate([dot, jnp.zeros_like(dot)], axis=0)
        dor = dot128.T.astype(BF16)
        dot128 = dot128.astype(BF16)
        qr = qm_ref[...].reshape(R, LANE)
        lse_v = jnp.concatenate([lse_ref[g] for g in range(G)], axis=1)
        dq_scr[...] = jnp.zeros_like(dq_scr)

        def chunk(c, masked):
            start = c * tk if isinstance(c, int) else pl.multiple_of(c * tk, tk)
            k = k_scr[pl.ds(start, tk), :]
            kt = kt_scr[c]
            v = v_scr[pl.ds(start, tk), :]
            st = _nn(k, qt)
            if masked:
                key = lax.broadcasted_iota(jnp.int32, st.shape, 0)
                st = jnp.where(key >= PAD - N_META, st, NEG)
            pt = jnp.exp(st - lse_v)
            dst = (pt * (_nn(v, dot128) - delta)).astype(BF16)
            dq_scr[...] += _nn(kt, dst)
            dk_scr[pl.ds(start, tk), :] += _nn(dst, qr)
            dv_scr[pl.ds(start, tk), :] += _nn(pt.astype(BF16), dor)

        chunk(0, True)

        def loop(c, carry):
            chunk(c, False)
            return carry

        lax.fori_loop(1, nk, loop, 0)
        dq_ref[...] = dq_scr[...].T.reshape(G, tq, LANE)

        @pl.when(i == nq - 1)
        def _():
            ck = pltpu.make_async_copy(dk_scr, dk_hbm.at[j], sem.at[0])
            cv = pltpu.make_async_copy(dv_scr, dv_hbm.at[j], sem.at[1])
            ck.start()
            cv.start()
            ck.wait()
            cv.wait()

    anyspec = pl.BlockSpec(memory_space=pl.ANY)
    ospec = pl.BlockSpec((tq, G * AT_HD), lambda j, i: (i, j))
    return pl.pallas_call(
        body, grid=(AT_KV, nq),
        in_specs=[pl.BlockSpec((G, tq, LANE), lambda j, i: (j, i, 0)), pl.BlockSpec((G, LANE, tq), lambda j, i: (j, 0, i)),
                  anyspec, anyspec, anyspec, ospec, ospec, pl.BlockSpec((G, 1, tq), lambda j, i: (j, 0, i))],
        out_specs=[pl.BlockSpec((G, tq, LANE), lambda j, i: (j, i, 0)), anyspec, anyspec],
        out_shape=[SDS((AT_HEADS, L, LANE), F32), SDS((AT_KV, L, LANE), F32), SDS((AT_KV, L, LANE), F32)],
        scratch_shapes=[pltpu.VMEM((L, LANE), BF16), pltpu.VMEM((nk, LANE, tk), BF16), pltpu.VMEM((L, LANE), BF16),
                        pltpu.VMEM((L, LANE), F32), pltpu.VMEM((L, LANE), F32), pltpu.VMEM((LANE, R), F32),
                        pltpu.SemaphoreType.DMA((3,))],
        compiler_params=_params(("arbitrary", "arbitrary"), VMEM_LARGE), name=name)(qm, qt, kr, krt, vb, do, of, lse)


def _merge_fwd(ya, o8, wua, wubp, z, *, name):
    L = ya.shape[0]
    D = wua.shape[1]
    tm, tn = _tile(L, 1536), 256
    ga, gb = (Z_HG + Z_AT) // tn, (Z_HG + Z_AT + D) // tn

    def body(ya_ref, o8_ref, wa_ref, wb_ref, za_ref, zb_ref, mix_ref):
        pa = _nn(ya_ref[...], wa_ref[...])
        pb = _nn(o8_ref[...], wb_ref[...])
        mix_ref[...] = (_sigmoid(za_ref[...]) * pa + _sigmoid(zb_ref[...]) * pb).astype(BF16)

    return pl.pallas_call(
        body, grid=(D // tn, L // tm),
        in_specs=[pl.BlockSpec((tm, ya.shape[1]), lambda j, i: (i, 0)), pl.BlockSpec((tm, o8.shape[1]), lambda j, i: (i, 0)),
                  pl.BlockSpec((wua.shape[0], tn), lambda j, i: (0, j)), pl.BlockSpec((wubp.shape[0], tn), lambda j, i: (0, j)),
                  pl.BlockSpec((tm, tn), lambda j, i: (i, ga + j)), pl.BlockSpec((tm, tn), lambda j, i: (i, gb + j))],
        out_specs=pl.BlockSpec((tm, tn), lambda j, i: (i, j)), out_shape=SDS((L, D), BF16),
        compiler_params=_params(("parallel", "parallel")), name=name)(ya, o8, wua, wubp, z, z)


def _merge_bwd(dh, wout, ya, o8, wua, wubp, z, *, name):
    L = ya.shape[0]
    D = wua.shape[1]
    tm, tn = _tile(L, 1536), 256
    ga, gb = (Z_HG + Z_AT) // tn, (Z_HG + Z_AT + D) // tn

    def body(dh_ref, wo_ref, ya_ref, o8_ref, wa_ref, wb_ref, za_ref, zb_ref, dpa_ref, dpb_ref, dza_ref, dzb_ref):
        dm = _nt(dh_ref[...].astype(BF16), wo_ref[...])
        pa = _nn(ya_ref[...], wa_ref[...])
        pb = _nn(o8_ref[...], wb_ref[...])
        sa, sb = _sigmoid(za_ref[...]), _sigmoid(zb_ref[...])
        dpa_ref[...] = (dm * sa).astype(BF16)
        dpb_ref[...] = (dm * sb).astype(BF16)
        dza_ref[...] = (dm * pa * sa * (1.0 - sa)).astype(BF16)
        dzb_ref[...] = (dm * pb * sb * (1.0 - sb)).astype(BF16)

    ospec = pl.BlockSpec((tm, tn), lambda j, i: (i, j))
    return pl.pallas_call(
        body, grid=(D // tn, L // tm),
        in_specs=[pl.BlockSpec((tm, D), lambda j, i: (i, 0)), pl.BlockSpec((tn, D), lambda j, i: (j, 0)),
                  pl.BlockSpec((tm, ya.shape[1]), lambda j, i: (i, 0)), pl.BlockSpec((tm, o8.shape[1]), lambda j, i: (i, 0)),
                  pl.BlockSpec((wua.shape[0], tn), lambda j, i: (0, j)), pl.BlockSpec((wubp.shape[0], tn), lambda j, i: (0, j)),
                  pl.BlockSpec((tm, tn), lambda j, i: (i, ga + j)), pl.BlockSpec((tm, tn), lambda j, i: (i, gb + j))],
        out_specs=[ospec] * 4, out_shape=[SDS((L, D), BF16)] * 4,
        compiler_params=_params(("parallel", "parallel")), name=name)(dh, wout, ya, o8, wua, wubp, z, z)


def _loss_head(h, tgt, *, name):
    L, D = h.shape
    tm = PAD

    def body(h_ref, t_ref, dh_ref, ls_ref):
        i = pl.program_id(0)

        @pl.when(i == 0)
        def _():
            ls_ref[...] = jnp.zeros_like(ls_ref)
            dh_ref[...] = jnp.zeros_like(dh_ref)

        @pl.when(i > 0)
        def _():
            e = h_ref[...] - t_ref[...]
            dh_ref[...] = e * (1.0 / D)
            s = jnp.sum(e * e, axis=0, keepdims=True)
            tot = s[:, :LANE]
            for c in range(1, D // LANE):
                tot = tot + s[:, LANE * c:LANE * (c + 1)]
            ls_ref[...] += tot

    return pl.pallas_call(
        body, grid=(L // tm,),
        in_specs=[pl.BlockSpec((tm, D), lambda i: (i, 0)), pl.BlockSpec((tm, D), lambda i: (jnp.maximum(i - 1, 0), 0))],
        out_specs=[pl.BlockSpec((tm, D), lambda i: (i, 0)), pl.BlockSpec((1, LANE), lambda i: (0, 0))],
        out_shape=[SDS((L, D), F32), SDS((1, LANE), F32)],
        compiler_params=_params(("arbitrary",)), name=name)(h, tgt)


def _adamw(w, g, m, v, *, name):
    shape = w.shape
    w2, g2, m2, v2 = [a.reshape(-1, shape[-1]) for a in (w, g, m, v)]
    rows, cols = w2.shape
    tr = _tile(rows, 256, 8)

    def body(w_ref, g_ref, m_ref, v_ref, d_ref, nm_ref, nv_ref):
        g = g_ref[...]
        m = ADAM_B1 * m_ref[...] + (1.0 - ADAM_B1) * g
        v = ADAM_B2 * v_ref[...] + (1.0 - ADAM_B2) * (g * g)
        m_hat = m / (1.0 - ADAM_B1 ** ADAM_STEP)
        v_hat = v / (1.0 - ADAM_B2 ** ADAM_STEP)
        d_ref[...] = -ADAM_LR * (m_hat / (jnp.sqrt(v_hat) + ADAM_EPS) + ADAM_WD * w_ref[...])
        nm_ref[...] = m
        nv_ref[...] = v

    spec = pl.BlockSpec((tr, cols), lambda i: (i, 0))
    outs = pl.pallas_call(
        body, grid=(rows // tr,), in_specs=[spec] * 4, out_specs=[spec] * 3, out_shape=[SDS((rows, cols), F32)] * 3,
        compiler_params=_params(("parallel",)), name=name)(w2, g2, m2, v2)
    return [o.reshape(shape) for o in outs]


def _place():
    return lax.axis_index("x"), lax.axis_index("y"), lax.axis_index("c")


def _allgather_small(v, *, name):
    m_per, n = v.shape

    def body(x_ref, out_ref, send_sems, recv_sems, local_sem):
        x, y, c = _place()
        me, sibling = (x, y, c), (x, y, 1 - c)
        chips = [(1 - x, y), (x, 1 - y), (1 - x, 1 - y)]

        def rows(px, py, pc):
            return out_ref.at[pl.ds((4 * px + 2 * py + pc) * m_per, m_per), :]

        def copy(k, block, to, src=None):
            return pltpu.make_async_remote_copy(
                src_ref=rows(*block) if src is None else src, dst_ref=rows(*block),
                send_sem=send_sems.at[k], recv_sem=recv_sems.at[k], device_id=to, device_id_type=MESH)

        mine = pltpu.make_async_copy(x_ref, rows(*me), local_sem)
        mine.start()
        first = [copy(0, me, sibling, src=x_ref)]
        first += [copy(1 + j, me, (*chip, c), src=x_ref) for j, chip in enumerate(chips)]
        for cp in first:
            cp.start()
        passed = [copy(4 + j, (*chip, c), sibling) for j, chip in enumerate(chips)]
        for j, chip in enumerate(chips):
            copy(1 + j, (*chip, c), me).wait_recv()
            passed[j].start()
        copy(0, sibling, me).wait_recv()
        for j, chip in enumerate(chips):
            copy(4 + j, (*chip, 1 - c), me).wait_recv()
        for cp in first + passed:
            cp.wait_send()
        mine.wait()

    return pl.pallas_call(
        body, out_shape=SDS((8 * m_per, n), v.dtype),
        in_specs=[pl.BlockSpec(memory_space=pltpu.VMEM)], out_specs=pl.BlockSpec(memory_space=pltpu.VMEM),
        scratch_shapes=[pltpu.SemaphoreType.DMA((7,)), pltpu.SemaphoreType.DMA((7,)), pltpu.SemaphoreType.DMA],
        name=name)(v)


def _chips(x, y):
    return [(1 - x, y), (x, 1 - y), (1 - x, 1 - y)]


def _gather_mats(shards, *, name):
    n = len(shards)

    def body(*refs):
        ins, outs = refs[:n], refs[n:2 * n]
        send_sems, recv_sems, fsend_sems, frecv_sems = refs[2 * n:]
        x, y, c = _place()
        s_me, sibling, chips = 2 * x + y, (x, y, 1 - c), _chips(x, y)

        def copy(src, dst, ssem, rsem, to):
            return pltpu.make_async_remote_copy(src_ref=src, dst_ref=dst, send_sem=ssem, recv_sem=rsem,
                                                device_id=to, device_id_type=MESH)

        first = [copy(ins[t].at[c], outs[t].at[s_me, c], send_sems.at[3 * t + k], recv_sems.at[3 * t + k], (*chip, c))
                 for t in range(n) for k, chip in enumerate(chips)]
        for cp in first:
            cp.start()
        passed = []
        for t in range(n):
            for k, chip in enumerate(chips):
                slot = outs[t].at[2 * chip[0] + chip[1], c]
                copy(ins[t].at[c], slot, send_sems.at[3 * t + k], recv_sems.at[3 * t + k], (*chip, c)).wait_recv()
                fw = copy(slot, slot, fsend_sems.at[3 * t + k], frecv_sems.at[3 * t + k], sibling)
                fw.start()
                passed.append(fw)
        for t in range(n):
            for k, chip in enumerate(chips):
                slot = outs[t].at[2 * chip[0] + chip[1], 1 - c]
                copy(slot, slot, fsend_sems.at[3 * t + k], frecv_sems.at[3 * t + k], sibling).wait_recv()
        for cp in first + passed:
            cp.wait_send()

    anyspec = pl.BlockSpec(memory_space=pl.ANY)
    return pl.pallas_call(
        body, out_shape=[SDS((4,) + s.shape, s.dtype) for s in shards], in_specs=[anyspec] * n, out_specs=[anyspec] * n,
        scratch_shapes=[pltpu.SemaphoreType.DMA((3 * n,))] * 4, name=name)(*shards)


def _rs_pair_exchange(gs, *, name):
    n = len(gs)

    def body(*refs):
        ins, outs = refs[:n], refs[n:2 * n]
        send_sems, recv_sems = refs[2 * n:]
        x, y, c = _place()
        cps = [pltpu.make_async_remote_copy(src_ref=ins[t].at[k, 1 - c], dst_ref=outs[t].at[k],
                                            send_sem=send_sems.at[4 * t + k], recv_sem=recv_sems.at[4 * t + k],
                                            device_id=(x, y, 1 - c), device_id_type=MESH)
               for t in range(n) for k in range(4)]
        for cp in cps:
            cp.start()
        for cp in cps:
            cp.wait()

    anyspec = pl.BlockSpec(memory_space=pl.ANY)
    return pl.pallas_call(
        body, out_shape=[SDS((4,) + g.shape[2:], g.dtype) for g in gs], in_specs=[anyspec] * n, out_specs=[anyspec] * n,
        scratch_shapes=[pltpu.SemaphoreType.DMA((4 * n,))] * 2, name=name)(*gs)


def _rs_chip_exchange(parts, *, name):
    n = len(parts)

    def body(*refs):
        ins, outs = refs[:n], refs[n:2 * n]
        send_sems, recv_sems, local_sems = refs[2 * n:]
        x, y, c = _place()
        s_me, chips = 2 * x + y, _chips(x, y)

        def copy(t, k, chip, src_slot, dst_slot):
            return pltpu.make_async_remote_copy(
                src_ref=ins[t].at[src_slot], dst_ref=outs[t].at[dst_slot], send_sem=send_sems.at[3 * t + k],
                recv_sem=recv_sems.at[3 * t + k], device_id=(*chip, c), device_id_type=MESH)

        mine = [pltpu.make_async_copy(ins[t].at[s_me], outs[t].at[s_me], local_sems.at[t]) for t in range(n)]
        for cp in mine:
            cp.start()
        sends = [copy(t, k, chip, 2 * chip[0] + chip[1], s_me) for t in range(n) for k, chip in enumerate(chips)]
        for cp in sends:
            cp.start()
        for t in range(n):
            for k, chip in enumerate(chips):
                copy(t, k, chip, s_me, 2 * chip[0] + chip[1]).wait_recv()
        for cp in sends:
            cp.wait_send()
        for cp in mine:
            cp.wait()

    anyspec = pl.BlockSpec(memory_space=pl.ANY)
    return pl.pallas_call(
        body, out_shape=[SDS(p.shape, p.dtype) for p in parts], in_specs=[anyspec] * n, out_specs=[anyspec] * n,
        scratch_shapes=[pltpu.SemaphoreType.DMA((3 * n,))] * 2 + [pltpu.SemaphoreType.DMA((n,))], name=name)(*parts)


def _rs_pair_share(fulls, *, name):
    n = len(fulls)

    def body(*refs):
        ins, outs = refs[:n], refs[n:2 * n]
        send_sems, recv_sems = refs[2 * n:]
        x, y, c = _place()

        def copy(t, half):
            return pltpu.make_async_remote_copy(src_ref=ins[t].at[c], dst_ref=outs[t].at[half], send_sem=send_sems.at[t],
                                                recv_sem=recv_sems.at[t], device_id=(x, y, 1 - c), device_id_type=MESH)

        sends = [copy(t, c) for t in range(n)]
        for cp in sends:
            cp.start()
        for t in range(n):
            copy(t, 1 - c).wait_recv()
        for cp in sends:
            cp.wait_send()

    anyspec = pl.BlockSpec(memory_space=pl.ANY)
    return pl.pallas_call(
        body, out_shape=[SDS(f.shape, f.dtype) for f in fulls], in_specs=[anyspec] * n, out_specs=[anyspec] * n,
        input_output_aliases={t: t for t in range(n)},
        scratch_shapes=[pltpu.SemaphoreType.DMA((n,))] * 2, name=name)(*fulls)


def _add_half(g, other, c1, *, out_dtype, name):
    _, _, h, cs = g.shape
    tr = _tile(h, 512, 16)

    def body(c_ref, g_ref, o_ref, out_ref):
        out_ref[...] = (g_ref[...] + o_ref[...]).astype(out_dtype)

    spec = pl.BlockSpec((None, tr, cs), lambda k, i, c: (k, i, 0))
    return pl.pallas_call(
        body, out_shape=SDS(other.shape, out_dtype),
        grid_spec=pltpu.PrefetchScalarGridSpec(
            num_scalar_prefetch=1, grid=(4, h // tr),
            in_specs=[pl.BlockSpec((None, None, tr, cs), lambda k, i, c: (k, c[0], i, 0)), spec], out_specs=spec),
        compiler_params=_params(("parallel", "parallel")), name=name)(c1, g, other)


def _sum4(x, c1, *, name):
    n, h, cs = x.shape
    tr = _tile(h, 512, 16)

    def body(c_ref, x_ref, o_ref):
        tot = x_ref[0].astype(F32)
        for s in range(1, n):
            tot = tot + x_ref[s].astype(F32)
        o_ref[...] = tot

    return pl.pallas_call(
        body, out_shape=SDS((2, h, cs), F32),
        grid_spec=pltpu.PrefetchScalarGridSpec(
            num_scalar_prefetch=1, grid=(h // tr,),
            in_specs=[pl.BlockSpec((n, tr, cs), lambda i, c: (0, i, 0))],
            out_specs=pl.BlockSpec((None, tr, cs), lambda i, c: (c[0], i, 0))),
        compiler_params=_params(("parallel",)), name=name)(c1, x)


def _finish_small(gathered, lbf, lbb, *, rows, name):
    r_lbf, r_lbb = rows['lb_f'], rows['lb_b']

    def body(g_ref, lbf_ref, lbb_ref, o_ref, dlf_ref, dlb_ref):
        tot = g_ref[0]
        for s in range(1, 8):
            tot = tot + g_ref[s]
        o_ref[...] = tot
        o_ref[0:1, :] = jnp.broadcast_to(jnp.sum(o_ref[0:1, :], axis=1, keepdims=True), (1, LANE))
        for lb_ref, d_ref, r0 in ((lbf_ref, dlf_ref, r_lbf), (lbb_ref, dlb_ref, r_lbb)):
            for hh in range(HG_HEADS):
                sl = slice(LANE * hh, LANE * (hh + 1))
                l0, l1 = lb_ref[0:1, sl], lb_ref[1:2, sl]
                mx = jnp.maximum(l0, l1)
                e0, e1 = jnp.exp(l0 - mx), jnp.exp(l1 - mx)
                p0 = e0 / (e0 + e1)
                d0 = o_ref[r0 + hh:r0 + hh + 1, :] * p0 * (1.0 - p0)
                d_ref[0:1, sl] = d0
                d_ref[1:2, sl] = -d0

    vm = pl.BlockSpec(memory_space=pltpu.VMEM)
    return pl.pallas_call(
        body, in_specs=[vm, vm, vm], out_specs=[vm, vm, vm],
        out_shape=[SDS(gathered.shape[1:], F32), SDS(lbf.shape, F32), SDS(lbb.shape, F32)], name=name)(gathered, lbf, lbb)


def _local_step(x2, tgt2, meta, W, S):
    T, D = x2.shape
    L = PAD + T
    h0 = jnp.concatenate([jnp.zeros((PAD - N_META, D), F32), meta, x2], axis=0)

    qk0 = Z_HG
    w_in = jnp.concatenate([W['w_in'][:, :qk0], _qk_to_group(W['w_in'][:, qk0:qk0 + AT_W + AT_KVW]),
                            W['w_in'][:, qk0 + AT_W + AT_KVW:]], axis=1)
    cc, ss = _rope_tables(L)
    wq_g, wk_g = _group_vec(S['q_norm']), _group_vec(S['k_norm'])

    def ffn_fwd(h, nw, wg, wu, wd, tag):
        n = _rmsnorm_fwd(h, nw, name=tag + "_norm")
        g, u, a = _ffn4_up(n, wg, wu, name=tag + "_up")
        hn = _ffn4_down(a, wd, h, name=tag + "_down")
        return hn, (n, g, u, a)

    def ffn_bwd(dh, h, nw, wg, wu, wd, saved, tag, split=False):
        n, g, u, a = saved
        dg, du = _ffn4_dact(dh, wd, g, u, name=tag + "_dact")
        dn = _ffn4_dn(dg, du, wg, wu, name=tag + "_dn")
        dwg = _ffn4_dw(n, dg, x_is_rows=True, name=tag + "_dwg")
        dwu = _ffn4_dw(n, du, x_is_rows=True, name=tag + "_dwu")
        dwd = _ffn4_dw(dh, a, x_is_rows=False, alpha=0.5, name=tag + "_dwd")
        *dhp, dnw = _rmsnorm_bwd(h, nw, dn, dh, split=split, name=tag + "_norm_bwd")
        return (dhp if split else dhp[0]), dnw, dwg, dwu, dwd

    h1, sv1 = ffn_fwd(h0, S['ffn1_norm'], W['ffn1_w_gate'], W['ffn1_w_up'], W['ffn1_w_down'], "ffn1")
    um = _rmsnorm_fwd(h1, S['mix_norm'], name="mix_norm")
    z = _mm([(um, w_in)], tm=512, tn=1792, tk=D, name="in_proj")
    of, sf = _hg_fwd(z, S['hg_lb_fwd'], rev=False, name="hg_fwd_f")
    ob, sb = _hg_fwd(z, S['hg_lb_bwd'], rev=True, name="hg_fwd_b")
    ya = _hg_post_fwd(of, ob, z, S['hg_out_norm'], name="hg_post")
    qm, qt, kr, krt, vb, vt = _at_prep(z, cc, ss, wq_g, wk_g, name="at_prep")
    yb, yb_f32, lse = _at_fwd(qt, kr, vt, name="at_fwd")
    mixed = _merge_fwd(ya, yb, W['w_up_a'], W['w_up_b'], z, name="merge")
    h2 = _mm([(mixed, W['w_out'])], res=h1, tm=512, tn=D, tk=D, name="out_proj")
    h3, sv2 = ffn_fwd(h2, S['ffn2_norm'], W['ffn2_w_gate'], W['ffn2_w_up'], W['ffn2_w_down'], "ffn2")
    dh3, loss_lanes = _loss_head(h3, tgt2, name="loss_head")

    G = {}
    dh2, dn_ffn2, G['ffn2_w_gate'], G['ffn2_w_up'], G['ffn2_w_down'] = ffn_bwd(
        dh3, h2, S['ffn2_norm'], W['ffn2_w_gate'], W['ffn2_w_up'], W['ffn2_w_down'], sv2, "ffn2")
    dpa, dpb, dzga, dzgb = _merge_bwd(dh2, W['w_out'], ya, yb, W['w_up_a'], W['w_up_b'], z, name="merge_bwd")
    G['w_out'] = _mm([(mixed, dh2)], ta=True, tm=D, tn=D, tk=512, name="d_w_out")
    dya = _mm([(dpa, W['w_up_a'])], tb=True, tm=512, tn=HG_W, tk=D, name="d_ya")
    dyb = _mm([(dpb, W['w_up_b'])], tb=True, tm=512, tn=AT_W, tk=D, name="d_yb")
    G['w_up_a'] = _mm([(ya, dpa)], ta=True, tm=HG_W, tn=D, tk=512, name="d_w_up_a")
    G['w_up_b'] = _mm([(yb, dpb)], ta=True, tm=AT_W, tn=D, tk=512, name="d_w_up_b")
    do_hg, dzg, d_hgn = _hg_post_bwd(dya, of, ob, z, S['hg_out_norm'], name="hg_post_bwd")
    dq_f, dv_f, dzf_f, dlb_f = _hg_bwd(z, S['hg_lb_fwd'], do_hg, sf, None, rev=False, name="hg_bwd_f")
    dzq, dzi, dzf_b, dlb_b = _hg_bwd(z, S['hg_lb_bwd'], do_hg, sb, (dq_f, dv_f), rev=True, name="hg_bwd_b")
    dqm, dk2, dv2 = _at_bwd(qm, qt, kr, krt, vb, dyb, yb_f32, lse, name="at_bwd")
    dz_at, dwq_g, dwk_g = _at_prep_bwd(dqm, dk2, dv2, z, cc, ss, wq_g, wk_g, name="at_prep_bwd")
    dz = jnp.concatenate([dzq, dzi, dzf_f, dzf_b, dzg, dz_at, dzga, dzgb], axis=1)
    dum = _mm([(dz, w_in)], tb=True, tm=512, tn=D, tk=1792, name="d_um")
    dw_in_p = _mm([(um, dz)], ta=True, tm=D, tn=1792, tk=512, name="d_w_in")
    G['w_in'] = jnp.concatenate([dw_in_p[:, :qk0], _qk_from_group(dw_in_p[:, qk0:qk0 + AT_W + AT_KVW]),
                                 dw_in_p[:, qk0 + AT_W + AT_KVW:]], axis=1)
    dh1, dn_mix = _rmsnorm_bwd(h1, S['mix_norm'], dum, dh2, name="mix_norm_bwd")
    (grad_x, dmeta), dn_ffn1, G['ffn1_w_gate'], G['ffn1_w_up'], G['ffn1_w_down'] = ffn_bwd(
        dh1, h0, S['ffn1_norm'], W['ffn1_w_gate'], W['ffn1_w_up'], W['ffn1_w_down'], sv1, "ffn1", split=True)

    small_rows = [('loss', loss_lanes), ('ffn1_norm', dn_ffn1.reshape(-1, LANE)), ('mix_norm', dn_mix.reshape(-1, LANE)),
                  ('ffn2_norm', dn_ffn2.reshape(-1, LANE)), ('hg_out_norm', d_hgn.reshape(-1, LANE)),
                  ('lb_f', dlb_f.reshape(-1, LANE)), ('lb_b', dlb_b.reshape(-1, LANE)), ('q_norm', dwq_g), ('k_norm', dwk_g)]
    return grad_x, dmeta, G, small_rows


def kernel(x, meta_tokens, ffn1_norm, ffn1_w_gate, ffn1_w_up, ffn1_w_down, mix_norm, w_in, hg_lb_fwd, hg_lb_bwd, hg_out_norm, q_norm, k_norm, w_up_a, w_up_b, w_out, ffn2_norm, ffn2_w_gate, ffn2_w_up, ffn2_w_down, loss_target, m_meta_tokens, m_ffn1_norm, m_ffn1_w_gate, m_ffn1_w_up, m_ffn1_w_down, m_mix_norm, m_w_in, m_hg_lb_fwd, m_hg_lb_bwd, m_hg_out_norm, m_q_norm, m_k_norm, m_w_up_a, m_w_up_b, m_w_out, m_ffn2_norm, m_ffn2_w_gate, m_ffn2_w_up, m_ffn2_w_down, v_meta_tokens, v_ffn1_norm, v_ffn1_w_gate, v_ffn1_w_up, v_ffn1_w_down, v_mix_norm, v_w_in, v_hg_lb_fwd, v_hg_lb_bwd, v_hg_out_norm, v_q_norm, v_k_norm, v_w_up_a, v_w_up_b, v_w_out, v_ffn2_norm, v_ffn2_w_gate, v_ffn2_w_up, v_ffn2_w_down):
    given = dict(locals())
    w = {n: given[n] for n in WEIGHTS}
    mom = {n: given["m_" + n] for n in WEIGHTS}
    var = {n: given["v_" + n] for n in WEIGHTS}
    c = lax.axis_index("c")
    D = x.shape[-1]

    shapes = {n: w[n].shape[-2:] for n in MATS + ('meta_tokens',)}
    halves = [w[n].astype(BF16).reshape(2, shapes[n][0] // 2, shapes[n][1]) for n in MATS]
    gathered = _gather_mats(halves, name="gather_weights")
    s_me = 2 * lax.axis_index("x") + lax.axis_index("y")
    W = {}
    for n, hv, g4 in zip(MATS, halves, gathered):
        r, cs = shapes[n]
        g4 = lax.dynamic_update_index_in_dim(g4, hv, s_me, 0).reshape(4, r, cs)
        if n in FFN_MATS:
            W[n] = g4
        elif n in ROW_SHARDED:
            W[n] = g4.reshape(4 * r, cs)
        else:
            W[n] = g4.transpose(1, 0, 2).reshape(r, 4 * cs)
    meta_rows = w['meta_tokens'].reshape(-1, LANE)
    mg = _allgather_small(meta_rows, name="gather_meta").reshape(4, 2, N_META, -1)[:, 0]
    meta = mg.transpose(1, 0, 2).reshape(N_META, D)
    S = {n: w[n] for n in SMALLS}

    grad_x, dmeta, G, small_rows = _local_step(x[0], loss_target[0], meta, W, S)
    G['meta_tokens'] = dmeta

    names = MATS + ('meta_tokens',)
    views = []
    for n in names:
        r, cs = shapes[n]
        if n in FFN_MATS:
            g4 = G[n]
        elif n in ROW_SHARDED:
            g4 = G[n].reshape(4, r, cs)
        else:
            g4 = G[n].reshape(r, 4, cs).transpose(1, 0, 2)
        views.append(g4.reshape(4, 2, r // 2, cs))
    c1 = c.astype(jnp.int32).reshape(1)
    from_sibling = _rs_pair_exchange(views, name="rs_pair_exchange")
    parts = [_add_half(v, o, c1, out_dtype=F32 if n == 'meta_tokens' else BF16, name="rs_pair_sum_" + n)
             for n, v, o in zip(names, views, from_sibling)]
    slabs = _rs_chip_exchange(parts, name="rs_chip_exchange")
    reds = [_sum4(s, c1, name="rs_chip_sum_" + n) for n, s in zip(names, slabs)]
    both = _rs_pair_share(reds, name="rs_pair_share")
    grads = {n: b.reshape(w[n].shape) for n, b in zip(names, both)}

    rows, off = {}, 0
    for nme, blk in small_rows:
        rows[nme] = off
        off += blk.shape[0]
    block = jnp.concatenate([blk for _, blk in small_rows], axis=0)
    n_rows = (off + 7) // 8 * 8
    block = jnp.pad(block, ((0, n_rows - off), (0, 0)))
    allsmall = _allgather_small(block, name="gather_small").reshape(8, n_rows, LANE)
    tot, d_lbf, d_lbb = _finish_small(allsmall, w['hg_lb_fwd'], w['hg_lb_bwd'], rows=rows, name="finish_small")
    loss = 0.5 * tot[0, 0] / D

    def small(nme, shape):
        r0 = rows[nme]
        return tot[r0:r0 + shape[-1] // LANE].reshape(shape)

    grads['ffn1_norm'] = small('ffn1_norm', w['ffn1_norm'].shape)
    grads['mix_norm'] = small('mix_norm', w['mix_norm'].shape)
    grads['ffn2_norm'] = small('ffn2_norm', w['ffn2_norm'].shape)
    grads['hg_out_norm'] = small('hg_out_norm', w['hg_out_norm'].shape)
    grads['hg_lb_fwd'] = d_lbf
    grads['hg_lb_bwd'] = d_lbb
    grads['q_norm'] = _ungroup_vec(tot[rows['q_norm']])
    grads['k_norm'] = _ungroup_vec(tot[rows['k_norm']])

    delta, new_m, new_v = {}, {}, {}
    for n in WEIGHTS:
        delta[n], new_m[n], new_v[n] = _adamw(w[n], grads[n], mom[n], var[n], name="adamw_" + n)
    return (loss, grad_x[None], *[grads[n] for n in WEIGHTS], *[delta[n] for n in WEIGHTS],
            *[new_m[n] for n in WEIGHTS], *[new_v[n] for n in WEIGHTS])
```

```python
import numpy as np
import jax
import jax.numpy as jnp
from jax import lax
from jax.experimental import pallas as pl
from jax.experimental.pallas import tpu as pltpu

F32 = jnp.float32
BF16 = jnp.bfloat16
SDS = jax.ShapeDtypeStruct
MESH = pl.DeviceIdType.MESH

EPS = 1e-6
N_META = 16
PAD = 512
LANE = 128
CHUNK = 128
HG_HEADS = 4
HG_W = HG_HEADS * 128
AT_HEADS = 8
AT_KV = 2
AT_HD = 64
AT_W = AT_HEADS * AT_HD
AT_KVW = AT_KV * AT_HD
VT_ROWS = AT_HD + 16
FWD_CHUNKS_PER_STEP = 4
GRID_W = 64
ROPE_THETA = 10000.0
Z_HG = 5 * HG_W
Z_AT = AT_W + 2 * AT_KVW
ADAM_LR, ADAM_B1, ADAM_B2, ADAM_EPS, ADAM_WD, ADAM_STEP = 0.001, 0.9, 0.999, 1e-08, 0.01, 10
VMEM_DEFAULT = 48 * 1024 * 1024
VMEM_LARGE = 60 * 1024 * 1024
NEG = -1e30

MATS = ('ffn1_w_gate', 'ffn1_w_up', 'ffn1_w_down', 'w_in', 'w_up_a', 'w_up_b', 'w_out',
        'ffn2_w_gate', 'ffn2_w_up', 'ffn2_w_down')
ROW_SHARDED = ('ffn1_w_down', 'w_out', 'ffn2_w_down')
FFN_MATS = ('ffn1_w_gate', 'ffn1_w_up', 'ffn1_w_down', 'ffn2_w_gate', 'ffn2_w_up', 'ffn2_w_down')
SMALLS = ('ffn1_norm', 'mix_norm', 'hg_lb_fwd', 'hg_lb_bwd', 'hg_out_norm', 'q_norm', 'k_norm', 'ffn2_norm')
WEIGHTS = ('meta_tokens', 'ffn1_norm', 'ffn1_w_gate', 'ffn1_w_up', 'ffn1_w_down', 'mix_norm', 'w_in', 'hg_lb_fwd',
           'hg_lb_bwd', 'hg_out_norm', 'q_norm', 'k_norm', 'w_up_a', 'w_up_b', 'w_out', 'ffn2_norm', 'ffn2_w_gate',
           'ffn2_w_up', 'ffn2_w_down')


def _params(sem=None, vmem=VMEM_DEFAULT):
    return pltpu.CompilerParams(dimension_semantics=sem, vmem_limit_bytes=vmem)


def _tile(n, pref, q=LANE):
    for d in range(min(pref, n), 0, -1):
        if n % d == 0 and d % q == 0:
            return d
    return n


def _sigmoid(x):
    return 0.5 * jnp.tanh(0.5 * x) + 0.5


def _dot(a, b, dims):
    return lax.dot_general(a, b, (dims, ((), ())), preferred_element_type=F32)


def _nn(a, b):
    return _dot(a, b, ((1,), (0,)))


def _nt(a, b):
    return _dot(a, b, ((1,), (1,)))


def _tn(a, b):
    return _dot(a, b, ((0,), (0,)))


def _split3(x):
    x1 = x.astype(BF16)
    r = x - x1.astype(F32)
    x2 = r.astype(BF16)
    x3 = (r - x2.astype(F32)).astype(BF16)
    return x1, x2, x3


def _exact_left(m01, x):
    x1, x2, x3 = _split3(x)
    return _nn(m01, x1) + _nn(m01, x2) + _nn(m01, x3)


def _exact_right(x, m01):
    x1, x2, x3 = _split3(x)
    return _nn(x1, m01) + _nn(x2, m01) + _nn(x3, m01)


def _mm(pairs, *, name, ta=False, tb=False, out_dtype=F32, tm=512, tn=1024, tk=1024, alpha=1.0, res=None):
    a0, b0 = pairs[0]
    M = a0.shape[1] if ta else a0.shape[0]
    K = a0.shape[0] if ta else a0.shape[1]
    N = b0.shape[0] if tb else b0.shape[1]
    tm, tn, tk = _tile(M, tm), _tile(N, tn), _tile(K, tk)
    nk = K // tk
    npair = len(pairs)
    dims = ((0 if ta else 1,), (1 if tb else 0,))

    def body(*refs):
        ab = refs[:2 * npair]
        pos = 2 * npair
        res_ref = None
        if res is not None:
            res_ref = refs[pos]
            pos += 1
        o_ref = refs[pos]

        def partial_sum():
            tot = None
            for p in range(npair):
                d = _dot(ab[2 * p][...].astype(BF16), ab[2 * p + 1][...].astype(BF16), dims)
                tot = d if tot is None else tot + d
            return tot

        def finish(acc):
            r = acc if alpha == 1.0 else acc * alpha
            if res_ref is not None:
                r = res_ref[...] + r
            o_ref[...] = r.astype(out_dtype)

        if nk == 1:
            finish(partial_sum())
        else:
            acc_ref = refs[pos + 1]
            k = pl.program_id(2)

            @pl.when(k == 0)
            def _():
                acc_ref[...] = jnp.zeros_like(acc_ref)

            acc_ref[...] += partial_sum()

            @pl.when(k == nk - 1)
            def _():
                finish(acc_ref[...])

    a_spec = pl.BlockSpec((tk, tm), lambda j, i, k: (k, i)) if ta else pl.BlockSpec((tm, tk), lambda j, i, k: (i, k))
    b_spec = pl.BlockSpec((tn, tk), lambda j, i, k: (j, k)) if tb else pl.BlockSpec((tk, tn), lambda j, i, k: (k, j))
    o_spec = pl.BlockSpec((tm, tn), lambda j, i, k: (i, j))
    in_specs, args = [], []
    for a, b in pairs:
        in_specs += [a_spec, b_spec]
        args += [a, b]
    if res is not None:
        in_specs.append(o_spec)
        args.append(res)
    return pl.pallas_call(
        body, grid=(N // tn, M // tm, nk), in_specs=in_specs, out_specs=o_spec,
        out_shape=SDS((M, N), out_dtype),
        scratch_shapes=[pltpu.VMEM((tm, tn), F32)] if nk > 1 else [],
        compiler_params=_params(("parallel", "parallel", "arbitrary")), name=name)(*args)


def _rmsnorm_fwd(h, w, *, name):
    L, D = h.shape
    tm = _tile(L, 512)

    def body(h_ref, w_ref, o_ref):
        x = h_ref[...]
        r = lax.rsqrt(jnp.mean(x * x, axis=-1, keepdims=True) + EPS)
        o_ref[...] = (x * r * w_ref[...]).astype(BF16)

    return pl.pallas_call(
        body, grid=(L // tm,),
        in_specs=[pl.BlockSpec((tm, D), lambda i: (i, 0)), pl.BlockSpec((1, D), lambda i: (0, 0))],
        out_specs=pl.BlockSpec((tm, D), lambda i: (i, 0)), out_shape=SDS((L, D), BF16),
        compiler_params=_params(("parallel",)), name=name)(h, w)


def _rmsnorm_bwd(h, w, dn, dres, *, split=False, name):
    L, D = h.shape
    tm = PAD if split else _tile(L, 512)

    def body(h_ref, w_ref, dn_ref, dres_ref, dh_ref, *rest):
        dw_ref = rest[-1]
        i = pl.program_id(0)
        x = h_ref[...]
        r = lax.rsqrt(jnp.mean(x * x, axis=-1, keepdims=True) + EPS)
        xh = x * r
        dn = dn_ref[...]
        dxh = dn * w_ref[...]
        dh = dres_ref[...] + r * (dxh - xh * jnp.mean(dxh * xh, axis=-1, keepdims=True))
        dh_ref[...] = dh

        @pl.when(i == 0)
        def _():
            dw_ref[...] = jnp.zeros_like(dw_ref)
            if split:
                rest[0][...] = dh[PAD - N_META:]

        dw_ref[...] += jnp.sum(dn * xh, axis=0, keepdims=True)

    row = pl.BlockSpec((tm, D), lambda i: (i, 0))
    vec = pl.BlockSpec((1, D), lambda i: (0, 0))
    if split:
        out_specs = [pl.BlockSpec((tm, D), lambda i: (jnp.maximum(i - 1, 0), 0)), pl.BlockSpec((N_META, D), lambda i: (0, 0)), vec]
        out_shape = [SDS((L - PAD, D), F32), SDS((N_META, D), F32), SDS((1, D), F32)]
    else:
        out_specs, out_shape = [row, vec], [SDS((L, D), F32), SDS((1, D), F32)]
    return pl.pallas_call(
        body, grid=(L // tm,), in_specs=[row, vec, row, row], out_specs=out_specs, out_shape=out_shape,
        compiler_params=_params(("arbitrary",)), name=name)(h, w, dn, dres)


def _ffn4_up(n, wg4, wu4, *, name):
    L, D = n.shape
    ns, _, cs = wg4.shape
    tm = _tile(L, 768)

    def body(n_ref, wg_ref, wu_ref, ag_ref, au_ref, a_ref):
        x = n_ref[...]
        g = _nn(x, wg_ref[...])
        u = _nn(x, wu_ref[...])
        sg = _sigmoid(g)
        silu = g * sg
        ag_ref[...] = (u * (sg * (1.0 + g * (1.0 - sg)))).astype(BF16)
        au_ref[...] = silu.astype(BF16)
        a_ref[...] = (silu * u).astype(BF16)

    wspec = pl.BlockSpec((None, D, cs), lambda j, i: (j, 0, 0))
    ospec = pl.BlockSpec((None, tm, cs), lambda j, i: (j, i, 0))
    return pl.pallas_call(
        body, grid=(ns, L // tm),
        in_specs=[pl.BlockSpec((tm, D), lambda j, i: (i, 0)), wspec, wspec], out_specs=[ospec, ospec, ospec],
        out_shape=[SDS((ns, L, cs), BF16), SDS((ns, L, cs), BF16), SDS((ns, L, cs), BF16)],
        compiler_params=_params(("parallel", "parallel")), name=name)(n, wg4, wu4)


def _ffn4_down(a4, wd4, h, *, name):
    ns, L, cs = a4.shape
    D = wd4.shape[2]
    tm = _tile(L, 512)

    def body(a_ref, w_ref, h_ref, o_ref):
        acc = _nn(a_ref[0], w_ref[0])
        for j in range(1, ns):
            acc = acc + _nn(a_ref[j], w_ref[j])
        o_ref[...] = h_ref[...] + 0.5 * acc

    row = pl.BlockSpec((tm, D), lambda i: (i, 0))
    return pl.pallas_call(
        body, grid=(L // tm,),
        in_specs=[pl.BlockSpec((ns, tm, cs), lambda i: (0, i, 0)), pl.BlockSpec((ns, cs, D), lambda i: (0, 0, 0)), row],
        out_specs=row, out_shape=SDS((L, D), F32),
        compiler_params=_params(("parallel",)), name=name)(a4, wd4, h)


def _ffn4_dact(dh, wdt4, ag4, au4, *, name):
    L, D = dh.shape
    ns, _, cs = wdt4.shape
    tm = _tile(L, 768)

    def body(dh_ref, wd_ref, ag_ref, au_ref, dg_ref, du_ref):
        da = 0.5 * _nn(dh_ref[...].astype(BF16), wd_ref[...])
        dg_ref[...] = (da * ag_ref[...].astype(F32)).astype(BF16)
        du_ref[...] = (da * au_ref[...].astype(F32)).astype(BF16)

    ospec = pl.BlockSpec((None, tm, cs), lambda j, i: (j, i, 0))
    return pl.pallas_call(
        body, grid=(ns, L // tm),
        in_specs=[pl.BlockSpec((tm, D), lambda j, i: (i, 0)), pl.BlockSpec((None, D, cs), lambda j, i: (j, 0, 0)), ospec, ospec],
        out_specs=[ospec, ospec], out_shape=[SDS((ns, L, cs), BF16), SDS((ns, L, cs), BF16)],
        compiler_params=_params(("parallel", "parallel")), name=name)(dh, wdt4, ag4, au4)


def _ffn4_dn(dg4, du4, wg4, wu4, *, name):
    ns, L, cs = dg4.shape
    D = wg4.shape[1]
    tm = _tile(L, 512)

    def body(dg_ref, du_ref, wg_ref, wu_ref, o_ref):
        acc = None
        for j in range(ns):
            t = _nt(dg_ref[j], wg_ref[j]) + _nt(du_ref[j], wu_ref[j])
            acc = t if acc is None else acc + t
        o_ref[...] = acc

    aspec = pl.BlockSpec((ns, tm, cs), lambda i: (0, i, 0))
    wspec = pl.BlockSpec((ns, D, cs), lambda i: (0, 0, 0))
    return pl.pallas_call(
        body, grid=(L // tm,), in_specs=[aspec, aspec, wspec, wspec],
        out_specs=pl.BlockSpec((tm, D), lambda i: (i, 0)), out_shape=SDS((L, D), F32),
        compiler_params=_params(("parallel",), VMEM_LARGE), name=name)(dg4, du4, wg4, wu4)


def _ffn4_dw(x, y4, *, x_is_rows, alpha=1.0, name):
    L, D = x.shape
    ns, _, cs = y4.shape
    tk = _tile(L, 512)
    nk = L // tk
    oshape = (D, cs) if x_is_rows else (cs, D)

    def body(x_ref, y_ref, o_ref):
        k = pl.program_id(0)

        @pl.when(k == 0)
        def _():
            o_ref[...] = jnp.zeros_like(o_ref)

        xb = x_ref[...].astype(BF16)
        if x_is_rows:
            xt = xb.T
            for j in range(ns):
                o_ref[j] += _nn(xt, y_ref[j])
        else:
            for j in range(ns):
                o_ref[j] += _tn(y_ref[j], xb)

        if alpha != 1.0:
            @pl.when(k == nk - 1)
            def _():
                o_ref[...] = o_ref[...] * alpha

    return pl.pallas_call(
        body, grid=(nk,),
        in_specs=[pl.BlockSpec((tk, D), lambda k: (k, 0)), pl.BlockSpec((ns, tk, cs), lambda k: (0, k, 0))],
        out_specs=pl.BlockSpec((ns,) + oshape, lambda k: (0, 0, 0)), out_shape=SDS((ns,) + oshape, F32),
        compiler_params=_params(("arbitrary",)), name=name)(x, y4)


def _hg_masks(rev):
    t = lax.broadcasted_iota(jnp.int32, (CHUNK, CHUNK), 0)
    s = lax.broadcasted_iota(jnp.int32, (CHUNK, CHUNK), 1)
    causal = (s >= t) if rev else (s <= t)
    levels = []
    for sh in (6, 5, 4):
        same = jnp.right_shift(t, sh + 1) == jnp.right_shift(s, sh + 1)
        tq = jnp.bitwise_and(jnp.right_shift(t, sh), 1)
        sk = jnp.bitwise_and(jnp.right_shift(s, sh), 1)
        levels.append(same & (tq == (0 if rev else 1)) & (sk == (1 if rev else 0)))
    diag = (jnp.right_shift(t, 4) == jnp.right_shift(s, 4)) & causal
    return causal, levels, diag


def _hg_intra_factors(q, k, b, b_scr, rev):
    b_scr[...] = b
    row = lax.broadcasted_iota(jnp.int32, (CHUNK, LANE), 0)
    out = []
    for sh in (6, 5, 4):
        lb = 1 << sh
        pieces = []
        for p in range(0, CHUNK, 2 * lb):
            r = p + lb if rev else p + lb - 1
            pieces.append(jnp.broadcast_to(b_scr[pl.ds(r, 1), :], (2 * lb, LANE)))
        ref = pieces[0] if len(pieces) == 1 else jnp.concatenate(pieces, axis=0)
        qside = jnp.bitwise_and(jnp.right_shift(row, sh), 1) == (0 if rev else 1)
        d = b - ref
        e = jnp.exp(jnp.minimum(jnp.where(qside, d, -d), 0.0))
        eq = jnp.where(qside, e, 0.0)
        ek = jnp.where(qside, 0.0, e)
        out.append((eq, ek, (q * eq).astype(BF16), (k * ek).astype(BF16)))
    pieces = []
    for a in range(0, CHUNK, 16):
        r = a + (8 if rev else 7)
        pieces.append(jnp.broadcast_to(b_scr[pl.ds(r, 1), :], (16, LANE)))
    ref = jnp.concatenate(pieces, axis=0)
    eq = jnp.exp(jnp.minimum(b - ref, 80.0))
    ek = jnp.exp(jnp.minimum(ref - b, 80.0))
    out.append((eq, ek, (q * eq).astype(BF16), (k * ek).astype(BF16)))
    return out


def _hg_gate(zf, l0, l1, valid):
    mx = jnp.maximum(l0, l1)
    e0, e1 = jnp.exp(l0 - mx), jnp.exp(l1 - mx)
    p0 = e0 / (e0 + e1)
    sg = _sigmoid(-zf)
    k = jnp.where(valid, (1.0 - p0) * sg, 0.0)
    return p0, sg, k, jnp.log(1.0 - k)


def _hg_fwd(z, lbp, *, rev, name):
    L = z.shape[0]
    nc = L // CHUNK
    fcol = 3 if rev else 2

    def cidx(j):
        return nc - 1 - j if rev else j

    def body(zq_ref, zi_ref, zf_ref, lb_ref, o_ref, ssave_ref, st_scr, b_scr):
        j = pl.program_id(0)

        @pl.when(j == 0)
        def _():
            st_scr[...] = jnp.zeros_like(st_scr)

        causal, lmasks, dmask = _hg_masks(rev)
        tri = jnp.where(causal, 1.0, 0.0).astype(BF16)
        rowg = cidx(j) * CHUNK + lax.broadcasted_iota(jnp.int32, (CHUNK, LANE), 0)
        valid = rowg >= PAD - N_META
        last = 0 if rev else CHUNK - 1
        for hh in range(HG_HEADS):
            sl = slice(LANE * hh, LANE * (hh + 1))
            zq = zq_ref[:, sl]
            q = zq * _sigmoid(zq)
            v = zi_ref[:, sl].astype(BF16)
            _, _, k, g = _hg_gate(zf_ref[:, sl], lb_ref[0:1, sl], lb_ref[1:2, sl], valid)
            b = _exact_left(tri, g)
            st = st_scr[hh]
            ssave_ref[0, hh] = st
            o = _nt((q * jnp.exp(b)).astype(BF16), st.astype(BF16))
            a = None
            fac = _hg_intra_factors(q, k, b, b_scr, rev)
            for (eq, ek, qq, kk), msk in zip(fac, lmasks + [dmask]):
                t = jnp.where(msk, _nt(qq, kk), 0.0)
                a = t if a is None else a + t
            o_ref[:, sl] = o + _nn(a.astype(BF16), v)
            bl = b_scr[pl.ds(last, 1), :]
            kd = (k * jnp.exp(bl - b)).astype(BF16)
            st_scr[hh] = st * jnp.exp(bl) + _tn(v, kd)

    zspec = lambda col: pl.BlockSpec((CHUNK, HG_W), lambda j: (cidx(j), col))
    return pl.pallas_call(
        body, grid=(nc,),
        in_specs=[zspec(0), zspec(1), zspec(fcol), pl.BlockSpec((2, HG_W), lambda j: (0, 0))],
        out_specs=[pl.BlockSpec((CHUNK, HG_W), lambda j: (cidx(j), 0)),
                   pl.BlockSpec((1, HG_HEADS, LANE, LANE), lambda j: (cidx(j), 0, 0, 0))],
        out_shape=[SDS((L, HG_W), F32), SDS((nc, HG_HEADS, LANE, LANE), F32)],
        scratch_shapes=[pltpu.VMEM((HG_HEADS, LANE, LANE), F32), pltpu.VMEM((CHUNK, LANE), F32)],
        compiler_params=_params(("arbitrary",)), name=name)(z, z, z, lbp)


def _hg_bwd(z, lbp, do, ssave, prev, *, rev, name):
    L = z.shape[0]
    nc = L // CHUNK
    fcol = 3 if rev else 2
    final = prev is not None

    def cidx(j):
        return j if rev else nc - 1 - j

    def body(*refs):
        zq_ref, zi_ref, zf_ref, lb_ref, do_ref, ss_ref = refs[:6]
        pos = 6
        if final:
            dqin_ref, dvin_ref = refs[6:8]
            pos = 8
        dq_ref, dv_ref, dzf_ref, dlb_ref, dst_scr, b_scr = refs[pos:pos + 6]
        j = pl.program_id(0)

        @pl.when(j == 0)
        def _():
            dst_scr[...] = jnp.zeros_like(dst_scr)
            dlb_ref[...] = jnp.zeros_like(dlb_ref)

        causal, lmasks, dmask = _hg_masks(rev)
        tri = jnp.where(causal, 1.0, 0.0).astype(BF16)
        ti = lax.broadcasted_iota(jnp.int32, (CHUNK, CHUNK), 0)
        si = lax.broadcasted_iota(jnp.int32, (CHUNK, CHUNK), 1)
        tri_t = jnp.where((si <= ti) if rev else (si >= ti), 1.0, 0.0).astype(BF16)
        rowg = cidx(j) * CHUNK + lax.broadcasted_iota(jnp.int32, (CHUNK, LANE), 0)
        valid = rowg >= PAD - N_META
        last = 0 if rev else CHUNK - 1
        for hh in range(HG_HEADS):
            sl = slice(LANE * hh, LANE * (hh + 1))
            zq = zq_ref[:, sl]
            sq = _sigmoid(zq)
            q = zq * sq
            v = zi_ref[:, sl].astype(BF16)
            p0, sg, k, g = _hg_gate(zf_ref[:, sl], lb_ref[0:1, sl], lb_ref[1:2, sl], valid)
            b = _exact_left(tri, g)
            dob = do_ref[:, sl].astype(BF16)
            st = ss_ref[0, hh]
            dst = dst_scr[hh]
            stb, dstb = st.astype(BF16), dst.astype(BF16)
            eb = jnp.exp(b)
            qe = (q * eb).astype(BF16)
            fac = _hg_intra_factors(q, k, b, b_scr, rev)
            bl = b_scr[pl.ds(last, 1), :]
            ebl = jnp.exp(bl)
            kde = jnp.exp(bl - b)
            kd = (k * kde).astype(BF16)
            da = jnp.where(causal, _nt(dob, v), 0.0)
            dq = eb * _nn(dob, stb)
            dk_inter = kde * _nn(v, dstb)
            dk = dk_inter
            dv = _nt(kd, dstb)
            a = None
            db = q * dq - k * dk
            for (eq, ek, qq, kk), msk in zip(fac, lmasks + [dmask]):
                t = jnp.where(msk, _nt(qq, kk), 0.0)
                a = t if a is None else a + t
                dal = jnp.where(msk, da, 0.0).astype(BF16)
                mq = _nn(dal, kk)
                mk = _tn(dal, qq)
                dq = dq + eq * mq
                dk = dk + ek * mk
                db = db + (qq.astype(F32) * mq - kk.astype(F32) * mk)
            dv = dv + _tn(a.astype(BF16), dob)
            extra = ebl * jnp.sum(st * dst, axis=0, keepdims=True) + jnp.sum(k * dk_inter, axis=0, keepdims=True)
            dst_scr[hh] = dst * ebl + _tn(dob, qe)
            dg = _exact_left(tri_t, db) + extra
            dk_tot = dk - dg / (1.0 - k)
            dzf_ref[:, sl] = jnp.where(valid, dk_tot * (1.0 - p0) * (-sg * (1.0 - sg)), 0.0).astype(BF16)
            dlb_ref[:, sl] += jnp.sum(jnp.where(valid, -sg * dk_tot, 0.0), axis=0, keepdims=True)
            if final:
                dq_ref[:, sl] = ((dq + dqin_ref[:, sl]) * (sq * (1.0 + zq * (1.0 - sq)))).astype(BF16)
                dv_ref[:, sl] = (dv + dvin_ref[:, sl]).astype(BF16)
            else:
                dq_ref[:, sl] = dq
                dv_ref[:, sl] = dv

    zspec = lambda col: pl.BlockSpec((CHUNK, HG_W), lambda j: (cidx(j), col))
    rspec = pl.BlockSpec((CHUNK, HG_W), lambda j: (cidx(j), 0))
    in_specs = [zspec(0), zspec(1), zspec(fcol), pl.BlockSpec((2, HG_W), lambda j: (0, 0)), rspec,
                pl.BlockSpec((1, HG_HEADS, LANE, LANE), lambda j: (cidx(j), 0, 0, 0))]
    args = [z, z, z, lbp, do, ssave]
    if final:
        in_specs += [rspec, rspec]
        args += list(prev)
    odt = BF16 if final else F32
    return pl.pallas_call(
        body, grid=(nc,), in_specs=in_specs,
        out_specs=[rspec, rspec, rspec, pl.BlockSpec((1, HG_W), lambda j: (0, 0))],
        out_shape=[SDS((L, HG_W), odt), SDS((L, HG_W), odt), SDS((L, HG_W), BF16), SDS((1, HG_W), F32)],
        scratch_shapes=[pltpu.VMEM((HG_HEADS, LANE, LANE), F32), pltpu.VMEM((CHUNK, LANE), F32)],
        compiler_params=_params(("arbitrary",)), name=name)(*args)


def _hg_post_fwd(of, ob, z, w, *, name):
    L = of.shape[0]
    tm = _tile(L, 512)

    def body(of_ref, ob_ref, zg_ref, w_ref, y_ref):
        for hh in range(HG_HEADS):
            sl = slice(LANE * hh, LANE * (hh + 1))
            o = of_ref[:, sl] + ob_ref[:, sl]
            r = lax.rsqrt(jnp.mean(o * o, axis=-1, keepdims=True) + EPS)
            zg = zg_ref[:, sl]
            y_ref[:, sl] = (o * r * w_ref[:, sl] * (zg * _sigmoid(zg))).astype(BF16)

    row = pl.BlockSpec((tm, HG_W), lambda i: (i, 0))
    return pl.pallas_call(
        body, grid=(L // tm,),
        in_specs=[row, row, pl.BlockSpec((tm, HG_W), lambda i: (i, 4)), pl.BlockSpec((1, HG_W), lambda i: (0, 0))],
        out_specs=row, out_shape=SDS((L, HG_W), BF16),
        compiler_params=_params(("parallel",)), name=name)(of, ob, z, w)


def _hg_post_bwd(dy, of, ob, z, w, *, name):
    L = of.shape[0]
    tm = _tile(L, 512)

    def body(dy_ref, of_ref, ob_ref, zg_ref, w_ref, do_ref, dzg_ref, dw_ref):
        @pl.when(pl.program_id(0) == 0)
        def _():
            dw_ref[...] = jnp.zeros_like(dw_ref)

        for hh in range(HG_HEADS):
            sl = slice(LANE * hh, LANE * (hh + 1))
            o = of_ref[:, sl] + ob_ref[:, sl]
            r = lax.rsqrt(jnp.mean(o * o, axis=-1, keepdims=True) + EPS)
            xh = o * r
            zg = zg_ref[:, sl]
            sg = _sigmoid(zg)
            w = w_ref[:, sl]
            dy = dy_ref[:, sl]
            dys = dy * (zg * sg)
            dzg_ref[:, sl] = (dy * xh * w * (sg * (1.0 + zg * (1.0 - sg)))).astype(BF16)
            dw_ref[:, sl] += jnp.sum(dys * xh, axis=0, keepdims=True)
            dxh = dys * w
            do_ref[:, sl] = r * (dxh - xh * jnp.mean(dxh * xh, axis=-1, keepdims=True))

    row = pl.BlockSpec((tm, HG_W), lambda i: (i, 0))
    vec = pl.BlockSpec((1, HG_W), lambda i: (0, 0))
    return pl.pallas_call(
        body, grid=(L // tm,),
        in_specs=[row, row, row, pl.BlockSpec((tm, HG_W), lambda i: (i, 4)), vec],
        out_specs=[row, row, vec],
        out_shape=[SDS((L, HG_W), F32), SDS((L, HG_W), BF16), SDS((1, HG_W), F32)],
        compiler_params=_params(("arbitrary",)), name=name)(dy, of, ob, z, w)


N_GROUPS = (AT_HEADS + AT_KV) // 2


def _qk_to_group(wqk):
    d = wqk.shape[0]
    return wqk.reshape(d, N_GROUPS, 2, AT_HD // 2, 2).transpose(0, 1, 4, 2, 3).reshape(d, N_GROUPS * LANE)


def _qk_from_group(wqk):
    d = wqk.shape[0]
    return wqk.reshape(d, N_GROUPS, 2, 2, AT_HD // 2).transpose(0, 1, 3, 4, 2).reshape(d, N_GROUPS * LANE)


def _group_vec(w64):
    halves = w64.reshape(AT_HD // 2, 2).T
    return jnp.broadcast_to(halves[:, None, :], (2, 2, AT_HD // 2)).reshape(1, LANE)


def _ungroup_vec(w128):
    w = w128.reshape(2, 2, 32).sum(axis=1)
    return w.T.reshape(1, AT_HD)


def _rope_tables(L):
    n_real = L - PAD
    t = np.arange(n_real)
    row = np.concatenate([np.zeros(PAD), t // GRID_W]).astype(np.float32)
    col = np.concatenate([np.zeros(PAD), t % GRID_W]).astype(np.float32)
    inv = jnp.asarray(ROPE_THETA, F32) ** (-jnp.arange(0, AT_HD // 2, 2, dtype=F32) / (AT_HD // 2))
    ang = jnp.concatenate([jnp.asarray(row)[:, None] * inv, jnp.asarray(col)[:, None] * inv], axis=-1)
    cos, sin = jnp.cos(ang), jnp.sin(ang)
    cc = jnp.tile(cos, (1, 4))
    ss = jnp.concatenate([-sin, -sin, sin, sin], axis=1)
    return cc, ss


def _seg_matrix():
    a = lax.broadcasted_iota(jnp.int32, (LANE, LANE), 0)
    b = lax.broadcasted_iota(jnp.int32, (LANE, LANE), 1)
    same = jnp.bitwise_and(jnp.right_shift(a, 5), 1) == jnp.bitwise_and(jnp.right_shift(b, 5), 1)
    return jnp.where(same, 1.0, 0.0).astype(BF16)


def _slot_mask(shape, hp):
    lane = lax.broadcasted_iota(jnp.int32, shape, 1)
    return jnp.bitwise_and(jnp.right_shift(lane, 5), 1) == hp


def _at_prep(z, cc, ss, wq, wk, *, name):
    L = z.shape[0]
    tm = PAD
    qcol = Z_HG // AT_W
    kvcol = (Z_HG + AT_W) // (2 * LANE)

    def body(zq_ref, zkv_ref, cc_ref, ss_ref, wq_ref, wk_ref, qm_ref, qt_ref, kr_ref, krt_ref, vb_ref, vt_ref):
        seg = _seg_matrix()
        cc, ss = cc_ref[...], ss_ref[...]

        def normrope(x, w):
            r = lax.rsqrt(_exact_right(x * x, seg) * (1.0 / AT_HD) + EPS)
            y = x * r * w
            return y * cc + pltpu.roll(y, 64, 1) * ss

        for g in range(AT_HEADS // 2):
            o = normrope(zq_ref[:, LANE * g:LANE * (g + 1)], wq_ref[...]) * (AT_HD ** -0.5)
            for hp in range(2):
                h = 2 * g + hp
                tgt = h // (AT_HEADS // AT_KV)
                xm = jnp.where(_slot_mask(o.shape, hp), o, 0.0)
                if tgt != hp:
                    xm = pltpu.roll(xm, 32 if tgt == 1 else 96, 1)
                qm_ref[h] = xm.astype(BF16)
                qt_ref[h] = xm.T.astype(BF16)
        kr = normrope(zkv_ref[:, :LANE], wk_ref[...])
        kr_ref[...] = kr.astype(BF16)
        krt_ref[0] = kr.T.astype(BF16)
        v = zkv_ref[:, LANE:]
        low = lax.broadcasted_iota(jnp.int32, v.shape, 1) < AT_HD
        vb_ref[0] = jnp.where(low, v, 0.0).astype(BF16)
        vb_ref[1] = jnp.where(low, pltpu.roll(v, AT_HD, 1), 0.0).astype(BF16)
        vt = v.T.astype(BF16)
        ones = jnp.ones((VT_ROWS - AT_HD, tm), BF16)
        for j in range(AT_KV):
            vt_ref[j, 0, 0:AT_HD, :] = vt[AT_HD * j:AT_HD * (j + 1)]
            vt_ref[j, 0, AT_HD:VT_ROWS, :] = ones

    tab = pl.BlockSpec((tm, LANE), lambda i: (i, 0))
    vec = pl.BlockSpec((1, LANE), lambda i: (0, 0))
    nt = L // tm
    return pl.pallas_call(
        body, grid=(nt,),
        in_specs=[pl.BlockSpec((tm, AT_W), lambda i: (i, qcol)), pl.BlockSpec((tm, 2 * LANE), lambda i: (i, kvcol)),
                  tab, tab, vec, vec],
        out_specs=[pl.BlockSpec((AT_HEADS, tm, LANE), lambda i: (0, i, 0)),
                   pl.BlockSpec((AT_HEADS, LANE, tm), lambda i: (0, 0, i)), tab,
                   pl.BlockSpec((1, LANE, tm), lambda i: (i, 0, 0)),
                   pl.BlockSpec((AT_KV, tm, LANE), lambda i: (0, i, 0)),
                   pl.BlockSpec((AT_KV, 1, VT_ROWS, tm), lambda i: (0, i, 0, 0))],
        out_shape=[SDS((AT_HEADS, L, LANE), BF16), SDS((AT_HEADS, LANE, L), BF16), SDS((L, LANE), BF16),
                   SDS((nt, LANE, tm), BF16), SDS((AT_KV, L, LANE), BF16), SDS((AT_KV, nt, VT_ROWS, tm), BF16)],
        compiler_params=_params(("parallel",)), name=name)(z, z, cc, ss, wq, wk)


def _at_prep_bwd(dqm, dk2, dv2, z, cc, ss, wq, wk, *, name):
    L = z.shape[0]
    tm = PAD
    qcol = Z_HG // AT_W
    kvcol = (Z_HG + AT_W) // (2 * LANE)

    def body(dqm_ref, dk2_ref, dv2_ref, zq_ref, zkv_ref, cc_ref, ss_ref, wq_ref, wk_ref, dz_ref, dwq_ref, dwk_ref):
        @pl.when(pl.program_id(0) == 0)
        def _():
            dwq_ref[...] = jnp.zeros_like(dwq_ref)
            dwk_ref[...] = jnp.zeros_like(dwk_ref)

        seg = _seg_matrix()
        cc, ss = cc_ref[...], ss_ref[...]

        def back(x, w, do):
            dy = do * cc + pltpu.roll(do * ss, 64, 1)
            r = lax.rsqrt(_exact_right(x * x, seg) * (1.0 / AT_HD) + EPS)
            xh = x * r
            dxh = dy * w
            dx = r * (dxh - xh * (_exact_right(dxh * xh, seg) * (1.0 / AT_HD)))
            return dx, jnp.sum(dy * xh, axis=0, keepdims=True)

        for g in range(AT_HEADS // 2):
            do = None
            for hp in range(2):
                h = 2 * g + hp
                tgt = h // (AT_HEADS // AT_KV)
                d = jnp.where(_slot_mask((tm, LANE), tgt), dqm_ref[h], 0.0)
                if tgt != hp:
                    d = pltpu.roll(d, 96 if tgt == 1 else 32, 1)
                do = d if do is None else do + d
            dx, dw = back(zq_ref[:, LANE * g:LANE * (g + 1)], wq_ref[...], do * (AT_HD ** -0.5))
            dz_ref[:, LANE * g:LANE * (g + 1)] = dx.astype(BF16)
            dwq_ref[...] += dw
        dx, dw = back(zkv_ref[:, :LANE], wk_ref[...], dk2_ref[0] + dk2_ref[1])
        dz_ref[:, AT_W:AT_W + LANE] = dx.astype(BF16)
        dwk_ref[...] += dw
        dv0 = dv2_ref[0]
        low = lax.broadcasted_iota(jnp.int32, dv0.shape, 1) < AT_HD
        dz_ref[:, AT_W + LANE:] = jnp.where(low, dv0, pltpu.roll(dv2_ref[1], AT_HD, 1)).astype(BF16)

    tab = pl.BlockSpec((tm, LANE), lambda i: (i, 0))
    vec = pl.BlockSpec((1, LANE), lambda i: (0, 0))
    two = pl.BlockSpec((AT_KV, tm, LANE), lambda i: (0, i, 0))
    return pl.pallas_call(
        body, grid=(L // tm,),
        in_specs=[pl.BlockSpec((AT_HEADS, tm, LANE), lambda i: (0, i, 0)), two, two,
                  pl.BlockSpec((tm, AT_W), lambda i: (i, qcol)), pl.BlockSpec((tm, 2 * LANE), lambda i: (i, kvcol)),
                  tab, tab, vec, vec],
        out_specs=[pl.BlockSpec((tm, Z_AT), lambda i: (i, 0)), vec, vec],
        out_shape=[SDS((L, Z_AT), BF16), SDS((1, LANE), F32), SDS((1, LANE), F32)],
        compiler_params=_params(("arbitrary",)), name=name)(dqm, dk2, dv2, z, z, cc, ss, wq, wk)


def _at_fwd(qt, kr, vt, *, name):
    L = kr.shape[0]
    G = AT_HEADS // AT_KV
    tq = _tile(L, 384)
    tk = PAD
    nk = L // tk
    R = G * tq
    per = FWD_CHUNKS_PER_STEP if (nk - 1) % FWD_CHUNKS_PER_STEP == 0 else 1

    def body(q_ref, k_ref, v_ref, ob_ref, of_ref, lse_ref, m_scr, acc_scr):
        i = pl.program_id(1)
        qt = jnp.concatenate([q_ref[g] for g in range(G)], axis=1)
        m_scr[...] = jnp.full_like(m_scr, NEG)
        acc_scr[...] = jnp.zeros_like(acc_scr)

        def chunks(c, n, masked):
            start = c * tk if isinstance(c, int) else pl.multiple_of(c * tk, tk)
            st = _nn(k_ref[pl.ds(start, n * tk), :], qt).astype(BF16)
            if masked:
                key = lax.broadcasted_iota(jnp.int32, st.shape, 0)
                st = jnp.where(key >= PAD - N_META, st, NEG)
            m_prev = m_scr[...]
            m_new = jnp.maximum(m_prev, jnp.max(st, axis=0, keepdims=True).astype(F32))
            pt = jnp.exp(st - m_new.astype(BF16))
            acc = jnp.exp(m_prev - m_new) * acc_scr[...]
            for u in range(n):
                acc = acc + _nn(v_ref[0, c + u], pt[u * tk:(u + 1) * tk])
            acc_scr[...] = acc
            m_scr[...] = m_new

        chunks(0, 1, True)

        def loop(t, carry):
            chunks(1 + per * t, per, False)
            return carry

        lax.fori_loop(0, (nk - 1) // per, loop, 0)
        l = acc_scr[pl.ds(AT_HD, 1), :]
        lse = m_scr[...] + jnp.log(l)
        on = acc_scr[0:AT_HD, :] / l
        o = jnp.concatenate([on[:, g * tq:(g + 1) * tq] for g in range(G)], axis=0).T
        rowg = i * tq + lax.broadcasted_iota(jnp.int32, o.shape, 0)
        o = jnp.where(rowg >= PAD - N_META, o, 0.0)
        ob_ref[...] = o.astype(BF16)
        of_ref[...] = o
        for g in range(G):
            lse_ref[g] = lse[:, g * tq:(g + 1) * tq]

    ospec = pl.BlockSpec((tq, G * AT_HD), lambda j, i: (i, j))
    return pl.pallas_call(
        body, grid=(AT_KV, L // tq),
        in_specs=[pl.BlockSpec((G, LANE, tq), lambda j, i: (j, 0, i)), pl.BlockSpec((L, LANE), lambda j, i: (0, 0)),
                  pl.BlockSpec((1, nk, VT_ROWS, tk), lambda j, i: (j, 0, 0, 0))],
        out_specs=[ospec, ospec, pl.BlockSpec((G, 1, tq), lambda j, i: (j, 0, i))],
        out_shape=[SDS((L, AT_W), BF16), SDS((L, AT_W), F32), SDS((AT_HEADS, 1, L), F32)],
        scratch_shapes=[pltpu.VMEM((1, R), F32), pltpu.VMEM((VT_ROWS, R), F32)],
        compiler_params=_params(("parallel", "parallel")), name=name)(qt, kr, vt)


def _at_bwd(qm, qt, kr, krt, vb, do, of, lse, *, name):
    L = kr.shape[0]
    G = AT_HEADS // AT_KV
    tq = _tile(L, 384)
    tk = PAD
    nk = L // tk
    nq = L // tq
    R = G * tq

    def body(qm_ref, q_ref, k_hbm, kt_hbm, v_hbm, do_ref, o_ref, lse_ref, dq_ref, dk_hbm, dv_hbm,
             k_scr, kt_scr, v_scr, dk_scr, dv_scr, dq_scr, sem):
        j, i = pl.program_id(0), pl.program_id(1)

        @pl.when(i == 0)
        def _():
            cps = [pltpu.make_async_copy(k_hbm, k_scr, sem.at[0]), pltpu.make_async_copy(kt_hbm, kt_scr, sem.at[1]),
                   pltpu.make_async_copy(v_hbm.at[j], v_scr, sem.at[2])]
            for cp in cps:
                cp.start()
            dk_scr[...] = jnp.zeros_like(dk_scr)
            dv_scr[...] = jnp.zeros_like(dv_scr)
            for cp in cps:
                cp.wait()

        qt = jnp.concatenate([q_ref[g] for g in range(G)], axis=1)
        rowg = i * tq + lax.broadcasted_iota(jnp.int32, (tq, G * AT_HD), 0)
        dot_all = jnp.where(rowg >= PAD - N_META, do_ref[...], 0.0).T
        ot_all = o_ref[...].T
        dot = jnp.concatenate([dot_all[AT_HD * g:AT_HD * (g + 1)] for g in range(G)], axis=1)
        ot = jnp.concatenate([ot_all[AT_HD * g:AT_HD * (g + 1)] for g in range(G)], axis=1)
        delta = jnp.sum(dot * ot, axis=0, keepdims=True)
        dot128 = jnp.concatenate([dot, jnp.zeros_like(dot)], axis=0)
        dor = dot128.T.astype(BF16)
        dot128 = dot128.astype(BF16)
        qr = qm_ref[...].reshape(R, LANE)
        lse_v = jnp.concatenate([lse_ref[g] for g in range(G)], axis=1)
        dq_scr[...] = jnp.zeros_like(dq_scr)

        def chunk(c, masked):
            start = c * tk if isinstance(c, int) else pl.multiple_of(c * tk, tk)
            k = k_scr[pl.ds(start, tk), :]
            kt = kt_scr[c]
            v = v_scr[pl.ds(start, tk), :]
            st = _nn(k, qt)
            if masked:
                key = lax.broadcasted_iota(jnp.int32, st.shape, 0)
                st = jnp.where(key >= PAD - N_META, st, NEG)
            pt = jnp.exp(st - lse_v)
            dst = (pt * (_nn(v, dot128) - delta)).astype(BF16)
            dq_scr[...] += _nn(kt, dst)
            dk_scr[pl.ds(start, tk), :] += _nn(dst, qr)
            dv_scr[pl.ds(start, tk), :] += _nn(pt.astype(BF16), dor)

        chunk(0, True)

        def loop(c, carry):
            chunk(c, False)
            return carry

        lax.fori_loop(1, nk, loop, 0)
        dq_ref[...] = dq_scr[...].T.reshape(G, tq, LANE)

        @pl.when(i == nq - 1)
        def _():
            ck = pltpu.make_async_copy(dk_scr, dk_hbm.at[j], sem.at[0])
            cv = pltpu.make_async_copy(dv_scr, dv_hbm.at[j], sem.at[1])
            ck.start()
            cv.start()
            ck.wait()
            cv.wait()

    anyspec = pl.BlockSpec(memory_space=pl.ANY)
    ospec = pl.BlockSpec((tq, G * AT_HD), lambda j, i: (i, j))
    return pl.pallas_call(
        body, grid=(AT_KV, nq),
        in_specs=[pl.BlockSpec((G, tq, LANE), lambda j, i: (j, i, 0)), pl.BlockSpec((G, LANE, tq), lambda j, i: (j, 0, i)),
                  anyspec, anyspec, anyspec, ospec, ospec, pl.BlockSpec((G, 1, tq), lambda j, i: (j, 0, i))],
        out_specs=[pl.BlockSpec((G, tq, LANE), lambda j, i: (j, i, 0)), anyspec, anyspec],
        out_shape=[SDS((AT_HEADS, L, LANE), F32), SDS((AT_KV, L, LANE), F32), SDS((AT_KV, L, LANE), F32)],
        scratch_shapes=[pltpu.VMEM((L, LANE), BF16), pltpu.VMEM((nk, LANE, tk), BF16), pltpu.VMEM((L, LANE), BF16),
                        pltpu.VMEM((L, LANE), F32), pltpu.VMEM((L, LANE), F32), pltpu.VMEM((LANE, R), F32),
                        pltpu.SemaphoreType.DMA((3,))],
        compiler_params=_params(("arbitrary", "arbitrary"), VMEM_LARGE), name=name)(qm, qt, kr, krt, vb, do, of, lse)


def _merge_fwd(ya, o8, wua, wubp, z, *, name):
    L = ya.shape[0]
    D = wua.shape[1]
    tm, tn = _tile(L, 1536), 256
    ga, gb = (Z_HG + Z_AT) // tn, (Z_HG + Z_AT + D) // tn

    def body(ya_ref, o8_ref, wa_ref, wb_ref, za_ref, zb_ref, mix_ref):
        pa = _nn(ya_ref[...], wa_ref[...])
        pb = _nn(o8_ref[...], wb_ref[...])
        mix_ref[...] = (_sigmoid(za_ref[...]) * pa + _sigmoid(zb_ref[...]) * pb).astype(BF16)

    return pl.pallas_call(
        body, grid=(D // tn, L // tm),
        in_specs=[pl.BlockSpec((tm, ya.shape[1]), lambda j, i: (i, 0)), pl.BlockSpec((tm, o8.shape[1]), lambda j, i: (i, 0)),
                  pl.BlockSpec((wua.shape[0], tn), lambda j, i: (0, j)), pl.BlockSpec((wubp.shape[0], tn), lambda j, i: (0, j)),
                  pl.BlockSpec((tm, tn), lambda j, i: (i, ga + j)), pl.BlockSpec((tm, tn), lambda j, i: (i, gb + j))],
        out_specs=pl.BlockSpec((tm, tn), lambda j, i: (i, j)), out_shape=SDS((L, D), BF16),
        compiler_params=_params(("parallel", "parallel")), name=name)(ya, o8, wua, wubp, z, z)


def _merge_bwd(dh, wout_t, ya, o8, wua, wubp, z, *, name):
    L = ya.shape[0]
    D = wua.shape[1]
    tm, tn = _tile(L, 1536), 256
    ga, gb = (Z_HG + Z_AT) // tn, (Z_HG + Z_AT + D) // tn

    def body(dh_ref, wo_ref, ya_ref, o8_ref, wa_ref, wb_ref, za_ref, zb_ref, dpa_ref, dpb_ref, dza_ref, dzb_ref):
        dm = _nn(dh_ref[...].astype(BF16), wo_ref[...])
        pa = _nn(ya_ref[...], wa_ref[...])
        pb = _nn(o8_ref[...], wb_ref[...])
        sa, sb = _sigmoid(za_ref[...]), _sigmoid(zb_ref[...])
        dpa_ref[...] = (dm * sa).astype(BF16)
        dpb_ref[...] = (dm * sb).astype(BF16)
        dza_ref[...] = (dm * pa * sa * (1.0 - sa)).astype(BF16)
        dzb_ref[...] = (dm * pb * sb * (1.0 - sb)).astype(BF16)

    ospec = pl.BlockSpec((tm, tn), lambda j, i: (i, j))
    return pl.pallas_call(
        body, grid=(D // tn, L // tm),
        in_specs=[pl.BlockSpec((tm, D), lambda j, i: (i, 0)), pl.BlockSpec((D, tn), lambda j, i: (0, j)),
                  pl.BlockSpec((tm, ya.shape[1]), lambda j, i: (i, 0)), pl.BlockSpec((tm, o8.shape[1]), lambda j, i: (i, 0)),
                  pl.BlockSpec((wua.shape[0], tn), lambda j, i: (0, j)), pl.BlockSpec((wubp.shape[0], tn), lambda j, i: (0, j)),
                  pl.BlockSpec((tm, tn), lambda j, i: (i, ga + j)), pl.BlockSpec((tm, tn), lambda j, i: (i, gb + j))],
        out_specs=[ospec] * 4, out_shape=[SDS((L, D), BF16)] * 4,
        compiler_params=_params(("parallel", "parallel")), name=name)(dh, wout_t, ya, o8, wua, wubp, z, z)


def _loss_head(h, tgt, *, name):
    L, D = h.shape
    tm = PAD

    def body(h_ref, t_ref, dh_ref, ls_ref):
        i = pl.program_id(0)

        @pl.when(i == 0)
        def _():
            ls_ref[...] = jnp.zeros_like(ls_ref)
            dh_ref[...] = jnp.zeros_like(dh_ref)

        @pl.when(i > 0)
        def _():
            e = h_ref[...] - t_ref[...]
            dh_ref[...] = e * (1.0 / D)
            s = jnp.sum(e * e, axis=0, keepdims=True)
            tot = s[:, :LANE]
            for c in range(1, D // LANE):
                tot = tot + s[:, LANE * c:LANE * (c + 1)]
            ls_ref[...] += tot

    return pl.pallas_call(
        body, grid=(L // tm,),
        in_specs=[pl.BlockSpec((tm, D), lambda i: (i, 0)), pl.BlockSpec((tm, D), lambda i: (jnp.maximum(i - 1, 0), 0))],
        out_specs=[pl.BlockSpec((tm, D), lambda i: (i, 0)), pl.BlockSpec((1, LANE), lambda i: (0, 0))],
        out_shape=[SDS((L, D), F32), SDS((1, LANE), F32)],
        compiler_params=_params(("arbitrary",)), name=name)(h, tgt)


def _adamw(w, g, m, v, *, name):
    shape = w.shape
    w2, g2, m2, v2 = [a.reshape(-1, shape[-1]) for a in (w, g, m, v)]
    rows, cols = w2.shape
    tr = _tile(rows, 256, 8)

    def body(w_ref, g_ref, m_ref, v_ref, d_ref, nm_ref, nv_ref):
        g = g_ref[...]
        m = ADAM_B1 * m_ref[...] + (1.0 - ADAM_B1) * g
        v = ADAM_B2 * v_ref[...] + (1.0 - ADAM_B2) * (g * g)
        m_hat = m / (1.0 - ADAM_B1 ** ADAM_STEP)
        v_hat = v / (1.0 - ADAM_B2 ** ADAM_STEP)
        d_ref[...] = -ADAM_LR * (m_hat / (jnp.sqrt(v_hat) + ADAM_EPS) + ADAM_WD * w_ref[...])
        nm_ref[...] = m
        nv_ref[...] = v

    spec = pl.BlockSpec((tr, cols), lambda i: (i, 0))
    outs = pl.pallas_call(
        body, grid=(rows // tr,), in_specs=[spec] * 4, out_specs=[spec] * 3, out_shape=[SDS((rows, cols), F32)] * 3,
        compiler_params=_params(("parallel",)), name=name)(w2, g2, m2, v2)
    return [o.reshape(shape) for o in outs]


def _place():
    return lax.axis_index("x"), lax.axis_index("y"), lax.axis_index("c")


def _allgather_small(v, *, name):
    m_per, n = v.shape

    def body(x_ref, out_ref, send_sems, recv_sems, local_sem):
        x, y, c = _place()
        me, sibling = (x, y, c), (x, y, 1 - c)
        chips = [(1 - x, y), (x, 1 - y), (1 - x, 1 - y)]

        def rows(px, py, pc):
            return out_ref.at[pl.ds((4 * px + 2 * py + pc) * m_per, m_per), :]

        def copy(k, block, to, src=None):
            return pltpu.make_async_remote_copy(
                src_ref=rows(*block) if src is None else src, dst_ref=rows(*block),
                send_sem=send_sems.at[k], recv_sem=recv_sems.at[k], device_id=to, device_id_type=MESH)

        mine = pltpu.make_async_copy(x_ref, rows(*me), local_sem)
        mine.start()
        first = [copy(0, me, sibling, src=x_ref)]
        first += [copy(1 + j, me, (*chip, c), src=x_ref) for j, chip in enumerate(chips)]
        for cp in first:
            cp.start()
        passed = [copy(4 + j, (*chip, c), sibling) for j, chip in enumerate(chips)]
        for j, chip in enumerate(chips):
            copy(1 + j, (*chip, c), me).wait_recv()
            passed[j].start()
        copy(0, sibling, me).wait_recv()
        for j, chip in enumerate(chips):
            copy(4 + j, (*chip, 1 - c), me).wait_recv()
        for cp in first + passed:
            cp.wait_send()
        mine.wait()

    return pl.pallas_call(
        body, out_shape=SDS((8 * m_per, n), v.dtype),
        in_specs=[pl.BlockSpec(memory_space=pltpu.VMEM)], out_specs=pl.BlockSpec(memory_space=pltpu.VMEM),
        scratch_shapes=[pltpu.SemaphoreType.DMA((7,)), pltpu.SemaphoreType.DMA((7,)), pltpu.SemaphoreType.DMA],
        name=name)(v)


def _chips(x, y):
    return [(1 - x, y), (x, 1 - y), (1 - x, 1 - y)]


def _gather_mats(shards, *, name):
    n = len(shards)

    def body(*refs):
        ins, outs = refs[:n], refs[n:2 * n]
        send_sems, recv_sems, fsend_sems, frecv_sems = refs[2 * n:]
        x, y, c = _place()
        s_me, sibling, chips = 2 * x + y, (x, y, 1 - c), _chips(x, y)

        def copy(src, dst, ssem, rsem, to):
            return pltpu.make_async_remote_copy(src_ref=src, dst_ref=dst, send_sem=ssem, recv_sem=rsem,
                                                device_id=to, device_id_type=MESH)

        first = [copy(ins[t].at[c], outs[t].at[s_me, c], send_sems.at[3 * t + k], recv_sems.at[3 * t + k], (*chip, c))
                 for t in range(n) for k, chip in enumerate(chips)]
        for cp in first:
            cp.start()
        passed = []
        for t in range(n):
            for k, chip in enumerate(chips):
                slot = outs[t].at[2 * chip[0] + chip[1], c]
                copy(ins[t].at[c], slot, send_sems.at[3 * t + k], recv_sems.at[3 * t + k], (*chip, c)).wait_recv()
                fw = copy(slot, slot, fsend_sems.at[3 * t + k], frecv_sems.at[3 * t + k], sibling)
                fw.start()
                passed.append(fw)
        for t in range(n):
            for k, chip in enumerate(chips):
                slot = outs[t].at[2 * chip[0] + chip[1], 1 - c]
                copy(slot, slot, fsend_sems.at[3 * t + k], frecv_sems.at[3 * t + k], sibling).wait_recv()
        for cp in first + passed:
            cp.wait_send()

    anyspec = pl.BlockSpec(memory_space=pl.ANY)
    return pl.pallas_call(
        body, out_shape=[SDS((4,) + s.shape, s.dtype) for s in shards], in_specs=[anyspec] * n, out_specs=[anyspec] * n,
        scratch_shapes=[pltpu.SemaphoreType.DMA((3 * n,))] * 4, name=name)(*shards)


def _rs_pair_exchange(gs, *, name):
    n = len(gs)

    def body(*refs):
        ins, outs = refs[:n], refs[n:2 * n]
        send_sems, recv_sems = refs[2 * n:]
        x, y, c = _place()
        cps = [pltpu.make_async_remote_copy(src_ref=ins[t].at[k, 1 - c], dst_ref=outs[t].at[k],
                                            send_sem=send_sems.at[4 * t + k], recv_sem=recv_sems.at[4 * t + k],
                                            device_id=(x, y, 1 - c), device_id_type=MESH)
               for t in range(n) for k in range(4)]
        for cp in cps:
            cp.start()
        for cp in cps:
            cp.wait()

    anyspec = pl.BlockSpec(memory_space=pl.ANY)
    return pl.pallas_call(
        body, out_shape=[SDS((4,) + g.shape[2:], g.dtype) for g in gs], in_specs=[anyspec] * n, out_specs=[anyspec] * n,
        scratch_shapes=[pltpu.SemaphoreType.DMA((4 * n,))] * 2, name=name)(*gs)


def _rs_chip_exchange(parts, *, name):
    n = len(parts)

    def body(*refs):
        ins, outs = refs[:n], refs[n:2 * n]
        send_sems, recv_sems, local_sems = refs[2 * n:]
        x, y, c = _place()
        s_me, chips = 2 * x + y, _chips(x, y)

        def copy(t, k, chip, src_slot, dst_slot):
            return pltpu.make_async_remote_copy(
                src_ref=ins[t].at[src_slot], dst_ref=outs[t].at[dst_slot], send_sem=send_sems.at[3 * t + k],
                recv_sem=recv_sems.at[3 * t + k], device_id=(*chip, c), device_id_type=MESH)

        mine = [pltpu.make_async_copy(ins[t].at[s_me], outs[t].at[s_me], local_sems.at[t]) for t in range(n)]
        for cp in mine:
            cp.start()
        sends = [copy(t, k, chip, 2 * chip[0] + chip[1], s_me) for t in range(n) for k, chip in enumerate(chips)]
        for cp in sends:
            cp.start()
        for t in range(n):
            for k, chip in enumerate(chips):
                copy(t, k, chip, s_me, 2 * chip[0] + chip[1]).wait_recv()
        for cp in sends:
            cp.wait_send()
        for cp in mine:
            cp.wait()

    anyspec = pl.BlockSpec(memory_space=pl.ANY)
    return pl.pallas_call(
        body, out_shape=[SDS(p.shape, p.dtype) for p in parts], in_specs=[anyspec] * n, out_specs=[anyspec] * n,
        scratch_shapes=[pltpu.SemaphoreType.DMA((3 * n,))] * 2 + [pltpu.SemaphoreType.DMA((n,))], name=name)(*parts)


def _rs_pair_share(fulls, *, name):
    n = len(fulls)

    def body(*refs):
        ins, outs = refs[:n], refs[n:2 * n]
        send_sems, recv_sems = refs[2 * n:]
        x, y, c = _place()

        def copy(t, half):
            return pltpu.make_async_remote_copy(src_ref=ins[t].at[c], dst_ref=outs[t].at[half], send_sem=send_sems.at[t],
                                                recv_sem=recv_sems.at[t], device_id=(x, y, 1 - c), device_id_type=MESH)

        sends = [copy(t, c) for t in range(n)]
        for cp in sends:
            cp.start()
        for t in range(n):
            copy(t, 1 - c).wait_recv()
        for cp in sends:
            cp.wait_send()

    anyspec = pl.BlockSpec(memory_space=pl.ANY)
    return pl.pallas_call(
        body, out_shape=[SDS(f.shape, f.dtype) for f in fulls], in_specs=[anyspec] * n, out_specs=[anyspec] * n,
        input_output_aliases={t: t for t in range(n)},
        scratch_shapes=[pltpu.SemaphoreType.DMA((n,))] * 2, name=name)(*fulls)


def _add_half(g, other, c1, *, out_dtype, name):
    _, _, h, cs = g.shape
    tr = _tile(h, 512, 16)

    def body(c_ref, g_ref, o_ref, out_ref):
        out_ref[...] = (g_ref[...] + o_ref[...]).astype(out_dtype)

    spec = pl.BlockSpec((None, tr, cs), lambda k, i, c: (k, i, 0))
    return pl.pallas_call(
        body, out_shape=SDS(other.shape, out_dtype),
        grid_spec=pltpu.PrefetchScalarGridSpec(
            num_scalar_prefetch=1, grid=(4, h // tr),
            in_specs=[pl.BlockSpec((None, None, tr, cs), lambda k, i, c: (k, c[0], i, 0)), spec], out_specs=spec),
        compiler_params=_params(("parallel", "parallel")), name=name)(c1, g, other)


def _sum4(x, c1, *, name):
    n, h, cs = x.shape
    tr = _tile(h, 512, 16)

    def body(c_ref, x_ref, o_ref):
        tot = x_ref[0].astype(F32)
        for s in range(1, n):
            tot = tot + x_ref[s].astype(F32)
        o_ref[...] = tot

    return pl.pallas_call(
        body, out_shape=SDS((2, h, cs), F32),
        grid_spec=pltpu.PrefetchScalarGridSpec(
            num_scalar_prefetch=1, grid=(h // tr,),
            in_specs=[pl.BlockSpec((n, tr, cs), lambda i, c: (0, i, 0))],
            out_specs=pl.BlockSpec((None, tr, cs), lambda i, c: (c[0], i, 0))),
        compiler_params=_params(("parallel",)), name=name)(c1, x)


def _finish_small(gathered, lbf, lbb, *, rows, name):
    r_lbf, r_lbb = rows['lb_f'], rows['lb_b']

    def body(g_ref, lbf_ref, lbb_ref, o_ref, dlf_ref, dlb_ref):
        tot = g_ref[0]
        for s in range(1, 8):
            tot = tot + g_ref[s]
        o_ref[...] = tot
        o_ref[0:1, :] = jnp.broadcast_to(jnp.sum(o_ref[0:1, :], axis=1, keepdims=True), (1, LANE))
        for lb_ref, d_ref, r0 in ((lbf_ref, dlf_ref, r_lbf), (lbb_ref, dlb_ref, r_lbb)):
            for hh in range(HG_HEADS):
                sl = slice(LANE * hh, LANE * (hh + 1))
                l0, l1 = lb_ref[0:1, sl], lb_ref[1:2, sl]
                mx = jnp.maximum(l0, l1)
                e0, e1 = jnp.exp(l0 - mx), jnp.exp(l1 - mx)
                p0 = e0 / (e0 + e1)
                d0 = o_ref[r0 + hh:r0 + hh + 1, :] * p0 * (1.0 - p0)
                d_ref[0:1, sl] = d0
                d_ref[1:2, sl] = -d0

    vm = pl.BlockSpec(memory_space=pltpu.VMEM)
    return pl.pallas_call(
        body, in_specs=[vm, vm, vm], out_specs=[vm, vm, vm],
        out_shape=[SDS(gathered.shape[1:], F32), SDS(lbf.shape, F32), SDS(lbb.shape, F32)], name=name)(gathered, lbf, lbb)


def _local_step(x2, tgt2, meta, W, S):
    T, D = x2.shape
    L = PAD + T
    h0 = jnp.concatenate([jnp.zeros((PAD - N_META, D), F32), meta, x2], axis=0)

    qk0 = Z_HG
    w_in = jnp.concatenate([W['w_in'][:, :qk0], _qk_to_group(W['w_in'][:, qk0:qk0 + AT_W + AT_KVW]),
                            W['w_in'][:, qk0 + AT_W + AT_KVW:]], axis=1)
    cc, ss = _rope_tables(L)
    wq_g, wk_g = _group_vec(S['q_norm']), _group_vec(S['k_norm'])

    def ffn_fwd(h, nw, wg, wu, wd, tag):
        n = _rmsnorm_fwd(h, nw, name=tag + "_norm")
        g, u, a = _ffn4_up(n, wg, wu, name=tag + "_up")
        hn = _ffn4_down(a, wd, h, name=tag + "_down")
        return hn, (n, g, u, a)

    def ffn_bwd(dh, h, nw, wg, wu, wd, saved, tag, split=False):
        n, g, u, a = saved
        dg, du = _ffn4_dact(dh, wd.transpose(0, 2, 1), g, u, name=tag + "_dact")
        dn = _ffn4_dn(dg, du, wg, wu, name=tag + "_dn")
        dwg = _ffn4_dw(n, dg, x_is_rows=True, name=tag + "_dwg")
        dwu = _ffn4_dw(n, du, x_is_rows=True, name=tag + "_dwu")
        dwd = _ffn4_dw(dh, a, x_is_rows=False, alpha=0.5, name=tag + "_dwd")
        *dhp, dnw = _rmsnorm_bwd(h, nw, dn, dh, split=split, name=tag + "_norm_bwd")
        return (dhp if split else dhp[0]), dnw, dwg, dwu, dwd

    h1, sv1 = ffn_fwd(h0, S['ffn1_norm'], W['ffn1_w_gate'], W['ffn1_w_up'], W['ffn1_w_down'], "ffn1")
    um = _rmsnorm_fwd(h1, S['mix_norm'], name="mix_norm")
    z = _mm([(um, w_in)], tm=512, tn=1792, tk=D, name="in_proj")
    of, sf = _hg_fwd(z, S['hg_lb_fwd'], rev=False, name="hg_fwd_f")
    ob, sb = _hg_fwd(z, S['hg_lb_bwd'], rev=True, name="hg_fwd_b")
    ya = _hg_post_fwd(of, ob, z, S['hg_out_norm'], name="hg_post")
    qm, qt, kr, krt, vb, vt = _at_prep(z, cc, ss, wq_g, wk_g, name="at_prep")
    yb, yb_f32, lse = _at_fwd(qt, kr, vt, name="at_fwd")
    mixed = _merge_fwd(ya, yb, W['w_up_a'], W['w_up_b'], z, name="merge")
    h2 = _mm([(mixed, W['w_out'])], res=h1, tm=512, tn=D, tk=D, name="out_proj")
    h3, sv2 = ffn_fwd(h2, S['ffn2_norm'], W['ffn2_w_gate'], W['ffn2_w_up'], W['ffn2_w_down'], "ffn2")
    dh3, loss_lanes = _loss_head(h3, tgt2, name="loss_head")

    G = {}
    dh2, dn_ffn2, G['ffn2_w_gate'], G['ffn2_w_up'], G['ffn2_w_down'] = ffn_bwd(
        dh3, h2, S['ffn2_norm'], W['ffn2_w_gate'], W['ffn2_w_up'], W['ffn2_w_down'], sv2, "ffn2")
    dpa, dpb, dzga, dzgb = _merge_bwd(dh2, W['w_out'].T, ya, yb, W['w_up_a'], W['w_up_b'], z, name="merge_bwd")
    G['w_out'] = _mm([(mixed, dh2)], ta=True, tm=D, tn=D, tk=512, name="d_w_out")
    dya = _mm([(dpa, W['w_up_a'])], tb=True, tm=512, tn=HG_W, tk=D, name="d_ya")
    dyb = _mm([(dpb, W['w_up_b'])], tb=True, tm=512, tn=AT_W, tk=D, name="d_yb")
    G['w_up_a'] = _mm([(ya, dpa)], ta=True, tm=HG_W, tn=D, tk=512, name="d_w_up_a")
    G['w_up_b'] = _mm([(yb, dpb)], ta=True, tm=AT_W, tn=D, tk=512, name="d_w_up_b")
    do_hg, dzg, d_hgn = _hg_post_bwd(dya, of, ob, z, S['hg_out_norm'], name="hg_post_bwd")
    dq_f, dv_f, dzf_f, dlb_f = _hg_bwd(z, S['hg_lb_fwd'], do_hg, sf, None, rev=False, name="hg_bwd_f")
    dzq, dzi, dzf_b, dlb_b = _hg_bwd(z, S['hg_lb_bwd'], do_hg, sb, (dq_f, dv_f), rev=True, name="hg_bwd_b")
    dqm, dk2, dv2 = _at_bwd(qm, qt, kr, krt, vb, dyb, yb_f32, lse, name="at_bwd")
    dz_at, dwq_g, dwk_g = _at_prep_bwd(dqm, dk2, dv2, z, cc, ss, wq_g, wk_g, name="at_prep_bwd")
    dz = jnp.concatenate([dzq, dzi, dzf_f, dzf_b, dzg, dz_at, dzga, dzgb], axis=1)
    dum = _mm([(dz, w_in.T)], tm=512, tn=D, tk=1792, name="d_um")
    dw_in_p = _mm([(um, dz)], ta=True, tm=D, tn=1792, tk=512, name="d_w_in")
    G['w_in'] = jnp.concatenate([dw_in_p[:, :qk0], _qk_from_group(dw_in_p[:, qk0:qk0 + AT_W + AT_KVW]),
                                 dw_in_p[:, qk0 + AT_W + AT_KVW:]], axis=1)
    dh1, dn_mix = _rmsnorm_bwd(h1, S['mix_norm'], dum, dh2, name="mix_norm_bwd")
    (grad_x, dmeta), dn_ffn1, G['ffn1_w_gate'], G['ffn1_w_up'], G['ffn1_w_down'] = ffn_bwd(
        dh1, h0, S['ffn1_norm'], W['ffn1_w_gate'], W['ffn1_w_up'], W['ffn1_w_down'], sv1, "ffn1", split=True)

    small_rows = [('loss', loss_lanes), ('ffn1_norm', dn_ffn1.reshape(-1, LANE)), ('mix_norm', dn_mix.reshape(-1, LANE)),
                  ('ffn2_norm', dn_ffn2.reshape(-1, LANE)), ('hg_out_norm', d_hgn.reshape(-1, LANE)),
                  ('lb_f', dlb_f.reshape(-1, LANE)), ('lb_b', dlb_b.reshape(-1, LANE)), ('q_norm', dwq_g), ('k_norm', dwk_g)]
    return grad_x, dmeta, G, small_rows


def kernel(x, meta_tokens, ffn1_norm, ffn1_w_gate, ffn1_w_up, ffn1_w_down, mix_norm, w_in, hg_lb_fwd, hg_lb_bwd, hg_out_norm, q_norm, k_norm, w_up_a, w_up_b, w_out, ffn2_norm, ffn2_w_gate, ffn2_w_up, ffn2_w_down, loss_target, m_meta_tokens, m_ffn1_norm, m_ffn1_w_gate, m_ffn1_w_up, m_ffn1_w_down, m_mix_norm, m_w_in, m_hg_lb_fwd, m_hg_lb_bwd, m_hg_out_norm, m_q_norm, m_k_norm, m_w_up_a, m_w_up_b, m_w_out, m_ffn2_norm, m_ffn2_w_gate, m_ffn2_w_up, m_ffn2_w_down, v_meta_tokens, v_ffn1_norm, v_ffn1_w_gate, v_ffn1_w_up, v_ffn1_w_down, v_mix_norm, v_w_in, v_hg_lb_fwd, v_hg_lb_bwd, v_hg_out_norm, v_q_norm, v_k_norm, v_w_up_a, v_w_up_b, v_w_out, v_ffn2_norm, v_ffn2_w_gate, v_ffn2_w_up, v_ffn2_w_down):
    given = dict(locals())
    w = {n: given[n] for n in WEIGHTS}
    mom = {n: given["m_" + n] for n in WEIGHTS}
    var = {n: given["v_" + n] for n in WEIGHTS}
    c = lax.axis_index("c")
    D = x.shape[-1]

    shapes = {n: w[n].shape[-2:] for n in MATS + ('meta_tokens',)}
    halves = [w[n].astype(BF16).reshape(2, shapes[n][0] // 2, shapes[n][1]) for n in MATS]
    gathered = _gather_mats(halves, name="gather_weights")
    s_me = 2 * lax.axis_index("x") + lax.axis_index("y")
    W = {}
    for n, hv, g4 in zip(MATS, halves, gathered):
        r, cs = shapes[n]
        g4 = lax.dynamic_update_index_in_dim(g4, hv, s_me, 0).reshape(4, r, cs)
        if n in FFN_MATS:
            W[n] = g4
        elif n in ROW_SHARDED:
            W[n] = g4.reshape(4 * r, cs)
        else:
            W[n] = g4.transpose(1, 0, 2).reshape(r, 4 * cs)
    meta_rows = w['meta_tokens'].reshape(-1, LANE)
    mg = _allgather_small(meta_rows, name="gather_meta").reshape(4, 2, N_META, -1)[:, 0]
    meta = mg.transpose(1, 0, 2).reshape(N_META, D)
    S = {n: w[n] for n in SMALLS}

    grad_x, dmeta, G, small_rows = _local_step(x[0], loss_target[0], meta, W, S)
    G['meta_tokens'] = dmeta

    names = MATS + ('meta_tokens',)
    views = []
    for n in names:
        r, cs = shapes[n]
        if n in FFN_MATS:
            g4 = G[n]
        elif n in ROW_SHARDED:
            g4 = G[n].reshape(4, r, cs)
        else:
            g4 = G[n].reshape(r, 4, cs).transpose(1, 0, 2)
        views.append(g4.reshape(4, 2, r // 2, cs))
    c1 = c.astype(jnp.int32).reshape(1)
    from_sibling = _rs_pair_exchange(views, name="rs_pair_exchange")
    parts = [_add_half(v, o, c1, out_dtype=F32 if n == 'meta_tokens' else BF16, name="rs_pair_sum_" + n)
             for n, v, o in zip(names, views, from_sibling)]
    slabs = _rs_chip_exchange(parts, name="rs_chip_exchange")
    reds = [_sum4(s, c1, name="rs_chip_sum_" + n) for n, s in zip(names, slabs)]
    both = _rs_pair_share(reds, name="rs_pair_share")
    grads = {n: b.reshape(w[n].shape) for n, b in zip(names, both)}

    rows, off = {}, 0
    for nme, blk in small_rows:
        rows[nme] = off
        off += blk.shape[0]
    block = jnp.concatenate([blk for _, blk in small_rows], axis=0)
    n_rows = (off + 7) // 8 * 8
    block = jnp.pad(block, ((0, n_rows - off), (0, 0)))
    allsmall = _allgather_small(block, name="gather_small").reshape(8, n_rows, LANE)
    tot, d_lbf, d_lbb = _finish_small(allsmall, w['hg_lb_fwd'], w['hg_lb_bwd'], rows=rows, name="finish_small")
    loss = 0.5 * tot[0, 0] / D

    def small(nme, shape):
        r0 = rows[nme]
        return tot[r0:r0 + shape[-1] // LANE].reshape(shape)

    grads['ffn1_norm'] = small('ffn1_norm', w['ffn1_norm'].shape)
    grads['mix_norm'] = small('mix_norm', w['mix_norm'].shape)
    grads['ffn2_norm'] = small('ffn2_norm', w['ffn2_norm'].shape)
    grads['hg_out_norm'] = small('hg_out_norm', w['hg_out_norm'].shape)
    grads['hg_lb_fwd'] = d_lbf
    grads['hg_lb_bwd'] = d_lbb
    grads['q_norm'] = _ungroup_vec(tot[rows['q_norm']])
    grads['k_norm'] = _ungroup_vec(tot[rows['k_norm']])

    delta, new_m, new_v = {}, {}, {}
    for n in WEIGHTS:
        delta[n], new_m[n], new_v[n] = _adamw(w[n], grads[n], mom[n], var[n], name="adamw_" + n)
    return (loss, grad_x[None], *[grads[n] for n in WEIGHTS], *[delta[n] for n in WEIGHTS],
            *[new_m[n] for n in WEIGHTS], *[new_v[n] for n in WEIGHTS])
```

```python
import numpy as np
import jax
import jax.numpy as jnp
from jax import lax
from jax.experimental import pallas as pl
from jax.experimental.pallas import tpu as pltpu

F32 = jnp.float32
BF16 = jnp.bfloat16
SDS = jax.ShapeDtypeStruct
MESH = pl.DeviceIdType.MESH

EPS = 1e-6
N_META = 16
PAD = 512
LANE = 128
CHUNK = 128
HG_HEADS = 4
HG_W = HG_HEADS * 128
AT_HEADS = 8
AT_KV = 2
AT_HD = 64
AT_W = AT_HEADS * AT_HD
AT_KVW = AT_KV * AT_HD
VT_ROWS = AT_HD + 16
FWD_CHUNKS_PER_STEP = 4
GRID_W = 64
ROPE_THETA = 10000.0
Z_HG = 5 * HG_W
Z_AT = AT_W + 2 * AT_KVW
ADAM_LR, ADAM_B1, ADAM_B2, ADAM_EPS, ADAM_WD, ADAM_STEP = 0.001, 0.9, 0.999, 1e-08, 0.01, 10
VMEM_DEFAULT = 48 * 1024 * 1024
VMEM_LARGE = 60 * 1024 * 1024
NEG = -1e30

MATS = ('ffn1_w_gate', 'ffn1_w_up', 'ffn1_w_down', 'w_in', 'w_up_a', 'w_up_b', 'w_out',
        'ffn2_w_gate', 'ffn2_w_up', 'ffn2_w_down')
ROW_SHARDED = ('ffn1_w_down', 'w_out', 'ffn2_w_down')
FFN_MATS = ('ffn1_w_gate', 'ffn1_w_up', 'ffn1_w_down', 'ffn2_w_gate', 'ffn2_w_up', 'ffn2_w_down')
SMALLS = ('ffn1_norm', 'mix_norm', 'hg_lb_fwd', 'hg_lb_bwd', 'hg_out_norm', 'q_norm', 'k_norm', 'ffn2_norm')
WEIGHTS = ('meta_tokens', 'ffn1_norm', 'ffn1_w_gate', 'ffn1_w_up', 'ffn1_w_down', 'mix_norm', 'w_in', 'hg_lb_fwd',
           'hg_lb_bwd', 'hg_out_norm', 'q_norm', 'k_norm', 'w_up_a', 'w_up_b', 'w_out', 'ffn2_norm', 'ffn2_w_gate',
           'ffn2_w_up', 'ffn2_w_down')


def _params(sem=None, vmem=VMEM_DEFAULT):
    return pltpu.CompilerParams(dimension_semantics=sem, vmem_limit_bytes=vmem)


def _tile(n, pref, q=LANE):
    for d in range(min(pref, n), 0, -1):
        if n % d == 0 and d % q == 0:
            return d
    return n


def _sigmoid(x):
    return 0.5 * jnp.tanh(0.5 * x) + 0.5


def _dot(a, b, dims):
    return lax.dot_general(a, b, (dims, ((), ())), preferred_element_type=F32)


def _nn(a, b):
    return _dot(a, b, ((1,), (0,)))


def _nt(a, b):
    return _dot(a, b, ((1,), (1,)))


def _tn(a, b):
    return _dot(a, b, ((0,), (0,)))


def _split3(x):
    x1 = x.astype(BF16)
    r = x - x1.astype(F32)
    x2 = r.astype(BF16)
    x3 = (r - x2.astype(F32)).astype(BF16)
    return x1, x2, x3


def _exact_left(m01, x):
    x1, x2, x3 = _split3(x)
    return _nn(m01, x1) + _nn(m01, x2) + _nn(m01, x3)


def _exact_right(x, m01):
    x1, x2, x3 = _split3(x)
    return _nn(x1, m01) + _nn(x2, m01) + _nn(x3, m01)


def _mm(pairs, *, name, ta=False, tb=False, out_dtype=F32, tm=512, tn=1024, tk=1024, alpha=1.0, res=None):
    a0, b0 = pairs[0]
    M = a0.shape[1] if ta else a0.shape[0]
    K = a0.shape[0] if ta else a0.shape[1]
    N = b0.shape[0] if tb else b0.shape[1]
    tm, tn, tk = _tile(M, tm), _tile(N, tn), _tile(K, tk)
    nk = K // tk
    npair = len(pairs)
    dims = ((0 if ta else 1,), (1 if tb else 0,))

    def body(*refs):
        ab = refs[:2 * npair]
        pos = 2 * npair
        res_ref = None
        if res is not None:
            res_ref = refs[pos]
            pos += 1
        o_ref = refs[pos]

        def partial_sum():
            tot = None
            for p in range(npair):
                d = _dot(ab[2 * p][...].astype(BF16), ab[2 * p + 1][...].astype(BF16), dims)
                tot = d if tot is None else tot + d
            return tot

        def finish(acc):
            r = acc if alpha == 1.0 else acc * alpha
            if res_ref is not None:
                r = res_ref[...] + r
            o_ref[...] = r.astype(out_dtype)

        if nk == 1:
            finish(partial_sum())
        else:
            acc_ref = refs[pos + 1]
            k = pl.program_id(2)

            @pl.when(k == 0)
            def _():
                acc_ref[...] = jnp.zeros_like(acc_ref)

            acc_ref[...] += partial_sum()

            @pl.when(k == nk - 1)
            def _():
                finish(acc_ref[...])

    a_spec = pl.BlockSpec((tk, tm), lambda j, i, k: (k, i)) if ta else pl.BlockSpec((tm, tk), lambda j, i, k: (i, k))
    b_spec = pl.BlockSpec((tn, tk), lambda j, i, k: (j, k)) if tb else pl.BlockSpec((tk, tn), lambda j, i, k: (k, j))
    o_spec = pl.BlockSpec((tm, tn), lambda j, i, k: (i, j))
    in_specs, args = [], []
    for a, b in pairs:
        in_specs += [a_spec, b_spec]
        args += [a, b]
    if res is not None:
        in_specs.append(o_spec)
        args.append(res)
    return pl.pallas_call(
        body, grid=(N // tn, M // tm, nk), in_specs=in_specs, out_specs=o_spec,
        out_shape=SDS((M, N), out_dtype),
        scratch_shapes=[pltpu.VMEM((tm, tn), F32)] if nk > 1 else [],
        compiler_params=_params(("parallel", "parallel", "arbitrary")), name=name)(*args)


def _rmsnorm_fwd(h, w, *, name):
    L, D = h.shape
    tm = _tile(L, 512)

    def body(h_ref, w_ref, o_ref):
        x = h_ref[...]
        r = lax.rsqrt(jnp.mean(x * x, axis=-1, keepdims=True) + EPS)
        o_ref[...] = (x * r * w_ref[...]).astype(BF16)

    return pl.pallas_call(
        body, grid=(L // tm,),
        in_specs=[pl.BlockSpec((tm, D), lambda i: (i, 0)), pl.BlockSpec((1, D), lambda i: (0, 0))],
        out_specs=pl.BlockSpec((tm, D), lambda i: (i, 0)), out_shape=SDS((L, D), BF16),
        compiler_params=_params(("parallel",)), name=name)(h, w)


def _rmsnorm_bwd(h, w, dn, dres, *, split=False, name):
    L, D = h.shape
    tm = PAD if split else _tile(L, 512)

    def body(h_ref, w_ref, dn_ref, dres_ref, dh_ref, *rest):
        dw_ref = rest[-1]
        i = pl.program_id(0)
        x = h_ref[...]
        r = lax.rsqrt(jnp.mean(x * x, axis=-1, keepdims=True) + EPS)
        xh = x * r
        dn = dn_ref[...]
        dxh = dn * w_ref[...]
        dh = dres_ref[...] + r * (dxh - xh * jnp.mean(dxh * xh, axis=-1, keepdims=True))
        dh_ref[...] = dh

        @pl.when(i == 0)
        def _():
            dw_ref[...] = jnp.zeros_like(dw_ref)
            if split:
                rest[0][...] = dh[PAD - N_META:]

        dw_ref[...] += jnp.sum(dn * xh, axis=0, keepdims=True)

    row = pl.BlockSpec((tm, D), lambda i: (i, 0))
    vec = pl.BlockSpec((1, D), lambda i: (0, 0))
    if split:
        out_specs = [pl.BlockSpec((tm, D), lambda i: (jnp.maximum(i - 1, 0), 0)), pl.BlockSpec((N_META, D), lambda i: (0, 0)), vec]
        out_shape = [SDS((L - PAD, D), F32), SDS((N_META, D), F32), SDS((1, D), F32)]
    else:
        out_specs, out_shape = [row, vec], [SDS((L, D), F32), SDS((1, D), F32)]
    return pl.pallas_call(
        body, grid=(L // tm,), in_specs=[row, vec, row, row], out_specs=out_specs, out_shape=out_shape,
        compiler_params=_params(("arbitrary",)), name=name)(h, w, dn, dres)


def _ffn4_up(n, wg4, wu4, *, name):
    L, D = n.shape
    ns, _, cs = wg4.shape
    tm = _tile(L, 768)

    def body(n_ref, wg_ref, wu_ref, ag_ref, au_ref, a_ref):
        x = n_ref[...]
        g = _nn(x, wg_ref[...])
        u = _nn(x, wu_ref[...])
        sg = _sigmoid(g)
        silu = g * sg
        ag_ref[...] = (u * (sg * (1.0 + g * (1.0 - sg)))).astype(BF16)
        au_ref[...] = silu.astype(BF16)
        a_ref[...] = (silu * u).astype(BF16)

    wspec = pl.BlockSpec((None, D, cs), lambda j, i: (j, 0, 0))
    ospec = pl.BlockSpec((None, tm, cs), lambda j, i: (j, i, 0))
    return pl.pallas_call(
        body, grid=(ns, L // tm),
        in_specs=[pl.BlockSpec((tm, D), lambda j, i: (i, 0)), wspec, wspec], out_specs=[ospec, ospec, ospec],
        out_shape=[SDS((ns, L, cs), BF16), SDS((ns, L, cs), BF16), SDS((ns, L, cs), BF16)],
        compiler_params=_params(("parallel", "parallel")), name=name)(n, wg4, wu4)


def _ffn4_down(a4, wd4, h, *, name):
    ns, L, cs = a4.shape
    D = wd4.shape[2]
    tm = _tile(L, 512)

    def body(a_ref, w_ref, h_ref, o_ref):
        acc = _nn(a_ref[0], w_ref[0])
        for j in range(1, ns):
            acc = acc + _nn(a_ref[j], w_ref[j])
        o_ref[...] = h_ref[...] + 0.5 * acc

    row = pl.BlockSpec((tm, D), lambda i: (i, 0))
    return pl.pallas_call(
        body, grid=(L // tm,),
        in_specs=[pl.BlockSpec((ns, tm, cs), lambda i: (0, i, 0)), pl.BlockSpec((ns, cs, D), lambda i: (0, 0, 0)), row],
        out_specs=row, out_shape=SDS((L, D), F32),
        compiler_params=_params(("parallel",)), name=name)(a4, wd4, h)


def _ffn4_dact(dh, wd4, ag4, au4, *, name):
    L, D = dh.shape
    ns, cs, _ = wd4.shape
    tm = _tile(L, 768)

    def body(dh_ref, wd_ref, ag_ref, au_ref, dg_ref, du_ref):
        da = 0.5 * _nt(dh_ref[...].astype(BF16), wd_ref[...])
        dg_ref[...] = (da * ag_ref[...].astype(F32)).astype(BF16)
        du_ref[...] = (da * au_ref[...].astype(F32)).astype(BF16)

    ospec = pl.BlockSpec((None, tm, cs), lambda j, i: (j, i, 0))
    return pl.pallas_call(
        body, grid=(ns, L // tm),
        in_specs=[pl.BlockSpec((tm, D), lambda j, i: (i, 0)), pl.BlockSpec((None, cs, D), lambda j, i: (j, 0, 0)), ospec, ospec],
        out_specs=[ospec, ospec], out_shape=[SDS((ns, L, cs), BF16), SDS((ns, L, cs), BF16)],
        compiler_params=_params(("parallel", "parallel")), name=name)(dh, wd4, ag4, au4)


def _ffn4_dn(dg4, du4, wg4, wu4, *, name):
    ns, L, cs = dg4.shape
    D = wg4.shape[1]
    tm = _tile(L, 512)

    def body(dg_ref, du_ref, wg_ref, wu_ref, o_ref):
        acc = None
        for j in range(ns):
            t = _nt(dg_ref[j], wg_ref[j]) + _nt(du_ref[j], wu_ref[j])
            acc = t if acc is None else acc + t
        o_ref[...] = acc

    aspec = pl.BlockSpec((ns, tm, cs), lambda i: (0, i, 0))
    wspec = pl.BlockSpec((ns, D, cs), lambda i: (0, 0, 0))
    return pl.pallas_call(
        body, grid=(L // tm,), in_specs=[aspec, aspec, wspec, wspec],
        out_specs=pl.BlockSpec((tm, D), lambda i: (i, 0)), out_shape=SDS((L, D), F32),
        compiler_params=_params(("parallel",), VMEM_LARGE), name=name)(dg4, du4, wg4, wu4)


def _ffn4_dw(x, y4, *, x_is_rows, alpha=1.0, name):
    L, D = x.shape
    ns, _, cs = y4.shape
    tk = _tile(L, 512)
    nk = L // tk
    oshape = (D, cs) if x_is_rows else (cs, D)

    def body(x_ref, y_ref, o_ref):
        k = pl.program_id(0)

        @pl.when(k == 0)
        def _():
            o_ref[...] = jnp.zeros_like(o_ref)

        xb = x_ref[...].astype(BF16)
        if x_is_rows:
            xt = xb.T
            for j in range(ns):
                o_ref[j] += _nn(xt, y_ref[j])
        else:
            for j in range(ns):
                o_ref[j] += _tn(y_ref[j], xb)

        if alpha != 1.0:
            @pl.when(k == nk - 1)
            def _():
                o_ref[...] = o_ref[...] * alpha

    return pl.pallas_call(
        body, grid=(nk,),
        in_specs=[pl.BlockSpec((tk, D), lambda k: (k, 0)), pl.BlockSpec((ns, tk, cs), lambda k: (0, k, 0))],
        out_specs=pl.BlockSpec((ns,) + oshape, lambda k: (0, 0, 0)), out_shape=SDS((ns,) + oshape, F32),
        compiler_params=_params(("arbitrary",)), name=name)(x, y4)


def _hg_mask_table(rev):
    t = np.arange(CHUNK)[:, None]
    s = np.arange(CHUNK)[None, :]
    causal = (s >= t) if rev else (s <= t)
    out = [causal]
    for sh in (6, 5, 4):
        same = (t >> (sh + 1)) == (s >> (sh + 1))
        out.append(same & (((t >> sh) & 1) == (0 if rev else 1)) & (((s >> sh) & 1) == (1 if rev else 0)))
    out.append(((t >> 4) == (s >> 4)) & causal)
    out.append(causal.T)
    return jnp.asarray(np.stack(out).astype(np.float32))


def _hg_masks(mk_ref):
    on = [mk_ref[i] > 0.5 for i in range(5)]
    return on[0], on[1:4], on[4], mk_ref[0].astype(BF16), mk_ref[5].astype(BF16)


def _hg_intra_factors(q, k, b, b_scr, rev):
    b_scr[...] = b
    row = lax.broadcasted_iota(jnp.int32, (CHUNK, LANE), 0)
    out = []
    for sh in (6, 5, 4):
        lb = 1 << sh
        pieces = []
        for p in range(0, CHUNK, 2 * lb):
            r = p + lb if rev else p + lb - 1
            pieces.append(jnp.broadcast_to(b_scr[pl.ds(r, 1), :], (2 * lb, LANE)))
        ref = pieces[0] if len(pieces) == 1 else jnp.concatenate(pieces, axis=0)
        qside = jnp.bitwise_and(jnp.right_shift(row, sh), 1) == (0 if rev else 1)
        d = b - ref
        e = jnp.exp(jnp.minimum(jnp.where(qside, d, -d), 0.0))
        eq = jnp.where(qside, e, 0.0)
        ek = jnp.where(qside, 0.0, e)
        out.append((eq, ek, (q * eq).astype(BF16), (k * ek).astype(BF16)))
    pieces = []
    for a in range(0, CHUNK, 16):
        r = a + (8 if rev else 7)
        pieces.append(jnp.broadcast_to(b_scr[pl.ds(r, 1), :], (16, LANE)))
    ref = jnp.concatenate(pieces, axis=0)
    eq = jnp.exp(jnp.minimum(b - ref, 80.0))
    ek = jnp.exp(jnp.minimum(ref - b, 80.0))
    out.append((eq, ek, (q * eq).astype(BF16), (k * ek).astype(BF16)))
    return out


def _hg_gate(zf, l0, l1, valid):
    mx = jnp.maximum(l0, l1)
    e0, e1 = jnp.exp(l0 - mx), jnp.exp(l1 - mx)
    p0 = e0 / (e0 + e1)
    sg = _sigmoid(-zf)
    k = jnp.where(valid, (1.0 - p0) * sg, 0.0)
    return p0, sg, k, jnp.log(1.0 - k)


def _hg_fwd(z, lbp, *, rev, name):
    L = z.shape[0]
    nc = L // CHUNK
    fcol = 3 if rev else 2

    def cidx(j):
        return nc - 1 - j if rev else j

    def body(zq_ref, zi_ref, zf_ref, lb_ref, mk_ref, o_ref, ssave_ref, st_scr, b_scr):
        j = pl.program_id(0)

        @pl.when(j == 0)
        def _():
            st_scr[...] = jnp.zeros_like(st_scr)

        causal, lmasks, dmask, tri, _ = _hg_masks(mk_ref)
        rowg = cidx(j) * CHUNK + lax.broadcasted_iota(jnp.int32, (CHUNK, LANE), 0)
        valid = rowg >= PAD - N_META
        last = 0 if rev else CHUNK - 1
        for hh in range(HG_HEADS):
            sl = slice(LANE * hh, LANE * (hh + 1))
            zq = zq_ref[:, sl]
            q = zq * _sigmoid(zq)
            v = zi_ref[:, sl].astype(BF16)
            _, _, k, g = _hg_gate(zf_ref[:, sl], lb_ref[0:1, sl], lb_ref[1:2, sl], valid)
            b = _exact_left(tri, g)
            st = st_scr[hh]
            ssave_ref[0, hh] = st
            o = _nt((q * jnp.exp(b)).astype(BF16), st.astype(BF16))
            a = None
            fac = _hg_intra_factors(q, k, b, b_scr, rev)
            for (eq, ek, qq, kk), msk in zip(fac, lmasks + [dmask]):
                t = jnp.where(msk, _nt(qq, kk), 0.0)
                a = t if a is None else a + t
            o_ref[:, sl] = o + _nn(a.astype(BF16), v)
            bl = b_scr[pl.ds(last, 1), :]
            kd = (k * jnp.exp(bl - b)).astype(BF16)
            st_scr[hh] = st * jnp.exp(bl) + _tn(v, kd)

    zspec = lambda col: pl.BlockSpec((CHUNK, HG_W), lambda j: (cidx(j), col))
    return pl.pallas_call(
        body, grid=(nc,),
        in_specs=[zspec(0), zspec(1), zspec(fcol), pl.BlockSpec((2, HG_W), lambda j: (0, 0)),
                  pl.BlockSpec((6, CHUNK, CHUNK), lambda j: (0, 0, 0))],
        out_specs=[pl.BlockSpec((CHUNK, HG_W), lambda j: (cidx(j), 0)),
                   pl.BlockSpec((1, HG_HEADS, LANE, LANE), lambda j: (cidx(j), 0, 0, 0))],
        out_shape=[SDS((L, HG_W), F32), SDS((nc, HG_HEADS, LANE, LANE), F32)],
        scratch_shapes=[pltpu.VMEM((HG_HEADS, LANE, LANE), F32), pltpu.VMEM((CHUNK, LANE), F32)],
        compiler_params=_params(("arbitrary",)), name=name)(z, z, z, lbp, _hg_mask_table(rev))


def _hg_bwd(z, lbp, do, ssave, prev, *, rev, name):
    L = z.shape[0]
    nc = L // CHUNK
    fcol = 3 if rev else 2
    final = prev is not None

    def cidx(j):
        return j if rev else nc - 1 - j

    def body(*refs):
        zq_ref, zi_ref, zf_ref, lb_ref, mk_ref, do_ref, ss_ref = refs[:7]
        pos = 7
        if final:
            dqin_ref, dvin_ref = refs[7:9]
            pos = 9
        dq_ref, dv_ref, dzf_ref, dlb_ref, dst_scr, b_scr = refs[pos:pos + 6]
        j = pl.program_id(0)

        @pl.when(j == 0)
        def _():
            dst_scr[...] = jnp.zeros_like(dst_scr)
            dlb_ref[...] = jnp.zeros_like(dlb_ref)

        causal, lmasks, dmask, tri, tri_t = _hg_masks(mk_ref)
        rowg = cidx(j) * CHUNK + lax.broadcasted_iota(jnp.int32, (CHUNK, LANE), 0)
        valid = rowg >= PAD - N_META
        last = 0 if rev else CHUNK - 1
        for hh in range(HG_HEADS):
            sl = slice(LANE * hh, LANE * (hh + 1))
            zq = zq_ref[:, sl]
            sq = _sigmoid(zq)
            q = zq * sq
            v = zi_ref[:, sl].astype(BF16)
            p0, sg, k, g = _hg_gate(zf_ref[:, sl], lb_ref[0:1, sl], lb_ref[1:2, sl], valid)
            b = _exact_left(tri, g)
            dob = do_ref[:, sl].astype(BF16)
            st = ss_ref[0, hh]
            dst = dst_scr[hh]
            stb, dstb = st.astype(BF16), dst.astype(BF16)
            eb = jnp.exp(b)
            qe = (q * eb).astype(BF16)
            fac = _hg_intra_factors(q, k, b, b_scr, rev)
            bl = b_scr[pl.ds(last, 1), :]
            ebl = jnp.exp(bl)
            kde = jnp.exp(bl - b)
            kd = (k * kde).astype(BF16)
            da = jnp.where(causal, _nt(dob, v), 0.0)
            dq = eb * _nn(dob, stb)
            dk_inter = kde * _nn(v, dstb)
            dk = dk_inter
            dv = _nt(kd, dstb)
            a = None
            db = q * dq - k * dk
            for (eq, ek, qq, kk), msk in zip(fac, lmasks + [dmask]):
                t = jnp.where(msk, _nt(qq, kk), 0.0)
                a = t if a is None else a + t
                dal = jnp.where(msk, da, 0.0).astype(BF16)
                mq = _nn(dal, kk)
                mk = _tn(dal, qq)
                dq = dq + eq * mq
                dk = dk + ek * mk
                db = db + (qq.astype(F32) * mq - kk.astype(F32) * mk)
            dv = dv + _tn(a.astype(BF16), dob)
            extra = ebl * jnp.sum(st * dst, axis=0, keepdims=True) + jnp.sum(k * dk_inter, axis=0, keepdims=True)
            dst_scr[hh] = dst * ebl + _tn(dob, qe)
            dg = _exact_left(tri_t, db) + extra
            dk_tot = dk - dg / (1.0 - k)
            dzf_ref[:, sl] = jnp.where(valid, dk_tot * (1.0 - p0) * (-sg * (1.0 - sg)), 0.0).astype(BF16)
            dlb_ref[:, sl] += jnp.sum(jnp.where(valid, -sg * dk_tot, 0.0), axis=0, keepdims=True)
            if final:
                dq_ref[:, sl] = ((dq + dqin_ref[:, sl]) * (sq * (1.0 + zq * (1.0 - sq)))).astype(BF16)
                dv_ref[:, sl] = (dv + dvin_ref[:, sl]).astype(BF16)
            else:
                dq_ref[:, sl] = dq
                dv_ref[:, sl] = dv

    zspec = lambda col: pl.BlockSpec((CHUNK, HG_W), lambda j: (cidx(j), col))
    rspec = pl.BlockSpec((CHUNK, HG_W), lambda j: (cidx(j), 0))
    in_specs = [zspec(0), zspec(1), zspec(fcol), pl.BlockSpec((2, HG_W), lambda j: (0, 0)),
                pl.BlockSpec((6, CHUNK, CHUNK), lambda j: (0, 0, 0)), rspec,
                pl.BlockSpec((1, HG_HEADS, LANE, LANE), lambda j: (cidx(j), 0, 0, 0))]
    args = [z, z, z, lbp, _hg_mask_table(rev), do, ssave]
    if final:
        in_specs += [rspec, rspec]
        args += list(prev)
    odt = BF16 if final else F32
    return pl.pallas_call(
        body, grid=(nc,), in_specs=in_specs,
        out_specs=[rspec, rspec, rspec, pl.BlockSpec((1, HG_W), lambda j: (0, 0))],
        out_shape=[SDS((L, HG_W), odt), SDS((L, HG_W), odt), SDS((L, HG_W), BF16), SDS((1, HG_W), F32)],
        scratch_shapes=[pltpu.VMEM((HG_HEADS, LANE, LANE), F32), pltpu.VMEM((CHUNK, LANE), F32)],
        compiler_params=_params(("arbitrary",)), name=name)(*args)


def _hg_post_fwd(of, ob, z, w, *, name):
    L = of.shape[0]
    tm = _tile(L, 512)

    def body(of_ref, ob_ref, zg_ref, w_ref, y_ref):
        for hh in range(HG_HEADS):
            sl = slice(LANE * hh, LANE * (hh + 1))
            o = of_ref[:, sl] + ob_ref[:, sl]
            r = lax.rsqrt(jnp.mean(o * o, axis=-1, keepdims=True) + EPS)
            zg = zg_ref[:, sl]
            y_ref[:, sl] = (o * r * w_ref[:, sl] * (zg * _sigmoid(zg))).astype(BF16)

    row = pl.BlockSpec((tm, HG_W), lambda i: (i, 0))
    return pl.pallas_call(
        body, grid=(L // tm,),
        in_specs=[row, row, pl.BlockSpec((tm, HG_W), lambda i: (i, 4)), pl.BlockSpec((1, HG_W), lambda i: (0, 0))],
        out_specs=row, out_shape=SDS((L, HG_W), BF16),
        compiler_params=_params(("parallel",)), name=name)(of, ob, z, w)


def _hg_post_bwd(dy, of, ob, z, w, *, name):
    L = of.shape[0]
    tm = _tile(L, 512)

    def body(dy_ref, of_ref, ob_ref, zg_ref, w_ref, do_ref, dzg_ref, dw_ref):
        @pl.when(pl.program_id(0) == 0)
        def _():
            dw_ref[...] = jnp.zeros_like(dw_ref)

        for hh in range(HG_HEADS):
            sl = slice(LANE * hh, LANE * (hh + 1))
            o = of_ref[:, sl] + ob_ref[:, sl]
            r = lax.rsqrt(jnp.mean(o * o, axis=-1, keepdims=True) + EPS)
            xh = o * r
            zg = zg_ref[:, sl]
            sg = _sigmoid(zg)
            w = w_ref[:, sl]
            dy = dy_ref[:, sl]
            dys = dy * (zg * sg)
            dzg_ref[:, sl] = (dy * xh * w * (sg * (1.0 + zg * (1.0 - sg)))).astype(BF16)
            dw_ref[:, sl] += jnp.sum(dys * xh, axis=0, keepdims=True)
            dxh = dys * w
            do_ref[:, sl] = r * (dxh - xh * jnp.mean(dxh * xh, axis=-1, keepdims=True))

    row = pl.BlockSpec((tm, HG_W), lambda i: (i, 0))
    vec = pl.BlockSpec((1, HG_W), lambda i: (0, 0))
    return pl.pallas_call(
        body, grid=(L // tm,),
        in_specs=[row, row, row, pl.BlockSpec((tm, HG_W), lambda i: (i, 4)), vec],
        out_specs=[row, row, vec],
        out_shape=[SDS((L, HG_W), F32), SDS((L, HG_W), BF16), SDS((1, HG_W), F32)],
        compiler_params=_params(("arbitrary",)), name=name)(dy, of, ob, z, w)


N_GROUPS = (AT_HEADS + AT_KV) // 2


def _qk_to_group(wqk):
    d = wqk.shape[0]
    return wqk.reshape(d, N_GROUPS, 2, AT_HD // 2, 2).transpose(0, 1, 4, 2, 3).reshape(d, N_GROUPS * LANE)


def _qk_from_group(wqk):
    d = wqk.shape[0]
    return wqk.reshape(d, N_GROUPS, 2, 2, AT_HD // 2).transpose(0, 1, 3, 4, 2).reshape(d, N_GROUPS * LANE)


def _group_vec(w64):
    halves = w64.reshape(AT_HD // 2, 2).T
    return jnp.broadcast_to(halves[:, None, :], (2, 2, AT_HD // 2)).reshape(1, LANE)


def _ungroup_vec(w128):
    w = w128.reshape(2, 2, 32).sum(axis=1)
    return w.T.reshape(1, AT_HD)


def _rope_tables(L):
    n_real = L - PAD
    t = np.arange(n_real)
    row = np.concatenate([np.zeros(PAD), t // GRID_W]).astype(np.float32)
    col = np.concatenate([np.zeros(PAD), t % GRID_W]).astype(np.float32)
    inv = jnp.asarray(ROPE_THETA, F32) ** (-jnp.arange(0, AT_HD // 2, 2, dtype=F32) / (AT_HD // 2))
    ang = jnp.concatenate([jnp.asarray(row)[:, None] * inv, jnp.asarray(col)[:, None] * inv], axis=-1)
    cos, sin = jnp.cos(ang), jnp.sin(ang)
    cc = jnp.tile(cos, (1, 4))
    ss = jnp.concatenate([-sin, -sin, sin, sin], axis=1)
    return cc, ss


def _seg_matrix():
    a = lax.broadcasted_iota(jnp.int32, (LANE, LANE), 0)
    b = lax.broadcasted_iota(jnp.int32, (LANE, LANE), 1)
    same = jnp.bitwise_and(jnp.right_shift(a, 5), 1) == jnp.bitwise_and(jnp.right_shift(b, 5), 1)
    return jnp.where(same, 1.0, 0.0).astype(BF16)


def _slot_mask(shape, hp):
    lane = lax.broadcasted_iota(jnp.int32, shape, 1)
    return jnp.bitwise_and(jnp.right_shift(lane, 5), 1) == hp


def _at_prep(z, cc, ss, wq, wk, *, name):
    L = z.shape[0]
    tm = PAD
    qcol = Z_HG // AT_W
    kvcol = (Z_HG + AT_W) // (2 * LANE)

    def body(zq_ref, zkv_ref, cc_ref, ss_ref, wq_ref, wk_ref, qm_ref, qt_ref, kr_ref, krt_ref, vb_ref, vt_ref):
        seg = _seg_matrix()
        cc, ss = cc_ref[...], ss_ref[...]

        def normrope(x, w):
            r = lax.rsqrt(_exact_right(x * x, seg) * (1.0 / AT_HD) + EPS)
            y = x * r * w
            return y * cc + pltpu.roll(y, 64, 1) * ss

        for g in range(AT_HEADS // 2):
            o = normrope(zq_ref[:, LANE * g:LANE * (g + 1)], wq_ref[...]) * (AT_HD ** -0.5)
            for hp in range(2):
                h = 2 * g + hp
                tgt = h // (AT_HEADS // AT_KV)
                xm = jnp.where(_slot_mask(o.shape, hp), o, 0.0)
                if tgt != hp:
                    xm = pltpu.roll(xm, 32 if tgt == 1 else 96, 1)
                qm_ref[h] = xm.astype(BF16)
                qt_ref[h] = xm.T.astype(BF16)
        kr = normrope(zkv_ref[:, :LANE], wk_ref[...])
        kr_ref[...] = kr.astype(BF16)
        krt_ref[0] = kr.T.astype(BF16)
        v = zkv_ref[:, LANE:]
        low = lax.broadcasted_iota(jnp.int32, v.shape, 1) < AT_HD
        vb_ref[0] = jnp.where(low, v, 0.0).astype(BF16)
        vb_ref[1] = jnp.where(low, pltpu.roll(v, AT_HD, 1), 0.0).astype(BF16)
        vt = v.T.astype(BF16)
        ones = jnp.ones((VT_ROWS - AT_HD, tm), BF16)
        for j in range(AT_KV):
            vt_ref[j, 0, 0:AT_HD, :] = vt[AT_HD * j:AT_HD * (j + 1)]
            vt_ref[j, 0, AT_HD:VT_ROWS, :] = ones

    tab = pl.BlockSpec((tm, LANE), lambda i: (i, 0))
    vec = pl.BlockSpec((1, LANE), lambda i: (0, 0))
    nt = L // tm
    return pl.pallas_call(
        body, grid=(nt,),
        in_specs=[pl.BlockSpec((tm, AT_W), lambda i: (i, qcol)), pl.BlockSpec((tm, 2 * LANE), lambda i: (i, kvcol)),
                  tab, tab, vec, vec],
        out_specs=[pl.BlockSpec((AT_HEADS, tm, LANE), lambda i: (0, i, 0)),
                   pl.BlockSpec((AT_HEADS, LANE, tm), lambda i: (0, 0, i)), tab,
                   pl.BlockSpec((1, LANE, tm), lambda i: (i, 0, 0)),
                   pl.BlockSpec((AT_KV, tm, LANE), lambda i: (0, i, 0)),
                   pl.BlockSpec((AT_KV, 1, VT_ROWS, tm), lambda i: (0, i, 0, 0))],
        out_shape=[SDS((AT_HEADS, L, LANE), BF16), SDS((AT_HEADS, LANE, L), BF16), SDS((L, LANE), BF16),
                   SDS((nt, LANE, tm), BF16), SDS((AT_KV, L, LANE), BF16), SDS((AT_KV, nt, VT_ROWS, tm), BF16)],
        compiler_params=_params(("parallel",)), name=name)(z, z, cc, ss, wq, wk)


def _at_prep_bwd(dqm, dk2, dv2, z, cc, ss, wq, wk, *, name):
    L = z.shape[0]
    tm = PAD
    qcol = Z_HG // AT_W
    kvcol = (Z_HG + AT_W) // (2 * LANE)

    def body(dqm_ref, dk2_ref, dv2_ref, zq_ref, zkv_ref, cc_ref, ss_ref, wq_ref, wk_ref, dz_ref, dwq_ref, dwk_ref):
        @pl.when(pl.program_id(0) == 0)
        def _():
            dwq_ref[...] = jnp.zeros_like(dwq_ref)
            dwk_ref[...] = jnp.zeros_like(dwk_ref)

        seg = _seg_matrix()
        cc, ss = cc_ref[...], ss_ref[...]

        def back(x, w, do):
            dy = do * cc + pltpu.roll(do * ss, 64, 1)
            r = lax.rsqrt(_exact_right(x * x, seg) * (1.0 / AT_HD) + EPS)
            xh = x * r
            dxh = dy * w
            dx = r * (dxh - xh * (_exact_right(dxh * xh, seg) * (1.0 / AT_HD)))
            return dx, jnp.sum(dy * xh, axis=0, keepdims=True)

        for g in range(AT_HEADS // 2):
            do = None
            for hp in range(2):
                h = 2 * g + hp
                tgt = h // (AT_HEADS // AT_KV)
                d = jnp.where(_slot_mask((tm, LANE), tgt), dqm_ref[h], 0.0)
                if tgt != hp:
                    d = pltpu.roll(d, 96 if tgt == 1 else 32, 1)
                do = d if do is None else do + d
            dx, dw = back(zq_ref[:, LANE * g:LANE * (g + 1)], wq_ref[...], do * (AT_HD ** -0.5))
            dz_ref[:, LANE * g:LANE * (g + 1)] = dx.astype(BF16)
            dwq_ref[...] += dw
        dx, dw = back(zkv_ref[:, :LANE], wk_ref[...], dk2_ref[0] + dk2_ref[1])
        dz_ref[:, AT_W:AT_W + LANE] = dx.astype(BF16)
        dwk_ref[...] += dw
        dv0 = dv2_ref[0]
        low = lax.broadcasted_iota(jnp.int32, dv0.shape, 1) < AT_HD
        dz_ref[:, AT_W + LANE:] = jnp.where(low, dv0, pltpu.roll(dv2_ref[1], AT_HD, 1)).astype(BF16)

    tab = pl.BlockSpec((tm, LANE), lambda i: (i, 0))
    vec = pl.BlockSpec((1, LANE), lambda i: (0, 0))
    two = pl.BlockSpec((AT_KV, tm, LANE), lambda i: (0, i, 0))
    return pl.pallas_call(
        body, grid=(L // tm,),
        in_specs=[pl.BlockSpec((AT_HEADS, tm, LANE), lambda i: (0, i, 0)), two, two,
                  pl.BlockSpec((tm, AT_W), lambda i: (i, qcol)), pl.BlockSpec((tm, 2 * LANE), lambda i: (i, kvcol)),
                  tab, tab, vec, vec],
        out_specs=[pl.BlockSpec((tm, Z_AT), lambda i: (i, 0)), vec, vec],
        out_shape=[SDS((L, Z_AT), BF16), SDS((1, LANE), F32), SDS((1, LANE), F32)],
        compiler_params=_params(("arbitrary",)), name=name)(dqm, dk2, dv2, z, z, cc, ss, wq, wk)


def _at_fwd(qt, kr, vt, *, name):
    L = kr.shape[0]
    G = AT_HEADS // AT_KV
    tq = _tile(L, 384)
    tk = PAD
    nk = L // tk
    R = G * tq
    per = FWD_CHUNKS_PER_STEP if (nk - 1) % FWD_CHUNKS_PER_STEP == 0 else 1

    def body(q_ref, k_ref, v_ref, ob_ref, of_ref, lse_ref, m_scr, acc_scr):
        i = pl.program_id(1)
        qt = jnp.concatenate([q_ref[g] for g in range(G)], axis=1)
        m_scr[...] = jnp.full_like(m_scr, NEG)
        acc_scr[...] = jnp.zeros_like(acc_scr)

        def chunks(c, n, masked):
            start = c * tk if isinstance(c, int) else pl.multiple_of(c * tk, tk)
            st = _nn(k_ref[pl.ds(start, n * tk), :], qt).astype(BF16)
            if masked:
                key = lax.broadcasted_iota(jnp.int32, st.shape, 0)
                st = jnp.where(key >= PAD - N_META, st, NEG)
            m_prev = m_scr[...]
            m_new = jnp.maximum(m_prev, jnp.max(st, axis=0, keepdims=True).astype(F32))
            pt = jnp.exp(st - m_new.astype(BF16))
            acc = jnp.exp(m_prev - m_new) * acc_scr[...]
            for u in range(n):
                acc = acc + _nn(v_ref[0, c + u], pt[u * tk:(u + 1) * tk])
            acc_scr[...] = acc
            m_scr[...] = m_new

        chunks(0, 1, True)

        def loop(t, carry):
            chunks(1 + per * t, per, False)
            return carry

        lax.fori_loop(0, (nk - 1) // per, loop, 0)
        l = acc_scr[pl.ds(AT_HD, 1), :]
        lse = m_scr[...] + jnp.log(l)
        on = acc_scr[0:AT_HD, :] / l
        o = jnp.concatenate([on[:, g * tq:(g + 1) * tq] for g in range(G)], axis=0).T
        rowg = i * tq + lax.broadcasted_iota(jnp.int32, o.shape, 0)
        o = jnp.where(rowg >= PAD - N_META, o, 0.0)
        ob_ref[...] = o.astype(BF16)
        of_ref[...] = o
        for g in range(G):
            lse_ref[g] = lse[:, g * tq:(g + 1) * tq]

    ospec = pl.BlockSpec((tq, G * AT_HD), lambda j, i: (i, j))
    return pl.pallas_call(
        body, grid=(AT_KV, L // tq),
        in_specs=[pl.BlockSpec((G, LANE, tq), lambda j, i: (j, 0, i)), pl.BlockSpec((L, LANE), lambda j, i: (0, 0)),
                  pl.BlockSpec((1, nk, VT_ROWS, tk), lambda j, i: (j, 0, 0, 0))],
        out_specs=[ospec, ospec, pl.BlockSpec((G, 1, tq), lambda j, i: (j, 0, i))],
        out_shape=[SDS((L, AT_W), BF16), SDS((L, AT_W), F32), SDS((AT_HEADS, 1, L), F32)],
        scratch_shapes=[pltpu.VMEM((1, R), F32), pltpu.VMEM((VT_ROWS, R), F32)],
        compiler_params=_params(("parallel", "parallel")), name=name)(qt, kr, vt)


def _at_bwd(qm, qt, kr, krt, vb, do, of, lse, *, name):
    L = kr.shape[0]
    G = AT_HEADS // AT_KV
    tq = _tile(L, 384)
    tk = PAD
    nk = L // tk
    nq = L // tq
    R = G * tq

    def body(qm_ref, q_ref, k_hbm, kt_hbm, v_hbm, do_ref, o_ref, lse_ref, dq_ref, dk_hbm, dv_hbm,
             k_scr, kt_scr, v_scr, dk_scr, dv_scr, dq_scr, sem):
        j, i = pl.program_id(0), pl.program_id(1)

        @pl.when(i == 0)
        def _():
            cps = [pltpu.make_async_copy(k_hbm, k_scr, sem.at[0]), pltpu.make_async_copy(kt_hbm, kt_scr, sem.at[1]),
                   pltpu.make_async_copy(v_hbm.at[j], v_scr, sem.at[2])]
            for cp in cps:
                cp.start()
            dk_scr[...] = jnp.zeros_like(dk_scr)
            dv_scr[...] = jnp.zeros_like(dv_scr)
            for cp in cps:
                cp.wait()

        qt = jnp.concatenate([q_ref[g] for g in range(G)], axis=1)
        rowg = i * tq + lax.broadcasted_iota(jnp.int32, (tq, G * AT_HD), 0)
        dot_all = jnp.where(rowg >= PAD - N_META, do_ref[...], 0.0).T
        ot_all = o_ref[...].T
        dot = jnp.concatenate([dot_all[AT_HD * g:AT_HD * (g + 1)] for g in range(G)], axis=1)
        ot = jnp.concatenate([ot_all[AT_HD * g:AT_HD * (g + 1)] for g in range(G)], axis=1)
        delta = jnp.sum(dot * ot, axis=0, keepdims=True)
        dot128 = jnp.concatenate([dot, jnp.zeros_like(dot)], axis=0)
        dor = dot128.T.astype(BF16)
        dot128 = dot128.astype(BF16)
        qr = qm_ref[...].reshape(R, LANE)
        lse_v = jnp.concatenate([lse_ref[g] for g in range(G)], axis=1)
        dq_scr[...] = jnp.zeros_like(dq_scr)

        def chunk(c, masked):
            start = c * tk if isinstance(c, int) else pl.multiple_of(c * tk, tk)
            k = k_scr[pl.ds(start, tk), :]
            kt = kt_scr[c]
            v = v_scr[pl.ds(start, tk), :]
            st = _nn(k, qt)
            if masked:
                key = lax.broadcasted_iota(jnp.int32, st.shape, 0)
                st = jnp.where(key >= PAD - N_META, st, NEG)
            pt = jnp.exp(st - lse_v)
            dst = (pt * (_nn(v, dot128) - delta)).astype(BF16)
            dq_scr[...] += _nn(kt, dst)
            dk_scr[pl.ds(start, tk), :] += _nn(dst, qr)
            dv_scr[pl.ds(start, tk), :] += _nn(pt.astype(BF16), dor)

        chunk(0, True)

        def loop(c, carry):
            chunk(c, False)
            return carry

        lax.fori_loop(1, nk, loop, 0)
        dq_ref[...] = dq_scr[...].T.reshape(G, tq, LANE)

        @pl.when(i == nq - 1)
        def _():
            ck = pltpu.make_async_copy(dk_scr, dk_hbm.at[j], sem.at[0])
            cv = pltpu.make_async_copy(dv_scr, dv_hbm.at[j], sem.at[1])
            ck.start()
            cv.start()
            ck.wait()
            cv.wait()

    anyspec = pl.BlockSpec(memory_space=pl.ANY)
    ospec = pl.BlockSpec((tq, G * AT_HD), lambda j, i: (i, j))
    return pl.pallas_call(
        body, grid=(AT_KV, nq),
        in_specs=[pl.BlockSpec((G, tq, LANE), lambda j, i: (j, i, 0)), pl.BlockSpec((G, LANE, tq), lambda j, i: (j, 0, i)),
                  anyspec, anyspec, anyspec, ospec, ospec, pl.BlockSpec((G, 1, tq), lambda j, i: (j, 0, i))],
        out_specs=[pl.BlockSpec((G, tq, LANE), lambda j, i: (j, i, 0)), anyspec, anyspec],
        out_shape=[SDS((AT_HEADS, L, LANE), F32), SDS((AT_KV, L, LANE), F32), SDS((AT_KV, L, LANE), F32)],
        scratch_shapes=[pltpu.VMEM((L, LANE), BF16), pltpu.VMEM((nk, LANE, tk), BF16), pltpu.VMEM((L, LANE), BF16),
                        pltpu.VMEM((L, LANE), F32), pltpu.VMEM((L, LANE), F32), pltpu.VMEM((LANE, R), F32),
                        pltpu.SemaphoreType.DMA((3,))],
        compiler_params=_params(("arbitrary", "arbitrary"), VMEM_LARGE), name=name)(qm, qt, kr, krt, vb, do, of, lse)


def _merge_fwd(ya, o8, wua, wubp, z, *, name):
    L = ya.shape[0]
    D = wua.shape[1]
    tm, tn = _tile(L, 1536), 256
    ga, gb = (Z_HG + Z_AT) // tn, (Z_HG + Z_AT + D) // tn

    def body(ya_ref, o8_ref, wa_ref, wb_ref, za_ref, zb_ref, mix_ref):
        pa = _nn(ya_ref[...], wa_ref[...])
        pb = _nn(o8_ref[...], wb_ref[...])
        mix_ref[...] = (_sigmoid(za_ref[...]) * pa + _sigmoid(zb_ref[...]) * pb).astype(BF16)

    return pl.pallas_call(
        body, grid=(D // tn, L // tm),
        in_specs=[pl.BlockSpec((tm, ya.shape[1]), lambda j, i: (i, 0)), pl.BlockSpec((tm, o8.shape[1]), lambda j, i: (i, 0)),
                  pl.BlockSpec((wua.shape[0], tn), lambda j, i: (0, j)), pl.BlockSpec((wubp.shape[0], tn), lambda j, i: (0, j)),
                  pl.BlockSpec((tm, tn), lambda j, i: (i, ga + j)), pl.BlockSpec((tm, tn), lambda j, i: (i, gb + j))],
        out_specs=pl.BlockSpec((tm, tn), lambda j, i: (i, j)), out_shape=SDS((L, D), BF16),
        compiler_params=_params(("parallel", "parallel")), name=name)(ya, o8, wua, wubp, z, z)


def _merge_bwd(dh, wout, ya, o8, wua, wubp, z, *, name):
    L = ya.shape[0]
    D = wua.shape[1]
    tm, tn = _tile(L, 1536), 256
    ga, gb = (Z_HG + Z_AT) // tn, (Z_HG + Z_AT + D) // tn

    def body(dh_ref, wo_ref, ya_ref, o8_ref, wa_ref, wb_ref, za_ref, zb_ref, dpa_ref, dpb_ref, dza_ref, dzb_ref):
        dm = _nt(dh_ref[...].astype(BF16), wo_ref[...])
        pa = _nn(ya_ref[...], wa_ref[...])
        pb = _nn(o8_ref[...], wb_ref[...])
        sa, sb = _sigmoid(za_ref[...]), _sigmoid(zb_ref[...])
        dpa_ref[...] = (dm * sa).astype(BF16)
        dpb_ref[...] = (dm * sb).astype(BF16)
        dza_ref[...] = (dm * pa * sa * (1.0 - sa)).astype(BF16)
        dzb_ref[...] = (dm * pb * sb * (1.0 - sb)).astype(BF16)

    ospec = pl.BlockSpec((tm, tn), lambda j, i: (i, j))
    return pl.pallas_call(
        body, grid=(D // tn, L // tm),
        in_specs=[pl.BlockSpec((tm, D), lambda j, i: (i, 0)), pl.BlockSpec((tn, D), lambda j, i: (j, 0)),
                  pl.BlockSpec((tm, ya.shape[1]), lambda j, i: (i, 0)), pl.BlockSpec((tm, o8.shape[1]), lambda j, i: (i, 0)),
                  pl.BlockSpec((wua.shape[0], tn), lambda j, i: (0, j)), pl.BlockSpec((wubp.shape[0], tn), lambda j, i: (0, j)),
                  pl.BlockSpec((tm, tn), lambda j, i: (i, ga + j)), pl.BlockSpec((tm, tn), lambda j, i: (i, gb + j))],
        out_specs=[ospec] * 4, out_shape=[SDS((L, D), BF16)] * 4,
        compiler_params=_params(("parallel", "parallel")), name=name)(dh, wout, ya, o8, wua, wubp, z, z)


def _loss_head(h, tgt, *, name):
    L, D = h.shape
    tm = PAD

    def body(h_ref, t_ref, dh_ref, ls_ref):
        i = pl.program_id(0)

        @pl.when(i == 0)
        def _():
            ls_ref[...] = jnp.zeros_like(ls_ref)
            dh_ref[...] = jnp.zeros_like(dh_ref)

        @pl.when(i > 0)
        def _():
            e = h_ref[...] - t_ref[...]
            dh_ref[...] = e * (1.0 / D)
            s = jnp.sum(e * e, axis=0, keepdims=True)
            tot = s[:, :LANE]
            for c in range(1, D // LANE):
                tot = tot + s[:, LANE * c:LANE * (c + 1)]
            ls_ref[...] += tot

    return pl.pallas_call(
        body, grid=(L // tm,),
        in_specs=[pl.BlockSpec((tm, D), lambda i: (i, 0)), pl.BlockSpec((tm, D), lambda i: (jnp.maximum(i - 1, 0), 0))],
        out_specs=[pl.BlockSpec((tm, D), lambda i: (i, 0)), pl.BlockSpec((1, LANE), lambda i: (0, 0))],
        out_shape=[SDS((L, D), F32), SDS((1, LANE), F32)],
        compiler_params=_params(("arbitrary",)), name=name)(h, tgt)


def _adamw(w, g, m, v, *, name):
    shape = w.shape
    w2, g2, m2, v2 = [a.reshape(-1, shape[-1]) for a in (w, g, m, v)]
    rows, cols = w2.shape
    tr = _tile(rows, 256, 8)

    def body(w_ref, g_ref, m_ref, v_ref, d_ref, nm_ref, nv_ref):
        g = g_ref[...]
        m = ADAM_B1 * m_ref[...] + (1.0 - ADAM_B1) * g
        v = ADAM_B2 * v_ref[...] + (1.0 - ADAM_B2) * (g * g)
        m_hat = m / (1.0 - ADAM_B1 ** ADAM_STEP)
        v_hat = v / (1.0 - ADAM_B2 ** ADAM_STEP)
        d_ref[...] = -ADAM_LR * (m_hat / (jnp.sqrt(v_hat) + ADAM_EPS) + ADAM_WD * w_ref[...])
        nm_ref[...] = m
        nv_ref[...] = v

    spec = pl.BlockSpec((tr, cols), lambda i: (i, 0))
    outs = pl.pallas_call(
        body, grid=(rows // tr,), in_specs=[spec] * 4, out_specs=[spec] * 3, out_shape=[SDS((rows, cols), F32)] * 3,
        compiler_params=_params(("parallel",)), name=name)(w2, g2, m2, v2)
    return [o.reshape(shape) for o in outs]


def _place():
    return lax.axis_index("x"), lax.axis_index("y"), lax.axis_index("c")


def _allgather_small(v, *, name):
    m_per, n = v.shape

    def body(x_ref, out_ref, send_sems, recv_sems, local_sem):
        x, y, c = _place()
        me, sibling = (x, y, c), (x, y, 1 - c)
        chips = [(1 - x, y), (x, 1 - y), (1 - x, 1 - y)]

        def rows(px, py, pc):
            return out_ref.at[pl.ds((4 * px + 2 * py + pc) * m_per, m_per), :]

        def copy(k, block, to, src=None):
            return pltpu.make_async_remote_copy(
                src_ref=rows(*block) if src is None else src, dst_ref=rows(*block),
                send_sem=send_sems.at[k], recv_sem=recv_sems.at[k], device_id=to, device_id_type=MESH)

        mine = pltpu.make_async_copy(x_ref, rows(*me), local_sem)
        mine.start()
        first = [copy(0, me, sibling, src=x_ref)]
        first += [copy(1 + j, me, (*chip, c), src=x_ref) for j, chip in enumerate(chips)]
        for cp in first:
            cp.start()
        passed = [copy(4 + j, (*chip, c), sibling) for j, chip in enumerate(chips)]
        for j, chip in enumerate(chips):
            copy(1 + j, (*chip, c), me).wait_recv()
            passed[j].start()
        copy(0, sibling, me).wait_recv()
        for j, chip in enumerate(chips):
            copy(4 + j, (*chip, 1 - c), me).wait_recv()
        for cp in first + passed:
            cp.wait_send()
        mine.wait()

    return pl.pallas_call(
        body, out_shape=SDS((8 * m_per, n), v.dtype),
        in_specs=[pl.BlockSpec(memory_space=pltpu.VMEM)], out_specs=pl.BlockSpec(memory_space=pltpu.VMEM),
        scratch_shapes=[pltpu.SemaphoreType.DMA((7,)), pltpu.SemaphoreType.DMA((7,)), pltpu.SemaphoreType.DMA],
        name=name)(v)


def _chips(x, y):
    return [(1 - x, y), (x, 1 - y), (1 - x, 1 - y)]


def _gather_mats(shards, *, name):
    n = len(shards)

    def body(*refs):
        ins, outs = refs[:n], refs[n:2 * n]
        send_sems, recv_sems, fsend_sems, frecv_sems = refs[2 * n:]
        x, y, c = _place()
        s_me, sibling, chips = 2 * x + y, (x, y, 1 - c), _chips(x, y)

        def copy(src, dst, ssem, rsem, to):
            return pltpu.make_async_remote_copy(src_ref=src, dst_ref=dst, send_sem=ssem, recv_sem=rsem,
                                                device_id=to, device_id_type=MESH)

        first = [copy(ins[t].at[c], outs[t].at[s_me, c], send_sems.at[3 * t + k], recv_sems.at[3 * t + k], (*chip, c))
                 for t in range(n) for k, chip in enumerate(chips)]
        for cp in first:
            cp.start()
        passed = []
        for t in range(n):
            for k, chip in enumerate(chips):
                slot = outs[t].at[2 * chip[0] + chip[1], c]
                copy(ins[t].at[c], slot, send_sems.at[3 * t + k], recv_sems.at[3 * t + k], (*chip, c)).wait_recv()
                fw = copy(slot, slot, fsend_sems.at[3 * t + k], frecv_sems.at[3 * t + k], sibling)
                fw.start()
                passed.append(fw)
        for t in range(n):
            for k, chip in enumerate(chips):
                slot = outs[t].at[2 * chip[0] + chip[1], 1 - c]
                copy(slot, slot, fsend_sems.at[3 * t + k], frecv_sems.at[3 * t + k], sibling).wait_recv()
        for cp in first + passed:
            cp.wait_send()

    anyspec = pl.BlockSpec(memory_space=pl.ANY)
    return pl.pallas_call(
        body, out_shape=[SDS((4,) + s.shape, s.dtype) for s in shards], in_specs=[anyspec] * n, out_specs=[anyspec] * n,
        scratch_shapes=[pltpu.SemaphoreType.DMA((3 * n,))] * 4, name=name)(*shards)


def _rs_pair_exchange(gs, *, name):
    n = len(gs)

    def body(*refs):
        ins, outs = refs[:n], refs[n:2 * n]
        send_sems, recv_sems = refs[2 * n:]
        x, y, c = _place()
        cps = [pltpu.make_async_remote_copy(src_ref=ins[t].at[k, 1 - c], dst_ref=outs[t].at[k],
                                            send_sem=send_sems.at[4 * t + k], recv_sem=recv_sems.at[4 * t + k],
                                            device_id=(x, y, 1 - c), device_id_type=MESH)
               for t in range(n) for k in range(4)]
        for cp in cps:
            cp.start()
        for cp in cps:
            cp.wait()

    anyspec = pl.BlockSpec(memory_space=pl.ANY)
    return pl.pallas_call(
        body, out_shape=[SDS((4,) + g.shape[2:], g.dtype) for g in gs], in_specs=[anyspec] * n, out_specs=[anyspec] * n,
        scratch_shapes=[pltpu.SemaphoreType.DMA((4 * n,))] * 2, name=name)(*gs)


def _rs_chip_exchange(parts, *, name):
    n = len(parts)

    def body(*refs):
        ins, outs = refs[:n], refs[n:2 * n]
        send_sems, recv_sems, local_sems = refs[2 * n:]
        x, y, c = _place()
        s_me, chips = 2 * x + y, _chips(x, y)

        def copy(t, k, chip, src_slot, dst_slot):
            return pltpu.make_async_remote_copy(
                src_ref=ins[t].at[src_slot], dst_ref=outs[t].at[dst_slot], send_sem=send_sems.at[3 * t + k],
                recv_sem=recv_sems.at[3 * t + k], device_id=(*chip, c), device_id_type=MESH)

        mine = [pltpu.make_async_copy(ins[t].at[s_me], outs[t].at[s_me], local_sems.at[t]) for t in range(n)]
        for cp in mine:
            cp.start()
        sends = [copy(t, k, chip, 2 * chip[0] + chip[1], s_me) for t in range(n) for k, chip in enumerate(chips)]
        for cp in sends:
            cp.start()
        for t in range(n):
            for k, chip in enumerate(chips):
                copy(t, k, chip, s_me, 2 * chip[0] + chip[1]).wait_recv()
        for cp in sends:
            cp.wait_send()
        for cp in mine:
            cp.wait()

    anyspec = pl.BlockSpec(memory_space=pl.ANY)
    return pl.pallas_call(
        body, out_shape=[SDS(p.shape, p.dtype) for p in parts], in_specs=[anyspec] * n, out_specs=[anyspec] * n,
        scratch_shapes=[pltpu.SemaphoreType.DMA((3 * n,))] * 2 + [pltpu.SemaphoreType.DMA((n,))], name=name)(*parts)


def _rs_pair_share(fulls, *, name):
    n = len(fulls)

    def body(*refs):
        ins, outs = refs[:n], refs[n:2 * n]
        send_sems, recv_sems = refs[2 * n:]
        x, y, c = _place()

        def copy(t, half):
            return pltpu.make_async_remote_copy(src_ref=ins[t].at[c], dst_ref=outs[t].at[half], send_sem=send_sems.at[t],
                                                recv_sem=recv_sems.at[t], device_id=(x, y, 1 - c), device_id_type=MESH)

        sends = [copy(t, c) for t in range(n)]
        for cp in sends:
            cp.start()
        for t in range(n):
            copy(t, 1 - c).wait_recv()
        for cp in sends:
            cp.wait_send()

    anyspec = pl.BlockSpec(memory_space=pl.ANY)
    return pl.pallas_call(
        body, out_shape=[SDS(f.shape, f.dtype) for f in fulls], in_specs=[anyspec] * n, out_specs=[anyspec] * n,
        input_output_aliases={t: t for t in range(n)},
        scratch_shapes=[pltpu.SemaphoreType.DMA((n,))] * 2, name=name)(*fulls)


def _add_half(g, other, c1, *, out_dtype, name):
    _, _, h, cs = g.shape
    tr = _tile(h, 512, 16)

    def body(c_ref, g_ref, o_ref, out_ref):
        out_ref[...] = (g_ref[...] + o_ref[...]).astype(out_dtype)

    spec = pl.BlockSpec((None, tr, cs), lambda k, i, c: (k, i, 0))
    return pl.pallas_call(
        body, out_shape=SDS(other.shape, out_dtype),
        grid_spec=pltpu.PrefetchScalarGridSpec(
            num_scalar_prefetch=1, grid=(4, h // tr),
            in_specs=[pl.BlockSpec((None, None, tr, cs), lambda k, i, c: (k, c[0], i, 0)), spec], out_specs=spec),
        compiler_params=_params(("parallel", "parallel")), name=name)(c1, g, other)


def _sum4(x, c1, *, name):
    n, h, cs = x.shape
    tr = _tile(h, 512, 16)

    def body(c_ref, x_ref, o_ref):
        tot = x_ref[0].astype(F32)
        for s in range(1, n):
            tot = tot + x_ref[s].astype(F32)
        o_ref[...] = tot

    return pl.pallas_call(
        body, out_shape=SDS((2, h, cs), F32),
        grid_spec=pltpu.PrefetchScalarGridSpec(
            num_scalar_prefetch=1, grid=(h // tr,),
            in_specs=[pl.BlockSpec((n, tr, cs), lambda i, c: (0, i, 0))],
            out_specs=pl.BlockSpec((None, tr, cs), lambda i, c: (c[0], i, 0))),
        compiler_params=_params(("parallel",)), name=name)(c1, x)


def _finish_small(gathered, lbf, lbb, *, rows, name):
    r_lbf, r_lbb = rows['lb_f'], rows['lb_b']

    def body(g_ref, lbf_ref, lbb_ref, o_ref, dlf_ref, dlb_ref):
        tot = g_ref[0]
        for s in range(1, 8):
            tot = tot + g_ref[s]
        o_ref[...] = tot
        o_ref[0:1, :] = jnp.broadcast_to(jnp.sum(o_ref[0:1, :], axis=1, keepdims=True), (1, LANE))
        for lb_ref, d_ref, r0 in ((lbf_ref, dlf_ref, r_lbf), (lbb_ref, dlb_ref, r_lbb)):
            for hh in range(HG_HEADS):
                sl = slice(LANE * hh, LANE * (hh + 1))
                l0, l1 = lb_ref[0:1, sl], lb_ref[1:2, sl]
                mx = jnp.maximum(l0, l1)
                e0, e1 = jnp.exp(l0 - mx), jnp.exp(l1 - mx)
                p0 = e0 / (e0 + e1)
                d0 = o_ref[r0 + hh:r0 + hh + 1, :] * p0 * (1.0 - p0)
                d_ref[0:1, sl] = d0
                d_ref[1:2, sl] = -d0

    vm = pl.BlockSpec(memory_space=pltpu.VMEM)
    return pl.pallas_call(
        body, in_specs=[vm, vm, vm], out_specs=[vm, vm, vm],
        out_shape=[SDS(gathered.shape[1:], F32), SDS(lbf.shape, F32), SDS(lbb.shape, F32)], name=name)(gathered, lbf, lbb)


def _local_step(x2, tgt2, meta, W, S):
    T, D = x2.shape
    L = PAD + T
    h0 = jnp.concatenate([jnp.zeros((PAD - N_META, D), F32), meta, x2], axis=0)

    qk0 = Z_HG
    w_in = jnp.concatenate([W['w_in'][:, :qk0], _qk_to_group(W['w_in'][:, qk0:qk0 + AT_W + AT_KVW]),
                            W['w_in'][:, qk0 + AT_W + AT_KVW:]], axis=1)
    cc, ss = _rope_tables(L)
    wq_g, wk_g = _group_vec(S['q_norm']), _group_vec(S['k_norm'])

    def ffn_fwd(h, nw, wg, wu, wd, tag):
        n = _rmsnorm_fwd(h, nw, name=tag + "_norm")
        g, u, a = _ffn4_up(n, wg, wu, name=tag + "_up")
        hn = _ffn4_down(a, wd, h, name=tag + "_down")
        return hn, (n, g, u, a)

    def ffn_bwd(dh, h, nw, wg, wu, wd, saved, tag, split=False):
        n, g, u, a = saved
        dg, du = _ffn4_dact(dh, wd, g, u, name=tag + "_dact")
        dn = _ffn4_dn(dg, du, wg, wu, name=tag + "_dn")
        dwg = _ffn4_dw(n, dg, x_is_rows=True, name=tag + "_dwg")
        dwu = _ffn4_dw(n, du, x_is_rows=True, name=tag + "_dwu")
        dwd = _ffn4_dw(dh, a, x_is_rows=False, alpha=0.5, name=tag + "_dwd")
        *dhp, dnw = _rmsnorm_bwd(h, nw, dn, dh, split=split, name=tag + "_norm_bwd")
        return (dhp if split else dhp[0]), dnw, dwg, dwu, dwd

    h1, sv1 = ffn_fwd(h0, S['ffn1_norm'], W['ffn1_w_gate'], W['ffn1_w_up'], W['ffn1_w_down'], "ffn1")
    um = _rmsnorm_fwd(h1, S['mix_norm'], name="mix_norm")
    z = _mm([(um, w_in)], tm=512, tn=1792, tk=D, name="in_proj")
    of, sf = _hg_fwd(z, S['hg_lb_fwd'], rev=False, name="hg_fwd_f")
    ob, sb = _hg_fwd(z, S['hg_lb_bwd'], rev=True, name="hg_fwd_b")
    ya = _hg_post_fwd(of, ob, z, S['hg_out_norm'], name="hg_post")
    qm, qt, kr, krt, vb, vt = _at_prep(z, cc, ss, wq_g, wk_g, name="at_prep")
    yb, yb_f32, lse = _at_fwd(qt, kr, vt, name="at_fwd")
    mixed = _merge_fwd(ya, yb, W['w_up_a'], W['w_up_b'], z, name="merge")
    h2 = _mm([(mixed, W['w_out'])], res=h1, tm=512, tn=D, tk=D, name="out_proj")
    h3, sv2 = ffn_fwd(h2, S['ffn2_norm'], W['ffn2_w_gate'], W['ffn2_w_up'], W['ffn2_w_down'], "ffn2")
    dh3, loss_lanes = _loss_head(h3, tgt2, name="loss_head")

    G = {}
    dh2, dn_ffn2, G['ffn2_w_gate'], G['ffn2_w_up'], G['ffn2_w_down'] = ffn_bwd(
        dh3, h2, S['ffn2_norm'], W['ffn2_w_gate'], W['ffn2_w_up'], W['ffn2_w_down'], sv2, "ffn2")
    dpa, dpb, dzga, dzgb = _merge_bwd(dh2, W['w_out'], ya, yb, W['w_up_a'], W['w_up_b'], z, name="merge_bwd")
    G['w_out'] = _mm([(mixed, dh2)], ta=True, tm=D, tn=D, tk=512, name="d_w_out")
    dya = _mm([(dpa, W['w_up_a'])], tb=True, tm=512, tn=HG_W, tk=D, name="d_ya")
    dyb = _mm([(dpb, W['w_up_b'])], tb=True, tm=512, tn=AT_W, tk=D, name="d_yb")
    G['w_up_a'] = _mm([(ya, dpa)], ta=True, tm=HG_W, tn=D, tk=512, name="d_w_up_a")
    G['w_up_b'] = _mm([(yb, dpb)], ta=True, tm=AT_W, tn=D, tk=512, name="d_w_up_b")
    do_hg, dzg, d_hgn = _hg_post_bwd(dya, of, ob, z, S['hg_out_norm'], name="hg_post_bwd")
    dq_f, dv_f, dzf_f, dlb_f = _hg_bwd(z, S['hg_lb_fwd'], do_hg, sf, None, rev=False, name="hg_bwd_f")
    dzq, dzi, dzf_b, dlb_b = _hg_bwd(z, S['hg_lb_bwd'], do_hg, sb, (dq_f, dv_f), rev=True, name="hg_bwd_b")
    dqm, dk2, dv2 = _at_bwd(qm, qt, kr, krt, vb, dyb, yb_f32, lse, name="at_bwd")
    dz_at, dwq_g, dwk_g = _at_prep_bwd(dqm, dk2, dv2, z, cc, ss, wq_g, wk_g, name="at_prep_bwd")
    dz = jnp.concatenate([dzq, dzi, dzf_f, dzf_b, dzg, dz_at, dzga, dzgb], axis=1)
    dum = _mm([(dz, w_in)], tb=True, tm=512, tn=D, tk=1792, name="d_um")
    dw_in_p = _mm([(um, dz)], ta=True, tm=D, tn=1792, tk=512, name="d_w_in")
    G['w_in'] = jnp.concatenate([dw_in_p[:, :qk0], _qk_from_group(dw_in_p[:, qk0:qk0 + AT_W + AT_KVW]),
                                 dw_in_p[:, qk0 + AT_W + AT_KVW:]], axis=1)
    dh1, dn_mix = _rmsnorm_bwd(h1, S['mix_norm'], dum, dh2, name="mix_norm_bwd")
    (grad_x, dmeta), dn_ffn1, G['ffn1_w_gate'], G['ffn1_w_up'], G['ffn1_w_down'] = ffn_bwd(
        dh1, h0, S['ffn1_norm'], W['ffn1_w_gate'], W['ffn1_w_up'], W['ffn1_w_down'], sv1, "ffn1", split=True)

    small_rows = [('loss', loss_lanes), ('ffn1_norm', dn_ffn1.reshape(-1, LANE)), ('mix_norm', dn_mix.reshape(-1, LANE)),
                  ('ffn2_norm', dn_ffn2.reshape(-1, LANE)), ('hg_out_norm', d_hgn.reshape(-1, LANE)),
                  ('lb_f', dlb_f.reshape(-1, LANE)), ('lb_b', dlb_b.reshape(-1, LANE)), ('q_norm', dwq_g), ('k_norm', dwk_g)]
    return grad_x, dmeta, G, small_rows


def kernel(x, meta_tokens, ffn1_norm, ffn1_w_gate, ffn1_w_up, ffn1_w_down, mix_norm, w_in, hg_lb_fwd, hg_lb_bwd, hg_out_norm, q_norm, k_norm, w_up_a, w_up_b, w_out, ffn2_norm, ffn2_w_gate, ffn2_w_up, ffn2_w_down, loss_target, m_meta_tokens, m_ffn1_norm, m_ffn1_w_gate, m_ffn1_w_up, m_ffn1_w_down, m_mix_norm, m_w_in, m_hg_lb_fwd, m_hg_lb_bwd, m_hg_out_norm, m_q_norm, m_k_norm, m_w_up_a, m_w_up_b, m_w_out, m_ffn2_norm, m_ffn2_w_gate, m_ffn2_w_up, m_ffn2_w_down, v_meta_tokens, v_ffn1_norm, v_ffn1_w_gate, v_ffn1_w_up, v_ffn1_w_down, v_mix_norm, v_w_in, v_hg_lb_fwd, v_hg_lb_bwd, v_hg_out_norm, v_q_norm, v_k_norm, v_w_up_a, v_w_up_b, v_w_out, v_ffn2_norm, v_ffn2_w_gate, v_ffn2_w_up, v_ffn2_w_down):
    given = dict(locals())
    w = {n: given[n] for n in WEIGHTS}
    mom = {n: given["m_" + n] for n in WEIGHTS}
    var = {n: given["v_" + n] for n in WEIGHTS}
    c = lax.axis_index("c")
    D = x.shape[-1]

    shapes = {n: w[n].shape[-2:] for n in MATS + ('meta_tokens',)}
    halves = [w[n].astype(BF16).reshape(2, shapes[n][0] // 2, shapes[n][1]) for n in MATS]
    gathered = _gather_mats(halves, name="gather_weights")
    s_me = 2 * lax.axis_index("x") + lax.axis_index("y")
    W = {}
    for n, hv, g4 in zip(MATS, halves, gathered):
        r, cs = shapes[n]
        g4 = lax.dynamic_update_index_in_dim(g4, hv, s_me, 0).reshape(4, r, cs)
        if n in FFN_MATS:
            W[n] = g4
        elif n in ROW_SHARDED:
            W[n] = g4.reshape(4 * r, cs)
        else:
            W[n] = g4.transpose(1, 0, 2).reshape(r, 4 * cs)
    meta_rows = w['meta_tokens'].reshape(-1, LANE)
    mg = _allgather_small(meta_rows, name="gather_meta").reshape(4, 2, N_META, -1)[:, 0]
    meta = mg.transpose(1, 0, 2).reshape(N_META, D)
    S = {n: w[n] for n in SMALLS}

    grad_x, dmeta, G, small_rows = _local_step(x[0], loss_target[0], meta, W, S)
    G['meta_tokens'] = dmeta

    names = MATS + ('meta_tokens',)
    views = []
    for n in names:
        r, cs = shapes[n]
        if n in FFN_MATS:
            g4 = G[n]
        elif n in ROW_SHARDED:
            g4 = G[n].reshape(4, r, cs)
        else:
            g4 = G[n].reshape(r, 4, cs).transpose(1, 0, 2)
        views.append(g4.reshape(4, 2, r // 2, cs))
    c1 = c.astype(jnp.int32).reshape(1)
    from_sibling = _rs_pair_exchange(views, name="rs_pair_exchange")
    parts = [_add_half(v, o, c1, out_dtype=F32 if n == 'meta_tokens' else BF16, name="rs_pair_sum_" + n)
             for n, v, o in zip(names, views, from_sibling)]
    slabs = _rs_chip_exchange(parts, name="rs_chip_exchange")
    reds = [_sum4(s, c1, name="rs_chip_sum_" + n) for n, s in zip(names, slabs)]
    both = _rs_pair_share(reds, name="rs_pair_share")
    grads = {n: b.reshape(w[n].shape) for n, b in zip(names, both)}

    rows, off = {}, 0
    for nme, blk in small_rows:
        rows[nme] = off
        off += blk.shape[0]
    block = jnp.concatenate([blk for _, blk in small_rows], axis=0)
    n_rows = (off + 7) // 8 * 8
    block = jnp.pad(block, ((0, n_rows - off), (0, 0)))
    allsmall = _allgather_small(block, name="gather_small").reshape(8, n_rows, LANE)
    tot, d_lbf, d_lbb = _finish_small(allsmall, w['hg_lb_fwd'], w['hg_lb_bwd'], rows=rows, name="finish_small")
    loss = 0.5 * tot[0, 0] / D

    def small(nme, shape):
        r0 = rows[nme]
        return tot[r0:r0 + shape[-1] // LANE].reshape(shape)

    grads['ffn1_norm'] = small('ffn1_norm', w['ffn1_norm'].shape)
    grads['mix_norm'] = small('mix_norm', w['mix_norm'].shape)
    grads['ffn2_norm'] = small('ffn2_norm', w['ffn2_norm'].shape)
    grads['hg_out_norm'] = small('hg_out_norm', w['hg_out_norm'].shape)
    grads['hg_lb_fwd'] = d_lbf
    grads['hg_lb_bwd'] = d_lbb
    grads['q_norm'] = _ungroup_vec(tot[rows['q_norm']])
    grads['k_norm'] = _ungroup_vec(tot[rows['k_norm']])

    delta, new_m, new_v = {}, {}, {}
    for n in WEIGHTS:
        delta[n], new_m[n], new_v[n] = _adamw(w[n], grads[n], mom[n], var[n], name="adamw_" + n)
    return (loss, grad_x[None], *[grads[n] for n in WEIGHTS], *[delta[n] for n in WEIGHTS],
            *[new_m[n] for n in WEIGHTS], *[new_v[n] for n in WEIGHTS])
```

```python
import numpy as np
import jax
import jax.numpy as jnp
from jax import lax
from jax.experimental import pallas as pl
from jax.experimental.pallas import tpu as pltpu

F32 = jnp.float32
BF16 = jnp.bfloat16
SDS = jax.ShapeDtypeStruct
MESH = pl.DeviceIdType.MESH

EPS = 1e-6
N_META = 16
PAD = 512
LANE = 128
CHUNK = 128
HG_HEADS = 4
HG_W = HG_HEADS * 128
AT_HEADS = 8
AT_KV = 2
AT_HD = 64
AT_W = AT_HEADS * AT_HD
AT_KVW = AT_KV * AT_HD
VT_ROWS = AT_HD + 16
FWD_CHUNKS_PER_STEP = 4
GRID_W = 64
ROPE_THETA = 10000.0
Z_HG = 5 * HG_W
Z_AT = AT_W + 2 * AT_KVW
ADAM_LR, ADAM_B1, ADAM_B2, ADAM_EPS, ADAM_WD, ADAM_STEP = 0.001, 0.9, 0.999, 1e-08, 0.01, 10
VMEM_DEFAULT = 48 * 1024 * 1024
VMEM_LARGE = 60 * 1024 * 1024
NEG = -1e30

MATS = ('ffn1_w_gate', 'ffn1_w_up', 'ffn1_w_down', 'w_in', 'w_up_a', 'w_up_b', 'w_out',
        'ffn2_w_gate', 'ffn2_w_up', 'ffn2_w_down')
ROW_SHARDED = ('ffn1_w_down', 'w_out', 'ffn2_w_down')
FFN_MATS = ('ffn1_w_gate', 'ffn1_w_up', 'ffn1_w_down', 'ffn2_w_gate', 'ffn2_w_up', 'ffn2_w_down')
SMALLS = ('ffn1_norm', 'mix_norm', 'hg_lb_fwd', 'hg_lb_bwd', 'hg_out_norm', 'q_norm', 'k_norm', 'ffn2_norm')
WEIGHTS = ('meta_tokens', 'ffn1_norm', 'ffn1_w_gate', 'ffn1_w_up', 'ffn1_w_down', 'mix_norm', 'w_in', 'hg_lb_fwd',
           'hg_lb_bwd', 'hg_out_norm', 'q_norm', 'k_norm', 'w_up_a', 'w_up_b', 'w_out', 'ffn2_norm', 'ffn2_w_gate',
           'ffn2_w_up', 'ffn2_w_down')


def _params(sem=None, vmem=VMEM_DEFAULT):
    return pltpu.CompilerParams(dimension_semantics=sem, vmem_limit_bytes=vmem)


def _tile(n, pref, q=LANE):
    for d in range(min(pref, n), 0, -1):
        if n % d == 0 and d % q == 0:
            return d
    return n


def _sigmoid(x):
    return 0.5 * jnp.tanh(0.5 * x) + 0.5


def _dot(a, b, dims):
    return lax.dot_general(a, b, (dims, ((), ())), preferred_element_type=F32)


def _nn(a, b):
    return _dot(a, b, ((1,), (0,)))


def _nt(a, b):
    return _dot(a, b, ((1,), (1,)))


def _tn(a, b):
    return _dot(a, b, ((0,), (0,)))


def _split3(x):
    x1 = x.astype(BF16)
    r = x - x1.astype(F32)
    x2 = r.astype(BF16)
    x3 = (r - x2.astype(F32)).astype(BF16)
    return x1, x2, x3


def _exact_left(m01, x):
    x1, x2, x3 = _split3(x)
    return _nn(m01, x1) + _nn(m01, x2) + _nn(m01, x3)


def _exact_right(x, m01):
    x1, x2, x3 = _split3(x)
    return _nn(x1, m01) + _nn(x2, m01) + _nn(x3, m01)


def _mm(pairs, *, name, ta=False, tb=False, out_dtype=F32, tm=512, tn=1024, tk=1024, alpha=1.0, res=None):
    a0, b0 = pairs[0]
    M = a0.shape[1] if ta else a0.shape[0]
    K = a0.shape[0] if ta else a0.shape[1]
    N = b0.shape[0] if tb else b0.shape[1]
    tm, tn, tk = _tile(M, tm), _tile(N, tn), _tile(K, tk)
    nk = K // tk
    npair = len(pairs)
    dims = ((0 if ta else 1,), (1 if tb else 0,))

    def body(*refs):
        ab = refs[:2 * npair]
        pos = 2 * npair
        res_ref = None
        if res is not None:
            res_ref = refs[pos]
            pos += 1
        o_ref = refs[pos]

        def partial_sum():
            tot = None
            for p in range(npair):
                d = _dot(ab[2 * p][...].astype(BF16), ab[2 * p + 1][...].astype(BF16), dims)
                tot = d if tot is None else tot + d
            return tot

        def finish(acc):
            r = acc if alpha == 1.0 else acc * alpha
            if res_ref is not None:
                r = res_ref[...] + r
            o_ref[...] = r.astype(out_dtype)

        if nk == 1:
            finish(partial_sum())
        else:
            acc_ref = refs[pos + 1]
            k = pl.program_id(2)

            @pl.when(k == 0)
            def _():
                acc_ref[...] = jnp.zeros_like(acc_ref)

            acc_ref[...] += partial_sum()

            @pl.when(k == nk - 1)
            def _():
                finish(acc_ref[...])

    a_spec = pl.BlockSpec((tk, tm), lambda j, i, k: (k, i)) if ta else pl.BlockSpec((tm, tk), lambda j, i, k: (i, k))
    b_spec = pl.BlockSpec((tn, tk), lambda j, i, k: (j, k)) if tb else pl.BlockSpec((tk, tn), lambda j, i, k: (k, j))
    o_spec = pl.BlockSpec((tm, tn), lambda j, i, k: (i, j))
    in_specs, args = [], []
    for a, b in pairs:
        in_specs += [a_spec, b_spec]
        args += [a, b]
    if res is not None:
        in_specs.append(o_spec)
        args.append(res)
    return pl.pallas_call(
        body, grid=(N // tn, M // tm, nk), in_specs=in_specs, out_specs=o_spec,
        out_shape=SDS((M, N), out_dtype),
        scratch_shapes=[pltpu.VMEM((tm, tn), F32)] if nk > 1 else [],
        compiler_params=_params(("parallel", "parallel", "arbitrary")), name=name)(*args)


def _rmsnorm_fwd(h, w, *, name):
    L, D = h.shape
    tm = _tile(L, 512)

    def body(h_ref, w_ref, o_ref):
        x = h_ref[...]
        r = lax.rsqrt(jnp.mean(x * x, axis=-1, keepdims=True) + EPS)
        o_ref[...] = (x * r * w_ref[...]).astype(BF16)

    return pl.pallas_call(
        body, grid=(L // tm,),
        in_specs=[pl.BlockSpec((tm, D), lambda i: (i, 0)), pl.BlockSpec((1, D), lambda i: (0, 0))],
        out_specs=pl.BlockSpec((tm, D), lambda i: (i, 0)), out_shape=SDS((L, D), BF16),
        compiler_params=_params(("parallel",)), name=name)(h, w)


def _rmsnorm_bwd(h, w, dn, dres, *, split=False, name):
    L, D = h.shape
    tm = PAD if split else _tile(L, 512)

    def body(h_ref, w_ref, dn_ref, dres_ref, dh_ref, *rest):
        dw_ref = rest[-1]
        i = pl.program_id(0)
        x = h_ref[...]
        r = lax.rsqrt(jnp.mean(x * x, axis=-1, keepdims=True) + EPS)
        xh = x * r
        dn = dn_ref[...]
        dxh = dn * w_ref[...]
        dh = dres_ref[...] + r * (dxh - xh * jnp.mean(dxh * xh, axis=-1, keepdims=True))
        dh_ref[...] = dh

        @pl.when(i == 0)
        def _():
            dw_ref[...] = jnp.zeros_like(dw_ref)
            if split:
                rest[0][...] = dh[PAD - N_META:]

        dw_ref[...] += jnp.sum(dn * xh, axis=0, keepdims=True)

    row = pl.BlockSpec((tm, D), lambda i: (i, 0))
    vec = pl.BlockSpec((1, D), lambda i: (0, 0))
    if split:
        out_specs = [pl.BlockSpec((tm, D), lambda i: (jnp.maximum(i - 1, 0), 0)), pl.BlockSpec((N_META, D), lambda i: (0, 0)), vec]
        out_shape = [SDS((L - PAD, D), F32), SDS((N_META, D), F32), SDS((1, D), F32)]
    else:
        out_specs, out_shape = [row, vec], [SDS((L, D), F32), SDS((1, D), F32)]
    return pl.pallas_call(
        body, grid=(L // tm,), in_specs=[row, vec, row, row], out_specs=out_specs, out_shape=out_shape,
        compiler_params=_params(("arbitrary",)), name=name)(h, w, dn, dres)


def _ffn4_up(n, wg4, wu4, *, name):
    L, D = n.shape
    ns, _, cs = wg4.shape
    tm = _tile(L, 768)

    def body(n_ref, wg_ref, wu_ref, ag_ref, au_ref, a_ref):
        x = n_ref[...]
        g = _nn(x, wg_ref[...])
        u = _nn(x, wu_ref[...])
        sg = _sigmoid(g)
        silu = g * sg
        ag_ref[...] = (u * (sg * (1.0 + g * (1.0 - sg)))).astype(BF16)
        au_ref[...] = silu.astype(BF16)
        a_ref[...] = (silu * u).astype(BF16)

    wspec = pl.BlockSpec((None, D, cs), lambda j, i: (j, 0, 0))
    ospec = pl.BlockSpec((None, tm, cs), lambda j, i: (j, i, 0))
    return pl.pallas_call(
        body, grid=(ns, L // tm),
        in_specs=[pl.BlockSpec((tm, D), lambda j, i: (i, 0)), wspec, wspec], out_specs=[ospec, ospec, ospec],
        out_shape=[SDS((ns, L, cs), BF16), SDS((ns, L, cs), BF16), SDS((ns, L, cs), BF16)],
        compiler_params=_params(("parallel", "parallel")), name=name)(n, wg4, wu4)


def _ffn4_down(a4, wd4, h, *, name):
    ns, L, cs = a4.shape
    D = wd4.shape[2]
    tm = _tile(L, 512)

    def body(a_ref, w_ref, h_ref, o_ref):
        acc = _nn(a_ref[0], w_ref[0])
        for j in range(1, ns):
            acc = acc + _nn(a_ref[j], w_ref[j])
        o_ref[...] = h_ref[...] + 0.5 * acc

    row = pl.BlockSpec((tm, D), lambda i: (i, 0))
    return pl.pallas_call(
        body, grid=(L // tm,),
        in_specs=[pl.BlockSpec((ns, tm, cs), lambda i: (0, i, 0)), pl.BlockSpec((ns, cs, D), lambda i: (0, 0, 0)), row],
        out_specs=row, out_shape=SDS((L, D), F32),
        compiler_params=_params(("parallel",)), name=name)(a4, wd4, h)


def _ffn4_dact(dh, wd4, ag4, au4, *, name):
    L, D = dh.shape
    ns, cs, _ = wd4.shape
    tm = _tile(L, 768)

    def body(dh_ref, wd_ref, ag_ref, au_ref, dg_ref, du_ref):
        da = 0.5 * _nt(dh_ref[...].astype(BF16), wd_ref[...])
        dg_ref[...] = (da * ag_ref[...].astype(F32)).astype(BF16)
        du_ref[...] = (da * au_ref[...].astype(F32)).astype(BF16)

    ospec = pl.BlockSpec((None, tm, cs), lambda j, i: (j, i, 0))
    return pl.pallas_call(
        body, grid=(ns, L // tm),
        in_specs=[pl.BlockSpec((tm, D), lambda j, i: (i, 0)), pl.BlockSpec((None, cs, D), lambda j, i: (j, 0, 0)), ospec, ospec],
        out_specs=[ospec, ospec], out_shape=[SDS((ns, L, cs), BF16), SDS((ns, L, cs), BF16)],
        compiler_params=_params(("parallel", "parallel")), name=name)(dh, wd4, ag4, au4)


def _ffn4_dn(dg4, du4, wg4, wu4, *, name):
    ns, L, cs = dg4.shape
    D = wg4.shape[1]
    tm = _tile(L, 512)

    def body(dg_ref, du_ref, wg_ref, wu_ref, o_ref):
        acc = None
        for j in range(ns):
            t = _nt(dg_ref[j], wg_ref[j]) + _nt(du_ref[j], wu_ref[j])
            acc = t if acc is None else acc + t
        o_ref[...] = acc

    aspec = pl.BlockSpec((ns, tm, cs), lambda i: (0, i, 0))
    wspec = pl.BlockSpec((ns, D, cs), lambda i: (0, 0, 0))
    return pl.pallas_call(
        body, grid=(L // tm,), in_specs=[aspec, aspec, wspec, wspec],
        out_specs=pl.BlockSpec((tm, D), lambda i: (i, 0)), out_shape=SDS((L, D), F32),
        compiler_params=_params(("parallel",), VMEM_LARGE), name=name)(dg4, du4, wg4, wu4)


def _ffn4_dw(x, y4, *, x_is_rows, alpha=1.0, name):
    L, D = x.shape
    ns, _, cs = y4.shape
    tk = _tile(L, 512)
    nk = L // tk
    oshape = (D, cs) if x_is_rows else (cs, D)

    def body(x_ref, y_ref, o_ref):
        k = pl.program_id(0)

        @pl.when(k == 0)
        def _():
            o_ref[...] = jnp.zeros_like(o_ref)

        xb = x_ref[...].astype(BF16)
        if x_is_rows:
            xt = xb.T
            for j in range(ns):
                o_ref[j] += _nn(xt, y_ref[j])
        else:
            for j in range(ns):
                o_ref[j] += _tn(y_ref[j], xb)

        if alpha != 1.0:
            @pl.when(k == nk - 1)
            def _():
                o_ref[...] = o_ref[...] * alpha

    return pl.pallas_call(
        body, grid=(nk,),
        in_specs=[pl.BlockSpec((tk, D), lambda k: (k, 0)), pl.BlockSpec((ns, tk, cs), lambda k: (0, k, 0))],
        out_specs=pl.BlockSpec((ns,) + oshape, lambda k: (0, 0, 0)), out_shape=SDS((ns,) + oshape, F32),
        compiler_params=_params(("arbitrary",)), name=name)(x, y4)


def _hg_mask_table(rev):
    t = np.arange(CHUNK)[:, None]
    s = np.arange(CHUNK)[None, :]
    causal = (s >= t) if rev else (s <= t)
    out = [causal]
    for sh in (6, 5, 4):
        same = (t >> (sh + 1)) == (s >> (sh + 1))
        out.append(same & (((t >> sh) & 1) == (0 if rev else 1)) & (((s >> sh) & 1) == (1 if rev else 0)))
    out.append(((t >> 4) == (s >> 4)) & causal)
    out.append(causal.T)
    return jnp.asarray(np.stack(out).astype(np.float32))


def _hg_masks(mk_ref):
    on = [mk_ref[i] > 0.5 for i in range(5)]
    return on[0], on[1:4], on[4], mk_ref[0].astype(BF16), mk_ref[5].astype(BF16)


def _hg_intra_factors(q, k, b, b_scr, rev):
    b_scr[...] = b
    row = lax.broadcasted_iota(jnp.int32, (CHUNK, LANE), 0)
    out = []
    for sh in (6, 5, 4):
        lb = 1 << sh
        pieces = []
        for p in range(0, CHUNK, 2 * lb):
            r = p + lb if rev else p + lb - 1
            pieces.append(jnp.broadcast_to(b_scr[pl.ds(r, 1), :], (2 * lb, LANE)))
        ref = pieces[0] if len(pieces) == 1 else jnp.concatenate(pieces, axis=0)
        qside = jnp.bitwise_and(jnp.right_shift(row, sh), 1) == (0 if rev else 1)
        d = b - ref
        e = jnp.exp(jnp.minimum(jnp.where(qside, d, -d), 0.0))
        eq = jnp.where(qside, e, 0.0)
        ek = jnp.where(qside, 0.0, e)
        out.append((eq, ek, (q * eq).astype(BF16), (k * ek).astype(BF16)))
    pieces = []
    for a in range(0, CHUNK, 16):
        r = a + (8 if rev else 7)
        pieces.append(jnp.broadcast_to(b_scr[pl.ds(r, 1), :], (16, LANE)))
    ref = jnp.concatenate(pieces, axis=0)
    eq = jnp.exp(jnp.minimum(b - ref, 80.0))
    ek = jnp.exp(jnp.minimum(ref - b, 80.0))
    out.append((eq, ek, (q * eq).astype(BF16), (k * ek).astype(BF16)))
    return out


def _hg_gate(zf, l0, l1, valid):
    mx = jnp.maximum(l0, l1)
    e0, e1 = jnp.exp(l0 - mx), jnp.exp(l1 - mx)
    p0 = e0 / (e0 + e1)
    sg = _sigmoid(-zf)
    k = jnp.where(valid, (1.0 - p0) * sg, 0.0)
    return p0, sg, k, jnp.log(1.0 - k)


def _hg_fwd(z, lbp, *, rev, name):
    L = z.shape[0]
    nc = L // CHUNK
    fcol = 3 if rev else 2

    def cidx(j):
        return nc - 1 - j if rev else j

    def body(zq_ref, zi_ref, zf_ref, lb_ref, mk_ref, o_ref, ssave_ref, st_scr, b_scr):
        j = pl.program_id(0)

        @pl.when(j == 0)
        def _():
            st_scr[...] = jnp.zeros_like(st_scr)

        causal, lmasks, dmask, tri, _ = _hg_masks(mk_ref)
        rowg = cidx(j) * CHUNK + lax.broadcasted_iota(jnp.int32, (CHUNK, LANE), 0)
        valid = rowg >= PAD - N_META
        last = 0 if rev else CHUNK - 1
        for hh in range(HG_HEADS):
            sl = slice(LANE * hh, LANE * (hh + 1))
            zq = zq_ref[:, sl]
            q = zq * _sigmoid(zq)
            v = zi_ref[:, sl].astype(BF16)
            _, _, k, g = _hg_gate(zf_ref[:, sl], lb_ref[0:1, sl], lb_ref[1:2, sl], valid)
            b = _exact_left(tri, g)
            st = st_scr[hh]
            ssave_ref[0, hh] = st
            o = _nt((q * jnp.exp(b)).astype(BF16), st.astype(BF16))
            a = None
            fac = _hg_intra_factors(q, k, b, b_scr, rev)
            for (eq, ek, qq, kk), msk in zip(fac, lmasks + [dmask]):
                t = jnp.where(msk, _nt(qq, kk), 0.0)
                a = t if a is None else a + t
            o_ref[:, sl] = o + _nn(a.astype(BF16), v)
            bl = b_scr[pl.ds(last, 1), :]
            kd = (k * jnp.exp(bl - b)).astype(BF16)
            st_scr[hh] = st * jnp.exp(bl) + _tn(v, kd)

    zspec = lambda col: pl.BlockSpec((CHUNK, HG_W), lambda j: (cidx(j), col))
    return pl.pallas_call(
        body, grid=(nc,),
        in_specs=[zspec(0), zspec(1), zspec(fcol), pl.BlockSpec((2, HG_W), lambda j: (0, 0)),
                  pl.BlockSpec((6, CHUNK, CHUNK), lambda j: (0, 0, 0))],
        out_specs=[pl.BlockSpec((CHUNK, HG_W), lambda j: (cidx(j), 0)),
                   pl.BlockSpec((1, HG_HEADS, LANE, LANE), lambda j: (cidx(j), 0, 0, 0))],
        out_shape=[SDS((L, HG_W), F32), SDS((nc, HG_HEADS, LANE, LANE), F32)],
        scratch_shapes=[pltpu.VMEM((HG_HEADS, LANE, LANE), F32), pltpu.VMEM((CHUNK, LANE), F32)],
        compiler_params=_params(("arbitrary",)), name=name)(z, z, z, lbp, _hg_mask_table(rev))


def _hg_bwd(z, lbp, do, ssave, prev, *, rev, name):
    L = z.shape[0]
    nc = L // CHUNK
    fcol = 3 if rev else 2
    final = prev is not None

    def cidx(j):
        return j if rev else nc - 1 - j

    def body(*refs):
        zq_ref, zi_ref, zf_ref, lb_ref, mk_ref, do_ref, ss_ref = refs[:7]
        pos = 7
        if final:
            dqin_ref, dvin_ref = refs[7:9]
            pos = 9
        dq_ref, dv_ref, dzf_ref, dlb_ref, dst_scr, b_scr = refs[pos:pos + 6]
        j = pl.program_id(0)

        @pl.when(j == 0)
        def _():
            dst_scr[...] = jnp.zeros_like(dst_scr)
            dlb_ref[...] = jnp.zeros_like(dlb_ref)

        causal, lmasks, dmask, tri, tri_t = _hg_masks(mk_ref)
        rowg = cidx(j) * CHUNK + lax.broadcasted_iota(jnp.int32, (CHUNK, LANE), 0)
        valid = rowg >= PAD - N_META
        last = 0 if rev else CHUNK - 1
        for hh in range(HG_HEADS):
            sl = slice(LANE * hh, LANE * (hh + 1))
            zq = zq_ref[:, sl]
            sq = _sigmoid(zq)
            q = zq * sq
            v = zi_ref[:, sl].astype(BF16)
            p0, sg, k, g = _hg_gate(zf_ref[:, sl], lb_ref[0:1, sl], lb_ref[1:2, sl], valid)
            b = _exact_left(tri, g)
            dob = do_ref[:, sl].astype(BF16)
            st = ss_ref[0, hh]
            dst = dst_scr[hh]
            stb, dstb = st.astype(BF16), dst.astype(BF16)
            eb = jnp.exp(b)
            qe = (q * eb).astype(BF16)
            fac = _hg_intra_factors(q, k, b, b_scr, rev)
            bl = b_scr[pl.ds(last, 1), :]
            ebl = jnp.exp(bl)
            kde = jnp.exp(bl - b)
            kd = (k * kde).astype(BF16)
            da = jnp.where(causal, _nt(dob, v), 0.0)
            dq = eb * _nn(dob, stb)
            dk_inter = kde * _nn(v, dstb)
            dk = dk_inter
            dv = _nt(kd, dstb)
            a = None
            db = q * dq - k * dk
            for (eq, ek, qq, kk), msk in zip(fac, lmasks + [dmask]):
                t = jnp.where(msk, _nt(qq, kk), 0.0)
                a = t if a is None else a + t
                dal = jnp.where(msk, da, 0.0).astype(BF16)
                mq = _nn(dal, kk)
                mk = _tn(dal, qq)
                dq = dq + eq * mq
                dk = dk + ek * mk
                db = db + (qq.astype(F32) * mq - kk.astype(F32) * mk)
            dv = dv + _tn(a.astype(BF16), dob)
            extra = ebl * jnp.sum(st * dst, axis=0, keepdims=True) + jnp.sum(k * dk_inter, axis=0, keepdims=True)
            dst_scr[hh] = dst * ebl + _tn(dob, qe)
            dg = _exact_left(tri_t, db) + extra
            dk_tot = dk - dg / (1.0 - k)
            dzf_ref[:, sl] = jnp.where(valid, dk_tot * (1.0 - p0) * (-sg * (1.0 - sg)), 0.0).astype(BF16)
            dlb_ref[:, sl] += jnp.sum(jnp.where(valid, -sg * dk_tot, 0.0), axis=0, keepdims=True)
            if final:
                dq_ref[:, sl] = ((dq + dqin_ref[:, sl]) * (sq * (1.0 + zq * (1.0 - sq)))).astype(BF16)
                dv_ref[:, sl] = (dv + dvin_ref[:, sl]).astype(BF16)
            else:
                dq_ref[:, sl] = dq
                dv_ref[:, sl] = dv

    zspec = lambda col: pl.BlockSpec((CHUNK, HG_W), lambda j: (cidx(j), col))
    rspec = pl.BlockSpec((CHUNK, HG_W), lambda j: (cidx(j), 0))
    in_specs = [zspec(0), zspec(1), zspec(fcol), pl.BlockSpec((2, HG_W), lambda j: (0, 0)),
                pl.BlockSpec((6, CHUNK, CHUNK), lambda j: (0, 0, 0)), rspec,
                pl.BlockSpec((1, HG_HEADS, LANE, LANE), lambda j: (cidx(j), 0, 0, 0))]
    args = [z, z, z, lbp, _hg_mask_table(rev), do, ssave]
    if final:
        in_specs += [rspec, rspec]
        args += list(prev)
    odt = BF16 if final else F32
    return pl.pallas_call(
        body, grid=(nc,), in_specs=in_specs,
        out_specs=[rspec, rspec, rspec, pl.BlockSpec((1, HG_W), lambda j: (0, 0))],
        out_shape=[SDS((L, HG_W), odt), SDS((L, HG_W), odt), SDS((L, HG_W), BF16), SDS((1, HG_W), F32)],
        scratch_shapes=[pltpu.VMEM((HG_HEADS, LANE, LANE), F32), pltpu.VMEM((CHUNK, LANE), F32)],
        compiler_params=_params(("arbitrary",)), name=name)(*args)


def _hg_post_fwd(of, ob, z, w, *, name):
    L = of.shape[0]
    tm = _tile(L, 512)

    def body(of_ref, ob_ref, zg_ref, w_ref, y_ref):
        for hh in range(HG_HEADS):
            sl = slice(LANE * hh, LANE * (hh + 1))
            o = of_ref[:, sl] + ob_ref[:, sl]
            r = lax.rsqrt(jnp.mean(o * o, axis=-1, keepdims=True) + EPS)
            zg = zg_ref[:, sl]
            y_ref[:, sl] = (o * r * w_ref[:, sl] * (zg * _sigmoid(zg))).astype(BF16)

    row = pl.BlockSpec((tm, HG_W), lambda i: (i, 0))
    return pl.pallas_call(
        body, grid=(L // tm,),
        in_specs=[row, row, pl.BlockSpec((tm, HG_W), lambda i: (i, 4)), pl.BlockSpec((1, HG_W), lambda i: (0, 0))],
        out_specs=row, out_shape=SDS((L, HG_W), BF16),
        compiler_params=_params(("parallel",)), name=name)(of, ob, z, w)


def _hg_post_bwd(dy, of, ob, z, w, *, name):
    L = of.shape[0]
    tm = _tile(L, 512)

    def body(dy_ref, of_ref, ob_ref, zg_ref, w_ref, do_ref, dzg_ref, dw_ref):
        @pl.when(pl.program_id(0) == 0)
        def _():
            dw_ref[...] = jnp.zeros_like(dw_ref)

        for hh in range(HG_HEADS):
            sl = slice(LANE * hh, LANE * (hh + 1))
            o = of_ref[:, sl] + ob_ref[:, sl]
            r = lax.rsqrt(jnp.mean(o * o, axis=-1, keepdims=True) + EPS)
            xh = o * r
            zg = zg_ref[:, sl]
            sg = _sigmoid(zg)
            w = w_ref[:, sl]
            dy = dy_ref[:, sl]
            dys = dy * (zg * sg)
            dzg_ref[:, sl] = (dy * xh * w * (sg * (1.0 + zg * (1.0 - sg)))).astype(BF16)
            dw_ref[:, sl] += jnp.sum(dys * xh, axis=0, keepdims=True)
            dxh = dys * w
            do_ref[:, sl] = r * (dxh - xh * jnp.mean(dxh * xh, axis=-1, keepdims=True))

    row = pl.BlockSpec((tm, HG_W), lambda i: (i, 0))
    vec = pl.BlockSpec((1, HG_W), lambda i: (0, 0))
    return pl.pallas_call(
        body, grid=(L // tm,),
        in_specs=[row, row, row, pl.BlockSpec((tm, HG_W), lambda i: (i, 4)), vec],
        out_specs=[row, row, vec],
        out_shape=[SDS((L, HG_W), F32), SDS((L, HG_W), BF16), SDS((1, HG_W), F32)],
        compiler_params=_params(("arbitrary",)), name=name)(dy, of, ob, z, w)


N_GROUPS = (AT_HEADS + AT_KV) // 2


def _qk_to_group(wqk):
    d = wqk.shape[0]
    return wqk.reshape(d, N_GROUPS, 2, AT_HD // 2, 2).transpose(0, 1, 4, 2, 3).reshape(d, N_GROUPS * LANE)


def _qk_from_group(wqk):
    d = wqk.shape[0]
    return wqk.reshape(d, N_GROUPS, 2, 2, AT_HD // 2).transpose(0, 1, 3, 4, 2).reshape(d, N_GROUPS * LANE)


def _group_vec(w64):
    halves = w64.reshape(AT_HD // 2, 2).T
    return jnp.broadcast_to(halves[:, None, :], (2, 2, AT_HD // 2)).reshape(1, LANE)


def _ungroup_vec(w128):
    w = w128.reshape(2, 2, 32).sum(axis=1)
    return w.T.reshape(1, AT_HD)


def _rope_tables(L):
    n_real = L - PAD
    t = np.arange(n_real)
    row = np.concatenate([np.zeros(PAD), t // GRID_W]).astype(np.float32)
    col = np.concatenate([np.zeros(PAD), t % GRID_W]).astype(np.float32)
    inv = jnp.asarray(ROPE_THETA, F32) ** (-jnp.arange(0, AT_HD // 2, 2, dtype=F32) / (AT_HD // 2))
    ang = jnp.concatenate([jnp.asarray(row)[:, None] * inv, jnp.asarray(col)[:, None] * inv], axis=-1)
    cos, sin = jnp.cos(ang), jnp.sin(ang)
    cc = jnp.tile(cos, (1, 4))
    ss = jnp.concatenate([-sin, -sin, sin, sin], axis=1)
    return cc, ss


def _seg_matrix():
    a = lax.broadcasted_iota(jnp.int32, (LANE, LANE), 0)
    b = lax.broadcasted_iota(jnp.int32, (LANE, LANE), 1)
    same = jnp.bitwise_and(jnp.right_shift(a, 5), 1) == jnp.bitwise_and(jnp.right_shift(b, 5), 1)
    return jnp.where(same, 1.0, 0.0).astype(BF16)


def _slot_mask(shape, hp):
    lane = lax.broadcasted_iota(jnp.int32, shape, 1)
    return jnp.bitwise_and(jnp.right_shift(lane, 5), 1) == hp


def _at_prep(z, cc, ss, wq, wk, *, name):
    L = z.shape[0]
    tm = PAD
    qcol = Z_HG // AT_W
    kvcol = (Z_HG + AT_W) // (2 * LANE)

    def body(zq_ref, zkv_ref, cc_ref, ss_ref, wq_ref, wk_ref, qm_ref, qt_ref, kr_ref, krt_ref, vb_ref, vt_ref):
        seg = _seg_matrix()
        cc, ss = cc_ref[...], ss_ref[...]

        def normrope(x, w):
            r = lax.rsqrt(_exact_right(x * x, seg) * (1.0 / AT_HD) + EPS)
            y = x * r * w
            return y * cc + pltpu.roll(y, 64, 1) * ss

        for g in range(AT_HEADS // 2):
            o = normrope(zq_ref[:, LANE * g:LANE * (g + 1)], wq_ref[...]) * (AT_HD ** -0.5)
            for hp in range(2):
                h = 2 * g + hp
                tgt = h // (AT_HEADS // AT_KV)
                xm = jnp.where(_slot_mask(o.shape, hp), o, 0.0)
                if tgt != hp:
                    xm = pltpu.roll(xm, 32 if tgt == 1 else 96, 1)
                qm_ref[h] = xm.astype(BF16)
                qt_ref[h] = xm.T.astype(BF16)
        kr = normrope(zkv_ref[:, :LANE], wk_ref[...])
        kr_ref[...] = kr.astype(BF16)
        krt_ref[0] = kr.T.astype(BF16)
        v = zkv_ref[:, LANE:]
        low = lax.broadcasted_iota(jnp.int32, v.shape, 1) < AT_HD
        vb_ref[0] = jnp.where(low, v, 0.0).astype(BF16)
        vb_ref[1] = jnp.where(low, pltpu.roll(v, AT_HD, 1), 0.0).astype(BF16)
        vt = v.T.astype(BF16)
        ones = jnp.ones((VT_ROWS - AT_HD, tm), BF16)
        for j in range(AT_KV):
            vt_ref[j, 0, 0:AT_HD, :] = vt[AT_HD * j:AT_HD * (j + 1)]
            vt_ref[j, 0, AT_HD:VT_ROWS, :] = ones

    tab = pl.BlockSpec((tm, LANE), lambda i: (i, 0))
    vec = pl.BlockSpec((1, LANE), lambda i: (0, 0))
    nt = L // tm
    return pl.pallas_call(
        body, grid=(nt,),
        in_specs=[pl.BlockSpec((tm, AT_W), lambda i: (i, qcol)), pl.BlockSpec((tm, 2 * LANE), lambda i: (i, kvcol)),
                  tab, tab, vec, vec],
        out_specs=[pl.BlockSpec((AT_HEADS, tm, LANE), lambda i: (0, i, 0)),
                   pl.BlockSpec((AT_HEADS, LANE, tm), lambda i: (0, 0, i)), tab,
                   pl.BlockSpec((1, LANE, tm), lambda i: (i, 0, 0)),
                   pl.BlockSpec((AT_KV, tm, LANE), lambda i: (0, i, 0)),
                   pl.BlockSpec((AT_KV, 1, VT_ROWS, tm), lambda i: (0, i, 0, 0))],
        out_shape=[SDS((AT_HEADS, L, LANE), BF16), SDS((AT_HEADS, LANE, L), BF16), SDS((L, LANE), BF16),
                   SDS((nt, LANE, tm), BF16), SDS((AT_KV, L, LANE), BF16), SDS((AT_KV, nt, VT_ROWS, tm), BF16)],
        compiler_params=_params(("parallel",)), name=name)(z, z, cc, ss, wq, wk)


def _at_prep_bwd(dqm, dk2, dv2, z, cc, ss, wq, wk, *, name):
    L = z.shape[0]
    tm = PAD
    qcol = Z_HG // AT_W
    kvcol = (Z_HG + AT_W) // (2 * LANE)

    def body(dqm_ref, dk2_ref, dv2_ref, zq_ref, zkv_ref, cc_ref, ss_ref, wq_ref, wk_ref, dz_ref, dwq_ref, dwk_ref):
        @pl.when(pl.program_id(0) == 0)
        def _():
            dwq_ref[...] = jnp.zeros_like(dwq_ref)
            dwk_ref[...] = jnp.zeros_like(dwk_ref)

        seg = _seg_matrix()
        cc, ss = cc_ref[...], ss_ref[...]

        def back(x, w, do):
            dy = do * cc + pltpu.roll(do * ss, 64, 1)
            r = lax.rsqrt(_exact_right(x * x, seg) * (1.0 / AT_HD) + EPS)
            xh = x * r
            dxh = dy * w
            dx = r * (dxh - xh * (_exact_right(dxh * xh, seg) * (1.0 / AT_HD)))
            return dx, jnp.sum(dy * xh, axis=0, keepdims=True)

        for g in range(AT_HEADS // 2):
            do = None
            for hp in range(2):
                h = 2 * g + hp
                tgt = h // (AT_HEADS // AT_KV)
                d = jnp.where(_slot_mask((tm, LANE), tgt), dqm_ref[h], 0.0)
                if tgt != hp:
                    d = pltpu.roll(d, 96 if tgt == 1 else 32, 1)
                do = d if do is None else do + d
            dx, dw = back(zq_ref[:, LANE * g:LANE * (g + 1)], wq_ref[...], do * (AT_HD ** -0.5))
            dz_ref[:, LANE * g:LANE * (g + 1)] = dx.astype(BF16)
            dwq_ref[...] += dw
        dx, dw = back(zkv_ref[:, :LANE], wk_ref[...], dk2_ref[0] + dk2_ref[1])
        dz_ref[:, AT_W:AT_W + LANE] = dx.astype(BF16)
        dwk_ref[...] += dw
        dv0 = dv2_ref[0]
        low = lax.broadcasted_iota(jnp.int32, dv0.shape, 1) < AT_HD
        dz_ref[:, AT_W + LANE:] = jnp.where(low, dv0, pltpu.roll(dv2_ref[1], AT_HD, 1)).astype(BF16)

    tab = pl.BlockSpec((tm, LANE), lambda i: (i, 0))
    vec = pl.BlockSpec((1, LANE), lambda i: (0, 0))
    two = pl.BlockSpec((AT_KV, tm, LANE), lambda i: (0, i, 0))
    return pl.pallas_call(
        body, grid=(L // tm,),
        in_specs=[pl.BlockSpec((AT_HEADS, tm, LANE), lambda i: (0, i, 0)), two, two,
                  pl.BlockSpec((tm, AT_W), lambda i: (i, qcol)), pl.BlockSpec((tm, 2 * LANE), lambda i: (i, kvcol)),
                  tab, tab, vec, vec],
        out_specs=[pl.BlockSpec((tm, Z_AT), lambda i: (i, 0)), vec, vec],
        out_shape=[SDS((L, Z_AT), BF16), SDS((1, LANE), F32), SDS((1, LANE), F32)],
        compiler_params=_params(("arbitrary",)), name=name)(dqm, dk2, dv2, z, z, cc, ss, wq, wk)


def _at_fwd(qt, kr, vt, *, name):
    L = kr.shape[0]
    G = AT_HEADS // AT_KV
    tq = _tile(L, 384)
    tk = PAD
    nk = L // tk
    R = G * tq
    per = FWD_CHUNKS_PER_STEP if (nk - 1) % FWD_CHUNKS_PER_STEP == 0 else 1

    def body(q_ref, k_ref, v_ref, ob_ref, of_ref, lse_ref, m_scr, acc_scr):
        i = pl.program_id(1)
        qt = jnp.concatenate([q_ref[g] for g in range(G)], axis=1)
        m_scr[...] = jnp.full_like(m_scr, NEG)
        acc_scr[...] = jnp.zeros_like(acc_scr)

        def chunks(c, n, masked):
            start = c * tk if isinstance(c, int) else pl.multiple_of(c * tk, tk)
            st = _nn(k_ref[pl.ds(start, n * tk), :], qt).astype(BF16)
            if masked:
                key = lax.broadcasted_iota(jnp.int32, st.shape, 0)
                st = jnp.where(key >= PAD - N_META, st, NEG)
            m_prev = m_scr[...]
            m_new = jnp.maximum(m_prev, jnp.max(st, axis=0, keepdims=True).astype(F32))
            pt = jnp.exp(st - m_new.astype(BF16))
            acc = jnp.exp(m_prev - m_new) * acc_scr[...]
            for u in range(n):
                acc = acc + _nn(v_ref[0, c + u], pt[u * tk:(u + 1) * tk])
            acc_scr[...] = acc
            m_scr[...] = m_new

        chunks(0, 1, True)

        def loop(t, carry):
            chunks(1 + per * t, per, False)
            return carry

        lax.fori_loop(0, (nk - 1) // per, loop, 0)
        l = acc_scr[pl.ds(AT_HD, 1), :]
        lse = m_scr[...] + jnp.log(l)
        on = acc_scr[0:AT_HD, :] / l
        o = jnp.concatenate([on[:, g * tq:(g + 1) * tq] for g in range(G)], axis=0).T
        rowg = i * tq + lax.broadcasted_iota(jnp.int32, o.shape, 0)
        o = jnp.where(rowg >= PAD - N_META, o, 0.0)
        ob_ref[...] = o.astype(BF16)
        of_ref[...] = o
        for g in range(G):
            lse_ref[g] = lse[:, g * tq:(g + 1) * tq]

    ospec = pl.BlockSpec((tq, G * AT_HD), lambda j, i: (i, j))
    return pl.pallas_call(
        body, grid=(AT_KV, L // tq),
        in_specs=[pl.BlockSpec((G, LANE, tq), lambda j, i: (j, 0, i)), pl.BlockSpec((L, LANE), lambda j, i: (0, 0)),
                  pl.BlockSpec((1, nk, VT_ROWS, tk), lambda j, i: (j, 0, 0, 0))],
        out_specs=[ospec, ospec, pl.BlockSpec((G, 1, tq), lambda j, i: (j, 0, i))],
        out_shape=[SDS((L, AT_W), BF16), SDS((L, AT_W), F32), SDS((AT_HEADS, 1, L), F32)],
        scratch_shapes=[pltpu.VMEM((1, R), F32), pltpu.VMEM((VT_ROWS, R), F32)],
        compiler_params=_params(("parallel", "parallel")), name=name)(qt, kr, vt)


def _at_bwd(qm, qt, kr, krt, vb, do, of, lse, *, name):
    L = kr.shape[0]
    G = AT_HEADS // AT_KV
    tq = _tile(L, 384)
    tk = PAD
    nk = L // tk
    nq = L // tq
    R = G * tq

    def body(qm_ref, q_ref, k_hbm, kt_hbm, v_hbm, do_ref, o_ref, lse_ref, dq_ref, dk_hbm, dv_hbm,
             k_scr, kt_scr, v_scr, dk_scr, dv_scr, dq_scr, sem):
        j, i = pl.program_id(0), pl.program_id(1)

        @pl.when(i == 0)
        def _():
            cps = [pltpu.make_async_copy(k_hbm, k_scr, sem.at[0]), pltpu.make_async_copy(kt_hbm, kt_scr, sem.at[1]),
                   pltpu.make_async_copy(v_hbm.at[j], v_scr, sem.at[2])]
            for cp in cps:
                cp.start()
            dk_scr[...] = jnp.zeros_like(dk_scr)
            dv_scr[...] = jnp.zeros_like(dv_scr)
            for cp in cps:
                cp.wait()

        qt = jnp.concatenate([q_ref[g] for g in range(G)], axis=1)
        rowg = i * tq + lax.broadcasted_iota(jnp.int32, (tq, G * AT_HD), 0)
        dot_all = jnp.where(rowg >= PAD - N_META, do_ref[...], 0.0).T
        ot_all = o_ref[...].T
        dot = jnp.concatenate([dot_all[AT_HD * g:AT_HD * (g + 1)] for g in range(G)], axis=1)
        ot = jnp.concatenate([ot_all[AT_HD * g:AT_HD * (g + 1)] for g in range(G)], axis=1)
        delta = jnp.sum(dot * ot, axis=0, keepdims=True)
        dot128 = jnp.concatenate([dot, jnp.zeros_like(dot)], axis=0)
        dor = dot128.T.astype(BF16)
        dot128 = dot128.astype(BF16)
        qr = qm_ref[...].reshape(R, LANE)
        lse_v = jnp.concatenate([lse_ref[g] for g in range(G)], axis=1)
        dq_scr[...] = jnp.zeros_like(dq_scr)

        def chunk(c, masked):
            start = c * tk if isinstance(c, int) else pl.multiple_of(c * tk, tk)
            k = k_scr[pl.ds(start, tk), :]
            kt = kt_scr[c]
            v = v_scr[pl.ds(start, tk), :]
            st = _nn(k, qt)
            if masked:
                key = lax.broadcasted_iota(jnp.int32, st.shape, 0)
                st = jnp.where(key >= PAD - N_META, st, NEG)
            pt = jnp.exp(st - lse_v)
            dst = (pt * (_nn(v, dot128) - delta)).astype(BF16)
            dq_scr[...] += _nn(kt, dst)
            dk_scr[pl.ds(start, tk), :] += _nn(dst, qr)
            dv_scr[pl.ds(start, tk), :] += _nn(pt.astype(BF16), dor)

        chunk(0, True)

        def loop(c, carry):
            chunk(c, False)
            return carry

        lax.fori_loop(1, nk, loop, 0)
        dq_ref[...] = dq_scr[...].T.reshape(G, tq, LANE)

        @pl.when(i == nq - 1)
        def _():
            ck = pltpu.make_async_copy(dk_scr, dk_hbm.at[j], sem.at[0])
            cv = pltpu.make_async_copy(dv_scr, dv_hbm.at[j], sem.at[1])
            ck.start()
            cv.start()
            ck.wait()
            cv.wait()

    anyspec = pl.BlockSpec(memory_space=pl.ANY)
    ospec = pl.BlockSpec((tq, G * AT_HD), lambda j, i: (i, j))
    return pl.pallas_call(
        body, grid=(AT_KV, nq),
        in_specs=[pl.BlockSpec((G, tq, LANE), lambda j, i: (j, i, 0)), pl.BlockSpec((G, LANE, tq), lambda j, i: (j, 0, i)),
                  anyspec, anyspec, anyspec, ospec, ospec, pl.BlockSpec((G, 1, tq), lambda j, i: (j, 0, i))],
        out_specs=[pl.BlockSpec((G, tq, LANE), lambda j, i: (j, i, 0)), anyspec, anyspec],
        out_shape=[SDS((AT_HEADS, L, LANE), F32), SDS((AT_KV, L, LANE), F32), SDS((AT_KV, L, LANE), F32)],
        scratch_shapes=[pltpu.VMEM((L, LANE), BF16), pltpu.VMEM((nk, LANE, tk), BF16), pltpu.VMEM((L, LANE), BF16),
                        pltpu.VMEM((L, LANE), F32), pltpu.VMEM((L, LANE), F32), pltpu.VMEM((LANE, R), F32),
                        pltpu.SemaphoreType.DMA((3,))],
        compiler_params=_params(("arbitrary", "arbitrary"), VMEM_LARGE), name=name)(qm, qt, kr, krt, vb, do, of, lse)


def _merge_fwd(ya, o8, wua, wubp, z, *, name):
    L = ya.shape[0]
    D = wua.shape[1]
    tm, tn = _tile(L, 1536), 256
    ga, gb = (Z_HG + Z_AT) // tn, (Z_HG + Z_AT + D) // tn

    def body(ya_ref, o8_ref, wa_ref, wb_ref, za_ref, zb_ref, mix_ref, sa_ref, sb_ref, ta_ref, tb_ref):
        pa = _nn(ya_ref[...], wa_ref[...])
        pb = _nn(o8_ref[...], wb_ref[...])
        sa, sb = _sigmoid(za_ref[...]), _sigmoid(zb_ref[...])
        mix_ref[...] = (sa * pa + sb * pb).astype(BF16)
        sa_ref[...] = sa.astype(BF16)
        sb_ref[...] = sb.astype(BF16)
        ta_ref[...] = (pa * sa * (1.0 - sa)).astype(BF16)
        tb_ref[...] = (pb * sb * (1.0 - sb)).astype(BF16)

    ospec = pl.BlockSpec((tm, tn), lambda j, i: (i, j))
    return pl.pallas_call(
        body, grid=(D // tn, L // tm),
        in_specs=[pl.BlockSpec((tm, ya.shape[1]), lambda j, i: (i, 0)), pl.BlockSpec((tm, o8.shape[1]), lambda j, i: (i, 0)),
                  pl.BlockSpec((wua.shape[0], tn), lambda j, i: (0, j)), pl.BlockSpec((wubp.shape[0], tn), lambda j, i: (0, j)),
                  pl.BlockSpec((tm, tn), lambda j, i: (i, ga + j)), pl.BlockSpec((tm, tn), lambda j, i: (i, gb + j))],
        out_specs=[ospec] * 5, out_shape=[SDS((L, D), BF16)] * 5,
        compiler_params=_params(("parallel", "parallel")), name=name)(ya, o8, wua, wubp, z, z)


def _merge_bwd(dh, wout, factors, *, name):
    L, D = dh.shape
    tm, tn = _tile(L, 1536), 256

    def body(dh_ref, wo_ref, sa_ref, sb_ref, ta_ref, tb_ref, dpa_ref, dpb_ref, dza_ref, dzb_ref):
        dm = _nt(dh_ref[...].astype(BF16), wo_ref[...])
        for f_ref, o_ref in ((sa_ref, dpa_ref), (sb_ref, dpb_ref), (ta_ref, dza_ref), (tb_ref, dzb_ref)):
            o_ref[...] = (dm * f_ref[...].astype(F32)).astype(BF16)

    ospec = pl.BlockSpec((tm, tn), lambda j, i: (i, j))
    return pl.pallas_call(
        body, grid=(D // tn, L // tm),
        in_specs=[pl.BlockSpec((tm, D), lambda j, i: (i, 0)), pl.BlockSpec((tn, D), lambda j, i: (j, 0))] + [ospec] * 4,
        out_specs=[ospec] * 4, out_shape=[SDS((L, D), BF16)] * 4,
        compiler_params=_params(("parallel", "parallel")), name=name)(dh, wout, *factors)


def _loss_head(h, tgt, *, name):
    L, D = h.shape
    tm = PAD

    def body(h_ref, t_ref, dh_ref, ls_ref):
        i = pl.program_id(0)

        @pl.when(i == 0)
        def _():
            ls_ref[...] = jnp.zeros_like(ls_ref)
            dh_ref[...] = jnp.zeros_like(dh_ref)

        @pl.when(i > 0)
        def _():
            e = h_ref[...] - t_ref[...]
            dh_ref[...] = e * (1.0 / D)
            s = jnp.sum(e * e, axis=0, keepdims=True)
            tot = s[:, :LANE]
            for c in range(1, D // LANE):
                tot = tot + s[:, LANE * c:LANE * (c + 1)]
            ls_ref[...] += tot

    return pl.pallas_call(
        body, grid=(L // tm,),
        in_specs=[pl.BlockSpec((tm, D), lambda i: (i, 0)), pl.BlockSpec((tm, D), lambda i: (jnp.maximum(i - 1, 0), 0))],
        out_specs=[pl.BlockSpec((tm, D), lambda i: (i, 0)), pl.BlockSpec((1, LANE), lambda i: (0, 0))],
        out_shape=[SDS((L, D), F32), SDS((1, LANE), F32)],
        compiler_params=_params(("arbitrary",)), name=name)(h, tgt)


def _adamw(w, g, m, v, *, name):
    shape = w.shape
    w2, g2, m2, v2 = [a.reshape(-1, shape[-1]) for a in (w, g, m, v)]
    rows, cols = w2.shape
    tr = _tile(rows, 256, 8)

    def body(w_ref, g_ref, m_ref, v_ref, d_ref, nm_ref, nv_ref):
        g = g_ref[...]
        m = ADAM_B1 * m_ref[...] + (1.0 - ADAM_B1) * g
        v = ADAM_B2 * v_ref[...] + (1.0 - ADAM_B2) * (g * g)
        m_hat = m / (1.0 - ADAM_B1 ** ADAM_STEP)
        v_hat = v / (1.0 - ADAM_B2 ** ADAM_STEP)
        d_ref[...] = -ADAM_LR * (m_hat / (jnp.sqrt(v_hat) + ADAM_EPS) + ADAM_WD * w_ref[...])
        nm_ref[...] = m
        nv_ref[...] = v

    spec = pl.BlockSpec((tr, cols), lambda i: (i, 0))
    outs = pl.pallas_call(
        body, grid=(rows // tr,), in_specs=[spec] * 4, out_specs=[spec] * 3, out_shape=[SDS((rows, cols), F32)] * 3,
        compiler_params=_params(("parallel",)), name=name)(w2, g2, m2, v2)
    return [o.reshape(shape) for o in outs]


def _place():
    return lax.axis_index("x"), lax.axis_index("y"), lax.axis_index("c")


def _allgather_small(v, *, name):
    m_per, n = v.shape

    def body(x_ref, out_ref, send_sems, recv_sems, local_sem):
        x, y, c = _place()
        me, sibling = (x, y, c), (x, y, 1 - c)
        chips = [(1 - x, y), (x, 1 - y), (1 - x, 1 - y)]

        def rows(px, py, pc):
            return out_ref.at[pl.ds((4 * px + 2 * py + pc) * m_per, m_per), :]

        def copy(k, block, to, src=None):
            return pltpu.make_async_remote_copy(
                src_ref=rows(*block) if src is None else src, dst_ref=rows(*block),
                send_sem=send_sems.at[k], recv_sem=recv_sems.at[k], device_id=to, device_id_type=MESH)

        mine = pltpu.make_async_copy(x_ref, rows(*me), local_sem)
        mine.start()
        first = [copy(0, me, sibling, src=x_ref)]
        first += [copy(1 + j, me, (*chip, c), src=x_ref) for j, chip in enumerate(chips)]
        for cp in first:
            cp.start()
        passed = [copy(4 + j, (*chip, c), sibling) for j, chip in enumerate(chips)]
        for j, chip in enumerate(chips):
            copy(1 + j, (*chip, c), me).wait_recv()
            passed[j].start()
        copy(0, sibling, me).wait_recv()
        for j, chip in enumerate(chips):
            copy(4 + j, (*chip, 1 - c), me).wait_recv()
        for cp in first + passed:
            cp.wait_send()
        mine.wait()

    return pl.pallas_call(
        body, out_shape=SDS((8 * m_per, n), v.dtype),
        in_specs=[pl.BlockSpec(memory_space=pltpu.VMEM)], out_specs=pl.BlockSpec(memory_space=pltpu.VMEM),
        scratch_shapes=[pltpu.SemaphoreType.DMA((7,)), pltpu.SemaphoreType.DMA((7,)), pltpu.SemaphoreType.DMA],
        name=name)(v)


def _chips(x, y):
    return [(1 - x, y), (x, 1 - y), (1 - x, 1 - y)]


def _gather_mats(shards, *, name):
    n = len(shards)

    def body(*refs):
        ins, outs = refs[:n], refs[n:2 * n]
        send_sems, recv_sems, fsend_sems, frecv_sems = refs[2 * n:]
        x, y, c = _place()
        s_me, sibling, chips = 2 * x + y, (x, y, 1 - c), _chips(x, y)

        def copy(src, dst, ssem, rsem, to):
            return pltpu.make_async_remote_copy(src_ref=src, dst_ref=dst, send_sem=ssem, recv_sem=rsem,
                                                device_id=to, device_id_type=MESH)

        first = [copy(ins[t].at[c], outs[t].at[s_me, c], send_sems.at[3 * t + k], recv_sems.at[3 * t + k], (*chip, c))
                 for t in range(n) for k, chip in enumerate(chips)]
        for cp in first:
            cp.start()
        passed = []
        for t in range(n):
            for k, chip in enumerate(chips):
                slot = outs[t].at[2 * chip[0] + chip[1], c]
                copy(ins[t].at[c], slot, send_sems.at[3 * t + k], recv_sems.at[3 * t + k], (*chip, c)).wait_recv()
                fw = copy(slot, slot, fsend_sems.at[3 * t + k], frecv_sems.at[3 * t + k], sibling)
                fw.start()
                passed.append(fw)
        for t in range(n):
            for k, chip in enumerate(chips):
                slot = outs[t].at[2 * chip[0] + chip[1], 1 - c]
                copy(slot, slot, fsend_sems.at[3 * t + k], frecv_sems.at[3 * t + k], sibling).wait_recv()
        for cp in first + passed:
            cp.wait_send()

    anyspec = pl.BlockSpec(memory_space=pl.ANY)
    return pl.pallas_call(
        body, out_shape=[SDS((4,) + s.shape, s.dtype) for s in shards], in_specs=[anyspec] * n, out_specs=[anyspec] * n,
        scratch_shapes=[pltpu.SemaphoreType.DMA((3 * n,))] * 4, name=name)(*shards)


def _rs_pair_exchange(gs, *, name):
    n = len(gs)

    def body(*refs):
        ins, outs = refs[:n], refs[n:2 * n]
        send_sems, recv_sems = refs[2 * n:]
        x, y, c = _place()
        cps = [pltpu.make_async_remote_copy(src_ref=ins[t].at[k, 1 - c], dst_ref=outs[t].at[k],
                                            send_sem=send_sems.at[4 * t + k], recv_sem=recv_sems.at[4 * t + k],
                                            device_id=(x, y, 1 - c), device_id_type=MESH)
               for t in range(n) for k in range(4)]
        for cp in cps:
            cp.start()
        for cp in cps:
            cp.wait()

    anyspec = pl.BlockSpec(memory_space=pl.ANY)
    return pl.pallas_call(
        body, out_shape=[SDS((4,) + g.shape[2:], g.dtype) for g in gs], in_specs=[anyspec] * n, out_specs=[anyspec] * n,
        scratch_shapes=[pltpu.SemaphoreType.DMA((4 * n,))] * 2, name=name)(*gs)


def _rs_chip_exchange(parts, *, name):
    n = len(parts)

    def body(*refs):
        ins, outs = refs[:n], refs[n:2 * n]
        send_sems, recv_sems, local_sems = refs[2 * n:]
        x, y, c = _place()
        s_me, chips = 2 * x + y, _chips(x, y)

        def copy(t, k, chip, src_slot, dst_slot):
            return pltpu.make_async_remote_copy(
                src_ref=ins[t].at[src_slot], dst_ref=outs[t].at[dst_slot], send_sem=send_sems.at[3 * t + k],
                recv_sem=recv_sems.at[3 * t + k], device_id=(*chip, c), device_id_type=MESH)

        mine = [pltpu.make_async_copy(ins[t].at[s_me], outs[t].at[s_me], local_sems.at[t]) for t in range(n)]
        for cp in mine:
            cp.start()
        sends = [copy(t, k, chip, 2 * chip[0] + chip[1], s_me) for t in range(n) for k, chip in enumerate(chips)]
        for cp in sends:
            cp.start()
        for t in range(n):
            for k, chip in enumerate(chips):
                copy(t, k, chip, s_me, 2 * chip[0] + chip[1]).wait_recv()
        for cp in sends:
            cp.wait_send()
        for cp in mine:
            cp.wait()

    anyspec = pl.BlockSpec(memory_space=pl.ANY)
    return pl.pallas_call(
        body, out_shape=[SDS(p.shape, p.dtype) for p in parts], in_specs=[anyspec] * n, out_specs=[anyspec] * n,
        scratch_shapes=[pltpu.SemaphoreType.DMA((3 * n,))] * 2 + [pltpu.SemaphoreType.DMA((n,))], name=name)(*parts)


def _rs_pair_share(fulls, *, name):
    n = len(fulls)

    def body(*refs):
        ins, outs = refs[:n], refs[n:2 * n]
        send_sems, recv_sems = refs[2 * n:]
        x, y, c = _place()

        def copy(t, half):
            return pltpu.make_async_remote_copy(src_ref=ins[t].at[c], dst_ref=outs[t].at[half], send_sem=send_sems.at[t],
                                                recv_sem=recv_sems.at[t], device_id=(x, y, 1 - c), device_id_type=MESH)

        sends = [copy(t, c) for t in range(n)]
        for cp in sends:
            cp.start()
        for t in range(n):
            copy(t, 1 - c).wait_recv()
        for cp in sends:
            cp.wait_send()

    anyspec = pl.BlockSpec(memory_space=pl.ANY)
    return pl.pallas_call(
        body, out_shape=[SDS(f.shape, f.dtype) for f in fulls], in_specs=[anyspec] * n, out_specs=[anyspec] * n,
        input_output_aliases={t: t for t in range(n)},
        scratch_shapes=[pltpu.SemaphoreType.DMA((n,))] * 2, name=name)(*fulls)


def _add_half(g, other, c1, *, out_dtype, name):
    _, _, h, cs = g.shape
    tr = _tile(h, 512, 16)

    def body(c_ref, g_ref, o_ref, out_ref):
        out_ref[...] = (g_ref[...] + o_ref[...]).astype(out_dtype)

    spec = pl.BlockSpec((None, tr, cs), lambda k, i, c: (k, i, 0))
    return pl.pallas_call(
        body, out_shape=SDS(other.shape, out_dtype),
        grid_spec=pltpu.PrefetchScalarGridSpec(
            num_scalar_prefetch=1, grid=(4, h // tr),
            in_specs=[pl.BlockSpec((None, None, tr, cs), lambda k, i, c: (k, c[0], i, 0)), spec], out_specs=spec),
        compiler_params=_params(("parallel", "parallel")), name=name)(c1, g, other)


def _sum4(x, c1, *, name):
    n, h, cs = x.shape
    tr = _tile(h, 512, 16)

    def body(c_ref, x_ref, o_ref):
        tot = x_ref[0].astype(F32)
        for s in range(1, n):
            tot = tot + x_ref[s].astype(F32)
        o_ref[...] = tot

    return pl.pallas_call(
        body, out_shape=SDS((2, h, cs), F32),
        grid_spec=pltpu.PrefetchScalarGridSpec(
            num_scalar_prefetch=1, grid=(h // tr,),
            in_specs=[pl.BlockSpec((n, tr, cs), lambda i, c: (0, i, 0))],
            out_specs=pl.BlockSpec((None, tr, cs), lambda i, c: (c[0], i, 0))),
        compiler_params=_params(("parallel",)), name=name)(c1, x)


def _finish_small(gathered, lbf, lbb, *, rows, name):
    r_lbf, r_lbb = rows['lb_f'], rows['lb_b']

    def body(g_ref, lbf_ref, lbb_ref, o_ref, dlf_ref, dlb_ref):
        tot = g_ref[0]
        for s in range(1, 8):
            tot = tot + g_ref[s]
        o_ref[...] = tot
        o_ref[0:1, :] = jnp.broadcast_to(jnp.sum(o_ref[0:1, :], axis=1, keepdims=True), (1, LANE))
        for lb_ref, d_ref, r0 in ((lbf_ref, dlf_ref, r_lbf), (lbb_ref, dlb_ref, r_lbb)):
            for hh in range(HG_HEADS):
                sl = slice(LANE * hh, LANE * (hh + 1))
                l0, l1 = lb_ref[0:1, sl], lb_ref[1:2, sl]
                mx = jnp.maximum(l0, l1)
                e0, e1 = jnp.exp(l0 - mx), jnp.exp(l1 - mx)
                p0 = e0 / (e0 + e1)
                d0 = o_ref[r0 + hh:r0 + hh + 1, :] * p0 * (1.0 - p0)
                d_ref[0:1, sl] = d0
                d_ref[1:2, sl] = -d0

    vm = pl.BlockSpec(memory_space=pltpu.VMEM)
    return pl.pallas_call(
        body, in_specs=[vm, vm, vm], out_specs=[vm, vm, vm],
        out_shape=[SDS(gathered.shape[1:], F32), SDS(lbf.shape, F32), SDS(lbb.shape, F32)], name=name)(gathered, lbf, lbb)


def _local_step(x2, tgt2, meta, W, S):
    T, D = x2.shape
    L = PAD + T
    h0 = jnp.concatenate([jnp.zeros((PAD - N_META, D), F32), meta, x2], axis=0)

    qk0 = Z_HG
    w_in = jnp.concatenate([W['w_in'][:, :qk0], _qk_to_group(W['w_in'][:, qk0:qk0 + AT_W + AT_KVW]),
                            W['w_in'][:, qk0 + AT_W + AT_KVW:]], axis=1)
    cc, ss = _rope_tables(L)
    wq_g, wk_g = _group_vec(S['q_norm']), _group_vec(S['k_norm'])

    def ffn_fwd(h, nw, wg, wu, wd, tag):
        n = _rmsnorm_fwd(h, nw, name=tag + "_norm")
        g, u, a = _ffn4_up(n, wg, wu, name=tag + "_up")
        hn = _ffn4_down(a, wd, h, name=tag + "_down")
        return hn, (n, g, u, a)

    def ffn_bwd(dh, h, nw, wg, wu, wd, saved, tag, split=False):
        n, g, u, a = saved
        dg, du = _ffn4_dact(dh, wd, g, u, name=tag + "_dact")
        dn = _ffn4_dn(dg, du, wg, wu, name=tag + "_dn")
        dwg = _ffn4_dw(n, dg, x_is_rows=True, name=tag + "_dwg")
        dwu = _ffn4_dw(n, du, x_is_rows=True, name=tag + "_dwu")
        dwd = _ffn4_dw(dh, a, x_is_rows=False, alpha=0.5, name=tag + "_dwd")
        *dhp, dnw = _rmsnorm_bwd(h, nw, dn, dh, split=split, name=tag + "_norm_bwd")
        return (dhp if split else dhp[0]), dnw, dwg, dwu, dwd

    h1, sv1 = ffn_fwd(h0, S['ffn1_norm'], W['ffn1_w_gate'], W['ffn1_w_up'], W['ffn1_w_down'], "ffn1")
    um = _rmsnorm_fwd(h1, S['mix_norm'], name="mix_norm")
    z = _mm([(um, w_in)], tm=512, tn=1792, tk=D, name="in_proj")
    of, sf = _hg_fwd(z, S['hg_lb_fwd'], rev=False, name="hg_fwd_f")
    ob, sb = _hg_fwd(z, S['hg_lb_bwd'], rev=True, name="hg_fwd_b")
    ya = _hg_post_fwd(of, ob, z, S['hg_out_norm'], name="hg_post")
    qm, qt, kr, krt, vb, vt = _at_prep(z, cc, ss, wq_g, wk_g, name="at_prep")
    yb, yb_f32, lse = _at_fwd(qt, kr, vt, name="at_fwd")
    mixed, *merge_factors = _merge_fwd(ya, yb, W['w_up_a'], W['w_up_b'], z, name="merge")
    h2 = _mm([(mixed, W['w_out'])], res=h1, tm=512, tn=D, tk=D, name="out_proj")
    h3, sv2 = ffn_fwd(h2, S['ffn2_norm'], W['ffn2_w_gate'], W['ffn2_w_up'], W['ffn2_w_down'], "ffn2")
    dh3, loss_lanes = _loss_head(h3, tgt2, name="loss_head")

    G = {}
    dh2, dn_ffn2, G['ffn2_w_gate'], G['ffn2_w_up'], G['ffn2_w_down'] = ffn_bwd(
        dh3, h2, S['ffn2_norm'], W['ffn2_w_gate'], W['ffn2_w_up'], W['ffn2_w_down'], sv2, "ffn2")
    dpa, dpb, dzga, dzgb = _merge_bwd(dh2, W['w_out'], merge_factors, name="merge_bwd")
    G['w_out'] = _mm([(mixed, dh2)], ta=True, tm=D, tn=D, tk=512, name="d_w_out")
    dya = _mm([(dpa, W['w_up_a'])], tb=True, tm=512, tn=HG_W, tk=D, name="d_ya")
    dyb = _mm([(dpb, W['w_up_b'])], tb=True, tm=512, tn=AT_W, tk=D, name="d_yb")
    G['w_up_a'] = _mm([(ya, dpa)], ta=True, tm=HG_W, tn=D, tk=512, name="d_w_up_a")
    G['w_up_b'] = _mm([(yb, dpb)], ta=True, tm=AT_W, tn=D, tk=512, name="d_w_up_b")
    do_hg, dzg, d_hgn = _hg_post_bwd(dya, of, ob, z, S['hg_out_norm'], name="hg_post_bwd")
    dq_f, dv_f, dzf_f, dlb_f = _hg_bwd(z, S['hg_lb_fwd'], do_hg, sf, None, rev=False, name="hg_bwd_f")
    dzq, dzi, dzf_b, dlb_b = _hg_bwd(z, S['hg_lb_bwd'], do_hg, sb, (dq_f, dv_f), rev=True, name="hg_bwd_b")
    dqm, dk2, dv2 = _at_bwd(qm, qt, kr, krt, vb, dyb, yb_f32, lse, name="at_bwd")
    dz_at, dwq_g, dwk_g = _at_prep_bwd(dqm, dk2, dv2, z, cc, ss, wq_g, wk_g, name="at_prep_bwd")
    dz = jnp.concatenate([dzq, dzi, dzf_f, dzf_b, dzg, dz_at, dzga, dzgb], axis=1)
    dum = _mm([(dz, w_in)], tb=True, tm=512, tn=D, tk=1792, name="d_um")
    dw_in_p = _mm([(um, dz)], ta=True, tm=D, tn=1792, tk=512, name="d_w_in")
    G['w_in'] = jnp.concatenate([dw_in_p[:, :qk0], _qk_from_group(dw_in_p[:, qk0:qk0 + AT_W + AT_KVW]),
                                 dw_in_p[:, qk0 + AT_W + AT_KVW:]], axis=1)
    dh1, dn_mix = _rmsnorm_bwd(h1, S['mix_norm'], dum, dh2, name="mix_norm_bwd")
    (grad_x, dmeta), dn_ffn1, G['ffn1_w_gate'], G['ffn1_w_up'], G['ffn1_w_down'] = ffn_bwd(
        dh1, h0, S['ffn1_norm'], W['ffn1_w_gate'], W['ffn1_w_up'], W['ffn1_w_down'], sv1, "ffn1", split=True)

    small_rows = [('loss', loss_lanes), ('ffn1_norm', dn_ffn1.reshape(-1, LANE)), ('mix_norm', dn_mix.reshape(-1, LANE)),
                  ('ffn2_norm', dn_ffn2.reshape(-1, LANE)), ('hg_out_norm', d_hgn.reshape(-1, LANE)),
                  ('lb_f', dlb_f.reshape(-1, LANE)), ('lb_b', dlb_b.reshape(-1, LANE)), ('q_norm', dwq_g), ('k_norm', dwk_g)]
    return grad_x, dmeta, G, small_rows


def kernel(x, meta_tokens, ffn1_norm, ffn1_w_gate, ffn1_w_up, ffn1_w_down, mix_norm, w_in, hg_lb_fwd, hg_lb_bwd, hg_out_norm, q_norm, k_norm, w_up_a, w_up_b, w_out, ffn2_norm, ffn2_w_gate, ffn2_w_up, ffn2_w_down, loss_target, m_meta_tokens, m_ffn1_norm, m_ffn1_w_gate, m_ffn1_w_up, m_ffn1_w_down, m_mix_norm, m_w_in, m_hg_lb_fwd, m_hg_lb_bwd, m_hg_out_norm, m_q_norm, m_k_norm, m_w_up_a, m_w_up_b, m_w_out, m_ffn2_norm, m_ffn2_w_gate, m_ffn2_w_up, m_ffn2_w_down, v_meta_tokens, v_ffn1_norm, v_ffn1_w_gate, v_ffn1_w_up, v_ffn1_w_down, v_mix_norm, v_w_in, v_hg_lb_fwd, v_hg_lb_bwd, v_hg_out_norm, v_q_norm, v_k_norm, v_w_up_a, v_w_up_b, v_w_out, v_ffn2_norm, v_ffn2_w_gate, v_ffn2_w_up, v_ffn2_w_down):
    given = dict(locals())
    w = {n: given[n] for n in WEIGHTS}
    mom = {n: given["m_" + n] for n in WEIGHTS}
    var = {n: given["v_" + n] for n in WEIGHTS}
    c = lax.axis_index("c")
    D = x.shape[-1]

    shapes = {n: w[n].shape[-2:] for n in MATS + ('meta_tokens',)}
    halves = [w[n].astype(BF16).reshape(2, shapes[n][0] // 2, shapes[n][1]) for n in MATS]
    gathered = _gather_mats(halves, name="gather_weights")
    s_me = 2 * lax.axis_index("x") + lax.axis_index("y")
    W = {}
    for n, hv, g4 in zip(MATS, halves, gathered):
        r, cs = shapes[n]
        g4 = lax.dynamic_update_index_in_dim(g4, hv, s_me, 0).reshape(4, r, cs)
        if n in FFN_MATS:
            W[n] = g4
        elif n in ROW_SHARDED:
            W[n] = g4.reshape(4 * r, cs)
        else:
            W[n] = g4.transpose(1, 0, 2).reshape(r, 4 * cs)
    meta_rows = w['meta_tokens'].reshape(-1, LANE)
    mg = _allgather_small(meta_rows, name="gather_meta").reshape(4, 2, N_META, -1)[:, 0]
    meta = mg.transpose(1, 0, 2).reshape(N_META, D)
    S = {n: w[n] for n in SMALLS}

    grad_x, dmeta, G, small_rows = _local_step(x[0], loss_target[0], meta, W, S)
    G['meta_tokens'] = dmeta

    names = MATS + ('meta_tokens',)
    views = []
    for n in names:
        r, cs = shapes[n]
        if n in FFN_MATS:
            g4 = G[n]
        elif n in ROW_SHARDED:
            g4 = G[n].reshape(4, r, cs)
        else:
            g4 = G[n].reshape(r, 4, cs).transpose(1, 0, 2)
        views.append(g4.reshape(4, 2, r // 2, cs))
    c1 = c.astype(jnp.int32).reshape(1)
    from_sibling = _rs_pair_exchange(views, name="rs_pair_exchange")
    parts = [_add_half(v, o, c1, out_dtype=F32 if n == 'meta_tokens' else BF16, name="rs_pair_sum_" + n)
             for n, v, o in zip(names, views, from_sibling)]
    slabs = _rs_chip_exchange(parts, name="rs_chip_exchange")
    reds = [_sum4(s, c1, name="rs_chip_sum_" + n) for n, s in zip(names, slabs)]
    both = _rs_pair_share(reds, name="rs_pair_share")
    grads = {n: b.reshape(w[n].shape) for n, b in zip(names, both)}

    rows, off = {}, 0
    for nme, blk in small_rows:
        rows[nme] = off
        off += blk.shape[0]
    block = jnp.concatenate([blk for _, blk in small_rows], axis=0)
    n_rows = (off + 7) // 8 * 8
    block = jnp.pad(block, ((0, n_rows - off), (0, 0)))
    allsmall = _allgather_small(block, name="gather_small").reshape(8, n_rows, LANE)
    tot, d_lbf, d_lbb = _finish_small(allsmall, w['hg_lb_fwd'], w['hg_lb_bwd'], rows=rows, name="finish_small")
    loss = 0.5 * tot[0, 0] / D

    def small(nme, shape):
        r0 = rows[nme]
        return tot[r0:r0 + shape[-1] // LANE].reshape(shape)

    grads['ffn1_norm'] = small('ffn1_norm', w['ffn1_norm'].shape)
    grads['mix_norm'] = small('mix_norm', w['mix_norm'].shape)
    grads['ffn2_norm'] = small('ffn2_norm', w['ffn2_norm'].shape)
    grads['hg_out_norm'] = small('hg_out_norm', w['hg_out_norm'].shape)
    grads['hg_lb_fwd'] = d_lbf
    grads['hg_lb_bwd'] = d_lbb
    grads['q_norm'] = _ungroup_vec(tot[rows['q_norm']])
    grads['k_norm'] = _ungroup_vec(tot[rows['k_norm']])

    delta, new_m, new_v = {}, {}, {}
    for n in WEIGHTS:
        delta[n], new_m[n], new_v[n] = _adamw(w[n], grads[n], mom[n], var[n], name="adamw_" + n)
    return (loss, grad_x[None], *[grads[n] for n in WEIGHTS], *[delta[n] for n in WEIGHTS],
            *[new_m[n] for n in WEIGHTS], *[new_v[n] for n in WEIGHTS])
```

```python
import numpy as np
import jax
import jax.numpy as jnp
from jax import lax
from jax.experimental import pallas as pl
from jax.experimental.pallas import tpu as pltpu

F32 = jnp.float32
BF16 = jnp.bfloat16
SDS = jax.ShapeDtypeStruct
MESH = pl.DeviceIdType.MESH

EPS = 1e-6
N_META = 16
PAD = 512
LANE = 128
CHUNK = 128
HG_HEADS = 4
HG_W = HG_HEADS * 128
AT_HEADS = 8
AT_KV = 2
AT_HD = 64
AT_W = AT_HEADS * AT_HD
AT_KVW = AT_KV * AT_HD
VT_ROWS = AT_HD + 16
FWD_CHUNKS_PER_STEP = 4
GRID_W = 64
ROPE_THETA = 10000.0
Z_HG = 5 * HG_W
Z_AT = AT_W + 2 * AT_KVW
ADAM_LR, ADAM_B1, ADAM_B2, ADAM_EPS, ADAM_WD, ADAM_STEP = 0.001, 0.9, 0.999, 1e-08, 0.01, 10
VMEM_DEFAULT = 48 * 1024 * 1024
VMEM_LARGE = 60 * 1024 * 1024
NEG = -1e30

MATS = ('ffn1_w_gate', 'ffn1_w_up', 'ffn1_w_down', 'w_in', 'w_up_a', 'w_up_b', 'w_out',
        'ffn2_w_gate', 'ffn2_w_up', 'ffn2_w_down')
ROW_SHARDED = ('ffn1_w_down', 'w_out', 'ffn2_w_down')
FFN_MATS = ('ffn1_w_gate', 'ffn1_w_up', 'ffn1_w_down', 'ffn2_w_gate', 'ffn2_w_up', 'ffn2_w_down')
SMALLS = ('ffn1_norm', 'mix_norm', 'hg_lb_fwd', 'hg_lb_bwd', 'hg_out_norm', 'q_norm', 'k_norm', 'ffn2_norm')
WEIGHTS = ('meta_tokens', 'ffn1_norm', 'ffn1_w_gate', 'ffn1_w_up', 'ffn1_w_down', 'mix_norm', 'w_in', 'hg_lb_fwd',
           'hg_lb_bwd', 'hg_out_norm', 'q_norm', 'k_norm', 'w_up_a', 'w_up_b', 'w_out', 'ffn2_norm', 'ffn2_w_gate',
           'ffn2_w_up', 'ffn2_w_down')


def _params(sem=None, vmem=VMEM_DEFAULT):
    return pltpu.CompilerParams(dimension_semantics=sem, vmem_limit_bytes=vmem)


def _tile(n, pref, q=LANE):
    for d in range(min(pref, n), 0, -1):
        if n % d == 0 and d % q == 0:
            return d
    return n


def _sigmoid(x):
    return 0.5 * jnp.tanh(0.5 * x) + 0.5


def _dot(a, b, dims):
    return lax.dot_general(a, b, (dims, ((), ())), preferred_element_type=F32)


def _nn(a, b):
    return _dot(a, b, ((1,), (0,)))


def _nt(a, b):
    return _dot(a, b, ((1,), (1,)))


def _tn(a, b):
    return _dot(a, b, ((0,), (0,)))


def _split3(x):
    x1 = x.astype(BF16)
    r = x - x1.astype(F32)
    x2 = r.astype(BF16)
    x3 = (r - x2.astype(F32)).astype(BF16)
    return x1, x2, x3


def _exact_left(m01, x):
    x1, x2, x3 = _split3(x)
    return _nn(m01, x1) + _nn(m01, x2) + _nn(m01, x3)


def _exact_right(x, m01):
    x1, x2, x3 = _split3(x)
    return _nn(x1, m01) + _nn(x2, m01) + _nn(x3, m01)


def _mm(pairs, *, name, ta=False, tb=False, out_dtype=F32, tm=512, tn=1024, tk=1024, alpha=1.0, res=None):
    a0, b0 = pairs[0]
    M = a0.shape[1] if ta else a0.shape[0]
    K = a0.shape[0] if ta else a0.shape[1]
    N = b0.shape[0] if tb else b0.shape[1]
    tm, tn, tk = _tile(M, tm), _tile(N, tn), _tile(K, tk)
    nk = K // tk
    npair = len(pairs)
    dims = ((0 if ta else 1,), (1 if tb else 0,))

    def body(*refs):
        ab = refs[:2 * npair]
        pos = 2 * npair
        res_ref = None
        if res is not None:
            res_ref = refs[pos]
            pos += 1
        o_ref = refs[pos]

        def partial_sum():
            tot = None
            for p in range(npair):
                d = _dot(ab[2 * p][...].astype(BF16), ab[2 * p + 1][...].astype(BF16), dims)
                tot = d if tot is None else tot + d
            return tot

        def finish(acc):
            r = acc if alpha == 1.0 else acc * alpha
            if res_ref is not None:
                r = res_ref[...] + r
            o_ref[...] = r.astype(out_dtype)

        if nk == 1:
            finish(partial_sum())
        else:
            acc_ref = refs[pos + 1]
            k = pl.program_id(2)

            @pl.when(k == 0)
            def _():
                acc_ref[...] = jnp.zeros_like(acc_ref)

            acc_ref[...] += partial_sum()

            @pl.when(k == nk - 1)
            def _():
                finish(acc_ref[...])

    a_spec = pl.BlockSpec((tk, tm), lambda j, i, k: (k, i)) if ta else pl.BlockSpec((tm, tk), lambda j, i, k: (i, k))
    b_spec = pl.BlockSpec((tn, tk), lambda j, i, k: (j, k)) if tb else pl.BlockSpec((tk, tn), lambda j, i, k: (k, j))
    o_spec = pl.BlockSpec((tm, tn), lambda j, i, k: (i, j))
    in_specs, args = [], []
    for a, b in pairs:
        in_specs += [a_spec, b_spec]
        args += [a, b]
    if res is not None:
        in_specs.append(o_spec)
        args.append(res)
    return pl.pallas_call(
        body, grid=(N // tn, M // tm, nk), in_specs=in_specs, out_specs=o_spec,
        out_shape=SDS((M, N), out_dtype),
        scratch_shapes=[pltpu.VMEM((tm, tn), F32)] if nk > 1 else [],
        compiler_params=_params(("parallel", "parallel", "arbitrary")), name=name)(*args)


def _rmsnorm_fwd(h, w, *, name):
    L, D = h.shape
    tm = _tile(L, 512)

    def body(h_ref, w_ref, o_ref):
        x = h_ref[...]
        r = lax.rsqrt(jnp.mean(x * x, axis=-1, keepdims=True) + EPS)
        o_ref[...] = (x * r * w_ref[...]).astype(BF16)

    return pl.pallas_call(
        body, grid=(L // tm,),
        in_specs=[pl.BlockSpec((tm, D), lambda i: (i, 0)), pl.BlockSpec((1, D), lambda i: (0, 0))],
        out_specs=pl.BlockSpec((tm, D), lambda i: (i, 0)), out_shape=SDS((L, D), BF16),
        compiler_params=_params(("parallel",)), name=name)(h, w)


def _rmsnorm_bwd(h, w, dn, dres, *, split=False, name):
    L, D = h.shape
    tm = PAD if split else _tile(L, 512)

    def body(h_ref, w_ref, dn_ref, dres_ref, dh_ref, *rest):
        dw_ref = rest[-1]
        i = pl.program_id(0)
        x = h_ref[...]
        r = lax.rsqrt(jnp.mean(x * x, axis=-1, keepdims=True) + EPS)
        xh = x * r
        dn = dn_ref[...]
        dxh = dn * w_ref[...]
        dh = dres_ref[...] + r * (dxh - xh * jnp.mean(dxh * xh, axis=-1, keepdims=True))
        dh_ref[...] = dh

        @pl.when(i == 0)
        def _():
            dw_ref[...] = jnp.zeros_like(dw_ref)
            if split:
                rest[0][...] = dh[PAD - N_META:]

        dw_ref[...] += jnp.sum(dn * xh, axis=0, keepdims=True)

    row = pl.BlockSpec((tm, D), lambda i: (i, 0))
    vec = pl.BlockSpec((1, D), lambda i: (0, 0))
    if split:
        out_specs = [pl.BlockSpec((tm, D), lambda i: (jnp.maximum(i - 1, 0), 0)), pl.BlockSpec((N_META, D), lambda i: (0, 0)), vec]
        out_shape = [SDS((L - PAD, D), F32), SDS((N_META, D), F32), SDS((1, D), F32)]
    else:
        out_specs, out_shape = [row, vec], [SDS((L, D), F32), SDS((1, D), F32)]
    return pl.pallas_call(
        body, grid=(L // tm,), in_specs=[row, vec, row, row], out_specs=out_specs, out_shape=out_shape,
        compiler_params=_params(("arbitrary",)), name=name)(h, w, dn, dres)


def _ffn4_up(n, wg4, wu4, *, name):
    L, D = n.shape
    ns, _, cs = wg4.shape
    tm = _tile(L, 768)

    def body(n_ref, wg_ref, wu_ref, ag_ref, au_ref, a_ref):
        x = n_ref[...]
        g = _nn(x, wg_ref[...])
        u = _nn(x, wu_ref[...])
        sg = _sigmoid(g)
        silu = g * sg
        ag_ref[...] = (u * (sg * (1.0 + g * (1.0 - sg)))).astype(BF16)
        au_ref[...] = silu.astype(BF16)
        a_ref[...] = (silu * u).astype(BF16)

    wspec = pl.BlockSpec((None, D, cs), lambda j, i: (j, 0, 0))
    ospec = pl.BlockSpec((None, tm, cs), lambda j, i: (j, i, 0))
    return pl.pallas_call(
        body, grid=(ns, L // tm),
        in_specs=[pl.BlockSpec((tm, D), lambda j, i: (i, 0)), wspec, wspec], out_specs=[ospec, ospec, ospec],
        out_shape=[SDS((ns, L, cs), BF16), SDS((ns, L, cs), BF16), SDS((ns, L, cs), BF16)],
        compiler_params=_params(("parallel", "parallel")), name=name)(n, wg4, wu4)


def _ffn4_down(a4, wd4, h, *, name):
    ns, L, cs = a4.shape
    D = wd4.shape[2]
    tm = _tile(L, 512)

    def body(a_ref, w_ref, h_ref, o_ref):
        acc = _nn(a_ref[0], w_ref[0])
        for j in range(1, ns):
            acc = acc + _nn(a_ref[j], w_ref[j])
        o_ref[...] = h_ref[...] + 0.5 * acc

    row = pl.BlockSpec((tm, D), lambda i: (i, 0))
    return pl.pallas_call(
        body, grid=(L // tm,),
        in_specs=[pl.BlockSpec((ns, tm, cs), lambda i: (0, i, 0)), pl.BlockSpec((ns, cs, D), lambda i: (0, 0, 0)), row],
        out_specs=row, out_shape=SDS((L, D), F32),
        compiler_params=_params(("parallel",)), name=name)(a4, wd4, h)


def _ffn4_dact(dh, wd4, ag4, au4, *, name):
    L, D = dh.shape
    ns, cs, _ = wd4.shape
    tm = _tile(L, 768)

    def body(dh_ref, wd_ref, ag_ref, au_ref, dg_ref, du_ref):
        da = 0.5 * _nt(dh_ref[...].astype(BF16), wd_ref[...])
        dg_ref[...] = (da * ag_ref[...].astype(F32)).astype(BF16)
        du_ref[...] = (da * au_ref[...].astype(F32)).astype(BF16)

    ospec = pl.BlockSpec((None, tm, cs), lambda j, i: (j, i, 0))
    return pl.pallas_call(
        body, grid=(ns, L // tm),
        in_specs=[pl.BlockSpec((tm, D), lambda j, i: (i, 0)), pl.BlockSpec((None, cs, D), lambda j, i: (j, 0, 0)), ospec, ospec],
        out_specs=[ospec, ospec], out_shape=[SDS((ns, L, cs), BF16), SDS((ns, L, cs), BF16)],
        compiler_params=_params(("parallel", "parallel")), name=name)(dh, wd4, ag4, au4)


def _ffn4_dn(dg4, du4, wg4, wu4, *, name):
    ns, L, cs = dg4.shape
    D = wg4.shape[1]
    tm = _tile(L, 512)

    def body(dg_ref, du_ref, wg_ref, wu_ref, o_ref):
        acc = None
        for j in range(ns):
            t = _nt(dg_ref[j], wg_ref[j]) + _nt(du_ref[j], wu_ref[j])
            acc = t if acc is None else acc + t
        o_ref[...] = acc

    aspec = pl.BlockSpec((ns, tm, cs), lambda i: (0, i, 0))
    wspec = pl.BlockSpec((ns, D, cs), lambda i: (0, 0, 0))
    return pl.pallas_call(
        body, grid=(L // tm,), in_specs=[aspec, aspec, wspec, wspec],
        out_specs=pl.BlockSpec((tm, D), lambda i: (i, 0)), out_shape=SDS((L, D), F32),
        compiler_params=_params(("parallel",), VMEM_LARGE), name=name)(dg4, du4, wg4, wu4)


def _ffn4_dw(x, y4, *, x_is_rows, alpha=1.0, name):
    L, D = x.shape
    ns, _, cs = y4.shape
    tk = _tile(L, 512)
    nk = L // tk
    oshape = (D, cs) if x_is_rows else (cs, D)

    def body(x_ref, y_ref, o_ref):
        k = pl.program_id(0)

        @pl.when(k == 0)
        def _():
            o_ref[...] = jnp.zeros_like(o_ref)

        xb = x_ref[...].astype(BF16)
        if x_is_rows:
            xt = xb.T
            for j in range(ns):
                o_ref[j] += _nn(xt, y_ref[j])
        else:
            for j in range(ns):
                o_ref[j] += _tn(y_ref[j], xb)

        if alpha != 1.0:
            @pl.when(k == nk - 1)
            def _():
                o_ref[...] = o_ref[...] * alpha

    return pl.pallas_call(
        body, grid=(nk,),
        in_specs=[pl.BlockSpec((tk, D), lambda k: (k, 0)), pl.BlockSpec((ns, tk, cs), lambda k: (0, k, 0))],
        out_specs=pl.BlockSpec((ns,) + oshape, lambda k: (0, 0, 0)), out_shape=SDS((ns,) + oshape, F32),
        compiler_params=_params(("arbitrary",)), name=name)(x, y4)


def _hg_mask_table(rev):
    t = np.arange(CHUNK)[:, None]
    s = np.arange(CHUNK)[None, :]
    causal = (s >= t) if rev else (s <= t)
    out = [causal]
    for sh in (6, 5, 4):
        same = (t >> (sh + 1)) == (s >> (sh + 1))
        out.append(same & (((t >> sh) & 1) == (0 if rev else 1)) & (((s >> sh) & 1) == (1 if rev else 0)))
    out.append(((t >> 4) == (s >> 4)) & causal)
    out.append(causal.T)
    return jnp.asarray(np.stack(out).astype(np.float32))


def _hg_masks(mk_ref):
    on = [mk_ref[i] > 0.5 for i in range(5)]
    return on[0], on[1:4], on[4], mk_ref[0].astype(BF16), mk_ref[5].astype(BF16)


def _hg_intra_factors(q, k, b, b_scr, rev):
    b_scr[...] = b
    row = lax.broadcasted_iota(jnp.int32, (CHUNK, LANE), 0)
    out = []
    for sh in (6, 5, 4):
        lb = 1 << sh
        pieces = []
        for p in range(0, CHUNK, 2 * lb):
            r = p + lb if rev else p + lb - 1
            pieces.append(jnp.broadcast_to(b_scr[pl.ds(r, 1), :], (2 * lb, LANE)))
        ref = pieces[0] if len(pieces) == 1 else jnp.concatenate(pieces, axis=0)
        qside = jnp.bitwise_and(jnp.right_shift(row, sh), 1) == (0 if rev else 1)
        d = b - ref
        e = jnp.exp(jnp.minimum(jnp.where(qside, d, -d), 0.0))
        eq = jnp.where(qside, e, 0.0)
        ek = jnp.where(qside, 0.0, e)
        out.append((eq, ek, (q * eq).astype(BF16), (k * ek).astype(BF16)))
    pieces = []
    for a in range(0, CHUNK, 16):
        r = a + (8 if rev else 7)
        pieces.append(jnp.broadcast_to(b_scr[pl.ds(r, 1), :], (16, LANE)))
    ref = jnp.concatenate(pieces, axis=0)
    eq = jnp.exp(jnp.minimum(b - ref, 80.0))
    ek = jnp.exp(jnp.minimum(ref - b, 80.0))
    out.append((eq, ek, (q * eq).astype(BF16), (k * ek).astype(BF16)))
    return out


def _hg_gate(zf, l0, l1, valid):
    mx = jnp.maximum(l0, l1)
    e0, e1 = jnp.exp(l0 - mx), jnp.exp(l1 - mx)
    p0 = e0 / (e0 + e1)
    sg = _sigmoid(-zf)
    k = jnp.where(valid, (1.0 - p0) * sg, 0.0)
    return p0, sg, k, jnp.log(1.0 - k)


def _hg_fwd(z, lbf, lbb, *, name):
    L = z.shape[0]
    nc = L // CHUNK

    def cidx(rev):
        return (lambda j: nc - 1 - j) if rev else (lambda j: j)

    def one(rev, j, hh, zq_ref, zi_ref, zf_ref, lb_ref, mk_ref, o_ref, ssave_ref, st_scr, b_scr):
        _, lmasks, dmask, tri, _ = _hg_masks(mk_ref)
        rowg = cidx(rev)(j) * CHUNK + lax.broadcasted_iota(jnp.int32, (CHUNK, LANE), 0)
        valid = rowg >= PAD - N_META
        last = 0 if rev else CHUNK - 1
        sl = slice(LANE * hh, LANE * (hh + 1))
        zq = zq_ref[:, sl]
        q = zq * _sigmoid(zq)
        v = zi_ref[:, sl].astype(BF16)
        _, _, k, g = _hg_gate(zf_ref[:, sl], lb_ref[0:1, sl], lb_ref[1:2, sl], valid)
        b = _exact_left(tri, g)
        st = st_scr[hh]
        ssave_ref[0, hh] = st
        o = _nt((q * jnp.exp(b)).astype(BF16), st.astype(BF16))
        a = None
        fac = _hg_intra_factors(q, k, b, b_scr, rev)
        for (eq, ek, qq, kk), msk in zip(fac, lmasks + [dmask]):
            t = jnp.where(msk, _nt(qq, kk), 0.0)
            a = t if a is None else a + t
        o_ref[:, sl] = o + _nn(a.astype(BF16), v)
        bl = b_scr[pl.ds(last, 1), :]
        kd = (k * jnp.exp(bl - b)).astype(BF16)
        st_scr[hh] = st * jnp.exp(bl) + _tn(v, kd)

    def body(zqf, zif, zff, lbf_ref, mkf, zqb, zib, zfb, lbb_ref, mkb, of_ref, sf_ref, ob_ref, sb_ref,
             stf_scr, stb_scr, bf_scr, bb_scr):
        j = pl.program_id(0)

        @pl.when(j == 0)
        def _():
            stf_scr[...] = jnp.zeros_like(stf_scr)
            stb_scr[...] = jnp.zeros_like(stb_scr)

        for hh in range(HG_HEADS):
            one(False, j, hh, zqf, zif, zff, lbf_ref, mkf, of_ref, sf_ref, stf_scr, bf_scr)
            one(True, j, hh, zqb, zib, zfb, lbb_ref, mkb, ob_ref, sb_ref, stb_scr, bb_scr)

    def specs(rev):
        zs = lambda col: pl.BlockSpec((CHUNK, HG_W), lambda j: (cidx(rev)(j), col))
        return [zs(0), zs(1), zs(3 if rev else 2), pl.BlockSpec((2, HG_W), lambda j: (0, 0)),
                pl.BlockSpec((6, CHUNK, CHUNK), lambda j: (0, 0, 0))]

    def outs(rev):
        return [pl.BlockSpec((CHUNK, HG_W), lambda j: (cidx(rev)(j), 0)),
                pl.BlockSpec((1, HG_HEADS, LANE, LANE), lambda j: (cidx(rev)(j), 0, 0, 0))]

    shp = [SDS((L, HG_W), F32), SDS((nc, HG_HEADS, LANE, LANE), F32)]
    of, sf, ob, sb = pl.pallas_call(
        body, grid=(nc,), in_specs=specs(False) + specs(True), out_specs=outs(False) + outs(True), out_shape=shp + shp,
        scratch_shapes=[pltpu.VMEM((HG_HEADS, LANE, LANE), F32)] * 2 + [pltpu.VMEM((CHUNK, LANE), F32)] * 2,
        compiler_params=_params(("arbitrary",)), name=name)(
            z, z, z, lbf, _hg_mask_table(False), z, z, z, lbb, _hg_mask_table(True))
    return (of, sf), (ob, sb)


def _hg_bwd(z, lbp, do, ssave, prev, *, rev, name):
    L = z.shape[0]
    nc = L // CHUNK
    fcol = 3 if rev else 2
    final = prev is not None

    def cidx(j):
        return j if rev else nc - 1 - j

    def body(*refs):
        zq_ref, zi_ref, zf_ref, lb_ref, mk_ref, do_ref, ss_ref = refs[:7]
        pos = 7
        if final:
            dqin_ref, dvin_ref = refs[7:9]
            pos = 9
        dq_ref, dv_ref, dzf_ref, dlb_ref, dst_scr, b_scr = refs[pos:pos + 6]
        j = pl.program_id(0)

        @pl.when(j == 0)
        def _():
            dst_scr[...] = jnp.zeros_like(dst_scr)
            dlb_ref[...] = jnp.zeros_like(dlb_ref)

        causal, lmasks, dmask, tri, tri_t = _hg_masks(mk_ref)
        rowg = cidx(j) * CHUNK + lax.broadcasted_iota(jnp.int32, (CHUNK, LANE), 0)
        valid = rowg >= PAD - N_META
        last = 0 if rev else CHUNK - 1
        for hh in range(HG_HEADS):
            sl = slice(LANE * hh, LANE * (hh + 1))
            zq = zq_ref[:, sl]
            sq = _sigmoid(zq)
            q = zq * sq
            v = zi_ref[:, sl].astype(BF16)
            p0, sg, k, g = _hg_gate(zf_ref[:, sl], lb_ref[0:1, sl], lb_ref[1:2, sl], valid)
            b = _exact_left(tri, g)
            dob = do_ref[:, sl].astype(BF16)
            st = ss_ref[0, hh]
            dst = dst_scr[hh]
            stb, dstb = st.astype(BF16), dst.astype(BF16)
            eb = jnp.exp(b)
            qe = (q * eb).astype(BF16)
            fac = _hg_intra_factors(q, k, b, b_scr, rev)
            bl = b_scr[pl.ds(last, 1), :]
            ebl = jnp.exp(bl)
            kde = jnp.exp(bl - b)
            kd = (k * kde).astype(BF16)
            da = jnp.where(causal, _nt(dob, v), 0.0)
            dq = eb * _nn(dob, stb)
            dk_inter = kde * _nn(v, dstb)
            dk = dk_inter
            dv = _nt(kd, dstb)
            a = None
            db = q * dq - k * dk
            for (eq, ek, qq, kk), msk in zip(fac, lmasks + [dmask]):
                t = jnp.where(msk, _nt(qq, kk), 0.0)
                a = t if a is None else a + t
                dal = jnp.where(msk, da, 0.0).astype(BF16)
                mq = _nn(dal, kk)
                mk = _tn(dal, qq)
                dq = dq + eq * mq
                dk = dk + ek * mk
                db = db + (qq.astype(F32) * mq - kk.astype(F32) * mk)
            dv = dv + _tn(a.astype(BF16), dob)
            extra = ebl * jnp.sum(st * dst, axis=0, keepdims=True) + jnp.sum(k * dk_inter, axis=0, keepdims=True)
            dst_scr[hh] = dst * ebl + _tn(dob, qe)
            dg = _exact_left(tri_t, db) + extra
            dk_tot = dk - dg / (1.0 - k)
            dzf_ref[:, sl] = jnp.where(valid, dk_tot * (1.0 - p0) * (-sg * (1.0 - sg)), 0.0).astype(BF16)
            dlb_ref[:, sl] += jnp.sum(jnp.where(valid, -sg * dk_tot, 0.0), axis=0, keepdims=True)
            if final:
                dq_ref[:, sl] = ((dq + dqin_ref[:, sl]) * (sq * (1.0 + zq * (1.0 - sq)))).astype(BF16)
                dv_ref[:, sl] = (dv + dvin_ref[:, sl]).astype(BF16)
            else:
                dq_ref[:, sl] = dq
                dv_ref[:, sl] = dv

    zspec = lambda col: pl.BlockSpec((CHUNK, HG_W), lambda j: (cidx(j), col))
    rspec = pl.BlockSpec((CHUNK, HG_W), lambda j: (cidx(j), 0))
    in_specs = [zspec(0), zspec(1), zspec(fcol), pl.BlockSpec((2, HG_W), lambda j: (0, 0)),
                pl.BlockSpec((6, CHUNK, CHUNK), lambda j: (0, 0, 0)), rspec,
                pl.BlockSpec((1, HG_HEADS, LANE, LANE), lambda j: (cidx(j), 0, 0, 0))]
    args = [z, z, z, lbp, _hg_mask_table(rev), do, ssave]
    if final:
        in_specs += [rspec, rspec]
        args += list(prev)
    odt = BF16 if final else F32
    return pl.pallas_call(
        body, grid=(nc,), in_specs=in_specs,
        out_specs=[rspec, rspec, rspec, pl.BlockSpec((1, HG_W), lambda j: (0, 0))],
        out_shape=[SDS((L, HG_W), odt), SDS((L, HG_W), odt), SDS((L, HG_W), BF16), SDS((1, HG_W), F32)],
        scratch_shapes=[pltpu.VMEM((HG_HEADS, LANE, LANE), F32), pltpu.VMEM((CHUNK, LANE), F32)],
        compiler_params=_params(("arbitrary",)), name=name)(*args)


def _hg_post_fwd(of, ob, z, w, *, name):
    L = of.shape[0]
    tm = _tile(L, 512)

    def body(of_ref, ob_ref, zg_ref, w_ref, y_ref):
        for hh in range(HG_HEADS):
            sl = slice(LANE * hh, LANE * (hh + 1))
            o = of_ref[:, sl] + ob_ref[:, sl]
            r = lax.rsqrt(jnp.mean(o * o, axis=-1, keepdims=True) + EPS)
            zg = zg_ref[:, sl]
            y_ref[:, sl] = (o * r * w_ref[:, sl] * (zg * _sigmoid(zg))).astype(BF16)

    row = pl.BlockSpec((tm, HG_W), lambda i: (i, 0))
    return pl.pallas_call(
        body, grid=(L // tm,),
        in_specs=[row, row, pl.BlockSpec((tm, HG_W), lambda i: (i, 4)), pl.BlockSpec((1, HG_W), lambda i: (0, 0))],
        out_specs=row, out_shape=SDS((L, HG_W), BF16),
        compiler_params=_params(("parallel",)), name=name)(of, ob, z, w)


def _hg_post_bwd(dy, of, ob, z, w, *, name):
    L = of.shape[0]
    tm = _tile(L, 512)

    def body(dy_ref, of_ref, ob_ref, zg_ref, w_ref, do_ref, dzg_ref, dw_ref):
        @pl.when(pl.program_id(0) == 0)
        def _():
            dw_ref[...] = jnp.zeros_like(dw_ref)

        for hh in range(HG_HEADS):
            sl = slice(LANE * hh, LANE * (hh + 1))
            o = of_ref[:, sl] + ob_ref[:, sl]
            r = lax.rsqrt(jnp.mean(o * o, axis=-1, keepdims=True) + EPS)
            xh = o * r
            zg = zg_ref[:, sl]
            sg = _sigmoid(zg)
            w = w_ref[:, sl]
            dy = dy_ref[:, sl]
            dys = dy * (zg * sg)
            dzg_ref[:, sl] = (dy * xh * w * (sg * (1.0 + zg * (1.0 - sg)))).astype(BF16)
            dw_ref[:, sl] += jnp.sum(dys * xh, axis=0, keepdims=True)
            dxh = dys * w
            do_ref[:, sl] = r * (dxh - xh * jnp.mean(dxh * xh, axis=-1, keepdims=True))

    row = pl.BlockSpec((tm, HG_W), lambda i: (i, 0))
    vec = pl.BlockSpec((1, HG_W), lambda i: (0, 0))
    return pl.pallas_call(
        body, grid=(L // tm,),
        in_specs=[row, row, row, pl.BlockSpec((tm, HG_W), lambda i: (i, 4)), vec],
        out_specs=[row, row, vec],
        out_shape=[SDS((L, HG_W), F32), SDS((L, HG_W), BF16), SDS((1, HG_W), F32)],
        compiler_params=_params(("arbitrary",)), name=name)(dy, of, ob, z, w)


N_GROUPS = (AT_HEADS + AT_KV) // 2


def _qk_to_group(wqk):
    d = wqk.shape[0]
    return wqk.reshape(d, N_GROUPS, 2, AT_HD // 2, 2).transpose(0, 1, 4, 2, 3).reshape(d, N_GROUPS * LANE)


def _qk_from_group(wqk):
    d = wqk.shape[0]
    return wqk.reshape(d, N_GROUPS, 2, 2, AT_HD // 2).transpose(0, 1, 3, 4, 2).reshape(d, N_GROUPS * LANE)


def _group_vec(w64):
    halves = w64.reshape(AT_HD // 2, 2).T
    return jnp.broadcast_to(halves[:, None, :], (2, 2, AT_HD // 2)).reshape(1, LANE)


def _ungroup_vec(w128):
    w = w128.reshape(2, 2, 32).sum(axis=1)
    return w.T.reshape(1, AT_HD)


def _rope_tables(L):
    n_real = L - PAD
    t = np.arange(n_real)
    row = np.concatenate([np.zeros(PAD), t // GRID_W]).astype(np.float32)
    col = np.concatenate([np.zeros(PAD), t % GRID_W]).astype(np.float32)
    inv = jnp.asarray(ROPE_THETA, F32) ** (-jnp.arange(0, AT_HD // 2, 2, dtype=F32) / (AT_HD // 2))
    ang = jnp.concatenate([jnp.asarray(row)[:, None] * inv, jnp.asarray(col)[:, None] * inv], axis=-1)
    cos, sin = jnp.cos(ang), jnp.sin(ang)
    cc = jnp.tile(cos, (1, 4))
    ss = jnp.concatenate([-sin, -sin, sin, sin], axis=1)
    return cc, ss


def _seg_matrix():
    a = lax.broadcasted_iota(jnp.int32, (LANE, LANE), 0)
    b = lax.broadcasted_iota(jnp.int32, (LANE, LANE), 1)
    same = jnp.bitwise_and(jnp.right_shift(a, 5), 1) == jnp.bitwise_and(jnp.right_shift(b, 5), 1)
    return jnp.where(same, 1.0, 0.0).astype(BF16)


def _slot_mask(shape, hp):
    lane = lax.broadcasted_iota(jnp.int32, shape, 1)
    return jnp.bitwise_and(jnp.right_shift(lane, 5), 1) == hp


def _at_prep(z, cc, ss, wq, wk, *, name):
    L = z.shape[0]
    tm = PAD
    qcol = Z_HG // AT_W
    kvcol = (Z_HG + AT_W) // (2 * LANE)

    def body(zq_ref, zkv_ref, cc_ref, ss_ref, wq_ref, wk_ref, qm_ref, qt_ref, kr_ref, krt_ref, vb_ref, vt_ref):
        seg = _seg_matrix()
        cc, ss = cc_ref[...], ss_ref[...]

        def normrope(x, w):
            r = lax.rsqrt(_exact_right(x * x, seg) * (1.0 / AT_HD) + EPS)
            y = x * r * w
            return y * cc + pltpu.roll(y, 64, 1) * ss

        for g in range(AT_HEADS // 2):
            o = normrope(zq_ref[:, LANE * g:LANE * (g + 1)], wq_ref[...]) * (AT_HD ** -0.5)
            for hp in range(2):
                h = 2 * g + hp
                tgt = h // (AT_HEADS // AT_KV)
                xm = jnp.where(_slot_mask(o.shape, hp), o, 0.0)
                if tgt != hp:
                    xm = pltpu.roll(xm, 32 if tgt == 1 else 96, 1)
                qm_ref[h] = xm.astype(BF16)
                qt_ref[h] = xm.T.astype(BF16)
        kr = normrope(zkv_ref[:, :LANE], wk_ref[...])
        kr_ref[...] = kr.astype(BF16)
        krt_ref[0] = kr.T.astype(BF16)
        v = zkv_ref[:, LANE:]
        low = lax.broadcasted_iota(jnp.int32, v.shape, 1) < AT_HD
        vb_ref[0] = jnp.where(low, v, 0.0).astype(BF16)
        vb_ref[1] = jnp.where(low, pltpu.roll(v, AT_HD, 1), 0.0).astype(BF16)
        vt = v.T.astype(BF16)
        ones = jnp.ones((VT_ROWS - AT_HD, tm), BF16)
        for j in range(AT_KV):
            vt_ref[j, 0, 0:AT_HD, :] = vt[AT_HD * j:AT_HD * (j + 1)]
            vt_ref[j, 0, AT_HD:VT_ROWS, :] = ones

    tab = pl.BlockSpec((tm, LANE), lambda i: (i, 0))
    vec = pl.BlockSpec((1, LANE), lambda i: (0, 0))
    nt = L // tm
    return pl.pallas_call(
        body, grid=(nt,),
        in_specs=[pl.BlockSpec((tm, AT_W), lambda i: (i, qcol)), pl.BlockSpec((tm, 2 * LANE), lambda i: (i, kvcol)),
                  tab, tab, vec, vec],
        out_specs=[pl.BlockSpec((AT_HEADS, tm, LANE), lambda i: (0, i, 0)),
                   pl.BlockSpec((AT_HEADS, LANE, tm), lambda i: (0, 0, i)), tab,
                   pl.BlockSpec((1, LANE, tm), lambda i: (i, 0, 0)),
                   pl.BlockSpec((AT_KV, tm, LANE), lambda i: (0, i, 0)),
                   pl.BlockSpec((AT_KV, 1, VT_ROWS, tm), lambda i: (0, i, 0, 0))],
        out_shape=[SDS((AT_HEADS, L, LANE), BF16), SDS((AT_HEADS, LANE, L), BF16), SDS((L, LANE), BF16),
                   SDS((nt, LANE, tm), BF16), SDS((AT_KV, L, LANE), BF16), SDS((AT_KV, nt, VT_ROWS, tm), BF16)],
        compiler_params=_params(("parallel",)), name=name)(z, z, cc, ss, wq, wk)


def _at_prep_bwd(dqm, dk2, dv2, z, cc, ss, wq, wk, *, name):
    L = z.shape[0]
    tm = PAD
    qcol = Z_HG // AT_W
    kvcol = (Z_HG + AT_W) // (2 * LANE)

    def body(dqm_ref, dk2_ref, dv2_ref, zq_ref, zkv_ref, cc_ref, ss_ref, wq_ref, wk_ref, dz_ref, dwq_ref, dwk_ref):
        @pl.when(pl.program_id(0) == 0)
        def _():
            dwq_ref[...] = jnp.zeros_like(dwq_ref)
            dwk_ref[...] = jnp.zeros_like(dwk_ref)

        seg = _seg_matrix()
        cc, ss = cc_ref[...], ss_ref[...]

        def back(x, w, do):
            dy = do * cc + pltpu.roll(do * ss, 64, 1)
            r = lax.rsqrt(_exact_right(x * x, seg) * (1.0 / AT_HD) + EPS)
            xh = x * r
            dxh = dy * w
            dx = r * (dxh - xh * (_exact_right(dxh * xh, seg) * (1.0 / AT_HD)))
            return dx, jnp.sum(dy * xh, axis=0, keepdims=True)

        for g in range(AT_HEADS // 2):
            do = None
            for hp in range(2):
                h = 2 * g + hp
                tgt = h // (AT_HEADS // AT_KV)
                d = jnp.where(_slot_mask((tm, LANE), tgt), dqm_ref[h], 0.0)
                if tgt != hp:
                    d = pltpu.roll(d, 96 if tgt == 1 else 32, 1)
                do = d if do is None else do + d
            dx, dw = back(zq_ref[:, LANE * g:LANE * (g + 1)], wq_ref[...], do * (AT_HD ** -0.5))
            dz_ref[:, LANE * g:LANE * (g + 1)] = dx.astype(BF16)
            dwq_ref[...] += dw
        dx, dw = back(zkv_ref[:, :LANE], wk_ref[...], dk2_ref[0] + dk2_ref[1])
        dz_ref[:, AT_W:AT_W + LANE] = dx.astype(BF16)
        dwk_ref[...] += dw
        dv0 = dv2_ref[0]
        low = lax.broadcasted_iota(jnp.int32, dv0.shape, 1) < AT_HD
        dz_ref[:, AT_W + LANE:] = jnp.where(low, dv0, pltpu.roll(dv2_ref[1], AT_HD, 1)).astype(BF16)

    tab = pl.BlockSpec((tm, LANE), lambda i: (i, 0))
    vec = pl.BlockSpec((1, LANE), lambda i: (0, 0))
    two = pl.BlockSpec((AT_KV, tm, LANE), lambda i: (0, i, 0))
    return pl.pallas_call(
        body, grid=(L // tm,),
        in_specs=[pl.BlockSpec((AT_HEADS, tm, LANE), lambda i: (0, i, 0)), two, two,
                  pl.BlockSpec((tm, AT_W), lambda i: (i, qcol)), pl.BlockSpec((tm, 2 * LANE), lambda i: (i, kvcol)),
                  tab, tab, vec, vec],
        out_specs=[pl.BlockSpec((tm, Z_AT), lambda i: (i, 0)), vec, vec],
        out_shape=[SDS((L, Z_AT), BF16), SDS((1, LANE), F32), SDS((1, LANE), F32)],
        compiler_params=_params(("arbitrary",)), name=name)(dqm, dk2, dv2, z, z, cc, ss, wq, wk)


def _at_fwd(qt, kr, vt, *, name):
    L = kr.shape[0]
    G = AT_HEADS // AT_KV
    tq = _tile(L, 384)
    tk = PAD
    nk = L // tk
    R = G * tq
    per = FWD_CHUNKS_PER_STEP if (nk - 1) % FWD_CHUNKS_PER_STEP == 0 else 1

    def body(q_ref, k_ref, v_ref, ob_ref, of_ref, lse_ref, m_scr, acc_scr):
        i = pl.program_id(1)
        qt = jnp.concatenate([q_ref[g] for g in range(G)], axis=1)
        m_scr[...] = jnp.full_like(m_scr, NEG)
        acc_scr[...] = jnp.zeros_like(acc_scr)

        def chunks(c, n, masked):
            start = c * tk if isinstance(c, int) else pl.multiple_of(c * tk, tk)
            st = _nn(k_ref[pl.ds(start, n * tk), :], qt).astype(BF16)
            if masked:
                key = lax.broadcasted_iota(jnp.int32, st.shape, 0)
                st = jnp.where(key >= PAD - N_META, st, NEG)
            m_prev = m_scr[...]
            m_new = jnp.maximum(m_prev, jnp.max(st, axis=0, keepdims=True).astype(F32))
            pt = jnp.exp(st - m_new.astype(BF16))
            acc = jnp.exp(m_prev - m_new) * acc_scr[...]
            for u in range(n):
                acc = acc + _nn(v_ref[0, c + u], pt[u * tk:(u + 1) * tk])
            acc_scr[...] = acc
            m_scr[...] = m_new

        chunks(0, 1, True)

        def loop(t, carry):
            chunks(1 + per * t, per, False)
            return carry

        lax.fori_loop(0, (nk - 1) // per, loop, 0)
        l = acc_scr[pl.ds(AT_HD, 1), :]
        lse = m_scr[...] + jnp.log(l)
        on = acc_scr[0:AT_HD, :] / l
        o = jnp.concatenate([on[:, g * tq:(g + 1) * tq] for g in range(G)], axis=0).T
        rowg = i * tq + lax.broadcasted_iota(jnp.int32, o.shape, 0)
        o = jnp.where(rowg >= PAD - N_META, o, 0.0)
        ob_ref[...] = o.astype(BF16)
        of_ref[...] = o
        for g in range(G):
            lse_ref[g] = lse[:, g * tq:(g + 1) * tq]

    ospec = pl.BlockSpec((tq, G * AT_HD), lambda j, i: (i, j))
    return pl.pallas_call(
        body, grid=(AT_KV, L // tq),
        in_specs=[pl.BlockSpec((G, LANE, tq), lambda j, i: (j, 0, i)), pl.BlockSpec((L, LANE), lambda j, i: (0, 0)),
                  pl.BlockSpec((1, nk, VT_ROWS, tk), lambda j, i: (j, 0, 0, 0))],
        out_specs=[ospec, ospec, pl.BlockSpec((G, 1, tq), lambda j, i: (j, 0, i))],
        out_shape=[SDS((L, AT_W), BF16), SDS((L, AT_W), F32), SDS((AT_HEADS, 1, L), F32)],
        scratch_shapes=[pltpu.VMEM((1, R), F32), pltpu.VMEM((VT_ROWS, R), F32)],
        compiler_params=_params(("parallel", "parallel")), name=name)(qt, kr, vt)


def _at_bwd(qm, qt, kr, krt, vb, do, of, lse, *, name):
    L = kr.shape[0]
    G = AT_HEADS // AT_KV
    tq = _tile(L, 384)
    tk = PAD
    nk = L // tk
    nq = L // tq
    R = G * tq

    def body(qm_ref, q_ref, k_hbm, kt_hbm, v_hbm, do_ref, o_ref, lse_ref, dq_ref, dk_hbm, dv_hbm,
             k_scr, kt_scr, v_scr, dk_scr, dv_scr, dq_scr, sem):
        j, i = pl.program_id(0), pl.program_id(1)

        @pl.when(i == 0)
        def _():
            cps = [pltpu.make_async_copy(k_hbm, k_scr, sem.at[0]), pltpu.make_async_copy(kt_hbm, kt_scr, sem.at[1]),
                   pltpu.make_async_copy(v_hbm.at[j], v_scr, sem.at[2])]
            for cp in cps:
                cp.start()
            dk_scr[...] = jnp.zeros_like(dk_scr)
            dv_scr[...] = jnp.zeros_like(dv_scr)
            for cp in cps:
                cp.wait()

        qt = jnp.concatenate([q_ref[g] for g in range(G)], axis=1)
        rowg = i * tq + lax.broadcasted_iota(jnp.int32, (tq, G * AT_HD), 0)
        dot_all = jnp.where(rowg >= PAD - N_META, do_ref[...], 0.0).T
        ot_all = o_ref[...].T
        dot = jnp.concatenate([dot_all[AT_HD * g:AT_HD * (g + 1)] for g in range(G)], axis=1)
        ot = jnp.concatenate([ot_all[AT_HD * g:AT_HD * (g + 1)] for g in range(G)], axis=1)
        delta = jnp.sum(dot * ot, axis=0, keepdims=True)
        dot128 = jnp.concatenate([dot, jnp.zeros_like(dot)], axis=0)
        dor = dot128.T.astype(BF16)
        dot128 = dot128.astype(BF16)
        qr = qm_ref[...].reshape(R, LANE)
        lse_v = jnp.concatenate([lse_ref[g] for g in range(G)], axis=1)
        dq_scr[...] = jnp.zeros_like(dq_scr)

        def chunk(c, masked):
            start = c * tk if isinstance(c, int) else pl.multiple_of(c * tk, tk)
            k = k_scr[pl.ds(start, tk), :]
            kt = kt_scr[c]
            v = v_scr[pl.ds(start, tk), :]
            st = _nn(k, qt)
            if masked:
                key = lax.broadcasted_iota(jnp.int32, st.shape, 0)
                st = jnp.where(key >= PAD - N_META, st, NEG)
            pt = jnp.exp(st - lse_v)
            dst = (pt * (_nn(v, dot128) - delta)).astype(BF16)
            dq_scr[...] += _nn(kt, dst)
            dk_scr[pl.ds(start, tk), :] += _nn(dst, qr)
            dv_scr[pl.ds(start, tk), :] += _nn(pt.astype(BF16), dor)

        chunk(0, True)

        def loop(c, carry):
            chunk(c, False)
            return carry

        lax.fori_loop(1, nk, loop, 0)
        dq_ref[...] = dq_scr[...].T.reshape(G, tq, LANE)

        @pl.when(i == nq - 1)
        def _():
            ck = pltpu.make_async_copy(dk_scr, dk_hbm.at[j], sem.at[0])
            cv = pltpu.make_async_copy(dv_scr, dv_hbm.at[j], sem.at[1])
            ck.start()
            cv.start()
            ck.wait()
            cv.wait()

    anyspec = pl.BlockSpec(memory_space=pl.ANY)
    ospec = pl.BlockSpec((tq, G * AT_HD), lambda j, i: (i, j))
    return pl.pallas_call(
        body, grid=(AT_KV, nq),
        in_specs=[pl.BlockSpec((G, tq, LANE), lambda j, i: (j, i, 0)), pl.BlockSpec((G, LANE, tq), lambda j, i: (j, 0, i)),
                  anyspec, anyspec, anyspec, ospec, ospec, pl.BlockSpec((G, 1, tq), lambda j, i: (j, 0, i))],
        out_specs=[pl.BlockSpec((G, tq, LANE), lambda j, i: (j, i, 0)), anyspec, anyspec],
        out_shape=[SDS((AT_HEADS, L, LANE), F32), SDS((AT_KV, L, LANE), F32), SDS((AT_KV, L, LANE), F32)],
        scratch_shapes=[pltpu.VMEM((L, LANE), BF16), pltpu.VMEM((nk, LANE, tk), BF16), pltpu.VMEM((L, LANE), BF16),
                        pltpu.VMEM((L, LANE), F32), pltpu.VMEM((L, LANE), F32), pltpu.VMEM((LANE, R), F32),
                        pltpu.SemaphoreType.DMA((3,))],
        compiler_params=_params(("arbitrary", "arbitrary"), VMEM_LARGE), name=name)(qm, qt, kr, krt, vb, do, of, lse)


def _merge_fwd(ya, o8, wua, wubp, z, *, name):
    L = ya.shape[0]
    D = wua.shape[1]
    tm, tn = _tile(L, 1536), 256
    ga, gb = (Z_HG + Z_AT) // tn, (Z_HG + Z_AT + D) // tn

    def body(ya_ref, o8_ref, wa_ref, wb_ref, za_ref, zb_ref, mix_ref):
        pa = _nn(ya_ref[...], wa_ref[...])
        pb = _nn(o8_ref[...], wb_ref[...])
        mix_ref[...] = (_sigmoid(za_ref[...]) * pa + _sigmoid(zb_ref[...]) * pb).astype(BF16)

    return pl.pallas_call(
        body, grid=(D // tn, L // tm),
        in_specs=[pl.BlockSpec((tm, ya.shape[1]), lambda j, i: (i, 0)), pl.BlockSpec((tm, o8.shape[1]), lambda j, i: (i, 0)),
                  pl.BlockSpec((wua.shape[0], tn), lambda j, i: (0, j)), pl.BlockSpec((wubp.shape[0], tn), lambda j, i: (0, j)),
                  pl.BlockSpec((tm, tn), lambda j, i: (i, ga + j)), pl.BlockSpec((tm, tn), lambda j, i: (i, gb + j))],
        out_specs=pl.BlockSpec((tm, tn), lambda j, i: (i, j)), out_shape=SDS((L, D), BF16),
        compiler_params=_params(("parallel", "parallel")), name=name)(ya, o8, wua, wubp, z, z)


def _merge_bwd(dh, wout, ya, o8, wua, wubp, z, *, name):
    L = ya.shape[0]
    D = wua.shape[1]
    tm, tn = _tile(L, 1536), 256
    ga, gb = (Z_HG + Z_AT) // tn, (Z_HG + Z_AT + D) // tn

    def body(dh_ref, wo_ref, ya_ref, o8_ref, wa_ref, wb_ref, za_ref, zb_ref, dpa_ref, dpb_ref, dza_ref, dzb_ref):
        dm = _nt(dh_ref[...].astype(BF16), wo_ref[...])
        pa = _nn(ya_ref[...], wa_ref[...])
        pb = _nn(o8_ref[...], wb_ref[...])
        sa, sb = _sigmoid(za_ref[...]), _sigmoid(zb_ref[...])
        dpa_ref[...] = (dm * sa).astype(BF16)
        dpb_ref[...] = (dm * sb).astype(BF16)
        dza_ref[...] = (dm * pa * sa * (1.0 - sa)).astype(BF16)
        dzb_ref[...] = (dm * pb * sb * (1.0 - sb)).astype(BF16)

    ospec = pl.BlockSpec((tm, tn), lambda j, i: (i, j))
    return pl.pallas_call(
        body, grid=(D // tn, L // tm),
        in_specs=[pl.BlockSpec((tm, D), lambda j, i: (i, 0)), pl.BlockSpec((tn, D), lambda j, i: (j, 0)),
                  pl.BlockSpec((tm, ya.shape[1]), lambda j, i: (i, 0)), pl.BlockSpec((tm, o8.shape[1]), lambda j, i: (i, 0)),
                  pl.BlockSpec((wua.shape[0], tn), lambda j, i: (0, j)), pl.BlockSpec((wubp.shape[0], tn), lambda j, i: (0, j)),
                  pl.BlockSpec((tm, tn), lambda j, i: (i, ga + j)), pl.BlockSpec((tm, tn), lambda j, i: (i, gb + j))],
        out_specs=[ospec] * 4, out_shape=[SDS((L, D), BF16)] * 4,
        compiler_params=_params(("parallel", "parallel")), name=name)(dh, wout, ya, o8, wua, wubp, z, z)


def _loss_head(h, tgt, *, name):
    L, D = h.shape
    tm = PAD

    def body(h_ref, t_ref, dh_ref, ls_ref):
        i = pl.program_id(0)

        @pl.when(i == 0)
        def _():
            ls_ref[...] = jnp.zeros_like(ls_ref)
            dh_ref[...] = jnp.zeros_like(dh_ref)

        @pl.when(i > 0)
        def _():
            e = h_ref[...] - t_ref[...]
            dh_ref[...] = e * (1.0 / D)
            s = jnp.sum(e * e, axis=0, keepdims=True)
            tot = s[:, :LANE]
            for c in range(1, D // LANE):
                tot = tot + s[:, LANE * c:LANE * (c + 1)]
            ls_ref[...] += tot

    return pl.pallas_call(
        body, grid=(L // tm,),
        in_specs=[pl.BlockSpec((tm, D), lambda i: (i, 0)), pl.BlockSpec((tm, D), lambda i: (jnp.maximum(i - 1, 0), 0))],
        out_specs=[pl.BlockSpec((tm, D), lambda i: (i, 0)), pl.BlockSpec((1, LANE), lambda i: (0, 0))],
        out_shape=[SDS((L, D), F32), SDS((1, LANE), F32)],
        compiler_params=_params(("arbitrary",)), name=name)(h, tgt)


def _adamw(w, g, m, v, *, name):
    shape = w.shape
    w2, g2, m2, v2 = [a.reshape(-1, shape[-1]) for a in (w, g, m, v)]
    rows, cols = w2.shape
    tr = _tile(rows, 256, 8)

    def body(w_ref, g_ref, m_ref, v_ref, d_ref, nm_ref, nv_ref):
        g = g_ref[...]
        m = ADAM_B1 * m_ref[...] + (1.0 - ADAM_B1) * g
        v = ADAM_B2 * v_ref[...] + (1.0 - ADAM_B2) * (g * g)
        m_hat = m / (1.0 - ADAM_B1 ** ADAM_STEP)
        v_hat = v / (1.0 - ADAM_B2 ** ADAM_STEP)
        d_ref[...] = -ADAM_LR * (m_hat / (jnp.sqrt(v_hat) + ADAM_EPS) + ADAM_WD * w_ref[...])
        nm_ref[...] = m
        nv_ref[...] = v

    spec = pl.BlockSpec((tr, cols), lambda i: (i, 0))
    outs = pl.pallas_call(
        body, grid=(rows // tr,), in_specs=[spec] * 4, out_specs=[spec] * 3, out_shape=[SDS((rows, cols), F32)] * 3,
        compiler_params=_params(("parallel",)), name=name)(w2, g2, m2, v2)
    return [o.reshape(shape) for o in outs]


def _place():
    return lax.axis_index("x"), lax.axis_index("y"), lax.axis_index("c")


def _allgather_small(v, *, name):
    m_per, n = v.shape

    def body(x_ref, out_ref, send_sems, recv_sems, local_sem):
        x, y, c = _place()
        me, sibling = (x, y, c), (x, y, 1 - c)
        chips = [(1 - x, y), (x, 1 - y), (1 - x, 1 - y)]

        def rows(px, py, pc):
            return out_ref.at[pl.ds((4 * px + 2 * py + pc) * m_per, m_per), :]

        def copy(k, block, to, src=None):
            return pltpu.make_async_remote_copy(
                src_ref=rows(*block) if src is None else src, dst_ref=rows(*block),
                send_sem=send_sems.at[k], recv_sem=recv_sems.at[k], device_id=to, device_id_type=MESH)

        mine = pltpu.make_async_copy(x_ref, rows(*me), local_sem)
        mine.start()
        first = [copy(0, me, sibling, src=x_ref)]
        first += [copy(1 + j, me, (*chip, c), src=x_ref) for j, chip in enumerate(chips)]
        for cp in first:
            cp.start()
        passed = [copy(4 + j, (*chip, c), sibling) for j, chip in enumerate(chips)]
        for j, chip in enumerate(chips):
            copy(1 + j, (*chip, c), me).wait_recv()
            passed[j].start()
        copy(0, sibling, me).wait_recv()
        for j, chip in enumerate(chips):
            copy(4 + j, (*chip, 1 - c), me).wait_recv()
        for cp in first + passed:
            cp.wait_send()
        mine.wait()

    return pl.pallas_call(
        body, out_shape=SDS((8 * m_per, n), v.dtype),
        in_specs=[pl.BlockSpec(memory_space=pltpu.VMEM)], out_specs=pl.BlockSpec(memory_space=pltpu.VMEM),
        scratch_shapes=[pltpu.SemaphoreType.DMA((7,)), pltpu.SemaphoreType.DMA((7,)), pltpu.SemaphoreType.DMA],
        name=name)(v)


def _chips(x, y):
    return [(1 - x, y), (x, 1 - y), (1 - x, 1 - y)]


def _gather_mats(shards, *, name):
    n = len(shards)

    def body(*refs):
        ins, outs = refs[:n], refs[n:2 * n]
        send_sems, recv_sems, fsend_sems, frecv_sems = refs[2 * n:]
        x, y, c = _place()
        s_me, sibling, chips = 2 * x + y, (x, y, 1 - c), _chips(x, y)

        def copy(src, dst, ssem, rsem, to):
            return pltpu.make_async_remote_copy(src_ref=src, dst_ref=dst, send_sem=ssem, recv_sem=rsem,
                                                device_id=to, device_id_type=MESH)

        first = [copy(ins[t].at[c], outs[t].at[s_me, c], send_sems.at[3 * t + k], recv_sems.at[3 * t + k], (*chip, c))
                 for t in range(n) for k, chip in enumerate(chips)]
        for cp in first:
            cp.start()
        passed = []
        for t in range(n):
            for k, chip in enumerate(chips):
                slot = outs[t].at[2 * chip[0] + chip[1], c]
                copy(ins[t].at[c], slot, send_sems.at[3 * t + k], recv_sems.at[3 * t + k], (*chip, c)).wait_recv()
                fw = copy(slot, slot, fsend_sems.at[3 * t + k], frecv_sems.at[3 * t + k], sibling)
                fw.start()
                passed.append(fw)
        for t in range(n):
            for k, chip in enumerate(chips):
                slot = outs[t].at[2 * chip[0] + chip[1], 1 - c]
                copy(slot, slot, fsend_sems.at[3 * t + k], frecv_sems.at[3 * t + k], sibling).wait_recv()
        for cp in first + passed:
            cp.wait_send()

    anyspec = pl.BlockSpec(memory_space=pl.ANY)
    return pl.pallas_call(
        body, out_shape=[SDS((4,) + s.shape, s.dtype) for s in shards], in_specs=[anyspec] * n, out_specs=[anyspec] * n,
        scratch_shapes=[pltpu.SemaphoreType.DMA((3 * n,))] * 4, name=name)(*shards)


def _rs_pair_exchange(gs, *, name):
    n = len(gs)

    def body(*refs):
        ins, outs = refs[:n], refs[n:2 * n]
        send_sems, recv_sems = refs[2 * n:]
        x, y, c = _place()
        cps = [pltpu.make_async_remote_copy(src_ref=ins[t].at[k, 1 - c], dst_ref=outs[t].at[k],
                                            send_sem=send_sems.at[4 * t + k], recv_sem=recv_sems.at[4 * t + k],
                                            device_id=(x, y, 1 - c), device_id_type=MESH)
               for t in range(n) for k in range(4)]
        for cp in cps:
            cp.start()
        for cp in cps:
            cp.wait()

    anyspec = pl.BlockSpec(memory_space=pl.ANY)
    return pl.pallas_call(
        body, out_shape=[SDS((4,) + g.shape[2:], g.dtype) for g in gs], in_specs=[anyspec] * n, out_specs=[anyspec] * n,
        scratch_shapes=[pltpu.SemaphoreType.DMA((4 * n,))] * 2, name=name)(*gs)


def _rs_chip_exchange(parts, *, name):
    n = len(parts)

    def body(*refs):
        ins, outs = refs[:n], refs[n:2 * n]
        send_sems, recv_sems, local_sems = refs[2 * n:]
        x, y, c = _place()
        s_me, chips = 2 * x + y, _chips(x, y)

        def copy(t, k, chip, src_slot, dst_slot):
            return pltpu.make_async_remote_copy(
                src_ref=ins[t].at[src_slot], dst_ref=outs[t].at[dst_slot], send_sem=send_sems.at[3 * t + k],
                recv_sem=recv_sems.at[3 * t + k], device_id=(*chip, c), device_id_type=MESH)

        mine = [pltpu.make_async_copy(ins[t].at[s_me], outs[t].at[s_me], local_sems.at[t]) for t in range(n)]
        for cp in mine:
            cp.start()
        sends = [copy(t, k, chip, 2 * chip[0] + chip[1], s_me) for t in range(n) for k, chip in enumerate(chips)]
        for cp in sends:
            cp.start()
        for t in range(n):
            for k, chip in enumerate(chips):
                copy(t, k, chip, s_me, 2 * chip[0] + chip[1]).wait_recv()
        for cp in sends:
            cp.wait_send()
        for cp in mine:
            cp.wait()

    anyspec = pl.BlockSpec(memory_space=pl.ANY)
    return pl.pallas_call(
        body, out_shape=[SDS(p.shape, p.dtype) for p in parts], in_specs=[anyspec] * n, out_specs=[anyspec] * n,
        scratch_shapes=[pltpu.SemaphoreType.DMA((3 * n,))] * 2 + [pltpu.SemaphoreType.DMA((n,))], name=name)(*parts)


def _rs_pair_share(fulls, *, name):
    n = len(fulls)

    def body(*refs):
        ins, outs = refs[:n], refs[n:2 * n]
        send_sems, recv_sems = refs[2 * n:]
        x, y, c = _place()

        def copy(t, half):
            return pltpu.make_async_remote_copy(src_ref=ins[t].at[c], dst_ref=outs[t].at[half], send_sem=send_sems.at[t],
                                                recv_sem=recv_sems.at[t], device_id=(x, y, 1 - c), device_id_type=MESH)

        sends = [copy(t, c) for t in range(n)]
        for cp in sends:
            cp.start()
        for t in range(n):
            copy(t, 1 - c).wait_recv()
        for cp in sends:
            cp.wait_send()

    anyspec = pl.BlockSpec(memory_space=pl.ANY)
    return pl.pallas_call(
        body, out_shape=[SDS(f.shape, f.dtype) for f in fulls], in_specs=[anyspec] * n, out_specs=[anyspec] * n,
        input_output_aliases={t: t for t in range(n)},
        scratch_shapes=[pltpu.SemaphoreType.DMA((n,))] * 2, name=name)(*fulls)


def _add_half(g, other, c1, *, out_dtype, name):
    _, _, h, cs = g.shape
    tr = _tile(h, 512, 16)

    def body(c_ref, g_ref, o_ref, out_ref):
        out_ref[...] = (g_ref[...] + o_ref[...]).astype(out_dtype)

    spec = pl.BlockSpec((None, tr, cs), lambda k, i, c: (k, i, 0))
    return pl.pallas_call(
        body, out_shape=SDS(other.shape, out_dtype),
        grid_spec=pltpu.PrefetchScalarGridSpec(
            num_scalar_prefetch=1, grid=(4, h // tr),
            in_specs=[pl.BlockSpec((None, None, tr, cs), lambda k, i, c: (k, c[0], i, 0)), spec], out_specs=spec),
        compiler_params=_params(("parallel", "parallel")), name=name)(c1, g, other)


def _sum4(x, c1, *, name):
    n, h, cs = x.shape
    tr = _tile(h, 512, 16)

    def body(c_ref, x_ref, o_ref):
        tot = x_ref[0].astype(F32)
        for s in range(1, n):
            tot = tot + x_ref[s].astype(F32)
        o_ref[...] = tot

    return pl.pallas_call(
        body, out_shape=SDS((2, h, cs), F32),
        grid_spec=pltpu.PrefetchScalarGridSpec(
            num_scalar_prefetch=1, grid=(h // tr,),
            in_specs=[pl.BlockSpec((n, tr, cs), lambda i, c: (0, i, 0))],
            out_specs=pl.BlockSpec((None, tr, cs), lambda i, c: (c[0], i, 0))),
        compiler_params=_params(("parallel",)), name=name)(c1, x)


def _finish_small(gathered, lbf, lbb, *, rows, name):
    r_lbf, r_lbb = rows['lb_f'], rows['lb_b']

    def body(g_ref, lbf_ref, lbb_ref, o_ref, dlf_ref, dlb_ref):
        tot = g_ref[0]
        for s in range(1, 8):
            tot = tot + g_ref[s]
        o_ref[...] = tot
        o_ref[0:1, :] = jnp.broadcast_to(jnp.sum(o_ref[0:1, :], axis=1, keepdims=True), (1, LANE))
        for lb_ref, d_ref, r0 in ((lbf_ref, dlf_ref, r_lbf), (lbb_ref, dlb_ref, r_lbb)):
            for hh in range(HG_HEADS):
                sl = slice(LANE * hh, LANE * (hh + 1))
                l0, l1 = lb_ref[0:1, sl], lb_ref[1:2, sl]
                mx = jnp.maximum(l0, l1)
                e0, e1 = jnp.exp(l0 - mx), jnp.exp(l1 - mx)
                p0 = e0 / (e0 + e1)
                d0 = o_ref[r0 + hh:r0 + hh + 1, :] * p0 * (1.0 - p0)
                d_ref[0:1, sl] = d0
                d_ref[1:2, sl] = -d0

    vm = pl.BlockSpec(memory_space=pltpu.VMEM)
    return pl.pallas_call(
        body, in_specs=[vm, vm, vm], out_specs=[vm, vm, vm],
        out_shape=[SDS(gathered.shape[1:], F32), SDS(lbf.shape, F32), SDS(lbb.shape, F32)], name=name)(gathered, lbf, lbb)


def _local_step(x2, tgt2, meta, W, S):
    T, D = x2.shape
    L = PAD + T
    h0 = jnp.concatenate([jnp.zeros((PAD - N_META, D), F32), meta, x2], axis=0)

    qk0 = Z_HG
    w_in = jnp.concatenate([W['w_in'][:, :qk0], _qk_to_group(W['w_in'][:, qk0:qk0 + AT_W + AT_KVW]),
                            W['w_in'][:, qk0 + AT_W + AT_KVW:]], axis=1)
    cc, ss = _rope_tables(L)
    wq_g, wk_g = _group_vec(S['q_norm']), _group_vec(S['k_norm'])

    def ffn_fwd(h, nw, wg, wu, wd, tag):
        n = _rmsnorm_fwd(h, nw, name=tag + "_norm")
        g, u, a = _ffn4_up(n, wg, wu, name=tag + "_up")
        hn = _ffn4_down(a, wd, h, name=tag + "_down")
        return hn, (n, g, u, a)

    def ffn_bwd(dh, h, nw, wg, wu, wd, saved, tag, split=False):
        n, g, u, a = saved
        dg, du = _ffn4_dact(dh, wd, g, u, name=tag + "_dact")
        dn = _ffn4_dn(dg, du, wg, wu, name=tag + "_dn")
        dwg = _ffn4_dw(n, dg, x_is_rows=True, name=tag + "_dwg")
        dwu = _ffn4_dw(n, du, x_is_rows=True, name=tag + "_dwu")
        dwd = _ffn4_dw(dh, a, x_is_rows=False, alpha=0.5, name=tag + "_dwd")
        *dhp, dnw = _rmsnorm_bwd(h, nw, dn, dh, split=split, name=tag + "_norm_bwd")
        return (dhp if split else dhp[0]), dnw, dwg, dwu, dwd

    h1, sv1 = ffn_fwd(h0, S['ffn1_norm'], W['ffn1_w_gate'], W['ffn1_w_up'], W['ffn1_w_down'], "ffn1")
    um = _rmsnorm_fwd(h1, S['mix_norm'], name="mix_norm")
    z = _mm([(um, w_in)], tm=512, tn=1792, tk=D, name="in_proj")
    (of, sf), (ob, sb) = _hg_fwd(z, S['hg_lb_fwd'], S['hg_lb_bwd'], name="hg_fwd")
    ya = _hg_post_fwd(of, ob, z, S['hg_out_norm'], name="hg_post")
    qm, qt, kr, krt, vb, vt = _at_prep(z, cc, ss, wq_g, wk_g, name="at_prep")
    yb, yb_f32, lse = _at_fwd(qt, kr, vt, name="at_fwd")
    mixed = _merge_fwd(ya, yb, W['w_up_a'], W['w_up_b'], z, name="merge")
    h2 = _mm([(mixed, W['w_out'])], res=h1, tm=512, tn=D, tk=D, name="out_proj")
    h3, sv2 = ffn_fwd(h2, S['ffn2_norm'], W['ffn2_w_gate'], W['ffn2_w_up'], W['ffn2_w_down'], "ffn2")
    dh3, loss_lanes = _loss_head(h3, tgt2, name="loss_head")

    G = {}
    dh2, dn_ffn2, G['ffn2_w_gate'], G['ffn2_w_up'], G['ffn2_w_down'] = ffn_bwd(
        dh3, h2, S['ffn2_norm'], W['ffn2_w_gate'], W['ffn2_w_up'], W['ffn2_w_down'], sv2, "ffn2")
    dpa, dpb, dzga, dzgb = _merge_bwd(dh2, W['w_out'], ya, yb, W['w_up_a'], W['w_up_b'], z, name="merge_bwd")
    G['w_out'] = _mm([(mixed, dh2)], ta=True, tm=D, tn=D, tk=512, name="d_w_out")
    dya = _mm([(dpa, W['w_up_a'])], tb=True, tm=512, tn=HG_W, tk=D, name="d_ya")
    dyb = _mm([(dpb, W['w_up_b'])], tb=True, tm=512, tn=AT_W, tk=D, name="d_yb")
    G['w_up_a'] = _mm([(ya, dpa)], ta=True, tm=HG_W, tn=D, tk=512, name="d_w_up_a")
    G['w_up_b'] = _mm([(yb, dpb)], ta=True, tm=AT_W, tn=D, tk=512, name="d_w_up_b")
    do_hg, dzg, d_hgn = _hg_post_bwd(dya, of, ob, z, S['hg_out_norm'], name="hg_post_bwd")
    dq_f, dv_f, dzf_f, dlb_f = _hg_bwd(z, S['hg_lb_fwd'], do_hg, sf, None, rev=False, name="hg_bwd_f")
    dzq, dzi, dzf_b, dlb_b = _hg_bwd(z, S['hg_lb_bwd'], do_hg, sb, (dq_f, dv_f), rev=True, name="hg_bwd_b")
    dqm, dk2, dv2 = _at_bwd(qm, qt, kr, krt, vb, dyb, yb_f32, lse, name="at_bwd")
    dz_at, dwq_g, dwk_g = _at_prep_bwd(dqm, dk2, dv2, z, cc, ss, wq_g, wk_g, name="at_prep_bwd")
    dz = jnp.concatenate([dzq, dzi, dzf_f, dzf_b, dzg, dz_at, dzga, dzgb], axis=1)
    dum = _mm([(dz, w_in)], tb=True, tm=512, tn=D, tk=1792, name="d_um")
    dw_in_p = _mm([(um, dz)], ta=True, tm=D, tn=1792, tk=512, name="d_w_in")
    G['w_in'] = jnp.concatenate([dw_in_p[:, :qk0], _qk_from_group(dw_in_p[:, qk0:qk0 + AT_W + AT_KVW]),
                                 dw_in_p[:, qk0 + AT_W + AT_KVW:]], axis=1)
    dh1, dn_mix = _rmsnorm_bwd(h1, S['mix_norm'], dum, dh2, name="mix_norm_bwd")
    (grad_x, dmeta), dn_ffn1, G['ffn1_w_gate'], G['ffn1_w_up'], G['ffn1_w_down'] = ffn_bwd(
        dh1, h0, S['ffn1_norm'], W['ffn1_w_gate'], W['ffn1_w_up'], W['ffn1_w_down'], sv1, "ffn1", split=True)

    small_rows = [('loss', loss_lanes), ('ffn1_norm', dn_ffn1.reshape(-1, LANE)), ('mix_norm', dn_mix.reshape(-1, LANE)),
                  ('ffn2_norm', dn_ffn2.reshape(-1, LANE)), ('hg_out_norm', d_hgn.reshape(-1, LANE)),
                  ('lb_f', dlb_f.reshape(-1, LANE)), ('lb_b', dlb_b.reshape(-1, LANE)), ('q_norm', dwq_g), ('k_norm', dwk_g)]
    return grad_x, dmeta, G, small_rows


def kernel(x, meta_tokens, ffn1_norm, ffn1_w_gate, ffn1_w_up, ffn1_w_down, mix_norm, w_in, hg_lb_fwd, hg_lb_bwd, hg_out_norm, q_norm, k_norm, w_up_a, w_up_b, w_out, ffn2_norm, ffn2_w_gate, ffn2_w_up, ffn2_w_down, loss_target, m_meta_tokens, m_ffn1_norm, m_ffn1_w_gate, m_ffn1_w_up, m_ffn1_w_down, m_mix_norm, m_w_in, m_hg_lb_fwd, m_hg_lb_bwd, m_hg_out_norm, m_q_norm, m_k_norm, m_w_up_a, m_w_up_b, m_w_out, m_ffn2_norm, m_ffn2_w_gate, m_ffn2_w_up, m_ffn2_w_down, v_meta_tokens, v_ffn1_norm, v_ffn1_w_gate, v_ffn1_w_up, v_ffn1_w_down, v_mix_norm, v_w_in, v_hg_lb_fwd, v_hg_lb_bwd, v_hg_out_norm, v_q_norm, v_k_norm, v_w_up_a, v_w_up_b, v_w_out, v_ffn2_norm, v_ffn2_w_gate, v_ffn2_w_up, v_ffn2_w_down):
    given = dict(locals())
    w = {n: given[n] for n in WEIGHTS}
    mom = {n: given["m_" + n] for n in WEIGHTS}
    var = {n: given["v_" + n] for n in WEIGHTS}
    c = lax.axis_index("c")
    D = x.shape[-1]

    shapes = {n: w[n].shape[-2:] for n in MATS + ('meta_tokens',)}
    halves = [w[n].astype(BF16).reshape(2, shapes[n][0] // 2, shapes[n][1]) for n in MATS]
    gathered = _gather_mats(halves, name="gather_weights")
    s_me = 2 * lax.axis_index("x") + lax.axis_index("y")
    W = {}
    for n, hv, g4 in zip(MATS, halves, gathered):
        r, cs = shapes[n]
        g4 = lax.dynamic_update_index_in_dim(g4, hv, s_me, 0).reshape(4, r, cs)
        if n in FFN_MATS:
            W[n] = g4
        elif n in ROW_SHARDED:
            W[n] = g4.reshape(4 * r, cs)
        else:
            W[n] = g4.transpose(1, 0, 2).reshape(r, 4 * cs)
    meta_rows = w['meta_tokens'].reshape(-1, LANE)
    mg = _allgather_small(meta_rows, name="gather_meta").reshape(4, 2, N_META, -1)[:, 0]
    meta = mg.transpose(1, 0, 2).reshape(N_META, D)
    S = {n: w[n] for n in SMALLS}

    grad_x, dmeta, G, small_rows = _local_step(x[0], loss_target[0], meta, W, S)
    G['meta_tokens'] = dmeta

    names = MATS + ('meta_tokens',)
    views = []
    for n in names:
        r, cs = shapes[n]
        if n in FFN_MATS:
            g4 = G[n]
        elif n in ROW_SHARDED:
            g4 = G[n].reshape(4, r, cs)
        else:
            g4 = G[n].reshape(r, 4, cs).transpose(1, 0, 2)
        views.append(g4.reshape(4, 2, r // 2, cs))
    c1 = c.astype(jnp.int32).reshape(1)
    from_sibling = _rs_pair_exchange(views, name="rs_pair_exchange")
    parts = [_add_half(v, o, c1, out_dtype=F32 if n == 'meta_tokens' else BF16, name="rs_pair_sum_" + n)
             for n, v, o in zip(names, views, from_sibling)]
    slabs = _rs_chip_exchange(parts, name="rs_chip_exchange")
    reds = [_sum4(s, c1, name="rs_chip_sum_" + n) for n, s in zip(names, slabs)]
    both = _rs_pair_share(reds, name="rs_pair_share")
    grads = {n: b.reshape(w[n].shape) for n, b in zip(names, both)}

    rows, off = {}, 0
    for nme, blk in small_rows:
        rows[nme] = off
        off += blk.shape[0]
    block = jnp.concatenate([blk for _, blk in small_rows], axis=0)
    n_rows = (off + 7) // 8 * 8
    block = jnp.pad(block, ((0, n_rows - off), (0, 0)))
    allsmall = _allgather_small(block, name="gather_small").reshape(8, n_rows, LANE)
    tot, d_lbf, d_lbb = _finish_small(allsmall, w['hg_lb_fwd'], w['hg_lb_bwd'], rows=rows, name="finish_small")
    loss = 0.5 * tot[0, 0] / D

    def small(nme, shape):
        r0 = rows[nme]
        return tot[r0:r0 + shape[-1] // LANE].reshape(shape)

    grads['ffn1_norm'] = small('ffn1_norm', w['ffn1_norm'].shape)
    grads['mix_norm'] = small('mix_norm', w['mix_norm'].shape)
    grads['ffn2_norm'] = small('ffn2_norm', w['ffn2_norm'].shape)
    grads['hg_out_norm'] = small('hg_out_norm', w['hg_out_norm'].shape)
    grads['hg_lb_fwd'] = d_lbf
    grads['hg_lb_bwd'] = d_lbb
    grads['q_norm'] = _ungroup_vec(tot[rows['q_norm']])
    grads['k_norm'] = _ungroup_vec(tot[rows['k_norm']])

    delta, new_m, new_v = {}, {}, {}
    for n in WEIGHTS:
        delta[n], new_m[n], new_v[n] = _adamw(w[n], grads[n], mom[n], var[n], name="adamw_" + n)
    return (loss, grad_x[None], *[grads[n] for n in WEIGHTS], *[delta[n] for n in WEIGHTS],
            *[new_m[n] for n in WEIGHTS], *[new_v[n] for n in WEIGHTS])
```

```python
import numpy as np
import jax
import jax.numpy as jnp
from jax import lax
from jax.experimental import pallas as pl
from jax.experimental.pallas import tpu as pltpu

F32 = jnp.float32
BF16 = jnp.bfloat16
SDS = jax.ShapeDtypeStruct
MESH = pl.DeviceIdType.MESH

EPS = 1e-6
N_META = 16
PAD = 512
LANE = 128
CHUNK = 128
HG_HEADS = 4
HG_W = HG_HEADS * 128
AT_HEADS = 8
AT_KV = 2
AT_HD = 64
AT_W = AT_HEADS * AT_HD
AT_KVW = AT_KV * AT_HD
VT_ROWS = AT_HD + 16
FWD_CHUNKS_PER_STEP = 4
GRID_W = 64
ROPE_THETA = 10000.0
Z_HG = 5 * HG_W
Z_AT = AT_W + 2 * AT_KVW
ADAM_LR, ADAM_B1, ADAM_B2, ADAM_EPS, ADAM_WD, ADAM_STEP = 0.001, 0.9, 0.999, 1e-08, 0.01, 10
VMEM_DEFAULT = 48 * 1024 * 1024
VMEM_LARGE = 60 * 1024 * 1024
NEG = -1e30

MATS = ('ffn1_w_gate', 'ffn1_w_up', 'ffn1_w_down', 'w_in', 'w_up_a', 'w_up_b', 'w_out',
        'ffn2_w_gate', 'ffn2_w_up', 'ffn2_w_down')
ROW_SHARDED = ('ffn1_w_down', 'w_out', 'ffn2_w_down')
FFN_MATS = ('ffn1_w_gate', 'ffn1_w_up', 'ffn1_w_down', 'ffn2_w_gate', 'ffn2_w_up', 'ffn2_w_down')
SMALLS = ('ffn1_norm', 'mix_norm', 'hg_lb_fwd', 'hg_lb_bwd', 'hg_out_norm', 'q_norm', 'k_norm', 'ffn2_norm')
WEIGHTS = ('meta_tokens', 'ffn1_norm', 'ffn1_w_gate', 'ffn1_w_up', 'ffn1_w_down', 'mix_norm', 'w_in', 'hg_lb_fwd',
           'hg_lb_bwd', 'hg_out_norm', 'q_norm', 'k_norm', 'w_up_a', 'w_up_b', 'w_out', 'ffn2_norm', 'ffn2_w_gate',
           'ffn2_w_up', 'ffn2_w_down')


def _params(sem=None, vmem=VMEM_DEFAULT):
    return pltpu.CompilerParams(dimension_semantics=sem, vmem_limit_bytes=vmem)


def _tile(n, pref, q=LANE):
    for d in range(min(pref, n), 0, -1):
        if n % d == 0 and d % q == 0:
            return d
    return n


def _sigmoid(x):
    return 0.5 * jnp.tanh(0.5 * x) + 0.5


def _dot(a, b, dims):
    return lax.dot_general(a, b, (dims, ((), ())), preferred_element_type=F32)


def _nn(a, b):
    return _dot(a, b, ((1,), (0,)))


def _nt(a, b):
    return _dot(a, b, ((1,), (1,)))


def _tn(a, b):
    return _dot(a, b, ((0,), (0,)))


def _split3(x):
    x1 = x.astype(BF16)
    r = x - x1.astype(F32)
    x2 = r.astype(BF16)
    x3 = (r - x2.astype(F32)).astype(BF16)
    return x1, x2, x3


def _exact_left(m01, x):
    x1, x2, x3 = _split3(x)
    return _nn(m01, x1) + _nn(m01, x2) + _nn(m01, x3)


def _exact_right(x, m01):
    x1, x2, x3 = _split3(x)
    return _nn(x1, m01) + _nn(x2, m01) + _nn(x3, m01)


def _mm(pairs, *, name, ta=False, tb=False, out_dtype=F32, tm=512, tn=1024, tk=1024, alpha=1.0, res=None):
    a0, b0 = pairs[0]
    M = a0.shape[1] if ta else a0.shape[0]
    K = a0.shape[0] if ta else a0.shape[1]
    N = b0.shape[0] if tb else b0.shape[1]
    tm, tn, tk = _tile(M, tm), _tile(N, tn), _tile(K, tk)
    nk = K // tk
    npair = len(pairs)
    dims = ((0 if ta else 1,), (1 if tb else 0,))

    def body(*refs):
        ab = refs[:2 * npair]
        pos = 2 * npair
        res_ref = None
        if res is not None:
            res_ref = refs[pos]
            pos += 1
        o_ref = refs[pos]

        def partial_sum():
            tot = None
            for p in range(npair):
                d = _dot(ab[2 * p][...].astype(BF16), ab[2 * p + 1][...].astype(BF16), dims)
                tot = d if tot is None else tot + d
            return tot

        def finish(acc):
            r = acc if alpha == 1.0 else acc * alpha
            if res_ref is not None:
                r = res_ref[...] + r
            o_ref[...] = r.astype(out_dtype)

        if nk == 1:
            finish(partial_sum())
        else:
            acc_ref = refs[pos + 1]
            k = pl.program_id(2)

            @pl.when(k == 0)
            def _():
                acc_ref[...] = jnp.zeros_like(acc_ref)

            acc_ref[...] += partial_sum()

            @pl.when(k == nk - 1)
            def _():
                finish(acc_ref[...])

    a_spec = pl.BlockSpec((tk, tm), lambda j, i, k: (k, i)) if ta else pl.BlockSpec((tm, tk), lambda j, i, k: (i, k))
    b_spec = pl.BlockSpec((tn, tk), lambda j, i, k: (j, k)) if tb else pl.BlockSpec((tk, tn), lambda j, i, k: (k, j))
    o_spec = pl.BlockSpec((tm, tn), lambda j, i, k: (i, j))
    in_specs, args = [], []
    for a, b in pairs:
        in_specs += [a_spec, b_spec]
        args += [a, b]
    if res is not None:
        in_specs.append(o_spec)
        args.append(res)
    return pl.pallas_call(
        body, grid=(N // tn, M // tm, nk), in_specs=in_specs, out_specs=o_spec,
        out_shape=SDS((M, N), out_dtype),
        scratch_shapes=[pltpu.VMEM((tm, tn), F32)] if nk > 1 else [],
        compiler_params=_params(("parallel", "parallel", "arbitrary")), name=name)(*args)


def _rmsnorm_fwd(h, w, *, name):
    L, D = h.shape
    tm = _tile(L, 512)

    def body(h_ref, w_ref, o_ref):
        x = h_ref[...]
        r = lax.rsqrt(jnp.mean(x * x, axis=-1, keepdims=True) + EPS)
        o_ref[...] = (x * r * w_ref[...]).astype(BF16)

    return pl.pallas_call(
        body, grid=(L // tm,),
        in_specs=[pl.BlockSpec((tm, D), lambda i: (i, 0)), pl.BlockSpec((1, D), lambda i: (0, 0))],
        out_specs=pl.BlockSpec((tm, D), lambda i: (i, 0)), out_shape=SDS((L, D), BF16),
        compiler_params=_params(("parallel",)), name=name)(h, w)


def _rmsnorm_bwd(h, w, dn, dres, *, split=False, name):
    L, D = h.shape
    tm = PAD if split else _tile(L, 512)

    def body(h_ref, w_ref, dn_ref, dres_ref, dh_ref, *rest):
        dw_ref = rest[-1]
        i = pl.program_id(0)
        x = h_ref[...]
        r = lax.rsqrt(jnp.mean(x * x, axis=-1, keepdims=True) + EPS)
        xh = x * r
        dn = dn_ref[...]
        dxh = dn * w_ref[...]
        dh = dres_ref[...] + r * (dxh - xh * jnp.mean(dxh * xh, axis=-1, keepdims=True))
        dh_ref[...] = dh

        @pl.when(i == 0)
        def _():
            dw_ref[...] = jnp.zeros_like(dw_ref)
            if split:
                rest[0][...] = dh[PAD - N_META:]

        dw_ref[...] += jnp.sum(dn * xh, axis=0, keepdims=True)

    row = pl.BlockSpec((tm, D), lambda i: (i, 0))
    vec = pl.BlockSpec((1, D), lambda i: (0, 0))
    if split:
        out_specs = [pl.BlockSpec((tm, D), lambda i: (jnp.maximum(i - 1, 0), 0)), pl.BlockSpec((N_META, D), lambda i: (0, 0)), vec]
        out_shape = [SDS((L - PAD, D), F32), SDS((N_META, D), F32), SDS((1, D), F32)]
    else:
        out_specs, out_shape = [row, vec], [SDS((L, D), F32), SDS((1, D), F32)]
    return pl.pallas_call(
        body, grid=(L // tm,), in_specs=[row, vec, row, row], out_specs=out_specs, out_shape=out_shape,
        compiler_params=_params(("arbitrary",)), name=name)(h, w, dn, dres)


def _ffn4_up(n, wg4, wu4, *, name):
    L, D = n.shape
    ns, _, cs = wg4.shape
    tm = _tile(L, 768)

    def body(n_ref, wg_ref, wu_ref, ag_ref, au_ref, a_ref):
        x = n_ref[...]
        g = _nn(x, wg_ref[...])
        u = _nn(x, wu_ref[...])
        sg = _sigmoid(g)
        silu = g * sg
        ag_ref[...] = (u * (sg * (1.0 + g * (1.0 - sg)))).astype(BF16)
        au_ref[...] = silu.astype(BF16)
        a_ref[...] = (silu * u).astype(BF16)

    wspec = pl.BlockSpec((None, D, cs), lambda j, i: (j, 0, 0))
    ospec = pl.BlockSpec((None, tm, cs), lambda j, i: (j, i, 0))
    return pl.pallas_call(
        body, grid=(ns, L // tm),
        in_specs=[pl.BlockSpec((tm, D), lambda j, i: (i, 0)), wspec, wspec], out_specs=[ospec, ospec, ospec],
        out_shape=[SDS((ns, L, cs), BF16), SDS((ns, L, cs), BF16), SDS((ns, L, cs), BF16)],
        compiler_params=_params(("parallel", "parallel")), name=name)(n, wg4, wu4)


def _ffn4_down(a4, wd4, h, *, name):
    ns, L, cs = a4.shape
    D = wd4.shape[2]
    tm = _tile(L, 512)

    def body(a_ref, w_ref, h_ref, o_ref):
        acc = _nn(a_ref[0], w_ref[0])
        for j in range(1, ns):
            acc = acc + _nn(a_ref[j], w_ref[j])
        o_ref[...] = h_ref[...] + 0.5 * acc

    row = pl.BlockSpec((tm, D), lambda i: (i, 0))
    return pl.pallas_call(
        body, grid=(L // tm,),
        in_specs=[pl.BlockSpec((ns, tm, cs), lambda i: (0, i, 0)), pl.BlockSpec((ns, cs, D), lambda i: (0, 0, 0)), row],
        out_specs=row, out_shape=SDS((L, D), F32),
        compiler_params=_params(("parallel",)), name=name)(a4, wd4, h)


def _ffn4_dact(dh, wd4, ag4, au4, *, name):
    L, D = dh.shape
    ns, cs, _ = wd4.shape
    tm = _tile(L, 768)

    def body(dh_ref, wd_ref, ag_ref, au_ref, dg_ref, du_ref):
        da = 0.5 * _nt(dh_ref[...].astype(BF16), wd_ref[...])
        dg_ref[...] = (da * ag_ref[...].astype(F32)).astype(BF16)
        du_ref[...] = (da * au_ref[...].astype(F32)).astype(BF16)

    ospec = pl.BlockSpec((None, tm, cs), lambda j, i: (j, i, 0))
    return pl.pallas_call(
        body, grid=(ns, L // tm),
        in_specs=[pl.BlockSpec((tm, D), lambda j, i: (i, 0)), pl.BlockSpec((None, cs, D), lambda j, i: (j, 0, 0)), ospec, ospec],
        out_specs=[ospec, ospec], out_shape=[SDS((ns, L, cs), BF16), SDS((ns, L, cs), BF16)],
        compiler_params=_params(("parallel", "parallel")), name=name)(dh, wd4, ag4, au4)


def _ffn4_dn(dg4, du4, wg4, wu4, *, name):
    ns, L, cs = dg4.shape
    D = wg4.shape[1]
    tm = _tile(L, 512)

    def body(dg_ref, du_ref, wg_ref, wu_ref, o_ref):
        acc = None
        for j in range(ns):
            t = _nt(dg_ref[j], wg_ref[j]) + _nt(du_ref[j], wu_ref[j])
            acc = t if acc is None else acc + t
        o_ref[...] = acc

    aspec = pl.BlockSpec((ns, tm, cs), lambda i: (0, i, 0))
    wspec = pl.BlockSpec((ns, D, cs), lambda i: (0, 0, 0))
    return pl.pallas_call(
        body, grid=(L // tm,), in_specs=[aspec, aspec, wspec, wspec],
        out_specs=pl.BlockSpec((tm, D), lambda i: (i, 0)), out_shape=SDS((L, D), F32),
        compiler_params=_params(("parallel",), VMEM_LARGE), name=name)(dg4, du4, wg4, wu4)


def _ffn4_dw(x, y4, *, x_is_rows, alpha=1.0, name):
    L, D = x.shape
    ns, _, cs = y4.shape
    tk = _tile(L, 512)
    nk = L // tk
    oshape = (D, cs) if x_is_rows else (cs, D)

    def body(x_ref, y_ref, o_ref):
        k = pl.program_id(0)

        @pl.when(k == 0)
        def _():
            o_ref[...] = jnp.zeros_like(o_ref)

        xb = x_ref[...].astype(BF16)
        if x_is_rows:
            xt = xb.T
            for j in range(ns):
                o_ref[j] += _nn(xt, y_ref[j])
        else:
            for j in range(ns):
                o_ref[j] += _tn(y_ref[j], xb)

        if alpha != 1.0:
            @pl.when(k == nk - 1)
            def _():
                o_ref[...] = o_ref[...] * alpha

    return pl.pallas_call(
        body, grid=(nk,),
        in_specs=[pl.BlockSpec((tk, D), lambda k: (k, 0)), pl.BlockSpec((ns, tk, cs), lambda k: (0, k, 0))],
        out_specs=pl.BlockSpec((ns,) + oshape, lambda k: (0, 0, 0)), out_shape=SDS((ns,) + oshape, F32),
        compiler_params=_params(("arbitrary",)), name=name)(x, y4)


def _hg_mask_table(rev):
    t = np.arange(CHUNK)[:, None]
    s = np.arange(CHUNK)[None, :]
    causal = (s >= t) if rev else (s <= t)
    out = [causal]
    for sh in (6, 5, 4):
        same = (t >> (sh + 1)) == (s >> (sh + 1))
        out.append(same & (((t >> sh) & 1) == (0 if rev else 1)) & (((s >> sh) & 1) == (1 if rev else 0)))
    out.append(((t >> 4) == (s >> 4)) & causal)
    out.append(causal.T)
    return jnp.asarray(np.stack(out).astype(np.float32))


def _hg_masks(mk_ref):
    on = [mk_ref[i] > 0.5 for i in range(5)]
    return on[0], on[1:4], on[4], mk_ref[0].astype(BF16), mk_ref[5].astype(BF16)


def _hg_intra_factors(q, k, b, b_scr, rev):
    b_scr[...] = b
    row = lax.broadcasted_iota(jnp.int32, (CHUNK, LANE), 0)
    out = []
    for sh in (6, 5, 4):
        lb = 1 << sh
        pieces = []
        for p in range(0, CHUNK, 2 * lb):
            r = p + lb if rev else p + lb - 1
            pieces.append(jnp.broadcast_to(b_scr[pl.ds(r, 1), :], (2 * lb, LANE)))
        ref = pieces[0] if len(pieces) == 1 else jnp.concatenate(pieces, axis=0)
        qside = jnp.bitwise_and(jnp.right_shift(row, sh), 1) == (0 if rev else 1)
        d = b - ref
        e = jnp.exp(jnp.minimum(jnp.where(qside, d, -d), 0.0))
        eq = jnp.where(qside, e, 0.0)
        ek = jnp.where(qside, 0.0, e)
        out.append((eq, ek, (q * eq).astype(BF16), (k * ek).astype(BF16)))
    pieces = []
    for a in range(0, CHUNK, 16):
        r = a + (8 if rev else 7)
        pieces.append(jnp.broadcast_to(b_scr[pl.ds(r, 1), :], (16, LANE)))
    ref = jnp.concatenate(pieces, axis=0)
    eq = jnp.exp(jnp.minimum(b - ref, 80.0))
    ek = jnp.exp(jnp.minimum(ref - b, 80.0))
    out.append((eq, ek, (q * eq).astype(BF16), (k * ek).astype(BF16)))
    return out


def _hg_gate(zf, l0, l1, valid):
    mx = jnp.maximum(l0, l1)
    e0, e1 = jnp.exp(l0 - mx), jnp.exp(l1 - mx)
    p0 = e0 / (e0 + e1)
    sg = _sigmoid(-zf)
    k = jnp.where(valid, (1.0 - p0) * sg, 0.0)
    return p0, sg, k, jnp.log(1.0 - k)


def _hg_fwd(z, lbf, lbb, *, name):
    L = z.shape[0]
    nc = L // CHUNK

    def cidx(rev):
        return (lambda j: nc - 1 - j) if rev else (lambda j: j)

    def one(rev, j, hh, zq_ref, zi_ref, zf_ref, lb_ref, mk_ref, o_ref, ssave_ref, st_scr, b_scr):
        _, lmasks, dmask, tri, _ = _hg_masks(mk_ref)
        rowg = cidx(rev)(j) * CHUNK + lax.broadcasted_iota(jnp.int32, (CHUNK, LANE), 0)
        valid = rowg >= PAD - N_META
        last = 0 if rev else CHUNK - 1
        sl = slice(LANE * hh, LANE * (hh + 1))
        zq = zq_ref[:, sl]
        q = zq * _sigmoid(zq)
        v = zi_ref[:, sl].astype(BF16)
        _, _, k, g = _hg_gate(zf_ref[:, sl], lb_ref[0:1, sl], lb_ref[1:2, sl], valid)
        b = _exact_left(tri, g)
        st = st_scr[hh]
        ssave_ref[0, hh] = st
        o = _nt((q * jnp.exp(b)).astype(BF16), st.astype(BF16))
        a = None
        fac = _hg_intra_factors(q, k, b, b_scr, rev)
        for (eq, ek, qq, kk), msk in zip(fac, lmasks + [dmask]):
            t = jnp.where(msk, _nt(qq, kk), 0.0)
            a = t if a is None else a + t
        o_ref[:, sl] = o + _nn(a.astype(BF16), v)
        bl = b_scr[pl.ds(last, 1), :]
        kd = (k * jnp.exp(bl - b)).astype(BF16)
        st_scr[hh] = st * jnp.exp(bl) + _tn(v, kd)

    def body(zqf, zif, zff, lbf_ref, mkf, zqb, zib, zfb, lbb_ref, mkb, of_ref, sf_ref, ob_ref, sb_ref,
             stf_scr, stb_scr, bf_scr, bb_scr):
        j = pl.program_id(0)

        @pl.when(j == 0)
        def _():
            stf_scr[...] = jnp.zeros_like(stf_scr)
            stb_scr[...] = jnp.zeros_like(stb_scr)

        for hh in range(HG_HEADS):
            one(False, j, hh, zqf, zif, zff, lbf_ref, mkf, of_ref, sf_ref, stf_scr, bf_scr)
            one(True, j, hh, zqb, zib, zfb, lbb_ref, mkb, ob_ref, sb_ref, stb_scr, bb_scr)

    def specs(rev):
        zs = lambda col: pl.BlockSpec((CHUNK, HG_W), lambda j: (cidx(rev)(j), col))
        return [zs(0), zs(1), zs(3 if rev else 2), pl.BlockSpec((2, HG_W), lambda j: (0, 0)),
                pl.BlockSpec((6, CHUNK, CHUNK), lambda j: (0, 0, 0))]

    def outs(rev):
        return [pl.BlockSpec((CHUNK, HG_W), lambda j: (cidx(rev)(j), 0)),
                pl.BlockSpec((1, HG_HEADS, LANE, LANE), lambda j: (cidx(rev)(j), 0, 0, 0))]

    shp = [SDS((L, HG_W), F32), SDS((nc, HG_HEADS, LANE, LANE), F32)]
    of, sf, ob, sb = pl.pallas_call(
        body, grid=(nc,), in_specs=specs(False) + specs(True), out_specs=outs(False) + outs(True), out_shape=shp + shp,
        scratch_shapes=[pltpu.VMEM((HG_HEADS, LANE, LANE), F32)] * 2 + [pltpu.VMEM((CHUNK, LANE), F32)] * 2,
        compiler_params=_params(("arbitrary",)), name=name)(
            z, z, z, lbf, _hg_mask_table(False), z, z, z, lbb, _hg_mask_table(True))
    return (of, sf), (ob, sb)


def _hg_bwd(z, lbp, do, ssave, prev, *, rev, name):
    L = z.shape[0]
    nc = L // CHUNK
    fcol = 3 if rev else 2
    final = prev is not None

    def cidx(j):
        return j if rev else nc - 1 - j

    def body(*refs):
        zq_ref, zi_ref, zf_ref, lb_ref, mk_ref, do_ref, ss_ref = refs[:7]
        pos = 7
        if final:
            dqin_ref, dvin_ref = refs[7:9]
            pos = 9
        dq_ref, dv_ref, dzf_ref, dlb_ref, dst_scr, b_scr = refs[pos:pos + 6]
        j = pl.program_id(0)

        @pl.when(j == 0)
        def _():
            dst_scr[...] = jnp.zeros_like(dst_scr)
            dlb_ref[...] = jnp.zeros_like(dlb_ref)

        causal, lmasks, dmask, tri, tri_t = _hg_masks(mk_ref)
        rowg = cidx(j) * CHUNK + lax.broadcasted_iota(jnp.int32, (CHUNK, LANE), 0)
        valid = rowg >= PAD - N_META
        last = 0 if rev else CHUNK - 1
        for hh in range(HG_HEADS):
            sl = slice(LANE * hh, LANE * (hh + 1))
            zq = zq_ref[:, sl]
            sq = _sigmoid(zq)
            q = zq * sq
            v = zi_ref[:, sl].astype(BF16)
            p0, sg, k, g = _hg_gate(zf_ref[:, sl], lb_ref[0:1, sl], lb_ref[1:2, sl], valid)
            b = _exact_left(tri, g)
            dob = do_ref[:, sl].astype(BF16)
            st = ss_ref[0, hh]
            dst = dst_scr[hh]
            stb, dstb = st.astype(BF16), dst.astype(BF16)
            eb = jnp.exp(b)
            qe = (q * eb).astype(BF16)
            fac = _hg_intra_factors(q, k, b, b_scr, rev)
            bl = b_scr[pl.ds(last, 1), :]
            ebl = jnp.exp(bl)
            kde = jnp.exp(bl - b)
            kd = (k * kde).astype(BF16)
            da = jnp.where(causal, _nt(dob, v), 0.0)
            dq = eb * _nn(dob, stb)
            dk_inter = kde * _nn(v, dstb)
            dk = dk_inter
            dv = _nt(kd, dstb)
            a = None
            db = q * dq - k * dk
            for (eq, ek, qq, kk), msk in zip(fac, lmasks + [dmask]):
                t = jnp.where(msk, _nt(qq, kk), 0.0)
                a = t if a is None else a + t
                dal = jnp.where(msk, da, 0.0).astype(BF16)
                mq = _nn(dal, kk)
                mk = _tn(dal, qq)
                dq = dq + eq * mq
                dk = dk + ek * mk
                db = db + (qq.astype(F32) * mq - kk.astype(F32) * mk)
            dv = dv + _tn(a.astype(BF16), dob)
            extra = ebl * jnp.sum(st * dst, axis=0, keepdims=True) + jnp.sum(k * dk_inter, axis=0, keepdims=True)
            dst_scr[hh] = dst * ebl + _tn(dob, qe)
            dg = _exact_left(tri_t, db) + extra
            dk_tot = dk - dg / (1.0 - k)
            dzf_ref[:, sl] = jnp.where(valid, dk_tot * (1.0 - p0) * (-sg * (1.0 - sg)), 0.0).astype(BF16)
            dlb_ref[:, sl] += jnp.sum(jnp.where(valid, -sg * dk_tot, 0.0), axis=0, keepdims=True)
            if final:
                dq_ref[:, sl] = ((dq + dqin_ref[:, sl]) * (sq * (1.0 + zq * (1.0 - sq)))).astype(BF16)
                dv_ref[:, sl] = (dv + dvin_ref[:, sl]).astype(BF16)
            else:
                dq_ref[:, sl] = dq
                dv_ref[:, sl] = dv

    zspec = lambda col: pl.BlockSpec((CHUNK, HG_W), lambda j: (cidx(j), col))
    rspec = pl.BlockSpec((CHUNK, HG_W), lambda j: (cidx(j), 0))
    in_specs = [zspec(0), zspec(1), zspec(fcol), pl.BlockSpec((2, HG_W), lambda j: (0, 0)),
                pl.BlockSpec((6, CHUNK, CHUNK), lambda j: (0, 0, 0)), rspec,
                pl.BlockSpec((1, HG_HEADS, LANE, LANE), lambda j: (cidx(j), 0, 0, 0))]
    args = [z, z, z, lbp, _hg_mask_table(rev), do, ssave]
    if final:
        in_specs += [rspec, rspec]
        args += list(prev)
    odt = BF16 if final else F32
    return pl.pallas_call(
        body, grid=(nc,), in_specs=in_specs,
        out_specs=[rspec, rspec, rspec, pl.BlockSpec((1, HG_W), lambda j: (0, 0))],
        out_shape=[SDS((L, HG_W), odt), SDS((L, HG_W), odt), SDS((L, HG_W), BF16), SDS((1, HG_W), F32)],
        scratch_shapes=[pltpu.VMEM((HG_HEADS, LANE, LANE), F32), pltpu.VMEM((CHUNK, LANE), F32)],
        compiler_params=_params(("arbitrary",)), name=name)(*args)


def _hg_bwd_both(z, lbf, lbb, do, sf, sb, *, name):
    L = z.shape[0]
    nc = L // CHUNK

    def cidx(rev):
        return (lambda j: j) if rev else (lambda j: nc - 1 - j)

    def one(rev, j, hh, zq_ref, zi_ref, zf_ref, lb_ref, mk_ref, do_ref, ss_ref, dq_ref, dv_ref, dzf_ref, dlb_ref,
            dst_scr, b_scr):
        causal, lmasks, dmask, tri, tri_t = _hg_masks(mk_ref)
        rowg = cidx(rev)(j) * CHUNK + lax.broadcasted_iota(jnp.int32, (CHUNK, LANE), 0)
        valid = rowg >= PAD - N_META
        last = 0 if rev else CHUNK - 1
        sl = slice(LANE * hh, LANE * (hh + 1))
        zq = zq_ref[:, sl]
        q = zq * _sigmoid(zq)
        v = zi_ref[:, sl].astype(BF16)
        p0, sg, k, g = _hg_gate(zf_ref[:, sl], lb_ref[0:1, sl], lb_ref[1:2, sl], valid)
        b = _exact_left(tri, g)
        dob = do_ref[:, sl].astype(BF16)
        st = ss_ref[0, hh]
        dst = dst_scr[hh]
        stb, dstb = st.astype(BF16), dst.astype(BF16)
        eb = jnp.exp(b)
        qe = (q * eb).astype(BF16)
        fac = _hg_intra_factors(q, k, b, b_scr, rev)
        bl = b_scr[pl.ds(last, 1), :]
        ebl = jnp.exp(bl)
        kde = jnp.exp(bl - b)
        kd = (k * kde).astype(BF16)
        da = jnp.where(causal, _nt(dob, v), 0.0)
        dq = eb * _nn(dob, stb)
        dk_inter = kde * _nn(v, dstb)
        dk = dk_inter
        dv = _nt(kd, dstb)
        a = None
        db = q * dq - k * dk
        for (eq, ek, qq, kk), msk in zip(fac, lmasks + [dmask]):
            t = jnp.where(msk, _nt(qq, kk), 0.0)
            a = t if a is None else a + t
            dal = jnp.where(msk, da, 0.0).astype(BF16)
            mq = _nn(dal, kk)
            mk = _tn(dal, qq)
            dq = dq + eq * mq
            dk = dk + ek * mk
            db = db + (qq.astype(F32) * mq - kk.astype(F32) * mk)
        dv = dv + _tn(a.astype(BF16), dob)
        extra = ebl * jnp.sum(st * dst, axis=0, keepdims=True) + jnp.sum(k * dk_inter, axis=0, keepdims=True)
        dst_scr[hh] = dst * ebl + _tn(dob, qe)
        dg = _exact_left(tri_t, db) + extra
        dk_tot = dk - dg / (1.0 - k)
        dzf_ref[:, sl] = jnp.where(valid, dk_tot * (1.0 - p0) * (-sg * (1.0 - sg)), 0.0).astype(BF16)
        dlb_ref[:, sl] += jnp.sum(jnp.where(valid, -sg * dk_tot, 0.0), axis=0, keepdims=True)
        dq_ref[:, sl] = dq
        dv_ref[:, sl] = dv

    def body(*refs):
        fin, bin_ = refs[0:7], refs[7:14]
        fout, bout = refs[14:18], refs[18:22]
        dstf, dstb_, bf_scr, bb_scr = refs[22:26]
        j = pl.program_id(0)

        @pl.when(j == 0)
        def _():
            dstf[...] = jnp.zeros_like(dstf)
            dstb_[...] = jnp.zeros_like(dstb_)
            fout[3][...] = jnp.zeros_like(fout[3])
            bout[3][...] = jnp.zeros_like(bout[3])

        for hh in range(HG_HEADS):
            one(False, j, hh, *fin, *fout, dstf, bf_scr)
            one(True, j, hh, *bin_, *bout, dstb_, bb_scr)

    def specs(rev):
        zs = lambda col: pl.BlockSpec((CHUNK, HG_W), lambda j: (cidx(rev)(j), col))
        return [zs(0), zs(1), zs(3 if rev else 2), pl.BlockSpec((2, HG_W), lambda j: (0, 0)),
                pl.BlockSpec((6, CHUNK, CHUNK), lambda j: (0, 0, 0)),
                pl.BlockSpec((CHUNK, HG_W), lambda j: (cidx(rev)(j), 0)),
                pl.BlockSpec((1, HG_HEADS, LANE, LANE), lambda j: (cidx(rev)(j), 0, 0, 0))]

    def outs(rev):
        r = pl.BlockSpec((CHUNK, HG_W), lambda j: (cidx(rev)(j), 0))
        return [r, r, r, pl.BlockSpec((1, HG_W), lambda j: (0, 0))]

    shp = [SDS((L, HG_W), F32), SDS((L, HG_W), F32), SDS((L, HG_W), BF16), SDS((1, HG_W), F32)]
    res = pl.pallas_call(
        body, grid=(nc,), in_specs=specs(False) + specs(True), out_specs=outs(False) + outs(True), out_shape=shp + shp,
        scratch_shapes=[pltpu.VMEM((HG_HEADS, LANE, LANE), F32)] * 2 + [pltpu.VMEM((CHUNK, LANE), F32)] * 2,
        compiler_params=_params(("arbitrary",)), name=name)(
            z, z, z, lbf, _hg_mask_table(False), do, sf, z, z, z, lbb, _hg_mask_table(True), do, sb)
    return res[:4], res[4:]


def _hg_combine(z, dq_f, dq_b, dv_f, dv_b, *, name):
    L = z.shape[0]
    tm = _tile(L, 512)

    def body(zq_ref, a_ref, b_ref, c_ref, d_ref, dzq_ref, dzi_ref):
        zq = zq_ref[...]
        sq = _sigmoid(zq)
        dzq_ref[...] = ((a_ref[...] + b_ref[...]) * (sq * (1.0 + zq * (1.0 - sq)))).astype(BF16)
        dzi_ref[...] = (c_ref[...] + d_ref[...]).astype(BF16)

    row = pl.BlockSpec((tm, HG_W), lambda i: (i, 0))
    return pl.pallas_call(
        body, grid=(L // tm,), in_specs=[row] * 5, out_specs=[row, row],
        out_shape=[SDS((L, HG_W), BF16), SDS((L, HG_W), BF16)],
        compiler_params=_params(("parallel",)), name=name)(z, dq_f, dq_b, dv_f, dv_b)


def _hg_post_fwd(of, ob, z, w, *, name):
    L = of.shape[0]
    tm = _tile(L, 512)

    def body(of_ref, ob_ref, zg_ref, w_ref, y_ref):
        for hh in range(HG_HEADS):
            sl = slice(LANE * hh, LANE * (hh + 1))
            o = of_ref[:, sl] + ob_ref[:, sl]
            r = lax.rsqrt(jnp.mean(o * o, axis=-1, keepdims=True) + EPS)
            zg = zg_ref[:, sl]
            y_ref[:, sl] = (o * r * w_ref[:, sl] * (zg * _sigmoid(zg))).astype(BF16)

    row = pl.BlockSpec((tm, HG_W), lambda i: (i, 0))
    return pl.pallas_call(
        body, grid=(L // tm,),
        in_specs=[row, row, pl.BlockSpec((tm, HG_W), lambda i: (i, 4)), pl.BlockSpec((1, HG_W), lambda i: (0, 0))],
        out_specs=row, out_shape=SDS((L, HG_W), BF16),
        compiler_params=_params(("parallel",)), name=name)(of, ob, z, w)


def _hg_post_bwd(dy, of, ob, z, w, *, name):
    L = of.shape[0]
    tm = _tile(L, 512)

    def body(dy_ref, of_ref, ob_ref, zg_ref, w_ref, do_ref, dzg_ref, dw_ref):
        @pl.when(pl.program_id(0) == 0)
        def _():
            dw_ref[...] = jnp.zeros_like(dw_ref)

        for hh in range(HG_HEADS):
            sl = slice(LANE * hh, LANE * (hh + 1))
            o = of_ref[:, sl] + ob_ref[:, sl]
            r = lax.rsqrt(jnp.mean(o * o, axis=-1, keepdims=True) + EPS)
            xh = o * r
            zg = zg_ref[:, sl]
            sg = _sigmoid(zg)
            w = w_ref[:, sl]
            dy = dy_ref[:, sl]
            dys = dy * (zg * sg)
            dzg_ref[:, sl] = (dy * xh * w * (sg * (1.0 + zg * (1.0 - sg)))).astype(BF16)
            dw_ref[:, sl] += jnp.sum(dys * xh, axis=0, keepdims=True)
            dxh = dys * w
            do_ref[:, sl] = r * (dxh - xh * jnp.mean(dxh * xh, axis=-1, keepdims=True))

    row = pl.BlockSpec((tm, HG_W), lambda i: (i, 0))
    vec = pl.BlockSpec((1, HG_W), lambda i: (0, 0))
    return pl.pallas_call(
        body, grid=(L // tm,),
        in_specs=[row, row, row, pl.BlockSpec((tm, HG_W), lambda i: (i, 4)), vec],
        out_specs=[row, row, vec],
        out_shape=[SDS((L, HG_W), F32), SDS((L, HG_W), BF16), SDS((1, HG_W), F32)],
        compiler_params=_params(("arbitrary",)), name=name)(dy, of, ob, z, w)


N_GROUPS = (AT_HEADS + AT_KV) // 2


def _qk_to_group(wqk):
    d = wqk.shape[0]
    return wqk.reshape(d, N_GROUPS, 2, AT_HD // 2, 2).transpose(0, 1, 4, 2, 3).reshape(d, N_GROUPS * LANE)


def _qk_from_group(wqk):
    d = wqk.shape[0]
    return wqk.reshape(d, N_GROUPS, 2, 2, AT_HD // 2).transpose(0, 1, 3, 4, 2).reshape(d, N_GROUPS * LANE)


def _group_vec(w64):
    halves = w64.reshape(AT_HD // 2, 2).T
    return jnp.broadcast_to(halves[:, None, :], (2, 2, AT_HD // 2)).reshape(1, LANE)


def _ungroup_vec(w128):
    w = w128.reshape(2, 2, 32).sum(axis=1)
    return w.T.reshape(1, AT_HD)


def _rope_tables(L):
    n_real = L - PAD
    t = np.arange(n_real)
    row = np.concatenate([np.zeros(PAD), t // GRID_W]).astype(np.float32)
    col = np.concatenate([np.zeros(PAD), t % GRID_W]).astype(np.float32)
    inv = jnp.asarray(ROPE_THETA, F32) ** (-jnp.arange(0, AT_HD // 2, 2, dtype=F32) / (AT_HD // 2))
    ang = jnp.concatenate([jnp.asarray(row)[:, None] * inv, jnp.asarray(col)[:, None] * inv], axis=-1)
    cos, sin = jnp.cos(ang), jnp.sin(ang)
    cc = jnp.tile(cos, (1, 4))
    ss = jnp.concatenate([-sin, -sin, sin, sin], axis=1)
    return cc, ss


def _seg_matrix():
    a = lax.broadcasted_iota(jnp.int32, (LANE, LANE), 0)
    b = lax.broadcasted_iota(jnp.int32, (LANE, LANE), 1)
    same = jnp.bitwise_and(jnp.right_shift(a, 5), 1) == jnp.bitwise_and(jnp.right_shift(b, 5), 1)
    return jnp.where(same, 1.0, 0.0).astype(BF16)


def _slot_mask(shape, hp):
    lane = lax.broadcasted_iota(jnp.int32, shape, 1)
    return jnp.bitwise_and(jnp.right_shift(lane, 5), 1) == hp


def _at_prep(z, cc, ss, wq, wk, *, name):
    L = z.shape[0]
    tm = PAD
    qcol = Z_HG // AT_W
    kvcol = (Z_HG + AT_W) // (2 * LANE)

    def body(zq_ref, zkv_ref, cc_ref, ss_ref, wq_ref, wk_ref, qm_ref, qt_ref, kr_ref, krt_ref, vb_ref, vt_ref):
        seg = _seg_matrix()
        cc, ss = cc_ref[...], ss_ref[...]

        def normrope(x, w):
            r = lax.rsqrt(_exact_right(x * x, seg) * (1.0 / AT_HD) + EPS)
            y = x * r * w
            return y * cc + pltpu.roll(y, 64, 1) * ss

        for g in range(AT_HEADS // 2):
            o = normrope(zq_ref[:, LANE * g:LANE * (g + 1)], wq_ref[...]) * (AT_HD ** -0.5)
            for hp in range(2):
                h = 2 * g + hp
                tgt = h // (AT_HEADS // AT_KV)
                xm = jnp.where(_slot_mask(o.shape, hp), o, 0.0)
                if tgt != hp:
                    xm = pltpu.roll(xm, 32 if tgt == 1 else 96, 1)
                qm_ref[h] = xm.astype(BF16)
                qt_ref[h] = xm.T.astype(BF16)
        kr = normrope(zkv_ref[:, :LANE], wk_ref[...])
        kr_ref[...] = kr.astype(BF16)
        krt_ref[0] = kr.T.astype(BF16)
        v = zkv_ref[:, LANE:]
        low = lax.broadcasted_iota(jnp.int32, v.shape, 1) < AT_HD
        vb_ref[0] = jnp.where(low, v, 0.0).astype(BF16)
        vb_ref[1] = jnp.where(low, pltpu.roll(v, AT_HD, 1), 0.0).astype(BF16)
        vt = v.T.astype(BF16)
        ones = jnp.ones((VT_ROWS - AT_HD, tm), BF16)
        for j in range(AT_KV):
            vt_ref[j, 0, 0:AT_HD, :] = vt[AT_HD * j:AT_HD * (j + 1)]
            vt_ref[j, 0, AT_HD:VT_ROWS, :] = ones

    tab = pl.BlockSpec((tm, LANE), lambda i: (i, 0))
    vec = pl.BlockSpec((1, LANE), lambda i: (0, 0))
    nt = L // tm
    return pl.pallas_call(
        body, grid=(nt,),
        in_specs=[pl.BlockSpec((tm, AT_W), lambda i: (i, qcol)), pl.BlockSpec((tm, 2 * LANE), lambda i: (i, kvcol)),
                  tab, tab, vec, vec],
        out_specs=[pl.BlockSpec((AT_HEADS, tm, LANE), lambda i: (0, i, 0)),
                   pl.BlockSpec((AT_HEADS, LANE, tm), lambda i: (0, 0, i)), tab,
                   pl.BlockSpec((1, LANE, tm), lambda i: (i, 0, 0)),
                   pl.BlockSpec((AT_KV, tm, LANE), lambda i: (0, i, 0)),
                   pl.BlockSpec((AT_KV, 1, VT_ROWS, tm), lambda i: (0, i, 0, 0))],
        out_shape=[SDS((AT_HEADS, L, LANE), BF16), SDS((AT_HEADS, LANE, L), BF16), SDS((L, LANE), BF16),
                   SDS((nt, LANE, tm), BF16), SDS((AT_KV, L, LANE), BF16), SDS((AT_KV, nt, VT_ROWS, tm), BF16)],
        compiler_params=_params(("parallel",)), name=name)(z, z, cc, ss, wq, wk)


def _at_prep_bwd(dqm, dk2, dv2, z, cc, ss, wq, wk, *, name):
    L = z.shape[0]
    tm = PAD
    qcol = Z_HG // AT_W
    kvcol = (Z_HG + AT_W) // (2 * LANE)

    def body(dqm_ref, dk2_ref, dv2_ref, zq_ref, zkv_ref, cc_ref, ss_ref, wq_ref, wk_ref, dz_ref, dwq_ref, dwk_ref):
        @pl.when(pl.program_id(0) == 0)
        def _():
            dwq_ref[...] = jnp.zeros_like(dwq_ref)
            dwk_ref[...] = jnp.zeros_like(dwk_ref)

        seg = _seg_matrix()
        cc, ss = cc_ref[...], ss_ref[...]

        def back(x, w, do):
            dy = do * cc + pltpu.roll(do * ss, 64, 1)
            r = lax.rsqrt(_exact_right(x * x, seg) * (1.0 / AT_HD) + EPS)
            xh = x * r
            dxh = dy * w
            dx = r * (dxh - xh * (_exact_right(dxh * xh, seg) * (1.0 / AT_HD)))
            return dx, jnp.sum(dy * xh, axis=0, keepdims=True)

        for g in range(AT_HEADS // 2):
            do = None
            for hp in range(2):
                h = 2 * g + hp
                tgt = h // (AT_HEADS // AT_KV)
                d = jnp.where(_slot_mask((tm, LANE), tgt), dqm_ref[h], 0.0)
                if tgt != hp:
                    d = pltpu.roll(d, 96 if tgt == 1 else 32, 1)
                do = d if do is None else do + d
            dx, dw = back(zq_ref[:, LANE * g:LANE * (g + 1)], wq_ref[...], do * (AT_HD ** -0.5))
            dz_ref[:, LANE * g:LANE * (g + 1)] = dx.astype(BF16)
            dwq_ref[...] += dw
        dx, dw = back(zkv_ref[:, :LANE], wk_ref[...], dk2_ref[0] + dk2_ref[1])
        dz_ref[:, AT_W:AT_W + LANE] = dx.astype(BF16)
        dwk_ref[...] += dw
        dv0 = dv2_ref[0]
        low = lax.broadcasted_iota(jnp.int32, dv0.shape, 1) < AT_HD
        dz_ref[:, AT_W + LANE:] = jnp.where(low, dv0, pltpu.roll(dv2_ref[1], AT_HD, 1)).astype(BF16)

    tab = pl.BlockSpec((tm, LANE), lambda i: (i, 0))
    vec = pl.BlockSpec((1, LANE), lambda i: (0, 0))
    two = pl.BlockSpec((AT_KV, tm, LANE), lambda i: (0, i, 0))
    return pl.pallas_call(
        body, grid=(L // tm,),
        in_specs=[pl.BlockSpec((AT_HEADS, tm, LANE), lambda i: (0, i, 0)), two, two,
                  pl.BlockSpec((tm, AT_W), lambda i: (i, qcol)), pl.BlockSpec((tm, 2 * LANE), lambda i: (i, kvcol)),
                  tab, tab, vec, vec],
        out_specs=[pl.BlockSpec((tm, Z_AT), lambda i: (i, 0)), vec, vec],
        out_shape=[SDS((L, Z_AT), BF16), SDS((1, LANE), F32), SDS((1, LANE), F32)],
        compiler_params=_params(("arbitrary",)), name=name)(dqm, dk2, dv2, z, z, cc, ss, wq, wk)


def _at_fwd(qt, kr, vt, *, name):
    L = kr.shape[0]
    G = AT_HEADS // AT_KV
    tq = _tile(L, 384)
    tk = PAD
    nk = L // tk
    R = G * tq
    per = FWD_CHUNKS_PER_STEP if (nk - 1) % FWD_CHUNKS_PER_STEP == 0 else 1

    def body(q_ref, k_ref, v_ref, ob_ref, of_ref, lse_ref, m_scr, acc_scr):
        i = pl.program_id(1)
        qt = jnp.concatenate([q_ref[g] for g in range(G)], axis=1)
        m_scr[...] = jnp.full_like(m_scr, NEG)
        acc_scr[...] = jnp.zeros_like(acc_scr)

        def chunks(c, n, masked):
            start = c * tk if isinstance(c, int) else pl.multiple_of(c * tk, tk)
            st = _nn(k_ref[pl.ds(start, n * tk), :], qt).astype(BF16)
            if masked:
                key = lax.broadcasted_iota(jnp.int32, st.shape, 0)
                st = jnp.where(key >= PAD - N_META, st, NEG)
            m_prev = m_scr[...]
            m_new = jnp.maximum(m_prev, jnp.max(st, axis=0, keepdims=True).astype(F32))
            pt = jnp.exp(st - m_new.astype(BF16))
            acc = jnp.exp(m_prev - m_new) * acc_scr[...]
            for u in range(n):
                acc = acc + _nn(v_ref[0, c + u], pt[u * tk:(u + 1) * tk])
            acc_scr[...] = acc
            m_scr[...] = m_new

        chunks(0, 1, True)

        def loop(t, carry):
            chunks(1 + per * t, per, False)
            return carry

        lax.fori_loop(0, (nk - 1) // per, loop, 0)
        l = acc_scr[pl.ds(AT_HD, 1), :]
        lse = m_scr[...] + jnp.log(l)
        on = acc_scr[0:AT_HD, :] / l
        o = jnp.concatenate([on[:, g * tq:(g + 1) * tq] for g in range(G)], axis=0).T
        rowg = i * tq + lax.broadcasted_iota(jnp.int32, o.shape, 0)
        o = jnp.where(rowg >= PAD - N_META, o, 0.0)
        ob_ref[...] = o.astype(BF16)
        of_ref[...] = o
        for g in range(G):
            lse_ref[g] = lse[:, g * tq:(g + 1) * tq]

    ospec = pl.BlockSpec((tq, G * AT_HD), lambda j, i: (i, j))
    return pl.pallas_call(
        body, grid=(AT_KV, L // tq),
        in_specs=[pl.BlockSpec((G, LANE, tq), lambda j, i: (j, 0, i)), pl.BlockSpec((L, LANE), lambda j, i: (0, 0)),
                  pl.BlockSpec((1, nk, VT_ROWS, tk), lambda j, i: (j, 0, 0, 0))],
        out_specs=[ospec, ospec, pl.BlockSpec((G, 1, tq), lambda j, i: (j, 0, i))],
        out_shape=[SDS((L, AT_W), BF16), SDS((L, AT_W), F32), SDS((AT_HEADS, 1, L), F32)],
        scratch_shapes=[pltpu.VMEM((1, R), F32), pltpu.VMEM((VT_ROWS, R), F32)],
        compiler_params=_params(("parallel", "parallel")), name=name)(qt, kr, vt)


def _at_bwd(qm, qt, kr, krt, vb, do, of, lse, *, name):
    L = kr.shape[0]
    G = AT_HEADS // AT_KV
    tq = _tile(L, 384)
    tk = PAD
    nk = L // tk
    nq = L // tq
    R = G * tq

    def body(qm_ref, q_ref, k_hbm, kt_hbm, v_hbm, do_ref, o_ref, lse_ref, dq_ref, dk_hbm, dv_hbm,
             k_scr, kt_scr, v_scr, dk_scr, dv_scr, dq_scr, sem):
        j, i = pl.program_id(0), pl.program_id(1)

        @pl.when(i == 0)
        def _():
            cps = [pltpu.make_async_copy(k_hbm, k_scr, sem.at[0]), pltpu.make_async_copy(kt_hbm, kt_scr, sem.at[1]),
                   pltpu.make_async_copy(v_hbm.at[j], v_scr, sem.at[2])]
            for cp in cps:
                cp.start()
            dk_scr[...] = jnp.zeros_like(dk_scr)
            dv_scr[...] = jnp.zeros_like(dv_scr)
            for cp in cps:
                cp.wait()

        qt = jnp.concatenate([q_ref[g] for g in range(G)], axis=1)
        rowg = i * tq + lax.broadcasted_iota(jnp.int32, (tq, G * AT_HD), 0)
        dot_all = jnp.where(rowg >= PAD - N_META, do_ref[...], 0.0).T
        ot_all = o_ref[...].T
        dot = jnp.concatenate([dot_all[AT_HD * g:AT_HD * (g + 1)] for g in range(G)], axis=1)
        ot = jnp.concatenate([ot_all[AT_HD * g:AT_HD * (g + 1)] for g in range(G)], axis=1)
        delta = jnp.sum(dot * ot, axis=0, keepdims=True)
        dot128 = jnp.concatenate([dot, jnp.zeros_like(dot)], axis=0)
        dor = dot128.T.astype(BF16)
        dot128 = dot128.astype(BF16)
        qr = qm_ref[...].reshape(R, LANE)
        lse_v = jnp.concatenate([lse_ref[g] for g in range(G)], axis=1)
        dq_scr[...] = jnp.zeros_like(dq_scr)

        def chunk(c, masked):
            start = c * tk if isinstance(c, int) else pl.multiple_of(c * tk, tk)
            k = k_scr[pl.ds(start, tk), :]
            kt = kt_scr[c]
            v = v_scr[pl.ds(start, tk), :]
            st = _nn(k, qt)
            if masked:
                key = lax.broadcasted_iota(jnp.int32, st.shape, 0)
                st = jnp.where(key >= PAD - N_META, st, NEG)
            pt = jnp.exp(st - lse_v)
            dst = (pt * (_nn(v, dot128) - delta)).astype(BF16)
            dq_scr[...] += _nn(kt, dst)
            dk_scr[pl.ds(start, tk), :] += _nn(dst, qr)
            dv_scr[pl.ds(start, tk), :] += _nn(pt.astype(BF16), dor)

        chunk(0, True)

        def loop(c, carry):
            chunk(c, False)
            return carry

        lax.fori_loop(1, nk, loop, 0)
        dq_ref[...] = dq_scr[...].T.reshape(G, tq, LANE)

        @pl.when(i == nq - 1)
        def _():
            ck = pltpu.make_async_copy(dk_scr, dk_hbm.at[j], sem.at[0])
            cv = pltpu.make_async_copy(dv_scr, dv_hbm.at[j], sem.at[1])
            ck.start()
            cv.start()
            ck.wait()
            cv.wait()

    anyspec = pl.BlockSpec(memory_space=pl.ANY)
    ospec = pl.BlockSpec((tq, G * AT_HD), lambda j, i: (i, j))
    return pl.pallas_call(
        body, grid=(AT_KV, nq),
        in_specs=[pl.BlockSpec((G, tq, LANE), lambda j, i: (j, i, 0)), pl.BlockSpec((G, LANE, tq), lambda j, i: (j, 0, i)),
                  anyspec, anyspec, anyspec, ospec, ospec, pl.BlockSpec((G, 1, tq), lambda j, i: (j, 0, i))],
        out_specs=[pl.BlockSpec((G, tq, LANE), lambda j, i: (j, i, 0)), anyspec, anyspec],
        out_shape=[SDS((AT_HEADS, L, LANE), F32), SDS((AT_KV, L, LANE), F32), SDS((AT_KV, L, LANE), F32)],
        scratch_shapes=[pltpu.VMEM((L, LANE), BF16), pltpu.VMEM((nk, LANE, tk), BF16), pltpu.VMEM((L, LANE), BF16),
                        pltpu.VMEM((L, LANE), F32), pltpu.VMEM((L, LANE), F32), pltpu.VMEM((LANE, R), F32),
                        pltpu.SemaphoreType.DMA((3,))],
        compiler_params=_params(("arbitrary", "arbitrary"), VMEM_LARGE), name=name)(qm, qt, kr, krt, vb, do, of, lse)


def _merge_fwd(ya, o8, wua, wubp, z, *, name):
    L = ya.shape[0]
    D = wua.shape[1]
    tm, tn = _tile(L, 1536), 256
    ga, gb = (Z_HG + Z_AT) // tn, (Z_HG + Z_AT + D) // tn

    def body(ya_ref, o8_ref, wa_ref, wb_ref, za_ref, zb_ref, mix_ref):
        pa = _nn(ya_ref[...], wa_ref[...])
        pb = _nn(o8_ref[...], wb_ref[...])
        mix_ref[...] = (_sigmoid(za_ref[...]) * pa + _sigmoid(zb_ref[...]) * pb).astype(BF16)

    return pl.pallas_call(
        body, grid=(D // tn, L // tm),
        in_specs=[pl.BlockSpec((tm, ya.shape[1]), lambda j, i: (i, 0)), pl.BlockSpec((tm, o8.shape[1]), lambda j, i: (i, 0)),
                  pl.BlockSpec((wua.shape[0], tn), lambda j, i: (0, j)), pl.BlockSpec((wubp.shape[0], tn), lambda j, i: (0, j)),
                  pl.BlockSpec((tm, tn), lambda j, i: (i, ga + j)), pl.BlockSpec((tm, tn), lambda j, i: (i, gb + j))],
        out_specs=pl.BlockSpec((tm, tn), lambda j, i: (i, j)), out_shape=SDS((L, D), BF16),
        compiler_params=_params(("parallel", "parallel")), name=name)(ya, o8, wua, wubp, z, z)


def _merge_bwd(dh, wout, ya, o8, wua, wubp, z, *, name):
    L = ya.shape[0]
    D = wua.shape[1]
    tm, tn = _tile(L, 1536), 256
    ga, gb = (Z_HG + Z_AT) // tn, (Z_HG + Z_AT + D) // tn

    def body(dh_ref, wo_ref, ya_ref, o8_ref, wa_ref, wb_ref, za_ref, zb_ref, dpa_ref, dpb_ref, dza_ref, dzb_ref):
        dm = _nt(dh_ref[...].astype(BF16), wo_ref[...])
        pa = _nn(ya_ref[...], wa_ref[...])
        pb = _nn(o8_ref[...], wb_ref[...])
        sa, sb = _sigmoid(za_ref[...]), _sigmoid(zb_ref[...])
        dpa_ref[...] = (dm * sa).astype(BF16)
        dpb_ref[...] = (dm * sb).astype(BF16)
        dza_ref[...] = (dm * pa * sa * (1.0 - sa)).astype(BF16)
        dzb_ref[...] = (dm * pb * sb * (1.0 - sb)).astype(BF16)

    ospec = pl.BlockSpec((tm, tn), lambda j, i: (i, j))
    return pl.pallas_call(
        body, grid=(D // tn, L // tm),
        in_specs=[pl.BlockSpec((tm, D), lambda j, i: (i, 0)), pl.BlockSpec((tn, D), lambda j, i: (j, 0)),
                  pl.BlockSpec((tm, ya.shape[1]), lambda j, i: (i, 0)), pl.BlockSpec((tm, o8.shape[1]), lambda j, i: (i, 0)),
                  pl.BlockSpec((wua.shape[0], tn), lambda j, i: (0, j)), pl.BlockSpec((wubp.shape[0], tn), lambda j, i: (0, j)),
                  pl.BlockSpec((tm, tn), lambda j, i: (i, ga + j)), pl.BlockSpec((tm, tn), lambda j, i: (i, gb + j))],
        out_specs=[ospec] * 4, out_shape=[SDS((L, D), BF16)] * 4,
        compiler_params=_params(("parallel", "parallel")), name=name)(dh, wout, ya, o8, wua, wubp, z, z)


def _loss_head(h, tgt, *, name):
    L, D = h.shape
    tm = PAD

    def body(h_ref, t_ref, dh_ref, ls_ref):
        i = pl.program_id(0)

        @pl.when(i == 0)
        def _():
            ls_ref[...] = jnp.zeros_like(ls_ref)
            dh_ref[...] = jnp.zeros_like(dh_ref)

        @pl.when(i > 0)
        def _():
            e = h_ref[...] - t_ref[...]
            dh_ref[...] = e * (1.0 / D)
            s = jnp.sum(e * e, axis=0, keepdims=True)
            tot = s[:, :LANE]
            for c in range(1, D // LANE):
                tot = tot + s[:, LANE * c:LANE * (c + 1)]
            ls_ref[...] += tot

    return pl.pallas_call(
        body, grid=(L // tm,),
        in_specs=[pl.BlockSpec((tm, D), lambda i: (i, 0)), pl.BlockSpec((tm, D), lambda i: (jnp.maximum(i - 1, 0), 0))],
        out_specs=[pl.BlockSpec((tm, D), lambda i: (i, 0)), pl.BlockSpec((1, LANE), lambda i: (0, 0))],
        out_shape=[SDS((L, D), F32), SDS((1, LANE), F32)],
        compiler_params=_params(("arbitrary",)), name=name)(h, tgt)


def _adamw(w, g, m, v, *, name):
    shape = w.shape
    w2, g2, m2, v2 = [a.reshape(-1, shape[-1]) for a in (w, g, m, v)]
    rows, cols = w2.shape
    tr = _tile(rows, 256, 8)

    def body(w_ref, g_ref, m_ref, v_ref, d_ref, nm_ref, nv_ref):
        g = g_ref[...]
        m = ADAM_B1 * m_ref[...] + (1.0 - ADAM_B1) * g
        v = ADAM_B2 * v_ref[...] + (1.0 - ADAM_B2) * (g * g)
        m_hat = m / (1.0 - ADAM_B1 ** ADAM_STEP)
        v_hat = v / (1.0 - ADAM_B2 ** ADAM_STEP)
        d_ref[...] = -ADAM_LR * (m_hat / (jnp.sqrt(v_hat) + ADAM_EPS) + ADAM_WD * w_ref[...])
        nm_ref[...] = m
        nv_ref[...] = v

    spec = pl.BlockSpec((tr, cols), lambda i: (i, 0))
    outs = pl.pallas_call(
        body, grid=(rows // tr,), in_specs=[spec] * 4, out_specs=[spec] * 3, out_shape=[SDS((rows, cols), F32)] * 3,
        compiler_params=_params(("parallel",)), name=name)(w2, g2, m2, v2)
    return [o.reshape(shape) for o in outs]


def _place():
    return lax.axis_index("x"), lax.axis_index("y"), lax.axis_index("c")


def _allgather_small(v, *, name):
    m_per, n = v.shape

    def body(x_ref, out_ref, send_sems, recv_sems, local_sem):
        x, y, c = _place()
        me, sibling = (x, y, c), (x, y, 1 - c)
        chips = [(1 - x, y), (x, 1 - y), (1 - x, 1 - y)]

        def rows(px, py, pc):
            return out_ref.at[pl.ds((4 * px + 2 * py + pc) * m_per, m_per), :]

        def copy(k, block, to, src=None):
            return pltpu.make_async_remote_copy(
                src_ref=rows(*block) if src is None else src, dst_ref=rows(*block),
                send_sem=send_sems.at[k], recv_sem=recv_sems.at[k], device_id=to, device_id_type=MESH)

        mine = pltpu.make_async_copy(x_ref, rows(*me), local_sem)
        mine.start()
        first = [copy(0, me, sibling, src=x_ref)]
        first += [copy(1 + j, me, (*chip, c), src=x_ref) for j, chip in enumerate(chips)]
        for cp in first:
            cp.start()
        passed = [copy(4 + j, (*chip, c), sibling) for j, chip in enumerate(chips)]
        for j, chip in enumerate(chips):
            copy(1 + j, (*chip, c), me).wait_recv()
            passed[j].start()
        copy(0, sibling, me).wait_recv()
        for j, chip in enumerate(chips):
            copy(4 + j, (*chip, 1 - c), me).wait_recv()
        for cp in first + passed:
            cp.wait_send()
        mine.wait()

    return pl.pallas_call(
        body, out_shape=SDS((8 * m_per, n), v.dtype),
        in_specs=[pl.BlockSpec(memory_space=pltpu.VMEM)], out_specs=pl.BlockSpec(memory_space=pltpu.VMEM),
        scratch_shapes=[pltpu.SemaphoreType.DMA((7,)), pltpu.SemaphoreType.DMA((7,)), pltpu.SemaphoreType.DMA],
        name=name)(v)


def _chips(x, y):
    return [(1 - x, y), (x, 1 - y), (1 - x, 1 - y)]


def _gather_mats(shards, *, name):
    n = len(shards)

    def body(*refs):
        ins, outs = refs[:n], refs[n:2 * n]
        send_sems, recv_sems, fsend_sems, frecv_sems = refs[2 * n:]
        x, y, c = _place()
        s_me, sibling, chips = 2 * x + y, (x, y, 1 - c), _chips(x, y)

        def copy(src, dst, ssem, rsem, to):
            return pltpu.make_async_remote_copy(src_ref=src, dst_ref=dst, send_sem=ssem, recv_sem=rsem,
                                                device_id=to, device_id_type=MESH)

        first = [copy(ins[t].at[c], outs[t].at[s_me, c], send_sems.at[3 * t + k], recv_sems.at[3 * t + k], (*chip, c))
                 for t in range(n) for k, chip in enumerate(chips)]
        for cp in first:
            cp.start()
        passed = []
        for t in range(n):
            for k, chip in enumerate(chips):
                slot = outs[t].at[2 * chip[0] + chip[1], c]
                copy(ins[t].at[c], slot, send_sems.at[3 * t + k], recv_sems.at[3 * t + k], (*chip, c)).wait_recv()
                fw = copy(slot, slot, fsend_sems.at[3 * t + k], frecv_sems.at[3 * t + k], sibling)
                fw.start()
                passed.append(fw)
        for t in range(n):
            for k, chip in enumerate(chips):
                slot = outs[t].at[2 * chip[0] + chip[1], 1 - c]
                copy(slot, slot, fsend_sems.at[3 * t + k], frecv_sems.at[3 * t + k], sibling).wait_recv()
        for cp in first + passed:
            cp.wait_send()

    anyspec = pl.BlockSpec(memory_space=pl.ANY)
    return pl.pallas_call(
        body, out_shape=[SDS((4,) + s.shape, s.dtype) for s in shards], in_specs=[anyspec] * n, out_specs=[anyspec] * n,
        scratch_shapes=[pltpu.SemaphoreType.DMA((3 * n,))] * 4, name=name)(*shards)


def _rs_pair_exchange(gs, *, name):
    n = len(gs)

    def body(*refs):
        ins, outs = refs[:n], refs[n:2 * n]
        send_sems, recv_sems = refs[2 * n:]
        x, y, c = _place()
        cps = [pltpu.make_async_remote_copy(src_ref=ins[t].at[k, 1 - c], dst_ref=outs[t].at[k],
                                            send_sem=send_sems.at[4 * t + k], recv_sem=recv_sems.at[4 * t + k],
                                            device_id=(x, y, 1 - c), device_id_type=MESH)
               for t in range(n) for k in range(4)]
        for cp in cps:
            cp.start()
        for cp in cps:
            cp.wait()

    anyspec = pl.BlockSpec(memory_space=pl.ANY)
    return pl.pallas_call(
        body, out_shape=[SDS((4,) + g.shape[2:], g.dtype) for g in gs], in_specs=[anyspec] * n, out_specs=[anyspec] * n,
        scratch_shapes=[pltpu.SemaphoreType.DMA((4 * n,))] * 2, name=name)(*gs)


def _rs_chip_exchange(parts, *, name):
    n = len(parts)

    def body(*refs):
        ins, outs = refs[:n], refs[n:2 * n]
        send_sems, recv_sems, local_sems = refs[2 * n:]
        x, y, c = _place()
        s_me, chips = 2 * x + y, _chips(x, y)

        def copy(t, k, chip, src_slot, dst_slot):
            return pltpu.make_async_remote_copy(
                src_ref=ins[t].at[src_slot], dst_ref=outs[t].at[dst_slot], send_sem=send_sems.at[3 * t + k],
                recv_sem=recv_sems.at[3 * t + k], device_id=(*chip, c), device_id_type=MESH)

        mine = [pltpu.make_async_copy(ins[t].at[s_me], outs[t].at[s_me], local_sems.at[t]) for t in range(n)]
        for cp in mine:
            cp.start()
        sends = [copy(t, k, chip, 2 * chip[0] + chip[1], s_me) for t in range(n) for k, chip in enumerate(chips)]
        for cp in sends:
            cp.start()
        for t in range(n):
            for k, chip in enumerate(chips):
                copy(t, k, chip, s_me, 2 * chip[0] + chip[1]).wait_recv()
        for cp in sends:
            cp.wait_send()
        for cp in mine:
            cp.wait()

    anyspec = pl.BlockSpec(memory_space=pl.ANY)
    return pl.pallas_call(
        body, out_shape=[SDS(p.shape, p.dtype) for p in parts], in_specs=[anyspec] * n, out_specs=[anyspec] * n,
        scratch_shapes=[pltpu.SemaphoreType.DMA((3 * n,))] * 2 + [pltpu.SemaphoreType.DMA((n,))], name=name)(*parts)


def _rs_pair_share(fulls, *, name):
    n = len(fulls)

    def body(*refs):
        ins, outs = refs[:n], refs[n:2 * n]
        send_sems, recv_sems = refs[2 * n:]
        x, y, c = _place()

        def copy(t, half):
            return pltpu.make_async_remote_copy(src_ref=ins[t].at[c], dst_ref=outs[t].at[half], send_sem=send_sems.at[t],
                                                recv_sem=recv_sems.at[t], device_id=(x, y, 1 - c), device_id_type=MESH)

        sends = [copy(t, c) for t in range(n)]
        for cp in sends:
            cp.start()
        for t in range(n):
            copy(t, 1 - c).wait_recv()
        for cp in sends:
            cp.wait_send()

    anyspec = pl.BlockSpec(memory_space=pl.ANY)
    return pl.pallas_call(
        body, out_shape=[SDS(f.shape, f.dtype) for f in fulls], in_specs=[anyspec] * n, out_specs=[anyspec] * n,
        input_output_aliases={t: t for t in range(n)},
        scratch_shapes=[pltpu.SemaphoreType.DMA((n,))] * 2, name=name)(*fulls)


def _add_half(g, other, c1, *, out_dtype, name):
    _, _, h, cs = g.shape
    tr = _tile(h, 512, 16)

    def body(c_ref, g_ref, o_ref, out_ref):
        out_ref[...] = (g_ref[...] + o_ref[...]).astype(out_dtype)

    spec = pl.BlockSpec((None, tr, cs), lambda k, i, c: (k, i, 0))
    return pl.pallas_call(
        body, out_shape=SDS(other.shape, out_dtype),
        grid_spec=pltpu.PrefetchScalarGridSpec(
            num_scalar_prefetch=1, grid=(4, h // tr),
            in_specs=[pl.BlockSpec((None, None, tr, cs), lambda k, i, c: (k, c[0], i, 0)), spec], out_specs=spec),
        compiler_params=_params(("parallel", "parallel")), name=name)(c1, g, other)


def _sum4(x, c1, *, name):
    n, h, cs = x.shape
    tr = _tile(h, 512, 16)

    def body(c_ref, x_ref, o_ref):
        tot = x_ref[0].astype(F32)
        for s in range(1, n):
            tot = tot + x_ref[s].astype(F32)
        o_ref[...] = tot

    return pl.pallas_call(
        body, out_shape=SDS((2, h, cs), F32),
        grid_spec=pltpu.PrefetchScalarGridSpec(
            num_scalar_prefetch=1, grid=(h // tr,),
            in_specs=[pl.BlockSpec((n, tr, cs), lambda i, c: (0, i, 0))],
            out_specs=pl.BlockSpec((None, tr, cs), lambda i, c: (c[0], i, 0))),
        compiler_params=_params(("parallel",)), name=name)(c1, x)


def _finish_small(gathered, lbf, lbb, *, rows, name):
    r_lbf, r_lbb = rows['lb_f'], rows['lb_b']

    def body(g_ref, lbf_ref, lbb_ref, o_ref, dlf_ref, dlb_ref):
        tot = g_ref[0]
        for s in range(1, 8):
            tot = tot + g_ref[s]
        o_ref[...] = tot
        o_ref[0:1, :] = jnp.broadcast_to(jnp.sum(o_ref[0:1, :], axis=1, keepdims=True), (1, LANE))
        for lb_ref, d_ref, r0 in ((lbf_ref, dlf_ref, r_lbf), (lbb_ref, dlb_ref, r_lbb)):
            for hh in range(HG_HEADS):
                sl = slice(LANE * hh, LANE * (hh + 1))
                l0, l1 = lb_ref[0:1, sl], lb_ref[1:2, sl]
                mx = jnp.maximum(l0, l1)
                e0, e1 = jnp.exp(l0 - mx), jnp.exp(l1 - mx)
                p0 = e0 / (e0 + e1)
                d0 = o_ref[r0 + hh:r0 + hh + 1, :] * p0 * (1.0 - p0)
                d_ref[0:1, sl] = d0
                d_ref[1:2, sl] = -d0

    vm = pl.BlockSpec(memory_space=pltpu.VMEM)
    return pl.pallas_call(
        body, in_specs=[vm, vm, vm], out_specs=[vm, vm, vm],
        out_shape=[SDS(gathered.shape[1:], F32), SDS(lbf.shape, F32), SDS(lbb.shape, F32)], name=name)(gathered, lbf, lbb)


def _local_step(x2, tgt2, meta, W, S):
    T, D = x2.shape
    L = PAD + T
    h0 = jnp.concatenate([jnp.zeros((PAD - N_META, D), F32), meta, x2], axis=0)

    qk0 = Z_HG
    w_in = jnp.concatenate([W['w_in'][:, :qk0], _qk_to_group(W['w_in'][:, qk0:qk0 + AT_W + AT_KVW]),
                            W['w_in'][:, qk0 + AT_W + AT_KVW:]], axis=1)
    cc, ss = _rope_tables(L)
    wq_g, wk_g = _group_vec(S['q_norm']), _group_vec(S['k_norm'])

    def ffn_fwd(h, nw, wg, wu, wd, tag):
        n = _rmsnorm_fwd(h, nw, name=tag + "_norm")
        g, u, a = _ffn4_up(n, wg, wu, name=tag + "_up")
        hn = _ffn4_down(a, wd, h, name=tag + "_down")
        return hn, (n, g, u, a)

    def ffn_bwd(dh, h, nw, wg, wu, wd, saved, tag, split=False):
        n, g, u, a = saved
        dg, du = _ffn4_dact(dh, wd, g, u, name=tag + "_dact")
        dn = _ffn4_dn(dg, du, wg, wu, name=tag + "_dn")
        dwg = _ffn4_dw(n, dg, x_is_rows=True, name=tag + "_dwg")
        dwu = _ffn4_dw(n, du, x_is_rows=True, name=tag + "_dwu")
        dwd = _ffn4_dw(dh, a, x_is_rows=False, alpha=0.5, name=tag + "_dwd")
        *dhp, dnw = _rmsnorm_bwd(h, nw, dn, dh, split=split, name=tag + "_norm_bwd")
        return (dhp if split else dhp[0]), dnw, dwg, dwu, dwd

    h1, sv1 = ffn_fwd(h0, S['ffn1_norm'], W['ffn1_w_gate'], W['ffn1_w_up'], W['ffn1_w_down'], "ffn1")
    um = _rmsnorm_fwd(h1, S['mix_norm'], name="mix_norm")
    z = _mm([(um, w_in)], tm=512, tn=1792, tk=D, name="in_proj")
    (of, sf), (ob, sb) = _hg_fwd(z, S['hg_lb_fwd'], S['hg_lb_bwd'], name="hg_fwd")
    ya = _hg_post_fwd(of, ob, z, S['hg_out_norm'], name="hg_post")
    qm, qt, kr, krt, vb, vt = _at_prep(z, cc, ss, wq_g, wk_g, name="at_prep")
    yb, yb_f32, lse = _at_fwd(qt, kr, vt, name="at_fwd")
    mixed = _merge_fwd(ya, yb, W['w_up_a'], W['w_up_b'], z, name="merge")
    h2 = _mm([(mixed, W['w_out'])], res=h1, tm=512, tn=D, tk=D, name="out_proj")
    h3, sv2 = ffn_fwd(h2, S['ffn2_norm'], W['ffn2_w_gate'], W['ffn2_w_up'], W['ffn2_w_down'], "ffn2")
    dh3, loss_lanes = _loss_head(h3, tgt2, name="loss_head")

    G = {}
    dh2, dn_ffn2, G['ffn2_w_gate'], G['ffn2_w_up'], G['ffn2_w_down'] = ffn_bwd(
        dh3, h2, S['ffn2_norm'], W['ffn2_w_gate'], W['ffn2_w_up'], W['ffn2_w_down'], sv2, "ffn2")
    dpa, dpb, dzga, dzgb = _merge_bwd(dh2, W['w_out'], ya, yb, W['w_up_a'], W['w_up_b'], z, name="merge_bwd")
    G['w_out'] = _mm([(mixed, dh2)], ta=True, tm=D, tn=D, tk=512, name="d_w_out")
    dya = _mm([(dpa, W['w_up_a'])], tb=True, tm=512, tn=HG_W, tk=D, name="d_ya")
    dyb = _mm([(dpb, W['w_up_b'])], tb=True, tm=512, tn=AT_W, tk=D, name="d_yb")
    G['w_up_a'] = _mm([(ya, dpa)], ta=True, tm=HG_W, tn=D, tk=512, name="d_w_up_a")
    G['w_up_b'] = _mm([(yb, dpb)], ta=True, tm=AT_W, tn=D, tk=512, name="d_w_up_b")
    do_hg, dzg, d_hgn = _hg_post_bwd(dya, of, ob, z, S['hg_out_norm'], name="hg_post_bwd")
    (dq_f, dv_f, dzf_f, dlb_f), (dq_b, dv_b, dzf_b, dlb_b) = _hg_bwd_both(
        z, S['hg_lb_fwd'], S['hg_lb_bwd'], do_hg, sf, sb, name="hg_bwd")
    dzq, dzi = _hg_combine(z, dq_f, dq_b, dv_f, dv_b, name="hg_combine")
    dqm, dk2, dv2 = _at_bwd(qm, qt, kr, krt, vb, dyb, yb_f32, lse, name="at_bwd")
    dz_at, dwq_g, dwk_g = _at_prep_bwd(dqm, dk2, dv2, z, cc, ss, wq_g, wk_g, name="at_prep_bwd")
    dz = jnp.concatenate([dzq, dzi, dzf_f, dzf_b, dzg, dz_at, dzga, dzgb], axis=1)
    dum = _mm([(dz, w_in)], tb=True, tm=512, tn=D, tk=1792, name="d_um")
    dw_in_p = _mm([(um, dz)], ta=True, tm=D, tn=1792, tk=512, name="d_w_in")
    G['w_in'] = jnp.concatenate([dw_in_p[:, :qk0], _qk_from_group(dw_in_p[:, qk0:qk0 + AT_W + AT_KVW]),
                                 dw_in_p[:, qk0 + AT_W + AT_KVW:]], axis=1)
    dh1, dn_mix = _rmsnorm_bwd(h1, S['mix_norm'], dum, dh2, name="mix_norm_bwd")
    (grad_x, dmeta), dn_ffn1, G['ffn1_w_gate'], G['ffn1_w_up'], G['ffn1_w_down'] = ffn_bwd(
        dh1, h0, S['ffn1_norm'], W['ffn1_w_gate'], W['ffn1_w_up'], W['ffn1_w_down'], sv1, "ffn1", split=True)

    small_rows = [('loss', loss_lanes), ('ffn1_norm', dn_ffn1.reshape(-1, LANE)), ('mix_norm', dn_mix.reshape(-1, LANE)),
                  ('ffn2_norm', dn_ffn2.reshape(-1, LANE)), ('hg_out_norm', d_hgn.reshape(-1, LANE)),
                  ('lb_f', dlb_f.reshape(-1, LANE)), ('lb_b', dlb_b.reshape(-1, LANE)), ('q_norm', dwq_g), ('k_norm', dwk_g)]
    return grad_x, dmeta, G, small_rows


def kernel(x, meta_tokens, ffn1_norm, ffn1_w_gate, ffn1_w_up, ffn1_w_down, mix_norm, w_in, hg_lb_fwd, hg_lb_bwd, hg_out_norm, q_norm, k_norm, w_up_a, w_up_b, w_out, ffn2_norm, ffn2_w_gate, ffn2_w_up, ffn2_w_down, loss_target, m_meta_tokens, m_ffn1_norm, m_ffn1_w_gate, m_ffn1_w_up, m_ffn1_w_down, m_mix_norm, m_w_in, m_hg_lb_fwd, m_hg_lb_bwd, m_hg_out_norm, m_q_norm, m_k_norm, m_w_up_a, m_w_up_b, m_w_out, m_ffn2_norm, m_ffn2_w_gate, m_ffn2_w_up, m_ffn2_w_down, v_meta_tokens, v_ffn1_norm, v_ffn1_w_gate, v_ffn1_w_up, v_ffn1_w_down, v_mix_norm, v_w_in, v_hg_lb_fwd, v_hg_lb_bwd, v_hg_out_norm, v_q_norm, v_k_norm, v_w_up_a, v_w_up_b, v_w_out, v_ffn2_norm, v_ffn2_w_gate, v_ffn2_w_up, v_ffn2_w_down):
    given = dict(locals())
    w = {n: given[n] for n in WEIGHTS}
    mom = {n: given["m_" + n] for n in WEIGHTS}
    var = {n: given["v_" + n] for n in WEIGHTS}
    c = lax.axis_index("c")
    D = x.shape[-1]

    shapes = {n: w[n].shape[-2:] for n in MATS + ('meta_tokens',)}
    halves = [w[n].astype(BF16).reshape(2, shapes[n][0] // 2, shapes[n][1]) for n in MATS]
    gathered = _gather_mats(halves, name="gather_weights")
    s_me = 2 * lax.axis_index("x") + lax.axis_index("y")
    W = {}
    for n, hv, g4 in zip(MATS, halves, gathered):
        r, cs = shapes[n]
        g4 = lax.dynamic_update_index_in_dim(g4, hv, s_me, 0).reshape(4, r, cs)
        if n in FFN_MATS:
            W[n] = g4
        elif n in ROW_SHARDED:
            W[n] = g4.reshape(4 * r, cs)
        else:
            W[n] = g4.transpose(1, 0, 2).reshape(r, 4 * cs)
    meta_rows = w['meta_tokens'].reshape(-1, LANE)
    mg = _allgather_small(meta_rows, name="gather_meta").reshape(4, 2, N_META, -1)[:, 0]
    meta = mg.transpose(1, 0, 2).reshape(N_META, D)
    S = {n: w[n] for n in SMALLS}

    grad_x, dmeta, G, small_rows = _local_step(x[0], loss_target[0], meta, W, S)
    G['meta_tokens'] = dmeta

    names = MATS + ('meta_tokens',)
    views = []
    for n in names:
        r, cs = shapes[n]
        if n in FFN_MATS:
            g4 = G[n]
        elif n in ROW_SHARDED:
            g4 = G[n].reshape(4, r, cs)
        else:
            g4 = G[n].reshape(r, 4, cs).transpose(1, 0, 2)
        views.append(g4.reshape(4, 2, r // 2, cs))
    c1 = c.astype(jnp.int32).reshape(1)
    from_sibling = _rs_pair_exchange(views, name="rs_pair_exchange")
    parts = [_add_half(v, o, c1, out_dtype=F32 if n == 'meta_tokens' else BF16, name="rs_pair_sum_" + n)
             for n, v, o in zip(names, views, from_sibling)]
    slabs = _rs_chip_exchange(parts, name="rs_chip_exchange")
    reds = [_sum4(s, c1, name="rs_chip_sum_" + n) for n, s in zip(names, slabs)]
    both = _rs_pair_share(reds, name="rs_pair_share")
    grads = {n: b.reshape(w[n].shape) for n, b in zip(names, both)}

    rows, off = {}, 0
    for nme, blk in small_rows:
        rows[nme] = off
        off += blk.shape[0]
    block = jnp.concatenate([blk for _, blk in small_rows], axis=0)
    n_rows = (off + 7) // 8 * 8
    block = jnp.pad(block, ((0, n_rows - off), (0, 0)))
    allsmall = _allgather_small(block, name="gather_small").reshape(8, n_rows, LANE)
    tot, d_lbf, d_lbb = _finish_small(allsmall, w['hg_lb_fwd'], w['hg_lb_bwd'], rows=rows, name="finish_small")
    loss = 0.5 * tot[0, 0] / D

    def small(nme, shape):
        r0 = rows[nme]
        return tot[r0:r0 + shape[-1] // LANE].reshape(shape)

    grads['ffn1_norm'] = small('ffn1_norm', w['ffn1_norm'].shape)
    grads['mix_norm'] = small('mix_norm', w['mix_norm'].shape)
    grads['ffn2_norm'] = small('ffn2_norm', w['ffn2_norm'].shape)
    grads['hg_out_norm'] = small('hg_out_norm', w['hg_out_norm'].shape)
    grads['hg_lb_fwd'] = d_lbf
    grads['hg_lb_bwd'] = d_lbb
    grads['q_norm'] = _ungroup_vec(tot[rows['q_norm']])
    grads['k_norm'] = _ungroup_vec(tot[rows['k_norm']])

    delta, new_m, new_v = {}, {}, {}
    for n in WEIGHTS:
        delta[n], new_m[n], new_v[n] = _adamw(w[n], grads[n], mom[n], var[n], name="adamw_" + n)
    return (loss, grad_x[None], *[grads[n] for n in WEIGHTS], *[delta[n] for n in WEIGHTS],
            *[new_m[n] for n in WEIGHTS], *[new_v[n] for n in WEIGHTS])
```
